```python
import math
import jax, jax.numpy as jnp
from jax import lax
import numpy as np

D_MODEL = 1024
BATCH = 8
SEQ = 4096
DEPTH = 2

D_MIX = D_MODEL
ATTN_WIDTH = 3 * D_MIX // 8
LRU_WIDTH = 3 * D_MIX // 8
S5_WIDTH = D_MIX - ATTN_WIDTH - LRU_WIDTH

HEAD_DIM = 64
N_ATTN_HEADS = ATTN_WIDTH // HEAD_DIM
DILATED_PAIRS = ((128, 1), (512, 4), (2048, 16))
ATTN_BLOCK = 128
ROPE_THETA = 10000.0

LRU_HEAD = 64
N_LRU_HEADS = LRU_WIDTH // LRU_HEAD
LRU_CONV = 4
LRU_C = 8.0

S5_GROUP = 16
N_S5_GROUPS = S5_WIDTH // S5_GROUP
S5_STATE = 64

D_FF = 3 * D_MODEL
FFN_CONV = 3

DEEPNORM_ALPHA = (2 * DEPTH) ** 0.25
DEEPNORM_BETA = (8 * DEPTH) ** -0.25
LN_EPS = 1e-5
RMS_EPS = 1e-6

Q_OFF = 0
K_OFF = ATTN_WIDTH
V_OFF = 2 * ATTN_WIDTH
LRU_X_OFF = 3 * ATTN_WIDTH
LRU_G_OFF = LRU_X_OFF + LRU_WIDTH
S5_OFF = LRU_G_OFF + LRU_WIDTH
D_IN = S5_OFF + S5_WIDTH

kernel_name = 'hybrid_dilated_attn_rglru_s5_deepnorm'


def _layer_norm(x, g, b):
    xf = x.astype(jnp.float32)
    mu = jnp.mean(xf, axis=-1, keepdims=True)
    var = jnp.mean(jnp.square(xf - mu), axis=-1, keepdims=True)
    y = (xf - mu) * lax.rsqrt(var + LN_EPS) * g.astype(jnp.float32) + b.astype(jnp.float32)
    return y.astype(x.dtype)


def _rms_norm(x, g):
    xf = x.astype(jnp.float32)
    ms = jnp.mean(jnp.square(xf), axis=-1, keepdims=True)
    return xf * lax.rsqrt(ms + RMS_EPS) * g.astype(jnp.float32)


def _causal_dwconv(x, w, b):
    k = w.shape[0]
    y = lax.conv_general_dilated(
        x, w[:, None, :].astype(x.dtype), window_strides=(1,), padding=((k - 1, 0),),
        dimension_numbers=('NWC', 'WIO', 'NWC'), feature_group_count=x.shape[-1])
    return y + b.astype(x.dtype)


def _rope(x):
    s = x.shape[1]
    half = HEAD_DIM // 2
    pos = jnp.arange(s, dtype=jnp.float32)
    inv = ROPE_THETA ** (-jnp.arange(half, dtype=jnp.float32) * 2.0 / HEAD_DIM)
    ang = pos[:, None] * inv[None, :]
    cos = jnp.cos(ang)[None, :, None, :]
    sin = jnp.sin(ang)[None, :, None, :]
    xf = x.astype(jnp.float32)
    x1, x2 = xf[..., :half], xf[..., half:]
    return jnp.concatenate([x1 * cos - x2 * sin, x2 * cos + x1 * sin], axis=-1)


def _dilated_branch(q, k, v, window, dilation):
    b, s, nh, hd = q.shape
    m = s // dilation
    nback = window // dilation
    nb = -(-m // ATTN_BLOCK)
    mp = nb * ATTN_BLOCK

    def strided(t, left):
        t = t.reshape(b, m, dilation, nh, hd)
        return jnp.pad(t, ((0, 0), (left, mp - m), (0, 0), (0, 0), (0, 0)))

    def key_blocks(t):
        t = strided(t, ATTN_BLOCK).reshape(b, nb + 1, ATTN_BLOCK, dilation, nh, hd)
        return jnp.concatenate([t[:, :-1], t[:, 1:]], axis=2)

    qb = strided(q, 0).reshape(b, nb, ATTN_BLOCK, dilation, nh, hd)
    kb = key_blocks(k)
    vb = key_blocks(v)
    scores = jnp.einsum('bnqchd,bnkchd->bnchqk', qb, kb) * (hd ** -0.5)
    qi = jnp.arange(ATTN_BLOCK)[:, None]
    ki = jnp.arange(2 * ATTN_BLOCK)[None, :]
    dist = qi + ATTN_BLOCK - ki
    blk = jnp.arange(nb)[:, None, None]
    valid = (dist >= 0) & (dist <= nback) & ((blk - 1) * ATTN_BLOCK + ki >= 0)
    scores = jnp.where(valid[None, :, None, None], scores, -jnp.inf)
    lse = jax.nn.logsumexp(scores, axis=-1)
    probs = jnp.exp(scores - lse[..., None])
    out = jnp.einsum('bnchqk,bnkchd->bnqchd', probs, vb)
    out = out.reshape(b, mp, dilation, nh, hd)[:, :m].reshape(b, s, nh, hd)
    lse = jnp.transpose(lse, (0, 1, 4, 2, 3)).reshape(b, mp, dilation, nh)[:, :m].reshape(b, s, nh)
    return out, lse


def _dilated_attention(q, k, v):
    q = _rope(q)
    k = _rope(k)
    v = v.astype(jnp.float32)
    outs, lses = [], []
    for window, dilation in DILATED_PAIRS:
        o, l = _dilated_branch(q, k, v, window, dilation)
        outs.append(o)
        lses.append(l)
    wts = jax.nn.softmax(jnp.stack(lses, axis=0), axis=0)
    return jnp.sum(wts[..., None] * jnp.stack(outs, axis=0), axis=0)


def _linear_combine(e1, e2):
    a1, b1 = e1
    a2, b2 = e2
    return (a1 * a2, a2 * b1 + b2)


def _complex_combine(e1, e2):
    ar1, ai1, br1, bi1 = e1
    ar2, ai2, br2, bi2 = e2
    return (ar2 * ar1 - ai2 * ai1,
            ar2 * ai1 + ai2 * ar1,
            ar2 * br1 - ai2 * bi1 + br2,
            ar2 * bi1 + ai2 * br1 + bi2)


def _rg_lru_branch(xr, gate, conv_w, conv_b, wr, br, wi, bi, lam):
    b, s, _ = xr.shape
    f32 = jnp.float32
    xc = _causal_dwconv(xr, conv_w, conv_b).astype(f32)
    xh = xc.reshape(b, s, N_LRU_HEADS, LRU_HEAD)
    r = jax.nn.sigmoid(jnp.einsum('bshi,hij->bshj', xh, wr.astype(f32)).reshape(b, s, LRU_WIDTH) + br.astype(f32))
    i = jax.nn.sigmoid(jnp.einsum('bshi,hij->bshj', xh, wi.astype(f32)).reshape(b, s, LRU_WIDTH) + bi.astype(f32))
    log_a = -LRU_C * r * jax.nn.softplus(-lam.astype(f32))
    a = jnp.exp(log_a)
    u = jnp.sqrt(-jnp.expm1(2.0 * log_a)) * (i * xc)
    _, h = lax.associative_scan(_linear_combine, (a, u), axis=1)
    return h * jax.nn.gelu(gate.astype(f32))


def _s5_branch(u, a_re, a_im, b_re, b_im, c_re, c_im, d, log_step, w_glu, b_glu):
    f32 = jnp.float32
    bsz, s, _ = u.shape
    uf = u.astype(f32).reshape(bsz, s, N_S5_GROUPS, S5_GROUP)
    a_re, a_im = a_re.astype(f32), a_im.astype(f32)
    b_re, b_im = b_re.astype(f32), b_im.astype(f32)
    step = jnp.exp(log_step.astype(f32))[:, None]
    dt_re, dt_im = step * a_re, step * a_im
    mag = jnp.exp(dt_re)
    ab_re, ab_im = mag * jnp.cos(dt_im), mag * jnp.sin(dt_im)
    z_re, z_im = ab_re - 1.0, ab_im
    den = a_re * a_re + a_im * a_im
    f_re = (z_re * a_re + z_im * a_im) / den
    f_im = (z_im * a_re - z_re * a_im) / den
    bb_re = f_re[..., None] * b_re - f_im[..., None] * b_im
    bb_im = f_re[..., None] * b_im + f_im[..., None] * b_re
    bu_re = jnp.einsum('bsgc,gpc->bsgp', uf, bb_re)
    bu_im = jnp.einsum('bsgc,gpc->bsgp', uf, bb_im)
    shape = bu_re.shape
    elems = (jnp.broadcast_to(ab_re, shape), jnp.broadcast_to(ab_im, shape), bu_re, bu_im)
    _, _, x_re, x_im = lax.associative_scan(_complex_combine, elems, axis=1)
    y = (jnp.einsum('bsgp,gcp->bsgc', x_re, c_re.astype(f32))
         - jnp.einsum('bsgp,gcp->bsgc', x_im, c_im.astype(f32))
         + d.astype(f32) * uf)
    y = jax.nn.gelu(y.reshape(bsz, s, S5_WIDTH))
    return y * jax.nn.sigmoid(y @ w_glu.astype(f32) + b_glu.astype(f32))


def _hybrid_mixer(h, w_in, lru_conv_w, lru_conv_b, lru_wr, lru_br, lru_wi, lru_bi, lru_lambda,
                  s5_a_re, s5_a_im, s5_b_re, s5_b_im, s5_c_re, s5_c_im, s5_d, s5_log_step,
                  s5_w_glu, s5_b_glu, mix_norm_g, w_out):
    b, s, _ = h.shape
    proj = h @ w_in

    def heads(off):
        return proj[..., off:off + ATTN_WIDTH].reshape(b, s, N_ATTN_HEADS, HEAD_DIM)

    attn = _dilated_attention(heads(Q_OFF), heads(K_OFF), heads(V_OFF)).reshape(b, s, ATTN_WIDTH)
    lru = _rg_lru_branch(proj[..., LRU_X_OFF:LRU_X_OFF + LRU_WIDTH],
                         proj[..., LRU_G_OFF:LRU_G_OFF + LRU_WIDTH],
                         lru_conv_w, lru_conv_b, lru_wr, lru_br, lru_wi, lru_bi, lru_lambda)
    ssm = _s5_branch(proj[..., S5_OFF:S5_OFF + S5_WIDTH], s5_a_re, s5_a_im, s5_b_re, s5_b_im,
                     s5_c_re, s5_c_im, s5_d, s5_log_step, s5_w_glu, s5_b_glu)
    g_attn = mix_norm_g[:ATTN_WIDTH]
    g_lru = mix_norm_g[ATTN_WIDTH:ATTN_WIDTH + LRU_WIDTH]
    g_s5 = mix_norm_g[ATTN_WIDTH + LRU_WIDTH:]
    mixed = jnp.concatenate([_rms_norm(attn, g_attn), _rms_norm(lru, g_lru), _rms_norm(ssm, g_s5)],
                            axis=-1).astype(h.dtype)
    return mixed @ w_out


def _conv_ffn(h, w_up, conv_w, conv_b, w_down):
    up = _causal_dwconv(h @ w_up, conv_w, conv_b)
    gate, val = jnp.split(up, 2, axis=-1)
    return (jax.nn.gelu(gate) * val) @ w_down


def _fwd_setup_inputs(seed: int = 0) -> dict:
    key = jax.random.key(seed)
    ks = iter(jax.random.split(key, 32))
    f32 = jnp.float32
    L = DEPTH

    def nrm(shape, scale):
        return jax.random.normal(next(ks), shape, f32) * scale

    x = nrm((BATCH, SEQ, D_MODEL), 1.0)
    w_in = nrm((L, D_MODEL, D_IN), D_MODEL ** -0.5)
    lru_conv_w = nrm((L, LRU_CONV, LRU_WIDTH), LRU_CONV ** -0.5)
    lru_conv_b = nrm((L, LRU_WIDTH), 0.02)
    lru_wr = nrm((L, N_LRU_HEADS, LRU_HEAD, LRU_HEAD), LRU_HEAD ** -0.5)
    lru_br = nrm((L, LRU_WIDTH), 0.02)
    lru_wi = nrm((L, N_LRU_HEADS, LRU_HEAD, LRU_HEAD), LRU_HEAD ** -0.5)
    lru_bi = nrm((L, LRU_WIDTH), 0.02)
    a_c = jax.random.uniform(next(ks), (L, LRU_WIDTH), f32, 0.9, 0.999)
    a0 = a_c ** (1.0 / LRU_C)
    lru_lambda = jnp.log(a0) - jnp.log1p(-a0)
    s5_a_re = -0.5 + nrm((L, N_S5_GROUPS, S5_STATE), 0.01)
    s5_a_im = jnp.pi * jnp.arange(S5_STATE, dtype=f32) + nrm((L, N_S5_GROUPS, S5_STATE), 0.01)
    s5_b_re = nrm((L, N_S5_GROUPS, S5_STATE, S5_GROUP), (2 * S5_GROUP) ** -0.5)
    s5_b_im = nrm((L, N_S5_GROUPS, S5_STATE, S5_GROUP), (2 * S5_GROUP) ** -0.5)
    s5_c_re = nrm((L, N_S5_GROUPS, S5_GROUP, S5_STATE), (2 * S5_STATE) ** -0.5)
    s5_c_im = nrm((L, N_S5_GROUPS, S5_GROUP, S5_STATE), (2 * S5_STATE) ** -0.5)
    s5_d = nrm((L, N_S5_GROUPS, S5_GROUP), 1.0)
    s5_log_step = jax.random.uniform(next(ks), (L, N_S5_GROUPS), f32, math.log(1e-3), math.log(1e-1))
    s5_w_glu = nrm((L, S5_WIDTH, S5_WIDTH), S5_WIDTH ** -0.5)
    s5_b_glu = nrm((L, S5_WIDTH), 0.02)
    mix_norm_g = 1.0 + nrm((L, D_MIX), 0.02)
    w_out = nrm((L, D_MIX, D_MODEL), D_MIX ** -0.5 * DEEPNORM_BETA)
    ln1_g = 1.0 + nrm((L, D_MODEL), 0.02)
    ln1_b = nrm((L, D_MODEL), 0.02)
    w_up = nrm((L, D_MODEL, 2 * D_FF), D_MODEL ** -0.5)
    ffn_conv_w = nrm((L, FFN_CONV, 2 * D_FF), FFN_CONV ** -0.5)
    ffn_conv_b = nrm((L, 2 * D_FF), 0.02)
    w_down = nrm((L, D_FF, D_MODEL), D_FF ** -0.5 * DEEPNORM_BETA)
    ln2_g = 1.0 + nrm((L, D_MODEL), 0.02)
    ln2_b = nrm((L, D_MODEL), 0.02)
    return {'x': x, 'w_in': w_in, 'lru_conv_w': lru_conv_w, 'lru_conv_b': lru_conv_b,
            'lru_wr': lru_wr, 'lru_br': lru_br, 'lru_wi': lru_wi, 'lru_bi': lru_bi,
            'lru_lambda': lru_lambda, 's5_a_re': s5_a_re, 's5_a_im': s5_a_im,
            's5_b_re': s5_b_re, 's5_b_im': s5_b_im, 's5_c_re': s5_c_re, 's5_c_im': s5_c_im,
            's5_d': s5_d, 's5_log_step': s5_log_step, 's5_w_glu': s5_w_glu, 's5_b_glu': s5_b_glu,
            'mix_norm_g': mix_norm_g, 'w_out': w_out, 'ln1_g': ln1_g, 'ln1_b': ln1_b,
            'w_up': w_up, 'ffn_conv_w': ffn_conv_w, 'ffn_conv_b': ffn_conv_b, 'w_down': w_down,
            'ln2_g': ln2_g, 'ln2_b': ln2_b}


def _fwd_reference(x, w_in, lru_conv_w, lru_conv_b, lru_wr, lru_br, lru_wi, lru_bi, lru_lambda,
              s5_a_re, s5_a_im, s5_b_re, s5_b_im, s5_c_re, s5_c_im, s5_d, s5_log_step,
              s5_w_glu, s5_b_glu, mix_norm_g, w_out, ln1_g, ln1_b, w_up, ffn_conv_w, ffn_conv_b,
              w_down, ln2_g, ln2_b):
    h = x
    for l in range(DEPTH):
        mix = _hybrid_mixer(h, w_in[l], lru_conv_w[l], lru_conv_b[l], lru_wr[l], lru_br[l],
                            lru_wi[l], lru_bi[l], lru_lambda[l], s5_a_re[l], s5_a_im[l],
                            s5_b_re[l], s5_b_im[l], s5_c_re[l], s5_c_im[l], s5_d[l],
                            s5_log_step[l], s5_w_glu[l], s5_b_glu[l], mix_norm_g[l], w_out[l])
        h = _layer_norm(DEEPNORM_ALPHA * h + mix, ln1_g[l], ln1_b[l])
        ffn = _conv_ffn(h, w_up[l], ffn_conv_w[l], ffn_conv_b[l], w_down[l])
        h = _layer_norm(DEEPNORM_ALPHA * h + ffn, ln2_g[l], ln2_b[l])
    return h


import jax as _jax
import jax.numpy as _jnp

TWIN_FORMAT = 'train_step'
FWD_PARAMS = ['x', 'w_in', 'lru_conv_w', 'lru_conv_b', 'lru_wr', 'lru_br', 'lru_wi', 'lru_bi', 'lru_lambda', 's5_a_re', 's5_a_im', 's5_b_re', 's5_b_im', 's5_c_re', 's5_c_im', 's5_d', 's5_log_step', 's5_w_glu', 's5_b_glu', 'mix_norm_g', 'w_out', 'ln1_g', 'ln1_b', 'w_up', 'ffn_conv_w', 'ffn_conv_b', 'w_down', 'ln2_g', 'ln2_b']
TWIN_WEIGHTS = ['w_in', 'lru_conv_w', 'lru_conv_b', 'lru_wr', 'lru_br', 'lru_wi', 'lru_bi', 'lru_lambda', 's5_a_re', 's5_a_im', 's5_b_re', 's5_b_im', 's5_c_re', 's5_c_im', 's5_d', 's5_log_step', 's5_w_glu', 's5_b_glu', 'mix_norm_g', 'w_out', 'ln1_g', 'ln1_b', 'w_up', 'ffn_conv_w', 'ffn_conv_b', 'w_down', 'ln2_g', 'ln2_b']
TWIN_DIFF_INPUT = 'x'
TWIN_INPUTS = ['x', 'w_in', 'lru_conv_w', 'lru_conv_b', 'lru_wr', 'lru_br', 'lru_wi', 'lru_bi', 'lru_lambda', 's5_a_re', 's5_a_im', 's5_b_re', 's5_b_im', 's5_c_re', 's5_c_im', 's5_d', 's5_log_step', 's5_w_glu', 's5_b_glu', 'mix_norm_g', 'w_out', 'ln1_g', 'ln1_b', 'w_up', 'ffn_conv_w', 'ffn_conv_b', 'w_down', 'ln2_g', 'ln2_b', 'loss_target', 'm_w_in', 'm_lru_conv_w', 'm_lru_conv_b', 'm_lru_wr', 'm_lru_br', 'm_lru_wi', 'm_lru_bi', 'm_lru_lambda', 'm_s5_a_re', 'm_s5_a_im', 'm_s5_b_re', 'm_s5_b_im', 'm_s5_c_re', 'm_s5_c_im', 'm_s5_d', 'm_s5_log_step', 'm_s5_w_glu', 'm_s5_b_glu', 'm_mix_norm_g', 'm_w_out', 'm_ln1_g', 'm_ln1_b', 'm_w_up', 'm_ffn_conv_w', 'm_ffn_conv_b', 'm_w_down', 'm_ln2_g', 'm_ln2_b', 'v_w_in', 'v_lru_conv_w', 'v_lru_conv_b', 'v_lru_wr', 'v_lru_br', 'v_lru_wi', 'v_lru_bi', 'v_lru_lambda', 'v_s5_a_re', 'v_s5_a_im', 'v_s5_b_re', 'v_s5_b_im', 'v_s5_c_re', 'v_s5_c_im', 'v_s5_d', 'v_s5_log_step', 'v_s5_w_glu', 'v_s5_b_glu', 'v_mix_norm_g', 'v_w_out', 'v_ln1_g', 'v_ln1_b', 'v_w_up', 'v_ffn_conv_w', 'v_ffn_conv_b', 'v_w_down', 'v_ln2_g', 'v_ln2_b']
TWIN_OUTPUTS = ['loss', 'grad_x', 'grad_w_in', 'grad_lru_conv_w', 'grad_lru_conv_b', 'grad_lru_wr', 'grad_lru_br', 'grad_lru_wi', 'grad_lru_bi', 'grad_lru_lambda', 'grad_s5_a_re', 'grad_s5_a_im', 'grad_s5_b_re', 'grad_s5_b_im', 'grad_s5_c_re', 'grad_s5_c_im', 'grad_s5_d', 'grad_s5_log_step', 'grad_s5_w_glu', 'grad_s5_b_glu', 'grad_mix_norm_g', 'grad_w_out', 'grad_ln1_g', 'grad_ln1_b', 'grad_w_up', 'grad_ffn_conv_w', 'grad_ffn_conv_b', 'grad_w_down', 'grad_ln2_g', 'grad_ln2_b', 'delta_w_in', 'delta_lru_conv_w', 'delta_lru_conv_b', 'delta_lru_wr', 'delta_lru_br', 'delta_lru_wi', 'delta_lru_bi', 'delta_lru_lambda', 'delta_s5_a_re', 'delta_s5_a_im', 'delta_s5_b_re', 'delta_s5_b_im', 'delta_s5_c_re', 'delta_s5_c_im', 'delta_s5_d', 'delta_s5_log_step', 'delta_s5_w_glu', 'delta_s5_b_glu', 'delta_mix_norm_g', 'delta_w_out', 'delta_ln1_g', 'delta_ln1_b', 'delta_w_up', 'delta_ffn_conv_w', 'delta_ffn_conv_b', 'delta_w_down', 'delta_ln2_g', 'delta_ln2_b', 'new_m_w_in', 'new_m_lru_conv_w', 'new_m_lru_conv_b', 'new_m_lru_wr', 'new_m_lru_br', 'new_m_lru_wi', 'new_m_lru_bi', 'new_m_lru_lambda', 'new_m_s5_a_re', 'new_m_s5_a_im', 'new_m_s5_b_re', 'new_m_s5_b_im', 'new_m_s5_c_re', 'new_m_s5_c_im', 'new_m_s5_d', 'new_m_s5_log_step', 'new_m_s5_w_glu', 'new_m_s5_b_glu', 'new_m_mix_norm_g', 'new_m_w_out', 'new_m_ln1_g', 'new_m_ln1_b', 'new_m_w_up', 'new_m_ffn_conv_w', 'new_m_ffn_conv_b', 'new_m_w_down', 'new_m_ln2_g', 'new_m_ln2_b', 'new_v_w_in', 'new_v_lru_conv_w', 'new_v_lru_conv_b', 'new_v_lru_wr', 'new_v_lru_br', 'new_v_lru_wi', 'new_v_lru_bi', 'new_v_lru_lambda', 'new_v_s5_a_re', 'new_v_s5_a_im', 'new_v_s5_b_re', 'new_v_s5_b_im', 'new_v_s5_c_re', 'new_v_s5_c_im', 'new_v_s5_d', 'new_v_s5_log_step', 'new_v_s5_w_glu', 'new_v_s5_b_glu', 'new_v_mix_norm_g', 'new_v_w_out', 'new_v_ln1_g', 'new_v_ln1_b', 'new_v_w_up', 'new_v_ffn_conv_w', 'new_v_ffn_conv_b', 'new_v_w_down', 'new_v_ln2_g', 'new_v_ln2_b']
TWIN_LEAF_KINDS = {'loss': 'loss', 'grad_x': 'grad_x', 'grad_w_in': 'grad_w', 'grad_lru_conv_w': 'grad_w', 'grad_lru_conv_b': 'grad_w', 'grad_lru_wr': 'grad_w', 'grad_lru_br': 'grad_w', 'grad_lru_wi': 'grad_w', 'grad_lru_bi': 'grad_w', 'grad_lru_lambda': 'grad_w', 'grad_s5_a_re': 'grad_w', 'grad_s5_a_im': 'grad_w', 'grad_s5_b_re': 'grad_w', 'grad_s5_b_im': 'grad_w', 'grad_s5_c_re': 'grad_w', 'grad_s5_c_im': 'grad_w', 'grad_s5_d': 'grad_w', 'grad_s5_log_step': 'grad_w', 'grad_s5_w_glu': 'grad_w', 'grad_s5_b_glu': 'grad_w', 'grad_mix_norm_g': 'grad_w', 'grad_w_out': 'grad_w', 'grad_ln1_g': 'grad_w', 'grad_ln1_b': 'grad_w', 'grad_w_up': 'grad_w', 'grad_ffn_conv_w': 'grad_w', 'grad_ffn_conv_b': 'grad_w', 'grad_w_down': 'grad_w', 'grad_ln2_g': 'grad_w', 'grad_ln2_b': 'grad_w', 'delta_w_in': 'delta_w', 'delta_lru_conv_w': 'delta_w', 'delta_lru_conv_b': 'delta_w', 'delta_lru_wr': 'delta_w', 'delta_lru_br': 'delta_w', 'delta_lru_wi': 'delta_w', 'delta_lru_bi': 'delta_w', 'delta_lru_lambda': 'delta_w', 'delta_s5_a_re': 'delta_w', 'delta_s5_a_im': 'delta_w', 'delta_s5_b_re': 'delta_w', 'delta_s5_b_im': 'delta_w', 'delta_s5_c_re': 'delta_w', 'delta_s5_c_im': 'delta_w', 'delta_s5_d': 'delta_w', 'delta_s5_log_step': 'delta_w', 'delta_s5_w_glu': 'delta_w', 'delta_s5_b_glu': 'delta_w', 'delta_mix_norm_g': 'delta_w', 'delta_w_out': 'delta_w', 'delta_ln1_g': 'delta_w', 'delta_ln1_b': 'delta_w', 'delta_w_up': 'delta_w', 'delta_ffn_conv_w': 'delta_w', 'delta_ffn_conv_b': 'delta_w', 'delta_w_down': 'delta_w', 'delta_ln2_g': 'delta_w', 'delta_ln2_b': 'delta_w', 'new_m_w_in': 'new_m', 'new_m_lru_conv_w': 'new_m', 'new_m_lru_conv_b': 'new_m', 'new_m_lru_wr': 'new_m', 'new_m_lru_br': 'new_m', 'new_m_lru_wi': 'new_m', 'new_m_lru_bi': 'new_m', 'new_m_lru_lambda': 'new_m', 'new_m_s5_a_re': 'new_m', 'new_m_s5_a_im': 'new_m', 'new_m_s5_b_re': 'new_m', 'new_m_s5_b_im': 'new_m', 'new_m_s5_c_re': 'new_m', 'new_m_s5_c_im': 'new_m', 'new_m_s5_d': 'new_m', 'new_m_s5_log_step': 'new_m', 'new_m_s5_w_glu': 'new_m', 'new_m_s5_b_glu': 'new_m', 'new_m_mix_norm_g': 'new_m', 'new_m_w_out': 'new_m', 'new_m_ln1_g': 'new_m', 'new_m_ln1_b': 'new_m', 'new_m_w_up': 'new_m', 'new_m_ffn_conv_w': 'new_m', 'new_m_ffn_conv_b': 'new_m', 'new_m_w_down': 'new_m', 'new_m_ln2_g': 'new_m', 'new_m_ln2_b': 'new_m', 'new_v_w_in': 'new_v', 'new_v_lru_conv_w': 'new_v', 'new_v_lru_conv_b': 'new_v', 'new_v_lru_wr': 'new_v', 'new_v_lru_br': 'new_v', 'new_v_lru_wi': 'new_v', 'new_v_lru_bi': 'new_v', 'new_v_lru_lambda': 'new_v', 'new_v_s5_a_re': 'new_v', 'new_v_s5_a_im': 'new_v', 'new_v_s5_b_re': 'new_v', 'new_v_s5_b_im': 'new_v', 'new_v_s5_c_re': 'new_v', 'new_v_s5_c_im': 'new_v', 'new_v_s5_d': 'new_v', 'new_v_s5_log_step': 'new_v', 'new_v_s5_w_glu': 'new_v', 'new_v_s5_b_glu': 'new_v', 'new_v_mix_norm_g': 'new_v', 'new_v_w_out': 'new_v', 'new_v_ln1_g': 'new_v', 'new_v_ln1_b': 'new_v', 'new_v_w_up': 'new_v', 'new_v_ffn_conv_w': 'new_v', 'new_v_ffn_conv_b': 'new_v', 'new_v_w_down': 'new_v', 'new_v_ln2_g': 'new_v', 'new_v_ln2_b': 'new_v'}


def _forward(args):
    return _fwd_reference(*[args[k] for k in FWD_PARAMS])


def _output_shape():
    def fwd():
        inp = _fwd_setup_inputs(0)
        return _fwd_reference(*[inp[k] for k in FWD_PARAMS])
    out = _jax.eval_shape(fwd)
    return out.shape, out.dtype

N_MICROBATCH = 1
ADAM_LR = 0.001
ADAM_B1 = 0.9
ADAM_B2 = 0.999
ADAM_EPS = 1e-08
ADAM_WD = 0.01
ADAM_STEP = 10
PER_EXAMPLE_BATCH_AXIS = {'x': 0, 'loss_target': 0}
SHARED_INPUTS = []
_WEIGHT_DTYPES = {'w_in': _jnp.float32, 'lru_conv_w': _jnp.float32, 'lru_conv_b': _jnp.float32, 'lru_wr': _jnp.float32, 'lru_br': _jnp.float32, 'lru_wi': _jnp.float32, 'lru_bi': _jnp.float32, 'lru_lambda': _jnp.float32, 's5_a_re': _jnp.float32, 's5_a_im': _jnp.float32, 's5_b_re': _jnp.float32, 's5_b_im': _jnp.float32, 's5_c_re': _jnp.float32, 's5_c_im': _jnp.float32, 's5_d': _jnp.float32, 's5_log_step': _jnp.float32, 's5_w_glu': _jnp.float32, 's5_b_glu': _jnp.float32, 'mix_norm_g': _jnp.float32, 'w_out': _jnp.float32, 'ln1_g': _jnp.float32, 'ln1_b': _jnp.float32, 'w_up': _jnp.float32, 'ffn_conv_w': _jnp.float32, 'ffn_conv_b': _jnp.float32, 'w_down': _jnp.float32, 'ln2_g': _jnp.float32, 'ln2_b': _jnp.float32}
MOMENT_SCALE = {'w_in': 6.658840e-02, 'lru_conv_w': 8.013134e-02, 'lru_conv_b': 1.014331e+00, 'lru_wr': 2.903332e-02, 'lru_br': 2.107802e-02, 'lru_wi': 5.333776e-02, 'lru_bi': 2.762687e-02, 'lru_lambda': 3.815290e-02, 's5_a_re': 3.820654e-03, 's5_a_im': 3.417523e-03, 's5_b_re': 2.292410e-03, 's5_b_im': 2.314758e-03, 's5_c_re': 4.815805e-03, 's5_c_im': 4.631903e-03, 's5_d': 1.406875e-01, 's5_log_step': 3.598282e+00, 's5_w_glu': 2.330365e-02, 's5_b_glu': 5.104208e-02, 'mix_norm_g': 9.592126e-02, 'w_out': 2.013025e-01, 'ln1_g': 1.090142e+00, 'ln1_b': 8.541727e-01, 'w_up': 2.465089e-02, 'ffn_conv_w': 2.517966e-02, 'ffn_conv_b': 4.062188e-02, 'w_down': 8.467598e-02, 'ln2_g': 2.270641e+01, 'ln2_b': 3.724663e+00}


def _to_microbatches(a, axis):
    t = _jnp.moveaxis(a, axis, 0)
    t = t.reshape((N_MICROBATCH, t.shape[0] // N_MICROBATCH) + t.shape[1:])
    return _jnp.moveaxis(t, 1, axis + 1)


def setup_inputs(seed: int = 0) -> dict:
    inp = _fwd_setup_inputs(seed)
    key = _jax.random.fold_in(_jax.random.key(seed), 7919)
    shape, _ = _output_shape()
    out = dict(inp)
    out["loss_target"] = _jax.random.normal(_jax.random.fold_in(key, 0), shape, _jnp.float32)
    for i, name in enumerate(TWIN_WEIGHTS):
        w = inp[name].astype(_jnp.float32)
        if MOMENT_SCALE is None:
            s = _jnp.sqrt(_jnp.mean(_jnp.square(w)) + 1e-30)
        else:
            s = MOMENT_SCALE[name]
        km, kv = _jax.random.split(_jax.random.fold_in(key, i + 1))
        out[name] = w
        out["m_" + name] = s * _jax.random.normal(km, w.shape, _jnp.float32)
        out["v_" + name] = (s * s) * _jax.random.uniform(kv, w.shape, _jnp.float32, 0.5, 1.5)
    if N_MICROBATCH > 1:
        for name, axis in PER_EXAMPLE_BATCH_AXIS.items():
            out[name] = _to_microbatches(out[name], axis)
    return {'x': out['x'], 'w_in': out['w_in'], 'lru_conv_w': out['lru_conv_w'], 'lru_conv_b': out['lru_conv_b'], 'lru_wr': out['lru_wr'], 'lru_br': out['lru_br'], 'lru_wi': out['lru_wi'], 'lru_bi': out['lru_bi'], 'lru_lambda': out['lru_lambda'], 's5_a_re': out['s5_a_re'], 's5_a_im': out['s5_a_im'], 's5_b_re': out['s5_b_re'], 's5_b_im': out['s5_b_im'], 's5_c_re': out['s5_c_re'], 's5_c_im': out['s5_c_im'], 's5_d': out['s5_d'], 's5_log_step': out['s5_log_step'], 's5_w_glu': out['s5_w_glu'], 's5_b_glu': out['s5_b_glu'], 'mix_norm_g': out['mix_norm_g'], 'w_out': out['w_out'], 'ln1_g': out['ln1_g'], 'ln1_b': out['ln1_b'], 'w_up': out['w_up'], 'ffn_conv_w': out['ffn_conv_w'], 'ffn_conv_b': out['ffn_conv_b'], 'w_down': out['w_down'], 'ln2_g': out['ln2_g'], 'ln2_b': out['ln2_b'], 'loss_target': out['loss_target'], 'm_w_in': out['m_w_in'], 'm_lru_conv_w': out['m_lru_conv_w'], 'm_lru_conv_b': out['m_lru_conv_b'], 'm_lru_wr': out['m_lru_wr'], 'm_lru_br': out['m_lru_br'], 'm_lru_wi': out['m_lru_wi'], 'm_lru_bi': out['m_lru_bi'], 'm_lru_lambda': out['m_lru_lambda'], 'm_s5_a_re': out['m_s5_a_re'], 'm_s5_a_im': out['m_s5_a_im'], 'm_s5_b_re': out['m_s5_b_re'], 'm_s5_b_im': out['m_s5_b_im'], 'm_s5_c_re': out['m_s5_c_re'], 'm_s5_c_im': out['m_s5_c_im'], 'm_s5_d': out['m_s5_d'], 'm_s5_log_step': out['m_s5_log_step'], 'm_s5_w_glu': out['m_s5_w_glu'], 'm_s5_b_glu': out['m_s5_b_glu'], 'm_mix_norm_g': out['m_mix_norm_g'], 'm_w_out': out['m_w_out'], 'm_ln1_g': out['m_ln1_g'], 'm_ln1_b': out['m_ln1_b'], 'm_w_up': out['m_w_up'], 'm_ffn_conv_w': out['m_ffn_conv_w'], 'm_ffn_conv_b': out['m_ffn_conv_b'], 'm_w_down': out['m_w_down'], 'm_ln2_g': out['m_ln2_g'], 'm_ln2_b': out['m_ln2_b'], 'v_w_in': out['v_w_in'], 'v_lru_conv_w': out['v_lru_conv_w'], 'v_lru_conv_b': out['v_lru_conv_b'], 'v_lru_wr': out['v_lru_wr'], 'v_lru_br': out['v_lru_br'], 'v_lru_wi': out['v_lru_wi'], 'v_lru_bi': out['v_lru_bi'], 'v_lru_lambda': out['v_lru_lambda'], 'v_s5_a_re': out['v_s5_a_re'], 'v_s5_a_im': out['v_s5_a_im'], 'v_s5_b_re': out['v_s5_b_re'], 'v_s5_b_im': out['v_s5_b_im'], 'v_s5_c_re': out['v_s5_c_re'], 'v_s5_c_im': out['v_s5_c_im'], 'v_s5_d': out['v_s5_d'], 'v_s5_log_step': out['v_s5_log_step'], 'v_s5_w_glu': out['v_s5_w_glu'], 'v_s5_b_glu': out['v_s5_b_glu'], 'v_mix_norm_g': out['v_mix_norm_g'], 'v_w_out': out['v_w_out'], 'v_ln1_g': out['v_ln1_g'], 'v_ln1_b': out['v_ln1_b'], 'v_w_up': out['v_w_up'], 'v_ffn_conv_w': out['v_ffn_conv_w'], 'v_ffn_conv_b': out['v_ffn_conv_b'], 'v_w_down': out['v_w_down'], 'v_ln2_g': out['v_ln2_g'], 'v_ln2_b': out['v_ln2_b']}


def _loss(weights, diff, rest, loss_target):
    with _jax.named_scope("forward"):
        args = {**rest, TWIN_DIFF_INPUT: diff, **{k: w.astype(_WEIGHT_DTYPES[k]) for k, w in weights.items()}}
        y = _forward(args)
    with _jax.named_scope("loss_head"):
        err = _jnp.square(y.astype(_jnp.float32) - loss_target)
        return 0.5 * _jnp.sum(_jnp.mean(err, axis=-1)) if err.ndim else 0.5 * err


def _adamw(w, g, m, v):
    m = ADAM_B1 * m + (1.0 - ADAM_B1) * g
    v = ADAM_B2 * v + (1.0 - ADAM_B2) * _jnp.square(g)
    m_hat = m / (1.0 - ADAM_B1 ** ADAM_STEP)
    v_hat = v / (1.0 - ADAM_B2 ** ADAM_STEP)
    delta = -ADAM_LR * (m_hat / (_jnp.sqrt(v_hat) + ADAM_EPS) + ADAM_WD * w)
    return delta, m, v


def reference(x, w_in, lru_conv_w, lru_conv_b, lru_wr, lru_br, lru_wi, lru_bi, lru_lambda, s5_a_re, s5_a_im, s5_b_re, s5_b_im, s5_c_re, s5_c_im, s5_d, s5_log_step, s5_w_glu, s5_b_glu, mix_norm_g, w_out, ln1_g, ln1_b, w_up, ffn_conv_w, ffn_conv_b, w_down, ln2_g, ln2_b, loss_target, m_w_in, m_lru_conv_w, m_lru_conv_b, m_lru_wr, m_lru_br, m_lru_wi, m_lru_bi, m_lru_lambda, m_s5_a_re, m_s5_a_im, m_s5_b_re, m_s5_b_im, m_s5_c_re, m_s5_c_im, m_s5_d, m_s5_log_step, m_s5_w_glu, m_s5_b_glu, m_mix_norm_g, m_w_out, m_ln1_g, m_ln1_b, m_w_up, m_ffn_conv_w, m_ffn_conv_b, m_w_down, m_ln2_g, m_ln2_b, v_w_in, v_lru_conv_w, v_lru_conv_b, v_lru_wr, v_lru_br, v_lru_wi, v_lru_bi, v_lru_lambda, v_s5_a_re, v_s5_a_im, v_s5_b_re, v_s5_b_im, v_s5_c_re, v_s5_c_im, v_s5_d, v_s5_log_step, v_s5_w_glu, v_s5_b_glu, v_mix_norm_g, v_w_out, v_ln1_g, v_ln1_b, v_w_up, v_ffn_conv_w, v_ffn_conv_b, v_w_down, v_ln2_g, v_ln2_b):
    given = dict(x=x, w_in=w_in, lru_conv_w=lru_conv_w, lru_conv_b=lru_conv_b, lru_wr=lru_wr, lru_br=lru_br, lru_wi=lru_wi, lru_bi=lru_bi, lru_lambda=lru_lambda, s5_a_re=s5_a_re, s5_a_im=s5_a_im, s5_b_re=s5_b_re, s5_b_im=s5_b_im, s5_c_re=s5_c_re, s5_c_im=s5_c_im, s5_d=s5_d, s5_log_step=s5_log_step, s5_w_glu=s5_w_glu, s5_b_glu=s5_b_glu, mix_norm_g=mix_norm_g, w_out=w_out, ln1_g=ln1_g, ln1_b=ln1_b, w_up=w_up, ffn_conv_w=ffn_conv_w, ffn_conv_b=ffn_conv_b, w_down=w_down, ln2_g=ln2_g, ln2_b=ln2_b, loss_target=loss_target, m_w_in=m_w_in, m_lru_conv_w=m_lru_conv_w, m_lru_conv_b=m_lru_conv_b, m_lru_wr=m_lru_wr, m_lru_br=m_lru_br, m_lru_wi=m_lru_wi, m_lru_bi=m_lru_bi, m_lru_lambda=m_lru_lambda, m_s5_a_re=m_s5_a_re, m_s5_a_im=m_s5_a_im, m_s5_b_re=m_s5_b_re, m_s5_b_im=m_s5_b_im, m_s5_c_re=m_s5_c_re, m_s5_c_im=m_s5_c_im, m_s5_d=m_s5_d, m_s5_log_step=m_s5_log_step, m_s5_w_glu=m_s5_w_glu, m_s5_b_glu=m_s5_b_glu, m_mix_norm_g=m_mix_norm_g, m_w_out=m_w_out, m_ln1_g=m_ln1_g, m_ln1_b=m_ln1_b, m_w_up=m_w_up, m_ffn_conv_w=m_ffn_conv_w, m_ffn_conv_b=m_ffn_conv_b, m_w_down=m_w_down, m_ln2_g=m_ln2_g, m_ln2_b=m_ln2_b, v_w_in=v_w_in, v_lru_conv_w=v_lru_conv_w, v_lru_conv_b=v_lru_conv_b, v_lru_wr=v_lru_wr, v_lru_br=v_lru_br, v_lru_wi=v_lru_wi, v_lru_bi=v_lru_bi, v_lru_lambda=v_lru_lambda, v_s5_a_re=v_s5_a_re, v_s5_a_im=v_s5_a_im, v_s5_b_re=v_s5_b_re, v_s5_b_im=v_s5_b_im, v_s5_c_re=v_s5_c_re, v_s5_c_im=v_s5_c_im, v_s5_d=v_s5_d, v_s5_log_step=v_s5_log_step, v_s5_w_glu=v_s5_w_glu, v_s5_b_glu=v_s5_b_glu, v_mix_norm_g=v_mix_norm_g, v_w_out=v_w_out, v_ln1_g=v_ln1_g, v_ln1_b=v_ln1_b, v_w_up=v_w_up, v_ffn_conv_w=v_ffn_conv_w, v_ffn_conv_b=v_ffn_conv_b, v_w_down=v_w_down, v_ln2_g=v_ln2_g, v_ln2_b=v_ln2_b)
    weights = {n: given[n] for n in TWIN_WEIGHTS}
    shared = {n: given[n] for n in SHARED_INPUTS}
    per_example = {n: given[n] for n in ['x']}
    grad_fn = _jax.value_and_grad(_loss, argnums=(0, 1))

    def one_microbatch(ex, loss_target):
        ex = dict(ex)
        diff = ex.pop(TWIN_DIFF_INPUT)
        return grad_fn(weights, diff, {**shared, **ex}, loss_target)

    if N_MICROBATCH == 1:
        loss, (grad_w, grad_x) = one_microbatch(per_example, given["loss_target"])
    else:
        def body(carry, xs):
            loss_sum, grad_sum = carry
            l_k, (gw_k, gx_k) = one_microbatch(xs[0], xs[1])
            with _jax.named_scope("update"):
                return (loss_sum + l_k, _jax.tree.map(_jnp.add, grad_sum, gw_k)), gx_k

        init = (_jnp.zeros((), _jnp.float32), _jax.tree.map(_jnp.zeros_like, weights))
        (loss, grad_w), grad_x = _jax.lax.scan(body, init, (per_example, given["loss_target"]))
    with _jax.named_scope("update"):
        delta_w, new_m, new_v = {}, {}, {}
        for n in TWIN_WEIGHTS:
            delta_w[n], new_m[n], new_v[n] = _adamw(weights[n], grad_w[n], given["m_" + n], given["v_" + n])
    return (loss, grad_x, *[grad_w[n] for n in TWIN_WEIGHTS], *[delta_w[n] for n in TWIN_WEIGHTS],
            *[new_m[n] for n in TWIN_WEIGHTS], *[new_v[n] for n in TWIN_WEIGHTS])
```

```python
import functools
import math

import jax
import jax.numpy as jnp
from jax import lax
from jax.experimental import pallas as pl
from jax.experimental.pallas import tpu as pltpu

f32 = jnp.float32
bf16 = jnp.bfloat16
MESH = pl.DeviceIdType.MESH

D_MODEL = 1024
ATTN_WIDTH = 384
LRU_WIDTH = 384
S5_WIDTH = 256
HEAD_DIM = 64
N_LRU_HEADS = 6
N_S5_GROUPS = 16
S5_GROUP = 16
S5_STATE = 64
S5_LANES = N_S5_GROUPS * S5_STATE
D_FF = 3072
D_IN = 2176
LRU_C = 8.0
ROPE_THETA = 10000.0
DILATIONS = (1, 4, 16)
ATTN_BLOCK = 128
DEPTH = 2
ALPHA = (2 * DEPTH) ** 0.25
LN_EPS = 1e-5
RMS_EPS = 1e-6
ADAM_LR, ADAM_B1, ADAM_B2, ADAM_EPS, ADAM_WD, ADAM_STEP = 0.001, 0.9, 0.999, 1e-08, 0.01, 10

SUBLANES = 8
LANES = 128
VMEM_LIMIT = 56 * 1024 * 1024
ROW_TILE = 512
NEG = -1e30


def _cp(*sem):
    return pltpu.CompilerParams(dimension_semantics=sem if sem else None, vmem_limit_bytes=VMEM_LIMIT)


def _pick(dim, pref, align=LANES):
    if dim <= pref:
        return dim
    t = (pref // align) * align
    while t >= align:
        if dim % t == 0:
            return t
        t -= align
    return dim


def _gelu(x):
    return jax.nn.gelu(x)


def _gelu_grad(x):
    c = math.sqrt(2.0 / math.pi)
    t = jnp.tanh(c * (x + 0.044715 * x * x * x))
    return 0.5 * (1.0 + t) + 0.5 * x * (1.0 - t * t) * c * (1.0 + 3 * 0.044715 * x * x)


def _sigmoid(x):
    return jax.nn.sigmoid(x)


def _expm1(x):
    p = 1.0 + x / 9.0
    for n in (8.0, 7.0, 6.0, 5.0, 4.0, 3.0, 2.0):
        p = 1.0 + (x / n) * p
    return jnp.where(jnp.abs(x) < 0.3, x * p, jnp.exp(x) - 1.0)


def _dot(a, b, dims):
    return lax.dot_general(a, b, (dims, ((), ())), preferred_element_type=f32)


NN = ((1,), (0,))
NT = ((1,), (1,))
TN = ((0,), (0,))


def _mm(a, b, mode, name, out_dtype=f32, tm=1024, tn=1024, tk=512, add=None):
    if mode == "nn":
        (M, K), N = a.shape, b.shape[1]
    elif mode == "nt":
        (M, K), N = a.shape, b.shape[0]
    else:
        (K, M), N = a.shape, b.shape[1]
    tm, tn, tk = _pick(M, tm), _pick(N, tn), _pick(K, tk)
    nk = K // tk
    dims = {"nn": NN, "nt": NT, "tn": TN}[mode]

    def body(a_ref, b_ref, *rest):
        o_ref, acc = rest[-2:]
        k = pl.program_id(2)

        @pl.when(k == 0)
        def _():
            acc[...] = jnp.zeros_like(acc) if add is None else rest[0][...]

        acc[...] += _dot(a_ref[...].astype(bf16), b_ref[...].astype(bf16), dims)

        @pl.when(k == nk - 1)
        def _():
            o_ref[...] = acc[...].astype(o_ref.dtype)

    if mode == "tn":
        a_spec = pl.BlockSpec((tk, tm), lambda i, j, k: (k, i))
    else:
        a_spec = pl.BlockSpec((tm, tk), lambda i, j, k: (i, k))
    if mode == "nt":
        b_spec = pl.BlockSpec((tn, tk), lambda i, j, k: (j, k))
    else:
        b_spec = pl.BlockSpec((tk, tn), lambda i, j, k: (k, j))
    o_spec = pl.BlockSpec((tm, tn), lambda i, j, k: (i, j))
    return pl.pallas_call(
        body, name=name, grid=(M // tm, N // tn, nk),
        in_specs=[a_spec, b_spec] + ([] if add is None else [o_spec]), out_specs=o_spec,
        out_shape=jax.ShapeDtypeStruct((M, N), out_dtype),
        scratch_shapes=[pltpu.VMEM((tm, tn), f32)],
        compiler_params=_cp("parallel", "parallel", "arbitrary"),
    )(*((a, b) if add is None else (a, b, add)))


def _shift_down(cur, prev8, k):
    if k == 0:
        return cur
    ext = jnp.concatenate([prev8, cur], axis=0)
    return pltpu.roll(ext, k, 0)[SUBLANES:]


def _shift_up(cur, next8, k):
    if k == 0:
        return cur
    n = cur.shape[0] + SUBLANES
    ext = jnp.concatenate([cur, next8], axis=0)
    return pltpu.roll(ext, n - k, 0)[: cur.shape[0]]


def _prev_halo_spec(rt, cols, ncolblk_fn):
    per = rt // SUBLANES
    return pl.BlockSpec((SUBLANES, cols), lambda *g: (jnp.maximum(g[-1] * per - 1, 0), ncolblk_fn(*g)))


def _next_halo_spec(rt, cols, nrows, ncolblk_fn):
    per = rt // SUBLANES
    last = nrows // SUBLANES - 1
    return pl.BlockSpec((SUBLANES, cols), lambda *g: (jnp.minimum((g[-1] + 1) * per, last), ncolblk_fn(*g)))


def _ln_fwd(h, branch, g, b, name):
    S, D = h.shape
    rt = _pick(S, ROW_TILE, SUBLANES)

    def body(h_ref, m_ref, g_ref, b_ref, o_ref, z_ref):
        z = ALPHA * h_ref[...] + m_ref[...]
        mu = jnp.mean(z, axis=-1, keepdims=True)
        zc = z - mu
        var = jnp.mean(zc * zc, axis=-1, keepdims=True)
        o_ref[...] = zc * lax.rsqrt(var + LN_EPS) * g_ref[...] + b_ref[...]
        z_ref[...] = z

    row = pl.BlockSpec((rt, D), lambda i: (i, 0))
    vec = pl.BlockSpec((1, D), lambda i: (0, 0))
    return pl.pallas_call(
        body, name=name, grid=(S // rt,), in_specs=[row, row, vec, vec], out_specs=[row, row],
        out_shape=[jax.ShapeDtypeStruct((S, D), f32)] * 2, compiler_params=_cp("parallel"),
    )(h, branch, g.reshape(1, D), b.reshape(1, D))


def _ln_bwd(dy_a, dy_b, z, g, name):
    S, D = z.shape
    rt = _pick(S, ROW_TILE, SUBLANES)
    two = dy_a is not None

    def body(*refs):
        if two:
            a_ref, b_ref, z_ref, g_ref, dz_ref, acc_ref = refs
            dy = ALPHA * a_ref[...] + b_ref[...]
        else:
            b_ref, z_ref, g_ref, dz_ref, acc_ref = refs
            dy = b_ref[...]
        z = z_ref[...]
        mu = jnp.mean(z, axis=-1, keepdims=True)
        zc = z - mu
        var = jnp.mean(zc * zc, axis=-1, keepdims=True)
        rstd = lax.rsqrt(var + LN_EPS)
        xhat = zc * rstd
        dxh = dy * g_ref[...]
        m1 = jnp.mean(dxh, axis=-1, keepdims=True)
        m2 = jnp.mean(dxh * xhat, axis=-1, keepdims=True)
        dz_ref[...] = rstd * (dxh - m1 - xhat * m2)

        @pl.when(pl.program_id(0) == 0)
        def _():
            acc_ref[...] = jnp.zeros_like(acc_ref)

        acc_ref[0:1, :] += jnp.sum(dy * xhat, axis=0, keepdims=True)
        acc_ref[1:2, :] += jnp.sum(dy, axis=0, keepdims=True)

    row = pl.BlockSpec((rt, D), lambda i: (i, 0))
    vec = pl.BlockSpec((1, D), lambda i: (0, 0))
    acc = pl.BlockSpec((SUBLANES, D), lambda i: (0, 0))
    ins = ([dy_a] if two else []) + [dy_b, z, g.reshape(1, D)]
    return pl.pallas_call(
        body, name=name, grid=(S // rt,), in_specs=[row] * (len(ins) - 1) + [vec], out_specs=[row, acc],
        out_shape=[jax.ShapeDtypeStruct((S, D), f32), jax.ShapeDtypeStruct((SUBLANES, D), f32)],
        compiler_params=_cp("arbitrary"),
    )(*ins)


def _loss_head(y, target):
    S, D = y.shape
    rt = _pick(S, ROW_TILE, SUBLANES)

    def body(y_ref, t_ref, dy_ref, acc_ref):
        e = y_ref[...] - t_ref[...]
        dy_ref[...] = e * (1.0 / D)

        @pl.when(pl.program_id(0) == 0)
        def _():
            acc_ref[...] = jnp.zeros_like(acc_ref)

        part = jnp.sum(jnp.mean(e * e, axis=-1, keepdims=True), axis=0, keepdims=True)
        acc_ref[...] += 0.5 * part

    row = pl.BlockSpec((rt, D), lambda i: (i, 0))
    return pl.pallas_call(
        body, name="loss_head", grid=(S // rt,), in_specs=[row, row],
        out_specs=[row, pl.BlockSpec((1, 1), lambda i: (0, 0))],
        out_shape=[jax.ShapeDtypeStruct((S, D), f32), jax.ShapeDtypeStruct((1, 1), f32)],
        compiler_params=_cp("arbitrary"),
    )(y, target)


def _axpy(a, b, name):
    S, D = a.shape
    rt = _pick(S, ROW_TILE, SUBLANES)

    def body(a_ref, b_ref, o_ref):
        o_ref[...] = ALPHA * a_ref[...] + b_ref[...]

    row = pl.BlockSpec((rt, D), lambda i: (i, 0))
    return pl.pallas_call(
        body, name=name, grid=(S // rt,), in_specs=[row, row], out_specs=row,
        out_shape=jax.ShapeDtypeStruct((S, D), f32), compiler_params=_cp("parallel"),
    )(a, b)


def _rope_tables(S):
    rt = _pick(S, ROW_TILE, SUBLANES)

    def body(c_ref, s_ref):
        pos = (pl.program_id(0) * rt + lax.broadcasted_iota(jnp.int32, (rt, LANES), 0)).astype(f32)
        lane = lax.broadcasted_iota(jnp.int32, (rt, LANES), 1)
        j = (lane % (HEAD_DIM // 2)).astype(f32)
        inv = jnp.exp((-j * 2.0 / HEAD_DIM) * math.log(ROPE_THETA))
        ang = pos * inv
        c = jnp.cos(ang)
        s = jnp.where(lane % HEAD_DIM < HEAD_DIM // 2, -jnp.sin(ang), jnp.sin(ang))
        c_ref[...] = jnp.concatenate([c, c, c], axis=1)
        s_ref[...] = jnp.concatenate([s, s, s], axis=1)

    row = pl.BlockSpec((rt, ATTN_WIDTH), lambda i: (i, 0))
    return pl.pallas_call(
        body, name="rope_tables", grid=(S // rt,), in_specs=[], out_specs=[row, row],
        out_shape=[jax.ShapeDtypeStruct((S, ATTN_WIDTH), f32)] * 2, compiler_params=_cp("parallel"),
    )()


def _swap_halves(x):
    lane = lax.broadcasted_iota(jnp.int32, x.shape, 1)
    half = HEAD_DIM // 2
    return jnp.where(lane % HEAD_DIM < half, pltpu.roll(x, x.shape[1] - half, 1), pltpu.roll(x, half, 1))


def _rope_fwd(q, k, cos, sin):
    S, W = q.shape
    rt = _pick(S, ROW_TILE, SUBLANES)

    def body(q_ref, k_ref, c_ref, s_ref, qo_ref, ko_ref):
        c, s = c_ref[...], s_ref[...]
        qo_ref[...] = q_ref[...] * c + _swap_halves(q_ref[...]) * s
        ko_ref[...] = k_ref[...] * c + _swap_halves(k_ref[...]) * s

    row = pl.BlockSpec((rt, W), lambda i: (i, 0))
    return pl.pallas_call(
        body, name="rope_fwd", grid=(S // rt,), in_specs=[row] * 4, out_specs=[row, row],
        out_shape=[jax.ShapeDtypeStruct((S, W), f32)] * 2, compiler_params=_cp("parallel"),
    )(q, k, cos, sin)


def _rope_bwd(dq3, dk3, dv3, cos, sin):
    _, S, W = dq3.shape
    rt = _pick(S, ROW_TILE, SUBLANES)

    def body(q_ref, k_ref, v_ref, c_ref, s_ref, qo_ref, ko_ref, vo_ref):
        c, s = c_ref[...], s_ref[...]
        dq = q_ref[0] + q_ref[1] + q_ref[2]
        dk = k_ref[0] + k_ref[1] + k_ref[2]
        qo_ref[...] = dq * c + _swap_halves(dq * s)
        ko_ref[...] = dk * c + _swap_halves(dk * s)
        vo_ref[...] = v_ref[0] + v_ref[1] + v_ref[2]

    row = pl.BlockSpec((rt, W), lambda i: (i, 0))
    row3 = pl.BlockSpec((3, rt, W), lambda i: (0, i, 0))
    return pl.pallas_call(
        body, name="rope_bwd", grid=(S // rt,), in_specs=[row3, row3, row3, row, row], out_specs=[row] * 3,
        out_shape=[jax.ShapeDtypeStruct((S, W), f32)] * 3, compiler_params=_cp("parallel"),
    )(dq3, dk3, dv3, cos, sin)


def _to_pairs(x, d):
    S = x.shape[0]
    return x.reshape(S // d, d, 3, LANES).transpose(2, 1, 0, 3).reshape(3, S, LANES)


def _from_pairs(xp, d):
    S = xp.shape[1]
    return xp.reshape(3, d, S // d, LANES).transpose(2, 1, 0, 3).reshape(S, 3 * LANES)


def _to_branches(x):
    return jnp.stack([_to_pairs(x, d) for d in DILATIONS])


def _from_branches(xp):
    return jnp.stack([_from_pairs(xp[i], d) for i, d in enumerate(DILATIONS)])


def _blocks_per_class(branch, nb):
    return jnp.where(branch == 0, nb // DILATIONS[0], jnp.where(branch == 1, nb // DILATIONS[1], nb // DILATIONS[2]))


def _attn_fwd(qp, kp, vp):
    _, _, S, _ = qp.shape
    B = ATTN_BLOCK
    nb = S // B
    scale = HEAD_DIM ** -0.5

    def body(q_ref, kp_ref, kc_ref, vp_ref, vc_ref, o_ref, l_ref):
        br, b = pl.program_id(0), pl.program_id(2)
        first = (b % _blocks_per_class(br, nb)) == 0
        qi = lax.broadcasted_iota(jnp.int32, (B, 2 * B), 0)
        ki = lax.broadcasted_iota(jnp.int32, (B, 2 * B), 1)
        dist = qi + B - ki
        valid = (dist >= 0) & (dist <= B) & ((ki >= B) | jnp.logical_not(first))
        q = q_ref[...].astype(bf16)
        kcat = jnp.concatenate([kp_ref[...], kc_ref[...]], axis=0).astype(bf16)
        vcat = jnp.concatenate([vp_ref[...], vc_ref[...]], axis=0).astype(bf16)
        outs, lses = [], []
        for h in range(2):
            sl = slice(h * HEAD_DIM, (h + 1) * HEAD_DIM)
            s = _dot(q[:, sl], kcat[:, sl], NT) * scale
            s = jnp.where(valid, s, NEG)
            m = jnp.max(s, axis=1, keepdims=True)
            p = jnp.exp(s - m)
            l = jnp.sum(p, axis=1, keepdims=True)
            o = _dot(p.astype(bf16), vcat[:, sl], NN) / l
            outs.append(o)
            lses.append(jnp.broadcast_to(m + jnp.log(l), (B, HEAD_DIM)))
        o_ref[...] = jnp.concatenate(outs, axis=1)
        l_ref[...] = jnp.concatenate(lses, axis=1)

    cur = pl.BlockSpec((None, None, B, LANES), lambda i, p, b: (i, p, b, 0))
    prev = pl.BlockSpec((None, None, B, LANES), lambda i, p, b: (i, p, jnp.maximum(b - 1, 0), 0))
    return pl.pallas_call(
        body, name="attn_fwd", grid=(3, 3, nb), in_specs=[cur, prev, cur, prev, cur], out_specs=[cur, cur],
        out_shape=[jax.ShapeDtypeStruct(qp.shape, f32)] * 2, compiler_params=_cp("parallel", "parallel", "parallel"),
    )(qp, kp, kp, vp, vp)


def _attn_combine(o3, l3):
    _, S, W = o3.shape
    rt = _pick(S, ROW_TILE, SUBLANES)

    def body(o_ref, l_ref, a_ref, lt_ref):
        l0, l1, l2 = l_ref[0], l_ref[1], l_ref[2]
        mx = jnp.maximum(jnp.maximum(l0, l1), l2)
        e0, e1, e2 = jnp.exp(l0 - mx), jnp.exp(l1 - mx), jnp.exp(l2 - mx)
        den = e0 + e1 + e2
        a_ref[...] = (e0 * o_ref[0] + e1 * o_ref[1] + e2 * o_ref[2]) / den
        lt_ref[...] = mx + jnp.log(den)

    row = pl.BlockSpec((rt, W), lambda i: (i, 0))
    row3 = pl.BlockSpec((3, rt, W), lambda i: (0, i, 0))
    return pl.pallas_call(
        body, name="attn_combine", grid=(S // rt,), in_specs=[row3, row3], out_specs=[row, row],
        out_shape=[jax.ShapeDtypeStruct((S, W), f32)] * 2, compiler_params=_cp("parallel"),
    )(o3, l3)


def _attn_bwd(qp, kp, vp, dop, lp, dlp):
    _, _, S, _ = qp.shape
    B = ATTN_BLOCK
    nb = S // B
    scale = HEAD_DIM ** -0.5

    def body(qc_ref, qn_ref, kp_ref, kc_ref, vp_ref, vc_ref, doc_ref, don_ref, lc_ref, ln_ref, dc_ref, dn_ref,
             dq_ref, dk_ref, dv_ref):
        br, b = pl.program_id(0), pl.program_id(2)
        bpc = _blocks_per_class(br, nb)
        first = (b % bpc) == 0
        nxt_ok = ((b + 1) % bpc != 0) & (b + 1 < nb)
        qi = lax.broadcasted_iota(jnp.int32, (B, 2 * B), 0)
        ki = lax.broadcasted_iota(jnp.int32, (B, 2 * B), 1)
        dist1 = qi + B - ki
        valid1 = (dist1 >= 0) & (dist1 <= B) & ((ki >= B) | jnp.logical_not(first))
        ri = lax.broadcasted_iota(jnp.int32, (2 * B, B), 0)
        ci = lax.broadcasted_iota(jnp.int32, (2 * B, B), 1)
        dist2 = ri - ci
        valid2 = (dist2 >= 0) & (dist2 <= B) & ((ri < B) | nxt_ok)
        qc = qc_ref[...].astype(bf16)
        qcat = jnp.concatenate([qc_ref[...], qn_ref[...]], axis=0).astype(bf16)
        kc = kc_ref[...].astype(bf16)
        kcat = jnp.concatenate([kp_ref[...], kc_ref[...]], axis=0).astype(bf16)
        vc = vc_ref[...].astype(bf16)
        vcat = jnp.concatenate([vp_ref[...], vc_ref[...]], axis=0).astype(bf16)
        doc = doc_ref[...].astype(bf16)
        docat = jnp.concatenate([doc_ref[...], don_ref[...]], axis=0).astype(bf16)
        lcat = jnp.concatenate([lc_ref[...], ln_ref[...]], axis=0)
        dcat = jnp.concatenate([dc_ref[...], dn_ref[...]], axis=0)
        dqs, dks, dvs = [], [], []
        for h in range(2):
            sl = slice(h * HEAD_DIM, (h + 1) * HEAD_DIM)
            c0 = h * HEAD_DIM
            s1 = _dot(qc[:, sl], kcat[:, sl], NT) * scale
            p1 = jnp.where(valid1, jnp.exp(s1 - lc_ref[:, c0:c0 + 1]), 0.0)
            dp1 = _dot(doc[:, sl], vcat[:, sl], NT)
            ds1 = p1 * (dp1 - dc_ref[:, c0:c0 + 1]) * scale
            dqs.append(_dot(ds1.astype(bf16), kcat[:, sl], NN))
            s2 = _dot(qcat[:, sl], kc[:, sl], NT) * scale
            p2 = jnp.where(valid2, jnp.exp(s2 - lcat[:, c0:c0 + 1]), 0.0)
            dvs.append(_dot(p2.astype(bf16), docat[:, sl], TN))
            dp2 = _dot(docat[:, sl], vc[:, sl], NT)
            ds2 = p2 * (dp2 - dcat[:, c0:c0 + 1]) * scale
            dks.append(_dot(ds2.astype(bf16), qcat[:, sl], TN))
        dq_ref[...] = jnp.concatenate(dqs, axis=1)
        dk_ref[...] = jnp.concatenate(dks, axis=1)
        dv_ref[...] = jnp.concatenate(dvs, axis=1)

    cur = pl.BlockSpec((None, None, B, LANES), lambda i, p, b: (i, p, b, 0))
    prev = pl.BlockSpec((None, None, B, LANES), lambda i, p, b: (i, p, jnp.maximum(b - 1, 0), 0))
    nxt = pl.BlockSpec((None, None, B, LANES), lambda i, p, b: (i, p, jnp.minimum(b + 1, nb - 1), 0))
    return pl.pallas_call(
        body, name="attn_bwd", grid=(3, 3, nb),
        in_specs=[cur, nxt, prev, cur, prev, cur, cur, nxt, cur, nxt, cur, nxt], out_specs=[cur, cur, cur],
        out_shape=[jax.ShapeDtypeStruct(qp.shape, f32)] * 3, compiler_params=_cp("parallel", "parallel", "parallel"),
    )(qp, qp, kp, kp, vp, vp, dop, dop, lp, lp, dlp, dlp)


def _softplus_neg(lam):
    return jnp.maximum(-lam, 0.0) + jnp.log1p(jnp.exp(-jnp.abs(lam)))


def _lru_pre(xr, conv_w, conv_b, wr, br, wi, bi, lam):
    S, W = xr.shape
    rt = _pick(S, ROW_TILE, SUBLANES)
    K = conv_w.shape[0]

    def body(x_ref, xp_ref, cw_ref, cb_ref, wr_ref, br_ref, wi_ref, bi_ref, lam_ref,
             xc_ref, r_ref, i_ref, la_ref, u_ref):
        prev = jnp.where(pl.program_id(0) == 0, 0.0, xp_ref[...])
        x = x_ref[...]
        xc = cb_ref[...] + cw_ref[K - 1:K, :] * x
        for k in range(K - 1):
            xc = xc + cw_ref[k:k + 1, :] * _shift_down(x, prev, K - 1 - k)
        xb = xc.astype(bf16)
        r = _sigmoid(_dot(xb, wr_ref[...], NN) + br_ref[...])
        i = _sigmoid(_dot(xb, wi_ref[...], NN) + bi_ref[...])
        log_a = -LRU_C * r * _softplus_neg(lam_ref[...])
        u = jnp.sqrt(-_expm1(2.0 * log_a)) * (i * xc)
        xc_ref[...], r_ref[...], i_ref[...], la_ref[...], u_ref[...] = xc, r, i, log_a, u

    row = pl.BlockSpec((rt, W), lambda i: (i, 0))
    halo = _prev_halo_spec(rt, W, lambda i: 0)
    vec = pl.BlockSpec((1, W), lambda i: (0, 0))
    return pl.pallas_call(
        body, name="lru_pre", grid=(S // rt,),
        in_specs=[row, halo, pl.BlockSpec((K, W), lambda i: (0, 0)), vec,
                  pl.BlockSpec((W, W), lambda i: (0, 0)), vec, pl.BlockSpec((W, W), lambda i: (0, 0)), vec, vec],
        out_specs=[row] * 5, out_shape=[jax.ShapeDtypeStruct((S, W), f32)] * 5, compiler_params=_cp("parallel"),
    )(xr, xr, conv_w, conv_b.reshape(1, W), wr, br.reshape(1, W), wi, bi.reshape(1, W), lam.reshape(1, W))


def _tile_rows(shape):
    return lax.broadcasted_iota(jnp.int32, shape, 0)


def _lru_scan(log_a, u, gate):
    S, W = u.shape
    rt = _pick(S, ROW_TILE, SUBLANES)
    T = SUBLANES

    def body(la_ref, u_ref, g_ref, h_ref, o_ref, carry):
        @pl.when(pl.program_id(0) == 0)
        def _():
            carry[...] = jnp.zeros_like(carry)

        row = _tile_rows((T, W))

        def step(t, hp):
            r0 = pl.multiple_of(t * T, T)
            a = jnp.exp(la_ref[pl.ds(r0, T), :])
            x = u_ref[pl.ds(r0, T), :]
            for k in (1, 2, 4):
                x = x + a * jnp.where(row >= k, pltpu.roll(x, k, 0), 0.0)
                a = a * jnp.where(row >= k, pltpu.roll(a, k, 0), 1.0)
            h = x + a * hp
            h_ref[pl.ds(r0, T), :] = h
            o_ref[pl.ds(r0, T), :] = h * _gelu(g_ref[pl.ds(r0, T), :])
            return h[T - 1:T, :]

        carry[0:1, :] = lax.fori_loop(0, rt // T, step, carry[0:1, :])

    row = pl.BlockSpec((rt, W), lambda i: (i, 0))
    return pl.pallas_call(
        body, name="lru_scan", grid=(S // rt,), in_specs=[row] * 3, out_specs=[row] * 2,
        out_shape=[jax.ShapeDtypeStruct((S, W), f32)] * 2, scratch_shapes=[pltpu.VMEM((T, W), f32)],
        compiler_params=_cp("arbitrary"),
    )(log_a, u, gate)


def _lru_scan_bwd(dlru, gate, h, log_a):
    S, W = h.shape
    rt = _pick(S, ROW_TILE, SUBLANES)
    T = SUBLANES
    nblk = S // rt

    def body(d_ref, g_ref, h_ref, la_ref, go_ref, dg_ref, carry):
        @pl.when(pl.program_id(0) == 0)
        def _():
            carry[...] = jnp.zeros_like(carry)

        row = _tile_rows((T, W))

        def step(j, c):
            gn, an = c
            t = rt // T - 1 - j
            r0 = pl.multiple_of(t * T, T)
            d = d_ref[pl.ds(r0, T), :]
            gate = g_ref[pl.ds(r0, T), :]
            a = jnp.exp(la_ref[pl.ds(r0, T), :])
            dg_ref[pl.ds(r0, T), :] = d * h_ref[pl.ds(r0, T), :] * _gelu_grad(gate)
            x = d * _gelu(gate)
            b = jnp.where(row < T - 1, pltpu.roll(a, T - 1, 0), an)
            for k in (1, 2, 4):
                x = x + b * jnp.where(row < T - k, pltpu.roll(x, T - k, 0), 0.0)
                b = b * jnp.where(row < T - k, pltpu.roll(b, T - k, 0), 1.0)
            g = x + b * gn
            go_ref[pl.ds(r0, T), :] = g
            return g[0:1, :], a[0:1, :]

        gn, an = lax.fori_loop(0, rt // T, step, (carry[0:1, :], carry[1:2, :]))
        carry[0:1, :] = gn
        carry[1:2, :] = an

    row = pl.BlockSpec((rt, W), lambda i: (nblk - 1 - i, 0))
    return pl.pallas_call(
        body, name="lru_scan_bwd", grid=(nblk,), in_specs=[row] * 4, out_specs=[row] * 2,
        out_shape=[jax.ShapeDtypeStruct((S, W), f32)] * 2, scratch_shapes=[pltpu.VMEM((T, W), f32)],
        compiler_params=_cp("arbitrary"),
    )(dlru, gate, h, log_a)


def _lru_gate_bwd(g, h, xc, r, i, log_a, wr, wi, lam):
    S, W = g.shape
    rt = _pick(S, ROW_TILE, SUBLANES)

    def body(g_ref, h_ref, hp_ref, xc_ref, r_ref, i_ref, la_ref, wr_ref, wi_ref, lam_ref,
             dxc_ref, dwr_ref, dwi_ref, acc_ref):
        @pl.when(pl.program_id(0) == 0)
        def _():
            dwr_ref[...] = jnp.zeros_like(dwr_ref)
            dwi_ref[...] = jnp.zeros_like(dwi_ref)
            acc_ref[...] = jnp.zeros_like(acc_ref)

        prev = jnp.where(pl.program_id(0) == 0, 0.0, hp_ref[...])
        gg, xc, r, i, log_a, lam = g_ref[...], xc_ref[...], r_ref[...], i_ref[...], la_ref[...], lam_ref[...]
        hm1 = _shift_down(h_ref[...], prev, 1)
        a = jnp.exp(log_a)
        s = jnp.sqrt(-_expm1(2.0 * log_a))
        da = gg * hm1
        di = gg * s * xc
        dxc = gg * s * i
        ds = gg * i * xc
        dlog_a = da * a - ds * (a * a / s)
        sp = _softplus_neg(lam)
        dr = dlog_a * (-LRU_C * sp)
        dsp = jnp.sum(dlog_a * (-LRU_C * r), axis=0, keepdims=True)
        dpr = dr * r * (1.0 - r)
        dpi = di * i * (1.0 - i)
        dprb, dpib, xb = dpr.astype(bf16), dpi.astype(bf16), xc.astype(bf16)
        dxc_ref[...] = dxc + _dot(dprb, wr_ref[...], NT) + _dot(dpib, wi_ref[...], NT)
        dwr_ref[...] += _dot(xb, dprb, TN)
        dwi_ref[...] += _dot(xb, dpib, TN)
        acc_ref[0:1, :] += jnp.sum(dpr, axis=0, keepdims=True)
        acc_ref[1:2, :] += jnp.sum(dpi, axis=0, keepdims=True)
        acc_ref[2:3, :] += dsp * (-_sigmoid(-lam))

    row = pl.BlockSpec((rt, W), lambda i: (i, 0))
    halo = _prev_halo_spec(rt, W, lambda i: 0)
    vec = pl.BlockSpec((1, W), lambda i: (0, 0))
    mat = pl.BlockSpec((W, W), lambda i: (0, 0))
    acc = pl.BlockSpec((SUBLANES, W), lambda i: (0, 0))
    return pl.pallas_call(
        body, name="lru_gate_bwd", grid=(S // rt,),
        in_specs=[row, row, halo, row, row, row, row, mat, mat, vec], out_specs=[row, mat, mat, acc],
        out_shape=[jax.ShapeDtypeStruct((S, W), f32), jax.ShapeDtypeStruct((W, W), f32),
                   jax.ShapeDtypeStruct((W, W), f32), jax.ShapeDtypeStruct((SUBLANES, W), f32)],
        compiler_params=_cp("arbitrary"),
    )(g, h, h, xc, r, i, log_a, wr, wi, lam.reshape(1, W))


def _conv_bwd(dy, x, conv_w, name, col_tile=None, out_dtype=f32):
    S, C = dy.shape
    K = conv_w.shape[0]
    ct = C if col_tile is None else col_tile
    rt = _pick(S, ROW_TILE, SUBLANES)
    nrt = S // rt

    def body(dy_ref, dyn_ref, x_ref, xp_ref, w_ref, dx_ref, acc_ref):
        i = pl.program_id(1)

        @pl.when(i == 0)
        def _():
            acc_ref[...] = jnp.zeros_like(acc_ref)

        nxt = jnp.where(i == nrt - 1, 0.0, dyn_ref[...])
        prev = jnp.where(i == 0, 0.0, xp_ref[...])
        dy, x = dy_ref[...], x_ref[...]
        dx = w_ref[K - 1:K, :] * dy
        for k in range(K - 1):
            dx = dx + w_ref[k:k + 1, :] * _shift_up(dy, nxt, K - 1 - k)
        dx_ref[...] = dx.astype(dx_ref.dtype)
        for k in range(K):
            acc_ref[k:k + 1, :] += jnp.sum(dy * _shift_down(x, prev, K - 1 - k), axis=0, keepdims=True)
        acc_ref[K:K + 1, :] += jnp.sum(dy, axis=0, keepdims=True)

    row = pl.BlockSpec((rt, ct), lambda j, i: (i, j))
    return pl.pallas_call(
        body, name=name, grid=(C // ct, nrt),
        in_specs=[row, _next_halo_spec(rt, ct, S, lambda j, i: j), row, _prev_halo_spec(rt, ct, lambda j, i: j),
                  pl.BlockSpec((K, ct), lambda j, i: (0, j))],
        out_specs=[row, pl.BlockSpec((SUBLANES, ct), lambda j, i: (0, j))],
        out_shape=[jax.ShapeDtypeStruct((S, C), out_dtype), jax.ShapeDtypeStruct((SUBLANES, C), f32)],
        compiler_params=_cp("parallel", "arbitrary"),
    )(dy, dy, x, x, conv_w)


def _s5_param_fn(a_re, a_im, ls, bt_re, bt_im):
    step = jnp.exp(ls)
    dt_re, dt_im = step * a_re, step * a_im
    mag = jnp.exp(dt_re)
    ab_re, ab_im = mag * jnp.cos(dt_im), mag * jnp.sin(dt_im)
    z_re, z_im = ab_re - 1.0, ab_im
    den = a_re * a_re + a_im * a_im
    f_re = (z_re * a_re + z_im * a_im) / den
    f_im = (z_im * a_re - z_re * a_im) / den
    bb_re = f_re[:, None, :] * bt_re - f_im[:, None, :] * bt_im
    bb_im = f_re[:, None, :] * bt_im + f_im[:, None, :] * bt_re
    return ab_re, ab_im, bb_re, bb_im


def _s5_params(a_re, a_im, ls, bt_re, bt_im):
    def body(ar, ai, l, br, bi, o_ar, o_ai, o_br, o_bi):
        o_ar[...], o_ai[...], o_br[...], o_bi[...] = _s5_param_fn(ar[...], ai[...], l[...], br[...], bi[...])

    return pl.pallas_call(
        body, name="s5_params",
        out_shape=[jax.ShapeDtypeStruct(a_re.shape, f32)] * 2 + [jax.ShapeDtypeStruct(bt_re.shape, f32)] * 2,
        compiler_params=_cp(),
    )(a_re, a_im, ls, bt_re, bt_im)


def _s5_params_bwd(a_re, a_im, ls, bt_re, bt_im, d_ar, d_ai, d_br, d_bi):
    def body(ar, ai, l, br, bi, c_ar, c_ai, c_br, c_bi, g_ar, g_ai, g_l, g_br, g_bi):
        _, vjp = jax.vjp(_s5_param_fn, ar[...], ai[...], l[...], br[...], bi[...])
        g_ar[...], g_ai[...], g_l[...], g_br[...], g_bi[...] = vjp((c_ar[...], c_ai[...], c_br[...], c_bi[...]))

    return pl.pallas_call(
        body, name="s5_params_bwd",
        out_shape=[jax.ShapeDtypeStruct(a_re.shape, f32)] * 2 + [jax.ShapeDtypeStruct(ls.shape, f32)]
        + [jax.ShapeDtypeStruct(bt_re.shape, f32)] * 2,
        compiler_params=_cp(),
    )(a_re, a_im, ls, bt_re, bt_im, d_ar, d_ai, d_br, d_bi)


S5_CHUNK = 256


def _s5_power_tables(ab_ref, p_ref, w_ref, conj):
    T, L = SUBLANES, S5_LANES
    are = ab_ref[0:1, 0:L]
    aim = ab_ref[0:1, L:2 * L]
    if conj:
        aim = -aim
    pre, pim = are, aim
    for n in range(3):
        p_ref[n:n + 1, 0:L] = pre
        p_ref[n:n + 1, L:2 * L] = pim
        pre, pim = pre * pre - pim * pim, 2.0 * pre * pim
    row = _tile_rows((T, L))
    wre = jnp.zeros((T, L), f32)
    wim = jnp.zeros((T, L), f32)
    pre, pim = are, aim
    for n in range(T):
        tgt = (T - 1 - n) if conj else n
        wre = jnp.where(row == tgt, pre, wre)
        wim = jnp.where(row == tgt, pim, wim)
        pre, pim = pre * are - pim * aim, pre * aim + pim * are
    w_ref[:, 0:L] = wre
    w_ref[:, L:2 * L] = wim


def _s5_scan(bu, ab):
    S, L2 = bu.shape
    L = L2 // 2
    rt = _pick(S, 256, SUBLANES)
    T = SUBLANES
    CH = S5_CHUNK

    def body(bu_ref, ab_ref, x_ref, p_ref, w_ref, carry):
        @pl.when(pl.program_id(0) == 0)
        def _():
            carry[...] = jnp.zeros_like(carry)
            _s5_power_tables(ab_ref, p_ref, w_ref, conj=False)

        row = _tile_rows((T, CH))

        def step(t, _):
            r0 = pl.multiple_of(t * T, T)
            for c in range(L // CH):
                lre, lim = pl.ds(c * CH, CH), pl.ds(L + c * CH, CH)
                xr, xi = bu_ref[pl.ds(r0, T), lre], bu_ref[pl.ds(r0, T), lim]
                for n, k in enumerate((1, 2, 4)):
                    pr, pi = p_ref[n:n + 1, lre], p_ref[n:n + 1, lim]
                    sr = jnp.where(row >= k, pltpu.roll(xr, k, 0), 0.0)
                    si = jnp.where(row >= k, pltpu.roll(xi, k, 0), 0.0)
                    xr, xi = xr + pr * sr - pi * si, xi + pr * si + pi * sr
                cr, ci = carry[T - 1:T, lre], carry[T - 1:T, lim]
                wr, wi = w_ref[:, lre], w_ref[:, lim]
                xr, xi = xr + wr * cr - wi * ci, xi + wr * ci + wi * cr
                carry[:, lre] = xr
                carry[:, lim] = xi
                x_ref[pl.ds(r0, T), lre] = xr
                x_ref[pl.ds(r0, T), lim] = xi
            return 0

        lax.fori_loop(0, rt // T, step, 0)

    row_spec = pl.BlockSpec((rt, L2), lambda i: (i, 0))
    return pl.pallas_call(
        body, name="s5_scan", grid=(S // rt,), in_specs=[row_spec, pl.BlockSpec((1, L2), lambda i: (0, 0))],
        out_specs=row_spec, out_shape=jax.ShapeDtypeStruct((S, L2), f32),
        scratch_shapes=[pltpu.VMEM((T, L2), f32), pltpu.VMEM((T, L2), f32), pltpu.VMEM((T, L2), f32)],
        compiler_params=_cp("arbitrary"),
    )(bu, ab)


def _s5_scan_bwd(dx, x, ab):
    S, L2 = dx.shape
    L = L2 // 2
    rt = _pick(S, 256, SUBLANES)
    T = SUBLANES
    CH = S5_CHUNK
    nblk = S // rt
    per = rt // T

    def body(dx_ref, x_ref, xp_ref, ab_ref, g_ref, da_ref, p_ref, w_ref, carry, acc):
        pid = pl.program_id(0)

        @pl.when(pid == 0)
        def _():
            carry[...] = jnp.zeros_like(carry)
            acc[...] = jnp.zeros_like(acc)
            _s5_power_tables(ab_ref, p_ref, w_ref, conj=True)

        row = _tile_rows((T, CH))
        first_block = pid == nblk - 1

        def step(j, _):
            t = per - 1 - j
            r0 = pl.multiple_of(t * T, T)
            rp = pl.multiple_of(jnp.maximum(t - 1, 0) * T, T)
            for c in range(L // CH):
                lre, lim = pl.ds(c * CH, CH), pl.ds(L + c * CH, CH)
                gr, gi = dx_ref[pl.ds(r0, T), lre], dx_ref[pl.ds(r0, T), lim]
                for n, k in enumerate((1, 2, 4)):
                    pr, pi = p_ref[n:n + 1, lre], p_ref[n:n + 1, lim]
                    sr = jnp.where(row < T - k, pltpu.roll(gr, T - k, 0), 0.0)
                    si = jnp.where(row < T - k, pltpu.roll(gi, T - k, 0), 0.0)
                    gr, gi = gr + pr * sr - pi * si, gi + pr * si + pi * sr
                cr, ci = carry[0:1, lre], carry[0:1, lim]
                wr, wi = w_ref[:, lre], w_ref[:, lim]
                gr, gi = gr + wr * cr - wi * ci, gi + wr * ci + wi * cr
                carry[:, lre] = gr
                carry[:, lim] = gi
                g_ref[pl.ds(r0, T), lre] = gr
                g_ref[pl.ds(r0, T), lim] = gi
                xr, xi = x_ref[pl.ds(r0, T), lre], x_ref[pl.ds(r0, T), lim]
                in_blk_r, in_blk_i = x_ref[pl.ds(rp, T), lre], x_ref[pl.ds(rp, T), lim]
                hal_r = jnp.where(first_block, 0.0, xp_ref[:, lre])
                hal_i = jnp.where(first_block, 0.0, xp_ref[:, lim])
                pvr = jnp.where(t == 0, hal_r, in_blk_r)[T - 1:T, :]
                pvi = jnp.where(t == 0, hal_i, in_blk_i)[T - 1:T, :]
                sxr = jnp.where(row >= 1, pltpu.roll(xr, 1, 0), pvr)
                sxi = jnp.where(row >= 1, pltpu.roll(xi, 1, 0), pvi)
                acc[:, lre] += gr * sxr + gi * sxi
                acc[:, lim] += gi * sxr - gr * sxi
            return 0

        lax.fori_loop(0, per, step, 0)

        @pl.when(pid == nblk - 1)
        def _():
            da_ref[...] = jnp.sum(acc[...], axis=0, keepdims=True)

    row_spec = pl.BlockSpec((rt, L2), lambda i: (nblk - 1 - i, 0))
    halo = pl.BlockSpec((T, L2), lambda i: (jnp.maximum((nblk - 1 - i) * per - 1, 0), 0))
    vec = pl.BlockSpec((1, L2), lambda i: (0, 0))
    return pl.pallas_call(
        body, name="s5_scan_bwd", grid=(nblk,), in_specs=[row_spec, row_spec, halo, vec],
        out_specs=[row_spec, vec],
        out_shape=[jax.ShapeDtypeStruct((S, L2), f32), jax.ShapeDtypeStruct((1, L2), f32)],
        scratch_shapes=[pltpu.VMEM((T, L2), f32)] * 4,
        compiler_params=_cp("arbitrary"),
    )(dx, x, x, ab)


def _s5_out(yc, u, d, wglu, bglu):
    S, W = yc.shape
    rt = _pick(S, ROW_TILE, SUBLANES)

    def body(yc_ref, u_ref, d_ref, w_ref, b_ref, o_ref, y_ref):
        y = yc_ref[...] + d_ref[...] * u_ref[...]
        yg = _gelu(y)
        z = _dot(yg.astype(bf16), w_ref[...], NN) + b_ref[...]
        o_ref[...] = yg * _sigmoid(z)
        y_ref[...] = y

    row = pl.BlockSpec((rt, W), lambda i: (i, 0))
    vec = pl.BlockSpec((1, W), lambda i: (0, 0))
    mat = pl.BlockSpec((W, W), lambda i: (0, 0))
    return pl.pallas_call(
        body, name="s5_out", grid=(S // rt,), in_specs=[row, row, vec, mat, vec], out_specs=[row, row],
        out_shape=[jax.ShapeDtypeStruct((S, W), f32)] * 2, compiler_params=_cp("parallel"),
    )(yc, u, d.reshape(1, W), wglu, bglu.reshape(1, W))


def _s5_out_bwd(dssm, y, u, d, wglu, bglu):
    S, W = y.shape
    rt = _pick(S, ROW_TILE, SUBLANES)

    def body(do_ref, y_ref, u_ref, d_ref, w_ref, b_ref, dy_ref, du_ref, dw_ref, acc_ref):
        @pl.when(pl.program_id(0) == 0)
        def _():
            dw_ref[...] = jnp.zeros_like(dw_ref)
            acc_ref[...] = jnp.zeros_like(acc_ref)

        do, y = do_ref[...], y_ref[...]
        yg = _gelu(y)
        ygb = yg.astype(bf16)
        sg = _sigmoid(_dot(ygb, w_ref[...], NN) + b_ref[...])
        dz = do * yg * sg * (1.0 - sg)
        dzb = dz.astype(bf16)
        dyg = do * sg + _dot(dzb, w_ref[...], NT)
        dy = dyg * _gelu_grad(y)
        dy_ref[...] = dy
        du_ref[...] = dy * d_ref[...]
        dw_ref[...] += _dot(ygb, dzb, TN)
        acc_ref[0:1, :] += jnp.sum(dz, axis=0, keepdims=True)
        acc_ref[1:2, :] += jnp.sum(dy * u_ref[...], axis=0, keepdims=True)

    row = pl.BlockSpec((rt, W), lambda i: (i, 0))
    vec = pl.BlockSpec((1, W), lambda i: (0, 0))
    mat = pl.BlockSpec((W, W), lambda i: (0, 0))
    acc = pl.BlockSpec((SUBLANES, W), lambda i: (0, 0))
    return pl.pallas_call(
        body, name="s5_out_bwd", grid=(S // rt,), in_specs=[row, row, row, vec, mat, vec],
        out_specs=[row, row, mat, acc],
        out_shape=[jax.ShapeDtypeStruct((S, W), f32)] * 2
        + [jax.ShapeDtypeStruct((W, W), f32), jax.ShapeDtypeStruct((SUBLANES, W), f32)],
        compiler_params=_cp("arbitrary"),
    )(dssm, y, u, d.reshape(1, W), wglu, bglu.reshape(1, W))


MIX_SPLITS = ((0, ATTN_WIDTH), (ATTN_WIDTH, ATTN_WIDTH + LRU_WIDTH), (ATTN_WIDTH + LRU_WIDTH, D_MODEL))


def _mixnorm(attn, lru, ssm, g):
    S = attn.shape[0]
    rt = _pick(S, ROW_TILE, SUBLANES)

    def body(a_ref, l_ref, s_ref, g_ref, o_ref):
        for ref, (lo, hi) in zip((a_ref, l_ref, s_ref), MIX_SPLITS):
            x = ref[...]
            ms = jnp.mean(x * x, axis=-1, keepdims=True)
            o_ref[:, lo:hi] = (x * lax.rsqrt(ms + RMS_EPS) * g_ref[:, lo:hi]).astype(o_ref.dtype)

    rows = [pl.BlockSpec((rt, hi - lo), lambda i: (i, 0)) for lo, hi in MIX_SPLITS]
    return pl.pallas_call(
        body, name="mixnorm", grid=(S // rt,), in_specs=rows + [pl.BlockSpec((1, D_MODEL), lambda i: (0, 0))],
        out_specs=pl.BlockSpec((rt, D_MODEL), lambda i: (i, 0)),
        out_shape=jax.ShapeDtypeStruct((S, D_MODEL), bf16), compiler_params=_cp("parallel"),
    )(attn, lru, ssm, g.reshape(1, D_MODEL))


def _mixnorm_bwd(dmixed, attn, lru, ssm, g):
    S = attn.shape[0]
    rt = _pick(S, ROW_TILE, SUBLANES)

    def body(d_ref, a_ref, l_ref, s_ref, g_ref, da_ref, dl_ref, ds_ref, dlt_ref, acc_ref):
        @pl.when(pl.program_id(0) == 0)
        def _():
            acc_ref[...] = jnp.zeros_like(acc_ref)

        outs = []
        for ref, (lo, hi) in zip((a_ref, l_ref, s_ref), MIX_SPLITS):
            x = ref[...]
            dy = d_ref[:, lo:hi]
            rinv = lax.rsqrt(jnp.mean(x * x, axis=-1, keepdims=True) + RMS_EPS)
            dyg = dy * g_ref[:, lo:hi]
            outs.append(rinv * dyg - x * (rinv * rinv * rinv) * jnp.mean(dyg * x, axis=-1, keepdims=True))
            acc_ref[0:1, lo:hi] += jnp.sum(dy * x * rinv, axis=0, keepdims=True)
        da_ref[...], dl_ref[...], ds_ref[...] = outs
        hi_ = lax.broadcasted_iota(jnp.int32, (ATTN_WIDTH, ATTN_WIDTH), 0) // HEAD_DIM
        hj_ = lax.broadcasted_iota(jnp.int32, (ATTN_WIDTH, ATTN_WIDTH), 1) // HEAD_DIM
        same = jnp.where(hi_ == hj_, 1.0, 0.0).astype(f32)
        dlt_ref[...] = jnp.dot(outs[0] * a_ref[...], same, precision=lax.Precision.HIGHEST, preferred_element_type=f32)

    rows = [pl.BlockSpec((rt, hi - lo), lambda i: (i, 0)) for lo, hi in MIX_SPLITS]
    full = pl.BlockSpec((rt, D_MODEL), lambda i: (i, 0))
    return pl.pallas_call(
        body, name="mixnorm_bwd", grid=(S // rt,),
        in_specs=[full] + rows + [pl.BlockSpec((1, D_MODEL), lambda i: (0, 0))],
        out_specs=rows + [rows[0], pl.BlockSpec((SUBLANES, D_MODEL), lambda i: (0, 0))],
        out_shape=[jax.ShapeDtypeStruct((S, hi - lo), f32) for lo, hi in MIX_SPLITS]
        + [jax.ShapeDtypeStruct((S, ATTN_WIDTH), f32), jax.ShapeDtypeStruct((SUBLANES, D_MODEL), f32)],
        compiler_params=_cp("arbitrary"),
    )(dmixed, attn, lru, ssm, g.reshape(1, D_MODEL))


FFN_COL_TILE = 512


def _ffn_conv(x, prev, w_ref, b_ref, K):
    y = b_ref[...] + w_ref[K - 1:K, :] * x
    for k in range(K - 1):
        y = y + w_ref[k:k + 1, :] * _shift_down(x, prev, K - 1 - k)
    return y


def _ffn_act(up, conv_w, conv_b):
    S, C2 = up.shape
    C = C2 // 2
    K = conv_w.shape[0]
    ct = FFN_COL_TILE
    nct = C // ct
    rt = _pick(S, ROW_TILE, SUBLANES)

    def body(g_ref, gp_ref, v_ref, vp_ref, wg_ref, wv_ref, bg_ref, bv_ref, o_ref):
        first = pl.program_id(1) == 0
        gate = _ffn_conv(g_ref[...], jnp.where(first, 0.0, gp_ref[...]), wg_ref, bg_ref, K)
        val = _ffn_conv(v_ref[...], jnp.where(first, 0.0, vp_ref[...]), wv_ref, bv_ref, K)
        o_ref[...] = (_gelu(gate) * val).astype(o_ref.dtype)

    def specs(off):
        return (pl.BlockSpec((rt, ct), lambda j, i: (i, j + off)), _prev_halo_spec(rt, ct, lambda j, i: j + off))

    def wspec(off, rows):
        return pl.BlockSpec((rows, ct), lambda j, i: (0, j + off))

    g_s, gp_s = specs(0)
    v_s, vp_s = specs(nct)
    return pl.pallas_call(
        body, name="ffn_act", grid=(nct, S // rt),
        in_specs=[g_s, gp_s, v_s, vp_s, wspec(0, K), wspec(nct, K), wspec(0, 1), wspec(nct, 1)],
        out_specs=pl.BlockSpec((rt, ct), lambda j, i: (i, j)),
        out_shape=jax.ShapeDtypeStruct((S, C), bf16), compiler_params=_cp("parallel", "parallel"),
    )(up, up, up, up, conv_w, conv_w, conv_b.reshape(1, C2), conv_b.reshape(1, C2))


def _ffn_act_bwd(dact, up, conv_w, conv_b):
    S, C2 = up.shape
    C = C2 // 2
    K = conv_w.shape[0]
    ct = FFN_COL_TILE
    nct = C // ct
    rt = _pick(S, ROW_TILE, SUBLANES)

    def body(d_ref, g_ref, gp_ref, v_ref, vp_ref, wg_ref, wv_ref, bg_ref, bv_ref, o_ref):
        first = pl.program_id(2) == 0
        gate = _ffn_conv(g_ref[...], jnp.where(first, 0.0, gp_ref[...]), wg_ref, bg_ref, K)
        val = _ffn_conv(v_ref[...], jnp.where(first, 0.0, vp_ref[...]), wv_ref, bv_ref, K)
        d = d_ref[...]
        o_ref[...] = jnp.where(pl.program_id(0) == 0, d * val * _gelu_grad(gate), d * _gelu(gate))

    def specs(off):
        return (pl.BlockSpec((rt, ct), lambda s, j, i: (i, j + off)),
                _prev_halo_spec(rt, ct, lambda s, j, i: j + off))

    def wspec(off, rows):
        return pl.BlockSpec((rows, ct), lambda s, j, i: (0, j + off))

    g_s, gp_s = specs(0)
    v_s, vp_s = specs(nct)
    return pl.pallas_call(
        body, name="ffn_act_bwd", grid=(2, nct, S // rt),
        in_specs=[pl.BlockSpec((rt, ct), lambda s, j, i: (i, j)), g_s, gp_s, v_s, vp_s,
                  wspec(0, K), wspec(nct, K), wspec(0, 1), wspec(nct, 1)],
        out_specs=pl.BlockSpec((rt, ct), lambda s, j, i: (i, s * nct + j)),
        out_shape=jax.ShapeDtypeStruct((S, C2), f32), compiler_params=_cp("parallel", "parallel", "parallel"),
    )(dact, up, up, up, up, conv_w, conv_w, conv_b.reshape(1, C2), conv_b.reshape(1, C2))


ANY = pl.BlockSpec(memory_space=pl.ANY)


def _xy_exchange(src, name, gather):
    shp = src.shape[-2:]

    def body(src_ref, out_ref, send_sems, recv_sems, local_sem):
        x, y, c = lax.axis_index("x"), lax.axis_index("y"), lax.axis_index("c")
        me = 2 * x + y
        peers = [(1 - x, y), (x, 1 - y), (1 - x, 1 - y)]

        def slab(q):
            return src_ref if gather else src_ref.at[q]

        own = pltpu.make_async_copy(slab(me), out_ref.at[me], local_sem)
        own.start()
        sends = []
        for k, (px, py) in enumerate(peers):
            cp = pltpu.make_async_remote_copy(
                src_ref=slab(2 * px + py), dst_ref=out_ref.at[me], send_sem=send_sems.at[k],
                recv_sem=recv_sems.at[k], device_id=(px, py, c), device_id_type=MESH)
            cp.start()
            sends.append(cp)
        for k, (px, py) in enumerate(peers):
            q = 2 * px + py
            pltpu.make_async_remote_copy(
                src_ref=slab(q), dst_ref=out_ref.at[q], send_sem=send_sems.at[k],
                recv_sem=recv_sems.at[k], device_id=(px, py, c), device_id_type=MESH).wait_recv()
        for cp in sends:
            cp.wait_send()
        own.wait()

    return pl.pallas_call(
        body, name=name, in_specs=[ANY], out_specs=ANY,
        out_shape=jax.ShapeDtypeStruct((4,) + shp, src.dtype),
        scratch_shapes=[pltpu.SemaphoreType.DMA((3,)), pltpu.SemaphoreType.DMA((3,)), pltpu.SemaphoreType.DMA],
        compiler_params=pltpu.CompilerParams(has_side_effects=True),
    )(src)


def _c_gather(src, name):
    def body(src_ref, out_ref, send_sem, recv_sem, local_sem):
        x, y, c = lax.axis_index("x"), lax.axis_index("y"), lax.axis_index("c")
        own = pltpu.make_async_copy(src_ref, out_ref.at[c], local_sem)
        own.start()
        cp = pltpu.make_async_remote_copy(
            src_ref=src_ref, dst_ref=out_ref.at[c], send_sem=send_sem, recv_sem=recv_sem,
            device_id=(x, y, 1 - c), device_id_type=MESH)
        cp.start()
        pltpu.make_async_remote_copy(
            src_ref=src_ref, dst_ref=out_ref.at[1 - c], send_sem=send_sem, recv_sem=recv_sem,
            device_id=(x, y, 1 - c), device_id_type=MESH).wait_recv()
        cp.wait_send()
        own.wait()

    return pl.pallas_call(
        body, name=name, in_specs=[ANY], out_specs=ANY,
        out_shape=jax.ShapeDtypeStruct((2,) + src.shape, src.dtype),
        scratch_shapes=[pltpu.SemaphoreType.DMA, pltpu.SemaphoreType.DMA, pltpu.SemaphoreType.DMA],
        compiler_params=pltpu.CompilerParams(has_side_effects=True),
    )(src)


def _all_gather8(src, name):
    flips = [(dx, dy, dc) for dx in (0, 1) for dy in (0, 1) for dc in (0, 1)][1:]

    def body(src_ref, out_ref, send_sems, recv_sems, local_sem):
        x, y, c = lax.axis_index("x"), lax.axis_index("y"), lax.axis_index("c")
        me = 4 * x + 2 * y + c
        own = pltpu.make_async_copy(src_ref, out_ref.at[me], local_sem)
        own.start()
        peers = [(x ^ dx, y ^ dy, c ^ dc) for dx, dy, dc in flips]
        sends = []
        for k, peer in enumerate(peers):
            cp = pltpu.make_async_remote_copy(
                src_ref=src_ref, dst_ref=out_ref.at[me], send_sem=send_sems.at[k], recv_sem=recv_sems.at[k],
                device_id=peer, device_id_type=MESH)
            cp.start()
            sends.append(cp)
        for k, (px, py, pc) in enumerate(peers):
            pltpu.make_async_remote_copy(
                src_ref=src_ref, dst_ref=out_ref.at[4 * px + 2 * py + pc], send_sem=send_sems.at[k],
                recv_sem=recv_sems.at[k], device_id=(px, py, pc), device_id_type=MESH).wait_recv()
        for cp in sends:
            cp.wait_send()
        own.wait()

    return pl.pallas_call(
        body, name=name, in_specs=[ANY], out_specs=ANY,
        out_shape=jax.ShapeDtypeStruct((8,) + src.shape, src.dtype),
        scratch_shapes=[pltpu.SemaphoreType.DMA((7,)), pltpu.SemaphoreType.DMA((7,)), pltpu.SemaphoreType.DMA],
        compiler_params=pltpu.CompilerParams(has_side_effects=True),
    )(src)


FLAT_TILE = 2048


def _sum_slabs(g, name):
    n, R, _ = g.shape
    rt = _pick(R, FLAT_TILE, SUBLANES)

    def body(g_ref, o_ref):
        acc = g_ref[0]
        for k in range(1, n):
            acc = acc + g_ref[k]
        o_ref[...] = acc

    return pl.pallas_call(
        body, name=name, grid=(R // rt,), in_specs=[pl.BlockSpec((n, rt, LANES), lambda i: (0, i, 0))],
        out_specs=pl.BlockSpec((rt, LANES), lambda i: (i, 0)),
        out_shape=jax.ShapeDtypeStruct((R, LANES), f32), compiler_params=_cp("parallel"),
    )(g)


def _adamw(g, w, m, v, name):
    R = g.shape[0]
    rt = _pick(R, FLAT_TILE, SUBLANES)

    def body(g_ref, w_ref, m_ref, v_ref, d_ref, mo_ref, vo_ref):
        gg = g_ref[...]
        m_new = ADAM_B1 * m_ref[...] + (1.0 - ADAM_B1) * gg
        v_new = ADAM_B2 * v_ref[...] + (1.0 - ADAM_B2) * (gg * gg)
        m_hat = m_new / (1.0 - ADAM_B1 ** ADAM_STEP)
        v_hat = v_new / (1.0 - ADAM_B2 ** ADAM_STEP)
        d_ref[...] = -ADAM_LR * (m_hat / (jnp.sqrt(v_hat) + ADAM_EPS) + ADAM_WD * w_ref[...])
        mo_ref[...] = m_new
        vo_ref[...] = v_new

    row = pl.BlockSpec((rt, LANES), lambda i: (i, 0))
    return pl.pallas_call(
        body, name=name, grid=(R // rt,), in_specs=[row] * 4, out_specs=[row] * 3,
        out_shape=[jax.ShapeDtypeStruct((R, LANES), f32)] * 3, compiler_params=_cp("parallel"),
    )(g, w, m, v)


def _pack(arrs, dtype):
    flat = jnp.concatenate([a.astype(dtype).reshape(-1) for a in arrs])
    per = FLAT_TILE * LANES
    flat = jnp.pad(flat, (0, (-flat.shape[0]) % per))
    return flat.reshape(-1, LANES)


def _unpack(buf, shapes):
    flat = buf.reshape(-1)
    out, off = [], 0
    for s in shapes:
        n = math.prod(s)
        out.append(flat[off:off + n].reshape(s))
        off += n
    return out


def _block_diag(w):
    n, a, b = w.shape
    eye = jnp.eye(n, dtype=w.dtype)
    return (w[:, :, None, :] * eye[:, None, :, None]).reshape(n * a, n * b)


def _diag_blocks(m, n):
    a, b = m.shape[0] // n, m.shape[1] // n
    idx = jnp.arange(n)
    return m.reshape(n, a, n, b)[idx, :, idx, :]


BIG = ("w_in", "w_out", "w_up", "w_down", "s5_w_glu")
BIG_COL_SHARDED = {"w_in": True, "w_out": False, "w_up": True, "w_down": False, "s5_w_glu": False}
CONV_SHARDED = ("lru_conv_w", "ffn_conv_w")
SMALL = ("lru_conv_b", "lru_wr", "lru_br", "lru_wi", "lru_bi", "lru_lambda", "s5_a_re", "s5_a_im", "s5_b_re",
         "s5_b_im", "s5_c_re", "s5_c_im", "s5_d", "s5_log_step", "s5_b_glu", "mix_norm_g", "ln1_g", "ln1_b",
         "ffn_conv_b", "ln2_g", "ln2_b")
WEIGHTS = ("w_in", "lru_conv_w", "lru_conv_b", "lru_wr", "lru_br", "lru_wi", "lru_bi", "lru_lambda", "s5_a_re",
           "s5_a_im", "s5_b_re", "s5_b_im", "s5_c_re", "s5_c_im", "s5_d", "s5_log_step", "s5_w_glu", "s5_b_glu",
           "mix_norm_g", "w_out", "ln1_g", "ln1_b", "w_up", "ffn_conv_w", "ffn_conv_b", "w_down", "ln2_g", "ln2_b")


def _assemble(slabs, col_sharded):
    _, L, r, c = slabs.shape
    if col_sharded:
        return slabs.transpose(1, 2, 0, 3).reshape(L, r, 4 * c)
    return slabs.transpose(1, 0, 2, 3).reshape(L, 4 * r, c)


def _split(full, col_sharded):
    L, R, C = full.shape
    if col_sharded:
        return full.reshape(L, R, 4, C // 4).transpose(2, 0, 1, 3)
    return full.reshape(L, 4, R // 4, C).transpose(1, 0, 2, 3)


def _s5_prepare(p):
    G = N_S5_GROUPS
    bt_re, bt_im = p["s5_b_re"].transpose(0, 2, 1), p["s5_b_im"].transpose(0, 2, 1)
    ls = p["s5_log_step"].reshape(G, 1)
    ab_re, ab_im, bb_re, bb_im = _s5_params(p["s5_a_re"], p["s5_a_im"], ls, bt_re, bt_im)
    ab = jnp.concatenate([ab_re.reshape(1, S5_LANES), ab_im.reshape(1, S5_LANES)], axis=1)
    bbcat = jnp.concatenate([_block_diag(bb_re), _block_diag(bb_im)], axis=1).astype(bf16)
    ccat = jnp.concatenate([_block_diag(p["s5_c_re"].transpose(0, 2, 1)),
                            -_block_diag(p["s5_c_im"].transpose(0, 2, 1))], axis=0).astype(bf16)
    return dict(bt_re=bt_re, bt_im=bt_im, ls=ls, ab=ab, bbcat=bbcat, ccat=ccat)


def _layer_fwd(h, p, cos, sin):
    sv = {"h": h}
    proj = _mm(h, p["w_in"], "nn", "mm_proj")
    q, k, v = proj[:, 0:384], proj[:, 384:768], proj[:, 768:1152]
    xr, gate, u5 = proj[:, 1152:1536], proj[:, 1536:1920], proj[:, 1920:2176]
    sv.update(xr=xr, gate=gate, u5=u5)
    qr, kr = _rope_fwd(q, k, cos, sin)
    qp, kp, vp = _to_branches(qr), _to_branches(kr), _to_branches(v)
    op, lp = _attn_fwd(qp, kp, vp)
    attn, ltot = _attn_combine(_from_branches(op), _from_branches(lp))
    sv.update(qp=qp, kp=kp, vp=vp, attn=attn, ltot=ltot)
    wr, wi = _block_diag(p["lru_wr"]).astype(bf16), _block_diag(p["lru_wi"]).astype(bf16)
    xc, r, i, log_a, u = _lru_pre(xr, p["lru_conv_w"], p["lru_conv_b"], wr, p["lru_br"], wi, p["lru_bi"],
                                  p["lru_lambda"])
    hl, lru = _lru_scan(log_a, u, gate)
    sv.update(wr=wr, wi=wi, xc=xc, r=r, i=i, log_a=log_a, hl=hl, lru=lru)
    s5 = _s5_prepare(p)
    bu = _mm(u5, s5["bbcat"], "nn", "mm_s5_bu")
    xs = _s5_scan(bu, s5["ab"])
    yc = _mm(xs, s5["ccat"], "nn", "mm_s5_y")
    ssm, y = _s5_out(yc, u5, p["s5_d"].reshape(-1), p["s5_w_glu"], p["s5_b_glu"])
    sv.update(s5=s5, xs=xs, y=y, ssm=ssm)
    mixed = _mixnorm(attn, lru, ssm, p["mix_norm_g"])
    mix = _mm(mixed, p["w_out"], "nn", "mm_out")
    h1, z1 = _ln_fwd(h, mix, p["ln1_g"], p["ln1_b"], "ln_fwd")
    sv.update(mixed=mixed, z1=z1, h1=h1)
    up = _mm(h1, p["w_up"], "nn", "mm_up")
    act = _ffn_act(up, p["ffn_conv_w"], p["ffn_conv_b"])
    ffn = _mm(act, p["w_down"], "nn", "mm_down")
    h2, z2 = _ln_fwd(h1, ffn, p["ln2_g"], p["ln2_b"], "ln_fwd")
    sv.update(up=up, act=act, z2=z2)
    return h2, sv


def _layer_bwd(dy_a, dy_b, p, sv, cos, sin):
    gr = {}
    dz2, acc = _ln_bwd(dy_a, dy_b, sv["z2"], p["ln2_g"], "ln_bwd_top" if dy_a is None else "ln_bwd")
    gr["ln2_g"], gr["ln2_b"] = acc[0], acc[1]
    dact = _mm(dz2, p["w_down"], "nt", "mm_dact")
    gr["w_down"] = _mm(sv["act"], dz2, "tn", "mm_dw_down")
    dupc = _ffn_act_bwd(dact, sv["up"], p["ffn_conv_w"], p["ffn_conv_b"])
    dup, acc = _conv_bwd(dupc, sv["up"], p["ffn_conv_w"], "ffn_conv_bwd", col_tile=FFN_COL_TILE, out_dtype=bf16)
    gr["ffn_conv_w"], gr["ffn_conv_b"] = acc[0:3], acc[3]
    dh1 = _mm(dup, p["w_up"], "nt", "mm_dh1")
    gr["w_up"] = _mm(sv["h1"], dup, "tn", "mm_dw_up")
    dz1, acc = _ln_bwd(dz2, dh1, sv["z1"], p["ln1_g"], "ln_bwd")
    gr["ln1_g"], gr["ln1_b"] = acc[0], acc[1]
    dmixed = _mm(dz1, p["w_out"], "nt", "mm_dmixed")
    gr["w_out"] = _mm(sv["mixed"], dz1, "tn", "mm_dw_out")
    dattn, dlru, dssm, delta, acc = _mixnorm_bwd(dmixed, sv["attn"], sv["lru"], sv["ssm"], p["mix_norm_g"])
    gr["mix_norm_g"] = acc[0]
    dqp, dkp, dvp = _attn_bwd(sv["qp"], sv["kp"], sv["vp"], _to_branches(dattn), _to_branches(sv["ltot"]),
                              _to_branches(delta))
    dq, dk, dv = _rope_bwd(_from_branches(dqp), _from_branches(dkp), _from_branches(dvp), cos, sin)
    g, dgate = _lru_scan_bwd(dlru, sv["gate"], sv["hl"], sv["log_a"])
    dxc, dwr, dwi, acc = _lru_gate_bwd(g, sv["hl"], sv["xc"], sv["r"], sv["i"], sv["log_a"], sv["wr"], sv["wi"],
                                       p["lru_lambda"])
    gr["lru_wr"], gr["lru_wi"] = _diag_blocks(dwr, N_LRU_HEADS), _diag_blocks(dwi, N_LRU_HEADS)
    gr["lru_br"], gr["lru_bi"], gr["lru_lambda"] = acc[0], acc[1], acc[2]
    dxr, acc = _conv_bwd(dxc, sv["xr"], p["lru_conv_w"], "lru_conv_bwd")
    gr["lru_conv_w"], gr["lru_conv_b"] = acc[0:4], acc[4]
    s5 = sv["s5"]
    G = N_S5_GROUPS
    dy, du_direct, dwglu, acc = _s5_out_bwd(dssm, sv["y"], sv["u5"], p["s5_d"].reshape(-1), p["s5_w_glu"],
                                            p["s5_b_glu"])
    gr["s5_w_glu"], gr["s5_b_glu"], gr["s5_d"] = dwglu, acc[0], acc[1].reshape(G, S5_GROUP)
    dxs = _mm(dy, s5["ccat"], "nt", "mm_s5_dx")
    dccat = _mm(sv["xs"], dy, "tn", "mm_s5_dc")
    gr["s5_c_re"] = _diag_blocks(dccat[:S5_LANES], G).transpose(0, 2, 1)
    gr["s5_c_im"] = -_diag_blocks(dccat[S5_LANES:], G).transpose(0, 2, 1)
    gs, dab = _s5_scan_bwd(dxs, sv["xs"], s5["ab"])
    du = _mm(gs, s5["bbcat"], "nt", "mm_s5_du", add=du_direct)
    dbbcat = _mm(sv["u5"], gs, "tn", "mm_s5_dbb")
    d_ar, d_ai, d_ls, d_btr, d_bti = _s5_params_bwd(
        p["s5_a_re"], p["s5_a_im"], s5["ls"], s5["bt_re"], s5["bt_im"],
        dab[:, :S5_LANES].reshape(G, S5_STATE), dab[:, S5_LANES:].reshape(G, S5_STATE),
        _diag_blocks(dbbcat[:, :S5_LANES], G), _diag_blocks(dbbcat[:, S5_LANES:], G))
    gr["s5_a_re"], gr["s5_a_im"], gr["s5_log_step"] = d_ar, d_ai, d_ls.reshape(G)
    gr["s5_b_re"], gr["s5_b_im"] = d_btr.transpose(0, 2, 1), d_bti.transpose(0, 2, 1)
    dproj = jnp.concatenate([dq, dk, dv, dxr, dgate, du], axis=1).astype(bf16)
    gr["w_in"] = _mm(sv["h"], dproj, "tn", "mm_dw_in")
    dh = _mm(dproj, p["w_in"], "nt", "mm_dh")
    return dz1, dh, gr


def _train_step(d):
    x, target = d["x"][0], d["loss_target"][0]
    S = x.shape[0]
    me = 2 * lax.axis_index("x") + lax.axis_index("y")

    big_shapes = [d[n].shape for n in BIG]
    conv_shapes = [d[n].shape for n in CONV_SHARDED]
    wg = _xy_exchange(_pack([d[n] for n in BIG], bf16), "gather_weights", gather=True)
    cg = _xy_exchange(_pack([d[n] for n in CONV_SHARDED], f32), "gather_conv", gather=True)
    full = {}
    for names, buf, shapes, col in ((BIG, wg, big_shapes, BIG_COL_SHARDED), (CONV_SHARDED, cg, conv_shapes, None)):
        per_chip = [_unpack(buf[q], shapes) for q in range(4)]
        for i, n in enumerate(names):
            full[n] = _assemble(jnp.stack([per_chip[q][i] for q in range(4)]), True if col is None else col[n])
    params = [{n: (full[n][l] if n in full else d[n][l]) for n in WEIGHTS} for l in range(DEPTH)]

    cos, sin = _rope_tables(S)
    h, saved = x, []
    for l in range(DEPTH):
        h, sv = _layer_fwd(h, params[l], cos, sin)
        saved.append(sv)
    dy, loss_acc = _loss_head(h, target)
    loss = lax.psum(loss_acc[0, 0], ("x", "y", "c"))
    da, db, grads = None, dy, [None] * DEPTH
    for l in reversed(range(DEPTH)):
        da, db, grads[l] = _layer_bwd(da, db, params[l], saved[l], cos, sin)
    out = {"loss": loss, "grad_x": _axpy(da, db, "grad_x")[None]}
    G = {n: jnp.stack([grads[l][n] for l in range(DEPTH)]) for n in WEIGHTS}

    slabs = [_split(G[n], BIG_COL_SHARDED[n]) for n in BIG]
    send = jnp.stack([_pack([s[q] for s in slabs], f32) for q in range(4)])
    recv = _xy_exchange(send, "reduce_scatter_grads", gather=False)
    both = _c_gather(_sum_slabs(recv, "sum_chips"), "pair_exchange")
    gbig = _sum_slabs(both, "sum_cores")
    upd = _adamw(gbig, _pack([d[n] for n in BIG], f32), _pack([d["m_" + n] for n in BIG], f32),
                 _pack([d["v_" + n] for n in BIG], f32), "adamw_big")
    for pre, buf in zip(("grad_", "delta_", "new_m_", "new_v_"), (gbig,) + tuple(upd)):
        for n, a in zip(BIG, _unpack(buf, big_shapes)):
            out[pre + n] = a

    small = SMALL + CONV_SHARDED
    sg = _all_gather8(_pack([G[n] for n in small], f32), "gather_small_grads")
    gs = dict(zip(small, _unpack(_sum_slabs(sg, "sum_devices"), [G[n].shape for n in small])))
    for n in CONV_SHARDED:
        L, K, C = gs[n].shape
        gs[n] = lax.dynamic_index_in_dim(gs[n].reshape(L, K, 4, C // 4), me, axis=2, keepdims=False)
    small_shapes = [d[n].shape for n in small]
    gsmall = _pack([gs[n] for n in small], f32)
    upd = _adamw(gsmall, _pack([d[n] for n in small], f32), _pack([d["m_" + n] for n in small], f32),
                 _pack([d["v_" + n] for n in small], f32), "adamw_small")
    for pre, buf in zip(("grad_", "delta_", "new_m_", "new_v_"), (gsmall,) + tuple(upd)):
        for n, a in zip(small, _unpack(buf, small_shapes)):
            out[pre + n] = a

    return (out["loss"], out["grad_x"]) + tuple(out[pre + n] for pre in ("grad_", "delta_", "new_m_", "new_v_")
                                                for n in WEIGHTS)


def kernel(
        x, w_in, lru_conv_w, lru_conv_b, lru_wr, lru_br, lru_wi, lru_bi, lru_lambda, s5_a_re, s5_a_im, s5_b_re,
        s5_b_im, s5_c_re, s5_c_im, s5_d, s5_log_step, s5_w_glu, s5_b_glu, mix_norm_g, w_out, ln1_g, ln1_b, w_up,
        ffn_conv_w, ffn_conv_b, w_down, ln2_g, ln2_b, loss_target, m_w_in, m_lru_conv_w, m_lru_conv_b, m_lru_wr,
        m_lru_br, m_lru_wi, m_lru_bi, m_lru_lambda, m_s5_a_re, m_s5_a_im, m_s5_b_re, m_s5_b_im, m_s5_c_re,
        m_s5_c_im, m_s5_d, m_s5_log_step, m_s5_w_glu, m_s5_b_glu, m_mix_norm_g, m_w_out, m_ln1_g, m_ln1_b,
        m_w_up, m_ffn_conv_w, m_ffn_conv_b, m_w_down, m_ln2_g, m_ln2_b, v_w_in, v_lru_conv_w, v_lru_conv_b,
        v_lru_wr, v_lru_br, v_lru_wi, v_lru_bi, v_lru_lambda, v_s5_a_re, v_s5_a_im, v_s5_b_re, v_s5_b_im,
        v_s5_c_re, v_s5_c_im, v_s5_d, v_s5_log_step, v_s5_w_glu, v_s5_b_glu, v_mix_norm_g, v_w_out, v_ln1_g,
        v_ln1_b, v_w_up, v_ffn_conv_w, v_ffn_conv_b, v_w_down, v_ln2_g, v_ln2_b
):
    return _train_step(dict(locals()))
```

```python
import functools
import math

import jax
import jax.numpy as jnp
from jax import lax
from jax.experimental import pallas as pl
from jax.experimental.pallas import tpu as pltpu

f32 = jnp.float32
bf16 = jnp.bfloat16
MESH = pl.DeviceIdType.MESH

D_MODEL = 1024
ATTN_WIDTH = 384
LRU_WIDTH = 384
S5_WIDTH = 256
HEAD_DIM = 64
N_LRU_HEADS = 6
N_S5_GROUPS = 16
S5_GROUP = 16
S5_STATE = 64
S5_LANES = N_S5_GROUPS * S5_STATE
D_FF = 3072
D_IN = 2176
LRU_C = 8.0
ROPE_THETA = 10000.0
DILATIONS = (1, 4, 16)
ATTN_BLOCK = 128
DEPTH = 2
ALPHA = (2 * DEPTH) ** 0.25
LN_EPS = 1e-5
RMS_EPS = 1e-6
ADAM_LR, ADAM_B1, ADAM_B2, ADAM_EPS, ADAM_WD, ADAM_STEP = 0.001, 0.9, 0.999, 1e-08, 0.01, 10

SUBLANES = 8
LANES = 128
VMEM_LIMIT = 56 * 1024 * 1024
ROW_TILE = 512
MM_SINGLE_K = 3072
D_IN_PAD = 2304
NEG = -1e30


def _cp(*sem):
    return pltpu.CompilerParams(dimension_semantics=sem if sem else None, vmem_limit_bytes=VMEM_LIMIT)


def _pick(dim, pref, align=LANES):
    if dim <= pref:
        return dim
    t = (pref // align) * align
    while t >= align:
        if dim % t == 0:
            return t
        t -= align
    return dim


def _gelu(x):
    return jax.nn.gelu(x)


def _gelu_grad(x):
    c = math.sqrt(2.0 / math.pi)
    t = jnp.tanh(c * (x + 0.044715 * x * x * x))
    return 0.5 * (1.0 + t) + 0.5 * x * (1.0 - t * t) * c * (1.0 + 3 * 0.044715 * x * x)


def _sigmoid(x):
    return jax.nn.sigmoid(x)


def _expm1(x):
    p = 1.0 + x / 9.0
    for n in (8.0, 7.0, 6.0, 5.0, 4.0, 3.0, 2.0):
        p = 1.0 + (x / n) * p
    return jnp.where(jnp.abs(x) < 0.3, x * p, jnp.exp(x) - 1.0)


def _dot(a, b, dims):
    return lax.dot_general(a, b, (dims, ((), ())), preferred_element_type=f32)


NN = ((1,), (0,))
NT = ((1,), (1,))
TN = ((0,), (0,))


def _mm(a, b, mode, name, out_dtype=f32, tm=1024, tn=1024, tk=1024, add=None, a_win=None):
    if mode == "nn":
        (M, K), N = a.shape, b.shape[1]
    elif mode == "nt":
        (M, K), N = a.shape, b.shape[0]
    else:
        (K, M), N = a.shape, b.shape[1]
    win = 0
    if a_win is not None:
        win, w = a_win
        if mode == "tn":
            M, tm = w, w
        else:
            K = w
    single = mode != "tn" and K <= MM_SINGLE_K
    tm, tn = _pick(M, tm), _pick(N, tn)
    tk = K if single else _pick(K, tk)
    nk = K // tk
    dims = {"nn": NN, "nt": NT, "tn": TN}[mode]

    def body(a_ref, b_ref, *rest):
        prod = _dot(a_ref[...].astype(bf16), b_ref[...].astype(bf16), dims)
        if single:
            o_ref = rest[-1]
            o_ref[...] = (prod if add is None else prod + rest[0][...]).astype(o_ref.dtype)
            return
        o_ref, acc = rest[-2:]
        k = pl.program_id(2)

        @pl.when(k == 0)
        def _():
            acc[...] = prod if add is None else prod + rest[0][...]

        @pl.when(k > 0)
        def _():
            acc[...] += prod

        @pl.when(k == nk - 1)
        def _():
            o_ref[...] = acc[...].astype(o_ref.dtype)

    if mode == "tn":
        a_spec = pl.BlockSpec((tk, tm), lambda i, j, k: (k, i + win))
    else:
        a_spec = pl.BlockSpec((tm, tk), lambda i, j, k: (i, k + win))
    if mode == "nt":
        b_spec = pl.BlockSpec((tn, tk), lambda i, j, k: (j, k))
    else:
        b_spec = pl.BlockSpec((tk, tn), lambda i, j, k: (k, j))
    o_spec = pl.BlockSpec((tm, tn), lambda i, j, k: (i, j))
    return pl.pallas_call(
        body, name=name, grid=(M // tm, N // tn, nk),
        in_specs=[a_spec, b_spec] + ([] if add is None else [o_spec]), out_specs=o_spec,
        out_shape=jax.ShapeDtypeStruct((M, N), out_dtype),
        scratch_shapes=[] if single else [pltpu.VMEM((tm, tn), f32)],
        compiler_params=_cp("parallel", "parallel", "arbitrary"),
    )(*((a, b) if add is None else (a, b, add)))


def _shift_down(cur, prev8, k):
    if k == 0:
        return cur
    ext = jnp.concatenate([prev8, cur], axis=0)
    return pltpu.roll(ext, k, 0)[SUBLANES:]


def _shift_up(cur, next8, k):
    if k == 0:
        return cur
    n = cur.shape[0] + SUBLANES
    ext = jnp.concatenate([cur, next8], axis=0)
    return pltpu.roll(ext, n - k, 0)[: cur.shape[0]]


def _prev_halo_spec(rt, cols, ncolblk_fn):
    per = rt // SUBLANES
    return pl.BlockSpec((SUBLANES, cols), lambda *g: (jnp.maximum(g[-1] * per - 1, 0), ncolblk_fn(*g)))


def _next_halo_spec(rt, cols, nrows, ncolblk_fn):
    per = rt // SUBLANES
    last = nrows // SUBLANES - 1
    return pl.BlockSpec((SUBLANES, cols), lambda *g: (jnp.minimum((g[-1] + 1) * per, last), ncolblk_fn(*g)))


def _ln_fwd(h, branch, g, b, name):
    S, D = h.shape
    rt = _pick(S, ROW_TILE, SUBLANES)

    def body(h_ref, m_ref, g_ref, b_ref, o_ref, z_ref):
        z = ALPHA * h_ref[...] + m_ref[...]
        mu = jnp.mean(z, axis=-1, keepdims=True)
        zc = z - mu
        var = jnp.mean(zc * zc, axis=-1, keepdims=True)
        o_ref[...] = zc * lax.rsqrt(var + LN_EPS) * g_ref[...] + b_ref[...]
        z_ref[...] = z

    row = pl.BlockSpec((rt, D), lambda i: (i, 0))
    vec = pl.BlockSpec((1, D), lambda i: (0, 0))
    return pl.pallas_call(
        body, name=name, grid=(S // rt,), in_specs=[row, row, vec, vec], out_specs=[row, row],
        out_shape=[jax.ShapeDtypeStruct((S, D), f32)] * 2, compiler_params=_cp("parallel"),
    )(h, branch, g.reshape(1, D), b.reshape(1, D))


def _ln_bwd(dy_a, dy_b, z, g, name):
    S, D = z.shape
    rt = _pick(S, ROW_TILE, SUBLANES)
    two = dy_a is not None

    def body(*refs):
        if two:
            a_ref, b_ref, z_ref, g_ref, dz_ref, acc_ref = refs
            dy = ALPHA * a_ref[...] + b_ref[...]
        else:
            b_ref, z_ref, g_ref, dz_ref, acc_ref = refs
            dy = b_ref[...]
        z = z_ref[...]
        mu = jnp.mean(z, axis=-1, keepdims=True)
        zc = z - mu
        var = jnp.mean(zc * zc, axis=-1, keepdims=True)
        rstd = lax.rsqrt(var + LN_EPS)
        xhat = zc * rstd
        dxh = dy * g_ref[...]
        m1 = jnp.mean(dxh, axis=-1, keepdims=True)
        m2 = jnp.mean(dxh * xhat, axis=-1, keepdims=True)
        dz_ref[...] = rstd * (dxh - m1 - xhat * m2)

        @pl.when(pl.program_id(0) == 0)
        def _():
            acc_ref[...] = jnp.zeros_like(acc_ref)

        acc_ref[0:1, :] += jnp.sum(dy * xhat, axis=0, keepdims=True)
        acc_ref[1:2, :] += jnp.sum(dy, axis=0, keepdims=True)

    row = pl.BlockSpec((rt, D), lambda i: (i, 0))
    vec = pl.BlockSpec((1, D), lambda i: (0, 0))
    acc = pl.BlockSpec((SUBLANES, D), lambda i: (0, 0))
    ins = ([dy_a] if two else []) + [dy_b, z, g.reshape(1, D)]
    return pl.pallas_call(
        body, name=name, grid=(S // rt,), in_specs=[row] * (len(ins) - 1) + [vec], out_specs=[row, acc],
        out_shape=[jax.ShapeDtypeStruct((S, D), f32), jax.ShapeDtypeStruct((SUBLANES, D), f32)],
        compiler_params=_cp("arbitrary"),
    )(*ins)


def _loss_head(y, target):
    S, D = y.shape
    rt = _pick(S, ROW_TILE, SUBLANES)

    def body(y_ref, t_ref, dy_ref, acc_ref):
        e = y_ref[...] - t_ref[...]
        dy_ref[...] = e * (1.0 / D)

        @pl.when(pl.program_id(0) == 0)
        def _():
            acc_ref[...] = jnp.zeros_like(acc_ref)

        part = jnp.sum(jnp.mean(e * e, axis=-1, keepdims=True), axis=0, keepdims=True)
        acc_ref[...] += 0.5 * part

    row = pl.BlockSpec((rt, D), lambda i: (i, 0))
    return pl.pallas_call(
        body, name="loss_head", grid=(S // rt,), in_specs=[row, row],
        out_specs=[row, pl.BlockSpec((1, 1), lambda i: (0, 0))],
        out_shape=[jax.ShapeDtypeStruct((S, D), f32), jax.ShapeDtypeStruct((1, 1), f32)],
        compiler_params=_cp("arbitrary"),
    )(y, target)


def _axpy(a, b, name):
    S, D = a.shape
    rt = _pick(S, ROW_TILE, SUBLANES)

    def body(a_ref, b_ref, o_ref):
        o_ref[...] = ALPHA * a_ref[...] + b_ref[...]

    row = pl.BlockSpec((rt, D), lambda i: (i, 0))
    return pl.pallas_call(
        body, name=name, grid=(S // rt,), in_specs=[row, row], out_specs=row,
        out_shape=jax.ShapeDtypeStruct((S, D), f32), compiler_params=_cp("parallel"),
    )(a, b)


def _rope_tables(S):
    rt = _pick(S, ROW_TILE, SUBLANES)

    def body(c_ref, s_ref):
        pos = (pl.program_id(0) * rt + lax.broadcasted_iota(jnp.int32, (rt, LANES), 0)).astype(f32)
        lane = lax.broadcasted_iota(jnp.int32, (rt, LANES), 1)
        j = (lane % (HEAD_DIM // 2)).astype(f32)
        inv = jnp.exp((-j * 2.0 / HEAD_DIM) * math.log(ROPE_THETA))
        ang = pos * inv
        c = jnp.cos(ang)
        s = jnp.where(lane % HEAD_DIM < HEAD_DIM // 2, -jnp.sin(ang), jnp.sin(ang))
        c_ref[...] = jnp.concatenate([c, c, c], axis=1)
        s_ref[...] = jnp.concatenate([s, s, s], axis=1)

    row = pl.BlockSpec((rt, ATTN_WIDTH), lambda i: (i, 0))
    return pl.pallas_call(
        body, name="rope_tables", grid=(S // rt,), in_specs=[], out_specs=[row, row],
        out_shape=[jax.ShapeDtypeStruct((S, ATTN_WIDTH), f32)] * 2, compiler_params=_cp("parallel"),
    )()


def _swap_halves(x):
    lane = lax.broadcasted_iota(jnp.int32, x.shape, 1)
    half = HEAD_DIM // 2
    return jnp.where(lane % HEAD_DIM < half, pltpu.roll(x, x.shape[1] - half, 1), pltpu.roll(x, half, 1))


def _rope_fwd(proj, cos, sin):
    S, W = proj.shape[0], ATTN_WIDTH
    rt = _pick(S, ROW_TILE, SUBLANES)

    def body(q_ref, k_ref, c_ref, s_ref, qo_ref, ko_ref):
        c, s = c_ref[...], s_ref[...]
        qo_ref[...] = q_ref[...] * c + _swap_halves(q_ref[...]) * s
        ko_ref[...] = k_ref[...] * c + _swap_halves(k_ref[...]) * s

    row = pl.BlockSpec((rt, W), lambda i: (i, 0))
    return pl.pallas_call(
        body, name="rope_fwd", grid=(S // rt,), in_specs=[row, pl.BlockSpec((rt, W), lambda i: (i, 1)), row, row],
        out_specs=[row, row], out_shape=[jax.ShapeDtypeStruct((S, W), f32)] * 2, compiler_params=_cp("parallel"),
    )(proj, proj, cos, sin)


def _rope_bwd(dq, dk, cos, sin):
    S, W = dq.shape
    rt = _pick(S, ROW_TILE, SUBLANES)

    def body(q_ref, k_ref, c_ref, s_ref, qo_ref, ko_ref):
        c, s = c_ref[...], s_ref[...]
        qo_ref[...] = q_ref[...] * c + _swap_halves(q_ref[...] * s)
        ko_ref[...] = k_ref[...] * c + _swap_halves(k_ref[...] * s)

    row = pl.BlockSpec((rt, W), lambda i: (i, 0))
    return pl.pallas_call(
        body, name="rope_bwd", grid=(S // rt,), in_specs=[row] * 4, out_specs=[row] * 2,
        out_shape=[jax.ShapeDtypeStruct((S, W), f32)] * 2, compiler_params=_cp("parallel"),
    )(dq, dk, cos, sin)


def _rows(ref, start, d):
    if d == 1:
        return ref[pl.ds(pl.multiple_of(start, ATTN_BLOCK), ATTN_BLOCK), :]
    return ref[pl.ds(start, ATTN_BLOCK, stride=d), :]


def _set_rows(ref, start, d, val):
    if d == 1:
        ref[pl.ds(pl.multiple_of(start, ATTN_BLOCK), ATTN_BLOCK), :] = val
    else:
        ref[pl.ds(start, ATTN_BLOCK, stride=d), :] = val


def _pair_spec(S, first_block):
    return pl.BlockSpec((S, LANES), lambda p: (0, p + first_block))


def _attn_fwd2(qr, kr, proj):
    S = qr.shape[0]
    B = ATTN_BLOCK
    nb = S // B
    scale = HEAD_DIM ** -0.5

    def body(q_ref, k_ref, v_ref, o_ref, l_ref, m_s, l_s):
        qi = lax.broadcasted_iota(jnp.int32, (B, 2 * B), 0)
        ki = lax.broadcasted_iota(jnp.int32, (B, 2 * B), 1)
        dist = qi + B - ki
        band = (dist >= 0) & (dist <= B)
        for bi, d in enumerate(DILATIONS):
            bpc = nb // d

            def blk(b, carry, bi=bi, d=d, bpc=bpc):
                c, n = b // bpc, b % bpc
                start = c + d * B * n
                pstart = c + d * B * jnp.maximum(n - 1, 0)
                valid = band & ((ki >= B) | (n > 0))
                q = _rows(q_ref, start, d).astype(bf16)
                kcat = jnp.concatenate([_rows(k_ref, pstart, d), _rows(k_ref, start, d)], axis=0).astype(bf16)
                vcat = jnp.concatenate([_rows(v_ref, pstart, d), _rows(v_ref, start, d)], axis=0).astype(bf16)
                if bi > 0:
                    m_old, l_old, a_old = _rows(m_s, start, d), _rows(l_s, start, d), _rows(o_ref, start, d)
                ms, ls, accs = [], [], []
                for h in range(2):
                    sl = slice(h * HEAD_DIM, (h + 1) * HEAD_DIM)
                    c0 = h * HEAD_DIM
                    s = jnp.where(valid, _dot(q[:, sl], kcat[:, sl], NT) * scale, NEG)
                    m = jnp.max(s, axis=1, keepdims=True)
                    if bi > 0:
                        mo = m_old[:, c0:c0 + 1]
                        m = jnp.maximum(m, mo)
                        alpha = jnp.exp(mo - m)
                    p = jnp.exp(s - m)
                    l = jnp.sum(p, axis=1, keepdims=True)
                    acc = _dot(p.astype(bf16), vcat[:, sl], NN)
                    if bi > 0:
                        l = l + alpha * l_old[:, c0:c0 + 1]
                        acc = acc + alpha * a_old[:, sl]
                    ms.append(jnp.broadcast_to(m, (B, HEAD_DIM)))
                    ls.append(jnp.broadcast_to(l, (B, HEAD_DIM)))
                    accs.append(acc)
                _set_rows(m_s, start, d, jnp.concatenate(ms, axis=1))
                _set_rows(l_s, start, d, jnp.concatenate(ls, axis=1))
                _set_rows(o_ref, start, d, jnp.concatenate(accs, axis=1))
                return carry

            lax.fori_loop(0, nb, blk, 0)

        def fin(t, carry):
            rows = pl.ds(pl.multiple_of(t * B, B), B)
            l = l_s[rows, :]
            o_ref[rows, :] = o_ref[rows, :] / l
            l_ref[rows, :] = m_s[rows, :] + jnp.log(l)
            return carry

        lax.fori_loop(0, nb, fin, 0)

    pair = _pair_spec(S, 0)
    return pl.pallas_call(
        body, name="attn_fwd", grid=(3,), in_specs=[pair, pair, _pair_spec(S, 2 * ATTN_WIDTH // LANES)],
        out_specs=[pair, pair], out_shape=[jax.ShapeDtypeStruct((S, ATTN_WIDTH), f32)] * 2,
        scratch_shapes=[pltpu.VMEM((S, LANES), f32)] * 2, compiler_params=_cp("parallel"),
    )(qr, kr, proj)


def _attn_bwd2(qr, kr, proj, dattn, ltot, delta):
    S = qr.shape[0]
    B = ATTN_BLOCK
    nb = S // B
    scale = HEAD_DIM ** -0.5

    def body(q_ref, k_ref, v_ref, do_ref, l_ref, d_ref, dq_ref, dk_ref, dv_ref):
        qi = lax.broadcasted_iota(jnp.int32, (B, 2 * B), 0)
        ki = lax.broadcasted_iota(jnp.int32, (B, 2 * B), 1)
        dist1 = qi + B - ki
        band1 = (dist1 >= 0) & (dist1 <= B)
        ri = lax.broadcasted_iota(jnp.int32, (2 * B, B), 0)
        ci = lax.broadcasted_iota(jnp.int32, (2 * B, B), 1)
        dist2 = ri - ci
        band2 = (dist2 >= 0) & (dist2 <= B)
        for bi, d in enumerate(DILATIONS):
            bpc = nb // d

            def blk(b, carry, bi=bi, d=d, bpc=bpc):
                c, n = b // bpc, b % bpc
                start = c + d * B * n
                pstart = c + d * B * jnp.maximum(n - 1, 0)
                nstart = c + d * B * jnp.minimum(n + 1, bpc - 1)
                valid1 = band1 & ((ki >= B) | (n > 0))
                valid2 = band2 & ((ri < B) | (n + 1 < bpc))
                q_c, q_n = _rows(q_ref, start, d), _rows(q_ref, nstart, d)
                k_p, k_c = _rows(k_ref, pstart, d), _rows(k_ref, start, d)
                v_p, v_c = _rows(v_ref, pstart, d), _rows(v_ref, start, d)
                do_c, do_n = _rows(do_ref, start, d), _rows(do_ref, nstart, d)
                l_c, l_n = _rows(l_ref, start, d), _rows(l_ref, nstart, d)
                d_c, d_n = _rows(d_ref, start, d), _rows(d_ref, nstart, d)
                qc = q_c.astype(bf16)
                qcat = jnp.concatenate([q_c, q_n], axis=0).astype(bf16)
                kc = k_c.astype(bf16)
                kcat = jnp.concatenate([k_p, k_c], axis=0).astype(bf16)
                vc = v_c.astype(bf16)
                vcat = jnp.concatenate([v_p, v_c], axis=0).astype(bf16)
                doc = do_c.astype(bf16)
                docat = jnp.concatenate([do_c, do_n], axis=0).astype(bf16)
                lcat = jnp.concatenate([l_c, l_n], axis=0)
                dcat = jnp.concatenate([d_c, d_n], axis=0)
                dqs, dks, dvs = [], [], []
                for h in range(2):
                    sl = slice(h * HEAD_DIM, (h + 1) * HEAD_DIM)
                    c0 = h * HEAD_DIM
                    s1 = _dot(qc[:, sl], kcat[:, sl], NT) * scale
                    p1 = jnp.where(valid1, jnp.exp(s1 - l_c[:, c0:c0 + 1]), 0.0)
                    dp1 = _dot(doc[:, sl], vcat[:, sl], NT)
                    ds1 = p1 * (dp1 - d_c[:, c0:c0 + 1]) * scale
                    dqs.append(_dot(ds1.astype(bf16), kcat[:, sl], NN))
                    s2 = _dot(qcat[:, sl], kc[:, sl], NT) * scale
                    p2 = jnp.where(valid2, jnp.exp(s2 - lcat[:, c0:c0 + 1]), 0.0)
                    dvs.append(_dot(p2.astype(bf16), docat[:, sl], TN))
                    dp2 = _dot(docat[:, sl], vc[:, sl], NT)
                    ds2 = p2 * (dp2 - dcat[:, c0:c0 + 1]) * scale
                    dks.append(_dot(ds2.astype(bf16), qcat[:, sl], TN))
                for ref, parts in ((dq_ref, dqs), (dk_ref, dks), (dv_ref, dvs)):
                    new = jnp.concatenate(parts, axis=1)
                    if bi > 0:
                        new = new + _rows(ref, start, d)
                    _set_rows(ref, start, d, new)
                return carry

            lax.fori_loop(0, nb, blk, 0)

    pair = _pair_spec(S, 0)
    return pl.pallas_call(
        body, name="attn_bwd", grid=(3,),
        in_specs=[pair, pair, _pair_spec(S, 2 * ATTN_WIDTH // LANES), pair, pair, pair], out_specs=[pair] * 3,
        out_shape=[jax.ShapeDtypeStruct((S, ATTN_WIDTH), f32)] * 3, compiler_params=_cp("parallel"),
    )(qr, kr, proj, dattn, ltot, delta)


def _softplus_neg(lam):
    return jnp.maximum(-lam, 0.0) + jnp.log1p(jnp.exp(-jnp.abs(lam)))


PROJ_LRU_X, PROJ_LRU_GATE, PROJ_S5_U = 3, 4, 5


def _lru_pre(proj, conv_w, conv_b, wr, br, wi, bi, lam):
    S, W = proj.shape[0], LRU_WIDTH
    rt = _pick(S, ROW_TILE, SUBLANES)
    K = conv_w.shape[0]

    def body(x_ref, xp_ref, cw_ref, cb_ref, wr_ref, br_ref, wi_ref, bi_ref, lam_ref,
             xc_ref, r_ref, i_ref, la_ref, u_ref):
        prev = jnp.where(pl.program_id(0) == 0, 0.0, xp_ref[...])
        x = x_ref[...]
        xc = cb_ref[...] + cw_ref[K - 1:K, :] * x
        for k in range(K - 1):
            xc = xc + cw_ref[k:k + 1, :] * _shift_down(x, prev, K - 1 - k)
        xb = xc.astype(bf16)
        r = _sigmoid(_dot(xb, wr_ref[...], NN) + br_ref[...])
        i = _sigmoid(_dot(xb, wi_ref[...], NN) + bi_ref[...])
        log_a = -LRU_C * r * _softplus_neg(lam_ref[...])
        u = jnp.sqrt(-_expm1(2.0 * log_a)) * (i * xc)
        xc_ref[...], r_ref[...], i_ref[...], la_ref[...], u_ref[...] = xc, r, i, log_a, u

    row = pl.BlockSpec((rt, W), lambda i: (i, 0))
    xrow = pl.BlockSpec((rt, W), lambda i: (i, PROJ_LRU_X))
    halo = _prev_halo_spec(rt, W, lambda i: PROJ_LRU_X)
    vec = pl.BlockSpec((1, W), lambda i: (0, 0))
    return pl.pallas_call(
        body, name="lru_pre", grid=(S // rt,),
        in_specs=[xrow, halo, pl.BlockSpec((K, W), lambda i: (0, 0)), vec,
                  pl.BlockSpec((W, W), lambda i: (0, 0)), vec, pl.BlockSpec((W, W), lambda i: (0, 0)), vec, vec],
        out_specs=[row] * 5, out_shape=[jax.ShapeDtypeStruct((S, W), f32)] * 5, compiler_params=_cp("parallel"),
    )(proj, proj, conv_w, conv_b.reshape(1, W), wr, br.reshape(1, W), wi, bi.reshape(1, W), lam.reshape(1, W))


def _tile_rows(shape):
    return lax.broadcasted_iota(jnp.int32, shape, 0)


def _lru_scan(log_a, u, proj):
    S, W = u.shape
    rt = _pick(S, ROW_TILE, SUBLANES)
    T = SUBLANES

    def body(la_ref, u_ref, g_ref, h_ref, o_ref, carry):
        @pl.when(pl.program_id(0) == 0)
        def _():
            carry[...] = jnp.zeros_like(carry)

        row = _tile_rows((T, W))

        def step(t, hp):
            r0 = pl.multiple_of(t * T, T)
            a = jnp.exp(la_ref[pl.ds(r0, T), :])
            x = u_ref[pl.ds(r0, T), :]
            for k in (1, 2, 4):
                x = x + a * jnp.where(row >= k, pltpu.roll(x, k, 0), 0.0)
                a = a * jnp.where(row >= k, pltpu.roll(a, k, 0), 1.0)
            h = x + a * hp
            h_ref[pl.ds(r0, T), :] = h
            o_ref[pl.ds(r0, T), :] = h * _gelu(g_ref[pl.ds(r0, T), :])
            return h[T - 1:T, :]

        carry[0:1, :] = lax.fori_loop(0, rt // T, step, carry[0:1, :])

    row = pl.BlockSpec((rt, W), lambda i: (i, 0))
    grow = pl.BlockSpec((rt, W), lambda i: (i, PROJ_LRU_GATE))
    return pl.pallas_call(
        body, name="lru_scan", grid=(S // rt,), in_specs=[row, row, grow], out_specs=[row] * 2,
        out_shape=[jax.ShapeDtypeStruct((S, W), f32)] * 2, scratch_shapes=[pltpu.VMEM((T, W), f32)],
        compiler_params=_cp("arbitrary"),
    )(log_a, u, proj)


def _lru_scan_bwd(dlru, proj, h, log_a):
    S, W = h.shape
    rt = _pick(S, ROW_TILE, SUBLANES)
    T = SUBLANES
    nblk = S // rt

    def body(d_ref, g_ref, h_ref, la_ref, go_ref, dg_ref, carry):
        @pl.when(pl.program_id(0) == 0)
        def _():
            carry[...] = jnp.zeros_like(carry)

        row = _tile_rows((T, W))

        def step(j, c):
            gn, an = c
            t = rt // T - 1 - j
            r0 = pl.multiple_of(t * T, T)
            d = d_ref[pl.ds(r0, T), :]
            gate = g_ref[pl.ds(r0, T), :]
            a = jnp.exp(la_ref[pl.ds(r0, T), :])
            dg_ref[pl.ds(r0, T), :] = d * h_ref[pl.ds(r0, T), :] * _gelu_grad(gate)
            x = d * _gelu(gate)
            b = jnp.where(row < T - 1, pltpu.roll(a, T - 1, 0), an)
            for k in (1, 2, 4):
                x = x + b * jnp.where(row < T - k, pltpu.roll(x, T - k, 0), 0.0)
                b = b * jnp.where(row < T - k, pltpu.roll(b, T - k, 0), 1.0)
            g = x + b * gn
            go_ref[pl.ds(r0, T), :] = g
            return g[0:1, :], a[0:1, :]

        gn, an = lax.fori_loop(0, rt // T, step, (carry[0:1, :], carry[1:2, :]))
        carry[0:1, :] = gn
        carry[1:2, :] = an

    row = pl.BlockSpec((rt, W), lambda i: (nblk - 1 - i, 0))
    grow = pl.BlockSpec((rt, W), lambda i: (nblk - 1 - i, PROJ_LRU_GATE))
    return pl.pallas_call(
        body, name="lru_scan_bwd", grid=(nblk,), in_specs=[row, grow, row, row], out_specs=[row] * 2,
        out_shape=[jax.ShapeDtypeStruct((S, W), f32)] * 2, scratch_shapes=[pltpu.VMEM((T, W), f32)],
        compiler_params=_cp("arbitrary"),
    )(dlru, proj, h, log_a)


def _lru_gate_bwd(g, h, xc, r, i, log_a, wr, wi, lam):
    S, W = g.shape
    rt = _pick(S, ROW_TILE, SUBLANES)

    def body(g_ref, h_ref, hp_ref, xc_ref, r_ref, i_ref, la_ref, wr_ref, wi_ref, lam_ref,
             dxc_ref, dwr_ref, dwi_ref, acc_ref):
        @pl.when(pl.program_id(0) == 0)
        def _():
            dwr_ref[...] = jnp.zeros_like(dwr_ref)
            dwi_ref[...] = jnp.zeros_like(dwi_ref)
            acc_ref[...] = jnp.zeros_like(acc_ref)

        prev = jnp.where(pl.program_id(0) == 0, 0.0, hp_ref[...])
        gg, xc, r, i, log_a, lam = g_ref[...], xc_ref[...], r_ref[...], i_ref[...], la_ref[...], lam_ref[...]
        hm1 = _shift_down(h_ref[...], prev, 1)
        a = jnp.exp(log_a)
        s = jnp.sqrt(-_expm1(2.0 * log_a))
        da = gg * hm1
        di = gg * s * xc
        dxc = gg * s * i
        ds = gg * i * xc
        dlog_a = da * a - ds * (a * a / s)
        sp = _softplus_neg(lam)
        dr = dlog_a * (-LRU_C * sp)
        dsp = jnp.sum(dlog_a * (-LRU_C * r), axis=0, keepdims=True)
        dpr = dr * r * (1.0 - r)
        dpi = di * i * (1.0 - i)
        dprb, dpib, xb = dpr.astype(bf16), dpi.astype(bf16), xc.astype(bf16)
        dxc_ref[...] = dxc + _dot(dprb, wr_ref[...], NT) + _dot(dpib, wi_ref[...], NT)
        dwr_ref[...] += _dot(xb, dprb, TN)
        dwi_ref[...] += _dot(xb, dpib, TN)
        acc_ref[0:1, :] += jnp.sum(dpr, axis=0, keepdims=True)
        acc_ref[1:2, :] += jnp.sum(dpi, axis=0, keepdims=True)
        acc_ref[2:3, :] += dsp * (-_sigmoid(-lam))

    row = pl.BlockSpec((rt, W), lambda i: (i, 0))
    halo = _prev_halo_spec(rt, W, lambda i: 0)
    vec = pl.BlockSpec((1, W), lambda i: (0, 0))
    mat = pl.BlockSpec((W, W), lambda i: (0, 0))
    acc = pl.BlockSpec((SUBLANES, W), lambda i: (0, 0))
    return pl.pallas_call(
        body, name="lru_gate_bwd", grid=(S // rt,),
        in_specs=[row, row, halo, row, row, row, row, mat, mat, vec], out_specs=[row, mat, mat, acc],
        out_shape=[jax.ShapeDtypeStruct((S, W), f32), jax.ShapeDtypeStruct((W, W), f32),
                   jax.ShapeDtypeStruct((W, W), f32), jax.ShapeDtypeStruct((SUBLANES, W), f32)],
        compiler_params=_cp("arbitrary"),
    )(g, h, h, xc, r, i, log_a, wr, wi, lam.reshape(1, W))


def _conv_bwd(dy, x, conv_w, name, col_tile=None, out_dtype=f32, x_col_block=0):
    S, C = dy.shape
    K = conv_w.shape[0]
    ct = C if col_tile is None else col_tile
    rt = _pick(S, ROW_TILE, SUBLANES)
    nrt = S // rt

    def body(dy_ref, dyn_ref, x_ref, xp_ref, w_ref, dx_ref, acc_ref):
        i = pl.program_id(1)

        @pl.when(i == 0)
        def _():
            acc_ref[...] = jnp.zeros_like(acc_ref)

        nxt = jnp.where(i == nrt - 1, 0.0, dyn_ref[...])
        prev = jnp.where(i == 0, 0.0, xp_ref[...])
        dy, x = dy_ref[...], x_ref[...]
        dx = w_ref[K - 1:K, :] * dy
        for k in range(K - 1):
            dx = dx + w_ref[k:k + 1, :] * _shift_up(dy, nxt, K - 1 - k)
        dx_ref[...] = dx.astype(dx_ref.dtype)
        for k in range(K):
            acc_ref[k:k + 1, :] += jnp.sum(dy * _shift_down(x, prev, K - 1 - k), axis=0, keepdims=True)
        acc_ref[K:K + 1, :] += jnp.sum(dy, axis=0, keepdims=True)

    row = pl.BlockSpec((rt, ct), lambda j, i: (i, j))
    xrow = pl.BlockSpec((rt, ct), lambda j, i: (i, j + x_col_block))
    return pl.pallas_call(
        body, name=name, grid=(C // ct, nrt),
        in_specs=[row, _next_halo_spec(rt, ct, S, lambda j, i: j), xrow,
                  _prev_halo_spec(rt, ct, lambda j, i: j + x_col_block), pl.BlockSpec((K, ct), lambda j, i: (0, j))],
        out_specs=[row, pl.BlockSpec((SUBLANES, ct), lambda j, i: (0, j))],
        out_shape=[jax.ShapeDtypeStruct((S, C), out_dtype), jax.ShapeDtypeStruct((SUBLANES, C), f32)],
        compiler_params=_cp("parallel", "arbitrary"),
    )(dy, dy, x, x, conv_w)


def _s5_param_fn(a_re, a_im, ls, bt_re, bt_im):
    step = jnp.exp(ls)
    dt_re, dt_im = step * a_re, step * a_im
    mag = jnp.exp(dt_re)
    ab_re, ab_im = mag * jnp.cos(dt_im), mag * jnp.sin(dt_im)
    z_re, z_im = ab_re - 1.0, ab_im
    den = a_re * a_re + a_im * a_im
    f_re = (z_re * a_re + z_im * a_im) / den
    f_im = (z_im * a_re - z_re * a_im) / den
    bb_re = f_re[:, None, :] * bt_re - f_im[:, None, :] * bt_im
    bb_im = f_re[:, None, :] * bt_im + f_im[:, None, :] * bt_re
    return ab_re, ab_im, bb_re, bb_im


def _s5_params(a_re, a_im, ls, bt_re, bt_im):
    def body(ar, ai, l, br, bi, o_ar, o_ai, o_br, o_bi):
        o_ar[...], o_ai[...], o_br[...], o_bi[...] = _s5_param_fn(ar[...], ai[...], l[...], br[...], bi[...])

    return pl.pallas_call(
        body, name="s5_params",
        out_shape=[jax.ShapeDtypeStruct(a_re.shape, f32)] * 2 + [jax.ShapeDtypeStruct(bt_re.shape, f32)] * 2,
        compiler_params=_cp(),
    )(a_re, a_im, ls, bt_re, bt_im)


def _s5_params_bwd(a_re, a_im, ls, bt_re, bt_im, d_ar, d_ai, d_br, d_bi):
    def body(ar, ai, l, br, bi, c_ar, c_ai, c_br, c_bi, g_ar, g_ai, g_l, g_br, g_bi):
        _, vjp = jax.vjp(_s5_param_fn, ar[...], ai[...], l[...], br[...], bi[...])
        g_ar[...], g_ai[...], g_l[...], g_br[...], g_bi[...] = vjp((c_ar[...], c_ai[...], c_br[...], c_bi[...]))

    return pl.pallas_call(
        body, name="s5_params_bwd",
        out_shape=[jax.ShapeDtypeStruct(a_re.shape, f32)] * 2 + [jax.ShapeDtypeStruct(ls.shape, f32)]
        + [jax.ShapeDtypeStruct(bt_re.shape, f32)] * 2,
        compiler_params=_cp(),
    )(a_re, a_im, ls, bt_re, bt_im, d_ar, d_ai, d_br, d_bi)


S5_CHUNK = 256


def _s5_power_tables(ab_ref, p_ref, w_ref, conj):
    T, L = SUBLANES, S5_LANES
    are = ab_ref[0:1, 0:L]
    aim = ab_ref[0:1, L:2 * L]
    if conj:
        aim = -aim
    pre, pim = are, aim
    for n in range(3):
        p_ref[n:n + 1, 0:L] = pre
        p_ref[n:n + 1, L:2 * L] = pim
        pre, pim = pre * pre - pim * pim, 2.0 * pre * pim
    row = _tile_rows((T, L))
    wre = jnp.zeros((T, L), f32)
    wim = jnp.zeros((T, L), f32)
    pre, pim = are, aim
    for n in range(T):
        tgt = (T - 1 - n) if conj else n
        wre = jnp.where(row == tgt, pre, wre)
        wim = jnp.where(row == tgt, pim, wim)
        pre, pim = pre * are - pim * aim, pre * aim + pim * are
    w_ref[:, 0:L] = wre
    w_ref[:, L:2 * L] = wim


def _s5_scan(bu, ab):
    S, L2 = bu.shape
    L = L2 // 2
    rt = _pick(S, 256, SUBLANES)
    T = SUBLANES
    CH = S5_CHUNK

    def body(bu_ref, ab_ref, x_ref, p_ref, w_ref, carry):
        @pl.when(pl.program_id(0) == 0)
        def _():
            carry[...] = jnp.zeros_like(carry)
            _s5_power_tables(ab_ref, p_ref, w_ref, conj=False)

        row = _tile_rows((T, CH))

        def step(t, _):
            r0 = pl.multiple_of(t * T, T)
            for c in range(L // CH):
                lre, lim = pl.ds(c * CH, CH), pl.ds(L + c * CH, CH)
                xr, xi = bu_ref[pl.ds(r0, T), lre], bu_ref[pl.ds(r0, T), lim]
                for n, k in enumerate((1, 2, 4)):
                    pr, pi = p_ref[n:n + 1, lre], p_ref[n:n + 1, lim]
                    sr = jnp.where(row >= k, pltpu.roll(xr, k, 0), 0.0)
                    si = jnp.where(row >= k, pltpu.roll(xi, k, 0), 0.0)
                    xr, xi = xr + pr * sr - pi * si, xi + pr * si + pi * sr
                cr, ci = carry[T - 1:T, lre], carry[T - 1:T, lim]
                wr, wi = w_ref[:, lre], w_ref[:, lim]
                xr, xi = xr + wr * cr - wi * ci, xi + wr * ci + wi * cr
                carry[:, lre] = xr
                carry[:, lim] = xi
                x_ref[pl.ds(r0, T), lre] = xr
                x_ref[pl.ds(r0, T), lim] = xi
            return 0

        lax.fori_loop(0, rt // T, step, 0)

    row_spec = pl.BlockSpec((rt, L2), lambda i: (i, 0))
    return pl.pallas_call(
        body, name="s5_scan", grid=(S // rt,), in_specs=[row_spec, pl.BlockSpec((1, L2), lambda i: (0, 0))],
        out_specs=row_spec, out_shape=jax.ShapeDtypeStruct((S, L2), f32),
        scratch_shapes=[pltpu.VMEM((T, L2), f32), pltpu.VMEM((T, L2), f32), pltpu.VMEM((T, L2), f32)],
        compiler_params=_cp("arbitrary"),
    )(bu, ab)


def _s5_scan_bwd(dx, x, ab):
    S, L2 = dx.shape
    L = L2 // 2
    rt = _pick(S, 256, SUBLANES)
    T = SUBLANES
    CH = S5_CHUNK
    nblk = S // rt
    per = rt // T

    def body(dx_ref, x_ref, xp_ref, ab_ref, g_ref, da_ref, p_ref, w_ref, carry, acc):
        pid = pl.program_id(0)

        @pl.when(pid == 0)
        def _():
            carry[...] = jnp.zeros_like(carry)
            acc[...] = jnp.zeros_like(acc)
            _s5_power_tables(ab_ref, p_ref, w_ref, conj=True)

        row = _tile_rows((T, CH))
        first_block = pid == nblk - 1

        def step(j, _):
            t = per - 1 - j
            r0 = pl.multiple_of(t * T, T)
            rp = pl.multiple_of(jnp.maximum(t - 1, 0) * T, T)
            for c in range(L // CH):
                lre, lim = pl.ds(c * CH, CH), pl.ds(L + c * CH, CH)
                gr, gi = dx_ref[pl.ds(r0, T), lre], dx_ref[pl.ds(r0, T), lim]
                for n, k in enumerate((1, 2, 4)):
                    pr, pi = p_ref[n:n + 1, lre], p_ref[n:n + 1, lim]
                    sr = jnp.where(row < T - k, pltpu.roll(gr, T - k, 0), 0.0)
                    si = jnp.where(row < T - k, pltpu.roll(gi, T - k, 0), 0.0)
                    gr, gi = gr + pr * sr - pi * si, gi + pr * si + pi * sr
                cr, ci = carry[0:1, lre], carry[0:1, lim]
                wr, wi = w_ref[:, lre], w_ref[:, lim]
                gr, gi = gr + wr * cr - wi * ci, gi + wr * ci + wi * cr
                carry[:, lre] = gr
                carry[:, lim] = gi
                g_ref[pl.ds(r0, T), lre] = gr
                g_ref[pl.ds(r0, T), lim] = gi
                xr, xi = x_ref[pl.ds(r0, T), lre], x_ref[pl.ds(r0, T), lim]
                in_blk_r, in_blk_i = x_ref[pl.ds(rp, T), lre], x_ref[pl.ds(rp, T), lim]
                hal_r = jnp.where(first_block, 0.0, xp_ref[:, lre])
                hal_i = jnp.where(first_block, 0.0, xp_ref[:, lim])
                pvr = jnp.where(t == 0, hal_r, in_blk_r)[T - 1:T, :]
                pvi = jnp.where(t == 0, hal_i, in_blk_i)[T - 1:T, :]
                sxr = jnp.where(row >= 1, pltpu.roll(xr, 1, 0), pvr)
                sxi = jnp.where(row >= 1, pltpu.roll(xi, 1, 0), pvi)
                acc[:, lre] += gr * sxr + gi * sxi
                acc[:, lim] += gi * sxr - gr * sxi
            return 0

        lax.fori_loop(0, per, step, 0)

        @pl.when(pid == nblk - 1)
        def _():
            da_ref[...] = jnp.sum(acc[...], axis=0, keepdims=True)

    row_spec = pl.BlockSpec((rt, L2), lambda i: (nblk - 1 - i, 0))
    halo = pl.BlockSpec((T, L2), lambda i: (jnp.maximum((nblk - 1 - i) * per - 1, 0), 0))
    vec = pl.BlockSpec((1, L2), lambda i: (0, 0))
    return pl.pallas_call(
        body, name="s5_scan_bwd", grid=(nblk,), in_specs=[row_spec, row_spec, halo, vec],
        out_specs=[row_spec, vec],
        out_shape=[jax.ShapeDtypeStruct((S, L2), f32), jax.ShapeDtypeStruct((1, L2), f32)],
        scratch_shapes=[pltpu.VMEM((T, L2), f32)] * 4,
        compiler_params=_cp("arbitrary"),
    )(dx, x, x, ab)


def _S5_U_SPEC(rt):
    return pl.BlockSpec((rt, LRU_WIDTH), lambda i: (i, PROJ_S5_U))


def _s5_out(yc, proj, d, wglu, bglu):
    S, W = yc.shape
    rt = _pick(S, ROW_TILE, SUBLANES)

    def body(yc_ref, u_ref, d_ref, w_ref, b_ref, o_ref, y_ref):
        y = yc_ref[...] + d_ref[...] * u_ref[:, 0:W]
        yg = _gelu(y)
        z = _dot(yg.astype(bf16), w_ref[...], NN) + b_ref[...]
        o_ref[...] = yg * _sigmoid(z)
        y_ref[...] = y

    row = pl.BlockSpec((rt, W), lambda i: (i, 0))
    vec = pl.BlockSpec((1, W), lambda i: (0, 0))
    mat = pl.BlockSpec((W, W), lambda i: (0, 0))
    return pl.pallas_call(
        body, name="s5_out", grid=(S // rt,), in_specs=[row, _S5_U_SPEC(rt), vec, mat, vec], out_specs=[row, row],
        out_shape=[jax.ShapeDtypeStruct((S, W), f32)] * 2, compiler_params=_cp("parallel"),
    )(yc, proj, d.reshape(1, W), wglu, bglu.reshape(1, W))


def _s5_out_bwd(dssm, y, proj, d, wglu, bglu):
    S, W = y.shape
    rt = _pick(S, ROW_TILE, SUBLANES)

    def body(do_ref, y_ref, u_ref, d_ref, w_ref, b_ref, dy_ref, du_ref, dw_ref, acc_ref):
        @pl.when(pl.program_id(0) == 0)
        def _():
            dw_ref[...] = jnp.zeros_like(dw_ref)
            acc_ref[...] = jnp.zeros_like(acc_ref)

        do, y = do_ref[...], y_ref[...]
        yg = _gelu(y)
        ygb = yg.astype(bf16)
        sg = _sigmoid(_dot(ygb, w_ref[...], NN) + b_ref[...])
        dz = do * yg * sg * (1.0 - sg)
        dzb = dz.astype(bf16)
        dyg = do * sg + _dot(dzb, w_ref[...], NT)
        dy = dyg * _gelu_grad(y)
        dy_ref[...] = dy
        du_ref[...] = dy * d_ref[...]
        dw_ref[...] += _dot(ygb, dzb, TN)
        acc_ref[0:1, :] += jnp.sum(dz, axis=0, keepdims=True)
        acc_ref[1:2, :] += jnp.sum(dy * u_ref[:, 0:W], axis=0, keepdims=True)

    row = pl.BlockSpec((rt, W), lambda i: (i, 0))
    vec = pl.BlockSpec((1, W), lambda i: (0, 0))
    mat = pl.BlockSpec((W, W), lambda i: (0, 0))
    acc = pl.BlockSpec((SUBLANES, W), lambda i: (0, 0))
    return pl.pallas_call(
        body, name="s5_out_bwd", grid=(S // rt,), in_specs=[row, row, _S5_U_SPEC(rt), vec, mat, vec],
        out_specs=[row, row, mat, acc],
        out_shape=[jax.ShapeDtypeStruct((S, W), f32)] * 2
        + [jax.ShapeDtypeStruct((W, W), f32), jax.ShapeDtypeStruct((SUBLANES, W), f32)],
        compiler_params=_cp("arbitrary"),
    )(dssm, y, proj, d.reshape(1, W), wglu, bglu.reshape(1, W))


MIX_SPLITS = ((0, ATTN_WIDTH), (ATTN_WIDTH, ATTN_WIDTH + LRU_WIDTH), (ATTN_WIDTH + LRU_WIDTH, D_MODEL))


def _mixnorm(attn, lru, ssm, g):
    S = attn.shape[0]
    rt = _pick(S, ROW_TILE, SUBLANES)

    def body(a_ref, l_ref, s_ref, g_ref, o_ref):
        for ref, (lo, hi) in zip((a_ref, l_ref, s_ref), MIX_SPLITS):
            x = ref[...]
            ms = jnp.mean(x * x, axis=-1, keepdims=True)
            o_ref[:, lo:hi] = (x * lax.rsqrt(ms + RMS_EPS) * g_ref[:, lo:hi]).astype(o_ref.dtype)

    rows = [pl.BlockSpec((rt, hi - lo), lambda i: (i, 0)) for lo, hi in MIX_SPLITS]
    return pl.pallas_call(
        body, name="mixnorm", grid=(S // rt,), in_specs=rows + [pl.BlockSpec((1, D_MODEL), lambda i: (0, 0))],
        out_specs=pl.BlockSpec((rt, D_MODEL), lambda i: (i, 0)),
        out_shape=jax.ShapeDtypeStruct((S, D_MODEL), bf16), compiler_params=_cp("parallel"),
    )(attn, lru, ssm, g.reshape(1, D_MODEL))


def _mixnorm_bwd(dmixed, attn, lru, ssm, g):
    S = attn.shape[0]
    rt = _pick(S, ROW_TILE, SUBLANES)

    def body(d_ref, a_ref, l_ref, s_ref, g_ref, da_ref, dl_ref, ds_ref, dlt_ref, acc_ref):
        @pl.when(pl.program_id(0) == 0)
        def _():
            acc_ref[...] = jnp.zeros_like(acc_ref)

        outs = []
        for ref, (lo, hi) in zip((a_ref, l_ref, s_ref), MIX_SPLITS):
            x = ref[...]
            dy = d_ref[:, lo:hi]
            rinv = lax.rsqrt(jnp.mean(x * x, axis=-1, keepdims=True) + RMS_EPS)
            dyg = dy * g_ref[:, lo:hi]
            outs.append(rinv * dyg - x * (rinv * rinv * rinv) * jnp.mean(dyg * x, axis=-1, keepdims=True))
            acc_ref[0:1, lo:hi] += jnp.sum(dy * x * rinv, axis=0, keepdims=True)
        da_ref[...], dl_ref[...], ds_ref[...] = outs
        hi_ = lax.broadcasted_iota(jnp.int32, (ATTN_WIDTH, ATTN_WIDTH), 0) // HEAD_DIM
        hj_ = lax.broadcasted_iota(jnp.int32, (ATTN_WIDTH, ATTN_WIDTH), 1) // HEAD_DIM
        same = jnp.where(hi_ == hj_, 1.0, 0.0).astype(f32)
        dlt_ref[...] = jnp.dot(outs[0] * a_ref[...], same, precision=lax.Precision.HIGHEST, preferred_element_type=f32)

    rows = [pl.BlockSpec((rt, hi - lo), lambda i: (i, 0)) for lo, hi in MIX_SPLITS]
    full = pl.BlockSpec((rt, D_MODEL), lambda i: (i, 0))
    return pl.pallas_call(
        body, name="mixnorm_bwd", grid=(S // rt,),
        in_specs=[full] + rows + [pl.BlockSpec((1, D_MODEL), lambda i: (0, 0))],
        out_specs=rows + [rows[0], pl.BlockSpec((SUBLANES, D_MODEL), lambda i: (0, 0))],
        out_shape=[jax.ShapeDtypeStruct((S, hi - lo), f32) for lo, hi in MIX_SPLITS]
        + [jax.ShapeDtypeStruct((S, ATTN_WIDTH), f32), jax.ShapeDtypeStruct((SUBLANES, D_MODEL), f32)],
        compiler_params=_cp("arbitrary"),
    )(dmixed, attn, lru, ssm, g.reshape(1, D_MODEL))


FFN_COL_TILE = 512


def _ffn_conv(x, prev, w_ref, b_ref, K):
    y = b_ref[...] + w_ref[K - 1:K, :] * x
    for k in range(K - 1):
        y = y + w_ref[k:k + 1, :] * _shift_down(x, prev, K - 1 - k)
    return y


def _ffn_act(up, conv_w, conv_b):
    S, C2 = up.shape
    C = C2 // 2
    K = conv_w.shape[0]
    ct = FFN_COL_TILE
    nct = C // ct
    rt = _pick(S, ROW_TILE, SUBLANES)

    def body(g_ref, gp_ref, v_ref, vp_ref, wg_ref, wv_ref, bg_ref, bv_ref, o_ref):
        first = pl.program_id(1) == 0
        gate = _ffn_conv(g_ref[...], jnp.where(first, 0.0, gp_ref[...]), wg_ref, bg_ref, K)
        val = _ffn_conv(v_ref[...], jnp.where(first, 0.0, vp_ref[...]), wv_ref, bv_ref, K)
        o_ref[...] = (_gelu(gate) * val).astype(o_ref.dtype)

    def specs(off):
        return (pl.BlockSpec((rt, ct), lambda j, i: (i, j + off)), _prev_halo_spec(rt, ct, lambda j, i: j + off))

    def wspec(off, rows):
        return pl.BlockSpec((rows, ct), lambda j, i: (0, j + off))

    g_s, gp_s = specs(0)
    v_s, vp_s = specs(nct)
    return pl.pallas_call(
        body, name="ffn_act", grid=(nct, S // rt),
        in_specs=[g_s, gp_s, v_s, vp_s, wspec(0, K), wspec(nct, K), wspec(0, 1), wspec(nct, 1)],
        out_specs=pl.BlockSpec((rt, ct), lambda j, i: (i, j)),
        out_shape=jax.ShapeDtypeStruct((S, C), bf16), compiler_params=_cp("parallel", "parallel"),
    )(up, up, up, up, conv_w, conv_w, conv_b.reshape(1, C2), conv_b.reshape(1, C2))


def _ffn_act_bwd(dact, up, conv_w, conv_b):
    S, C2 = up.shape
    C = C2 // 2
    K = conv_w.shape[0]
    ct = FFN_COL_TILE
    nct = C // ct
    rt = _pick(S, ROW_TILE, SUBLANES)

    def body(d_ref, g_ref, gp_ref, v_ref, vp_ref, wg_ref, wv_ref, bg_ref, bv_ref, o_ref):
        first = pl.program_id(2) == 0
        gate = _ffn_conv(g_ref[...], jnp.where(first, 0.0, gp_ref[...]), wg_ref, bg_ref, K)
        val = _ffn_conv(v_ref[...], jnp.where(first, 0.0, vp_ref[...]), wv_ref, bv_ref, K)
        d = d_ref[...]
        o_ref[...] = jnp.where(pl.program_id(0) == 0, d * val * _gelu_grad(gate), d * _gelu(gate))

    def specs(off):
        return (pl.BlockSpec((rt, ct), lambda s, j, i: (i, j + off)),
                _prev_halo_spec(rt, ct, lambda s, j, i: j + off))

    def wspec(off, rows):
        return pl.BlockSpec((rows, ct), lambda s, j, i: (0, j + off))

    g_s, gp_s = specs(0)
    v_s, vp_s = specs(nct)
    return pl.pallas_call(
        body, name="ffn_act_bwd", grid=(2, nct, S // rt),
        in_specs=[pl.BlockSpec((rt, ct), lambda s, j, i: (i, j)), g_s, gp_s, v_s, vp_s,
                  wspec(0, K), wspec(nct, K), wspec(0, 1), wspec(nct, 1)],
        out_specs=pl.BlockSpec((rt, ct), lambda s, j, i: (i, s * nct + j)),
        out_shape=jax.ShapeDtypeStruct((S, C2), f32), compiler_params=_cp("parallel", "parallel", "parallel"),
    )(dact, up, up, up, up, conv_w, conv_w, conv_b.reshape(1, C2), conv_b.reshape(1, C2))


ANY = pl.BlockSpec(memory_space=pl.ANY)


def _rows_for(cols):
    return max(16, (1 << 17) // cols)


def _chips(x, y):
    return [(1 - x, y), (x, 1 - y), (1 - x, 1 - y)]


def _gather_weights(shards, split):
    nt = len(shards)

    def body(*refs):
        ins, outs = refs[:nt], refs[nt:2 * nt]
        send1, recv1, send2, recv2, local = refs[2 * nt:]
        x, y, c = lax.axis_index("x"), lax.axis_index("y"), lax.axis_index("c")
        me = 2 * x + y
        sib = (x, y, 1 - c)
        chips = _chips(x, y)

        def part(ref, t, layer):
            if not split[t]:
                return ref
            r = shards[t].shape[0] // 2
            return ref.at[pl.ds(layer * r, r), :]

        owns = [pltpu.make_async_copy(ins[t], outs[t].at[me], local.at[t]) for t in range(nt)]
        for cp in owns:
            cp.start()
        sends = []
        for k, (px, py) in enumerate(chips):
            for t in range(nt):
                cp = pltpu.make_async_remote_copy(
                    src_ref=part(ins[t], t, c), dst_ref=part(outs[t].at[me], t, c), send_sem=send1.at[k, t],
                    recv_sem=recv1.at[k, t], device_id=(px, py, c), device_id_type=MESH)
                cp.start()
                sends.append(cp)
        for k, (px, py) in enumerate(chips):
            q = 2 * px + py
            for t in range(nt):
                landed = part(outs[t].at[q], t, c)
                pltpu.make_async_remote_copy(
                    src_ref=part(ins[t], t, c), dst_ref=landed, send_sem=send1.at[k, t], recv_sem=recv1.at[k, t],
                    device_id=(px, py, c), device_id_type=MESH).wait_recv()
                if split[t]:
                    cp = pltpu.make_async_remote_copy(
                        src_ref=landed, dst_ref=landed, send_sem=send2.at[k, t], recv_sem=recv2.at[k, t],
                        device_id=sib, device_id_type=MESH)
                    cp.start()
                    sends.append(cp)
        for k, (px, py) in enumerate(chips):
            q = 2 * px + py
            for t in range(nt):
                if split[t]:
                    other = part(outs[t].at[q], t, 1 - c)
                    pltpu.make_async_remote_copy(
                        src_ref=other, dst_ref=other, send_sem=send2.at[k, t], recv_sem=recv2.at[k, t],
                        device_id=sib, device_id_type=MESH).wait_recv()
        for cp in sends:
            cp.wait_send()
        for cp in owns:
            cp.wait()

    sems = [pltpu.SemaphoreType.DMA((3, nt))] * 4 + [pltpu.SemaphoreType.DMA((nt,))]
    return pl.pallas_call(
        body, name="gather_weights", in_specs=[ANY] * nt, out_specs=[ANY] * nt,
        out_shape=[jax.ShapeDtypeStruct((4,) + s.shape, s.dtype) for s in shards], scratch_shapes=sems,
    )(*shards)


def _pair_send(grads):
    nt = len(grads)

    def body(*refs):
        ins, outs = refs[:nt], refs[nt:2 * nt]
        send, recv = refs[2 * nt:]
        x, y, c = lax.axis_index("x"), lax.axis_index("y"), lax.axis_index("c")
        cps = [pltpu.make_async_remote_copy(
            src_ref=ins[t].at[1 - c], dst_ref=outs[t], send_sem=send.at[t], recv_sem=recv.at[t],
            device_id=(x, y, 1 - c), device_id_type=MESH) for t in range(nt)]
        for cp in cps:
            cp.start()
        for cp in cps:
            cp.wait()

    return pl.pallas_call(
        body, name="pair_send", in_specs=[ANY] * nt, out_specs=[ANY] * nt,
        out_shape=[jax.ShapeDtypeStruct(g.shape[1:], g.dtype) for g in grads],
        scratch_shapes=[pltpu.SemaphoreType.DMA((nt,))] * 2,
    )(*grads)


def _pair_sum(g, other, name):
    _, R, C = g.shape
    rt = _pick(R, _rows_for(C), 16)

    def body(g_ref, o_ref, out_ref):
        out_ref[...] = (g_ref[lax.axis_index("c")] + o_ref[...]).astype(out_ref.dtype)

    return pl.pallas_call(
        body, name=name, grid=(R // rt,),
        in_specs=[pl.BlockSpec((2, rt, C), lambda i: (0, i, 0)), pl.BlockSpec((rt, C), lambda i: (i, 0))],
        out_specs=pl.BlockSpec((rt, C), lambda i: (i, 0)), out_shape=jax.ShapeDtypeStruct((R, C), bf16),
        compiler_params=_cp("parallel"),
    )(g, other)


def _chip_exchange(slabs):
    nt = len(slabs)

    def body(*refs):
        ins, outs = refs[:nt], refs[nt:2 * nt]
        send, recv, local = refs[2 * nt:]
        x, y, c = lax.axis_index("x"), lax.axis_index("y"), lax.axis_index("c")
        me = 2 * x + y
        chips = _chips(x, y)
        owns = [pltpu.make_async_copy(ins[t].at[me], outs[t].at[me], local.at[t]) for t in range(nt)]
        for cp in owns:
            cp.start()
        sends = []
        for k, (px, py) in enumerate(chips):
            for t in range(nt):
                cp = pltpu.make_async_remote_copy(
                    src_ref=ins[t].at[2 * px + py], dst_ref=outs[t].at[me], send_sem=send.at[k, t],
                    recv_sem=recv.at[k, t], device_id=(px, py, c), device_id_type=MESH)
                cp.start()
                sends.append(cp)
        for k, (px, py) in enumerate(chips):
            q = 2 * px + py
            for t in range(nt):
                pltpu.make_async_remote_copy(
                    src_ref=ins[t].at[q], dst_ref=outs[t].at[q], send_sem=send.at[k, t], recv_sem=recv.at[k, t],
                    device_id=(px, py, c), device_id_type=MESH).wait_recv()
        for cp in sends:
            cp.wait_send()
        for cp in owns:
            cp.wait()

    return pl.pallas_call(
        body, name="chip_exchange", in_specs=[ANY] * nt, out_specs=[ANY] * nt,
        out_shape=[jax.ShapeDtypeStruct(s.shape, s.dtype) for s in slabs],
        scratch_shapes=[pltpu.SemaphoreType.DMA((3, nt))] * 2 + [pltpu.SemaphoreType.DMA((nt,))],
    )(*slabs)


def _sum_chips(slabs, name):
    n, r, C = slabs.shape
    rt = _pick(r, _rows_for(C), 16)

    def body(s_ref, o_ref):
        acc = s_ref[0].astype(f32)
        for k in range(1, n):
            acc = acc + s_ref[k].astype(f32)
        o_ref[...] = acc

    return pl.pallas_call(
        body, name=name, grid=(r // rt,), in_specs=[pl.BlockSpec((n, rt, C), lambda i: (0, i, 0))],
        out_specs=pl.BlockSpec((rt, C), lambda i: (i, 0)), out_shape=jax.ShapeDtypeStruct((r, C), f32),
        compiler_params=_cp("parallel"),
    )(slabs)


def _pair_gather(parts):
    nt = len(parts)

    def body(*refs):
        ins, outs = refs[:nt], refs[nt:2 * nt]
        send, recv, local = refs[2 * nt:]
        x, y, c = lax.axis_index("x"), lax.axis_index("y"), lax.axis_index("c")
        owns = [pltpu.make_async_copy(ins[t], outs[t].at[c], local.at[t]) for t in range(nt)]
        for cp in owns:
            cp.start()
        sends = [pltpu.make_async_remote_copy(
            src_ref=ins[t], dst_ref=outs[t].at[c], send_sem=send.at[t], recv_sem=recv.at[t],
            device_id=(x, y, 1 - c), device_id_type=MESH) for t in range(nt)]
        for cp in sends:
            cp.start()
        for t in range(nt):
            pltpu.make_async_remote_copy(
                src_ref=ins[t], dst_ref=outs[t].at[1 - c], send_sem=send.at[t], recv_sem=recv.at[t],
                device_id=(x, y, 1 - c), device_id_type=MESH).wait_recv()
        for cp in sends:
            cp.wait_send()
        for cp in owns:
            cp.wait()

    return pl.pallas_call(
        body, name="pair_gather", in_specs=[ANY] * nt, out_specs=[ANY] * nt,
        out_shape=[jax.ShapeDtypeStruct((2,) + p.shape, p.dtype) for p in parts],
        scratch_shapes=[pltpu.SemaphoreType.DMA((nt,))] * 3,
    )(*parts)


def _adamw2d(g, w, m, v, name):
    R, C = g.shape
    rt = _pick(R, _rows_for(C), 16)

    def body(g_ref, w_ref, m_ref, v_ref, d_ref, mo_ref, vo_ref):
        _adamw_math(g_ref, w_ref, m_ref, v_ref, d_ref, mo_ref, vo_ref)

    row = pl.BlockSpec((rt, C), lambda i: (i, 0))
    return pl.pallas_call(
        body, name=name, grid=(R // rt,), in_specs=[row] * 4, out_specs=[row] * 3,
        out_shape=[jax.ShapeDtypeStruct((R, C), f32)] * 3, compiler_params=_cp("parallel"),
    )(g, w, m, v)


def _adamw_math(g_ref, w_ref, m_ref, v_ref, d_ref, mo_ref, vo_ref):
    gg = g_ref[...]
    m_new = ADAM_B1 * m_ref[...] + (1.0 - ADAM_B1) * gg
    v_new = ADAM_B2 * v_ref[...] + (1.0 - ADAM_B2) * (gg * gg)
    m_hat = m_new / (1.0 - ADAM_B1 ** ADAM_STEP)
    v_hat = v_new / (1.0 - ADAM_B2 ** ADAM_STEP)
    d_ref[...] = -ADAM_LR * (m_hat / (jnp.sqrt(v_hat) + ADAM_EPS) + ADAM_WD * w_ref[...])
    mo_ref[...] = m_new
    vo_ref[...] = v_new


def _all_gather8(src, name):
    flips = [(dx, dy, dc) for dx in (0, 1) for dy in (0, 1) for dc in (0, 1)][1:]

    def body(src_ref, out_ref, send_sems, recv_sems, local_sem):
        x, y, c = lax.axis_index("x"), lax.axis_index("y"), lax.axis_index("c")
        me = 4 * x + 2 * y + c
        own = pltpu.make_async_copy(src_ref, out_ref.at[me], local_sem)
        own.start()
        peers = [(x ^ dx, y ^ dy, c ^ dc) for dx, dy, dc in flips]
        sends = []
        for k, peer in enumerate(peers):
            cp = pltpu.make_async_remote_copy(
                src_ref=src_ref, dst_ref=out_ref.at[me], send_sem=send_sems.at[k], recv_sem=recv_sems.at[k],
                device_id=peer, device_id_type=MESH)
            cp.start()
            sends.append(cp)
        for k, (px, py, pc) in enumerate(peers):
            pltpu.make_async_remote_copy(
                src_ref=src_ref, dst_ref=out_ref.at[4 * px + 2 * py + pc], send_sem=send_sems.at[k],
                recv_sem=recv_sems.at[k], device_id=(px, py, pc), device_id_type=MESH).wait_recv()
        for cp in sends:
            cp.wait_send()
        own.wait()

    return pl.pallas_call(
        body, name=name, in_specs=[ANY], out_specs=ANY,
        out_shape=jax.ShapeDtypeStruct((8,) + src.shape, src.dtype),
        scratch_shapes=[pltpu.SemaphoreType.DMA((7,)), pltpu.SemaphoreType.DMA((7,)), pltpu.SemaphoreType.DMA],
        compiler_params=pltpu.CompilerParams(has_side_effects=True),
    )(src)


FLAT_TILE = 2048


def _sum_slabs(g, name):
    n, R, _ = g.shape
    rt = _pick(R, FLAT_TILE, SUBLANES)

    def body(g_ref, o_ref):
        acc = g_ref[0]
        for k in range(1, n):
            acc = acc + g_ref[k]
        o_ref[...] = acc

    return pl.pallas_call(
        body, name=name, grid=(R // rt,), in_specs=[pl.BlockSpec((n, rt, LANES), lambda i: (0, i, 0))],
        out_specs=pl.BlockSpec((rt, LANES), lambda i: (i, 0)),
        out_shape=jax.ShapeDtypeStruct((R, LANES), f32), compiler_params=_cp("parallel"),
    )(g)


def _adamw(g, w, m, v, name):
    R = g.shape[0]
    rt = _pick(R, FLAT_TILE, SUBLANES)

    def body(g_ref, w_ref, m_ref, v_ref, d_ref, mo_ref, vo_ref):
        gg = g_ref[...]
        m_new = ADAM_B1 * m_ref[...] + (1.0 - ADAM_B1) * gg
        v_new = ADAM_B2 * v_ref[...] + (1.0 - ADAM_B2) * (gg * gg)
        m_hat = m_new / (1.0 - ADAM_B1 ** ADAM_STEP)
        v_hat = v_new / (1.0 - ADAM_B2 ** ADAM_STEP)
        d_ref[...] = -ADAM_LR * (m_hat / (jnp.sqrt(v_hat) + ADAM_EPS) + ADAM_WD * w_ref[...])
        mo_ref[...] = m_new
        vo_ref[...] = v_new

    row = pl.BlockSpec((rt, LANES), lambda i: (i, 0))
    return pl.pallas_call(
        body, name=name, grid=(R // rt,), in_specs=[row] * 4, out_specs=[row] * 3,
        out_shape=[jax.ShapeDtypeStruct((R, LANES), f32)] * 3, compiler_params=_cp("parallel"),
    )(g, w, m, v)


def _pack(arrs, dtype):
    flat = jnp.concatenate([a.astype(dtype).reshape(-1) for a in arrs])
    per = FLAT_TILE * LANES
    flat = jnp.pad(flat, (0, (-flat.shape[0]) % per))
    return flat.reshape(-1, LANES)


def _unpack(buf, shapes):
    flat = buf.reshape(-1)
    out, off = [], 0
    for s in shapes:
        n = math.prod(s)
        out.append(flat[off:off + n].reshape(s))
        off += n
    return out


def _block_diag(w):
    n, a, b = w.shape
    eye = jnp.eye(n, dtype=w.dtype)
    return (w[:, :, None, :] * eye[:, None, :, None]).reshape(n * a, n * b)


def _diag_blocks(m, n):
    a, b = m.shape[0] // n, m.shape[1] // n
    idx = jnp.arange(n)
    return m.reshape(n, a, n, b)[idx, :, idx, :]


BIG = ("w_in", "w_out", "w_up", "w_down", "s5_w_glu")
BIG_COL_SHARDED = {"w_in": True, "w_out": False, "w_up": True, "w_down": False, "s5_w_glu": False}
CONV_SHARDED = ("lru_conv_w", "ffn_conv_w")
SMALL = ("lru_conv_b", "lru_wr", "lru_br", "lru_wi", "lru_bi", "lru_lambda", "s5_a_re", "s5_a_im", "s5_b_re",
         "s5_b_im", "s5_c_re", "s5_c_im", "s5_d", "s5_log_step", "s5_b_glu", "mix_norm_g", "ln1_g", "ln1_b",
         "ffn_conv_b", "ln2_g", "ln2_b")
WEIGHTS = ("w_in", "lru_conv_w", "lru_conv_b", "lru_wr", "lru_br", "lru_wi", "lru_bi", "lru_lambda", "s5_a_re",
           "s5_a_im", "s5_b_re", "s5_b_im", "s5_c_re", "s5_c_im", "s5_d", "s5_log_step", "s5_w_glu", "s5_b_glu",
           "mix_norm_g", "w_out", "ln1_g", "ln1_b", "w_up", "ffn_conv_w", "ffn_conv_b", "w_down", "ln2_g", "ln2_b")


def _assemble(slabs, col_sharded):
    _, L, r, c = slabs.shape
    if col_sharded:
        return slabs.transpose(1, 2, 0, 3).reshape(L, r, 4 * c)
    return slabs.transpose(1, 0, 2, 3).reshape(L, 4 * r, c)


def _s5_prepare(p):
    G = N_S5_GROUPS
    bt_re, bt_im = p["s5_b_re"].transpose(0, 2, 1), p["s5_b_im"].transpose(0, 2, 1)
    ls = p["s5_log_step"].reshape(G, 1)
    ab_re, ab_im, bb_re, bb_im = _s5_params(p["s5_a_re"], p["s5_a_im"], ls, bt_re, bt_im)
    ab = jnp.concatenate([ab_re.reshape(1, S5_LANES), ab_im.reshape(1, S5_LANES)], axis=1)
    bbcat = jnp.concatenate([_block_diag(bb_re), _block_diag(bb_im)], axis=1).astype(bf16)
    ccat = jnp.concatenate([_block_diag(p["s5_c_re"].transpose(0, 2, 1)),
                            -_block_diag(p["s5_c_im"].transpose(0, 2, 1))], axis=0).astype(bf16)
    bbcat_pad = jnp.concatenate([bbcat, jnp.zeros((LRU_WIDTH - S5_WIDTH, 2 * S5_LANES), bf16)], axis=0)
    return dict(bt_re=bt_re, bt_im=bt_im, ls=ls, ab=ab, bbcat=bbcat, bbcat_pad=bbcat_pad, ccat=ccat)


def _layer_fwd(h, p, cos, sin):
    sv = {"h": h}
    proj = _mm(h, p["w_in"], "nn", "mm_proj", tn=768)
    sv.update(proj=proj)
    qr, kr = _rope_fwd(proj, cos, sin)
    attn, ltot = _attn_fwd2(qr, kr, proj)
    sv.update(qr=qr, kr=kr, attn=attn, ltot=ltot)
    wr, wi = _block_diag(p["lru_wr"]).astype(bf16), _block_diag(p["lru_wi"]).astype(bf16)
    xc, r, i, log_a, u = _lru_pre(proj, p["lru_conv_w"], p["lru_conv_b"], wr, p["lru_br"], wi, p["lru_bi"],
                                  p["lru_lambda"])
    hl, lru = _lru_scan(log_a, u, proj)
    sv.update(wr=wr, wi=wi, xc=xc, r=r, i=i, log_a=log_a, hl=hl, lru=lru)
    s5 = _s5_prepare(p)
    bu = _mm(proj, s5["bbcat_pad"], "nn", "mm_s5_bu", a_win=(PROJ_S5_U, LRU_WIDTH))
    xs = _s5_scan(bu, s5["ab"])
    yc = _mm(xs, s5["ccat"], "nn", "mm_s5_y")
    ssm, y = _s5_out(yc, proj, p["s5_d"].reshape(-1), p["s5_w_glu"], p["s5_b_glu"])
    sv.update(s5=s5, xs=xs, y=y, ssm=ssm)
    mixed = _mixnorm(attn, lru, ssm, p["mix_norm_g"])
    mix = _mm(mixed, p["w_out"], "nn", "mm_out")
    h1, z1 = _ln_fwd(h, mix, p["ln1_g"], p["ln1_b"], "ln_fwd")
    sv.update(mixed=mixed, z1=z1, h1=h1)
    up = _mm(h1, p["w_up"], "nn", "mm_up", tn=1536)
    act = _ffn_act(up, p["ffn_conv_w"], p["ffn_conv_b"])
    ffn = _mm(act, p["w_down"], "nn", "mm_down")
    h2, z2 = _ln_fwd(h1, ffn, p["ln2_g"], p["ln2_b"], "ln_fwd")
    sv.update(up=up, act=act, z2=z2)
    return h2, sv


def _layer_bwd(dy_a, dy_b, p, sv, cos, sin):
    gr = {}
    dz2, acc = _ln_bwd(dy_a, dy_b, sv["z2"], p["ln2_g"], "ln_bwd_top" if dy_a is None else "ln_bwd")
    gr["ln2_g"], gr["ln2_b"] = acc[0], acc[1]
    dact = _mm(dz2, p["w_down"], "nt", "mm_dact")
    gr["w_down"] = _mm(sv["act"], dz2, "tn", "mm_dw_down")
    dupc = _ffn_act_bwd(dact, sv["up"], p["ffn_conv_w"], p["ffn_conv_b"])
    dup, acc = _conv_bwd(dupc, sv["up"], p["ffn_conv_w"], "ffn_conv_bwd", col_tile=FFN_COL_TILE, out_dtype=bf16)
    gr["ffn_conv_w"], gr["ffn_conv_b"] = acc[0:3], acc[3]
    dh1 = _mm(dup, p["w_up"], "nt", "mm_dh1", tk=2048)
    gr["w_up"] = _mm(sv["h1"], dup, "tn", "mm_dw_up", tn=1536)
    dz1, acc = _ln_bwd(dz2, dh1, sv["z1"], p["ln1_g"], "ln_bwd")
    gr["ln1_g"], gr["ln1_b"] = acc[0], acc[1]
    dmixed = _mm(dz1, p["w_out"], "nt", "mm_dmixed")
    gr["w_out"] = _mm(sv["mixed"], dz1, "tn", "mm_dw_out")
    dattn, dlru, dssm, delta, acc = _mixnorm_bwd(dmixed, sv["attn"], sv["lru"], sv["ssm"], p["mix_norm_g"])
    gr["mix_norm_g"] = acc[0]
    proj = sv["proj"]
    dqr, dkr, dv = _attn_bwd2(sv["qr"], sv["kr"], proj, dattn, sv["ltot"], delta)
    dq, dk = _rope_bwd(dqr, dkr, cos, sin)
    g, dgate = _lru_scan_bwd(dlru, proj, sv["hl"], sv["log_a"])
    dxc, dwr, dwi, acc = _lru_gate_bwd(g, sv["hl"], sv["xc"], sv["r"], sv["i"], sv["log_a"], sv["wr"], sv["wi"],
                                       p["lru_lambda"])
    gr["lru_wr"], gr["lru_wi"] = _diag_blocks(dwr, N_LRU_HEADS), _diag_blocks(dwi, N_LRU_HEADS)
    gr["lru_br"], gr["lru_bi"], gr["lru_lambda"] = acc[0], acc[1], acc[2]
    dxr, acc = _conv_bwd(dxc, proj, p["lru_conv_w"], "lru_conv_bwd", x_col_block=PROJ_LRU_X)
    gr["lru_conv_w"], gr["lru_conv_b"] = acc[0:4], acc[4]
    s5 = sv["s5"]
    G = N_S5_GROUPS
    dy, du_direct, dwglu, acc = _s5_out_bwd(dssm, sv["y"], proj, p["s5_d"].reshape(-1), p["s5_w_glu"],
                                            p["s5_b_glu"])
    gr["s5_w_glu"], gr["s5_b_glu"], gr["s5_d"] = dwglu, acc[0], acc[1].reshape(G, S5_GROUP)
    dxs = _mm(dy, s5["ccat"], "nt", "mm_s5_dx")
    dccat = _mm(sv["xs"], dy, "tn", "mm_s5_dc")
    gr["s5_c_re"] = _diag_blocks(dccat[:S5_LANES], G).transpose(0, 2, 1)
    gr["s5_c_im"] = -_diag_blocks(dccat[S5_LANES:], G).transpose(0, 2, 1)
    gs, dab = _s5_scan_bwd(dxs, sv["xs"], s5["ab"])
    du = _mm(gs, s5["bbcat"], "nt", "mm_s5_du", add=du_direct)
    dbbcat = _mm(proj, gs, "tn", "mm_s5_dbb", a_win=(PROJ_S5_U, LRU_WIDTH))[:S5_WIDTH]
    d_ar, d_ai, d_ls, d_btr, d_bti = _s5_params_bwd(
        p["s5_a_re"], p["s5_a_im"], s5["ls"], s5["bt_re"], s5["bt_im"],
        dab[:, :S5_LANES].reshape(G, S5_STATE), dab[:, S5_LANES:].reshape(G, S5_STATE),
        _diag_blocks(dbbcat[:, :S5_LANES], G), _diag_blocks(dbbcat[:, S5_LANES:], G))
    gr["s5_a_re"], gr["s5_a_im"], gr["s5_log_step"] = d_ar, d_ai, d_ls.reshape(G)
    gr["s5_b_re"], gr["s5_b_im"] = d_btr.transpose(0, 2, 1), d_bti.transpose(0, 2, 1)
    pad = jnp.zeros((du.shape[0], D_IN_PAD - D_IN), f32)
    dproj = jnp.concatenate([dq, dk, dv, dxr, dgate, du, pad], axis=1).astype(bf16)
    gr["w_in"] = _mm(sv["h"], dproj, "tn", "mm_dw_in", tn=768)[:, :D_IN]
    dh = _mm(dproj, p["w_in"], "nt", "mm_dh")
    return dz1, dh, gr


def _train_step(d):
    x, target = d["x"][0], d["loss_target"][0]
    S = x.shape[0]
    me = 2 * lax.axis_index("x") + lax.axis_index("y")

    def rows2d(a):
        return a.reshape(a.shape[0] * a.shape[1], a.shape[2])

    shards = [rows2d(d[n]).astype(bf16) for n in BIG] + [rows2d(d[n]) for n in CONV_SHARDED]
    gathered = _gather_weights(shards, [True] * len(BIG) + [False] * len(CONV_SHARDED))
    full = {}
    for n, g in zip(BIG + CONV_SHARDED, gathered):
        full[n] = _assemble(g.reshape((4,) + d[n].shape), BIG_COL_SHARDED.get(n, True))
    full["w_in"] = jnp.pad(full["w_in"], ((0, 0), (0, 0), (0, D_IN_PAD - D_IN)))
    params = [{n: (full[n][l] if n in full else d[n][l]) for n in WEIGHTS} for l in range(DEPTH)]

    cos, sin = _rope_tables(S)
    h, saved = x, []
    for l in range(DEPTH):
        h, sv = _layer_fwd(h, params[l], cos, sin)
        saved.append(sv)
    dy, loss_acc = _loss_head(h, target)
    loss = lax.psum(loss_acc[0, 0], ("x", "y", "c"))
    da, db, grads = None, dy, [None] * DEPTH
    for l in reversed(range(DEPTH)):
        da, db, grads[l] = _layer_bwd(da, db, params[l], saved[l], cos, sin)
    out = {"loss": loss, "grad_x": _axpy(da, db, "grad_x")[None]}
    G = {n: jnp.stack([grads[l][n] for l in range(DEPTH)]) for n in WEIGHTS}

    others = _pair_send([G[n] for n in BIG])
    chip = [_pair_sum(G[n], o, "pair_sum_" + n) for n, o in zip(BIG, others)]
    slabs = [p.reshape(p.shape[0], 4, p.shape[1] // 4).transpose(1, 0, 2) if BIG_COL_SHARDED[n]
             else p.reshape(4, p.shape[0] // 4, p.shape[1]) for n, p in zip(BIG, chip)]
    recv = _chip_exchange(slabs)
    both = _pair_gather([_sum_chips(r, "sum_chips_" + n) for n, r in zip(BIG, recv)])
    for n, g in zip(BIG, both):
        upd = _adamw2d(rows2d(g), rows2d(d[n]), rows2d(d["m_" + n]), rows2d(d["v_" + n]), "adamw_" + n)
        out["grad_" + n] = g
        for pre, a in zip(("delta_", "new_m_", "new_v_"), upd):
            out[pre + n] = a.reshape(d[n].shape)

    small = SMALL + CONV_SHARDED
    sg = _all_gather8(_pack([G[n] for n in small], f32), "gather_small_grads")
    gs = dict(zip(small, _unpack(_sum_slabs(sg, "sum_devices"), [G[n].shape for n in small])))
    for n in CONV_SHARDED:
        L, K, C = gs[n].shape
        gs[n] = lax.dynamic_index_in_dim(gs[n].reshape(L, K, 4, C // 4), me, axis=2, keepdims=False)
    small_shapes = [d[n].shape for n in small]
    gsmall = _pack([gs[n] for n in small], f32)
    upd = _adamw(gsmall, _pack([d[n] for n in small], f32), _pack([d["m_" + n] for n in small], f32),
                 _pack([d["v_" + n] for n in small], f32), "adamw_small")
    for pre, buf in zip(("grad_", "delta_", "new_m_", "new_v_"), (gsmall,) + tuple(upd)):
        for n, a in zip(small, _unpack(buf, small_shapes)):
            out[pre + n] = a

    return (out["loss"], out["grad_x"]) + tuple(out[pre + n] for pre in ("grad_", "delta_", "new_m_", "new_v_")
                                                for n in WEIGHTS)


def kernel(
        x, w_in, lru_conv_w, lru_conv_b, lru_wr, lru_br, lru_wi, lru_bi, lru_lambda, s5_a_re, s5_a_im, s5_b_re,
        s5_b_im, s5_c_re, s5_c_im, s5_d, s5_log_step, s5_w_glu, s5_b_glu, mix_norm_g, w_out, ln1_g, ln1_b, w_up,
        ffn_conv_w, ffn_conv_b, w_down, ln2_g, ln2_b, loss_target, m_w_in, m_lru_conv_w, m_lru_conv_b, m_lru_wr,
        m_lru_br, m_lru_wi, m_lru_bi, m_lru_lambda, m_s5_a_re, m_s5_a_im, m_s5_b_re, m_s5_b_im, m_s5_c_re,
        m_s5_c_im, m_s5_d, m_s5_log_step, m_s5_w_glu, m_s5_b_glu, m_mix_norm_g, m_w_out, m_ln1_g, m_ln1_b,
        m_w_up, m_ffn_conv_w, m_ffn_conv_b, m_w_down, m_ln2_g, m_ln2_b, v_w_in, v_lru_conv_w, v_lru_conv_b,
        v_lru_wr, v_lru_br, v_lru_wi, v_lru_bi, v_lru_lambda, v_s5_a_re, v_s5_a_im, v_s5_b_re, v_s5_b_im,
        v_s5_c_re, v_s5_c_im, v_s5_d, v_s5_log_step, v_s5_w_glu, v_s5_b_glu, v_mix_norm_g, v_w_out, v_ln1_g,
        v_ln1_b, v_w_up, v_ffn_conv_w, v_ffn_conv_b, v_w_down, v_ln2_g, v_ln2_b
):
    return _train_step(dict(locals()))
```

```python
import functools
import math

import jax
import jax.numpy as jnp
from jax import lax
from jax.experimental import pallas as pl
from jax.experimental.pallas import tpu as pltpu

f32 = jnp.float32
bf16 = jnp.bfloat16
MESH = pl.DeviceIdType.MESH

D_MODEL = 1024
ATTN_WIDTH = 384
LRU_WIDTH = 384
S5_WIDTH = 256
HEAD_DIM = 64
N_LRU_HEADS = 6
N_S5_GROUPS = 16
S5_GROUP = 16
S5_STATE = 64
S5_LANES = N_S5_GROUPS * S5_STATE
D_FF = 3072
D_IN = 2176
LRU_C = 8.0
ROPE_THETA = 10000.0
DILATIONS = (1, 4, 16)
ATTN_BLOCK = 128
DEPTH = 2
ALPHA = (2 * DEPTH) ** 0.25
LN_EPS = 1e-5
RMS_EPS = 1e-6
ADAM_LR, ADAM_B1, ADAM_B2, ADAM_EPS, ADAM_WD, ADAM_STEP = 0.001, 0.9, 0.999, 1e-08, 0.01, 10

SUBLANES = 8
LANES = 128
VMEM_LIMIT = 56 * 1024 * 1024
ROW_TILE = 512
MM_SINGLE_K = 3072
D_IN_PAD = 2304
NEG = -1e30


def _cp(*sem):
    return pltpu.CompilerParams(dimension_semantics=sem if sem else None, vmem_limit_bytes=VMEM_LIMIT)


def _pick(dim, pref, align=LANES):
    if dim <= pref:
        return dim
    t = (pref // align) * align
    while t >= align:
        if dim % t == 0:
            return t
        t -= align
    return dim


def _gelu(x):
    return jax.nn.gelu(x)


def _gelu_grad(x):
    c = math.sqrt(2.0 / math.pi)
    t = jnp.tanh(c * (x + 0.044715 * x * x * x))
    return 0.5 * (1.0 + t) + 0.5 * x * (1.0 - t * t) * c * (1.0 + 3 * 0.044715 * x * x)


def _gelu_pair(x):
    c = math.sqrt(2.0 / math.pi)
    x2 = x * x
    t = jnp.tanh(c * x * (1.0 + 0.044715 * x2))
    return 0.5 * x * (1.0 + t), 0.5 * (1.0 + t) + 0.5 * x * (1.0 - t * t) * c * (1.0 + 3 * 0.044715 * x2)


def _sigmoid(x):
    return jax.nn.sigmoid(x)


def _expm1(x):
    p = 1.0 + x / 9.0
    for n in (8.0, 7.0, 6.0, 5.0, 4.0, 3.0, 2.0):
        p = 1.0 + (x / n) * p
    return jnp.where(jnp.abs(x) < 0.3, x * p, jnp.exp(x) - 1.0)


def _dot(a, b, dims):
    return lax.dot_general(a, b, (dims, ((), ())), preferred_element_type=f32)


NN = ((1,), (0,))
NT = ((1,), (1,))
TN = ((0,), (0,))


def _mm(a, b, mode, name, out_dtype=f32, tm=1024, tn=1024, tk=1024, add=None, a_win=None):
    if mode == "nn":
        (M, K), N = a.shape, b.shape[1]
    elif mode == "nt":
        (M, K), N = a.shape, b.shape[0]
    else:
        (K, M), N = a.shape, b.shape[1]
    win = 0
    if a_win is not None:
        win, w = a_win
        if mode == "tn":
            M, tm = w, w
        else:
            K = w
    single = mode != "tn" and K <= MM_SINGLE_K
    tm, tn = _pick(M, tm), _pick(N, tn)
    tk = K if single else _pick(K, tk)
    nk = K // tk
    dims = {"nn": NN, "nt": NT, "tn": TN}[mode]

    def body(a_ref, b_ref, *rest):
        prod = _dot(a_ref[...].astype(bf16), b_ref[...].astype(bf16), dims)
        if single:
            o_ref = rest[-1]
            o_ref[...] = (prod if add is None else prod + rest[0][...]).astype(o_ref.dtype)
            return
        o_ref, acc = rest[-2:]
        k = pl.program_id(2)

        @pl.when(k == 0)
        def _():
            acc[...] = prod if add is None else prod + rest[0][...]

        @pl.when(k > 0)
        def _():
            acc[...] += prod

        @pl.when(k == nk - 1)
        def _():
            o_ref[...] = acc[...].astype(o_ref.dtype)

    if mode == "tn":
        a_spec = pl.BlockSpec((tk, tm), lambda i, j, k: (k, i + win))
    else:
        a_spec = pl.BlockSpec((tm, tk), lambda i, j, k: (i, k + win))
    if mode == "nt":
        b_spec = pl.BlockSpec((tn, tk), lambda i, j, k: (j, k))
    else:
        b_spec = pl.BlockSpec((tk, tn), lambda i, j, k: (k, j))
    o_spec = pl.BlockSpec((tm, tn), lambda i, j, k: (i, j))
    return pl.pallas_call(
        body, name=name, grid=(M // tm, N // tn, nk),
        in_specs=[a_spec, b_spec] + ([] if add is None else [o_spec]), out_specs=o_spec,
        out_shape=jax.ShapeDtypeStruct((M, N), out_dtype),
        scratch_shapes=[] if single else [pltpu.VMEM((tm, tn), f32)],
        compiler_params=_cp("parallel", "parallel", "arbitrary"),
    )(*((a, b) if add is None else (a, b, add)))


def _shift_down(cur, prev8, k):
    if k == 0:
        return cur
    ext = jnp.concatenate([prev8, cur], axis=0)
    return pltpu.roll(ext, k, 0)[SUBLANES:]


def _shift_up(cur, next8, k):
    if k == 0:
        return cur
    n = cur.shape[0] + SUBLANES
    ext = jnp.concatenate([cur, next8], axis=0)
    return pltpu.roll(ext, n - k, 0)[: cur.shape[0]]


def _prev_halo_spec(rt, cols, ncolblk_fn):
    per = rt // SUBLANES
    return pl.BlockSpec((SUBLANES, cols), lambda *g: (jnp.maximum(g[-1] * per - 1, 0), ncolblk_fn(*g)))


def _next_halo_spec(rt, cols, nrows, ncolblk_fn):
    per = rt // SUBLANES
    last = nrows // SUBLANES - 1
    return pl.BlockSpec((SUBLANES, cols), lambda *g: (jnp.minimum((g[-1] + 1) * per, last), ncolblk_fn(*g)))


def _ln_fwd(h, branch, g, b, name):
    S, D = h.shape
    rt = _pick(S, ROW_TILE, SUBLANES)

    def body(h_ref, m_ref, g_ref, b_ref, o_ref, z_ref):
        z = ALPHA * h_ref[...] + m_ref[...]
        mu = jnp.mean(z, axis=-1, keepdims=True)
        zc = z - mu
        var = jnp.mean(zc * zc, axis=-1, keepdims=True)
        o_ref[...] = zc * lax.rsqrt(var + LN_EPS) * g_ref[...] + b_ref[...]
        z_ref[...] = z

    row = pl.BlockSpec((rt, D), lambda i: (i, 0))
    vec = pl.BlockSpec((1, D), lambda i: (0, 0))
    return pl.pallas_call(
        body, name=name, grid=(S // rt,), in_specs=[row, row, vec, vec], out_specs=[row, row],
        out_shape=[jax.ShapeDtypeStruct((S, D), f32)] * 2, compiler_params=_cp("parallel"),
    )(h, branch, g.reshape(1, D), b.reshape(1, D))


def _ln_bwd(dy_a, dy_b, z, g, name):
    S, D = z.shape
    rt = _pick(S, ROW_TILE, SUBLANES)
    two = dy_a is not None

    def body(*refs):
        if two:
            a_ref, b_ref, z_ref, g_ref, dz_ref, acc_ref = refs
            dy = ALPHA * a_ref[...] + b_ref[...]
        else:
            b_ref, z_ref, g_ref, dz_ref, acc_ref = refs
            dy = b_ref[...]
        z = z_ref[...]
        mu = jnp.mean(z, axis=-1, keepdims=True)
        zc = z - mu
        var = jnp.mean(zc * zc, axis=-1, keepdims=True)
        rstd = lax.rsqrt(var + LN_EPS)
        xhat = zc * rstd
        dxh = dy * g_ref[...]
        m1 = jnp.mean(dxh, axis=-1, keepdims=True)
        m2 = jnp.mean(dxh * xhat, axis=-1, keepdims=True)
        dz_ref[...] = rstd * (dxh - m1 - xhat * m2)

        @pl.when(pl.program_id(0) == 0)
        def _():
            acc_ref[...] = jnp.zeros_like(acc_ref)

        acc_ref[0:1, :] += jnp.sum(dy * xhat, axis=0, keepdims=True)
        acc_ref[1:2, :] += jnp.sum(dy, axis=0, keepdims=True)

    row = pl.BlockSpec((rt, D), lambda i: (i, 0))
    vec = pl.BlockSpec((1, D), lambda i: (0, 0))
    acc = pl.BlockSpec((SUBLANES, D), lambda i: (0, 0))
    ins = ([dy_a] if two else []) + [dy_b, z, g.reshape(1, D)]
    return pl.pallas_call(
        body, name=name, grid=(S // rt,), in_specs=[row] * (len(ins) - 1) + [vec], out_specs=[row, acc],
        out_shape=[jax.ShapeDtypeStruct((S, D), f32), jax.ShapeDtypeStruct((SUBLANES, D), f32)],
        compiler_params=_cp("arbitrary"),
    )(*ins)


def _loss_head(y, target):
    S, D = y.shape
    rt = _pick(S, ROW_TILE, SUBLANES)

    def body(y_ref, t_ref, dy_ref, acc_ref):
        e = y_ref[...] - t_ref[...]
        dy_ref[...] = e * (1.0 / D)

        @pl.when(pl.program_id(0) == 0)
        def _():
            acc_ref[...] = jnp.zeros_like(acc_ref)

        part = jnp.sum(jnp.mean(e * e, axis=-1, keepdims=True), axis=0, keepdims=True)
        acc_ref[...] += 0.5 * part

    row = pl.BlockSpec((rt, D), lambda i: (i, 0))
    return pl.pallas_call(
        body, name="loss_head", grid=(S // rt,), in_specs=[row, row],
        out_specs=[row, pl.BlockSpec((1, 1), lambda i: (0, 0))],
        out_shape=[jax.ShapeDtypeStruct((S, D), f32), jax.ShapeDtypeStruct((1, 1), f32)],
        compiler_params=_cp("arbitrary"),
    )(y, target)


def _axpy(a, b, name):
    S, D = a.shape
    rt = _pick(S, ROW_TILE, SUBLANES)

    def body(a_ref, b_ref, o_ref):
        o_ref[...] = ALPHA * a_ref[...] + b_ref[...]

    row = pl.BlockSpec((rt, D), lambda i: (i, 0))
    return pl.pallas_call(
        body, name=name, grid=(S // rt,), in_specs=[row, row], out_specs=row,
        out_shape=jax.ShapeDtypeStruct((S, D), f32), compiler_params=_cp("parallel"),
    )(a, b)


def _rope_tables(S):
    rt = _pick(S, ROW_TILE, SUBLANES)

    def body(c_ref, s_ref):
        pos = (pl.program_id(0) * rt + lax.broadcasted_iota(jnp.int32, (rt, LANES), 0)).astype(f32)
        lane = lax.broadcasted_iota(jnp.int32, (rt, LANES), 1)
        j = (lane % (HEAD_DIM // 2)).astype(f32)
        inv = jnp.exp((-j * 2.0 / HEAD_DIM) * math.log(ROPE_THETA))
        ang = pos * inv
        c = jnp.cos(ang)
        s = jnp.where(lane % HEAD_DIM < HEAD_DIM // 2, -jnp.sin(ang), jnp.sin(ang))
        c_ref[...] = jnp.concatenate([c, c, c], axis=1)
        s_ref[...] = jnp.concatenate([s, s, s], axis=1)

    row = pl.BlockSpec((rt, ATTN_WIDTH), lambda i: (i, 0))
    return pl.pallas_call(
        body, name="rope_tables", grid=(S // rt,), in_specs=[], out_specs=[row, row],
        out_shape=[jax.ShapeDtypeStruct((S, ATTN_WIDTH), f32)] * 2, compiler_params=_cp("parallel"),
    )()


def _swap_halves(x):
    lane = lax.broadcasted_iota(jnp.int32, x.shape, 1)
    half = HEAD_DIM // 2
    return jnp.where(lane % HEAD_DIM < half, pltpu.roll(x, x.shape[1] - half, 1), pltpu.roll(x, half, 1))


def _rope_fwd(proj, cos, sin):
    S, W = proj.shape[0], ATTN_WIDTH
    rt = _pick(S, ROW_TILE, SUBLANES)

    def body(q_ref, k_ref, c_ref, s_ref, qo_ref, ko_ref):
        c, s = c_ref[...], s_ref[...]
        qo_ref[...] = q_ref[...] * c + _swap_halves(q_ref[...]) * s
        ko_ref[...] = k_ref[...] * c + _swap_halves(k_ref[...]) * s

    row = pl.BlockSpec((rt, W), lambda i: (i, 0))
    return pl.pallas_call(
        body, name="rope_fwd", grid=(S // rt,), in_specs=[row, pl.BlockSpec((rt, W), lambda i: (i, 1)), row, row],
        out_specs=[row, row], out_shape=[jax.ShapeDtypeStruct((S, W), f32)] * 2, compiler_params=_cp("parallel"),
    )(proj, proj, cos, sin)


def _rope_bwd(dq, dk, cos, sin):
    S, W = dq.shape
    rt = _pick(S, ROW_TILE, SUBLANES)

    def body(q_ref, k_ref, c_ref, s_ref, qo_ref, ko_ref):
        c, s = c_ref[...], s_ref[...]
        qo_ref[...] = q_ref[...] * c + _swap_halves(q_ref[...] * s)
        ko_ref[...] = k_ref[...] * c + _swap_halves(k_ref[...] * s)

    row = pl.BlockSpec((rt, W), lambda i: (i, 0))
    return pl.pallas_call(
        body, name="rope_bwd", grid=(S // rt,), in_specs=[row] * 4, out_specs=[row] * 2,
        out_shape=[jax.ShapeDtypeStruct((S, W), f32)] * 2, compiler_params=_cp("parallel"),
    )(dq, dk, cos, sin)


def _rows(ref, start, d):
    if d == 1:
        return ref[pl.ds(pl.multiple_of(start, ATTN_BLOCK), ATTN_BLOCK), :]
    return ref[pl.ds(start, ATTN_BLOCK, stride=d), :]


def _set_rows(ref, start, d, val):
    if d == 1:
        ref[pl.ds(pl.multiple_of(start, ATTN_BLOCK), ATTN_BLOCK), :] = val
    else:
        ref[pl.ds(start, ATTN_BLOCK, stride=d), :] = val


def _pair_spec(S, first_block):
    return pl.BlockSpec((S, LANES), lambda p: (0, p + first_block))


def _attn_fwd2(qr, kr, proj):
    S = qr.shape[0]
    B = ATTN_BLOCK
    nb = S // B
    scale = HEAD_DIM ** -0.5

    def body(q_ref, k_ref, v_ref, o_ref, l_ref, m_s, l_s):
        qi = lax.broadcasted_iota(jnp.int32, (B, 2 * B), 0)
        ki = lax.broadcasted_iota(jnp.int32, (B, 2 * B), 1)
        dist = qi + B - ki
        band = (dist >= 0) & (dist <= B)
        for bi, d in enumerate(DILATIONS):
            bpc = nb // d

            def blk(b, carry, bi=bi, d=d, bpc=bpc):
                c, n = b // bpc, b % bpc
                start = c + d * B * n
                pstart = c + d * B * jnp.maximum(n - 1, 0)
                valid = band & ((ki >= B) | (n > 0))
                q = _rows(q_ref, start, d).astype(bf16)
                kcat = jnp.concatenate([_rows(k_ref, pstart, d), _rows(k_ref, start, d)], axis=0).astype(bf16)
                vcat = jnp.concatenate([_rows(v_ref, pstart, d), _rows(v_ref, start, d)], axis=0).astype(bf16)
                if bi > 0:
                    m_old, l_old, a_old = _rows(m_s, start, d), _rows(l_s, start, d), _rows(o_ref, start, d)
                ms, ls, accs = [], [], []
                for h in range(2):
                    sl = slice(h * HEAD_DIM, (h + 1) * HEAD_DIM)
                    c0 = h * HEAD_DIM
                    s = jnp.where(valid, _dot(q[:, sl], kcat[:, sl], NT) * scale, NEG)
                    m = jnp.max(s, axis=1, keepdims=True)
                    if bi > 0:
                        mo = m_old[:, c0:c0 + 1]
                        m = jnp.maximum(m, mo)
                        alpha = jnp.exp(mo - m)
                    p = jnp.exp(s - m)
                    l = jnp.sum(p, axis=1, keepdims=True)
                    acc = _dot(p.astype(bf16), vcat[:, sl], NN)
                    if bi > 0:
                        l = l + alpha * l_old[:, c0:c0 + 1]
                        acc = acc + alpha * a_old[:, sl]
                    ms.append(jnp.broadcast_to(m, (B, HEAD_DIM)))
                    ls.append(jnp.broadcast_to(l, (B, HEAD_DIM)))
                    accs.append(acc)
                _set_rows(m_s, start, d, jnp.concatenate(ms, axis=1))
                _set_rows(l_s, start, d, jnp.concatenate(ls, axis=1))
                _set_rows(o_ref, start, d, jnp.concatenate(accs, axis=1))
                return carry

            lax.fori_loop(0, nb, blk, 0, unroll=2)

        def fin(t, carry):
            rows = pl.ds(pl.multiple_of(t * B, B), B)
            l = l_s[rows, :]
            o_ref[rows, :] = o_ref[rows, :] / l
            l_ref[rows, :] = m_s[rows, :] + jnp.log(l)
            return carry

        lax.fori_loop(0, nb, fin, 0)

    pair = _pair_spec(S, 0)
    return pl.pallas_call(
        body, name="attn_fwd", grid=(3,), in_specs=[pair, pair, _pair_spec(S, 2 * ATTN_WIDTH // LANES)],
        out_specs=[pair, pair], out_shape=[jax.ShapeDtypeStruct((S, ATTN_WIDTH), f32)] * 2,
        scratch_shapes=[pltpu.VMEM((S, LANES), f32)] * 2, compiler_params=_cp("parallel"),
    )(qr, kr, proj)


def _attn_bwd2(qr, kr, proj, dattn, ltot, delta):
    S = qr.shape[0]
    B = ATTN_BLOCK
    nb = S // B
    scale = HEAD_DIM ** -0.5

    def body(q_ref, k_ref, v_ref, do_ref, l_ref, d_ref, dq_ref, dk_ref, dv_ref):
        qi = lax.broadcasted_iota(jnp.int32, (B, 2 * B), 0)
        ki = lax.broadcasted_iota(jnp.int32, (B, 2 * B), 1)
        dist1 = qi + B - ki
        band1 = (dist1 >= 0) & (dist1 <= B)
        ri = lax.broadcasted_iota(jnp.int32, (2 * B, B), 0)
        ci = lax.broadcasted_iota(jnp.int32, (2 * B, B), 1)
        dist2 = ri - ci
        band2 = (dist2 >= 0) & (dist2 <= B)
        for bi, d in enumerate(DILATIONS):
            bpc = nb // d

            def blk(b, carry, bi=bi, d=d, bpc=bpc):
                c, n = b // bpc, b % bpc
                start = c + d * B * n
                pstart = c + d * B * jnp.maximum(n - 1, 0)
                nstart = c + d * B * jnp.minimum(n + 1, bpc - 1)
                valid1 = band1 & ((ki >= B) | (n > 0))
                valid2 = band2 & ((ri < B) | (n + 1 < bpc))
                q_c, q_n = _rows(q_ref, start, d), _rows(q_ref, nstart, d)
                k_p, k_c = _rows(k_ref, pstart, d), _rows(k_ref, start, d)
                v_p, v_c = _rows(v_ref, pstart, d), _rows(v_ref, start, d)
                do_c, do_n = _rows(do_ref, start, d), _rows(do_ref, nstart, d)
                l_c, l_n = _rows(l_ref, start, d), _rows(l_ref, nstart, d)
                d_c, d_n = _rows(d_ref, start, d), _rows(d_ref, nstart, d)
                qc = q_c.astype(bf16)
                qcat = jnp.concatenate([q_c, q_n], axis=0).astype(bf16)
                kc = k_c.astype(bf16)
                kcat = jnp.concatenate([k_p, k_c], axis=0).astype(bf16)
                vc = v_c.astype(bf16)
                vcat = jnp.concatenate([v_p, v_c], axis=0).astype(bf16)
                doc = do_c.astype(bf16)
                docat = jnp.concatenate([do_c, do_n], axis=0).astype(bf16)
                lcat = jnp.concatenate([l_c, l_n], axis=0)
                dcat = jnp.concatenate([d_c, d_n], axis=0)
                dqs, dks, dvs = [], [], []
                for h in range(2):
                    sl = slice(h * HEAD_DIM, (h + 1) * HEAD_DIM)
                    c0 = h * HEAD_DIM
                    s1 = _dot(qc[:, sl], kcat[:, sl], NT) * scale
                    p1 = jnp.where(valid1, jnp.exp(s1 - l_c[:, c0:c0 + 1]), 0.0)
                    dp1 = _dot(doc[:, sl], vcat[:, sl], NT)
                    ds1 = p1 * (dp1 - d_c[:, c0:c0 + 1]) * scale
                    dqs.append(_dot(ds1.astype(bf16), kcat[:, sl], NN))
                    s2 = _dot(qcat[:, sl], kc[:, sl], NT) * scale
                    p2 = jnp.where(valid2, jnp.exp(s2 - lcat[:, c0:c0 + 1]), 0.0)
                    dvs.append(_dot(p2.astype(bf16), docat[:, sl], TN))
                    dp2 = _dot(docat[:, sl], vc[:, sl], NT)
                    ds2 = p2 * (dp2 - dcat[:, c0:c0 + 1]) * scale
                    dks.append(_dot(ds2.astype(bf16), qcat[:, sl], TN))
                for ref, parts in ((dq_ref, dqs), (dk_ref, dks), (dv_ref, dvs)):
                    new = jnp.concatenate(parts, axis=1)
                    if bi > 0:
                        new = new + _rows(ref, start, d)
                    _set_rows(ref, start, d, new)
                return carry

            lax.fori_loop(0, nb, blk, 0, unroll=2)

    pair = _pair_spec(S, 0)
    return pl.pallas_call(
        body, name="attn_bwd", grid=(3,),
        in_specs=[pair, pair, _pair_spec(S, 2 * ATTN_WIDTH // LANES), pair, pair, pair], out_specs=[pair] * 3,
        out_shape=[jax.ShapeDtypeStruct((S, ATTN_WIDTH), f32)] * 3, compiler_params=_cp("parallel"),
    )(qr, kr, proj, dattn, ltot, delta)


def _softplus_neg(lam):
    return jnp.maximum(-lam, 0.0) + jnp.log1p(jnp.exp(-jnp.abs(lam)))


PROJ_LRU_X, PROJ_LRU_GATE, PROJ_S5_U = 3, 4, 5


def _lru_pre(proj, conv_w, conv_b, wr, br, wi, bi, lam):
    S, W = proj.shape[0], LRU_WIDTH
    rt = _pick(S, ROW_TILE, SUBLANES)
    K = conv_w.shape[0]

    def body(x_ref, xp_ref, cw_ref, cb_ref, wr_ref, br_ref, wi_ref, bi_ref, lam_ref,
             xc_ref, r_ref, i_ref, la_ref, u_ref):
        prev = jnp.where(pl.program_id(0) == 0, 0.0, xp_ref[...])
        x = x_ref[...]
        xc = cb_ref[...] + cw_ref[K - 1:K, :] * x
        for k in range(K - 1):
            xc = xc + cw_ref[k:k + 1, :] * _shift_down(x, prev, K - 1 - k)
        xb = xc.astype(bf16)
        r = _sigmoid(_dot(xb, wr_ref[...], NN) + br_ref[...])
        i = _sigmoid(_dot(xb, wi_ref[...], NN) + bi_ref[...])
        log_a = -LRU_C * r * _softplus_neg(lam_ref[...])
        u = jnp.sqrt(-_expm1(2.0 * log_a)) * (i * xc)
        xc_ref[...], r_ref[...], i_ref[...], la_ref[...], u_ref[...] = xc, r, i, log_a, u

    row = pl.BlockSpec((rt, W), lambda i: (i, 0))
    xrow = pl.BlockSpec((rt, W), lambda i: (i, PROJ_LRU_X))
    halo = _prev_halo_spec(rt, W, lambda i: PROJ_LRU_X)
    vec = pl.BlockSpec((1, W), lambda i: (0, 0))
    return pl.pallas_call(
        body, name="lru_pre", grid=(S // rt,),
        in_specs=[xrow, halo, pl.BlockSpec((K, W), lambda i: (0, 0)), vec,
                  pl.BlockSpec((W, W), lambda i: (0, 0)), vec, pl.BlockSpec((W, W), lambda i: (0, 0)), vec, vec],
        out_specs=[row] * 5, out_shape=[jax.ShapeDtypeStruct((S, W), f32)] * 5, compiler_params=_cp("parallel"),
    )(proj, proj, conv_w, conv_b.reshape(1, W), wr, br.reshape(1, W), wi, bi.reshape(1, W), lam.reshape(1, W))


def _tile_rows(shape):
    return lax.broadcasted_iota(jnp.int32, shape, 0)


def _lru_scan(log_a, u, proj):
    S, W = u.shape
    rt = _pick(S, ROW_TILE, SUBLANES)
    T = SUBLANES

    def body(la_ref, u_ref, g_ref, h_ref, o_ref, carry):
        @pl.when(pl.program_id(0) == 0)
        def _():
            carry[...] = jnp.zeros_like(carry)

        row = _tile_rows((T, W))

        def step(t, hp):
            r0 = pl.multiple_of(t * T, T)
            a = jnp.exp(la_ref[pl.ds(r0, T), :])
            x = u_ref[pl.ds(r0, T), :]
            for k in (1, 2, 4):
                x = x + a * jnp.where(row >= k, pltpu.roll(x, k, 0), 0.0)
                a = a * jnp.where(row >= k, pltpu.roll(a, k, 0), 1.0)
            h = x + a * hp
            h_ref[pl.ds(r0, T), :] = h
            o_ref[pl.ds(r0, T), :] = h * _gelu(g_ref[pl.ds(r0, T), :])
            return h[T - 1:T, :]

        carry[0:1, :] = lax.fori_loop(0, rt // T, step, carry[0:1, :])

    row = pl.BlockSpec((rt, W), lambda i: (i, 0))
    grow = pl.BlockSpec((rt, W), lambda i: (i, PROJ_LRU_GATE))
    return pl.pallas_call(
        body, name="lru_scan", grid=(S // rt,), in_specs=[row, row, grow], out_specs=[row] * 2,
        out_shape=[jax.ShapeDtypeStruct((S, W), f32)] * 2, scratch_shapes=[pltpu.VMEM((T, W), f32)],
        compiler_params=_cp("arbitrary"),
    )(log_a, u, proj)


def _lru_scan_bwd(dlru, proj, h, log_a):
    S, W = h.shape
    rt = _pick(S, ROW_TILE, SUBLANES)
    T = SUBLANES
    nblk = S // rt

    def body(d_ref, g_ref, h_ref, la_ref, go_ref, dg_ref, carry):
        @pl.when(pl.program_id(0) == 0)
        def _():
            carry[...] = jnp.zeros_like(carry)

        row = _tile_rows((T, W))

        def step(j, c):
            gn, an = c
            t = rt // T - 1 - j
            r0 = pl.multiple_of(t * T, T)
            d = d_ref[pl.ds(r0, T), :]
            gate = g_ref[pl.ds(r0, T), :]
            a = jnp.exp(la_ref[pl.ds(r0, T), :])
            dg_ref[pl.ds(r0, T), :] = d * h_ref[pl.ds(r0, T), :] * _gelu_grad(gate)
            x = d * _gelu(gate)
            b = jnp.where(row < T - 1, pltpu.roll(a, T - 1, 0), an)
            for k in (1, 2, 4):
                x = x + b * jnp.where(row < T - k, pltpu.roll(x, T - k, 0), 0.0)
                b = b * jnp.where(row < T - k, pltpu.roll(b, T - k, 0), 1.0)
            g = x + b * gn
            go_ref[pl.ds(r0, T), :] = g
            return g[0:1, :], a[0:1, :]

        gn, an = lax.fori_loop(0, rt // T, step, (carry[0:1, :], carry[1:2, :]))
        carry[0:1, :] = gn
        carry[1:2, :] = an

    row = pl.BlockSpec((rt, W), lambda i: (nblk - 1 - i, 0))
    grow = pl.BlockSpec((rt, W), lambda i: (nblk - 1 - i, PROJ_LRU_GATE))
    return pl.pallas_call(
        body, name="lru_scan_bwd", grid=(nblk,), in_specs=[row, grow, row, row], out_specs=[row] * 2,
        out_shape=[jax.ShapeDtypeStruct((S, W), f32)] * 2, scratch_shapes=[pltpu.VMEM((T, W), f32)],
        compiler_params=_cp("arbitrary"),
    )(dlru, proj, h, log_a)


def _lru_gate_bwd(g, h, xc, r, i, log_a, wr, wi, lam):
    S, W = g.shape
    rt = _pick(S, ROW_TILE, SUBLANES)

    def body(g_ref, h_ref, hp_ref, xc_ref, r_ref, i_ref, la_ref, wr_ref, wi_ref, lam_ref,
             dxc_ref, dwr_ref, dwi_ref, acc_ref):
        @pl.when(pl.program_id(0) == 0)
        def _():
            dwr_ref[...] = jnp.zeros_like(dwr_ref)
            dwi_ref[...] = jnp.zeros_like(dwi_ref)
            acc_ref[...] = jnp.zeros_like(acc_ref)

        prev = jnp.where(pl.program_id(0) == 0, 0.0, hp_ref[...])
        gg, xc, r, i, log_a, lam = g_ref[...], xc_ref[...], r_ref[...], i_ref[...], la_ref[...], lam_ref[...]
        hm1 = _shift_down(h_ref[...], prev, 1)
        a = jnp.exp(log_a)
        s = jnp.sqrt(-_expm1(2.0 * log_a))
        da = gg * hm1
        di = gg * s * xc
        dxc = gg * s * i
        ds = gg * i * xc
        dlog_a = da * a - ds * (a * a / s)
        sp = _softplus_neg(lam)
        dr = dlog_a * (-LRU_C * sp)
        dsp = jnp.sum(dlog_a * (-LRU_C * r), axis=0, keepdims=True)
        dpr = dr * r * (1.0 - r)
        dpi = di * i * (1.0 - i)
        dprb, dpib, xb = dpr.astype(bf16), dpi.astype(bf16), xc.astype(bf16)
        dxc_ref[...] = dxc + _dot(dprb, wr_ref[...], NT) + _dot(dpib, wi_ref[...], NT)
        dwr_ref[...] += _dot(xb, dprb, TN)
        dwi_ref[...] += _dot(xb, dpib, TN)
        acc_ref[0:1, :] += jnp.sum(dpr, axis=0, keepdims=True)
        acc_ref[1:2, :] += jnp.sum(dpi, axis=0, keepdims=True)
        acc_ref[2:3, :] += dsp * (-_sigmoid(-lam))

    row = pl.BlockSpec((rt, W), lambda i: (i, 0))
    halo = _prev_halo_spec(rt, W, lambda i: 0)
    vec = pl.BlockSpec((1, W), lambda i: (0, 0))
    mat = pl.BlockSpec((W, W), lambda i: (0, 0))
    acc = pl.BlockSpec((SUBLANES, W), lambda i: (0, 0))
    return pl.pallas_call(
        body, name="lru_gate_bwd", grid=(S // rt,),
        in_specs=[row, row, halo, row, row, row, row, mat, mat, vec], out_specs=[row, mat, mat, acc],
        out_shape=[jax.ShapeDtypeStruct((S, W), f32), jax.ShapeDtypeStruct((W, W), f32),
                   jax.ShapeDtypeStruct((W, W), f32), jax.ShapeDtypeStruct((SUBLANES, W), f32)],
        compiler_params=_cp("arbitrary"),
    )(g, h, h, xc, r, i, log_a, wr, wi, lam.reshape(1, W))


def _conv_bwd(dy, x, conv_w, name, col_tile=None, out_dtype=f32, x_col_block=0):
    if dy.ndim == 2:
        dy = dy[None]
    H, S, Ch = dy.shape
    C = H * Ch
    K = conv_w.shape[0]
    ct = Ch if col_tile is None else col_tile
    nct = Ch // ct
    rt = _pick(S, ROW_TILE, SUBLANES)
    nrt = S // rt

    def body(dy_ref, dyn_ref, x_ref, xp_ref, w_ref, dx_ref, acc_ref):
        i = pl.program_id(2)

        @pl.when(i == 0)
        def _():
            acc_ref[...] = jnp.zeros_like(acc_ref)

        nxt = jnp.where(i == nrt - 1, 0.0, dyn_ref[...])
        prev = jnp.where(i == 0, 0.0, xp_ref[...])
        dy, x = dy_ref[...], x_ref[...]
        dx = w_ref[K - 1:K, :] * dy
        for k in range(K - 1):
            dx = dx + w_ref[k:k + 1, :] * _shift_up(dy, nxt, K - 1 - k)
        dx_ref[...] = dx.astype(dx_ref.dtype)
        for k in range(K):
            acc_ref[k:k + 1, :] += jnp.sum(dy * _shift_down(x, prev, K - 1 - k), axis=0, keepdims=True)
        acc_ref[K:K + 1, :] += jnp.sum(dy, axis=0, keepdims=True)

    per, last = rt // SUBLANES, S // SUBLANES - 1
    dy_row = pl.BlockSpec((None, rt, ct), lambda h, j, i: (h, i, j))
    dy_next = pl.BlockSpec((None, SUBLANES, ct), lambda h, j, i: (h, jnp.minimum((i + 1) * per, last), j))
    row = pl.BlockSpec((rt, ct), lambda h, j, i: (i, h * nct + j))
    xrow = pl.BlockSpec((rt, ct), lambda h, j, i: (i, h * nct + j + x_col_block))
    return pl.pallas_call(
        body, name=name, grid=(H, nct, nrt),
        in_specs=[dy_row, dy_next, xrow, _prev_halo_spec(rt, ct, lambda h, j, i: h * nct + j + x_col_block),
                  pl.BlockSpec((K, ct), lambda h, j, i: (0, h * nct + j))],
        out_specs=[row, pl.BlockSpec((SUBLANES, ct), lambda h, j, i: (0, h * nct + j))],
        out_shape=[jax.ShapeDtypeStruct((S, C), out_dtype), jax.ShapeDtypeStruct((SUBLANES, C), f32)],
        compiler_params=_cp("parallel", "parallel", "arbitrary"),
    )(dy, dy, x, x, conv_w)


def _s5_param_fn(a_re, a_im, ls, bt_re, bt_im):
    step = jnp.exp(ls)
    dt_re, dt_im = step * a_re, step * a_im
    mag = jnp.exp(dt_re)
    ab_re, ab_im = mag * jnp.cos(dt_im), mag * jnp.sin(dt_im)
    z_re, z_im = ab_re - 1.0, ab_im
    den = a_re * a_re + a_im * a_im
    f_re = (z_re * a_re + z_im * a_im) / den
    f_im = (z_im * a_re - z_re * a_im) / den
    bb_re = f_re[:, None, :] * bt_re - f_im[:, None, :] * bt_im
    bb_im = f_re[:, None, :] * bt_im + f_im[:, None, :] * bt_re
    return ab_re, ab_im, bb_re, bb_im


def _s5_params(a_re, a_im, ls, bt_re, bt_im):
    def body(ar, ai, l, br, bi, o_ar, o_ai, o_br, o_bi):
        o_ar[...], o_ai[...], o_br[...], o_bi[...] = _s5_param_fn(ar[...], ai[...], l[...], br[...], bi[...])

    return pl.pallas_call(
        body, name="s5_params",
        out_shape=[jax.ShapeDtypeStruct(a_re.shape, f32)] * 2 + [jax.ShapeDtypeStruct(bt_re.shape, f32)] * 2,
        compiler_params=_cp(),
    )(a_re, a_im, ls, bt_re, bt_im)


def _s5_params_bwd(a_re, a_im, ls, bt_re, bt_im, d_ar, d_ai, d_br, d_bi):
    def body(ar, ai, l, br, bi, c_ar, c_ai, c_br, c_bi, g_ar, g_ai, g_l, g_br, g_bi):
        _, vjp = jax.vjp(_s5_param_fn, ar[...], ai[...], l[...], br[...], bi[...])
        g_ar[...], g_ai[...], g_l[...], g_br[...], g_bi[...] = vjp((c_ar[...], c_ai[...], c_br[...], c_bi[...]))

    return pl.pallas_call(
        body, name="s5_params_bwd",
        out_shape=[jax.ShapeDtypeStruct(a_re.shape, f32)] * 2 + [jax.ShapeDtypeStruct(ls.shape, f32)]
        + [jax.ShapeDtypeStruct(bt_re.shape, f32)] * 2,
        compiler_params=_cp(),
    )(a_re, a_im, ls, bt_re, bt_im, d_ar, d_ai, d_br, d_bi)


S5_CHUNK = 256


def _s5_power_tables(ab_ref, p_ref, w_ref, conj):
    T, L = SUBLANES, S5_LANES
    are = ab_ref[0:1, 0:L]
    aim = ab_ref[0:1, L:2 * L]
    if conj:
        aim = -aim
    pre, pim = are, aim
    for n in range(3):
        p_ref[n:n + 1, 0:L] = pre
        p_ref[n:n + 1, L:2 * L] = pim
        pre, pim = pre * pre - pim * pim, 2.0 * pre * pim
    row = _tile_rows((T, L))
    wre = jnp.zeros((T, L), f32)
    wim = jnp.zeros((T, L), f32)
    pre, pim = are, aim
    for n in range(T):
        tgt = (T - 1 - n) if conj else n
        wre = jnp.where(row == tgt, pre, wre)
        wim = jnp.where(row == tgt, pim, wim)
        pre, pim = pre * are - pim * aim, pre * aim + pim * are
    w_ref[:, 0:L] = wre
    w_ref[:, L:2 * L] = wim


def _s5_scan(bu, ab):
    S, L2 = bu.shape
    L = L2 // 2
    rt = _pick(S, 256, SUBLANES)
    T = SUBLANES
    CH = S5_CHUNK

    def body(bu_ref, ab_ref, x_ref, p_ref, w_ref, carry):
        @pl.when(pl.program_id(0) == 0)
        def _():
            carry[...] = jnp.zeros_like(carry)
            _s5_power_tables(ab_ref, p_ref, w_ref, conj=False)

        row = _tile_rows((T, CH))

        def step(t, _):
            r0 = pl.multiple_of(t * T, T)
            for c in range(L // CH):
                lre, lim = pl.ds(c * CH, CH), pl.ds(L + c * CH, CH)
                xr, xi = bu_ref[pl.ds(r0, T), lre], bu_ref[pl.ds(r0, T), lim]
                for n, k in enumerate((1, 2, 4)):
                    pr, pi = p_ref[n:n + 1, lre], p_ref[n:n + 1, lim]
                    sr = jnp.where(row >= k, pltpu.roll(xr, k, 0), 0.0)
                    si = jnp.where(row >= k, pltpu.roll(xi, k, 0), 0.0)
                    xr, xi = xr + pr * sr - pi * si, xi + pr * si + pi * sr
                cr, ci = carry[T - 1:T, lre], carry[T - 1:T, lim]
                wr, wi = w_ref[:, lre], w_ref[:, lim]
                xr, xi = xr + wr * cr - wi * ci, xi + wr * ci + wi * cr
                carry[:, lre] = xr
                carry[:, lim] = xi
                x_ref[pl.ds(r0, T), lre] = xr
                x_ref[pl.ds(r0, T), lim] = xi
            return 0

        lax.fori_loop(0, rt // T, step, 0)

    row_spec = pl.BlockSpec((rt, L2), lambda i: (i, 0))
    return pl.pallas_call(
        body, name="s5_scan", grid=(S // rt,), in_specs=[row_spec, pl.BlockSpec((1, L2), lambda i: (0, 0))],
        out_specs=row_spec, out_shape=jax.ShapeDtypeStruct((S, L2), f32),
        scratch_shapes=[pltpu.VMEM((T, L2), f32), pltpu.VMEM((T, L2), f32), pltpu.VMEM((T, L2), f32)],
        compiler_params=_cp("arbitrary"),
    )(bu, ab)


def _s5_scan_bwd(dx, x, ab):
    S, L2 = dx.shape
    L = L2 // 2
    rt = _pick(S, 256, SUBLANES)
    T = SUBLANES
    CH = S5_CHUNK
    nblk = S // rt
    per = rt // T

    def body(dx_ref, x_ref, xp_ref, ab_ref, g_ref, da_ref, p_ref, w_ref, carry, acc):
        pid = pl.program_id(0)

        @pl.when(pid == 0)
        def _():
            carry[...] = jnp.zeros_like(carry)
            acc[...] = jnp.zeros_like(acc)
            _s5_power_tables(ab_ref, p_ref, w_ref, conj=True)

        row = _tile_rows((T, CH))
        first_block = pid == nblk - 1

        def step(j, _):
            t = per - 1 - j
            r0 = pl.multiple_of(t * T, T)
            rp = pl.multiple_of(jnp.maximum(t - 1, 0) * T, T)
            for c in range(L // CH):
                lre, lim = pl.ds(c * CH, CH), pl.ds(L + c * CH, CH)
                gr, gi = dx_ref[pl.ds(r0, T), lre], dx_ref[pl.ds(r0, T), lim]
                for n, k in enumerate((1, 2, 4)):
                    pr, pi = p_ref[n:n + 1, lre], p_ref[n:n + 1, lim]
                    sr = jnp.where(row < T - k, pltpu.roll(gr, T - k, 0), 0.0)
                    si = jnp.where(row < T - k, pltpu.roll(gi, T - k, 0), 0.0)
                    gr, gi = gr + pr * sr - pi * si, gi + pr * si + pi * sr
                cr, ci = carry[0:1, lre], carry[0:1, lim]
                wr, wi = w_ref[:, lre], w_ref[:, lim]
                gr, gi = gr + wr * cr - wi * ci, gi + wr * ci + wi * cr
                carry[:, lre] = gr
                carry[:, lim] = gi
                g_ref[pl.ds(r0, T), lre] = gr
                g_ref[pl.ds(r0, T), lim] = gi
                xr, xi = x_ref[pl.ds(r0, T), lre], x_ref[pl.ds(r0, T), lim]
                in_blk_r, in_blk_i = x_ref[pl.ds(rp, T), lre], x_ref[pl.ds(rp, T), lim]
                hal_r = jnp.where(first_block, 0.0, xp_ref[:, lre])
                hal_i = jnp.where(first_block, 0.0, xp_ref[:, lim])
                pvr = jnp.where(t == 0, hal_r, in_blk_r)[T - 1:T, :]
                pvi = jnp.where(t == 0, hal_i, in_blk_i)[T - 1:T, :]
                sxr = jnp.where(row >= 1, pltpu.roll(xr, 1, 0), pvr)
                sxi = jnp.where(row >= 1, pltpu.roll(xi, 1, 0), pvi)
                acc[:, lre] += gr * sxr + gi * sxi
                acc[:, lim] += gi * sxr - gr * sxi
            return 0

        lax.fori_loop(0, per, step, 0)

        @pl.when(pid == nblk - 1)
        def _():
            da_ref[...] = jnp.sum(acc[...], axis=0, keepdims=True)

    row_spec = pl.BlockSpec((rt, L2), lambda i: (nblk - 1 - i, 0))
    halo = pl.BlockSpec((T, L2), lambda i: (jnp.maximum((nblk - 1 - i) * per - 1, 0), 0))
    vec = pl.BlockSpec((1, L2), lambda i: (0, 0))
    return pl.pallas_call(
        body, name="s5_scan_bwd", grid=(nblk,), in_specs=[row_spec, row_spec, halo, vec],
        out_specs=[row_spec, vec],
        out_shape=[jax.ShapeDtypeStruct((S, L2), f32), jax.ShapeDtypeStruct((1, L2), f32)],
        scratch_shapes=[pltpu.VMEM((T, L2), f32)] * 4,
        compiler_params=_cp("arbitrary"),
    )(dx, x, x, ab)


def _S5_U_SPEC(rt):
    return pl.BlockSpec((rt, LRU_WIDTH), lambda i: (i, PROJ_S5_U))


def _s5_out(yc, proj, d, wglu, bglu):
    S, W = yc.shape
    rt = _pick(S, ROW_TILE, SUBLANES)

    def body(yc_ref, u_ref, d_ref, w_ref, b_ref, o_ref, y_ref):
        y = yc_ref[...] + d_ref[...] * u_ref[:, 0:W]
        yg = _gelu(y)
        z = _dot(yg.astype(bf16), w_ref[...], NN) + b_ref[...]
        o_ref[...] = yg * _sigmoid(z)
        y_ref[...] = y

    row = pl.BlockSpec((rt, W), lambda i: (i, 0))
    vec = pl.BlockSpec((1, W), lambda i: (0, 0))
    mat = pl.BlockSpec((W, W), lambda i: (0, 0))
    return pl.pallas_call(
        body, name="s5_out", grid=(S // rt,), in_specs=[row, _S5_U_SPEC(rt), vec, mat, vec], out_specs=[row, row],
        out_shape=[jax.ShapeDtypeStruct((S, W), f32)] * 2, compiler_params=_cp("parallel"),
    )(yc, proj, d.reshape(1, W), wglu, bglu.reshape(1, W))


def _s5_out_bwd(dssm, y, proj, d, wglu, bglu):
    S, W = y.shape
    rt = _pick(S, ROW_TILE, SUBLANES)

    def body(do_ref, y_ref, u_ref, d_ref, w_ref, b_ref, dy_ref, du_ref, dw_ref, acc_ref):
        @pl.when(pl.program_id(0) == 0)
        def _():
            dw_ref[...] = jnp.zeros_like(dw_ref)
            acc_ref[...] = jnp.zeros_like(acc_ref)

        do, y = do_ref[...], y_ref[...]
        yg = _gelu(y)
        ygb = yg.astype(bf16)
        sg = _sigmoid(_dot(ygb, w_ref[...], NN) + b_ref[...])
        dz = do * yg * sg * (1.0 - sg)
        dzb = dz.astype(bf16)
        dyg = do * sg + _dot(dzb, w_ref[...], NT)
        dy = dyg * _gelu_grad(y)
        dy_ref[...] = dy
        du_ref[...] = dy * d_ref[...]
        dw_ref[...] += _dot(ygb, dzb, TN)
        acc_ref[0:1, :] += jnp.sum(dz, axis=0, keepdims=True)
        acc_ref[1:2, :] += jnp.sum(dy * u_ref[:, 0:W], axis=0, keepdims=True)

    row = pl.BlockSpec((rt, W), lambda i: (i, 0))
    vec = pl.BlockSpec((1, W), lambda i: (0, 0))
    mat = pl.BlockSpec((W, W), lambda i: (0, 0))
    acc = pl.BlockSpec((SUBLANES, W), lambda i: (0, 0))
    return pl.pallas_call(
        body, name="s5_out_bwd", grid=(S // rt,), in_specs=[row, row, _S5_U_SPEC(rt), vec, mat, vec],
        out_specs=[row, row, mat, acc],
        out_shape=[jax.ShapeDtypeStruct((S, W), f32)] * 2
        + [jax.ShapeDtypeStruct((W, W), f32), jax.ShapeDtypeStruct((SUBLANES, W), f32)],
        compiler_params=_cp("arbitrary"),
    )(dssm, y, proj, d.reshape(1, W), wglu, bglu.reshape(1, W))


MIX_SPLITS = ((0, ATTN_WIDTH), (ATTN_WIDTH, ATTN_WIDTH + LRU_WIDTH), (ATTN_WIDTH + LRU_WIDTH, D_MODEL))


def _mixnorm(attn, lru, ssm, g):
    S = attn.shape[0]
    rt = _pick(S, ROW_TILE, SUBLANES)

    def body(a_ref, l_ref, s_ref, g_ref, o_ref):
        for ref, (lo, hi) in zip((a_ref, l_ref, s_ref), MIX_SPLITS):
            x = ref[...]
            ms = jnp.mean(x * x, axis=-1, keepdims=True)
            o_ref[:, lo:hi] = (x * lax.rsqrt(ms + RMS_EPS) * g_ref[:, lo:hi]).astype(o_ref.dtype)

    rows = [pl.BlockSpec((rt, hi - lo), lambda i: (i, 0)) for lo, hi in MIX_SPLITS]
    return pl.pallas_call(
        body, name="mixnorm", grid=(S // rt,), in_specs=rows + [pl.BlockSpec((1, D_MODEL), lambda i: (0, 0))],
        out_specs=pl.BlockSpec((rt, D_MODEL), lambda i: (i, 0)),
        out_shape=jax.ShapeDtypeStruct((S, D_MODEL), bf16), compiler_params=_cp("parallel"),
    )(attn, lru, ssm, g.reshape(1, D_MODEL))


def _mixnorm_bwd(dmixed, attn, lru, ssm, g):
    S = attn.shape[0]
    rt = _pick(S, ROW_TILE, SUBLANES)

    def body(d_ref, a_ref, l_ref, s_ref, g_ref, da_ref, dl_ref, ds_ref, dlt_ref, acc_ref):
        @pl.when(pl.program_id(0) == 0)
        def _():
            acc_ref[...] = jnp.zeros_like(acc_ref)

        outs = []
        for ref, (lo, hi) in zip((a_ref, l_ref, s_ref), MIX_SPLITS):
            x = ref[...]
            dy = d_ref[:, lo:hi]
            rinv = lax.rsqrt(jnp.mean(x * x, axis=-1, keepdims=True) + RMS_EPS)
            dyg = dy * g_ref[:, lo:hi]
            outs.append(rinv * dyg - x * (rinv * rinv * rinv) * jnp.mean(dyg * x, axis=-1, keepdims=True))
            acc_ref[0:1, lo:hi] += jnp.sum(dy * x * rinv, axis=0, keepdims=True)
        da_ref[...], dl_ref[...], ds_ref[...] = outs
        hi_ = lax.broadcasted_iota(jnp.int32, (ATTN_WIDTH, ATTN_WIDTH), 0) // HEAD_DIM
        hj_ = lax.broadcasted_iota(jnp.int32, (ATTN_WIDTH, ATTN_WIDTH), 1) // HEAD_DIM
        same = jnp.where(hi_ == hj_, 1.0, 0.0).astype(f32)
        dlt_ref[...] = jnp.dot(outs[0] * a_ref[...], same, precision=lax.Precision.HIGHEST, preferred_element_type=f32)

    rows = [pl.BlockSpec((rt, hi - lo), lambda i: (i, 0)) for lo, hi in MIX_SPLITS]
    full = pl.BlockSpec((rt, D_MODEL), lambda i: (i, 0))
    return pl.pallas_call(
        body, name="mixnorm_bwd", grid=(S // rt,),
        in_specs=[full] + rows + [pl.BlockSpec((1, D_MODEL), lambda i: (0, 0))],
        out_specs=rows + [rows[0], pl.BlockSpec((SUBLANES, D_MODEL), lambda i: (0, 0))],
        out_shape=[jax.ShapeDtypeStruct((S, hi - lo), f32) for lo, hi in MIX_SPLITS]
        + [jax.ShapeDtypeStruct((S, ATTN_WIDTH), f32), jax.ShapeDtypeStruct((SUBLANES, D_MODEL), f32)],
        compiler_params=_cp("arbitrary"),
    )(dmixed, attn, lru, ssm, g.reshape(1, D_MODEL))


FFN_COL_TILE = 512


def _ffn_conv(x, prev, w_ref, b_ref, K):
    y = b_ref[...] + w_ref[K - 1:K, :] * x
    for k in range(K - 1):
        y = y + w_ref[k:k + 1, :] * _shift_down(x, prev, K - 1 - k)
    return y


def _ffn_act(up, conv_w, conv_b):
    S, C2 = up.shape
    C = C2 // 2
    K = conv_w.shape[0]
    ct = FFN_COL_TILE
    nct = C // ct
    rt = _pick(S, ROW_TILE, SUBLANES)

    def body(g_ref, gp_ref, v_ref, vp_ref, wg_ref, wv_ref, bg_ref, bv_ref, o_ref):
        first = pl.program_id(1) == 0
        gate = _ffn_conv(g_ref[...], jnp.where(first, 0.0, gp_ref[...]), wg_ref, bg_ref, K)
        val = _ffn_conv(v_ref[...], jnp.where(first, 0.0, vp_ref[...]), wv_ref, bv_ref, K)
        o_ref[...] = (_gelu(gate) * val).astype(o_ref.dtype)

    def specs(off):
        return (pl.BlockSpec((rt, ct), lambda j, i: (i, j + off)), _prev_halo_spec(rt, ct, lambda j, i: j + off))

    def wspec(off, rows):
        return pl.BlockSpec((rows, ct), lambda j, i: (0, j + off))

    g_s, gp_s = specs(0)
    v_s, vp_s = specs(nct)
    return pl.pallas_call(
        body, name="ffn_act", grid=(nct, S // rt),
        in_specs=[g_s, gp_s, v_s, vp_s, wspec(0, K), wspec(nct, K), wspec(0, 1), wspec(nct, 1)],
        out_specs=pl.BlockSpec((rt, ct), lambda j, i: (i, j)),
        out_shape=jax.ShapeDtypeStruct((S, C), bf16), compiler_params=_cp("parallel", "parallel"),
    )(up, up, up, up, conv_w, conv_w, conv_b.reshape(1, C2), conv_b.reshape(1, C2))


def _ffn_act_bwd(dact, up, conv_w, conv_b):
    S, C2 = up.shape
    C = C2 // 2
    K = conv_w.shape[0]
    ct = FFN_COL_TILE
    nct = C // ct
    rt = _pick(S, ROW_TILE, SUBLANES)

    def body(d_ref, g_ref, gp_ref, v_ref, vp_ref, wg_ref, wv_ref, bg_ref, bv_ref, o_ref):
        first = pl.program_id(1) == 0
        gate = _ffn_conv(g_ref[...], jnp.where(first, 0.0, gp_ref[...]), wg_ref, bg_ref, K)
        val = _ffn_conv(v_ref[...], jnp.where(first, 0.0, vp_ref[...]), wv_ref, bv_ref, K)
        d = d_ref[...]
        gl, dgl = _gelu_pair(gate)
        o_ref[0] = d * val * dgl
        o_ref[1] = d * gl

    def specs(off):
        return (pl.BlockSpec((rt, ct), lambda j, i: (i, j + off)), _prev_halo_spec(rt, ct, lambda j, i: j + off))

    def wspec(off, rows):
        return pl.BlockSpec((rows, ct), lambda j, i: (0, j + off))

    g_s, gp_s = specs(0)
    v_s, vp_s = specs(nct)
    return pl.pallas_call(
        body, name="ffn_act_bwd", grid=(nct, S // rt),
        in_specs=[pl.BlockSpec((rt, ct), lambda j, i: (i, j)), g_s, gp_s, v_s, vp_s,
                  wspec(0, K), wspec(nct, K), wspec(0, 1), wspec(nct, 1)],
        out_specs=pl.BlockSpec((2, rt, ct), lambda j, i: (0, i, j)),
        out_shape=jax.ShapeDtypeStruct((2, S, C), f32), compiler_params=_cp("parallel", "parallel"),
    )(dact, up, up, up, up, conv_w, conv_w, conv_b.reshape(1, C2), conv_b.reshape(1, C2))


ANY = pl.BlockSpec(memory_space=pl.ANY)


def _rows_for(cols):
    return max(16, (1 << 17) // cols)


def _chips(x, y):
    return [(1 - x, y), (x, 1 - y), (1 - x, 1 - y)]


def _gather_weights(shards, split):
    nt = len(shards)

    def body(*refs):
        ins, outs = refs[:nt], refs[nt:2 * nt]
        send1, recv1, send2, recv2 = refs[2 * nt:]
        x, y, c = lax.axis_index("x"), lax.axis_index("y"), lax.axis_index("c")
        me = 2 * x + y
        sib = (x, y, 1 - c)
        chips = _chips(x, y)

        def part(ref, t, layer):
            if not split[t]:
                return ref
            r = shards[t].shape[0] // 2
            return ref.at[pl.ds(layer * r, r), :]

        sends = []
        for k, (px, py) in enumerate(chips):
            for t in range(nt):
                cp = pltpu.make_async_remote_copy(
                    src_ref=part(ins[t], t, c), dst_ref=part(outs[t].at[me], t, c), send_sem=send1.at[k, t],
                    recv_sem=recv1.at[k, t], device_id=(px, py, c), device_id_type=MESH)
                cp.start()
                sends.append(cp)
        for k, (px, py) in enumerate(chips):
            q = 2 * px + py
            for t in range(nt):
                landed = part(outs[t].at[q], t, c)
                pltpu.make_async_remote_copy(
                    src_ref=part(ins[t], t, c), dst_ref=landed, send_sem=send1.at[k, t], recv_sem=recv1.at[k, t],
                    device_id=(px, py, c), device_id_type=MESH).wait_recv()
                if split[t]:
                    cp = pltpu.make_async_remote_copy(
                        src_ref=landed, dst_ref=landed, send_sem=send2.at[k, t], recv_sem=recv2.at[k, t],
                        device_id=sib, device_id_type=MESH)
                    cp.start()
                    sends.append(cp)
        for k, (px, py) in enumerate(chips):
            q = 2 * px + py
            for t in range(nt):
                if split[t]:
                    other = part(outs[t].at[q], t, 1 - c)
                    pltpu.make_async_remote_copy(
                        src_ref=other, dst_ref=other, send_sem=send2.at[k, t], recv_sem=recv2.at[k, t],
                        device_id=sib, device_id_type=MESH).wait_recv()
        for cp in sends:
            cp.wait_send()

    sems = [pltpu.SemaphoreType.DMA((3, nt))] * 4
    return pl.pallas_call(
        body, name="gather_weights", in_specs=[ANY] * nt, out_specs=[ANY] * nt,
        out_shape=[jax.ShapeDtypeStruct((4,) + s.shape, s.dtype) for s in shards], scratch_shapes=sems,
    )(*shards)


def _pair_send(grads):
    nt = len(grads)

    def body(*refs):
        ins, outs = refs[:nt], refs[nt:2 * nt]
        send, recv = refs[2 * nt:]
        x, y, c = lax.axis_index("x"), lax.axis_index("y"), lax.axis_index("c")
        cps = [pltpu.make_async_remote_copy(
            src_ref=ins[t].at[1 - c], dst_ref=outs[t], send_sem=send.at[t], recv_sem=recv.at[t],
            device_id=(x, y, 1 - c), device_id_type=MESH) for t in range(nt)]
        for cp in cps:
            cp.start()
        for cp in cps:
            cp.wait()

    return pl.pallas_call(
        body, name="pair_send", in_specs=[ANY] * nt, out_specs=[ANY] * nt,
        out_shape=[jax.ShapeDtypeStruct(g.shape[1:], g.dtype) for g in grads],
        scratch_shapes=[pltpu.SemaphoreType.DMA((nt,))] * 2,
    )(*grads)


def _pair_sum(g, other, name):
    _, R, C = g.shape
    rt = _pick(R, _rows_for(C), 16)

    def body(g_ref, o_ref, out_ref):
        out_ref[...] = (g_ref[lax.axis_index("c")] + o_ref[...]).astype(out_ref.dtype)

    return pl.pallas_call(
        body, name=name, grid=(R // rt,),
        in_specs=[pl.BlockSpec((2, rt, C), lambda i: (0, i, 0)), pl.BlockSpec((rt, C), lambda i: (i, 0))],
        out_specs=pl.BlockSpec((rt, C), lambda i: (i, 0)), out_shape=jax.ShapeDtypeStruct((R, C), bf16),
        compiler_params=_cp("parallel"),
    )(g, other)


def _chip_exchange(slabs):
    nt = len(slabs)

    def body(*refs):
        ins, outs = refs[:nt], refs[nt:2 * nt]
        send, recv = refs[2 * nt:]
        x, y, c = lax.axis_index("x"), lax.axis_index("y"), lax.axis_index("c")
        me = 2 * x + y
        chips = _chips(x, y)
        sends = []
        for k, (px, py) in enumerate(chips):
            for t in range(nt):
                cp = pltpu.make_async_remote_copy(
                    src_ref=ins[t].at[2 * px + py], dst_ref=outs[t].at[me], send_sem=send.at[k, t],
                    recv_sem=recv.at[k, t], device_id=(px, py, c), device_id_type=MESH)
                cp.start()
                sends.append(cp)
        for k, (px, py) in enumerate(chips):
            q = 2 * px + py
            for t in range(nt):
                pltpu.make_async_remote_copy(
                    src_ref=ins[t].at[q], dst_ref=outs[t].at[q], send_sem=send.at[k, t], recv_sem=recv.at[k, t],
                    device_id=(px, py, c), device_id_type=MESH).wait_recv()
        for cp in sends:
            cp.wait_send()

    return pl.pallas_call(
        body, name="chip_exchange", in_specs=[ANY] * nt, out_specs=[ANY] * nt,
        out_shape=[jax.ShapeDtypeStruct(s.shape, s.dtype) for s in slabs],
        scratch_shapes=[pltpu.SemaphoreType.DMA((3, nt))] * 2,
    )(*slabs)


def _sum_chips(recv, own, name):
    n, r, C = recv.shape
    rt = _pick(r, _rows_for(C), 16)
    own3 = own.ndim == 3

    def body(r_ref, o_ref, out_ref):
        me = 2 * lax.axis_index("x") + lax.axis_index("y")
        acc = None
        for q in range(n):
            term = jnp.where(me == q, o_ref[q] if own3 else o_ref[...], r_ref[q]).astype(f32)
            acc = term if acc is None else acc + term
        out_ref[...] = acc

    blk = pl.BlockSpec((n, rt, C), lambda i: (0, i, 0))
    return pl.pallas_call(
        body, name=name, grid=(r // rt,), in_specs=[blk, blk if own3 else pl.BlockSpec((rt, C), lambda i: (i, 0))],
        out_specs=pl.BlockSpec((rt, C), lambda i: (i, 0)), out_shape=jax.ShapeDtypeStruct((r, C), f32),
        compiler_params=_cp("parallel"),
    )(recv, own)


def _pair_swap(parts):
    nt = len(parts)

    def body(*refs):
        ins, outs = refs[:nt], refs[nt:2 * nt]
        send, recv = refs[2 * nt:]
        x, y, c = lax.axis_index("x"), lax.axis_index("y"), lax.axis_index("c")
        cps = [pltpu.make_async_remote_copy(
            src_ref=ins[t], dst_ref=outs[t], send_sem=send.at[t], recv_sem=recv.at[t],
            device_id=(x, y, 1 - c), device_id_type=MESH) for t in range(nt)]
        for cp in cps:
            cp.start()
        for cp in cps:
            cp.wait()

    return pl.pallas_call(
        body, name="pair_swap", in_specs=[ANY] * nt, out_specs=[ANY] * nt,
        out_shape=[jax.ShapeDtypeStruct(p.shape, p.dtype) for p in parts],
        scratch_shapes=[pltpu.SemaphoreType.DMA((nt,))] * 2,
    )(*parts)


def _adamw_layers(mine, theirs, w, m, v, name):
    L, r, C = w.shape
    rt = _pick(r, _rows_for(C), 16)

    def body(a_ref, b_ref, w_ref, m_ref, v_ref, g_ref, d_ref, mo_ref, vo_ref):
        g_ref[...] = jnp.where(pl.program_id(0) == lax.axis_index("c"), a_ref[...], b_ref[...])
        _adamw_math(g_ref, w_ref, m_ref, v_ref, d_ref, mo_ref, vo_ref)

    flat = pl.BlockSpec((rt, C), lambda l, i: (i, 0))
    lay = pl.BlockSpec((None, rt, C), lambda l, i: (l, i, 0))
    return pl.pallas_call(
        body, name=name, grid=(L, r // rt), in_specs=[flat, flat, lay, lay, lay], out_specs=[lay] * 4,
        out_shape=[jax.ShapeDtypeStruct((L, r, C), f32)] * 4, compiler_params=_cp("parallel", "parallel"),
    )(mine, theirs, w, m, v)


def _adamw_math(g_ref, w_ref, m_ref, v_ref, d_ref, mo_ref, vo_ref):
    gg = g_ref[...]
    m_new = ADAM_B1 * m_ref[...] + (1.0 - ADAM_B1) * gg
    v_new = ADAM_B2 * v_ref[...] + (1.0 - ADAM_B2) * (gg * gg)
    m_hat = m_new / (1.0 - ADAM_B1 ** ADAM_STEP)
    v_hat = v_new / (1.0 - ADAM_B2 ** ADAM_STEP)
    d_ref[...] = -ADAM_LR * (m_hat / (jnp.sqrt(v_hat) + ADAM_EPS) + ADAM_WD * w_ref[...])
    mo_ref[...] = m_new
    vo_ref[...] = v_new


def _chip_gather(src):
    def body(src_ref, out_ref, send, recv):
        x, y, c = lax.axis_index("x"), lax.axis_index("y"), lax.axis_index("c")
        me = 2 * x + y
        chips = _chips(x, y)
        sends = []
        for k, (px, py) in enumerate(chips):
            cp = pltpu.make_async_remote_copy(
                src_ref=src_ref, dst_ref=out_ref.at[me], send_sem=send.at[k], recv_sem=recv.at[k],
                device_id=(px, py, c), device_id_type=MESH)
            cp.start()
            sends.append(cp)
        for k, (px, py) in enumerate(chips):
            pltpu.make_async_remote_copy(
                src_ref=src_ref, dst_ref=out_ref.at[2 * px + py], send_sem=send.at[k], recv_sem=recv.at[k],
                device_id=(px, py, c), device_id_type=MESH).wait_recv()
        for cp in sends:
            cp.wait_send()

    return pl.pallas_call(
        body, name="chip_gather", in_specs=[ANY], out_specs=ANY,
        out_shape=jax.ShapeDtypeStruct((4,) + src.shape, src.dtype),
        scratch_shapes=[pltpu.SemaphoreType.DMA((3,))] * 2,
    )(src)


FLAT_TILE = 2048


def _add2(a, b, name):
    R = a.shape[0]
    rt = _pick(R, FLAT_TILE, SUBLANES)

    def body(a_ref, b_ref, o_ref):
        o_ref[...] = a_ref[...] + b_ref[...]

    row = pl.BlockSpec((rt, LANES), lambda i: (i, 0))
    return pl.pallas_call(
        body, name=name, grid=(R // rt,), in_specs=[row, row], out_specs=row,
        out_shape=jax.ShapeDtypeStruct((R, LANES), f32), compiler_params=_cp("parallel"),
    )(a, b)


def _adamw(g, w, m, v, name):
    R = g.shape[0]
    rt = _pick(R, FLAT_TILE, SUBLANES)

    def body(g_ref, w_ref, m_ref, v_ref, d_ref, mo_ref, vo_ref):
        gg = g_ref[...]
        m_new = ADAM_B1 * m_ref[...] + (1.0 - ADAM_B1) * gg
        v_new = ADAM_B2 * v_ref[...] + (1.0 - ADAM_B2) * (gg * gg)
        m_hat = m_new / (1.0 - ADAM_B1 ** ADAM_STEP)
        v_hat = v_new / (1.0 - ADAM_B2 ** ADAM_STEP)
        d_ref[...] = -ADAM_LR * (m_hat / (jnp.sqrt(v_hat) + ADAM_EPS) + ADAM_WD * w_ref[...])
        mo_ref[...] = m_new
        vo_ref[...] = v_new

    row = pl.BlockSpec((rt, LANES), lambda i: (i, 0))
    return pl.pallas_call(
        body, name=name, grid=(R // rt,), in_specs=[row] * 4, out_specs=[row] * 3,
        out_shape=[jax.ShapeDtypeStruct((R, LANES), f32)] * 3, compiler_params=_cp("parallel"),
    )(g, w, m, v)


def _pack(arrs, dtype):
    flat = jnp.concatenate([a.astype(dtype).reshape(-1) for a in arrs])
    per = FLAT_TILE * LANES
    flat = jnp.pad(flat, (0, (-flat.shape[0]) % per))
    return flat.reshape(-1, LANES)


def _unpack(buf, shapes):
    flat = buf.reshape(-1)
    out, off = [], 0
    for s in shapes:
        n = math.prod(s)
        out.append(flat[off:off + n].reshape(s))
        off += n
    return out


def _block_diag(w):
    n, a, b = w.shape
    eye = jnp.eye(n, dtype=w.dtype)
    return (w[:, :, None, :] * eye[:, None, :, None]).reshape(n * a, n * b)


def _diag_blocks(m, n):
    a, b = m.shape[0] // n, m.shape[1] // n
    idx = jnp.arange(n)
    return m.reshape(n, a, n, b)[idx, :, idx, :]


BIG = ("w_in", "w_out", "w_up", "w_down", "s5_w_glu")
BIG_COL_SHARDED = {"w_in": True, "w_out": False, "w_up": True, "w_down": False, "s5_w_glu": False}
CONV_SHARDED = ("lru_conv_w", "ffn_conv_w")
SMALL = ("lru_conv_b", "lru_wr", "lru_br", "lru_wi", "lru_bi", "lru_lambda", "s5_a_re", "s5_a_im", "s5_b_re",
         "s5_b_im", "s5_c_re", "s5_c_im", "s5_d", "s5_log_step", "s5_b_glu", "mix_norm_g", "ln1_g", "ln1_b",
         "ffn_conv_b", "ln2_g", "ln2_b")
WEIGHTS = ("w_in", "lru_conv_w", "lru_conv_b", "lru_wr", "lru_br", "lru_wi", "lru_bi", "lru_lambda", "s5_a_re",
           "s5_a_im", "s5_b_re", "s5_b_im", "s5_c_re", "s5_c_im", "s5_d", "s5_log_step", "s5_w_glu", "s5_b_glu",
           "mix_norm_g", "w_out", "ln1_g", "ln1_b", "w_up", "ffn_conv_w", "ffn_conv_b", "w_down", "ln2_g", "ln2_b")


def _assemble(slabs, col_sharded):
    _, L, r, c = slabs.shape
    if col_sharded:
        return slabs.transpose(1, 2, 0, 3).reshape(L, r, 4 * c)
    return slabs.transpose(1, 0, 2, 3).reshape(L, 4 * r, c)


def _s5_prepare(p):
    G = N_S5_GROUPS
    bt_re, bt_im = p["s5_b_re"].transpose(0, 2, 1), p["s5_b_im"].transpose(0, 2, 1)
    ls = p["s5_log_step"].reshape(G, 1)
    ab_re, ab_im, bb_re, bb_im = _s5_params(p["s5_a_re"], p["s5_a_im"], ls, bt_re, bt_im)
    ab = jnp.concatenate([ab_re.reshape(1, S5_LANES), ab_im.reshape(1, S5_LANES)], axis=1)
    bbcat = jnp.concatenate([_block_diag(bb_re), _block_diag(bb_im)], axis=1).astype(bf16)
    ccat = jnp.concatenate([_block_diag(p["s5_c_re"].transpose(0, 2, 1)),
                            -_block_diag(p["s5_c_im"].transpose(0, 2, 1))], axis=0).astype(bf16)
    bbcat_pad = jnp.concatenate([bbcat, jnp.zeros((LRU_WIDTH - S5_WIDTH, 2 * S5_LANES), bf16)], axis=0)
    return dict(bt_re=bt_re, bt_im=bt_im, ls=ls, ab=ab, bbcat=bbcat, bbcat_pad=bbcat_pad, ccat=ccat)


def _layer_fwd(h, p, cos, sin):
    sv = {"h": h}
    proj = _mm(h, p["w_in"], "nn", "mm_proj", tn=768)
    sv.update(proj=proj)
    qr, kr = _rope_fwd(proj, cos, sin)
    attn, ltot = _attn_fwd2(qr, kr, proj)
    sv.update(qr=qr, kr=kr, attn=attn, ltot=ltot)
    wr, wi = _block_diag(p["lru_wr"]).astype(bf16), _block_diag(p["lru_wi"]).astype(bf16)
    xc, r, i, log_a, u = _lru_pre(proj, p["lru_conv_w"], p["lru_conv_b"], wr, p["lru_br"], wi, p["lru_bi"],
                                  p["lru_lambda"])
    hl, lru = _lru_scan(log_a, u, proj)
    sv.update(wr=wr, wi=wi, xc=xc, r=r, i=i, log_a=log_a, hl=hl, lru=lru)
    s5 = _s5_prepare(p)
    bu = _mm(proj, s5["bbcat_pad"], "nn", "mm_s5_bu", a_win=(PROJ_S5_U, LRU_WIDTH))
    xs = _s5_scan(bu, s5["ab"])
    yc = _mm(xs, s5["ccat"], "nn", "mm_s5_y")
    ssm, y = _s5_out(yc, proj, p["s5_d"].reshape(-1), p["s5_w_glu"], p["s5_b_glu"])
    sv.update(s5=s5, xs=xs, y=y, ssm=ssm)
    mixed = _mixnorm(attn, lru, ssm, p["mix_norm_g"])
    mix = _mm(mixed, p["w_out"], "nn", "mm_out")
    h1, z1 = _ln_fwd(h, mix, p["ln1_g"], p["ln1_b"], "ln_fwd")
    sv.update(mixed=mixed, z1=z1, h1=h1)
    up = _mm(h1, p["w_up"], "nn", "mm_up", tn=1536)
    act = _ffn_act(up, p["ffn_conv_w"], p["ffn_conv_b"])
    ffn = _mm(act, p["w_down"], "nn", "mm_down")
    h2, z2 = _ln_fwd(h1, ffn, p["ln2_g"], p["ln2_b"], "ln_fwd")
    sv.update(up=up, act=act, z2=z2)
    return h2, sv


def _layer_bwd(dy_a, dy_b, p, sv, cos, sin):
    gr = {}
    dz2, acc = _ln_bwd(dy_a, dy_b, sv["z2"], p["ln2_g"], "ln_bwd_top" if dy_a is None else "ln_bwd")
    gr["ln2_g"], gr["ln2_b"] = acc[0], acc[1]
    dact = _mm(dz2, p["w_down"], "nt", "mm_dact")
    gr["w_down"] = _mm(sv["act"], dz2, "tn", "mm_dw_down")
    dupc = _ffn_act_bwd(dact, sv["up"], p["ffn_conv_w"], p["ffn_conv_b"])
    dup, acc = _conv_bwd(dupc, sv["up"], p["ffn_conv_w"], "ffn_conv_bwd", col_tile=FFN_COL_TILE, out_dtype=bf16)
    gr["ffn_conv_w"], gr["ffn_conv_b"] = acc[0:3], acc[3]
    dh1 = _mm(dup, p["w_up"], "nt", "mm_dh1", tk=2048)
    gr["w_up"] = _mm(sv["h1"], dup, "tn", "mm_dw_up", tn=1536)
    dz1, acc = _ln_bwd(dz2, dh1, sv["z1"], p["ln1_g"], "ln_bwd")
    gr["ln1_g"], gr["ln1_b"] = acc[0], acc[1]
    dmixed = _mm(dz1, p["w_out"], "nt", "mm_dmixed")
    gr["w_out"] = _mm(sv["mixed"], dz1, "tn", "mm_dw_out")
    dattn, dlru, dssm, delta, acc = _mixnorm_bwd(dmixed, sv["attn"], sv["lru"], sv["ssm"], p["mix_norm_g"])
    gr["mix_norm_g"] = acc[0]
    proj = sv["proj"]
    dqr, dkr, dv = _attn_bwd2(sv["qr"], sv["kr"], proj, dattn, sv["ltot"], delta)
    dq, dk = _rope_bwd(dqr, dkr, cos, sin)
    g, dgate = _lru_scan_bwd(dlru, proj, sv["hl"], sv["log_a"])
    dxc, dwr, dwi, acc = _lru_gate_bwd(g, sv["hl"], sv["xc"], sv["r"], sv["i"], sv["log_a"], sv["wr"], sv["wi"],
                                       p["lru_lambda"])
    gr["lru_wr"], gr["lru_wi"] = _diag_blocks(dwr, N_LRU_HEADS), _diag_blocks(dwi, N_LRU_HEADS)
    gr["lru_br"], gr["lru_bi"], gr["lru_lambda"] = acc[0], acc[1], acc[2]
    dxr, acc = _conv_bwd(dxc, proj, p["lru_conv_w"], "lru_conv_bwd", x_col_block=PROJ_LRU_X)
    gr["lru_conv_w"], gr["lru_conv_b"] = acc[0:4], acc[4]
    s5 = sv["s5"]
    G = N_S5_GROUPS
    dy, du_direct, dwglu, acc = _s5_out_bwd(dssm, sv["y"], proj, p["s5_d"].reshape(-1), p["s5_w_glu"],
                                            p["s5_b_glu"])
    gr["s5_w_glu"], gr["s5_b_glu"], gr["s5_d"] = dwglu, acc[0], acc[1].reshape(G, S5_GROUP)
    dxs = _mm(dy, s5["ccat"], "nt", "mm_s5_dx")
    dccat = _mm(sv["xs"], dy, "tn", "mm_s5_dc")
    gr["s5_c_re"] = _diag_blocks(dccat[:S5_LANES], G).transpose(0, 2, 1)
    gr["s5_c_im"] = -_diag_blocks(dccat[S5_LANES:], G).transpose(0, 2, 1)
    gs, dab = _s5_scan_bwd(dxs, sv["xs"], s5["ab"])
    du = _mm(gs, s5["bbcat"], "nt", "mm_s5_du", add=du_direct)
    dbbcat = _mm(proj, gs, "tn", "mm_s5_dbb", a_win=(PROJ_S5_U, LRU_WIDTH))[:S5_WIDTH]
    d_ar, d_ai, d_ls, d_btr, d_bti = _s5_params_bwd(
        p["s5_a_re"], p["s5_a_im"], s5["ls"], s5["bt_re"], s5["bt_im"],
        dab[:, :S5_LANES].reshape(G, S5_STATE), dab[:, S5_LANES:].reshape(G, S5_STATE),
        _diag_blocks(dbbcat[:, :S5_LANES], G), _diag_blocks(dbbcat[:, S5_LANES:], G))
    gr["s5_a_re"], gr["s5_a_im"], gr["s5_log_step"] = d_ar, d_ai, d_ls.reshape(G)
    gr["s5_b_re"], gr["s5_b_im"] = d_btr.transpose(0, 2, 1), d_bti.transpose(0, 2, 1)
    pad = jnp.zeros((du.shape[0], D_IN_PAD - D_IN), f32)
    dproj = jnp.concatenate([dq, dk, dv, dxr, dgate, du, pad], axis=1).astype(bf16)
    gr["w_in"] = _mm(sv["h"], dproj, "tn", "mm_dw_in", tn=768)[:, :D_IN]
    dh = _mm(dproj, p["w_in"], "nt", "mm_dh")
    return dz1, dh, gr


def _train_step(d):
    x, target = d["x"][0], d["loss_target"][0]
    S = x.shape[0]
    me = 2 * lax.axis_index("x") + lax.axis_index("y")

    def rows2d(a):
        return a.reshape(a.shape[0] * a.shape[1], a.shape[2])

    shards = [rows2d(d[n]).astype(bf16) for n in BIG] + [rows2d(d[n]) for n in CONV_SHARDED]
    gathered = _gather_weights(shards, [True] * len(BIG) + [False] * len(CONV_SHARDED))
    full = {}
    for n, g, mine in zip(BIG + CONV_SHARDED, gathered, shards):
        g = lax.dynamic_update_slice_in_dim(g, mine[None], me, axis=0)
        full[n] = _assemble(g.reshape((4,) + d[n].shape), BIG_COL_SHARDED.get(n, True))
    full["w_in"] = jnp.pad(full["w_in"], ((0, 0), (0, 0), (0, D_IN_PAD - D_IN)))
    params = [{n: (full[n][l] if n in full else d[n][l]) for n in WEIGHTS} for l in range(DEPTH)]

    cos, sin = _rope_tables(S)
    h, saved = x, []
    for l in range(DEPTH):
        h, sv = _layer_fwd(h, params[l], cos, sin)
        saved.append(sv)
    dy, loss_acc = _loss_head(h, target)
    loss = lax.psum(loss_acc[0, 0], ("x", "y", "c"))
    da, db, grads = None, dy, [None] * DEPTH
    for l in reversed(range(DEPTH)):
        da, db, grads[l] = _layer_bwd(da, db, params[l], saved[l], cos, sin)
    out = {"loss": loss, "grad_x": _axpy(da, db, "grad_x")[None]}
    G = {n: jnp.stack([grads[l][n] for l in range(DEPTH)]) for n in WEIGHTS}

    others = _pair_send([G[n] for n in BIG])
    chip = [_pair_sum(G[n], o, "pair_sum_" + n) for n, o in zip(BIG, others)]
    slabs = [p.reshape(p.shape[0], 4, p.shape[1] // 4).transpose(1, 0, 2) if BIG_COL_SHARDED[n]
             else p.reshape(4, p.shape[0] // 4, p.shape[1]) for n, p in zip(BIG, chip)]
    recv = _chip_exchange(slabs)
    mine = [_sum_chips(r, s, "sum_chips_" + n) for n, r, s in zip(BIG, recv, slabs)]
    theirs = _pair_swap(mine)
    for n, a, b in zip(BIG, mine, theirs):
        upd = _adamw_layers(a, b, d[n], d["m_" + n], d["v_" + n], "adamw_" + n)
        for pre, u in zip(("grad_", "delta_", "new_m_", "new_v_"), upd):
            out[pre + n] = u

    small = SMALL + CONV_SHARDED
    sp = _pack([G[n] for n in small], f32)
    chip_sum = _add2(sp, _pair_swap([sp])[0], "pair_sum_small")
    total = _sum_chips(_chip_gather(chip_sum), chip_sum, "sum_chips_small")
    gs = dict(zip(small, _unpack(total, [G[n].shape for n in small])))
    for n in CONV_SHARDED:
        L, K, C = gs[n].shape
        gs[n] = lax.dynamic_index_in_dim(gs[n].reshape(L, K, 4, C // 4), me, axis=2, keepdims=False)
    small_shapes = [d[n].shape for n in small]
    gsmall = _pack([gs[n] for n in small], f32)
    upd = _adamw(gsmall, _pack([d[n] for n in small], f32), _pack([d["m_" + n] for n in small], f32),
                 _pack([d["v_" + n] for n in small], f32), "adamw_small")
    for pre, buf in zip(("grad_", "delta_", "new_m_", "new_v_"), (gsmall,) + tuple(upd)):
        for n, a in zip(small, _unpack(buf, small_shapes)):
            out[pre + n] = a

    return (out["loss"], out["grad_x"]) + tuple(out[pre + n] for pre in ("grad_", "delta_", "new_m_", "new_v_")
                                                for n in WEIGHTS)


def kernel(
        x, w_in, lru_conv_w, lru_conv_b, lru_wr, lru_br, lru_wi, lru_bi, lru_lambda, s5_a_re, s5_a_im, s5_b_re,
        s5_b_im, s5_c_re, s5_c_im, s5_d, s5_log_step, s5_w_glu, s5_b_glu, mix_norm_g, w_out, ln1_g, ln1_b, w_up,
        ffn_conv_w, ffn_conv_b, w_down, ln2_g, ln2_b, loss_target, m_w_in, m_lru_conv_w, m_lru_conv_b, m_lru_wr,
        m_lru_br, m_lru_wi, m_lru_bi, m_lru_lambda, m_s5_a_re, m_s5_a_im, m_s5_b_re, m_s5_b_im, m_s5_c_re,
        m_s5_c_im, m_s5_d, m_s5_log_step, m_s5_w_glu, m_s5_b_glu, m_mix_norm_g, m_w_out, m_ln1_g, m_ln1_b,
        m_w_up, m_ffn_conv_w, m_ffn_conv_b, m_w_down, m_ln2_g, m_ln2_b, v_w_in, v_lru_conv_w, v_lru_conv_b,
        v_lru_wr, v_lru_br, v_lru_wi, v_lru_bi, v_lru_lambda, v_s5_a_re, v_s5_a_im, v_s5_b_re, v_s5_b_im,
        v_s5_c_re, v_s5_c_im, v_s5_d, v_s5_log_step, v_s5_w_glu, v_s5_b_glu, v_mix_norm_g, v_w_out, v_ln1_g,
        v_ln1_b, v_w_up, v_ffn_conv_w, v_ffn_conv_b, v_w_down, v_ln2_g, v_ln2_b
):
    return _train_step(dict(locals()))
```

```python
import functools
import math

import jax
import jax.numpy as jnp
from jax import lax
from jax.experimental import pallas as pl
from jax.experimental.pallas import tpu as pltpu

f32 = jnp.float32
bf16 = jnp.bfloat16
MESH = pl.DeviceIdType.MESH

D_MODEL = 1024
ATTN_WIDTH = 384
LRU_WIDTH = 384
S5_WIDTH = 256
HEAD_DIM = 64
N_LRU_HEADS = 6
N_S5_GROUPS = 16
S5_GROUP = 16
S5_STATE = 64
S5_LANES = N_S5_GROUPS * S5_STATE
D_FF = 3072
D_IN = 2176
LRU_C = 8.0
ROPE_THETA = 10000.0
DILATIONS = (1, 4, 16)
ATTN_BLOCK = 128
DEPTH = 2
ALPHA = (2 * DEPTH) ** 0.25
LN_EPS = 1e-5
RMS_EPS = 1e-6
ADAM_LR, ADAM_B1, ADAM_B2, ADAM_EPS, ADAM_WD, ADAM_STEP = 0.001, 0.9, 0.999, 1e-08, 0.01, 10

SUBLANES = 8
LANES = 128
VMEM_LIMIT = 56 * 1024 * 1024
ROW_TILE = 512
MM_SINGLE_K = 3072
D_IN_PAD = 2304
NEG = -1e30


def _cp(*sem):
    return pltpu.CompilerParams(dimension_semantics=sem if sem else None, vmem_limit_bytes=VMEM_LIMIT)


def _pick(dim, pref, align=LANES):
    if dim <= pref:
        return dim
    t = (pref // align) * align
    while t >= align:
        if dim % t == 0:
            return t
        t -= align
    return dim


def _gelu(x):
    return jax.nn.gelu(x)


def _gelu_grad(x):
    c = math.sqrt(2.0 / math.pi)
    t = jnp.tanh(c * (x + 0.044715 * x * x * x))
    return 0.5 * (1.0 + t) + 0.5 * x * (1.0 - t * t) * c * (1.0 + 3 * 0.044715 * x * x)


def _gelu_pair(x):
    c = math.sqrt(2.0 / math.pi)
    x2 = x * x
    t = jnp.tanh(c * x * (1.0 + 0.044715 * x2))
    return 0.5 * x * (1.0 + t), 0.5 * (1.0 + t) + 0.5 * x * (1.0 - t * t) * c * (1.0 + 3 * 0.044715 * x2)


def _sigmoid(x):
    return jax.nn.sigmoid(x)


def _expm1(x):
    p = 1.0 + x / 9.0
    for n in (8.0, 7.0, 6.0, 5.0, 4.0, 3.0, 2.0):
        p = 1.0 + (x / n) * p
    return jnp.where(jnp.abs(x) < 0.3, x * p, jnp.exp(x) - 1.0)


def _dot(a, b, dims):
    return lax.dot_general(a, b, (dims, ((), ())), preferred_element_type=f32)


NN = ((1,), (0,))
NT = ((1,), (1,))
TN = ((0,), (0,))


def _mm(a, b, mode, name, out_dtype=f32, tm=1024, tn=1024, tk=1024, add=None, a_win=None):
    if mode == "nn":
        (M, K), N = a.shape, b.shape[1]
    elif mode == "nt":
        (M, K), N = a.shape, b.shape[0]
    else:
        (K, M), N = a.shape, b.shape[1]
    win = 0
    if a_win is not None:
        win, w = a_win
        if mode == "tn":
            M, tm = w, w
        else:
            K = w
    single = mode != "tn" and K <= MM_SINGLE_K
    tm, tn = _pick(M, tm), _pick(N, tn)
    tk = K if single else _pick(K, tk)
    nk = K // tk
    dims = {"nn": NN, "nt": NT, "tn": TN}[mode]

    def body(a_ref, b_ref, *rest):
        prod = _dot(a_ref[...].astype(bf16), b_ref[...].astype(bf16), dims)
        if single:
            o_ref = rest[-1]
            o_ref[...] = (prod if add is None else prod + rest[0][...]).astype(o_ref.dtype)
            return
        o_ref, acc = rest[-2:]
        k = pl.program_id(2)

        @pl.when(k == 0)
        def _():
            acc[...] = prod if add is None else prod + rest[0][...]

        @pl.when(k > 0)
        def _():
            acc[...] += prod

        @pl.when(k == nk - 1)
        def _():
            o_ref[...] = acc[...].astype(o_ref.dtype)

    if mode == "tn":
        a_spec = pl.BlockSpec((tk, tm), lambda i, j, k: (k, i + win))
    else:
        a_spec = pl.BlockSpec((tm, tk), lambda i, j, k: (i, k + win))
    if mode == "nt":
        b_spec = pl.BlockSpec((tn, tk), lambda i, j, k: (j, k))
    else:
        b_spec = pl.BlockSpec((tk, tn), lambda i, j, k: (k, j))
    o_spec = pl.BlockSpec((tm, tn), lambda i, j, k: (i, j))
    return pl.pallas_call(
        body, name=name, grid=(M // tm, N // tn, nk),
        in_specs=[a_spec, b_spec] + ([] if add is None else [o_spec]), out_specs=o_spec,
        out_shape=jax.ShapeDtypeStruct((M, N), out_dtype),
        scratch_shapes=[] if single else [pltpu.VMEM((tm, tn), f32)],
        compiler_params=_cp("parallel", "parallel", "arbitrary"),
    )(*((a, b) if add is None else (a, b, add)))


def _shift_down(cur, prev8, k):
    if k == 0:
        return cur
    ext = jnp.concatenate([prev8, cur], axis=0)
    return pltpu.roll(ext, k, 0)[SUBLANES:]


def _shift_up(cur, next8, k):
    if k == 0:
        return cur
    n = cur.shape[0] + SUBLANES
    ext = jnp.concatenate([cur, next8], axis=0)
    return pltpu.roll(ext, n - k, 0)[: cur.shape[0]]


def _prev_halo_spec(rt, cols, ncolblk_fn):
    per = rt // SUBLANES
    return pl.BlockSpec((SUBLANES, cols), lambda *g: (jnp.maximum(g[-1] * per - 1, 0), ncolblk_fn(*g)))


def _next_halo_spec(rt, cols, nrows, ncolblk_fn):
    per = rt // SUBLANES
    last = nrows // SUBLANES - 1
    return pl.BlockSpec((SUBLANES, cols), lambda *g: (jnp.minimum((g[-1] + 1) * per, last), ncolblk_fn(*g)))


def _ln_fwd(h, branch, g, b, name):
    S, D = h.shape
    rt = _pick(S, ROW_TILE, SUBLANES)

    def body(h_ref, m_ref, g_ref, b_ref, o_ref, z_ref):
        z = ALPHA * h_ref[...] + m_ref[...]
        mu = jnp.mean(z, axis=-1, keepdims=True)
        zc = z - mu
        var = jnp.mean(zc * zc, axis=-1, keepdims=True)
        o_ref[...] = zc * lax.rsqrt(var + LN_EPS) * g_ref[...] + b_ref[...]
        z_ref[...] = z

    row = pl.BlockSpec((rt, D), lambda i: (i, 0))
    vec = pl.BlockSpec((1, D), lambda i: (0, 0))
    return pl.pallas_call(
        body, name=name, grid=(S // rt,), in_specs=[row, row, vec, vec], out_specs=[row, row],
        out_shape=[jax.ShapeDtypeStruct((S, D), f32)] * 2, compiler_params=_cp("parallel"),
    )(h, branch, g.reshape(1, D), b.reshape(1, D))


def _ln_bwd(dy_a, dy_b, z, g, name):
    S, D = z.shape
    rt = _pick(S, ROW_TILE, SUBLANES)
    two = dy_a is not None

    def body(*refs):
        if two:
            a_ref, b_ref, z_ref, g_ref, dz_ref, acc_ref = refs
            dy = ALPHA * a_ref[...] + b_ref[...]
        else:
            b_ref, z_ref, g_ref, dz_ref, acc_ref = refs
            dy = b_ref[...]
        z = z_ref[...]
        mu = jnp.mean(z, axis=-1, keepdims=True)
        zc = z - mu
        var = jnp.mean(zc * zc, axis=-1, keepdims=True)
        rstd = lax.rsqrt(var + LN_EPS)
        xhat = zc * rstd
        dxh = dy * g_ref[...]
        m1 = jnp.mean(dxh, axis=-1, keepdims=True)
        m2 = jnp.mean(dxh * xhat, axis=-1, keepdims=True)
        dz_ref[...] = rstd * (dxh - m1 - xhat * m2)

        @pl.when(pl.program_id(0) == 0)
        def _():
            acc_ref[...] = jnp.zeros_like(acc_ref)

        acc_ref[0:1, :] += jnp.sum(dy * xhat, axis=0, keepdims=True)
        acc_ref[1:2, :] += jnp.sum(dy, axis=0, keepdims=True)

    row = pl.BlockSpec((rt, D), lambda i: (i, 0))
    vec = pl.BlockSpec((1, D), lambda i: (0, 0))
    acc = pl.BlockSpec((SUBLANES, D), lambda i: (0, 0))
    ins = ([dy_a] if two else []) + [dy_b, z, g.reshape(1, D)]
    return pl.pallas_call(
        body, name=name, grid=(S // rt,), in_specs=[row] * (len(ins) - 1) + [vec], out_specs=[row, acc],
        out_shape=[jax.ShapeDtypeStruct((S, D), f32), jax.ShapeDtypeStruct((SUBLANES, D), f32)],
        compiler_params=_cp("arbitrary"),
    )(*ins)


def _loss_head(y, target):
    S, D = y.shape
    rt = _pick(S, ROW_TILE, SUBLANES)

    def body(y_ref, t_ref, dy_ref, acc_ref):
        e = y_ref[...] - t_ref[...]
        dy_ref[...] = e * (1.0 / D)

        @pl.when(pl.program_id(0) == 0)
        def _():
            acc_ref[...] = jnp.zeros_like(acc_ref)

        part = jnp.sum(jnp.mean(e * e, axis=-1, keepdims=True), axis=0, keepdims=True)
        acc_ref[...] += 0.5 * part

    row = pl.BlockSpec((rt, D), lambda i: (i, 0))
    return pl.pallas_call(
        body, name="loss_head", grid=(S // rt,), in_specs=[row, row],
        out_specs=[row, pl.BlockSpec((1, 1), lambda i: (0, 0))],
        out_shape=[jax.ShapeDtypeStruct((S, D), f32), jax.ShapeDtypeStruct((1, 1), f32)],
        compiler_params=_cp("arbitrary"),
    )(y, target)


def _axpy(a, b, name):
    S, D = a.shape
    rt = _pick(S, ROW_TILE, SUBLANES)

    def body(a_ref, b_ref, o_ref):
        o_ref[...] = ALPHA * a_ref[...] + b_ref[...]

    row = pl.BlockSpec((rt, D), lambda i: (i, 0))
    return pl.pallas_call(
        body, name=name, grid=(S // rt,), in_specs=[row, row], out_specs=row,
        out_shape=jax.ShapeDtypeStruct((S, D), f32), compiler_params=_cp("parallel"),
    )(a, b)


def _rope_tables(S):
    rt = _pick(S, ROW_TILE, SUBLANES)

    def body(c_ref, s_ref):
        pos = (pl.program_id(0) * rt + lax.broadcasted_iota(jnp.int32, (rt, LANES), 0)).astype(f32)
        lane = lax.broadcasted_iota(jnp.int32, (rt, LANES), 1)
        j = (lane % (HEAD_DIM // 2)).astype(f32)
        inv = jnp.exp((-j * 2.0 / HEAD_DIM) * math.log(ROPE_THETA))
        ang = pos * inv
        c = jnp.cos(ang)
        s = jnp.where(lane % HEAD_DIM < HEAD_DIM // 2, -jnp.sin(ang), jnp.sin(ang))
        c_ref[...] = jnp.concatenate([c, c, c], axis=1)
        s_ref[...] = jnp.concatenate([s, s, s], axis=1)

    row = pl.BlockSpec((rt, ATTN_WIDTH), lambda i: (i, 0))
    return pl.pallas_call(
        body, name="rope_tables", grid=(S // rt,), in_specs=[], out_specs=[row, row],
        out_shape=[jax.ShapeDtypeStruct((S, ATTN_WIDTH), f32)] * 2, compiler_params=_cp("parallel"),
    )()


def _swap_halves(x):
    lane = lax.broadcasted_iota(jnp.int32, x.shape, 1)
    half = HEAD_DIM // 2
    return jnp.where(lane % HEAD_DIM < half, pltpu.roll(x, x.shape[1] - half, 1), pltpu.roll(x, half, 1))


def _rope_fwd(proj, cos, sin):
    S, W = proj.shape[0], ATTN_WIDTH
    rt = _pick(S, ROW_TILE, SUBLANES)

    def body(q_ref, k_ref, c_ref, s_ref, qo_ref, ko_ref):
        c, s = c_ref[...], s_ref[...]
        qo_ref[...] = q_ref[...] * c + _swap_halves(q_ref[...]) * s
        ko_ref[...] = k_ref[...] * c + _swap_halves(k_ref[...]) * s

    row = pl.BlockSpec((rt, W), lambda i: (i, 0))
    return pl.pallas_call(
        body, name="rope_fwd", grid=(S // rt,), in_specs=[row, pl.BlockSpec((rt, W), lambda i: (i, 1)), row, row],
        out_specs=[row, row], out_shape=[jax.ShapeDtypeStruct((S, W), f32)] * 2, compiler_params=_cp("parallel"),
    )(proj, proj, cos, sin)


def _rope_bwd(dq, dk, cos, sin):
    S, W = dq.shape
    rt = _pick(S, ROW_TILE, SUBLANES)

    def body(q_ref, k_ref, c_ref, s_ref, qo_ref, ko_ref):
        c, s = c_ref[...], s_ref[...]
        qo_ref[...] = q_ref[...] * c + _swap_halves(q_ref[...] * s)
        ko_ref[...] = k_ref[...] * c + _swap_halves(k_ref[...] * s)

    row = pl.BlockSpec((rt, W), lambda i: (i, 0))
    return pl.pallas_call(
        body, name="rope_bwd", grid=(S // rt,), in_specs=[row] * 4, out_specs=[row] * 2,
        out_shape=[jax.ShapeDtypeStruct((S, W), f32)] * 2, compiler_params=_cp("parallel"),
    )(dq, dk, cos, sin)


def _rows(ref, start, d):
    if d == 1:
        return ref[pl.ds(pl.multiple_of(start, ATTN_BLOCK), ATTN_BLOCK), :]
    return ref[pl.ds(start, ATTN_BLOCK, stride=d), :]


def _set_rows(ref, start, d, val):
    if d == 1:
        ref[pl.ds(pl.multiple_of(start, ATTN_BLOCK), ATTN_BLOCK), :] = val
    else:
        ref[pl.ds(start, ATTN_BLOCK, stride=d), :] = val


def _pair_spec(S, first_block):
    return pl.BlockSpec((S, LANES), lambda p: (0, p + first_block))


def _attn_fwd2(qr, kr, proj, shards=(), split=()):
    S = qr.shape[0]
    B = ATTN_BLOCK
    nb = S // B
    scale = HEAD_DIM ** -0.5

    gather = _Gather(shards, split)
    nt = gather.nt

    def body(*refs):
        q_ref, k_ref, v_ref = refs[:3]
        g_ins = refs[3:3 + nt]
        o_ref, l_ref = refs[3 + nt:5 + nt]
        g_outs = refs[5 + nt:5 + 2 * nt]
        m_s, l_s = refs[5 + 2 * nt:7 + 2 * nt]
        g_sems = refs[7 + 2 * nt:]
        if nt:
            @pl.when(pl.program_id(0) == 0)
            def _():
                gather.start(g_ins, g_outs, g_sems)

        qi = lax.broadcasted_iota(jnp.int32, (B, 2 * B), 0)
        ki = lax.broadcasted_iota(jnp.int32, (B, 2 * B), 1)
        dist = qi + B - ki
        band = (dist >= 0) & (dist <= B)
        for bi, d in enumerate(DILATIONS):
            bpc = nb // d

            def blk(b, carry, bi=bi, d=d, bpc=bpc):
                c, n = b // bpc, b % bpc
                start = c + d * B * n
                pstart = c + d * B * jnp.maximum(n - 1, 0)
                valid = band & ((ki >= B) | (n > 0))
                q = _rows(q_ref, start, d).astype(bf16)
                kcat = jnp.concatenate([_rows(k_ref, pstart, d), _rows(k_ref, start, d)], axis=0).astype(bf16)
                vcat = jnp.concatenate([_rows(v_ref, pstart, d), _rows(v_ref, start, d)], axis=0).astype(bf16)
                if bi > 0:
                    m_old, l_old, a_old = _rows(m_s, start, d), _rows(l_s, start, d), _rows(o_ref, start, d)
                ms, ls, accs = [], [], []
                for h in range(2):
                    sl = slice(h * HEAD_DIM, (h + 1) * HEAD_DIM)
                    c0 = h * HEAD_DIM
                    s = jnp.where(valid, _dot(q[:, sl], kcat[:, sl], NT) * scale, NEG)
                    m = jnp.max(s, axis=1, keepdims=True)
                    if bi > 0:
                        mo = m_old[:, c0:c0 + 1]
                        m = jnp.maximum(m, mo)
                        alpha = jnp.exp(mo - m)
                    p = jnp.exp(s - m)
                    l = jnp.sum(p, axis=1, keepdims=True)
                    acc = _dot(p.astype(bf16), vcat[:, sl], NN)
                    if bi > 0:
                        l = l + alpha * l_old[:, c0:c0 + 1]
                        acc = acc + alpha * a_old[:, sl]
                    ms.append(jnp.broadcast_to(m, (B, HEAD_DIM)))
                    ls.append(jnp.broadcast_to(l, (B, HEAD_DIM)))
                    accs.append(acc)
                _set_rows(m_s, start, d, jnp.concatenate(ms, axis=1))
                _set_rows(l_s, start, d, jnp.concatenate(ls, axis=1))
                _set_rows(o_ref, start, d, jnp.concatenate(accs, axis=1))
                return carry

            lax.fori_loop(0, nb, blk, 0, unroll=4)

        def fin(t, carry):
            rows = pl.ds(pl.multiple_of(t * B, B), B)
            l = l_s[rows, :]
            o_ref[rows, :] = o_ref[rows, :] / l
            l_ref[rows, :] = m_s[rows, :] + jnp.log(l)
            return carry

        lax.fori_loop(0, nb, fin, 0)
        if nt:
            @pl.when(pl.program_id(0) == pl.num_programs(0) - 1)
            def _():
                gather.finish(g_ins, g_outs, g_sems)

    pair = _pair_spec(S, 0)
    res = pl.pallas_call(
        body, name="attn_fwd_gather" if nt else "attn_fwd", grid=(3,),
        in_specs=[pair, pair, _pair_spec(S, 2 * ATTN_WIDTH // LANES)] + gather.in_specs,
        out_specs=[pair, pair] + gather.out_specs,
        out_shape=[jax.ShapeDtypeStruct((S, ATTN_WIDTH), f32)] * 2 + gather.out_shape,
        scratch_shapes=[pltpu.VMEM((S, LANES), f32)] * 2 + gather.scratch,
        compiler_params=_cp("arbitrary"),
    )(qr, kr, proj, *shards)
    return res[0], res[1], list(res[2:])


def _attn_bwd2(qr, kr, proj, dattn, ltot, delta):
    S = qr.shape[0]
    B = ATTN_BLOCK
    nb = S // B
    scale = HEAD_DIM ** -0.5

    def body(q_ref, k_ref, v_ref, do_ref, l_ref, d_ref, dq_ref, dk_ref, dv_ref):
        qi = lax.broadcasted_iota(jnp.int32, (B, 2 * B), 0)
        ki = lax.broadcasted_iota(jnp.int32, (B, 2 * B), 1)
        dist1 = qi + B - ki
        band1 = (dist1 >= 0) & (dist1 <= B)
        ri = lax.broadcasted_iota(jnp.int32, (2 * B, B), 0)
        ci = lax.broadcasted_iota(jnp.int32, (2 * B, B), 1)
        dist2 = ri - ci
        band2 = (dist2 >= 0) & (dist2 <= B)
        for bi, d in enumerate(DILATIONS):
            bpc = nb // d

            def blk(b, carry, bi=bi, d=d, bpc=bpc):
                c, n = b // bpc, b % bpc
                start = c + d * B * n
                pstart = c + d * B * jnp.maximum(n - 1, 0)
                nstart = c + d * B * jnp.minimum(n + 1, bpc - 1)
                valid1 = band1 & ((ki >= B) | (n > 0))
                valid2 = band2 & ((ri < B) | (n + 1 < bpc))
                q_c, q_n = _rows(q_ref, start, d), _rows(q_ref, nstart, d)
                k_p, k_c = _rows(k_ref, pstart, d), _rows(k_ref, start, d)
                v_p, v_c = _rows(v_ref, pstart, d), _rows(v_ref, start, d)
                do_c, do_n = _rows(do_ref, start, d), _rows(do_ref, nstart, d)
                l_c, l_n = _rows(l_ref, start, d), _rows(l_ref, nstart, d)
                d_c, d_n = _rows(d_ref, start, d), _rows(d_ref, nstart, d)
                qc = q_c.astype(bf16)
                qcat = jnp.concatenate([q_c, q_n], axis=0).astype(bf16)
                kc = k_c.astype(bf16)
                kcat = jnp.concatenate([k_p, k_c], axis=0).astype(bf16)
                vc = v_c.astype(bf16)
                vcat = jnp.concatenate([v_p, v_c], axis=0).astype(bf16)
                doc = do_c.astype(bf16)
                docat = jnp.concatenate([do_c, do_n], axis=0).astype(bf16)
                lcat = jnp.concatenate([l_c, l_n], axis=0)
                dcat = jnp.concatenate([d_c, d_n], axis=0)
                dqs, dks, dvs = [], [], []
                for h in range(2):
                    sl = slice(h * HEAD_DIM, (h + 1) * HEAD_DIM)
                    c0 = h * HEAD_DIM
                    s1 = _dot(qc[:, sl], kcat[:, sl], NT) * scale
                    p1 = jnp.where(valid1, jnp.exp(s1 - l_c[:, c0:c0 + 1]), 0.0)
                    dp1 = _dot(doc[:, sl], vcat[:, sl], NT)
                    ds1 = p1 * (dp1 - d_c[:, c0:c0 + 1]) * scale
                    dqs.append(_dot(ds1.astype(bf16), kcat[:, sl], NN))
                    s2 = _dot(qcat[:, sl], kc[:, sl], NT) * scale
                    p2 = jnp.where(valid2, jnp.exp(s2 - lcat[:, c0:c0 + 1]), 0.0)
                    dvs.append(_dot(p2.astype(bf16), docat[:, sl], TN))
                    dp2 = _dot(docat[:, sl], vc[:, sl], NT)
                    ds2 = p2 * (dp2 - dcat[:, c0:c0 + 1]) * scale
                    dks.append(_dot(ds2.astype(bf16), qcat[:, sl], TN))
                for ref, parts in ((dq_ref, dqs), (dk_ref, dks), (dv_ref, dvs)):
                    new = jnp.concatenate(parts, axis=1)
                    if bi > 0:
                        new = new + _rows(ref, start, d)
                    _set_rows(ref, start, d, new)
                return carry

            lax.fori_loop(0, nb, blk, 0, unroll=2)

    pair = _pair_spec(S, 0)
    return pl.pallas_call(
        body, name="attn_bwd", grid=(3,),
        in_specs=[pair, pair, _pair_spec(S, 2 * ATTN_WIDTH // LANES), pair, pair, pair], out_specs=[pair] * 3,
        out_shape=[jax.ShapeDtypeStruct((S, ATTN_WIDTH), f32)] * 3, compiler_params=_cp("parallel"),
    )(qr, kr, proj, dattn, ltot, delta)


def _softplus_neg(lam):
    return jnp.maximum(-lam, 0.0) + jnp.log1p(jnp.exp(-jnp.abs(lam)))


PROJ_LRU_X, PROJ_LRU_GATE, PROJ_S5_U = 3, 4, 5


def _lru_pre(proj, conv_w, conv_b, wr, br, wi, bi, lam):
    S, W = proj.shape[0], LRU_WIDTH
    rt = _pick(S, ROW_TILE, SUBLANES)
    K = conv_w.shape[0]

    def body(x_ref, xp_ref, cw_ref, cb_ref, wr_ref, br_ref, wi_ref, bi_ref, lam_ref,
             xc_ref, r_ref, i_ref, la_ref, u_ref):
        prev = jnp.where(pl.program_id(0) == 0, 0.0, xp_ref[...])
        x = x_ref[...]
        xc = cb_ref[...] + cw_ref[K - 1:K, :] * x
        for k in range(K - 1):
            xc = xc + cw_ref[k:k + 1, :] * _shift_down(x, prev, K - 1 - k)
        xb = xc.astype(bf16)
        r = _sigmoid(_dot(xb, wr_ref[...], NN) + br_ref[...])
        i = _sigmoid(_dot(xb, wi_ref[...], NN) + bi_ref[...])
        log_a = -LRU_C * r * _softplus_neg(lam_ref[...])
        u = jnp.sqrt(-_expm1(2.0 * log_a)) * (i * xc)
        xc_ref[...], r_ref[...], i_ref[...], la_ref[...], u_ref[...] = xc, r, i, log_a, u

    row = pl.BlockSpec((rt, W), lambda i: (i, 0))
    xrow = pl.BlockSpec((rt, W), lambda i: (i, PROJ_LRU_X))
    halo = _prev_halo_spec(rt, W, lambda i: PROJ_LRU_X)
    vec = pl.BlockSpec((1, W), lambda i: (0, 0))
    return pl.pallas_call(
        body, name="lru_pre", grid=(S // rt,),
        in_specs=[xrow, halo, pl.BlockSpec((K, W), lambda i: (0, 0)), vec,
                  pl.BlockSpec((W, W), lambda i: (0, 0)), vec, pl.BlockSpec((W, W), lambda i: (0, 0)), vec, vec],
        out_specs=[row] * 5, out_shape=[jax.ShapeDtypeStruct((S, W), f32)] * 5, compiler_params=_cp("parallel"),
    )(proj, proj, conv_w, conv_b.reshape(1, W), wr, br.reshape(1, W), wi, bi.reshape(1, W), lam.reshape(1, W))


def _tile_rows(shape):
    return lax.broadcasted_iota(jnp.int32, shape, 0)


def _lru_scan(log_a, u, proj):
    S, W = u.shape
    rt = _pick(S, ROW_TILE, SUBLANES)
    T = SUBLANES

    def body(la_ref, u_ref, g_ref, h_ref, o_ref, carry):
        @pl.when(pl.program_id(0) == 0)
        def _():
            carry[...] = jnp.zeros_like(carry)

        row = _tile_rows((T, W))

        def step(t, hp):
            r0 = pl.multiple_of(t * T, T)
            a = jnp.exp(la_ref[pl.ds(r0, T), :])
            x = u_ref[pl.ds(r0, T), :]
            for k in (1, 2, 4):
                x = x + a * jnp.where(row >= k, pltpu.roll(x, k, 0), 0.0)
                a = a * jnp.where(row >= k, pltpu.roll(a, k, 0), 1.0)
            h = x + a * hp
            h_ref[pl.ds(r0, T), :] = h
            o_ref[pl.ds(r0, T), :] = h * _gelu(g_ref[pl.ds(r0, T), :])
            return h[T - 1:T, :]

        carry[0:1, :] = lax.fori_loop(0, rt // T, step, carry[0:1, :])

    row = pl.BlockSpec((rt, W), lambda i: (i, 0))
    grow = pl.BlockSpec((rt, W), lambda i: (i, PROJ_LRU_GATE))
    return pl.pallas_call(
        body, name="lru_scan", grid=(S // rt,), in_specs=[row, row, grow], out_specs=[row] * 2,
        out_shape=[jax.ShapeDtypeStruct((S, W), f32)] * 2, scratch_shapes=[pltpu.VMEM((T, W), f32)],
        compiler_params=_cp("arbitrary"),
    )(log_a, u, proj)


def _lru_scan_bwd(dlru, proj, h, log_a):
    S, W = h.shape
    rt = _pick(S, ROW_TILE, SUBLANES)
    T = SUBLANES
    nblk = S // rt

    def body(d_ref, g_ref, h_ref, la_ref, go_ref, dg_ref, carry):
        @pl.when(pl.program_id(0) == 0)
        def _():
            carry[...] = jnp.zeros_like(carry)

        row = _tile_rows((T, W))

        def step(j, c):
            gn, an = c
            t = rt // T - 1 - j
            r0 = pl.multiple_of(t * T, T)
            d = d_ref[pl.ds(r0, T), :]
            gate = g_ref[pl.ds(r0, T), :]
            a = jnp.exp(la_ref[pl.ds(r0, T), :])
            dg_ref[pl.ds(r0, T), :] = d * h_ref[pl.ds(r0, T), :] * _gelu_grad(gate)
            x = d * _gelu(gate)
            b = jnp.where(row < T - 1, pltpu.roll(a, T - 1, 0), an)
            for k in (1, 2, 4):
                x = x + b * jnp.where(row < T - k, pltpu.roll(x, T - k, 0), 0.0)
                b = b * jnp.where(row < T - k, pltpu.roll(b, T - k, 0), 1.0)
            g = x + b * gn
            go_ref[pl.ds(r0, T), :] = g
            return g[0:1, :], a[0:1, :]

        gn, an = lax.fori_loop(0, rt // T, step, (carry[0:1, :], carry[1:2, :]))
        carry[0:1, :] = gn
        carry[1:2, :] = an

    row = pl.BlockSpec((rt, W), lambda i: (nblk - 1 - i, 0))
    grow = pl.BlockSpec((rt, W), lambda i: (nblk - 1 - i, PROJ_LRU_GATE))
    return pl.pallas_call(
        body, name="lru_scan_bwd", grid=(nblk,), in_specs=[row, grow, row, row], out_specs=[row] * 2,
        out_shape=[jax.ShapeDtypeStruct((S, W), f32)] * 2, scratch_shapes=[pltpu.VMEM((T, W), f32)],
        compiler_params=_cp("arbitrary"),
    )(dlru, proj, h, log_a)


def _lru_gate_bwd(g, h, xc, r, i, log_a, wr, wi, lam):
    S, W = g.shape
    rt = _pick(S, ROW_TILE, SUBLANES)

    def body(g_ref, h_ref, hp_ref, xc_ref, r_ref, i_ref, la_ref, wr_ref, wi_ref, lam_ref,
             dxc_ref, dwr_ref, dwi_ref, acc_ref):
        @pl.when(pl.program_id(0) == 0)
        def _():
            dwr_ref[...] = jnp.zeros_like(dwr_ref)
            dwi_ref[...] = jnp.zeros_like(dwi_ref)
            acc_ref[...] = jnp.zeros_like(acc_ref)

        prev = jnp.where(pl.program_id(0) == 0, 0.0, hp_ref[...])
        gg, xc, r, i, log_a, lam = g_ref[...], xc_ref[...], r_ref[...], i_ref[...], la_ref[...], lam_ref[...]
        hm1 = _shift_down(h_ref[...], prev, 1)
        a = jnp.exp(log_a)
        s = jnp.sqrt(-_expm1(2.0 * log_a))
        da = gg * hm1
        di = gg * s * xc
        dxc = gg * s * i
        ds = gg * i * xc
        dlog_a = da * a - ds * (a * a / s)
        sp = _softplus_neg(lam)
        dr = dlog_a * (-LRU_C * sp)
        dsp = jnp.sum(dlog_a * (-LRU_C * r), axis=0, keepdims=True)
        dpr = dr * r * (1.0 - r)
        dpi = di * i * (1.0 - i)
        dprb, dpib, xb = dpr.astype(bf16), dpi.astype(bf16), xc.astype(bf16)
        dxc_ref[...] = dxc + _dot(dprb, wr_ref[...], NT) + _dot(dpib, wi_ref[...], NT)
        dwr_ref[...] += _dot(xb, dprb, TN)
        dwi_ref[...] += _dot(xb, dpib, TN)
        acc_ref[0:1, :] += jnp.sum(dpr, axis=0, keepdims=True)
        acc_ref[1:2, :] += jnp.sum(dpi, axis=0, keepdims=True)
        acc_ref[2:3, :] += dsp * (-_sigmoid(-lam))

    row = pl.BlockSpec((rt, W), lambda i: (i, 0))
    halo = _prev_halo_spec(rt, W, lambda i: 0)
    vec = pl.BlockSpec((1, W), lambda i: (0, 0))
    mat = pl.BlockSpec((W, W), lambda i: (0, 0))
    acc = pl.BlockSpec((SUBLANES, W), lambda i: (0, 0))
    return pl.pallas_call(
        body, name="lru_gate_bwd", grid=(S // rt,),
        in_specs=[row, row, halo, row, row, row, row, mat, mat, vec], out_specs=[row, mat, mat, acc],
        out_shape=[jax.ShapeDtypeStruct((S, W), f32), jax.ShapeDtypeStruct((W, W), f32),
                   jax.ShapeDtypeStruct((W, W), f32), jax.ShapeDtypeStruct((SUBLANES, W), f32)],
        compiler_params=_cp("arbitrary"),
    )(g, h, h, xc, r, i, log_a, wr, wi, lam.reshape(1, W))


def _conv_bwd(dy, x, conv_w, name, col_tile=None, out_dtype=f32, x_col_block=0):
    if dy.ndim == 2:
        dy = dy[None]
    H, S, Ch = dy.shape
    C = H * Ch
    K = conv_w.shape[0]
    ct = Ch if col_tile is None else col_tile
    nct = Ch // ct
    rt = _pick(S, ROW_TILE, SUBLANES)
    nrt = S // rt

    def body(dy_ref, dyn_ref, x_ref, xp_ref, w_ref, dx_ref, acc_ref):
        i = pl.program_id(2)

        @pl.when(i == 0)
        def _():
            acc_ref[...] = jnp.zeros_like(acc_ref)

        nxt = jnp.where(i == nrt - 1, 0.0, dyn_ref[...])
        prev = jnp.where(i == 0, 0.0, xp_ref[...])
        dy, x = dy_ref[...], x_ref[...]
        dx = w_ref[K - 1:K, :] * dy
        for k in range(K - 1):
            dx = dx + w_ref[k:k + 1, :] * _shift_up(dy, nxt, K - 1 - k)
        dx_ref[...] = dx.astype(dx_ref.dtype)
        for k in range(K):
            acc_ref[k:k + 1, :] += jnp.sum(dy * _shift_down(x, prev, K - 1 - k), axis=0, keepdims=True)
        acc_ref[K:K + 1, :] += jnp.sum(dy, axis=0, keepdims=True)

    per, last = rt // SUBLANES, S // SUBLANES - 1
    dy_row = pl.BlockSpec((None, rt, ct), lambda h, j, i: (h, i, j))
    dy_next = pl.BlockSpec((None, SUBLANES, ct), lambda h, j, i: (h, jnp.minimum((i + 1) * per, last), j))
    row = pl.BlockSpec((rt, ct), lambda h, j, i: (i, h * nct + j))
    xrow = pl.BlockSpec((rt, ct), lambda h, j, i: (i, h * nct + j + x_col_block))
    return pl.pallas_call(
        body, name=name, grid=(H, nct, nrt),
        in_specs=[dy_row, dy_next, xrow, _prev_halo_spec(rt, ct, lambda h, j, i: h * nct + j + x_col_block),
                  pl.BlockSpec((K, ct), lambda h, j, i: (0, h * nct + j))],
        out_specs=[row, pl.BlockSpec((SUBLANES, ct), lambda h, j, i: (0, h * nct + j))],
        out_shape=[jax.ShapeDtypeStruct((S, C), out_dtype), jax.ShapeDtypeStruct((SUBLANES, C), f32)],
        compiler_params=_cp("parallel", "parallel", "arbitrary"),
    )(dy, dy, x, x, conv_w)


def _s5_param_fn(a_re, a_im, ls, bt_re, bt_im):
    step = jnp.exp(ls)
    dt_re, dt_im = step * a_re, step * a_im
    mag = jnp.exp(dt_re)
    ab_re, ab_im = mag * jnp.cos(dt_im), mag * jnp.sin(dt_im)
    z_re, z_im = ab_re - 1.0, ab_im
    den = a_re * a_re + a_im * a_im
    f_re = (z_re * a_re + z_im * a_im) / den
    f_im = (z_im * a_re - z_re * a_im) / den
    bb_re = f_re[:, None, :] * bt_re - f_im[:, None, :] * bt_im
    bb_im = f_re[:, None, :] * bt_im + f_im[:, None, :] * bt_re
    return ab_re, ab_im, bb_re, bb_im


def _s5_params(a_re, a_im, ls, bt_re, bt_im):
    def body(ar, ai, l, br, bi, o_ar, o_ai, o_br, o_bi):
        o_ar[...], o_ai[...], o_br[...], o_bi[...] = _s5_param_fn(ar[...], ai[...], l[...], br[...], bi[...])

    return pl.pallas_call(
        body, name="s5_params",
        out_shape=[jax.ShapeDtypeStruct(a_re.shape, f32)] * 2 + [jax.ShapeDtypeStruct(bt_re.shape, f32)] * 2,
        compiler_params=_cp(),
    )(a_re, a_im, ls, bt_re, bt_im)


def _s5_params_bwd(a_re, a_im, ls, bt_re, bt_im, d_ar, d_ai, d_br, d_bi):
    def body(ar, ai, l, br, bi, c_ar, c_ai, c_br, c_bi, g_ar, g_ai, g_l, g_br, g_bi):
        _, vjp = jax.vjp(_s5_param_fn, ar[...], ai[...], l[...], br[...], bi[...])
        g_ar[...], g_ai[...], g_l[...], g_br[...], g_bi[...] = vjp((c_ar[...], c_ai[...], c_br[...], c_bi[...]))

    return pl.pallas_call(
        body, name="s5_params_bwd",
        out_shape=[jax.ShapeDtypeStruct(a_re.shape, f32)] * 2 + [jax.ShapeDtypeStruct(ls.shape, f32)]
        + [jax.ShapeDtypeStruct(bt_re.shape, f32)] * 2,
        compiler_params=_cp(),
    )(a_re, a_im, ls, bt_re, bt_im, d_ar, d_ai, d_br, d_bi)


S5_CHUNK = 256


def _s5_power_tables(ab_ref, p_ref, w_ref, conj):
    T, L = SUBLANES, S5_LANES
    are = ab_ref[0:1, 0:L]
    aim = ab_ref[0:1, L:2 * L]
    if conj:
        aim = -aim
    pre, pim = are, aim
    for n in range(3):
        p_ref[n:n + 1, 0:L] = pre
        p_ref[n:n + 1, L:2 * L] = pim
        pre, pim = pre * pre - pim * pim, 2.0 * pre * pim
    row = _tile_rows((T, L))
    wre = jnp.zeros((T, L), f32)
    wim = jnp.zeros((T, L), f32)
    pre, pim = are, aim
    for n in range(T):
        tgt = (T - 1 - n) if conj else n
        wre = jnp.where(row == tgt, pre, wre)
        wim = jnp.where(row == tgt, pim, wim)
        pre, pim = pre * are - pim * aim, pre * aim + pim * are
    w_ref[:, 0:L] = wre
    w_ref[:, L:2 * L] = wim


def _s5_scan(bu, ab):
    S, L2 = bu.shape
    L = L2 // 2
    rt = _pick(S, 256, SUBLANES)
    T = SUBLANES
    CH = S5_CHUNK

    def body(bu_ref, ab_ref, x_ref, p_ref, w_ref, carry):
        @pl.when(pl.program_id(0) == 0)
        def _():
            carry[...] = jnp.zeros_like(carry)
            _s5_power_tables(ab_ref, p_ref, w_ref, conj=False)

        row = _tile_rows((T, CH))

        def step(t, _):
            r0 = pl.multiple_of(t * T, T)
            for c in range(L // CH):
                lre, lim = pl.ds(c * CH, CH), pl.ds(L + c * CH, CH)
                xr, xi = bu_ref[pl.ds(r0, T), lre], bu_ref[pl.ds(r0, T), lim]
                for n, k in enumerate((1, 2, 4)):
                    pr, pi = p_ref[n:n + 1, lre], p_ref[n:n + 1, lim]
                    sr = jnp.where(row >= k, pltpu.roll(xr, k, 0), 0.0)
                    si = jnp.where(row >= k, pltpu.roll(xi, k, 0), 0.0)
                    xr, xi = xr + pr * sr - pi * si, xi + pr * si + pi * sr
                cr, ci = carry[T - 1:T, lre], carry[T - 1:T, lim]
                wr, wi = w_ref[:, lre], w_ref[:, lim]
                xr, xi = xr + wr * cr - wi * ci, xi + wr * ci + wi * cr
                carry[:, lre] = xr
                carry[:, lim] = xi
                x_ref[pl.ds(r0, T), lre] = xr
                x_ref[pl.ds(r0, T), lim] = xi
            return 0

        lax.fori_loop(0, rt // T, step, 0)

    row_spec = pl.BlockSpec((rt, L2), lambda i: (i, 0))
    return pl.pallas_call(
        body, name="s5_scan", grid=(S // rt,), in_specs=[row_spec, pl.BlockSpec((1, L2), lambda i: (0, 0))],
        out_specs=row_spec, out_shape=jax.ShapeDtypeStruct((S, L2), f32),
        scratch_shapes=[pltpu.VMEM((T, L2), f32), pltpu.VMEM((T, L2), f32), pltpu.VMEM((T, L2), f32)],
        compiler_params=_cp("arbitrary"),
    )(bu, ab)


def _s5_scan_bwd(dx, x, ab):
    S, L2 = dx.shape
    L = L2 // 2
    rt = _pick(S, 256, SUBLANES)
    T = SUBLANES
    CH = S5_CHUNK
    nblk = S // rt
    per = rt // T

    def body(dx_ref, x_ref, xp_ref, ab_ref, g_ref, da_ref, p_ref, w_ref, carry, acc):
        pid = pl.program_id(0)

        @pl.when(pid == 0)
        def _():
            carry[...] = jnp.zeros_like(carry)
            acc[...] = jnp.zeros_like(acc)
            _s5_power_tables(ab_ref, p_ref, w_ref, conj=True)

        row = _tile_rows((T, CH))
        first_block = pid == nblk - 1

        def step(j, _):
            t = per - 1 - j
            r0 = pl.multiple_of(t * T, T)
            rp = pl.multiple_of(jnp.maximum(t - 1, 0) * T, T)
            for c in range(L // CH):
                lre, lim = pl.ds(c * CH, CH), pl.ds(L + c * CH, CH)
                gr, gi = dx_ref[pl.ds(r0, T), lre], dx_ref[pl.ds(r0, T), lim]
                for n, k in enumerate((1, 2, 4)):
                    pr, pi = p_ref[n:n + 1, lre], p_ref[n:n + 1, lim]
                    sr = jnp.where(row < T - k, pltpu.roll(gr, T - k, 0), 0.0)
                    si = jnp.where(row < T - k, pltpu.roll(gi, T - k, 0), 0.0)
                    gr, gi = gr + pr * sr - pi * si, gi + pr * si + pi * sr
                cr, ci = carry[0:1, lre], carry[0:1, lim]
                wr, wi = w_ref[:, lre], w_ref[:, lim]
                gr, gi = gr + wr * cr - wi * ci, gi + wr * ci + wi * cr
                carry[:, lre] = gr
                carry[:, lim] = gi
                g_ref[pl.ds(r0, T), lre] = gr
                g_ref[pl.ds(r0, T), lim] = gi
                xr, xi = x_ref[pl.ds(r0, T), lre], x_ref[pl.ds(r0, T), lim]
                in_blk_r, in_blk_i = x_ref[pl.ds(rp, T), lre], x_ref[pl.ds(rp, T), lim]
                hal_r = jnp.where(first_block, 0.0, xp_ref[:, lre])
                hal_i = jnp.where(first_block, 0.0, xp_ref[:, lim])
                pvr = jnp.where(t == 0, hal_r, in_blk_r)[T - 1:T, :]
                pvi = jnp.where(t == 0, hal_i, in_blk_i)[T - 1:T, :]
                sxr = jnp.where(row >= 1, pltpu.roll(xr, 1, 0), pvr)
                sxi = jnp.where(row >= 1, pltpu.roll(xi, 1, 0), pvi)
                acc[:, lre] += gr * sxr + gi * sxi
                acc[:, lim] += gi * sxr - gr * sxi
            return 0

        lax.fori_loop(0, per, step, 0)

        @pl.when(pid == nblk - 1)
        def _():
            da_ref[...] = jnp.sum(acc[...], axis=0, keepdims=True)

    row_spec = pl.BlockSpec((rt, L2), lambda i: (nblk - 1 - i, 0))
    halo = pl.BlockSpec((T, L2), lambda i: (jnp.maximum((nblk - 1 - i) * per - 1, 0), 0))
    vec = pl.BlockSpec((1, L2), lambda i: (0, 0))
    return pl.pallas_call(
        body, name="s5_scan_bwd", grid=(nblk,), in_specs=[row_spec, row_spec, halo, vec],
        out_specs=[row_spec, vec],
        out_shape=[jax.ShapeDtypeStruct((S, L2), f32), jax.ShapeDtypeStruct((1, L2), f32)],
        scratch_shapes=[pltpu.VMEM((T, L2), f32)] * 4,
        compiler_params=_cp("arbitrary"),
    )(dx, x, x, ab)


def _S5_U_SPEC(rt):
    return pl.BlockSpec((rt, LRU_WIDTH), lambda i: (i, PROJ_S5_U))


def _s5_out(yc, proj, d, wglu, bglu):
    S, W = yc.shape
    rt = _pick(S, ROW_TILE, SUBLANES)

    def body(yc_ref, u_ref, d_ref, w_ref, b_ref, o_ref, y_ref):
        y = yc_ref[...] + d_ref[...] * u_ref[:, 0:W]
        yg = _gelu(y)
        z = _dot(yg.astype(bf16), w_ref[...], NN) + b_ref[...]
        o_ref[...] = yg * _sigmoid(z)
        y_ref[...] = y

    row = pl.BlockSpec((rt, W), lambda i: (i, 0))
    vec = pl.BlockSpec((1, W), lambda i: (0, 0))
    mat = pl.BlockSpec((W, W), lambda i: (0, 0))
    return pl.pallas_call(
        body, name="s5_out", grid=(S // rt,), in_specs=[row, _S5_U_SPEC(rt), vec, mat, vec], out_specs=[row, row],
        out_shape=[jax.ShapeDtypeStruct((S, W), f32)] * 2, compiler_params=_cp("parallel"),
    )(yc, proj, d.reshape(1, W), wglu, bglu.reshape(1, W))


def _s5_out_bwd(dssm, y, proj, d, wglu, bglu):
    S, W = y.shape
    rt = _pick(S, ROW_TILE, SUBLANES)

    def body(do_ref, y_ref, u_ref, d_ref, w_ref, b_ref, dy_ref, du_ref, dw_ref, acc_ref):
        @pl.when(pl.program_id(0) == 0)
        def _():
            dw_ref[...] = jnp.zeros_like(dw_ref)
            acc_ref[...] = jnp.zeros_like(acc_ref)

        do, y = do_ref[...], y_ref[...]
        yg = _gelu(y)
        ygb = yg.astype(bf16)
        sg = _sigmoid(_dot(ygb, w_ref[...], NN) + b_ref[...])
        dz = do * yg * sg * (1.0 - sg)
        dzb = dz.astype(bf16)
        dyg = do * sg + _dot(dzb, w_ref[...], NT)
        dy = dyg * _gelu_grad(y)
        dy_ref[...] = dy
        du_ref[...] = dy * d_ref[...]
        dw_ref[...] += _dot(ygb, dzb, TN)
        acc_ref[0:1, :] += jnp.sum(dz, axis=0, keepdims=True)
        acc_ref[1:2, :] += jnp.sum(dy * u_ref[:, 0:W], axis=0, keepdims=True)

    row = pl.BlockSpec((rt, W), lambda i: (i, 0))
    vec = pl.BlockSpec((1, W), lambda i: (0, 0))
    mat = pl.BlockSpec((W, W), lambda i: (0, 0))
    acc = pl.BlockSpec((SUBLANES, W), lambda i: (0, 0))
    return pl.pallas_call(
        body, name="s5_out_bwd", grid=(S // rt,), in_specs=[row, row, _S5_U_SPEC(rt), vec, mat, vec],
        out_specs=[row, row, mat, acc],
        out_shape=[jax.ShapeDtypeStruct((S, W), f32)] * 2
        + [jax.ShapeDtypeStruct((W, W), f32), jax.ShapeDtypeStruct((SUBLANES, W), f32)],
        compiler_params=_cp("arbitrary"),
    )(dssm, y, proj, d.reshape(1, W), wglu, bglu.reshape(1, W))


MIX_SPLITS = ((0, ATTN_WIDTH), (ATTN_WIDTH, ATTN_WIDTH + LRU_WIDTH), (ATTN_WIDTH + LRU_WIDTH, D_MODEL))


def _mixnorm(attn, lru, ssm, g):
    S = attn.shape[0]
    rt = _pick(S, ROW_TILE, SUBLANES)

    def body(a_ref, l_ref, s_ref, g_ref, o_ref):
        for ref, (lo, hi) in zip((a_ref, l_ref, s_ref), MIX_SPLITS):
            x = ref[...]
            ms = jnp.mean(x * x, axis=-1, keepdims=True)
            o_ref[:, lo:hi] = (x * lax.rsqrt(ms + RMS_EPS) * g_ref[:, lo:hi]).astype(o_ref.dtype)

    rows = [pl.BlockSpec((rt, hi - lo), lambda i: (i, 0)) for lo, hi in MIX_SPLITS]
    return pl.pallas_call(
        body, name="mixnorm", grid=(S // rt,), in_specs=rows + [pl.BlockSpec((1, D_MODEL), lambda i: (0, 0))],
        out_specs=pl.BlockSpec((rt, D_MODEL), lambda i: (i, 0)),
        out_shape=jax.ShapeDtypeStruct((S, D_MODEL), bf16), compiler_params=_cp("parallel"),
    )(attn, lru, ssm, g.reshape(1, D_MODEL))


def _mixnorm_bwd(dmixed, attn, lru, ssm, g):
    S = attn.shape[0]
    rt = _pick(S, ROW_TILE, SUBLANES)

    def body(d_ref, a_ref, l_ref, s_ref, g_ref, da_ref, dl_ref, ds_ref, dlt_ref, acc_ref):
        @pl.when(pl.program_id(0) == 0)
        def _():
            acc_ref[...] = jnp.zeros_like(acc_ref)

        outs = []
        for ref, (lo, hi) in zip((a_ref, l_ref, s_ref), MIX_SPLITS):
            x = ref[...]
            dy = d_ref[:, lo:hi]
            rinv = lax.rsqrt(jnp.mean(x * x, axis=-1, keepdims=True) + RMS_EPS)
            dyg = dy * g_ref[:, lo:hi]
            outs.append(rinv * dyg - x * (rinv * rinv * rinv) * jnp.mean(dyg * x, axis=-1, keepdims=True))
            acc_ref[0:1, lo:hi] += jnp.sum(dy * x * rinv, axis=0, keepdims=True)
        da_ref[...], dl_ref[...], ds_ref[...] = outs
        hi_ = lax.broadcasted_iota(jnp.int32, (ATTN_WIDTH, ATTN_WIDTH), 0) // HEAD_DIM
        hj_ = lax.broadcasted_iota(jnp.int32, (ATTN_WIDTH, ATTN_WIDTH), 1) // HEAD_DIM
        same = jnp.where(hi_ == hj_, 1.0, 0.0).astype(f32)
        dlt_ref[...] = jnp.dot(outs[0] * a_ref[...], same, precision=lax.Precision.HIGHEST, preferred_element_type=f32)

    rows = [pl.BlockSpec((rt, hi - lo), lambda i: (i, 0)) for lo, hi in MIX_SPLITS]
    full = pl.BlockSpec((rt, D_MODEL), lambda i: (i, 0))
    return pl.pallas_call(
        body, name="mixnorm_bwd", grid=(S // rt,),
        in_specs=[full] + rows + [pl.BlockSpec((1, D_MODEL), lambda i: (0, 0))],
        out_specs=rows + [rows[0], pl.BlockSpec((SUBLANES, D_MODEL), lambda i: (0, 0))],
        out_shape=[jax.ShapeDtypeStruct((S, hi - lo), f32) for lo, hi in MIX_SPLITS]
        + [jax.ShapeDtypeStruct((S, ATTN_WIDTH), f32), jax.ShapeDtypeStruct((SUBLANES, D_MODEL), f32)],
        compiler_params=_cp("arbitrary"),
    )(dmixed, attn, lru, ssm, g.reshape(1, D_MODEL))


FFN_COL_TILE = 512


def _ffn_conv(x, prev, w_ref, b_ref, K):
    y = b_ref[...] + w_ref[K - 1:K, :] * x
    for k in range(K - 1):
        y = y + w_ref[k:k + 1, :] * _shift_down(x, prev, K - 1 - k)
    return y


def _ffn_act(up, conv_w, conv_b):
    S, C2 = up.shape
    C = C2 // 2
    K = conv_w.shape[0]
    ct = FFN_COL_TILE
    nct = C // ct
    rt = _pick(S, ROW_TILE, SUBLANES)

    def body(g_ref, gp_ref, v_ref, vp_ref, wg_ref, wv_ref, bg_ref, bv_ref, o_ref):
        first = pl.program_id(1) == 0
        gate = _ffn_conv(g_ref[...], jnp.where(first, 0.0, gp_ref[...]), wg_ref, bg_ref, K)
        val = _ffn_conv(v_ref[...], jnp.where(first, 0.0, vp_ref[...]), wv_ref, bv_ref, K)
        o_ref[...] = (_gelu(gate) * val).astype(o_ref.dtype)

    def specs(off):
        return (pl.BlockSpec((rt, ct), lambda j, i: (i, j + off)), _prev_halo_spec(rt, ct, lambda j, i: j + off))

    def wspec(off, rows):
        return pl.BlockSpec((rows, ct), lambda j, i: (0, j + off))

    g_s, gp_s = specs(0)
    v_s, vp_s = specs(nct)
    return pl.pallas_call(
        body, name="ffn_act", grid=(nct, S // rt),
        in_specs=[g_s, gp_s, v_s, vp_s, wspec(0, K), wspec(nct, K), wspec(0, 1), wspec(nct, 1)],
        out_specs=pl.BlockSpec((rt, ct), lambda j, i: (i, j)),
        out_shape=jax.ShapeDtypeStruct((S, C), bf16), compiler_params=_cp("parallel", "parallel"),
    )(up, up, up, up, conv_w, conv_w, conv_b.reshape(1, C2), conv_b.reshape(1, C2))


def _ffn_act_bwd(dact, up, conv_w, conv_b):
    S, C2 = up.shape
    C = C2 // 2
    K = conv_w.shape[0]
    ct = FFN_COL_TILE
    nct = C // ct
    rt = _pick(S, ROW_TILE, SUBLANES)

    def body(d_ref, g_ref, gp_ref, v_ref, vp_ref, wg_ref, wv_ref, bg_ref, bv_ref, o_ref):
        first = pl.program_id(1) == 0
        gate = _ffn_conv(g_ref[...], jnp.where(first, 0.0, gp_ref[...]), wg_ref, bg_ref, K)
        val = _ffn_conv(v_ref[...], jnp.where(first, 0.0, vp_ref[...]), wv_ref, bv_ref, K)
        d = d_ref[...]
        gl, dgl = _gelu_pair(gate)
        o_ref[0] = d * val * dgl
        o_ref[1] = d * gl

    def specs(off):
        return (pl.BlockSpec((rt, ct), lambda j, i: (i, j + off)), _prev_halo_spec(rt, ct, lambda j, i: j + off))

    def wspec(off, rows):
        return pl.BlockSpec((rows, ct), lambda j, i: (0, j + off))

    g_s, gp_s = specs(0)
    v_s, vp_s = specs(nct)
    return pl.pallas_call(
        body, name="ffn_act_bwd", grid=(nct, S // rt),
        in_specs=[pl.BlockSpec((rt, ct), lambda j, i: (i, j)), g_s, gp_s, v_s, vp_s,
                  wspec(0, K), wspec(nct, K), wspec(0, 1), wspec(nct, 1)],
        out_specs=pl.BlockSpec((2, rt, ct), lambda j, i: (0, i, j)),
        out_shape=jax.ShapeDtypeStruct((2, S, C), f32), compiler_params=_cp("parallel", "parallel"),
    )(dact, up, up, up, up, conv_w, conv_w, conv_b.reshape(1, C2), conv_b.reshape(1, C2))


ANY = pl.BlockSpec(memory_space=pl.ANY)


def _rows_for(cols):
    return max(16, (1 << 17) // cols)


def _chips(x, y):
    return [(1 - x, y), (x, 1 - y), (1 - x, 1 - y)]


class _Gather:
    def __init__(self, shards, split):
        self.shapes = [s.shape for s in shards]
        self.dtypes = [s.dtype for s in shards]
        self.split = list(split)
        self.nt = len(shards)
        self.in_specs = [ANY] * self.nt
        self.out_specs = [ANY] * self.nt
        self.out_shape = [jax.ShapeDtypeStruct((4,) + s, dt) for s, dt in zip(self.shapes, self.dtypes)]
        self.scratch = [pltpu.SemaphoreType.DMA((3, self.nt))] * 4 if self.nt else []

    def _part(self, ref, t, half):
        if not self.split[t]:
            return ref
        r = self.shapes[t][0] // 2
        return ref.at[pl.ds(half * r, r), :]

    def _ici(self, ins, outs, sems, k, t, chip, landing_chip):
        x, y, c = lax.axis_index("x"), lax.axis_index("y"), lax.axis_index("c")
        return pltpu.make_async_remote_copy(
            src_ref=self._part(ins[t], t, c), dst_ref=self._part(outs[t].at[landing_chip], t, c),
            send_sem=sems[0].at[k, t], recv_sem=sems[1].at[k, t], device_id=(chip[0], chip[1], c), device_id_type=MESH)

    def _d2d(self, outs, sems, k, t, q, half):
        x, y, c = lax.axis_index("x"), lax.axis_index("y"), lax.axis_index("c")
        rows = self._part(outs[t].at[q], t, half)
        return pltpu.make_async_remote_copy(
            src_ref=rows, dst_ref=rows, send_sem=sems[2].at[k, t], recv_sem=sems[3].at[k, t],
            device_id=(x, y, 1 - c), device_id_type=MESH)

    def start(self, ins, outs, sems):
        x, y = lax.axis_index("x"), lax.axis_index("y")
        me = 2 * x + y
        for k, chip in enumerate(_chips(x, y)):
            for t in range(self.nt):
                self._ici(ins, outs, sems, k, t, chip, me).start()

    def finish(self, ins, outs, sems):
        x, y, c = lax.axis_index("x"), lax.axis_index("y"), lax.axis_index("c")
        me = 2 * x + y
        chips = _chips(x, y)
        for k, chip in enumerate(chips):
            q = 2 * chip[0] + chip[1]
            for t in range(self.nt):
                self._ici(ins, outs, sems, k, t, chip, q).wait_recv()
                if self.split[t]:
                    self._d2d(outs, sems, k, t, q, c).start()
        for k, chip in enumerate(chips):
            q = 2 * chip[0] + chip[1]
            for t in range(self.nt):
                if self.split[t]:
                    self._d2d(outs, sems, k, t, q, 1 - c).wait_recv()
        for k, chip in enumerate(chips):
            q = 2 * chip[0] + chip[1]
            for t in range(self.nt):
                self._ici(ins, outs, sems, k, t, chip, me).wait_send()
                if self.split[t]:
                    self._d2d(outs, sems, k, t, q, c).wait_send()


def _gather_weights(shards, split):
    g = _Gather(shards, split)
    nt = g.nt

    def body(*refs):
        ins, outs, sems = refs[:nt], refs[nt:2 * nt], refs[2 * nt:]
        g.start(ins, outs, sems)
        g.finish(ins, outs, sems)

    return pl.pallas_call(
        body, name="gather_weights", in_specs=g.in_specs, out_specs=g.out_specs, out_shape=g.out_shape,
        scratch_shapes=g.scratch,
    )(*shards)


def _pair_send(g0, g1):
    nt = len(g0)

    def body(*refs):
        in0, in1, outs = refs[:nt], refs[nt:2 * nt], refs[2 * nt:3 * nt]
        send, recv = refs[3 * nt:]
        x, y, c = lax.axis_index("x"), lax.axis_index("y"), lax.axis_index("c")

        def copies(ins):
            return [pltpu.make_async_remote_copy(
                src_ref=ins[t], dst_ref=outs[t], send_sem=send.at[t], recv_sem=recv.at[t],
                device_id=(x, y, 1 - c), device_id_type=MESH) for t in range(nt)]

        @pl.when(c == 0)
        def _():
            for cp in copies(in1):
                cp.start()

        @pl.when(c == 1)
        def _():
            for cp in copies(in0):
                cp.start()

        for cp in copies(in0):
            cp.wait()

    return pl.pallas_call(
        body, name="pair_send", in_specs=[ANY] * (2 * nt), out_specs=[ANY] * nt,
        out_shape=[jax.ShapeDtypeStruct(g.shape, g.dtype) for g in g0],
        scratch_shapes=[pltpu.SemaphoreType.DMA((nt,))] * 2,
    )(*g0, *g1)


def _pair_sum(g0, g1, other, name):
    R, C = g0.shape
    rt = _pick(R, _rows_for(C), 16)

    def body(a_ref, b_ref, o_ref, out_ref):
        mine = jnp.where(lax.axis_index("c") == 0, a_ref[...], b_ref[...])
        out_ref[...] = (mine + o_ref[...]).astype(out_ref.dtype)

    row = pl.BlockSpec((rt, C), lambda i: (i, 0))
    return pl.pallas_call(
        body, name=name, grid=(R // rt,), in_specs=[row, row, row], out_specs=row,
        out_shape=jax.ShapeDtypeStruct((R, C), bf16), compiler_params=_cp("parallel"),
    )(g0, g1, other)


def _chip_exchange(slabs):
    nt = len(slabs)

    def body(*refs):
        ins, outs = refs[:nt], refs[nt:2 * nt]
        send, recv = refs[2 * nt:]
        x, y, c = lax.axis_index("x"), lax.axis_index("y"), lax.axis_index("c")
        me = 2 * x + y
        chips = _chips(x, y)
        sends = []
        for k, (px, py) in enumerate(chips):
            for t in range(nt):
                cp = pltpu.make_async_remote_copy(
                    src_ref=ins[t].at[2 * px + py], dst_ref=outs[t].at[me], send_sem=send.at[k, t],
                    recv_sem=recv.at[k, t], device_id=(px, py, c), device_id_type=MESH)
                cp.start()
                sends.append(cp)
        for k, (px, py) in enumerate(chips):
            q = 2 * px + py
            for t in range(nt):
                pltpu.make_async_remote_copy(
                    src_ref=ins[t].at[q], dst_ref=outs[t].at[q], send_sem=send.at[k, t], recv_sem=recv.at[k, t],
                    device_id=(px, py, c), device_id_type=MESH).wait_recv()
        for cp in sends:
            cp.wait_send()

    return pl.pallas_call(
        body, name="chip_exchange", in_specs=[ANY] * nt, out_specs=[ANY] * nt,
        out_shape=[jax.ShapeDtypeStruct(s.shape, s.dtype) for s in slabs],
        scratch_shapes=[pltpu.SemaphoreType.DMA((3, nt))] * 2,
    )(*slabs)


def _sum_chips(recv, own, name):
    n, r, C = recv.shape
    rt = _pick(r, _rows_for(C), 16)
    own3 = own.ndim == 3

    def body(r_ref, o_ref, out_ref):
        me = 2 * lax.axis_index("x") + lax.axis_index("y")
        acc = None
        for q in range(n):
            term = jnp.where(me == q, o_ref[q] if own3 else o_ref[...], r_ref[q]).astype(f32)
            acc = term if acc is None else acc + term
        out_ref[...] = acc

    blk = pl.BlockSpec((n, rt, C), lambda i: (0, i, 0))
    return pl.pallas_call(
        body, name=name, grid=(r // rt,), in_specs=[blk, blk if own3 else pl.BlockSpec((rt, C), lambda i: (i, 0))],
        out_specs=pl.BlockSpec((rt, C), lambda i: (i, 0)), out_shape=jax.ShapeDtypeStruct((r, C), f32),
        compiler_params=_cp("parallel"),
    )(recv, own)


def _pair_swap(parts):
    nt = len(parts)

    def body(*refs):
        ins, outs = refs[:nt], refs[nt:2 * nt]
        send, recv = refs[2 * nt:]
        x, y, c = lax.axis_index("x"), lax.axis_index("y"), lax.axis_index("c")
        cps = [pltpu.make_async_remote_copy(
            src_ref=ins[t], dst_ref=outs[t], send_sem=send.at[t], recv_sem=recv.at[t],
            device_id=(x, y, 1 - c), device_id_type=MESH) for t in range(nt)]
        for cp in cps:
            cp.start()
        for cp in cps:
            cp.wait()

    return pl.pallas_call(
        body, name="pair_swap", in_specs=[ANY] * nt, out_specs=[ANY] * nt,
        out_shape=[jax.ShapeDtypeStruct(p.shape, p.dtype) for p in parts],
        scratch_shapes=[pltpu.SemaphoreType.DMA((nt,))] * 2,
    )(*parts)


def _adamw_layers(mine, theirs, w, m, v, name):
    L, r, C = w.shape
    rt = _pick(r, _rows_for(C), 16)

    def body(a_ref, b_ref, w_ref, m_ref, v_ref, g_ref, d_ref, mo_ref, vo_ref):
        g_ref[...] = jnp.where(pl.program_id(0) == lax.axis_index("c"), a_ref[...], b_ref[...])
        _adamw_math(g_ref, w_ref, m_ref, v_ref, d_ref, mo_ref, vo_ref)

    flat = pl.BlockSpec((rt, C), lambda l, i: (i, 0))
    lay = pl.BlockSpec((None, rt, C), lambda l, i: (l, i, 0))
    return pl.pallas_call(
        body, name=name, grid=(L, r // rt), in_specs=[flat, flat, lay, lay, lay], out_specs=[lay] * 4,
        out_shape=[jax.ShapeDtypeStruct((L, r, C), f32)] * 4, compiler_params=_cp("parallel", "parallel"),
    )(mine, theirs, w, m, v)


def _adamw_math(g_ref, w_ref, m_ref, v_ref, d_ref, mo_ref, vo_ref):
    gg = g_ref[...]
    m_new = ADAM_B1 * m_ref[...] + (1.0 - ADAM_B1) * gg
    v_new = ADAM_B2 * v_ref[...] + (1.0 - ADAM_B2) * (gg * gg)
    m_hat = m_new / (1.0 - ADAM_B1 ** ADAM_STEP)
    v_hat = v_new / (1.0 - ADAM_B2 ** ADAM_STEP)
    d_ref[...] = -ADAM_LR * (m_hat / (jnp.sqrt(v_hat) + ADAM_EPS) + ADAM_WD * w_ref[...])
    mo_ref[...] = m_new
    vo_ref[...] = v_new


def _chip_gather(src):
    def body(src_ref, out_ref, send, recv):
        x, y, c = lax.axis_index("x"), lax.axis_index("y"), lax.axis_index("c")
        me = 2 * x + y
        chips = _chips(x, y)
        sends = []
        for k, (px, py) in enumerate(chips):
            cp = pltpu.make_async_remote_copy(
                src_ref=src_ref, dst_ref=out_ref.at[me], send_sem=send.at[k], recv_sem=recv.at[k],
                device_id=(px, py, c), device_id_type=MESH)
            cp.start()
            sends.append(cp)
        for k, (px, py) in enumerate(chips):
            pltpu.make_async_remote_copy(
                src_ref=src_ref, dst_ref=out_ref.at[2 * px + py], send_sem=send.at[k], recv_sem=recv.at[k],
                device_id=(px, py, c), device_id_type=MESH).wait_recv()
        for cp in sends:
            cp.wait_send()

    return pl.pallas_call(
        body, name="chip_gather", in_specs=[ANY], out_specs=ANY,
        out_shape=jax.ShapeDtypeStruct((4,) + src.shape, src.dtype),
        scratch_shapes=[pltpu.SemaphoreType.DMA((3,))] * 2,
    )(src)


FLAT_TILE = 2048


def _add2(a, b, name):
    R = a.shape[0]
    rt = _pick(R, FLAT_TILE, SUBLANES)

    def body(a_ref, b_ref, o_ref):
        o_ref[...] = a_ref[...] + b_ref[...]

    row = pl.BlockSpec((rt, LANES), lambda i: (i, 0))
    return pl.pallas_call(
        body, name=name, grid=(R // rt,), in_specs=[row, row], out_specs=row,
        out_shape=jax.ShapeDtypeStruct((R, LANES), f32), compiler_params=_cp("parallel"),
    )(a, b)


def _adamw(g, w, m, v, name):
    R = g.shape[0]
    rt = _pick(R, FLAT_TILE, SUBLANES)

    def body(g_ref, w_ref, m_ref, v_ref, d_ref, mo_ref, vo_ref):
        gg = g_ref[...]
        m_new = ADAM_B1 * m_ref[...] + (1.0 - ADAM_B1) * gg
        v_new = ADAM_B2 * v_ref[...] + (1.0 - ADAM_B2) * (gg * gg)
        m_hat = m_new / (1.0 - ADAM_B1 ** ADAM_STEP)
        v_hat = v_new / (1.0 - ADAM_B2 ** ADAM_STEP)
        d_ref[...] = -ADAM_LR * (m_hat / (jnp.sqrt(v_hat) + ADAM_EPS) + ADAM_WD * w_ref[...])
        mo_ref[...] = m_new
        vo_ref[...] = v_new

    row = pl.BlockSpec((rt, LANES), lambda i: (i, 0))
    return pl.pallas_call(
        body, name=name, grid=(R // rt,), in_specs=[row] * 4, out_specs=[row] * 3,
        out_shape=[jax.ShapeDtypeStruct((R, LANES), f32)] * 3, compiler_params=_cp("parallel"),
    )(g, w, m, v)


def _pack(arrs, dtype):
    flat = jnp.concatenate([a.astype(dtype).reshape(-1) for a in arrs])
    per = FLAT_TILE * LANES
    flat = jnp.pad(flat, (0, (-flat.shape[0]) % per))
    return flat.reshape(-1, LANES)


def _unpack(buf, shapes):
    flat = buf.reshape(-1)
    out, off = [], 0
    for s in shapes:
        n = math.prod(s)
        out.append(flat[off:off + n].reshape(s))
        off += n
    return out


def _block_diag(w):
    n, a, b = w.shape
    eye = jnp.eye(n, dtype=w.dtype)
    return (w[:, :, None, :] * eye[:, None, :, None]).reshape(n * a, n * b)


def _diag_blocks(m, n):
    a, b = m.shape[0] // n, m.shape[1] // n
    idx = jnp.arange(n)
    return m.reshape(n, a, n, b)[idx, :, idx, :]


BIG = ("w_in", "w_out", "w_up", "w_down", "s5_w_glu")
BIG_COL_SHARDED = {"w_in": True, "w_out": False, "w_up": True, "w_down": False, "s5_w_glu": False}
CONV_SHARDED = ("lru_conv_w", "ffn_conv_w")
SMALL = ("lru_conv_b", "lru_wr", "lru_br", "lru_wi", "lru_bi", "lru_lambda", "s5_a_re", "s5_a_im", "s5_b_re",
         "s5_b_im", "s5_c_re", "s5_c_im", "s5_d", "s5_log_step", "s5_b_glu", "mix_norm_g", "ln1_g", "ln1_b",
         "ffn_conv_b", "ln2_g", "ln2_b")
WEIGHTS = ("w_in", "lru_conv_w", "lru_conv_b", "lru_wr", "lru_br", "lru_wi", "lru_bi", "lru_lambda", "s5_a_re",
           "s5_a_im", "s5_b_re", "s5_b_im", "s5_c_re", "s5_c_im", "s5_d", "s5_log_step", "s5_w_glu", "s5_b_glu",
           "mix_norm_g", "w_out", "ln1_g", "ln1_b", "w_up", "ffn_conv_w", "ffn_conv_b", "w_down", "ln2_g", "ln2_b")


def _assemble(slabs, col_sharded):
    _, L, r, c = slabs.shape
    if col_sharded:
        return slabs.transpose(1, 2, 0, 3).reshape(L, r, 4 * c)
    return slabs.transpose(1, 0, 2, 3).reshape(L, 4 * r, c)


def _s5_prepare(p):
    G = N_S5_GROUPS
    bt_re, bt_im = p["s5_b_re"].transpose(0, 2, 1), p["s5_b_im"].transpose(0, 2, 1)
    ls = p["s5_log_step"].reshape(G, 1)
    ab_re, ab_im, bb_re, bb_im = _s5_params(p["s5_a_re"], p["s5_a_im"], ls, bt_re, bt_im)
    ab = jnp.concatenate([ab_re.reshape(1, S5_LANES), ab_im.reshape(1, S5_LANES)], axis=1)
    bbcat = jnp.concatenate([_block_diag(bb_re), _block_diag(bb_im)], axis=1).astype(bf16)
    ccat = jnp.concatenate([_block_diag(p["s5_c_re"].transpose(0, 2, 1)),
                            -_block_diag(p["s5_c_im"].transpose(0, 2, 1))], axis=0).astype(bf16)
    bbcat_pad = jnp.concatenate([bbcat, jnp.zeros((LRU_WIDTH - S5_WIDTH, 2 * S5_LANES), bf16)], axis=0)
    return dict(bt_re=bt_re, bt_im=bt_im, ls=ls, ab=ab, bbcat=bbcat, bbcat_pad=bbcat_pad, ccat=ccat)


def _layer_fwd(h, p, cos, sin, pending, install):
    sv = {"h": h}
    proj = _mm(h, p["w_in"], "nn", "mm_proj", tn=768)
    sv.update(proj=proj)
    qr, kr = _rope_fwd(proj, cos, sin)
    attn, ltot, gathered = _attn_fwd2(qr, kr, proj, [s for _, _, s in pending], [True] * len(pending))
    install(pending, gathered)
    sv.update(qr=qr, kr=kr, attn=attn, ltot=ltot)
    wr, wi = _block_diag(p["lru_wr"]).astype(bf16), _block_diag(p["lru_wi"]).astype(bf16)
    xc, r, i, log_a, u = _lru_pre(proj, p["lru_conv_w"], p["lru_conv_b"], wr, p["lru_br"], wi, p["lru_bi"],
                                  p["lru_lambda"])
    hl, lru = _lru_scan(log_a, u, proj)
    sv.update(wr=wr, wi=wi, xc=xc, r=r, i=i, log_a=log_a, hl=hl, lru=lru)
    s5 = _s5_prepare(p)
    bu = _mm(proj, s5["bbcat_pad"], "nn", "mm_s5_bu", a_win=(PROJ_S5_U, LRU_WIDTH))
    xs = _s5_scan(bu, s5["ab"])
    yc = _mm(xs, s5["ccat"], "nn", "mm_s5_y")
    ssm, y = _s5_out(yc, proj, p["s5_d"].reshape(-1), p["s5_w_glu"], p["s5_b_glu"])
    sv.update(s5=s5, xs=xs, y=y, ssm=ssm)
    mixed = _mixnorm(attn, lru, ssm, p["mix_norm_g"])
    mix = _mm(mixed, p["w_out"], "nn", "mm_out")
    h1, z1 = _ln_fwd(h, mix, p["ln1_g"], p["ln1_b"], "ln_fwd")
    sv.update(mixed=mixed, z1=z1, h1=h1)
    up = _mm(h1, p["w_up"], "nn", "mm_up", tn=1536)
    act = _ffn_act(up, p["ffn_conv_w"], p["ffn_conv_b"])
    ffn = _mm(act, p["w_down"], "nn", "mm_down")
    h2, z2 = _ln_fwd(h1, ffn, p["ln2_g"], p["ln2_b"], "ln_fwd")
    sv.update(up=up, act=act, z2=z2)
    return h2, sv


def _layer_bwd(dy_a, dy_b, p, sv, cos, sin):
    gr = {}
    dz2, acc = _ln_bwd(dy_a, dy_b, sv["z2"], p["ln2_g"], "ln_bwd_top" if dy_a is None else "ln_bwd")
    gr["ln2_g"], gr["ln2_b"] = acc[0], acc[1]
    dact = _mm(dz2, p["w_down"], "nt", "mm_dact")
    gr["w_down"] = _mm(sv["act"], dz2, "tn", "mm_dw_down")
    dupc = _ffn_act_bwd(dact, sv["up"], p["ffn_conv_w"], p["ffn_conv_b"])
    dup, acc = _conv_bwd(dupc, sv["up"], p["ffn_conv_w"], "ffn_conv_bwd", col_tile=FFN_COL_TILE, out_dtype=bf16)
    gr["ffn_conv_w"], gr["ffn_conv_b"] = acc[0:3], acc[3]
    dh1 = _mm(dup, p["w_up"], "nt", "mm_dh1", tk=2048)
    gr["w_up"] = _mm(sv["h1"], dup, "tn", "mm_dw_up", tn=1536)
    dz1, acc = _ln_bwd(dz2, dh1, sv["z1"], p["ln1_g"], "ln_bwd")
    gr["ln1_g"], gr["ln1_b"] = acc[0], acc[1]
    dmixed = _mm(dz1, p["w_out"], "nt", "mm_dmixed")
    gr["w_out"] = _mm(sv["mixed"], dz1, "tn", "mm_dw_out")
    dattn, dlru, dssm, delta, acc = _mixnorm_bwd(dmixed, sv["attn"], sv["lru"], sv["ssm"], p["mix_norm_g"])
    gr["mix_norm_g"] = acc[0]
    proj = sv["proj"]
    dqr, dkr, dv = _attn_bwd2(sv["qr"], sv["kr"], proj, dattn, sv["ltot"], delta)
    dq, dk = _rope_bwd(dqr, dkr, cos, sin)
    g, dgate = _lru_scan_bwd(dlru, proj, sv["hl"], sv["log_a"])
    dxc, dwr, dwi, acc = _lru_gate_bwd(g, sv["hl"], sv["xc"], sv["r"], sv["i"], sv["log_a"], sv["wr"], sv["wi"],
                                       p["lru_lambda"])
    gr["lru_wr"], gr["lru_wi"] = _diag_blocks(dwr, N_LRU_HEADS), _diag_blocks(dwi, N_LRU_HEADS)
    gr["lru_br"], gr["lru_bi"], gr["lru_lambda"] = acc[0], acc[1], acc[2]
    dxr, acc = _conv_bwd(dxc, proj, p["lru_conv_w"], "lru_conv_bwd", x_col_block=PROJ_LRU_X)
    gr["lru_conv_w"], gr["lru_conv_b"] = acc[0:4], acc[4]
    s5 = sv["s5"]
    G = N_S5_GROUPS
    dy, du_direct, dwglu, acc = _s5_out_bwd(dssm, sv["y"], proj, p["s5_d"].reshape(-1), p["s5_w_glu"],
                                            p["s5_b_glu"])
    gr["s5_w_glu"], gr["s5_b_glu"], gr["s5_d"] = dwglu, acc[0], acc[1].reshape(G, S5_GROUP)
    dxs = _mm(dy, s5["ccat"], "nt", "mm_s5_dx")
    dccat = _mm(sv["xs"], dy, "tn", "mm_s5_dc")
    gr["s5_c_re"] = _diag_blocks(dccat[:S5_LANES], G).transpose(0, 2, 1)
    gr["s5_c_im"] = -_diag_blocks(dccat[S5_LANES:], G).transpose(0, 2, 1)
    gs, dab = _s5_scan_bwd(dxs, sv["xs"], s5["ab"])
    du = _mm(gs, s5["bbcat"], "nt", "mm_s5_du", add=du_direct)
    dbbcat = _mm(proj, gs, "tn", "mm_s5_dbb", a_win=(PROJ_S5_U, LRU_WIDTH))[:S5_WIDTH]
    d_ar, d_ai, d_ls, d_btr, d_bti = _s5_params_bwd(
        p["s5_a_re"], p["s5_a_im"], s5["ls"], s5["bt_re"], s5["bt_im"],
        dab[:, :S5_LANES].reshape(G, S5_STATE), dab[:, S5_LANES:].reshape(G, S5_STATE),
        _diag_blocks(dbbcat[:, :S5_LANES], G), _diag_blocks(dbbcat[:, S5_LANES:], G))
    gr["s5_a_re"], gr["s5_a_im"], gr["s5_log_step"] = d_ar, d_ai, d_ls.reshape(G)
    gr["s5_b_re"], gr["s5_b_im"] = d_btr.transpose(0, 2, 1), d_bti.transpose(0, 2, 1)
    pad = jnp.zeros((du.shape[0], D_IN_PAD - D_IN), f32)
    dproj = jnp.concatenate([dq, dk, dv, dxr, dgate, du, pad], axis=1).astype(bf16)
    gr["w_in"] = _mm(sv["h"], dproj, "tn", "mm_dw_in", tn=768)[:, :D_IN]
    dh = _mm(dproj, p["w_in"], "nt", "mm_dh")
    return dz1, dh, gr


def _train_step(d):
    x, target = d["x"][0], d["loss_target"][0]
    S = x.shape[0]
    me = 2 * lax.axis_index("x") + lax.axis_index("y")

    def rows2d(a):
        return a.reshape(a.shape[0] * a.shape[1], a.shape[2])

    params = [{n: d[n][l] for n in SMALL} for l in range(DEPTH)]

    def install(items, gathered):
        for (n, l, mine), g in zip(items, gathered):
            g = lax.dynamic_update_slice_in_dim(g, mine[None], me, axis=0)
            if n in CONV_SHARDED:
                full = _assemble(g.reshape((4,) + d[n].shape), True)
                for k in range(DEPTH):
                    params[k][n] = full[k]
                continue
            full = _assemble(g[:, None], BIG_COL_SHARDED[n])[0]
            if n == "w_in":
                full = jnp.pad(full, ((0, 0), (0, D_IN_PAD - D_IN)))
            params[l][n] = full

    def shard(n, l):
        return (n, l, d[n][l].astype(bf16))

    first = [shard("w_in", 0)] + [(n, None, rows2d(d[n])) for n in CONV_SHARDED]
    install(first, _gather_weights([s for _, _, s in first], [True] + [False] * len(CONV_SHARDED)))
    later = [[shard(n, 0) for n in BIG[1:]] + [shard("w_in", 1)], [shard(n, 1) for n in BIG[1:]]]

    cos, sin = _rope_tables(S)
    h, saved = x, []
    for l in range(DEPTH):
        h, sv = _layer_fwd(h, params[l], cos, sin, later[l], install)
        saved.append(sv)
    dy, loss_acc = _loss_head(h, target)
    loss = lax.psum(loss_acc[0, 0], ("x", "y", "c"))
    da, db, grads = None, dy, [None] * DEPTH
    for l in reversed(range(DEPTH)):
        da, db, grads[l] = _layer_bwd(da, db, params[l], saved[l], cos, sin)
    out = {"loss": loss, "grad_x": _axpy(da, db, "grad_x")[None]}
    G = {n: jnp.stack([grads[l][n] for l in range(DEPTH)]) for n in SMALL + CONV_SHARDED}

    others = _pair_send([grads[0][n] for n in BIG], [grads[1][n] for n in BIG])
    chip = [_pair_sum(grads[0][n], grads[1][n], o, "pair_sum_" + n) for n, o in zip(BIG, others)]
    slabs = [p.reshape(p.shape[0], 4, p.shape[1] // 4).transpose(1, 0, 2) if BIG_COL_SHARDED[n]
             else p.reshape(4, p.shape[0] // 4, p.shape[1]) for n, p in zip(BIG, chip)]
    recv = _chip_exchange(slabs)
    mine = [_sum_chips(r, s, "sum_chips_" + n) for n, r, s in zip(BIG, recv, slabs)]
    theirs = _pair_swap(mine)
    for n, a, b in zip(BIG, mine, theirs):
        upd = _adamw_layers(a, b, d[n], d["m_" + n], d["v_" + n], "adamw_" + n)
        for pre, u in zip(("grad_", "delta_", "new_m_", "new_v_"), upd):
            out[pre + n] = u

    small = SMALL + CONV_SHARDED
    sp = _pack([G[n] for n in small], f32)
    chip_sum = _add2(sp, _pair_swap([sp])[0], "pair_sum_small")
    total = _sum_chips(_chip_gather(chip_sum), chip_sum, "sum_chips_small")
    gs = dict(zip(small, _unpack(total, [G[n].shape for n in small])))
    for n in CONV_SHARDED:
        L, K, C = gs[n].shape
        gs[n] = lax.dynamic_index_in_dim(gs[n].reshape(L, K, 4, C // 4), me, axis=2, keepdims=False)
    small_shapes = [d[n].shape for n in small]
    gsmall = _pack([gs[n] for n in small], f32)
    upd = _adamw(gsmall, _pack([d[n] for n in small], f32), _pack([d["m_" + n] for n in small], f32),
                 _pack([d["v_" + n] for n in small], f32), "adamw_small")
    for pre, buf in zip(("grad_", "delta_", "new_m_", "new_v_"), (gsmall,) + tuple(upd)):
        for n, a in zip(small, _unpack(buf, small_shapes)):
            out[pre + n] = a

    return (out["loss"], out["grad_x"]) + tuple(out[pre + n] for pre in ("grad_", "delta_", "new_m_", "new_v_")
                                                for n in WEIGHTS)


def kernel(
        x, w_in, lru_conv_w, lru_conv_b, lru_wr, lru_br, lru_wi, lru_bi, lru_lambda, s5_a_re, s5_a_im, s5_b_re,
        s5_b_im, s5_c_re, s5_c_im, s5_d, s5_log_step, s5_w_glu, s5_b_glu, mix_norm_g, w_out, ln1_g, ln1_b, w_up,
        ffn_conv_w, ffn_conv_b, w_down, ln2_g, ln2_b, loss_target, m_w_in, m_lru_conv_w, m_lru_conv_b, m_lru_wr,
        m_lru_br, m_lru_wi, m_lru_bi, m_lru_lambda, m_s5_a_re, m_s5_a_im, m_s5_b_re, m_s5_b_im, m_s5_c_re,
        m_s5_c_im, m_s5_d, m_s5_log_step, m_s5_w_glu, m_s5_b_glu, m_mix_norm_g, m_w_out, m_ln1_g, m_ln1_b,
        m_w_up, m_ffn_conv_w, m_ffn_conv_b, m_w_down, m_ln2_g, m_ln2_b, v_w_in, v_lru_conv_w, v_lru_conv_b,
        v_lru_wr, v_lru_br, v_lru_wi, v_lru_bi, v_lru_lambda, v_s5_a_re, v_s5_a_im, v_s5_b_re, v_s5_b_im,
        v_s5_c_re, v_s5_c_im, v_s5_d, v_s5_log_step, v_s5_w_glu, v_s5_b_glu, v_mix_norm_g, v_w_out, v_ln1_g,
        v_ln1_b, v_w_up, v_ffn_conv_w, v_ffn_conv_b, v_w_down, v_ln2_g, v_ln2_b
):
    return _train_step(dict(locals()))
```

```python
import functools
import math

import jax
import jax.numpy as jnp
from jax import lax
from jax.experimental import pallas as pl
from jax.experimental.pallas import tpu as pltpu

f32 = jnp.float32
bf16 = jnp.bfloat16
MESH = pl.DeviceIdType.MESH

D_MODEL = 1024
ATTN_WIDTH = 384
LRU_WIDTH = 384
S5_WIDTH = 256
HEAD_DIM = 64
N_LRU_HEADS = 6
N_S5_GROUPS = 16
S5_GROUP = 16
S5_STATE = 64
S5_LANES = N_S5_GROUPS * S5_STATE
D_FF = 3072
D_IN = 2176
LRU_C = 8.0
ROPE_THETA = 10000.0
DILATIONS = (1, 4, 16)
ATTN_BLOCK = 128
DEPTH = 2
ALPHA = (2 * DEPTH) ** 0.25
LN_EPS = 1e-5
RMS_EPS = 1e-6
ADAM_LR, ADAM_B1, ADAM_B2, ADAM_EPS, ADAM_WD, ADAM_STEP = 0.001, 0.9, 0.999, 1e-08, 0.01, 10

SUBLANES = 8
LANES = 128
VMEM_LIMIT = 56 * 1024 * 1024
ROW_TILE = 512
MM_SINGLE_K = 3072
D_IN_PAD = 2304
NEG = -1e30


def _cp(*sem):
    return pltpu.CompilerParams(dimension_semantics=sem if sem else None, vmem_limit_bytes=VMEM_LIMIT)


def _pick(dim, pref, align=LANES):
    if dim <= pref:
        return dim
    t = (pref // align) * align
    while t >= align:
        if dim % t == 0:
            return t
        t -= align
    return dim


def _gelu(x):
    return jax.nn.gelu(x)


def _gelu_grad(x):
    c = math.sqrt(2.0 / math.pi)
    t = jnp.tanh(c * (x + 0.044715 * x * x * x))
    return 0.5 * (1.0 + t) + 0.5 * x * (1.0 - t * t) * c * (1.0 + 3 * 0.044715 * x * x)


def _gelu_pair(x):
    c = math.sqrt(2.0 / math.pi)
    x2 = x * x
    t = jnp.tanh(c * x * (1.0 + 0.044715 * x2))
    return 0.5 * x * (1.0 + t), 0.5 * (1.0 + t) + 0.5 * x * (1.0 - t * t) * c * (1.0 + 3 * 0.044715 * x2)


def _sigmoid(x):
    return jax.nn.sigmoid(x)


def _expm1(x):
    p = 1.0 + x / 9.0
    for n in (8.0, 7.0, 6.0, 5.0, 4.0, 3.0, 2.0):
        p = 1.0 + (x / n) * p
    return jnp.where(jnp.abs(x) < 0.3, x * p, jnp.exp(x) - 1.0)


def _dot(a, b, dims):
    return lax.dot_general(a, b, (dims, ((), ())), preferred_element_type=f32)


NN = ((1,), (0,))
NT = ((1,), (1,))
TN = ((0,), (0,))


def _mm(a, b, mode, name, out_dtype=f32, tm=1024, tn=1024, tk=1024, add=None, a_win=None):
    if mode == "nn":
        (M, K), N = a.shape, b.shape[1]
    elif mode == "nt":
        (M, K), N = a.shape, b.shape[0]
    else:
        (K, M), N = a.shape, b.shape[1]
    win = 0
    if a_win is not None:
        win, w = a_win
        if mode == "tn":
            M, tm = w, w
        else:
            K = w
    single = mode != "tn" and K <= MM_SINGLE_K
    tm, tn = _pick(M, tm), _pick(N, tn)
    tk = K if single else _pick(K, tk)
    nk = K // tk
    dims = {"nn": NN, "nt": NT, "tn": TN}[mode]

    def body(a_ref, b_ref, *rest):
        prod = _dot(a_ref[...].astype(bf16), b_ref[...].astype(bf16), dims)
        if single:
            o_ref = rest[-1]
            o_ref[...] = (prod if add is None else prod + rest[0][...]).astype(o_ref.dtype)
            return
        o_ref, acc = rest[-2:]
        k = pl.program_id(2)

        @pl.when(k == 0)
        def _():
            acc[...] = prod if add is None else prod + rest[0][...]

        @pl.when(k > 0)
        def _():
            acc[...] += prod

        @pl.when(k == nk - 1)
        def _():
            o_ref[...] = acc[...].astype(o_ref.dtype)

    if mode == "tn":
        a_spec = pl.BlockSpec((tk, tm), lambda i, j, k: (k, i + win))
    else:
        a_spec = pl.BlockSpec((tm, tk), lambda i, j, k: (i, k + win))
    if mode == "nt":
        b_spec = pl.BlockSpec((tn, tk), lambda i, j, k: (j, k))
    else:
        b_spec = pl.BlockSpec((tk, tn), lambda i, j, k: (k, j))
    o_spec = pl.BlockSpec((tm, tn), lambda i, j, k: (i, j))
    return pl.pallas_call(
        body, name=name, grid=(M // tm, N // tn, nk),
        in_specs=[a_spec, b_spec] + ([] if add is None else [o_spec]), out_specs=o_spec,
        out_shape=jax.ShapeDtypeStruct((M, N), out_dtype),
        scratch_shapes=[] if single else [pltpu.VMEM((tm, tn), f32)],
        compiler_params=_cp("parallel", "parallel", "arbitrary"),
    )(*((a, b) if add is None else (a, b, add)))


def _shift_down(cur, prev8, k):
    if k == 0:
        return cur
    ext = jnp.concatenate([prev8, cur], axis=0)
    return pltpu.roll(ext, k, 0)[SUBLANES:]


def _shift_up(cur, next8, k):
    if k == 0:
        return cur
    n = cur.shape[0] + SUBLANES
    ext = jnp.concatenate([cur, next8], axis=0)
    return pltpu.roll(ext, n - k, 0)[: cur.shape[0]]


def _prev_halo_spec(rt, cols, ncolblk_fn):
    per = rt // SUBLANES
    return pl.BlockSpec((SUBLANES, cols), lambda *g: (jnp.maximum(g[-1] * per - 1, 0), ncolblk_fn(*g)))


def _ln_fwd(h, branch, g, b, name):
    S, D = h.shape
    rt = _pick(S, ROW_TILE, SUBLANES)

    def body(h_ref, m_ref, g_ref, b_ref, o_ref, z_ref):
        z = ALPHA * h_ref[...] + m_ref[...]
        mu = jnp.mean(z, axis=-1, keepdims=True)
        zc = z - mu
        var = jnp.mean(zc * zc, axis=-1, keepdims=True)
        o_ref[...] = zc * lax.rsqrt(var + LN_EPS) * g_ref[...] + b_ref[...]
        z_ref[...] = z

    row = pl.BlockSpec((rt, D), lambda i: (i, 0))
    vec = pl.BlockSpec((1, D), lambda i: (0, 0))
    return pl.pallas_call(
        body, name=name, grid=(S // rt,), in_specs=[row, row, vec, vec], out_specs=[row, row],
        out_shape=[jax.ShapeDtypeStruct((S, D), f32)] * 2, compiler_params=_cp("parallel"),
    )(h, branch, g.reshape(1, D), b.reshape(1, D))


def _ln_bwd(dy_a, dy_b, z, g, name):
    S, D = z.shape
    rt = _pick(S, ROW_TILE, SUBLANES)
    two = dy_a is not None

    def body(*refs):
        if two:
            a_ref, b_ref, z_ref, g_ref, dz_ref, acc_ref = refs
            dy = ALPHA * a_ref[...] + b_ref[...]
        else:
            b_ref, z_ref, g_ref, dz_ref, acc_ref = refs
            dy = b_ref[...]
        z = z_ref[...]
        mu = jnp.mean(z, axis=-1, keepdims=True)
        zc = z - mu
        var = jnp.mean(zc * zc, axis=-1, keepdims=True)
        rstd = lax.rsqrt(var + LN_EPS)
        xhat = zc * rstd
        dxh = dy * g_ref[...]
        m1 = jnp.mean(dxh, axis=-1, keepdims=True)
        m2 = jnp.mean(dxh * xhat, axis=-1, keepdims=True)
        dz_ref[...] = rstd * (dxh - m1 - xhat * m2)

        @pl.when(pl.program_id(0) == 0)
        def _():
            acc_ref[...] = jnp.zeros_like(acc_ref)

        acc_ref[0:1, :] += jnp.sum(dy * xhat, axis=0, keepdims=True)
        acc_ref[1:2, :] += jnp.sum(dy, axis=0, keepdims=True)

    row = pl.BlockSpec((rt, D), lambda i: (i, 0))
    vec = pl.BlockSpec((1, D), lambda i: (0, 0))
    acc = pl.BlockSpec((SUBLANES, D), lambda i: (0, 0))
    ins = ([dy_a] if two else []) + [dy_b, z, g.reshape(1, D)]
    return pl.pallas_call(
        body, name=name, grid=(S // rt,), in_specs=[row] * (len(ins) - 1) + [vec], out_specs=[row, acc],
        out_shape=[jax.ShapeDtypeStruct((S, D), f32), jax.ShapeDtypeStruct((SUBLANES, D), f32)],
        compiler_params=_cp("arbitrary"),
    )(*ins)


def _loss_head(y, target):
    S, D = y.shape
    rt = _pick(S, ROW_TILE, SUBLANES)

    def body(y_ref, t_ref, dy_ref, acc_ref):
        e = y_ref[...] - t_ref[...]
        dy_ref[...] = e * (1.0 / D)

        @pl.when(pl.program_id(0) == 0)
        def _():
            acc_ref[...] = jnp.zeros_like(acc_ref)

        part = jnp.sum(jnp.mean(e * e, axis=-1, keepdims=True), axis=0, keepdims=True)
        acc_ref[...] += 0.5 * part

    row = pl.BlockSpec((rt, D), lambda i: (i, 0))
    return pl.pallas_call(
        body, name="loss_head", grid=(S // rt,), in_specs=[row, row],
        out_specs=[row, pl.BlockSpec((1, 1), lambda i: (0, 0))],
        out_shape=[jax.ShapeDtypeStruct((S, D), f32), jax.ShapeDtypeStruct((1, 1), f32)],
        compiler_params=_cp("arbitrary"),
    )(y, target)


def _axpy(a, b, name):
    S, D = a.shape
    rt = _pick(S, ROW_TILE, SUBLANES)

    def body(a_ref, b_ref, o_ref):
        o_ref[...] = ALPHA * a_ref[...] + b_ref[...]

    row = pl.BlockSpec((rt, D), lambda i: (i, 0))
    return pl.pallas_call(
        body, name=name, grid=(S // rt,), in_specs=[row, row], out_specs=row,
        out_shape=jax.ShapeDtypeStruct((S, D), f32), compiler_params=_cp("parallel"),
    )(a, b)


def _rope_tables(S):
    rt = _pick(S, ROW_TILE, SUBLANES)

    def body(c_ref, s_ref):
        pos = (pl.program_id(0) * rt + lax.broadcasted_iota(jnp.int32, (rt, LANES), 0)).astype(f32)
        lane = lax.broadcasted_iota(jnp.int32, (rt, LANES), 1)
        j = (lane % (HEAD_DIM // 2)).astype(f32)
        inv = jnp.exp((-j * 2.0 / HEAD_DIM) * math.log(ROPE_THETA))
        ang = pos * inv
        c = jnp.cos(ang)
        s = jnp.where(lane % HEAD_DIM < HEAD_DIM // 2, -jnp.sin(ang), jnp.sin(ang))
        c_ref[...] = jnp.concatenate([c, c, c], axis=1)
        s_ref[...] = jnp.concatenate([s, s, s], axis=1)

    row = pl.BlockSpec((rt, ATTN_WIDTH), lambda i: (i, 0))
    return pl.pallas_call(
        body, name="rope_tables", grid=(S // rt,), in_specs=[], out_specs=[row, row],
        out_shape=[jax.ShapeDtypeStruct((S, ATTN_WIDTH), f32)] * 2, compiler_params=_cp("parallel"),
    )()


def _swap_halves(x):
    lane = lax.broadcasted_iota(jnp.int32, x.shape, 1)
    half = HEAD_DIM // 2
    return jnp.where(lane % HEAD_DIM < half, pltpu.roll(x, x.shape[1] - half, 1), pltpu.roll(x, half, 1))


def _rope_fwd(proj, cos, sin):
    S, W = proj.shape[0], ATTN_WIDTH
    rt = _pick(S, ROW_TILE, SUBLANES)

    def body(q_ref, k_ref, c_ref, s_ref, qo_ref, ko_ref):
        c, s = c_ref[...], s_ref[...]
        qo_ref[...] = q_ref[...] * c + _swap_halves(q_ref[...]) * s
        ko_ref[...] = k_ref[...] * c + _swap_halves(k_ref[...]) * s

    row = pl.BlockSpec((rt, W), lambda i: (i, 0))
    return pl.pallas_call(
        body, name="rope_fwd", grid=(S // rt,), in_specs=[row, pl.BlockSpec((rt, W), lambda i: (i, 1)), row, row],
        out_specs=[row, row], out_shape=[jax.ShapeDtypeStruct((S, W), f32)] * 2, compiler_params=_cp("parallel"),
    )(proj, proj, cos, sin)


def _rope_bwd(dq, dk, cos, sin):
    S, W = dq.shape
    rt = _pick(S, ROW_TILE, SUBLANES)

    def body(q_ref, k_ref, c_ref, s_ref, qo_ref, ko_ref):
        c, s = c_ref[...], s_ref[...]
        qo_ref[...] = q_ref[...] * c + _swap_halves(q_ref[...] * s)
        ko_ref[...] = k_ref[...] * c + _swap_halves(k_ref[...] * s)

    row = pl.BlockSpec((rt, W), lambda i: (i, 0))
    return pl.pallas_call(
        body, name="rope_bwd", grid=(S // rt,), in_specs=[row] * 4, out_specs=[row] * 2,
        out_shape=[jax.ShapeDtypeStruct((S, W), f32)] * 2, compiler_params=_cp("parallel"),
    )(dq, dk, cos, sin)


def _rows(ref, start, d):
    if d == 1:
        return ref[pl.ds(pl.multiple_of(start, ATTN_BLOCK), ATTN_BLOCK), :]
    return ref[pl.ds(start, ATTN_BLOCK, stride=d), :]


def _set_rows(ref, start, d, val):
    if d == 1:
        ref[pl.ds(pl.multiple_of(start, ATTN_BLOCK), ATTN_BLOCK), :] = val
    else:
        ref[pl.ds(start, ATTN_BLOCK, stride=d), :] = val


def _pair_spec(S, first_block):
    return pl.BlockSpec((S, LANES), lambda p: (0, p + first_block))


def _attn_fwd2(qr, kr, proj, shards=(), split=()):
    S = qr.shape[0]
    B = ATTN_BLOCK
    nb = S // B
    scale = HEAD_DIM ** -0.5

    gather = _Gather(shards, split)
    nt = gather.nt

    def body(*refs):
        q_ref, k_ref, v_ref = refs[:3]
        g_ins = refs[3:3 + nt]
        o_ref, l_ref = refs[3 + nt:5 + nt]
        g_outs = refs[5 + nt:5 + 2 * nt]
        m_s, l_s = refs[5 + 2 * nt:7 + 2 * nt]
        g_sems = refs[7 + 2 * nt:]
        if nt:
            @pl.when(pl.program_id(0) == 0)
            def _():
                gather.start(g_ins, g_outs, g_sems)

        qi = lax.broadcasted_iota(jnp.int32, (B, 2 * B), 0)
        ki = lax.broadcasted_iota(jnp.int32, (B, 2 * B), 1)
        dist = qi + B - ki
        band = (dist >= 0) & (dist <= B)
        for bi, d in enumerate(DILATIONS):
            bpc = nb // d

            def blk(b, carry, bi=bi, d=d, bpc=bpc):
                c, n = b // bpc, b % bpc
                start = c + d * B * n
                pstart = c + d * B * jnp.maximum(n - 1, 0)
                valid = band & ((ki >= B) | (n > 0))
                q = _rows(q_ref, start, d).astype(bf16)
                kcat = jnp.concatenate([_rows(k_ref, pstart, d), _rows(k_ref, start, d)], axis=0).astype(bf16)
                vcat = jnp.concatenate([_rows(v_ref, pstart, d), _rows(v_ref, start, d)], axis=0).astype(bf16)
                if bi > 0:
                    m_old, l_old, a_old = _rows(m_s, start, d), _rows(l_s, start, d), _rows(o_ref, start, d)
                ms, ls, accs = [], [], []
                for h in range(2):
                    sl = slice(h * HEAD_DIM, (h + 1) * HEAD_DIM)
                    c0 = h * HEAD_DIM
                    s = jnp.where(valid, _dot(q[:, sl], kcat[:, sl], NT) * scale, NEG)
                    m = jnp.max(s, axis=1, keepdims=True)
                    if bi > 0:
                        mo = m_old[:, c0:c0 + 1]
                        m = jnp.maximum(m, mo)
                        alpha = jnp.exp(mo - m)
                    p = jnp.exp(s - m)
                    l = jnp.sum(p, axis=1, keepdims=True)
                    acc = _dot(p.astype(bf16), vcat[:, sl], NN)
                    if bi > 0:
                        l = l + alpha * l_old[:, c0:c0 + 1]
                        acc = acc + alpha * a_old[:, sl]
                    ms.append(jnp.broadcast_to(m, (B, HEAD_DIM)))
                    ls.append(jnp.broadcast_to(l, (B, HEAD_DIM)))
                    accs.append(acc)
                _set_rows(m_s, start, d, jnp.concatenate(ms, axis=1))
                _set_rows(l_s, start, d, jnp.concatenate(ls, axis=1))
                _set_rows(o_ref, start, d, jnp.concatenate(accs, axis=1))
                return carry

            lax.fori_loop(0, nb, blk, 0, unroll=4)

        def fin(t, carry):
            rows = pl.ds(pl.multiple_of(t * B, B), B)
            l = l_s[rows, :]
            o_ref[rows, :] = o_ref[rows, :] / l
            l_ref[rows, :] = m_s[rows, :] + jnp.log(l)
            return carry

        lax.fori_loop(0, nb, fin, 0)
        if nt:
            @pl.when(pl.program_id(0) == pl.num_programs(0) - 1)
            def _():
                gather.finish(g_ins, g_outs, g_sems)

    pair = _pair_spec(S, 0)
    res = pl.pallas_call(
        body, name="attn_fwd_gather" if nt else "attn_fwd", grid=(3,),
        in_specs=[pair, pair, _pair_spec(S, 2 * ATTN_WIDTH // LANES)] + gather.in_specs,
        out_specs=[pair, pair] + gather.out_specs,
        out_shape=[jax.ShapeDtypeStruct((S, ATTN_WIDTH), f32)] * 2 + gather.out_shape,
        scratch_shapes=[pltpu.VMEM((S, LANES), f32)] * 2 + gather.scratch,
        compiler_params=_cp("arbitrary"),
    )(qr, kr, proj, *shards)
    return res[0], res[1], list(res[2:])


def _attn_bwd2(qr, kr, proj, dattn, ltot, delta, slabs=(), only_c=0):
    S = qr.shape[0]
    B = ATTN_BLOCK
    nb = S // B
    scale = HEAD_DIM ** -0.5
    ex = _ChipExchange(slabs, (), only_c)
    n = ex.n

    def body(*refs):
        q_ref, k_ref, v_ref, do_ref, l_ref, d_ref = refs[:6]
        x_ins = refs[6:6 + n]
        dq_ref, dk_ref, dv_ref = refs[6 + n:9 + n]
        x_outs = refs[9 + n:9 + 2 * n]
        x_sems = refs[9 + 2 * n:]
        if n:
            @pl.when(pl.program_id(0) == 0)
            def _():
                ex.start(x_ins, x_outs, x_sems)

        qi = lax.broadcasted_iota(jnp.int32, (B, 2 * B), 0)
        ki = lax.broadcasted_iota(jnp.int32, (B, 2 * B), 1)
        dist1 = qi + B - ki
        band1 = (dist1 >= 0) & (dist1 <= B)
        ri = lax.broadcasted_iota(jnp.int32, (2 * B, B), 0)
        ci = lax.broadcasted_iota(jnp.int32, (2 * B, B), 1)
        dist2 = ri - ci
        band2 = (dist2 >= 0) & (dist2 <= B)
        for bi, d in enumerate(DILATIONS):
            bpc = nb // d

            def blk(b, carry, bi=bi, d=d, bpc=bpc):
                c, n = b // bpc, b % bpc
                start = c + d * B * n
                pstart = c + d * B * jnp.maximum(n - 1, 0)
                nstart = c + d * B * jnp.minimum(n + 1, bpc - 1)
                valid1 = band1 & ((ki >= B) | (n > 0))
                valid2 = band2 & ((ri < B) | (n + 1 < bpc))
                q_c, q_n = _rows(q_ref, start, d), _rows(q_ref, nstart, d)
                k_p, k_c = _rows(k_ref, pstart, d), _rows(k_ref, start, d)
                v_p, v_c = _rows(v_ref, pstart, d), _rows(v_ref, start, d)
                do_c, do_n = _rows(do_ref, start, d), _rows(do_ref, nstart, d)
                l_c, l_n = _rows(l_ref, start, d), _rows(l_ref, nstart, d)
                d_c, d_n = _rows(d_ref, start, d), _rows(d_ref, nstart, d)
                qc = q_c.astype(bf16)
                qcat = jnp.concatenate([q_c, q_n], axis=0).astype(bf16)
                kc = k_c.astype(bf16)
                kcat = jnp.concatenate([k_p, k_c], axis=0).astype(bf16)
                vc = v_c.astype(bf16)
                vcat = jnp.concatenate([v_p, v_c], axis=0).astype(bf16)
                doc = do_c.astype(bf16)
                docat = jnp.concatenate([do_c, do_n], axis=0).astype(bf16)
                lcat = jnp.concatenate([l_c, l_n], axis=0)
                dcat = jnp.concatenate([d_c, d_n], axis=0)
                dqs, dks, dvs = [], [], []
                for h in range(2):
                    sl = slice(h * HEAD_DIM, (h + 1) * HEAD_DIM)
                    c0 = h * HEAD_DIM
                    s1 = _dot(qc[:, sl], kcat[:, sl], NT) * scale
                    p1 = jnp.where(valid1, jnp.exp(s1 - l_c[:, c0:c0 + 1]), 0.0)
                    dp1 = _dot(doc[:, sl], vcat[:, sl], NT)
                    ds1 = p1 * (dp1 - d_c[:, c0:c0 + 1]) * scale
                    dqs.append(_dot(ds1.astype(bf16), kcat[:, sl], NN))
                    s2 = _dot(qcat[:, sl], kc[:, sl], NT) * scale
                    p2 = jnp.where(valid2, jnp.exp(s2 - lcat[:, c0:c0 + 1]), 0.0)
                    dvs.append(_dot(p2.astype(bf16), docat[:, sl], TN))
                    dp2 = _dot(docat[:, sl], vc[:, sl], NT)
                    ds2 = p2 * (dp2 - dcat[:, c0:c0 + 1]) * scale
                    dks.append(_dot(ds2.astype(bf16), qcat[:, sl], TN))
                for ref, parts in ((dq_ref, dqs), (dk_ref, dks), (dv_ref, dvs)):
                    new = jnp.concatenate(parts, axis=1)
                    if bi > 0:
                        new = new + _rows(ref, start, d)
                    _set_rows(ref, start, d, new)
                return carry

            lax.fori_loop(0, nb, blk, 0, unroll=2)

        if n:
            @pl.when(pl.program_id(0) == pl.num_programs(0) - 1)
            def _():
                ex.finish(x_ins, x_outs, x_sems)

    pair = _pair_spec(S, 0)
    res = pl.pallas_call(
        body, name="attn_bwd_exchange" if n else "attn_bwd", grid=(3,),
        in_specs=[pair, pair, _pair_spec(S, 2 * ATTN_WIDTH // LANES), pair, pair, pair] + ex.in_specs,
        out_specs=[pair] * 3 + ex.out_specs,
        out_shape=[jax.ShapeDtypeStruct((S, ATTN_WIDTH), f32)] * 3 + ex.out_shape,
        scratch_shapes=ex.scratch, compiler_params=_cp("arbitrary"),
    )(qr, kr, proj, dattn, ltot, delta, *slabs)
    return res[0], res[1], res[2], list(res[3:])


def _softplus_neg(lam):
    return jnp.maximum(-lam, 0.0) + jnp.log1p(jnp.exp(-jnp.abs(lam)))


PROJ_LRU_X, PROJ_LRU_GATE, PROJ_S5_U = 3, 4, 5


def _lru_pre(proj, conv_w, conv_b, wr, br, wi, bi, lam):
    S, W = proj.shape[0], LRU_WIDTH
    rt = _pick(S, ROW_TILE, SUBLANES)
    K = conv_w.shape[0]

    def body(x_ref, xp_ref, cw_ref, cb_ref, wr_ref, br_ref, wi_ref, bi_ref, lam_ref,
             xc_ref, r_ref, i_ref, la_ref, u_ref):
        prev = jnp.where(pl.program_id(0) == 0, 0.0, xp_ref[...])
        x = x_ref[...]
        xc = cb_ref[...] + cw_ref[K - 1:K, :] * x
        for k in range(K - 1):
            xc = xc + cw_ref[k:k + 1, :] * _shift_down(x, prev, K - 1 - k)
        xb = xc.astype(bf16)
        r = _sigmoid(_dot(xb, wr_ref[...], NN) + br_ref[...])
        i = _sigmoid(_dot(xb, wi_ref[...], NN) + bi_ref[...])
        log_a = -LRU_C * r * _softplus_neg(lam_ref[...])
        u = jnp.sqrt(-_expm1(2.0 * log_a)) * (i * xc)
        xc_ref[...], r_ref[...], i_ref[...], la_ref[...], u_ref[...] = xc, r, i, log_a, u

    row = pl.BlockSpec((rt, W), lambda i: (i, 0))
    xrow = pl.BlockSpec((rt, W), lambda i: (i, PROJ_LRU_X))
    halo = _prev_halo_spec(rt, W, lambda i: PROJ_LRU_X)
    vec = pl.BlockSpec((1, W), lambda i: (0, 0))
    return pl.pallas_call(
        body, name="lru_pre", grid=(S // rt,),
        in_specs=[xrow, halo, pl.BlockSpec((K, W), lambda i: (0, 0)), vec,
                  pl.BlockSpec((W, W), lambda i: (0, 0)), vec, pl.BlockSpec((W, W), lambda i: (0, 0)), vec, vec],
        out_specs=[row] * 5, out_shape=[jax.ShapeDtypeStruct((S, W), f32)] * 5, compiler_params=_cp("parallel"),
    )(proj, proj, conv_w, conv_b.reshape(1, W), wr, br.reshape(1, W), wi, bi.reshape(1, W), lam.reshape(1, W))


def _tile_rows(shape):
    return lax.broadcasted_iota(jnp.int32, shape, 0)


def _lru_scan(log_a, u, proj):
    S, W = u.shape
    rt = _pick(S, ROW_TILE, SUBLANES)
    T = SUBLANES

    def body(la_ref, u_ref, g_ref, h_ref, o_ref, carry):
        @pl.when(pl.program_id(0) == 0)
        def _():
            carry[...] = jnp.zeros_like(carry)

        row = _tile_rows((T, W))

        def step(t, hp):
            r0 = pl.multiple_of(t * T, T)
            a = jnp.exp(la_ref[pl.ds(r0, T), :])
            x = u_ref[pl.ds(r0, T), :]
            for k in (1, 2, 4):
                x = x + a * jnp.where(row >= k, pltpu.roll(x, k, 0), 0.0)
                a = a * jnp.where(row >= k, pltpu.roll(a, k, 0), 1.0)
            h = x + a * hp
            h_ref[pl.ds(r0, T), :] = h
            o_ref[pl.ds(r0, T), :] = h * _gelu(g_ref[pl.ds(r0, T), :])
            return h[T - 1:T, :]

        carry[0:1, :] = lax.fori_loop(0, rt // T, step, carry[0:1, :])

    row = pl.BlockSpec((rt, W), lambda i: (i, 0))
    grow = pl.BlockSpec((rt, W), lambda i: (i, PROJ_LRU_GATE))
    return pl.pallas_call(
        body, name="lru_scan", grid=(S // rt,), in_specs=[row, row, grow], out_specs=[row] * 2,
        out_shape=[jax.ShapeDtypeStruct((S, W), f32)] * 2, scratch_shapes=[pltpu.VMEM((T, W), f32)],
        compiler_params=_cp("arbitrary"),
    )(log_a, u, proj)


def _lru_scan_bwd(dlru, proj, h, log_a):
    S, W = h.shape
    rt = _pick(S, ROW_TILE, SUBLANES)
    T = SUBLANES
    nblk = S // rt

    def body(d_ref, g_ref, h_ref, la_ref, go_ref, dg_ref, carry):
        @pl.when(pl.program_id(0) == 0)
        def _():
            carry[...] = jnp.zeros_like(carry)

        row = _tile_rows((T, W))

        def step(j, c):
            gn, an = c
            t = rt // T - 1 - j
            r0 = pl.multiple_of(t * T, T)
            d = d_ref[pl.ds(r0, T), :]
            gate = g_ref[pl.ds(r0, T), :]
            a = jnp.exp(la_ref[pl.ds(r0, T), :])
            dg_ref[pl.ds(r0, T), :] = d * h_ref[pl.ds(r0, T), :] * _gelu_grad(gate)
            x = d * _gelu(gate)
            b = jnp.where(row < T - 1, pltpu.roll(a, T - 1, 0), an)
            for k in (1, 2, 4):
                x = x + b * jnp.where(row < T - k, pltpu.roll(x, T - k, 0), 0.0)
                b = b * jnp.where(row < T - k, pltpu.roll(b, T - k, 0), 1.0)
            g = x + b * gn
            go_ref[pl.ds(r0, T), :] = g
            return g[0:1, :], a[0:1, :]

        gn, an = lax.fori_loop(0, rt // T, step, (carry[0:1, :], carry[1:2, :]))
        carry[0:1, :] = gn
        carry[1:2, :] = an

    row = pl.BlockSpec((rt, W), lambda i: (nblk - 1 - i, 0))
    grow = pl.BlockSpec((rt, W), lambda i: (nblk - 1 - i, PROJ_LRU_GATE))
    return pl.pallas_call(
        body, name="lru_scan_bwd", grid=(nblk,), in_specs=[row, grow, row, row], out_specs=[row] * 2,
        out_shape=[jax.ShapeDtypeStruct((S, W), f32)] * 2, scratch_shapes=[pltpu.VMEM((T, W), f32)],
        compiler_params=_cp("arbitrary"),
    )(dlru, proj, h, log_a)


def _lru_gate_bwd(g, h, xc, r, i, log_a, wr, wi, lam):
    S, W = g.shape
    rt = _pick(S, ROW_TILE, SUBLANES)

    def body(g_ref, h_ref, hp_ref, xc_ref, r_ref, i_ref, la_ref, wr_ref, wi_ref, lam_ref,
             dxc_ref, dwr_ref, dwi_ref, acc_ref):
        @pl.when(pl.program_id(0) == 0)
        def _():
            dwr_ref[...] = jnp.zeros_like(dwr_ref)
            dwi_ref[...] = jnp.zeros_like(dwi_ref)
            acc_ref[...] = jnp.zeros_like(acc_ref)

        prev = jnp.where(pl.program_id(0) == 0, 0.0, hp_ref[...])
        gg, xc, r, i, log_a, lam = g_ref[...], xc_ref[...], r_ref[...], i_ref[...], la_ref[...], lam_ref[...]
        hm1 = _shift_down(h_ref[...], prev, 1)
        a = jnp.exp(log_a)
        s = jnp.sqrt(-_expm1(2.0 * log_a))
        da = gg * hm1
        di = gg * s * xc
        dxc = gg * s * i
        ds = gg * i * xc
        dlog_a = da * a - ds * (a * a / s)
        sp = _softplus_neg(lam)
        dr = dlog_a * (-LRU_C * sp)
        dsp = jnp.sum(dlog_a * (-LRU_C * r), axis=0, keepdims=True)
        dpr = dr * r * (1.0 - r)
        dpi = di * i * (1.0 - i)
        dprb, dpib, xb = dpr.astype(bf16), dpi.astype(bf16), xc.astype(bf16)
        dxc_ref[...] = dxc + _dot(dprb, wr_ref[...], NT) + _dot(dpib, wi_ref[...], NT)
        dwr_ref[...] += _dot(xb, dprb, TN)
        dwi_ref[...] += _dot(xb, dpib, TN)
        acc_ref[0:1, :] += jnp.sum(dpr, axis=0, keepdims=True)
        acc_ref[1:2, :] += jnp.sum(dpi, axis=0, keepdims=True)
        acc_ref[2:3, :] += dsp * (-_sigmoid(-lam))

    row = pl.BlockSpec((rt, W), lambda i: (i, 0))
    halo = _prev_halo_spec(rt, W, lambda i: 0)
    vec = pl.BlockSpec((1, W), lambda i: (0, 0))
    mat = pl.BlockSpec((W, W), lambda i: (0, 0))
    acc = pl.BlockSpec((SUBLANES, W), lambda i: (0, 0))
    return pl.pallas_call(
        body, name="lru_gate_bwd", grid=(S // rt,),
        in_specs=[row, row, halo, row, row, row, row, mat, mat, vec], out_specs=[row, mat, mat, acc],
        out_shape=[jax.ShapeDtypeStruct((S, W), f32), jax.ShapeDtypeStruct((W, W), f32),
                   jax.ShapeDtypeStruct((W, W), f32), jax.ShapeDtypeStruct((SUBLANES, W), f32)],
        compiler_params=_cp("arbitrary"),
    )(g, h, h, xc, r, i, log_a, wr, wi, lam.reshape(1, W))


def _conv_bwd(dy, x, conv_w, name, col_tile=None, out_dtype=f32, x_col_block=0):
    if dy.ndim == 2:
        dy = dy[None]
    H, S, Ch = dy.shape
    C = H * Ch
    K = conv_w.shape[0]
    ct = Ch if col_tile is None else col_tile
    nct = Ch // ct
    rt = _pick(S, ROW_TILE, SUBLANES)
    nrt = S // rt

    def body(dy_ref, dyn_ref, x_ref, w_ref, dx_ref, acc_ref):
        i = pl.program_id(2)

        @pl.when(i == 0)
        def _():
            acc_ref[...] = jnp.zeros_like(acc_ref)

        nxt = jnp.where(i == nrt - 1, 0.0, dyn_ref[...])
        dy, x = dy_ref[...], x_ref[...]
        ahead = [dy] + [_shift_up(dy, nxt, j) for j in range(1, K)]
        dx = w_ref[K - 1:K, :] * dy
        for k in range(K - 1):
            dx = dx + w_ref[k:k + 1, :] * ahead[K - 1 - k]
        dx_ref[...] = dx.astype(dx_ref.dtype)
        for k in range(K):
            acc_ref[k:k + 1, :] += jnp.sum(ahead[K - 1 - k] * x, axis=0, keepdims=True)
        acc_ref[K:K + 1, :] += jnp.sum(dy, axis=0, keepdims=True)

    per, last = rt // SUBLANES, S // SUBLANES - 1
    dy_row = pl.BlockSpec((None, rt, ct), lambda h, j, i: (h, i, j))
    dy_next = pl.BlockSpec((None, SUBLANES, ct), lambda h, j, i: (h, jnp.minimum((i + 1) * per, last), j))
    row = pl.BlockSpec((rt, ct), lambda h, j, i: (i, h * nct + j))
    xrow = pl.BlockSpec((rt, ct), lambda h, j, i: (i, h * nct + j + x_col_block))
    return pl.pallas_call(
        body, name=name, grid=(H, nct, nrt),
        in_specs=[dy_row, dy_next, xrow, pl.BlockSpec((K, ct), lambda h, j, i: (0, h * nct + j))],
        out_specs=[row, pl.BlockSpec((SUBLANES, ct), lambda h, j, i: (0, h * nct + j))],
        out_shape=[jax.ShapeDtypeStruct((S, C), out_dtype), jax.ShapeDtypeStruct((SUBLANES, C), f32)],
        compiler_params=_cp("parallel", "parallel", "arbitrary"),
    )(dy, dy, x, conv_w)


def _s5_param_fn(a_re, a_im, ls, bt_re, bt_im):
    step = jnp.exp(ls)
    dt_re, dt_im = step * a_re, step * a_im
    mag = jnp.exp(dt_re)
    ab_re, ab_im = mag * jnp.cos(dt_im), mag * jnp.sin(dt_im)
    z_re, z_im = ab_re - 1.0, ab_im
    den = a_re * a_re + a_im * a_im
    f_re = (z_re * a_re + z_im * a_im) / den
    f_im = (z_im * a_re - z_re * a_im) / den
    bb_re = f_re[:, None, :] * bt_re - f_im[:, None, :] * bt_im
    bb_im = f_re[:, None, :] * bt_im + f_im[:, None, :] * bt_re
    return ab_re, ab_im, bb_re, bb_im


def _s5_params(a_re, a_im, ls, bt_re, bt_im):
    def body(ar, ai, l, br, bi, o_ar, o_ai, o_br, o_bi):
        o_ar[...], o_ai[...], o_br[...], o_bi[...] = _s5_param_fn(ar[...], ai[...], l[...], br[...], bi[...])

    return pl.pallas_call(
        body, name="s5_params",
        out_shape=[jax.ShapeDtypeStruct(a_re.shape, f32)] * 2 + [jax.ShapeDtypeStruct(bt_re.shape, f32)] * 2,
        compiler_params=_cp(),
    )(a_re, a_im, ls, bt_re, bt_im)


def _s5_params_bwd(a_re, a_im, ls, bt_re, bt_im, d_ar, d_ai, d_br, d_bi):
    def body(ar, ai, l, br, bi, c_ar, c_ai, c_br, c_bi, g_ar, g_ai, g_l, g_br, g_bi):
        _, vjp = jax.vjp(_s5_param_fn, ar[...], ai[...], l[...], br[...], bi[...])
        g_ar[...], g_ai[...], g_l[...], g_br[...], g_bi[...] = vjp((c_ar[...], c_ai[...], c_br[...], c_bi[...]))

    return pl.pallas_call(
        body, name="s5_params_bwd",
        out_shape=[jax.ShapeDtypeStruct(a_re.shape, f32)] * 2 + [jax.ShapeDtypeStruct(ls.shape, f32)]
        + [jax.ShapeDtypeStruct(bt_re.shape, f32)] * 2,
        compiler_params=_cp(),
    )(a_re, a_im, ls, bt_re, bt_im, d_ar, d_ai, d_br, d_bi)


S5_CHUNK = 256


def _s5_power_tables(ab_ref, p_ref, w_ref, conj):
    T, L = SUBLANES, S5_LANES
    are = ab_ref[0:1, 0:L]
    aim = ab_ref[0:1, L:2 * L]
    if conj:
        aim = -aim
    pre, pim = are, aim
    for n in range(3):
        p_ref[n:n + 1, 0:L] = pre
        p_ref[n:n + 1, L:2 * L] = pim
        pre, pim = pre * pre - pim * pim, 2.0 * pre * pim
    row = _tile_rows((T, L))
    wre = jnp.zeros((T, L), f32)
    wim = jnp.zeros((T, L), f32)
    pre, pim = are, aim
    for n in range(T):
        tgt = (T - 1 - n) if conj else n
        wre = jnp.where(row == tgt, pre, wre)
        wim = jnp.where(row == tgt, pim, wim)
        pre, pim = pre * are - pim * aim, pre * aim + pim * are
    w_ref[:, 0:L] = wre
    w_ref[:, L:2 * L] = wim


def _s5_scan(bu, ab):
    S, L2 = bu.shape
    L = L2 // 2
    rt = _pick(S, 256, SUBLANES)
    T = SUBLANES
    CH = S5_CHUNK

    def body(bu_ref, ab_ref, x_ref, p_ref, w_ref, carry):
        @pl.when(pl.program_id(0) == 0)
        def _():
            carry[...] = jnp.zeros_like(carry)
            _s5_power_tables(ab_ref, p_ref, w_ref, conj=False)

        row = _tile_rows((T, CH))

        def step(t, _):
            r0 = pl.multiple_of(t * T, T)
            for c in range(L // CH):
                lre, lim = pl.ds(c * CH, CH), pl.ds(L + c * CH, CH)
                xr, xi = bu_ref[pl.ds(r0, T), lre], bu_ref[pl.ds(r0, T), lim]
                for n, k in enumerate((1, 2, 4)):
                    pr, pi = p_ref[n:n + 1, lre], p_ref[n:n + 1, lim]
                    sr = jnp.where(row >= k, pltpu.roll(xr, k, 0), 0.0)
                    si = jnp.where(row >= k, pltpu.roll(xi, k, 0), 0.0)
                    xr, xi = xr + pr * sr - pi * si, xi + pr * si + pi * sr
                cr, ci = carry[T - 1:T, lre], carry[T - 1:T, lim]
                wr, wi = w_ref[:, lre], w_ref[:, lim]
                xr, xi = xr + wr * cr - wi * ci, xi + wr * ci + wi * cr
                carry[:, lre] = xr
                carry[:, lim] = xi
                x_ref[pl.ds(r0, T), lre] = xr
                x_ref[pl.ds(r0, T), lim] = xi
            return 0

        lax.fori_loop(0, rt // T, step, 0)

    row_spec = pl.BlockSpec((rt, L2), lambda i: (i, 0))
    return pl.pallas_call(
        body, name="s5_scan", grid=(S // rt,), in_specs=[row_spec, pl.BlockSpec((1, L2), lambda i: (0, 0))],
        out_specs=row_spec, out_shape=jax.ShapeDtypeStruct((S, L2), f32),
        scratch_shapes=[pltpu.VMEM((T, L2), f32), pltpu.VMEM((T, L2), f32), pltpu.VMEM((T, L2), f32)],
        compiler_params=_cp("arbitrary"),
    )(bu, ab)


def _s5_scan_bwd(dx, x, ab):
    S, L2 = dx.shape
    L = L2 // 2
    rt = _pick(S, 256, SUBLANES)
    T = SUBLANES
    CH = S5_CHUNK
    nblk = S // rt
    per = rt // T

    def body(dx_ref, x_ref, xp_ref, ab_ref, g_ref, da_ref, p_ref, w_ref, carry, acc):
        pid = pl.program_id(0)

        @pl.when(pid == 0)
        def _():
            carry[...] = jnp.zeros_like(carry)
            acc[...] = jnp.zeros_like(acc)
            _s5_power_tables(ab_ref, p_ref, w_ref, conj=True)

        row = _tile_rows((T, CH))
        first_block = pid == nblk - 1

        def step(j, _):
            t = per - 1 - j
            r0 = pl.multiple_of(t * T, T)
            rp = pl.multiple_of(jnp.maximum(t - 1, 0) * T, T)
            for c in range(L // CH):
                lre, lim = pl.ds(c * CH, CH), pl.ds(L + c * CH, CH)
                gr, gi = dx_ref[pl.ds(r0, T), lre], dx_ref[pl.ds(r0, T), lim]
                for n, k in enumerate((1, 2, 4)):
                    pr, pi = p_ref[n:n + 1, lre], p_ref[n:n + 1, lim]
                    sr = jnp.where(row < T - k, pltpu.roll(gr, T - k, 0), 0.0)
                    si = jnp.where(row < T - k, pltpu.roll(gi, T - k, 0), 0.0)
                    gr, gi = gr + pr * sr - pi * si, gi + pr * si + pi * sr
                cr, ci = carry[0:1, lre], carry[0:1, lim]
                wr, wi = w_ref[:, lre], w_ref[:, lim]
                gr, gi = gr + wr * cr - wi * ci, gi + wr * ci + wi * cr
                carry[:, lre] = gr
                carry[:, lim] = gi
                g_ref[pl.ds(r0, T), lre] = gr
                g_ref[pl.ds(r0, T), lim] = gi
                xr, xi = x_ref[pl.ds(r0, T), lre], x_ref[pl.ds(r0, T), lim]
                in_blk_r, in_blk_i = x_ref[pl.ds(rp, T), lre], x_ref[pl.ds(rp, T), lim]
                hal_r = jnp.where(first_block, 0.0, xp_ref[:, lre])
                hal_i = jnp.where(first_block, 0.0, xp_ref[:, lim])
                pvr = jnp.where(t == 0, hal_r, in_blk_r)[T - 1:T, :]
                pvi = jnp.where(t == 0, hal_i, in_blk_i)[T - 1:T, :]
                sxr = jnp.where(row >= 1, pltpu.roll(xr, 1, 0), pvr)
                sxi = jnp.where(row >= 1, pltpu.roll(xi, 1, 0), pvi)
                acc[:, lre] += gr * sxr + gi * sxi
                acc[:, lim] += gi * sxr - gr * sxi
            return 0

        lax.fori_loop(0, per, step, 0)

        @pl.when(pid == nblk - 1)
        def _():
            da_ref[...] = jnp.sum(acc[...], axis=0, keepdims=True)

    row_spec = pl.BlockSpec((rt, L2), lambda i: (nblk - 1 - i, 0))
    halo = pl.BlockSpec((T, L2), lambda i: (jnp.maximum((nblk - 1 - i) * per - 1, 0), 0))
    vec = pl.BlockSpec((1, L2), lambda i: (0, 0))
    return pl.pallas_call(
        body, name="s5_scan_bwd", grid=(nblk,), in_specs=[row_spec, row_spec, halo, vec],
        out_specs=[row_spec, vec],
        out_shape=[jax.ShapeDtypeStruct((S, L2), f32), jax.ShapeDtypeStruct((1, L2), f32)],
        scratch_shapes=[pltpu.VMEM((T, L2), f32)] * 4,
        compiler_params=_cp("arbitrary"),
    )(dx, x, x, ab)


def _S5_U_SPEC(rt):
    return pl.BlockSpec((rt, LRU_WIDTH), lambda i: (i, PROJ_S5_U))


def _s5_out(yc, proj, d, wglu, bglu):
    S, W = yc.shape
    rt = _pick(S, ROW_TILE, SUBLANES)

    def body(yc_ref, u_ref, d_ref, w_ref, b_ref, o_ref, y_ref):
        y = yc_ref[...] + d_ref[...] * u_ref[:, 0:W]
        yg = _gelu(y)
        z = _dot(yg.astype(bf16), w_ref[...], NN) + b_ref[...]
        o_ref[...] = yg * _sigmoid(z)
        y_ref[...] = y

    row = pl.BlockSpec((rt, W), lambda i: (i, 0))
    vec = pl.BlockSpec((1, W), lambda i: (0, 0))
    mat = pl.BlockSpec((W, W), lambda i: (0, 0))
    return pl.pallas_call(
        body, name="s5_out", grid=(S // rt,), in_specs=[row, _S5_U_SPEC(rt), vec, mat, vec], out_specs=[row, row],
        out_shape=[jax.ShapeDtypeStruct((S, W), f32)] * 2, compiler_params=_cp("parallel"),
    )(yc, proj, d.reshape(1, W), wglu, bglu.reshape(1, W))


def _s5_out_bwd(dssm, y, proj, d, wglu, bglu):
    S, W = y.shape
    rt = _pick(S, ROW_TILE, SUBLANES)

    def body(do_ref, y_ref, u_ref, d_ref, w_ref, b_ref, dy_ref, du_ref, dw_ref, acc_ref):
        @pl.when(pl.program_id(0) == 0)
        def _():
            dw_ref[...] = jnp.zeros_like(dw_ref)
            acc_ref[...] = jnp.zeros_like(acc_ref)

        do, y = do_ref[...], y_ref[...]
        yg = _gelu(y)
        ygb = yg.astype(bf16)
        sg = _sigmoid(_dot(ygb, w_ref[...], NN) + b_ref[...])
        dz = do * yg * sg * (1.0 - sg)
        dzb = dz.astype(bf16)
        dyg = do * sg + _dot(dzb, w_ref[...], NT)
        dy = dyg * _gelu_grad(y)
        dy_ref[...] = dy
        du_ref[...] = dy * d_ref[...]
        dw_ref[...] += _dot(ygb, dzb, TN)
        acc_ref[0:1, :] += jnp.sum(dz, axis=0, keepdims=True)
        acc_ref[1:2, :] += jnp.sum(dy * u_ref[:, 0:W], axis=0, keepdims=True)

    row = pl.BlockSpec((rt, W), lambda i: (i, 0))
    vec = pl.BlockSpec((1, W), lambda i: (0, 0))
    mat = pl.BlockSpec((W, W), lambda i: (0, 0))
    acc = pl.BlockSpec((SUBLANES, W), lambda i: (0, 0))
    return pl.pallas_call(
        body, name="s5_out_bwd", grid=(S // rt,), in_specs=[row, row, _S5_U_SPEC(rt), vec, mat, vec],
        out_specs=[row, row, mat, acc],
        out_shape=[jax.ShapeDtypeStruct((S, W), f32)] * 2
        + [jax.ShapeDtypeStruct((W, W), f32), jax.ShapeDtypeStruct((SUBLANES, W), f32)],
        compiler_params=_cp("arbitrary"),
    )(dssm, y, proj, d.reshape(1, W), wglu, bglu.reshape(1, W))


MIX_SPLITS = ((0, ATTN_WIDTH), (ATTN_WIDTH, ATTN_WIDTH + LRU_WIDTH), (ATTN_WIDTH + LRU_WIDTH, D_MODEL))


def _mixnorm(attn, lru, ssm, g):
    S = attn.shape[0]
    rt = _pick(S, ROW_TILE, SUBLANES)

    def body(a_ref, l_ref, s_ref, g_ref, o_ref):
        for ref, (lo, hi) in zip((a_ref, l_ref, s_ref), MIX_SPLITS):
            x = ref[...]
            ms = jnp.mean(x * x, axis=-1, keepdims=True)
            o_ref[:, lo:hi] = (x * lax.rsqrt(ms + RMS_EPS) * g_ref[:, lo:hi]).astype(o_ref.dtype)

    rows = [pl.BlockSpec((rt, hi - lo), lambda i: (i, 0)) for lo, hi in MIX_SPLITS]
    return pl.pallas_call(
        body, name="mixnorm", grid=(S // rt,), in_specs=rows + [pl.BlockSpec((1, D_MODEL), lambda i: (0, 0))],
        out_specs=pl.BlockSpec((rt, D_MODEL), lambda i: (i, 0)),
        out_shape=jax.ShapeDtypeStruct((S, D_MODEL), bf16), compiler_params=_cp("parallel"),
    )(attn, lru, ssm, g.reshape(1, D_MODEL))


def _mixnorm_bwd(dmixed, attn, lru, ssm, g):
    S = attn.shape[0]
    rt = _pick(S, ROW_TILE, SUBLANES)

    def body(d_ref, a_ref, l_ref, s_ref, g_ref, da_ref, dl_ref, ds_ref, dlt_ref, acc_ref):
        @pl.when(pl.program_id(0) == 0)
        def _():
            acc_ref[...] = jnp.zeros_like(acc_ref)

        outs = []
        for ref, (lo, hi) in zip((a_ref, l_ref, s_ref), MIX_SPLITS):
            x = ref[...]
            dy = d_ref[:, lo:hi]
            rinv = lax.rsqrt(jnp.mean(x * x, axis=-1, keepdims=True) + RMS_EPS)
            dyg = dy * g_ref[:, lo:hi]
            outs.append(rinv * dyg - x * (rinv * rinv * rinv) * jnp.mean(dyg * x, axis=-1, keepdims=True))
            acc_ref[0:1, lo:hi] += jnp.sum(dy * x * rinv, axis=0, keepdims=True)
        da_ref[...], dl_ref[...], ds_ref[...] = outs
        hi_ = lax.broadcasted_iota(jnp.int32, (ATTN_WIDTH, ATTN_WIDTH), 0) // HEAD_DIM
        hj_ = lax.broadcasted_iota(jnp.int32, (ATTN_WIDTH, ATTN_WIDTH), 1) // HEAD_DIM
        same = jnp.where(hi_ == hj_, 1.0, 0.0).astype(f32)
        dlt_ref[...] = jnp.dot(outs[0] * a_ref[...], same, precision=lax.Precision.HIGHEST, preferred_element_type=f32)

    rows = [pl.BlockSpec((rt, hi - lo), lambda i: (i, 0)) for lo, hi in MIX_SPLITS]
    full = pl.BlockSpec((rt, D_MODEL), lambda i: (i, 0))
    return pl.pallas_call(
        body, name="mixnorm_bwd", grid=(S // rt,),
        in_specs=[full] + rows + [pl.BlockSpec((1, D_MODEL), lambda i: (0, 0))],
        out_specs=rows + [rows[0], pl.BlockSpec((SUBLANES, D_MODEL), lambda i: (0, 0))],
        out_shape=[jax.ShapeDtypeStruct((S, hi - lo), f32) for lo, hi in MIX_SPLITS]
        + [jax.ShapeDtypeStruct((S, ATTN_WIDTH), f32), jax.ShapeDtypeStruct((SUBLANES, D_MODEL), f32)],
        compiler_params=_cp("arbitrary"),
    )(dmixed, attn, lru, ssm, g.reshape(1, D_MODEL))


FFN_COL_TILE = 512


def _ffn_conv(x, prev, w_ref, b_ref, K):
    y = b_ref[...] + w_ref[K - 1:K, :] * x
    for k in range(K - 1):
        y = y + w_ref[k:k + 1, :] * _shift_down(x, prev, K - 1 - k)
    return y


def _ffn_act(up, conv_w, conv_b):
    S, C2 = up.shape
    C = C2 // 2
    K = conv_w.shape[0]
    ct = FFN_COL_TILE
    nct = C // ct
    rt = _pick(S, ROW_TILE, SUBLANES)

    def body(g_ref, gp_ref, v_ref, vp_ref, wg_ref, wv_ref, bg_ref, bv_ref, o_ref):
        first = pl.program_id(1) == 0
        gate = _ffn_conv(g_ref[...], jnp.where(first, 0.0, gp_ref[...]), wg_ref, bg_ref, K)
        val = _ffn_conv(v_ref[...], jnp.where(first, 0.0, vp_ref[...]), wv_ref, bv_ref, K)
        o_ref[...] = (_gelu(gate) * val).astype(o_ref.dtype)

    def specs(off):
        return (pl.BlockSpec((rt, ct), lambda j, i: (i, j + off)), _prev_halo_spec(rt, ct, lambda j, i: j + off))

    def wspec(off, rows):
        return pl.BlockSpec((rows, ct), lambda j, i: (0, j + off))

    g_s, gp_s = specs(0)
    v_s, vp_s = specs(nct)
    return pl.pallas_call(
        body, name="ffn_act", grid=(nct, S // rt),
        in_specs=[g_s, gp_s, v_s, vp_s, wspec(0, K), wspec(nct, K), wspec(0, 1), wspec(nct, 1)],
        out_specs=pl.BlockSpec((rt, ct), lambda j, i: (i, j)),
        out_shape=jax.ShapeDtypeStruct((S, C), bf16), compiler_params=_cp("parallel", "parallel"),
    )(up, up, up, up, conv_w, conv_w, conv_b.reshape(1, C2), conv_b.reshape(1, C2))


def _ffn_act_bwd(dact, up, conv_w, conv_b):
    S, C2 = up.shape
    C = C2 // 2
    K = conv_w.shape[0]
    ct = FFN_COL_TILE
    nct = C // ct
    rt = _pick(S, ROW_TILE, SUBLANES)

    def body(d_ref, g_ref, gp_ref, v_ref, vp_ref, wg_ref, wv_ref, bg_ref, bv_ref, o_ref):
        first = pl.program_id(1) == 0
        gate = _ffn_conv(g_ref[...], jnp.where(first, 0.0, gp_ref[...]), wg_ref, bg_ref, K)
        val = _ffn_conv(v_ref[...], jnp.where(first, 0.0, vp_ref[...]), wv_ref, bv_ref, K)
        d = d_ref[...]
        gl, dgl = _gelu_pair(gate)
        o_ref[0] = d * val * dgl
        o_ref[1] = d * gl

    def specs(off):
        return (pl.BlockSpec((rt, ct), lambda j, i: (i, j + off)), _prev_halo_spec(rt, ct, lambda j, i: j + off))

    def wspec(off, rows):
        return pl.BlockSpec((rows, ct), lambda j, i: (0, j + off))

    g_s, gp_s = specs(0)
    v_s, vp_s = specs(nct)
    return pl.pallas_call(
        body, name="ffn_act_bwd", grid=(nct, S // rt),
        in_specs=[pl.BlockSpec((rt, ct), lambda j, i: (i, j)), g_s, gp_s, v_s, vp_s,
                  wspec(0, K), wspec(nct, K), wspec(0, 1), wspec(nct, 1)],
        out_specs=pl.BlockSpec((2, rt, ct), lambda j, i: (0, i, j)),
        out_shape=jax.ShapeDtypeStruct((2, S, C), f32), compiler_params=_cp("parallel", "parallel"),
    )(dact, up, up, up, up, conv_w, conv_w, conv_b.reshape(1, C2), conv_b.reshape(1, C2))


ANY = pl.BlockSpec(memory_space=pl.ANY)


def _rows_for(cols):
    return max(16, (1 << 17) // cols)


def _chips(x, y):
    return [(1 - x, y), (x, 1 - y), (1 - x, 1 - y)]


class _Gather:
    def __init__(self, shards, split):
        self.shapes = [s.shape for s in shards]
        self.dtypes = [s.dtype for s in shards]
        self.split = list(split)
        self.nt = len(shards)
        self.in_specs = [ANY] * self.nt
        self.out_specs = [ANY] * self.nt
        self.out_shape = [jax.ShapeDtypeStruct((4,) + s, dt) for s, dt in zip(self.shapes, self.dtypes)]
        self.scratch = [pltpu.SemaphoreType.DMA((3, self.nt))] * 4 if self.nt else []

    def _part(self, ref, t, half):
        if not self.split[t]:
            return ref
        r = self.shapes[t][0] // 2
        return ref.at[pl.ds(half * r, r), :]

    def _ici(self, ins, outs, sems, k, t, chip, landing_chip):
        x, y, c = lax.axis_index("x"), lax.axis_index("y"), lax.axis_index("c")
        return pltpu.make_async_remote_copy(
            src_ref=self._part(ins[t], t, c), dst_ref=self._part(outs[t].at[landing_chip], t, c),
            send_sem=sems[0].at[k, t], recv_sem=sems[1].at[k, t], device_id=(chip[0], chip[1], c), device_id_type=MESH)

    def _d2d(self, outs, sems, k, t, q, half):
        x, y, c = lax.axis_index("x"), lax.axis_index("y"), lax.axis_index("c")
        rows = self._part(outs[t].at[q], t, half)
        return pltpu.make_async_remote_copy(
            src_ref=rows, dst_ref=rows, send_sem=sems[2].at[k, t], recv_sem=sems[3].at[k, t],
            device_id=(x, y, 1 - c), device_id_type=MESH)

    def start(self, ins, outs, sems):
        x, y = lax.axis_index("x"), lax.axis_index("y")
        me = 2 * x + y
        for k, chip in enumerate(_chips(x, y)):
            for t in range(self.nt):
                self._ici(ins, outs, sems, k, t, chip, me).start()

    def finish(self, ins, outs, sems):
        x, y, c = lax.axis_index("x"), lax.axis_index("y"), lax.axis_index("c")
        me = 2 * x + y
        chips = _chips(x, y)
        for k, chip in enumerate(chips):
            q = 2 * chip[0] + chip[1]
            for t in range(self.nt):
                self._ici(ins, outs, sems, k, t, chip, q).wait_recv()
                if self.split[t]:
                    self._d2d(outs, sems, k, t, q, c).start()
        for k, chip in enumerate(chips):
            q = 2 * chip[0] + chip[1]
            for t in range(self.nt):
                if self.split[t]:
                    self._d2d(outs, sems, k, t, q, 1 - c).wait_recv()
        for k, chip in enumerate(chips):
            q = 2 * chip[0] + chip[1]
            for t in range(self.nt):
                self._ici(ins, outs, sems, k, t, chip, me).wait_send()
                if self.split[t]:
                    self._d2d(outs, sems, k, t, q, c).wait_send()


def _gather_weights(shards, split):
    g = _Gather(shards, split)
    nt = g.nt

    def body(*refs):
        ins, outs, sems = refs[:nt], refs[nt:2 * nt], refs[2 * nt:]
        g.start(ins, outs, sems)
        g.finish(ins, outs, sems)

    return pl.pallas_call(
        body, name="gather_weights", in_specs=g.in_specs, out_specs=g.out_specs, out_shape=g.out_shape,
        scratch_shapes=g.scratch,
    )(*shards)


def _sibling_send(gs, dst_c, swap=()):
    nt, n = len(gs), len(gs) + len(swap)

    def body(*refs):
        ins, outs = refs[:n], refs[n:2 * n]
        send, recv = refs[2 * n:]
        x, y, c = lax.axis_index("x"), lax.axis_index("y"), lax.axis_index("c")

        def cp(t):
            return pltpu.make_async_remote_copy(
                src_ref=ins[t], dst_ref=outs[t], send_sem=send.at[t], recv_sem=recv.at[t],
                device_id=(x, y, 1 - c), device_id_type=MESH)

        def one_way(fn):
            if nt:
                return [fn(t) for t in range(nt)]

        @pl.when(c != dst_c)
        def _():
            one_way(lambda t: cp(t).start())

        for t in range(nt, n):
            cp(t).start()

        @pl.when(c == dst_c)
        def _():
            one_way(lambda t: cp(t).wait_recv())

        for t in range(nt, n):
            cp(t).wait_recv()

        @pl.when(c != dst_c)
        def _():
            one_way(lambda t: cp(t).wait_send())

        for t in range(nt, n):
            cp(t).wait_send()

    res = pl.pallas_call(
        body, name="sibling_send", in_specs=[ANY] * n, out_specs=[ANY] * n,
        out_shape=[jax.ShapeDtypeStruct(g.shape, g.dtype) for g in list(gs) + list(swap)],
        scratch_shapes=[pltpu.SemaphoreType.DMA((n,))] * 2,
    )(*gs, *swap)
    return list(res[:nt]), list(res[nt:])


def _pair_sum(g, other, name):
    R, C = g.shape
    rt = _pick(R, _rows_for(C), 16)

    def body(a_ref, o_ref, out_ref):
        out_ref[...] = (a_ref[...] + o_ref[...]).astype(out_ref.dtype)

    row = pl.BlockSpec((rt, C), lambda i: (i, 0))
    return pl.pallas_call(
        body, name=name, grid=(R // rt,), in_specs=[row, row], out_specs=row,
        out_shape=jax.ShapeDtypeStruct((R, C), bf16), compiler_params=_cp("parallel"),
    )(g, other)


class _ChipExchange:
    def __init__(self, slabs, whole, only_c):
        self.ns, self.nw, self.only_c = len(slabs), len(whole), only_c
        self.n = self.ns + self.nw
        self.in_specs = [ANY] * self.n
        self.out_specs = [ANY] * self.n
        self.out_shape = ([jax.ShapeDtypeStruct(s.shape, s.dtype) for s in slabs]
                          + [jax.ShapeDtypeStruct((4,) + w.shape, w.dtype) for w in whole])
        self.scratch = [pltpu.SemaphoreType.DMA((3, self.n))] * 2 if self.n else []

    def _copy(self, ins, outs, sems, k, t, chip, landing_chip):
        c = lax.axis_index("c")
        src = ins[t].at[2 * chip[0] + chip[1]] if t < self.ns else ins[t]
        return pltpu.make_async_remote_copy(
            src_ref=src, dst_ref=outs[t].at[landing_chip], send_sem=sems[0].at[k, t], recv_sem=sems[1].at[k, t],
            device_id=(chip[0], chip[1], c), device_id_type=MESH)

    def _each(self, fn):
        x, y, c = lax.axis_index("x"), lax.axis_index("y"), lax.axis_index("c")
        chips = _chips(x, y)
        if self.ns:
            @pl.when(c == self.only_c)
            def _():
                for k, chip in enumerate(chips):
                    for t in range(self.ns):
                        fn(k, t, chip)
        for k, chip in enumerate(chips):
            for t in range(self.ns, self.n):
                fn(k, t, chip)

    def start(self, ins, outs, sems):
        me = 2 * lax.axis_index("x") + lax.axis_index("y")
        self._each(lambda k, t, chip: self._copy(ins, outs, sems, k, t, chip, me).start())

    def finish(self, ins, outs, sems):
        me = 2 * lax.axis_index("x") + lax.axis_index("y")
        self._each(lambda k, t, chip: self._copy(ins, outs, sems, k, t, chip, 2 * chip[0] + chip[1]).wait_recv())
        self._each(lambda k, t, chip: self._copy(ins, outs, sems, k, t, chip, me).wait_send())


def _chip_exchange(slabs, whole, only_c):
    ex = _ChipExchange(slabs, whole, only_c)
    n = ex.n

    def body(*refs):
        ins, outs, sems = refs[:n], refs[n:2 * n], refs[2 * n:]
        ex.start(ins, outs, sems)
        ex.finish(ins, outs, sems)

    res = pl.pallas_call(
        body, name="chip_exchange", in_specs=ex.in_specs, out_specs=ex.out_specs, out_shape=ex.out_shape,
        scratch_shapes=ex.scratch,
    )(*slabs, *whole)
    return list(res[:ex.ns]), list(res[ex.ns:])


def _sum_chips(recv, own, name):
    n, r, C = recv.shape
    rt = _pick(r, _rows_for(C), 16)
    own3 = own.ndim == 3

    def body(r_ref, o_ref, out_ref):
        me = 2 * lax.axis_index("x") + lax.axis_index("y")
        acc = None
        for q in range(n):
            term = jnp.where(me == q, o_ref[q] if own3 else o_ref[...], r_ref[q]).astype(f32)
            acc = term if acc is None else acc + term
        out_ref[...] = acc

    blk = pl.BlockSpec((n, rt, C), lambda i: (0, i, 0))
    return pl.pallas_call(
        body, name=name, grid=(r // rt,), in_specs=[blk, blk if own3 else pl.BlockSpec((rt, C), lambda i: (i, 0))],
        out_specs=pl.BlockSpec((rt, C), lambda i: (i, 0)), out_shape=jax.ShapeDtypeStruct((r, C), f32),
        compiler_params=_cp("parallel"),
    )(recv, own)


def _pair_swap(parts0, parts1):
    nt = len(parts0)

    def body(*refs):
        in0, in1, outs = refs[:nt], refs[nt:2 * nt], refs[2 * nt:3 * nt]
        send, recv = refs[3 * nt:]
        x, y, c = lax.axis_index("x"), lax.axis_index("y"), lax.axis_index("c")

        def copies(ins):
            return [pltpu.make_async_remote_copy(
                src_ref=ins[t], dst_ref=outs[t], send_sem=send.at[t], recv_sem=recv.at[t],
                device_id=(x, y, 1 - c), device_id_type=MESH) for t in range(nt)]

        @pl.when(c == 0)
        def _():
            for cp in copies(in0):
                cp.start()

        @pl.when(c == 1)
        def _():
            for cp in copies(in1):
                cp.start()

        for cp in copies(in0):
            cp.wait()

    return pl.pallas_call(
        body, name="pair_swap", in_specs=[ANY] * (2 * nt), out_specs=[ANY] * nt,
        out_shape=[jax.ShapeDtypeStruct(p.shape, p.dtype) for p in parts0],
        scratch_shapes=[pltpu.SemaphoreType.DMA((nt,))] * 2,
    )(*parts0, *parts1)


def _adamw_layers(mine0, mine1, theirs, w, m, v, name):
    L, r, C = w.shape
    rt = _pick(r, _rows_for(C), 16)

    def body(a0_ref, a1_ref, b_ref, w_ref, m_ref, v_ref, g_ref, d_ref, mo_ref, vo_ref):
        layer, c = pl.program_id(0), lax.axis_index("c")
        own = jnp.where(layer == 0, a0_ref[...], a1_ref[...])
        g_ref[...] = jnp.where(layer == c, own, b_ref[...])
        _adamw_math(g_ref, w_ref, m_ref, v_ref, d_ref, mo_ref, vo_ref)

    flat = pl.BlockSpec((rt, C), lambda l, i: (i, 0))
    lay = pl.BlockSpec((None, rt, C), lambda l, i: (l, i, 0))
    return pl.pallas_call(
        body, name=name, grid=(L, r // rt), in_specs=[flat, flat, flat, lay, lay, lay], out_specs=[lay] * 4,
        out_shape=[jax.ShapeDtypeStruct((L, r, C), f32)] * 4, compiler_params=_cp("parallel", "parallel"),
    )(mine0, mine1, theirs, w, m, v)


def _adamw_math(g_ref, w_ref, m_ref, v_ref, d_ref, mo_ref, vo_ref):
    gg = g_ref[...]
    m_new = ADAM_B1 * m_ref[...] + (1.0 - ADAM_B1) * gg
    v_new = ADAM_B2 * v_ref[...] + (1.0 - ADAM_B2) * (gg * gg)
    m_hat = m_new / (1.0 - ADAM_B1 ** ADAM_STEP)
    v_hat = v_new / (1.0 - ADAM_B2 ** ADAM_STEP)
    d_ref[...] = -ADAM_LR * (m_hat / (jnp.sqrt(v_hat) + ADAM_EPS) + ADAM_WD * w_ref[...])
    mo_ref[...] = m_new
    vo_ref[...] = v_new


FLAT_TILE = 2048


def _add2(a, b, name):
    R = a.shape[0]
    rt = _pick(R, FLAT_TILE, SUBLANES)

    def body(a_ref, b_ref, o_ref):
        o_ref[...] = a_ref[...] + b_ref[...]

    row = pl.BlockSpec((rt, LANES), lambda i: (i, 0))
    return pl.pallas_call(
        body, name=name, grid=(R // rt,), in_specs=[row, row], out_specs=row,
        out_shape=jax.ShapeDtypeStruct((R, LANES), f32), compiler_params=_cp("parallel"),
    )(a, b)


def _adamw(g, w, m, v, name):
    R = g.shape[0]
    rt = _pick(R, FLAT_TILE, SUBLANES)

    def body(g_ref, w_ref, m_ref, v_ref, d_ref, mo_ref, vo_ref):
        gg = g_ref[...]
        m_new = ADAM_B1 * m_ref[...] + (1.0 - ADAM_B1) * gg
        v_new = ADAM_B2 * v_ref[...] + (1.0 - ADAM_B2) * (gg * gg)
        m_hat = m_new / (1.0 - ADAM_B1 ** ADAM_STEP)
        v_hat = v_new / (1.0 - ADAM_B2 ** ADAM_STEP)
        d_ref[...] = -ADAM_LR * (m_hat / (jnp.sqrt(v_hat) + ADAM_EPS) + ADAM_WD * w_ref[...])
        mo_ref[...] = m_new
        vo_ref[...] = v_new

    row = pl.BlockSpec((rt, LANES), lambda i: (i, 0))
    return pl.pallas_call(
        body, name=name, grid=(R // rt,), in_specs=[row] * 4, out_specs=[row] * 3,
        out_shape=[jax.ShapeDtypeStruct((R, LANES), f32)] * 3, compiler_params=_cp("parallel"),
    )(g, w, m, v)


def _pack(arrs, dtype):
    flat = jnp.concatenate([a.astype(dtype).reshape(-1) for a in arrs])
    per = FLAT_TILE * LANES
    flat = jnp.pad(flat, (0, (-flat.shape[0]) % per))
    return flat.reshape(-1, LANES)


def _unpack(buf, shapes):
    flat = buf.reshape(-1)
    out, off = [], 0
    for s in shapes:
        n = math.prod(s)
        out.append(flat[off:off + n].reshape(s))
        off += n
    return out


def _block_diag(w):
    n, a, b = w.shape
    eye = jnp.eye(n, dtype=w.dtype)
    return (w[:, :, None, :] * eye[:, None, :, None]).reshape(n * a, n * b)


def _diag_blocks(m, n):
    a, b = m.shape[0] // n, m.shape[1] // n
    idx = jnp.arange(n)
    return m.reshape(n, a, n, b)[idx, :, idx, :]


BIG = ("w_in", "w_out", "w_up", "w_down", "s5_w_glu")
BIG_COL_SHARDED = {"w_in": True, "w_out": False, "w_up": True, "w_down": False, "s5_w_glu": False}
CONV_SHARDED = ("lru_conv_w", "ffn_conv_w")
SMALL = ("lru_conv_b", "lru_wr", "lru_br", "lru_wi", "lru_bi", "lru_lambda", "s5_a_re", "s5_a_im", "s5_b_re",
         "s5_b_im", "s5_c_re", "s5_c_im", "s5_d", "s5_log_step", "s5_b_glu", "mix_norm_g", "ln1_g", "ln1_b",
         "ffn_conv_b", "ln2_g", "ln2_b")
WEIGHTS = ("w_in", "lru_conv_w", "lru_conv_b", "lru_wr", "lru_br", "lru_wi", "lru_bi", "lru_lambda", "s5_a_re",
           "s5_a_im", "s5_b_re", "s5_b_im", "s5_c_re", "s5_c_im", "s5_d", "s5_log_step", "s5_w_glu", "s5_b_glu",
           "mix_norm_g", "w_out", "ln1_g", "ln1_b", "w_up", "ffn_conv_w", "ffn_conv_b", "w_down", "ln2_g", "ln2_b")


def _assemble(slabs, col_sharded):
    _, L, r, c = slabs.shape
    if col_sharded:
        return slabs.transpose(1, 2, 0, 3).reshape(L, r, 4 * c)
    return slabs.transpose(1, 0, 2, 3).reshape(L, 4 * r, c)


def _s5_prepare(p):
    G = N_S5_GROUPS
    bt_re, bt_im = p["s5_b_re"].transpose(0, 2, 1), p["s5_b_im"].transpose(0, 2, 1)
    ls = p["s5_log_step"].reshape(G, 1)
    ab_re, ab_im, bb_re, bb_im = _s5_params(p["s5_a_re"], p["s5_a_im"], ls, bt_re, bt_im)
    ab = jnp.concatenate([ab_re.reshape(1, S5_LANES), ab_im.reshape(1, S5_LANES)], axis=1)
    bbcat = jnp.concatenate([_block_diag(bb_re), _block_diag(bb_im)], axis=1).astype(bf16)
    ccat = jnp.concatenate([_block_diag(p["s5_c_re"].transpose(0, 2, 1)),
                            -_block_diag(p["s5_c_im"].transpose(0, 2, 1))], axis=0).astype(bf16)
    bbcat_pad = jnp.concatenate([bbcat, jnp.zeros((LRU_WIDTH - S5_WIDTH, 2 * S5_LANES), bf16)], axis=0)
    return dict(bt_re=bt_re, bt_im=bt_im, ls=ls, ab=ab, bbcat=bbcat, bbcat_pad=bbcat_pad, ccat=ccat)


def _layer_fwd(h, p, cos, sin, pending, install):
    sv = {"h": h}
    proj = _mm(h, p["w_in"], "nn", "mm_proj", tn=768)
    sv.update(proj=proj)
    qr, kr = _rope_fwd(proj, cos, sin)
    attn, ltot, gathered = _attn_fwd2(qr, kr, proj, [s for _, _, s in pending], [True] * len(pending))
    install(pending, gathered)
    sv.update(qr=qr, kr=kr, attn=attn, ltot=ltot)
    wr, wi = _block_diag(p["lru_wr"]).astype(bf16), _block_diag(p["lru_wi"]).astype(bf16)
    xc, r, i, log_a, u = _lru_pre(proj, p["lru_conv_w"], p["lru_conv_b"], wr, p["lru_br"], wi, p["lru_bi"],
                                  p["lru_lambda"])
    hl, lru = _lru_scan(log_a, u, proj)
    sv.update(wr=wr, wi=wi, xc=xc, r=r, i=i, log_a=log_a, hl=hl, lru=lru)
    s5 = _s5_prepare(p)
    bu = _mm(proj, s5["bbcat_pad"], "nn", "mm_s5_bu", a_win=(PROJ_S5_U, LRU_WIDTH))
    xs = _s5_scan(bu, s5["ab"])
    yc = _mm(xs, s5["ccat"], "nn", "mm_s5_y")
    ssm, y = _s5_out(yc, proj, p["s5_d"].reshape(-1), p["s5_w_glu"], p["s5_b_glu"])
    sv.update(s5=s5, xs=xs, y=y, ssm=ssm)
    mixed = _mixnorm(attn, lru, ssm, p["mix_norm_g"])
    mix = _mm(mixed, p["w_out"], "nn", "mm_out")
    h1, z1 = _ln_fwd(h, mix, p["ln1_g"], p["ln1_b"], "ln_fwd")
    sv.update(mixed=mixed, z1=z1, h1=h1)
    up = _mm(h1, p["w_up"], "nn", "mm_up", tn=1536)
    act = _ffn_act(up, p["ffn_conv_w"], p["ffn_conv_b"])
    ffn = _mm(act, p["w_down"], "nn", "mm_down")
    h2, z2 = _ln_fwd(h1, ffn, p["ln2_g"], p["ln2_b"], "ln_fwd")
    sv.update(up=up, act=act, z2=z2)
    return h2, sv


def _layer_bwd(dy_a, dy_b, p, sv, cos, sin, hosted=(), hosted_c=0):
    gr = {}
    dz2, acc = _ln_bwd(dy_a, dy_b, sv["z2"], p["ln2_g"], "ln_bwd_top" if dy_a is None else "ln_bwd")
    gr["ln2_g"], gr["ln2_b"] = acc[0], acc[1]
    dact = _mm(dz2, p["w_down"], "nt", "mm_dact")
    gr["w_down"] = _mm(sv["act"], dz2, "tn", "mm_dw_down")
    dupc = _ffn_act_bwd(dact, sv["up"], p["ffn_conv_w"], p["ffn_conv_b"])
    dup, acc = _conv_bwd(dupc, sv["up"], p["ffn_conv_w"], "ffn_conv_bwd", col_tile=FFN_COL_TILE, out_dtype=bf16)
    gr["ffn_conv_w"], gr["ffn_conv_b"] = acc[0:3], acc[3]
    dh1 = _mm(dup, p["w_up"], "nt", "mm_dh1", tk=2048)
    gr["w_up"] = _mm(sv["h1"], dup, "tn", "mm_dw_up", tn=1536)
    dz1, acc = _ln_bwd(dz2, dh1, sv["z1"], p["ln1_g"], "ln_bwd")
    gr["ln1_g"], gr["ln1_b"] = acc[0], acc[1]
    dmixed = _mm(dz1, p["w_out"], "nt", "mm_dmixed")
    gr["w_out"] = _mm(sv["mixed"], dz1, "tn", "mm_dw_out")
    dattn, dlru, dssm, delta, acc = _mixnorm_bwd(dmixed, sv["attn"], sv["lru"], sv["ssm"], p["mix_norm_g"])
    gr["mix_norm_g"] = acc[0]
    proj = sv["proj"]
    dqr, dkr, dv, received = _attn_bwd2(sv["qr"], sv["kr"], proj, dattn, sv["ltot"], delta, hosted, hosted_c)
    dq, dk = _rope_bwd(dqr, dkr, cos, sin)
    g, dgate = _lru_scan_bwd(dlru, proj, sv["hl"], sv["log_a"])
    dxc, dwr, dwi, acc = _lru_gate_bwd(g, sv["hl"], sv["xc"], sv["r"], sv["i"], sv["log_a"], sv["wr"], sv["wi"],
                                       p["lru_lambda"])
    gr["lru_wr"], gr["lru_wi"] = _diag_blocks(dwr, N_LRU_HEADS), _diag_blocks(dwi, N_LRU_HEADS)
    gr["lru_br"], gr["lru_bi"], gr["lru_lambda"] = acc[0], acc[1], acc[2]
    dxr, acc = _conv_bwd(dxc, proj, p["lru_conv_w"], "lru_conv_bwd", x_col_block=PROJ_LRU_X)
    gr["lru_conv_w"], gr["lru_conv_b"] = acc[0:4], acc[4]
    s5 = sv["s5"]
    G = N_S5_GROUPS
    dy, du_direct, dwglu, acc = _s5_out_bwd(dssm, sv["y"], proj, p["s5_d"].reshape(-1), p["s5_w_glu"],
                                            p["s5_b_glu"])
    gr["s5_w_glu"], gr["s5_b_glu"], gr["s5_d"] = dwglu, acc[0], acc[1].reshape(G, S5_GROUP)
    dxs = _mm(dy, s5["ccat"], "nt", "mm_s5_dx")
    dccat = _mm(sv["xs"], dy, "tn", "mm_s5_dc")
    gr["s5_c_re"] = _diag_blocks(dccat[:S5_LANES], G).transpose(0, 2, 1)
    gr["s5_c_im"] = -_diag_blocks(dccat[S5_LANES:], G).transpose(0, 2, 1)
    gs, dab = _s5_scan_bwd(dxs, sv["xs"], s5["ab"])
    du = _mm(gs, s5["bbcat"], "nt", "mm_s5_du", add=du_direct)
    dbbcat = _mm(proj, gs, "tn", "mm_s5_dbb", a_win=(PROJ_S5_U, LRU_WIDTH))[:S5_WIDTH]
    d_ar, d_ai, d_ls, d_btr, d_bti = _s5_params_bwd(
        p["s5_a_re"], p["s5_a_im"], s5["ls"], s5["bt_re"], s5["bt_im"],
        dab[:, :S5_LANES].reshape(G, S5_STATE), dab[:, S5_LANES:].reshape(G, S5_STATE),
        _diag_blocks(dbbcat[:, :S5_LANES], G), _diag_blocks(dbbcat[:, S5_LANES:], G))
    gr["s5_a_re"], gr["s5_a_im"], gr["s5_log_step"] = d_ar, d_ai, d_ls.reshape(G)
    gr["s5_b_re"], gr["s5_b_im"] = d_btr.transpose(0, 2, 1), d_bti.transpose(0, 2, 1)
    pad = jnp.zeros((du.shape[0], D_IN_PAD - D_IN), f32)
    dproj = jnp.concatenate([dq, dk, dv, dxr, dgate, du, pad], axis=1).astype(bf16)
    gr["w_in"] = _mm(sv["h"], dproj, "tn", "mm_dw_in", tn=768)[:, :D_IN]
    dh = _mm(dproj, p["w_in"], "nt", "mm_dh")
    return dz1, dh, gr, received


def _train_step(d):
    x, target = d["x"][0], d["loss_target"][0]
    S = x.shape[0]
    me = 2 * lax.axis_index("x") + lax.axis_index("y")

    def rows2d(a):
        return a.reshape(a.shape[0] * a.shape[1], a.shape[2])

    params = [{n: d[n][l] for n in SMALL} for l in range(DEPTH)]

    def install(items, gathered):
        for (n, l, mine), g in zip(items, gathered):
            g = lax.dynamic_update_slice_in_dim(g, mine[None], me, axis=0)
            if n in CONV_SHARDED:
                full = _assemble(g.reshape((4,) + d[n].shape), True)
                for k in range(DEPTH):
                    params[k][n] = full[k]
                continue
            full = _assemble(g[:, None], BIG_COL_SHARDED[n])[0]
            if n == "w_in":
                full = jnp.pad(full, ((0, 0), (0, D_IN_PAD - D_IN)))
            params[l][n] = full

    def shard(n, l):
        return (n, l, d[n][l].astype(bf16))

    first = [shard("w_in", 0)] + [(n, None, rows2d(d[n])) for n in CONV_SHARDED]
    install(first, _gather_weights([s for _, _, s in first], [True] + [False] * len(CONV_SHARDED)))
    later = [[shard(n, 0) for n in BIG[1:]] + [shard("w_in", 1)], [shard(n, 1) for n in BIG[1:]]]

    cos, sin = _rope_tables(S)
    h, saved = x, []
    for l in range(DEPTH):
        h, sv = _layer_fwd(h, params[l], cos, sin, later[l], install)
        saved.append(sv)
    dy, loss_acc = _loss_head(h, target)
    loss = lax.psum(loss_acc[0, 0], ("x", "y", "c"))
    def chip_slabs(layer, others):
        chip = [_pair_sum(grads[layer][n], o, "pair_sum_" + n) for n, o in zip(BIG, others)]
        return [p.reshape(p.shape[0], 4, p.shape[1] // 4).transpose(1, 0, 2) if BIG_COL_SHARDED[n]
                else p.reshape(4, p.shape[0] // 4, p.shape[1]) for n, p in zip(BIG, chip)]

    da, db, grads = None, dy, [None] * DEPTH
    da, db, grads[1], _ = _layer_bwd(da, db, params[1], saved[1], cos, sin)
    slabs1 = chip_slabs(1, _sibling_send([grads[1][n] for n in BIG], 1)[0])
    da, db, grads[0], recv1 = _layer_bwd(da, db, params[0], saved[0], cos, sin, slabs1, 1)
    out = {"loss": loss, "grad_x": _axpy(da, db, "grad_x")[None]}
    G = {n: jnp.stack([grads[l][n] for l in range(DEPTH)]) for n in SMALL + CONV_SHARDED}

    small = SMALL + CONV_SHARDED
    sp = _pack([G[n] for n in small], f32)
    others0, (sp_sibling,) = _sibling_send([grads[0][n] for n in BIG], 0, [sp])
    slabs0 = chip_slabs(0, others0)
    chip_small = _add2(sp, sp_sibling, "pair_sum_small")
    recv0, (recv_small,) = _chip_exchange(slabs0, [chip_small], 0)
    mine0 = [_sum_chips(r, s, "sum_chips_" + n) for n, r, s in zip(BIG, recv0, slabs0)]
    mine1 = [_sum_chips(r, s, "sum_chips_" + n) for n, r, s in zip(BIG, recv1, slabs1)]
    theirs = _pair_swap(mine0, mine1)
    for n, a0, a1, b in zip(BIG, mine0, mine1, theirs):
        upd = _adamw_layers(a0, a1, b, d[n], d["m_" + n], d["v_" + n], "adamw_" + n)
        for pre, u in zip(("grad_", "delta_", "new_m_", "new_v_"), upd):
            out[pre + n] = u

    total = _sum_chips(recv_small, chip_small, "sum_chips_small")
    gs = dict(zip(small, _unpack(total, [G[n].shape for n in small])))
    for n in CONV_SHARDED:
        L, K, C = gs[n].shape
        gs[n] = lax.dynamic_index_in_dim(gs[n].reshape(L, K, 4, C // 4), me, axis=2, keepdims=False)
    small_shapes = [d[n].shape for n in small]
    gsmall = _pack([gs[n] for n in small], f32)
    upd = _adamw(gsmall, _pack([d[n] for n in small], f32), _pack([d["m_" + n] for n in small], f32),
                 _pack([d["v_" + n] for n in small], f32), "adamw_small")
    for pre, buf in zip(("grad_", "delta_", "new_m_", "new_v_"), (gsmall,) + tuple(upd)):
        for n, a in zip(small, _unpack(buf, small_shapes)):
            out[pre + n] = a

    return (out["loss"], out["grad_x"]) + tuple(out[pre + n] for pre in ("grad_", "delta_", "new_m_", "new_v_")
                                                for n in WEIGHTS)


def kernel(
        x, w_in, lru_conv_w, lru_conv_b, lru_wr, lru_br, lru_wi, lru_bi, lru_lambda, s5_a_re, s5_a_im, s5_b_re,
        s5_b_im, s5_c_re, s5_c_im, s5_d, s5_log_step, s5_w_glu, s5_b_glu, mix_norm_g, w_out, ln1_g, ln1_b, w_up,
        ffn_conv_w, ffn_conv_b, w_down, ln2_g, ln2_b, loss_target, m_w_in, m_lru_conv_w, m_lru_conv_b, m_lru_wr,
        m_lru_br, m_lru_wi, m_lru_bi, m_lru_lambda, m_s5_a_re, m_s5_a_im, m_s5_b_re, m_s5_b_im, m_s5_c_re,
        m_s5_c_im, m_s5_d, m_s5_log_step, m_s5_w_glu, m_s5_b_glu, m_mix_norm_g, m_w_out, m_ln1_g, m_ln1_b,
        m_w_up, m_ffn_conv_w, m_ffn_conv_b, m_w_down, m_ln2_g, m_ln2_b, v_w_in, v_lru_conv_w, v_lru_conv_b,
        v_lru_wr, v_lru_br, v_lru_wi, v_lru_bi, v_lru_lambda, v_s5_a_re, v_s5_a_im, v_s5_b_re, v_s5_b_im,
        v_s5_c_re, v_s5_c_im, v_s5_d, v_s5_log_step, v_s5_w_glu, v_s5_b_glu, v_mix_norm_g, v_w_out, v_ln1_g,
        v_ln1_b, v_w_up, v_ffn_conv_w, v_ffn_conv_b, v_w_down, v_ln2_g, v_ln2_b
):
    return _train_step(dict(locals()))
```

```python
import functools
import math

import jax
import jax.numpy as jnp
from jax import lax
from jax.experimental import pallas as pl
from jax.experimental.pallas import tpu as pltpu

f32 = jnp.float32
bf16 = jnp.bfloat16
MESH = pl.DeviceIdType.MESH

D_MODEL = 1024
ATTN_WIDTH = 384
LRU_WIDTH = 384
S5_WIDTH = 256
HEAD_DIM = 64
N_LRU_HEADS = 6
N_S5_GROUPS = 16
S5_GROUP = 16
S5_STATE = 64
S5_LANES = N_S5_GROUPS * S5_STATE
D_FF = 3072
D_IN = 2176
LRU_C = 8.0
ROPE_THETA = 10000.0
DILATIONS = (1, 4, 16)
ATTN_BLOCK = 128
DEPTH = 2
ALPHA = (2 * DEPTH) ** 0.25
LN_EPS = 1e-5
RMS_EPS = 1e-6
ADAM_LR, ADAM_B1, ADAM_B2, ADAM_EPS, ADAM_WD, ADAM_STEP = 0.001, 0.9, 0.999, 1e-08, 0.01, 10

SUBLANES = 8
LANES = 128
VMEM_LIMIT = 56 * 1024 * 1024
ROW_TILE = 512
MM_SINGLE_K = 3072
D_IN_PAD = 2304
NEG = -1e30


def _cp(*sem):
    return pltpu.CompilerParams(dimension_semantics=sem if sem else None, vmem_limit_bytes=VMEM_LIMIT)


def _pick(dim, pref, align=LANES):
    if dim <= pref:
        return dim
    t = (pref // align) * align
    while t >= align:
        if dim % t == 0:
            return t
        t -= align
    return dim


def _gelu(x):
    return jax.nn.gelu(x)


def _gelu_grad(x):
    c = math.sqrt(2.0 / math.pi)
    t = jnp.tanh(c * (x + 0.044715 * x * x * x))
    return 0.5 * (1.0 + t) + 0.5 * x * (1.0 - t * t) * c * (1.0 + 3 * 0.044715 * x * x)


def _gelu_pair(x):
    c = math.sqrt(2.0 / math.pi)
    x2 = x * x
    t = jnp.tanh(c * x * (1.0 + 0.044715 * x2))
    return 0.5 * x * (1.0 + t), 0.5 * (1.0 + t) + 0.5 * x * (1.0 - t * t) * c * (1.0 + 3 * 0.044715 * x2)


def _sigmoid(x):
    return jax.nn.sigmoid(x)


def _expm1(x):
    p = 1.0 + x / 9.0
    for n in (8.0, 7.0, 6.0, 5.0, 4.0, 3.0, 2.0):
        p = 1.0 + (x / n) * p
    return jnp.where(jnp.abs(x) < 0.3, x * p, jnp.exp(x) - 1.0)


def _dot(a, b, dims):
    return lax.dot_general(a, b, (dims, ((), ())), preferred_element_type=f32)


NN = ((1,), (0,))
NT = ((1,), (1,))
TN = ((0,), (0,))


def _mm(a, b, mode, name, out_dtype=f32, tm=1024, tn=1024, tk=1024, add=None, a_win=None):
    if mode == "nn":
        (M, K), N = a.shape, b.shape[1]
    elif mode == "nt":
        (M, K), N = a.shape, b.shape[0]
    else:
        (K, M), N = a.shape, b.shape[1]
    win = 0
    if a_win is not None:
        win, w = a_win
        if mode == "tn":
            M, tm = w, w
        else:
            K = w
    single = mode != "tn" and K <= MM_SINGLE_K
    tm, tn = _pick(M, tm), _pick(N, tn)
    tk = K if single else _pick(K, tk)
    nk = K // tk
    dims = {"nn": NN, "nt": NT, "tn": TN}[mode]

    def body(a_ref, b_ref, *rest):
        prod = _dot(a_ref[...].astype(bf16), b_ref[...].astype(bf16), dims)
        if single:
            o_ref = rest[-1]
            o_ref[...] = (prod if add is None else prod + rest[0][...]).astype(o_ref.dtype)
            return
        o_ref, acc = rest[-2:]
        k = pl.program_id(2)

        @pl.when(k == 0)
        def _():
            acc[...] = prod if add is None else prod + rest[0][...]

        @pl.when(k > 0)
        def _():
            acc[...] += prod

        @pl.when(k == nk - 1)
        def _():
            o_ref[...] = acc[...].astype(o_ref.dtype)

    if mode == "tn":
        a_spec = pl.BlockSpec((tk, tm), lambda i, j, k: (k, i + win))
    else:
        a_spec = pl.BlockSpec((tm, tk), lambda i, j, k: (i, k + win))
    if mode == "nt":
        b_spec = pl.BlockSpec((tn, tk), lambda i, j, k: (j, k))
    else:
        b_spec = pl.BlockSpec((tk, tn), lambda i, j, k: (k, j))
    o_spec = pl.BlockSpec((tm, tn), lambda i, j, k: (i, j))
    return pl.pallas_call(
        body, name=name, grid=(M // tm, N // tn, nk),
        in_specs=[a_spec, b_spec] + ([] if add is None else [o_spec]), out_specs=o_spec,
        out_shape=jax.ShapeDtypeStruct((M, N), out_dtype),
        scratch_shapes=[] if single else [pltpu.VMEM((tm, tn), f32)],
        compiler_params=_cp("parallel", "parallel", "arbitrary"),
    )(*((a, b) if add is None else (a, b, add)))


def _shift_down(cur, prev8, k):
    if k == 0:
        return cur
    ext = jnp.concatenate([prev8, cur], axis=0)
    return pltpu.roll(ext, k, 0)[SUBLANES:]


def _shift_up(cur, next8, k):
    if k == 0:
        return cur
    n = cur.shape[0] + SUBLANES
    ext = jnp.concatenate([cur, next8], axis=0)
    return pltpu.roll(ext, n - k, 0)[: cur.shape[0]]


def _prev_halo_spec(rt, cols, ncolblk_fn):
    per = rt // SUBLANES
    return pl.BlockSpec((SUBLANES, cols), lambda *g: (jnp.maximum(g[-1] * per - 1, 0), ncolblk_fn(*g)))


def _ln_fwd(h, branch, g, b, name):
    S, D = h.shape
    rt = _pick(S, ROW_TILE, SUBLANES)

    def body(h_ref, m_ref, g_ref, b_ref, o_ref, z_ref):
        z = ALPHA * h_ref[...] + m_ref[...]
        mu = jnp.mean(z, axis=-1, keepdims=True)
        zc = z - mu
        var = jnp.mean(zc * zc, axis=-1, keepdims=True)
        o_ref[...] = zc * lax.rsqrt(var + LN_EPS) * g_ref[...] + b_ref[...]
        z_ref[...] = z

    row = pl.BlockSpec((rt, D), lambda i: (i, 0))
    vec = pl.BlockSpec((1, D), lambda i: (0, 0))
    return pl.pallas_call(
        body, name=name, grid=(S // rt,), in_specs=[row, row, vec, vec], out_specs=[row, row],
        out_shape=[jax.ShapeDtypeStruct((S, D), f32)] * 2, compiler_params=_cp("parallel"),
    )(h, branch, g.reshape(1, D), b.reshape(1, D))


def _ln_bwd(dy_a, dy_b, z, g, name):
    S, D = z.shape
    rt = _pick(S, ROW_TILE, SUBLANES)
    two = dy_a is not None

    def body(*refs):
        if two:
            a_ref, b_ref, z_ref, g_ref, dz_ref, acc_ref = refs
            dy = ALPHA * a_ref[...] + b_ref[...]
        else:
            b_ref, z_ref, g_ref, dz_ref, acc_ref = refs
            dy = b_ref[...]
        z = z_ref[...]
        mu = jnp.mean(z, axis=-1, keepdims=True)
        zc = z - mu
        var = jnp.mean(zc * zc, axis=-1, keepdims=True)
        rstd = lax.rsqrt(var + LN_EPS)
        xhat = zc * rstd
        dxh = dy * g_ref[...]
        m1 = jnp.mean(dxh, axis=-1, keepdims=True)
        m2 = jnp.mean(dxh * xhat, axis=-1, keepdims=True)
        dz_ref[...] = rstd * (dxh - m1 - xhat * m2)

        @pl.when(pl.program_id(0) == 0)
        def _():
            acc_ref[...] = jnp.zeros_like(acc_ref)

        acc_ref[0:1, :] += jnp.sum(dy * xhat, axis=0, keepdims=True)
        acc_ref[1:2, :] += jnp.sum(dy, axis=0, keepdims=True)

    row = pl.BlockSpec((rt, D), lambda i: (i, 0))
    vec = pl.BlockSpec((1, D), lambda i: (0, 0))
    acc = pl.BlockSpec((SUBLANES, D), lambda i: (0, 0))
    ins = ([dy_a] if two else []) + [dy_b, z, g.reshape(1, D)]
    return pl.pallas_call(
        body, name=name, grid=(S // rt,), in_specs=[row] * (len(ins) - 1) + [vec], out_specs=[row, acc],
        out_shape=[jax.ShapeDtypeStruct((S, D), f32), jax.ShapeDtypeStruct((SUBLANES, D), f32)],
        compiler_params=_cp("arbitrary"),
    )(*ins)


def _loss_head(y, target):
    S, D = y.shape
    rt = _pick(S, ROW_TILE, SUBLANES)

    def body(y_ref, t_ref, dy_ref, acc_ref):
        e = y_ref[...] - t_ref[...]
        dy_ref[...] = e * (1.0 / D)

        @pl.when(pl.program_id(0) == 0)
        def _():
            acc_ref[...] = jnp.zeros_like(acc_ref)

        part = jnp.sum(jnp.mean(e * e, axis=-1, keepdims=True), axis=0, keepdims=True)
        acc_ref[...] += 0.5 * part

    row = pl.BlockSpec((rt, D), lambda i: (i, 0))
    return pl.pallas_call(
        body, name="loss_head", grid=(S // rt,), in_specs=[row, row],
        out_specs=[row, pl.BlockSpec((1, 1), lambda i: (0, 0))],
        out_shape=[jax.ShapeDtypeStruct((S, D), f32), jax.ShapeDtypeStruct((1, 1), f32)],
        compiler_params=_cp("arbitrary"),
    )(y, target)


def _axpy(a, b, name):
    S, D = a.shape
    rt = _pick(S, ROW_TILE, SUBLANES)

    def body(a_ref, b_ref, o_ref):
        o_ref[...] = ALPHA * a_ref[...] + b_ref[...]

    row = pl.BlockSpec((rt, D), lambda i: (i, 0))
    return pl.pallas_call(
        body, name=name, grid=(S // rt,), in_specs=[row, row], out_specs=row,
        out_shape=jax.ShapeDtypeStruct((S, D), f32), compiler_params=_cp("parallel"),
    )(a, b)


def _rope_tables(S):
    rt = _pick(S, ROW_TILE, SUBLANES)

    def body(c_ref, s_ref):
        pos = (pl.program_id(0) * rt + lax.broadcasted_iota(jnp.int32, (rt, LANES), 0)).astype(f32)
        lane = lax.broadcasted_iota(jnp.int32, (rt, LANES), 1)
        j = (lane % (HEAD_DIM // 2)).astype(f32)
        inv = jnp.exp((-j * 2.0 / HEAD_DIM) * math.log(ROPE_THETA))
        ang = pos * inv
        c = jnp.cos(ang)
        s = jnp.where(lane % HEAD_DIM < HEAD_DIM // 2, -jnp.sin(ang), jnp.sin(ang))
        c_ref[...] = jnp.concatenate([c, c, c], axis=1)
        s_ref[...] = jnp.concatenate([s, s, s], axis=1)

    row = pl.BlockSpec((rt, ATTN_WIDTH), lambda i: (i, 0))
    return pl.pallas_call(
        body, name="rope_tables", grid=(S // rt,), in_specs=[], out_specs=[row, row],
        out_shape=[jax.ShapeDtypeStruct((S, ATTN_WIDTH), f32)] * 2, compiler_params=_cp("parallel"),
    )()


def _swap_halves(x):
    lane = lax.broadcasted_iota(jnp.int32, x.shape, 1)
    half = HEAD_DIM // 2
    return jnp.where(lane % HEAD_DIM < half, pltpu.roll(x, x.shape[1] - half, 1), pltpu.roll(x, half, 1))


def _rope_fwd(proj, cos, sin):
    S, W = proj.shape[0], ATTN_WIDTH
    rt = _pick(S, ROW_TILE, SUBLANES)

    def body(q_ref, k_ref, c_ref, s_ref, qo_ref, ko_ref):
        c, s = c_ref[...], s_ref[...]
        qo_ref[...] = q_ref[...] * c + _swap_halves(q_ref[...]) * s
        ko_ref[...] = k_ref[...] * c + _swap_halves(k_ref[...]) * s

    row = pl.BlockSpec((rt, W), lambda i: (i, 0))
    return pl.pallas_call(
        body, name="rope_fwd", grid=(S // rt,), in_specs=[row, pl.BlockSpec((rt, W), lambda i: (i, 1)), row, row],
        out_specs=[row, row], out_shape=[jax.ShapeDtypeStruct((S, W), f32)] * 2, compiler_params=_cp("parallel"),
    )(proj, proj, cos, sin)


def _rope_bwd(dq, dk, cos, sin):
    S, W = dq.shape
    rt = _pick(S, ROW_TILE, SUBLANES)

    def body(q_ref, k_ref, c_ref, s_ref, qo_ref, ko_ref):
        c, s = c_ref[...], s_ref[...]
        qo_ref[...] = q_ref[...] * c + _swap_halves(q_ref[...] * s)
        ko_ref[...] = k_ref[...] * c + _swap_halves(k_ref[...] * s)

    row = pl.BlockSpec((rt, W), lambda i: (i, 0))
    return pl.pallas_call(
        body, name="rope_bwd", grid=(S // rt,), in_specs=[row] * 4, out_specs=[row] * 2,
        out_shape=[jax.ShapeDtypeStruct((S, W), f32)] * 2, compiler_params=_cp("parallel"),
    )(dq, dk, cos, sin)


def _rows(ref, start, d):
    if d == 1:
        return ref[pl.ds(pl.multiple_of(start, ATTN_BLOCK), ATTN_BLOCK), :]
    return ref[pl.ds(start, ATTN_BLOCK, stride=d), :]


def _set_rows(ref, start, d, val):
    if d == 1:
        ref[pl.ds(pl.multiple_of(start, ATTN_BLOCK), ATTN_BLOCK), :] = val
    else:
        ref[pl.ds(start, ATTN_BLOCK, stride=d), :] = val


def _pair_spec(S, first_block):
    return pl.BlockSpec((S, LANES), lambda p: (0, p + first_block))


def _attn_fwd2(qr, kr, proj, shards=(), split=()):
    S = qr.shape[0]
    B = ATTN_BLOCK
    nb = S // B
    scale = HEAD_DIM ** -0.5

    gather = _Gather(shards, split)
    nt = gather.nt

    def body(*refs):
        q_ref, k_ref, v_ref = refs[:3]
        g_ins = refs[3:3 + nt]
        o_ref, l_ref = refs[3 + nt:5 + nt]
        g_outs = refs[5 + nt:5 + 2 * nt]
        m_s, l_s = refs[5 + 2 * nt:7 + 2 * nt]
        g_sems = refs[7 + 2 * nt:]
        if nt:
            @pl.when(pl.program_id(0) == 0)
            def _():
                gather.start(g_ins, g_outs, g_sems)

        qi = lax.broadcasted_iota(jnp.int32, (B, 2 * B), 0)
        ki = lax.broadcasted_iota(jnp.int32, (B, 2 * B), 1)
        dist = qi + B - ki
        band = (dist >= 0) & (dist <= B)
        for bi, d in enumerate(DILATIONS):
            bpc = nb // d

            def blk(b, carry, bi=bi, d=d, bpc=bpc):
                c, n = b // bpc, b % bpc
                start = c + d * B * n
                pstart = c + d * B * jnp.maximum(n - 1, 0)
                valid = band & ((ki >= B) | (n > 0))
                q = _rows(q_ref, start, d).astype(bf16)
                kcat = jnp.concatenate([_rows(k_ref, pstart, d), _rows(k_ref, start, d)], axis=0).astype(bf16)
                vcat = jnp.concatenate([_rows(v_ref, pstart, d), _rows(v_ref, start, d)], axis=0).astype(bf16)
                if bi > 0:
                    m_old, l_old, a_old = _rows(m_s, start, d), _rows(l_s, start, d), _rows(o_ref, start, d)
                ms, ls, accs = [], [], []
                for h in range(2):
                    sl = slice(h * HEAD_DIM, (h + 1) * HEAD_DIM)
                    c0 = h * HEAD_DIM
                    s = jnp.where(valid, _dot(q[:, sl], kcat[:, sl], NT) * scale, NEG)
                    m = jnp.max(s, axis=1, keepdims=True)
                    if bi > 0:
                        mo = m_old[:, c0:c0 + 1]
                        m = jnp.maximum(m, mo)
                        alpha = jnp.exp(mo - m)
                    p = jnp.exp(s - m)
                    l = jnp.sum(p, axis=1, keepdims=True)
                    acc = _dot(p.astype(bf16), vcat[:, sl], NN)
                    if bi > 0:
                        l = l + alpha * l_old[:, c0:c0 + 1]
                        acc = acc + alpha * a_old[:, sl]
                    ms.append(jnp.broadcast_to(m, (B, HEAD_DIM)))
                    ls.append(jnp.broadcast_to(l, (B, HEAD_DIM)))
                    accs.append(acc)
                _set_rows(m_s, start, d, jnp.concatenate(ms, axis=1))
                _set_rows(l_s, start, d, jnp.concatenate(ls, axis=1))
                _set_rows(o_ref, start, d, jnp.concatenate(accs, axis=1))
                return carry

            lax.fori_loop(0, nb, blk, 0, unroll=4)

        def fin(t, carry):
            rows = pl.ds(pl.multiple_of(t * B, B), B)
            l = l_s[rows, :]
            o_ref[rows, :] = o_ref[rows, :] / l
            l_ref[rows, :] = m_s[rows, :] + jnp.log(l)
            return carry

        lax.fori_loop(0, nb, fin, 0)
        if nt:
            @pl.when(pl.program_id(0) == pl.num_programs(0) - 1)
            def _():
                gather.finish(g_ins, g_outs, g_sems)

    pair = _pair_spec(S, 0)
    res = pl.pallas_call(
        body, name="attn_fwd_gather" if nt else "attn_fwd", grid=(3,),
        in_specs=[pair, pair, _pair_spec(S, 2 * ATTN_WIDTH // LANES)] + gather.in_specs,
        out_specs=[pair, pair] + gather.out_specs,
        out_shape=[jax.ShapeDtypeStruct((S, ATTN_WIDTH), f32)] * 2 + gather.out_shape,
        scratch_shapes=[pltpu.VMEM((S, LANES), f32)] * 2 + gather.scratch,
        compiler_params=_cp("arbitrary"),
    )(qr, kr, proj, *shards)
    return res[0], res[1], list(res[2:])


def _attn_bwd2(qr, kr, proj, dattn, ltot, delta, slabs=(), only_c=0):
    S = qr.shape[0]
    B = ATTN_BLOCK
    nb = S // B
    scale = HEAD_DIM ** -0.5
    ex = _ChipExchange(slabs, (), only_c)
    n = ex.n

    def body(*refs):
        q_ref, k_ref, v_ref, do_ref, l_ref, d_ref = refs[:6]
        x_ins = refs[6:6 + n]
        dq_ref, dk_ref, dv_ref = refs[6 + n:9 + n]
        x_outs = refs[9 + n:9 + 2 * n]
        x_sems = refs[9 + 2 * n:]
        if n:
            @pl.when(pl.program_id(0) == 0)
            def _():
                ex.start(x_ins, x_outs, x_sems)

        qi = lax.broadcasted_iota(jnp.int32, (B, 2 * B), 0)
        ki = lax.broadcasted_iota(jnp.int32, (B, 2 * B), 1)
        dist1 = qi + B - ki
        band1 = (dist1 >= 0) & (dist1 <= B)
        ri = lax.broadcasted_iota(jnp.int32, (2 * B, B), 0)
        ci = lax.broadcasted_iota(jnp.int32, (2 * B, B), 1)
        dist2 = ri - ci
        band2 = (dist2 >= 0) & (dist2 <= B)
        for bi, d in enumerate(DILATIONS):
            bpc = nb // d

            def blk(b, carry, bi=bi, d=d, bpc=bpc):
                c, n = b // bpc, b % bpc
                start = c + d * B * n
                pstart = c + d * B * jnp.maximum(n - 1, 0)
                nstart = c + d * B * jnp.minimum(n + 1, bpc - 1)
                valid1 = band1 & ((ki >= B) | (n > 0))
                valid2 = band2 & ((ri < B) | (n + 1 < bpc))
                q_c, q_n = _rows(q_ref, start, d), _rows(q_ref, nstart, d)
                k_p, k_c = _rows(k_ref, pstart, d), _rows(k_ref, start, d)
                v_p, v_c = _rows(v_ref, pstart, d), _rows(v_ref, start, d)
                do_c, do_n = _rows(do_ref, start, d), _rows(do_ref, nstart, d)
                l_c, l_n = _rows(l_ref, start, d), _rows(l_ref, nstart, d)
                d_c, d_n = _rows(d_ref, start, d), _rows(d_ref, nstart, d)
                qc = q_c.astype(bf16)
                qcat = jnp.concatenate([q_c, q_n], axis=0).astype(bf16)
                kc = k_c.astype(bf16)
                kcat = jnp.concatenate([k_p, k_c], axis=0).astype(bf16)
                vc = v_c.astype(bf16)
                vcat = jnp.concatenate([v_p, v_c], axis=0).astype(bf16)
                doc = do_c.astype(bf16)
                docat = jnp.concatenate([do_c, do_n], axis=0).astype(bf16)
                lcat = jnp.concatenate([l_c, l_n], axis=0)
                dcat = jnp.concatenate([d_c, d_n], axis=0)
                dqs, dks, dvs = [], [], []
                for h in range(2):
                    sl = slice(h * HEAD_DIM, (h + 1) * HEAD_DIM)
                    c0 = h * HEAD_DIM
                    s1 = _dot(qc[:, sl], kcat[:, sl], NT) * scale
                    p1 = jnp.where(valid1, jnp.exp(s1 - l_c[:, c0:c0 + 1]), 0.0)
                    dp1 = _dot(doc[:, sl], vcat[:, sl], NT)
                    ds1 = p1 * (dp1 - d_c[:, c0:c0 + 1]) * scale
                    dqs.append(_dot(ds1.astype(bf16), kcat[:, sl], NN))
                    s2 = _dot(qcat[:, sl], kc[:, sl], NT) * scale
                    p2 = jnp.where(valid2, jnp.exp(s2 - lcat[:, c0:c0 + 1]), 0.0)
                    dvs.append(_dot(p2.astype(bf16), docat[:, sl], TN))
                    dp2 = _dot(docat[:, sl], vc[:, sl], NT)
                    ds2 = p2 * (dp2 - dcat[:, c0:c0 + 1]) * scale
                    dks.append(_dot(ds2.astype(bf16), qcat[:, sl], TN))
                for ref, parts in ((dq_ref, dqs), (dk_ref, dks), (dv_ref, dvs)):
                    new = jnp.concatenate(parts, axis=1)
                    if bi > 0:
                        new = new + _rows(ref, start, d)
                    _set_rows(ref, start, d, new)
                return carry

            lax.fori_loop(0, nb, blk, 0, unroll=2)

        if n:
            @pl.when(pl.program_id(0) == pl.num_programs(0) - 1)
            def _():
                ex.finish(x_ins, x_outs, x_sems)

    pair = _pair_spec(S, 0)
    res = pl.pallas_call(
        body, name="attn_bwd_exchange" if n else "attn_bwd", grid=(3,),
        in_specs=[pair, pair, _pair_spec(S, 2 * ATTN_WIDTH // LANES), pair, pair, pair] + ex.in_specs,
        out_specs=[pair] * 3 + ex.out_specs,
        out_shape=[jax.ShapeDtypeStruct((S, ATTN_WIDTH), f32)] * 3 + ex.out_shape,
        scratch_shapes=ex.scratch, compiler_params=_cp("arbitrary"),
    )(qr, kr, proj, dattn, ltot, delta, *slabs)
    return res[0], res[1], res[2], list(res[3:])


def _softplus_neg(lam):
    return jnp.maximum(-lam, 0.0) + jnp.log1p(jnp.exp(-jnp.abs(lam)))


PROJ_LRU_X, PROJ_LRU_GATE, PROJ_S5_U = 3, 4, 5


def _lru_pre(proj, conv_w, conv_b, wr, br, wi, bi, lam):
    S, W = proj.shape[0], LRU_WIDTH
    rt = _pick(S, ROW_TILE, SUBLANES)
    K = conv_w.shape[0]

    def body(x_ref, xp_ref, cw_ref, cb_ref, wr_ref, br_ref, wi_ref, bi_ref, lam_ref,
             xc_ref, r_ref, i_ref, la_ref, u_ref):
        prev = jnp.where(pl.program_id(0) == 0, 0.0, xp_ref[...])
        x = x_ref[...]
        xc = cb_ref[...] + cw_ref[K - 1:K, :] * x
        for k in range(K - 1):
            xc = xc + cw_ref[k:k + 1, :] * _shift_down(x, prev, K - 1 - k)
        xb = xc.astype(bf16)
        r = _sigmoid(_dot(xb, wr_ref[...], NN) + br_ref[...])
        i = _sigmoid(_dot(xb, wi_ref[...], NN) + bi_ref[...])
        log_a = -LRU_C * r * _softplus_neg(lam_ref[...])
        u = jnp.sqrt(-_expm1(2.0 * log_a)) * (i * xc)
        xc_ref[...], r_ref[...], i_ref[...], la_ref[...], u_ref[...] = xc, r, i, log_a, u

    row = pl.BlockSpec((rt, W), lambda i: (i, 0))
    xrow = pl.BlockSpec((rt, W), lambda i: (i, PROJ_LRU_X))
    halo = _prev_halo_spec(rt, W, lambda i: PROJ_LRU_X)
    vec = pl.BlockSpec((1, W), lambda i: (0, 0))
    return pl.pallas_call(
        body, name="lru_pre", grid=(S // rt,),
        in_specs=[xrow, halo, pl.BlockSpec((K, W), lambda i: (0, 0)), vec,
                  pl.BlockSpec((W, W), lambda i: (0, 0)), vec, pl.BlockSpec((W, W), lambda i: (0, 0)), vec, vec],
        out_specs=[row] * 5, out_shape=[jax.ShapeDtypeStruct((S, W), f32)] * 5, compiler_params=_cp("parallel"),
    )(proj, proj, conv_w, conv_b.reshape(1, W), wr, br.reshape(1, W), wi, bi.reshape(1, W), lam.reshape(1, W))


def _tile_rows(shape):
    return lax.broadcasted_iota(jnp.int32, shape, 0)


def _lru_scan(log_a, u, proj):
    S, W = u.shape
    rt = _pick(S, ROW_TILE, SUBLANES)
    T = SUBLANES

    def body(la_ref, u_ref, g_ref, h_ref, o_ref, carry):
        @pl.when(pl.program_id(0) == 0)
        def _():
            carry[...] = jnp.zeros_like(carry)

        row = _tile_rows((T, W))

        def step(t, hp):
            r0 = pl.multiple_of(t * T, T)
            a = jnp.exp(la_ref[pl.ds(r0, T), :])
            x = u_ref[pl.ds(r0, T), :]
            for k in (1, 2, 4):
                x = x + a * jnp.where(row >= k, pltpu.roll(x, k, 0), 0.0)
                a = a * jnp.where(row >= k, pltpu.roll(a, k, 0), 1.0)
            h = x + a * hp
            h_ref[pl.ds(r0, T), :] = h
            o_ref[pl.ds(r0, T), :] = h * _gelu(g_ref[pl.ds(r0, T), :])
            return h[T - 1:T, :]

        carry[0:1, :] = lax.fori_loop(0, rt // T, step, carry[0:1, :])

    row = pl.BlockSpec((rt, W), lambda i: (i, 0))
    grow = pl.BlockSpec((rt, W), lambda i: (i, PROJ_LRU_GATE))
    return pl.pallas_call(
        body, name="lru_scan", grid=(S // rt,), in_specs=[row, row, grow], out_specs=[row] * 2,
        out_shape=[jax.ShapeDtypeStruct((S, W), f32)] * 2, scratch_shapes=[pltpu.VMEM((T, W), f32)],
        compiler_params=_cp("arbitrary"),
    )(log_a, u, proj)


def _lru_scan_bwd(dlru, proj, h, log_a):
    S, W = h.shape
    rt = _pick(S, ROW_TILE, SUBLANES)
    T = SUBLANES
    nblk = S // rt

    def body(d_ref, g_ref, h_ref, la_ref, go_ref, dg_ref, carry):
        @pl.when(pl.program_id(0) == 0)
        def _():
            carry[...] = jnp.zeros_like(carry)

        row = _tile_rows((T, W))

        def step(j, c):
            gn, an = c
            t = rt // T - 1 - j
            r0 = pl.multiple_of(t * T, T)
            d = d_ref[pl.ds(r0, T), :]
            gate = g_ref[pl.ds(r0, T), :]
            a = jnp.exp(la_ref[pl.ds(r0, T), :])
            dg_ref[pl.ds(r0, T), :] = d * h_ref[pl.ds(r0, T), :] * _gelu_grad(gate)
            x = d * _gelu(gate)
            b = jnp.where(row < T - 1, pltpu.roll(a, T - 1, 0), an)
            for k in (1, 2, 4):
                x = x + b * jnp.where(row < T - k, pltpu.roll(x, T - k, 0), 0.0)
                b = b * jnp.where(row < T - k, pltpu.roll(b, T - k, 0), 1.0)
            g = x + b * gn
            go_ref[pl.ds(r0, T), :] = g
            return g[0:1, :], a[0:1, :]

        gn, an = lax.fori_loop(0, rt // T, step, (carry[0:1, :], carry[1:2, :]))
        carry[0:1, :] = gn
        carry[1:2, :] = an

    row = pl.BlockSpec((rt, W), lambda i: (nblk - 1 - i, 0))
    grow = pl.BlockSpec((rt, W), lambda i: (nblk - 1 - i, PROJ_LRU_GATE))
    return pl.pallas_call(
        body, name="lru_scan_bwd", grid=(nblk,), in_specs=[row, grow, row, row], out_specs=[row] * 2,
        out_shape=[jax.ShapeDtypeStruct((S, W), f32)] * 2, scratch_shapes=[pltpu.VMEM((T, W), f32)],
        compiler_params=_cp("arbitrary"),
    )(dlru, proj, h, log_a)


def _lru_gate_bwd(g, h, xc, r, i, log_a, wr, wi, lam):
    S, W = g.shape
    rt = _pick(S, ROW_TILE, SUBLANES)

    def body(g_ref, h_ref, hp_ref, xc_ref, r_ref, i_ref, la_ref, wr_ref, wi_ref, lam_ref,
             dxc_ref, dwr_ref, dwi_ref, acc_ref):
        @pl.when(pl.program_id(0) == 0)
        def _():
            dwr_ref[...] = jnp.zeros_like(dwr_ref)
            dwi_ref[...] = jnp.zeros_like(dwi_ref)
            acc_ref[...] = jnp.zeros_like(acc_ref)

        prev = jnp.where(pl.program_id(0) == 0, 0.0, hp_ref[...])
        gg, xc, r, i, log_a, lam = g_ref[...], xc_ref[...], r_ref[...], i_ref[...], la_ref[...], lam_ref[...]
        hm1 = _shift_down(h_ref[...], prev, 1)
        a = jnp.exp(log_a)
        s = jnp.sqrt(-_expm1(2.0 * log_a))
        da = gg * hm1
        di = gg * s * xc
        dxc = gg * s * i
        ds = gg * i * xc
        dlog_a = da * a - ds * (a * a / s)
        sp = _softplus_neg(lam)
        dr = dlog_a * (-LRU_C * sp)
        dsp = jnp.sum(dlog_a * (-LRU_C * r), axis=0, keepdims=True)
        dpr = dr * r * (1.0 - r)
        dpi = di * i * (1.0 - i)
        dprb, dpib, xb = dpr.astype(bf16), dpi.astype(bf16), xc.astype(bf16)
        dxc_ref[...] = dxc + _dot(dprb, wr_ref[...], NT) + _dot(dpib, wi_ref[...], NT)
        dwr_ref[...] += _dot(xb, dprb, TN)
        dwi_ref[...] += _dot(xb, dpib, TN)
        acc_ref[0:1, :] += jnp.sum(dpr, axis=0, keepdims=True)
        acc_ref[1:2, :] += jnp.sum(dpi, axis=0, keepdims=True)
        acc_ref[2:3, :] += dsp * (-_sigmoid(-lam))

    row = pl.BlockSpec((rt, W), lambda i: (i, 0))
    halo = _prev_halo_spec(rt, W, lambda i: 0)
    vec = pl.BlockSpec((1, W), lambda i: (0, 0))
    mat = pl.BlockSpec((W, W), lambda i: (0, 0))
    acc = pl.BlockSpec((SUBLANES, W), lambda i: (0, 0))
    return pl.pallas_call(
        body, name="lru_gate_bwd", grid=(S // rt,),
        in_specs=[row, row, halo, row, row, row, row, mat, mat, vec], out_specs=[row, mat, mat, acc],
        out_shape=[jax.ShapeDtypeStruct((S, W), f32), jax.ShapeDtypeStruct((W, W), f32),
                   jax.ShapeDtypeStruct((W, W), f32), jax.ShapeDtypeStruct((SUBLANES, W), f32)],
        compiler_params=_cp("arbitrary"),
    )(g, h, h, xc, r, i, log_a, wr, wi, lam.reshape(1, W))


def _conv_bwd(dy, x, conv_w, name, col_tile=None, out_dtype=f32, x_col_block=0, send=(), send_dst=0):
    if dy.ndim == 2:
        dy = dy[None]
    H, S, Ch = dy.shape
    C = H * Ch
    K = conv_w.shape[0]
    ct = Ch if col_tile is None else col_tile
    nct = Ch // ct
    rt = _pick(S, ROW_TILE, SUBLANES)
    nrt = S // rt
    snd = _SiblingSend(send, send_dst)
    n = snd.n

    def body(*refs):
        dy_ref, dyn_ref, x_ref, w_ref = refs[:4]
        s_ins = refs[4:4 + n]
        dx_ref, acc_ref = refs[4 + n:6 + n]
        s_outs, s_sems = refs[6 + n:6 + 2 * n], refs[6 + 2 * n:]
        i = pl.program_id(2)
        if n:
            @pl.when((pl.program_id(0) == 0) & (pl.program_id(1) == 0) & (i == 0))
            def _():
                snd.start(s_ins, s_outs, s_sems)

        @pl.when(i == 0)
        def _():
            acc_ref[...] = jnp.zeros_like(acc_ref)

        nxt = jnp.where(i == nrt - 1, 0.0, dyn_ref[...])
        dy, x = dy_ref[...], x_ref[...]
        ahead = [dy] + [_shift_up(dy, nxt, j) for j in range(1, K)]
        dx = w_ref[K - 1:K, :] * dy
        for k in range(K - 1):
            dx = dx + w_ref[k:k + 1, :] * ahead[K - 1 - k]
        dx_ref[...] = dx.astype(dx_ref.dtype)
        for k in range(K):
            acc_ref[k:k + 1, :] += jnp.sum(ahead[K - 1 - k] * x, axis=0, keepdims=True)
        acc_ref[K:K + 1, :] += jnp.sum(dy, axis=0, keepdims=True)
        if n:
            @pl.when((pl.program_id(0) == H - 1) & (pl.program_id(1) == nct - 1) & (i == nrt - 1))
            def _():
                snd.finish(s_ins, s_outs, s_sems)

    per, last = rt // SUBLANES, S // SUBLANES - 1
    dy_row = pl.BlockSpec((None, rt, ct), lambda h, j, i: (h, i, j))
    dy_next = pl.BlockSpec((None, SUBLANES, ct), lambda h, j, i: (h, jnp.minimum((i + 1) * per, last), j))
    row = pl.BlockSpec((rt, ct), lambda h, j, i: (i, h * nct + j))
    xrow = pl.BlockSpec((rt, ct), lambda h, j, i: (i, h * nct + j + x_col_block))
    res = pl.pallas_call(
        body, name=name, grid=(H, nct, nrt),
        in_specs=[dy_row, dy_next, xrow, pl.BlockSpec((K, ct), lambda h, j, i: (0, h * nct + j))] + snd.in_specs,
        out_specs=[row, pl.BlockSpec((SUBLANES, ct), lambda h, j, i: (0, h * nct + j))] + snd.out_specs,
        out_shape=[jax.ShapeDtypeStruct((S, C), out_dtype), jax.ShapeDtypeStruct((SUBLANES, C), f32)] + snd.out_shape,
        scratch_shapes=snd.scratch,
        compiler_params=_cp(*(("arbitrary",) * 3 if n else ("parallel", "parallel", "arbitrary"))),
    )(dy, dy, x, conv_w, *send)
    return res[0], res[1], list(res[2:])


def _s5_param_fn(a_re, a_im, ls, bt_re, bt_im):
    step = jnp.exp(ls)
    dt_re, dt_im = step * a_re, step * a_im
    mag = jnp.exp(dt_re)
    ab_re, ab_im = mag * jnp.cos(dt_im), mag * jnp.sin(dt_im)
    z_re, z_im = ab_re - 1.0, ab_im
    den = a_re * a_re + a_im * a_im
    f_re = (z_re * a_re + z_im * a_im) / den
    f_im = (z_im * a_re - z_re * a_im) / den
    bb_re = f_re[:, None, :] * bt_re - f_im[:, None, :] * bt_im
    bb_im = f_re[:, None, :] * bt_im + f_im[:, None, :] * bt_re
    return ab_re, ab_im, bb_re, bb_im


def _s5_params(a_re, a_im, ls, bt_re, bt_im):
    def body(ar, ai, l, br, bi, o_ar, o_ai, o_br, o_bi):
        o_ar[...], o_ai[...], o_br[...], o_bi[...] = _s5_param_fn(ar[...], ai[...], l[...], br[...], bi[...])

    return pl.pallas_call(
        body, name="s5_params",
        out_shape=[jax.ShapeDtypeStruct(a_re.shape, f32)] * 2 + [jax.ShapeDtypeStruct(bt_re.shape, f32)] * 2,
        compiler_params=_cp(),
    )(a_re, a_im, ls, bt_re, bt_im)


def _s5_params_bwd(a_re, a_im, ls, bt_re, bt_im, d_ar, d_ai, d_br, d_bi):
    def body(ar, ai, l, br, bi, c_ar, c_ai, c_br, c_bi, g_ar, g_ai, g_l, g_br, g_bi):
        _, vjp = jax.vjp(_s5_param_fn, ar[...], ai[...], l[...], br[...], bi[...])
        g_ar[...], g_ai[...], g_l[...], g_br[...], g_bi[...] = vjp((c_ar[...], c_ai[...], c_br[...], c_bi[...]))

    return pl.pallas_call(
        body, name="s5_params_bwd",
        out_shape=[jax.ShapeDtypeStruct(a_re.shape, f32)] * 2 + [jax.ShapeDtypeStruct(ls.shape, f32)]
        + [jax.ShapeDtypeStruct(bt_re.shape, f32)] * 2,
        compiler_params=_cp(),
    )(a_re, a_im, ls, bt_re, bt_im, d_ar, d_ai, d_br, d_bi)


S5_CHUNK = 256


def _s5_power_tables(ab_ref, p_ref, w_ref, conj):
    T, L = SUBLANES, S5_LANES
    are = ab_ref[0:1, 0:L]
    aim = ab_ref[0:1, L:2 * L]
    if conj:
        aim = -aim
    pre, pim = are, aim
    for n in range(3):
        p_ref[n:n + 1, 0:L] = pre
        p_ref[n:n + 1, L:2 * L] = pim
        pre, pim = pre * pre - pim * pim, 2.0 * pre * pim
    row = _tile_rows((T, L))
    wre = jnp.zeros((T, L), f32)
    wim = jnp.zeros((T, L), f32)
    pre, pim = are, aim
    for n in range(T):
        tgt = (T - 1 - n) if conj else n
        wre = jnp.where(row == tgt, pre, wre)
        wim = jnp.where(row == tgt, pim, wim)
        pre, pim = pre * are - pim * aim, pre * aim + pim * are
    w_ref[:, 0:L] = wre
    w_ref[:, L:2 * L] = wim


def _s5_scan(bu, ab):
    S, L2 = bu.shape
    L = L2 // 2
    rt = _pick(S, 256, SUBLANES)
    T = SUBLANES
    CH = S5_CHUNK

    def body(bu_ref, ab_ref, x_ref, p_ref, w_ref, carry):
        @pl.when(pl.program_id(0) == 0)
        def _():
            carry[...] = jnp.zeros_like(carry)
            _s5_power_tables(ab_ref, p_ref, w_ref, conj=False)

        row = _tile_rows((T, CH))

        def step(t, _):
            r0 = pl.multiple_of(t * T, T)
            for c in range(L // CH):
                lre, lim = pl.ds(c * CH, CH), pl.ds(L + c * CH, CH)
                xr, xi = bu_ref[pl.ds(r0, T), lre], bu_ref[pl.ds(r0, T), lim]
                for n, k in enumerate((1, 2, 4)):
                    pr, pi = p_ref[n:n + 1, lre], p_ref[n:n + 1, lim]
                    sr = jnp.where(row >= k, pltpu.roll(xr, k, 0), 0.0)
                    si = jnp.where(row >= k, pltpu.roll(xi, k, 0), 0.0)
                    xr, xi = xr + pr * sr - pi * si, xi + pr * si + pi * sr
                cr, ci = carry[T - 1:T, lre], carry[T - 1:T, lim]
                wr, wi = w_ref[:, lre], w_ref[:, lim]
                xr, xi = xr + wr * cr - wi * ci, xi + wr * ci + wi * cr
                carry[:, lre] = xr
                carry[:, lim] = xi
                x_ref[pl.ds(r0, T), lre] = xr
                x_ref[pl.ds(r0, T), lim] = xi
            return 0

        lax.fori_loop(0, rt // T, step, 0)

    row_spec = pl.BlockSpec((rt, L2), lambda i: (i, 0))
    return pl.pallas_call(
        body, name="s5_scan", grid=(S // rt,), in_specs=[row_spec, pl.BlockSpec((1, L2), lambda i: (0, 0))],
        out_specs=row_spec, out_shape=jax.ShapeDtypeStruct((S, L2), f32),
        scratch_shapes=[pltpu.VMEM((T, L2), f32), pltpu.VMEM((T, L2), f32), pltpu.VMEM((T, L2), f32)],
        compiler_params=_cp("arbitrary"),
    )(bu, ab)


def _s5_scan_bwd(dx, x, ab):
    S, L2 = dx.shape
    L = L2 // 2
    rt = _pick(S, 256, SUBLANES)
    T = SUBLANES
    CH = S5_CHUNK
    nblk = S // rt
    per = rt // T

    def body(dx_ref, x_ref, xp_ref, ab_ref, g_ref, da_ref, p_ref, w_ref, carry, acc):
        pid = pl.program_id(0)

        @pl.when(pid == 0)
        def _():
            carry[...] = jnp.zeros_like(carry)
            acc[...] = jnp.zeros_like(acc)
            _s5_power_tables(ab_ref, p_ref, w_ref, conj=True)

        row = _tile_rows((T, CH))
        first_block = pid == nblk - 1

        def step(j, _):
            t = per - 1 - j
            r0 = pl.multiple_of(t * T, T)
            rp = pl.multiple_of(jnp.maximum(t - 1, 0) * T, T)
            for c in range(L // CH):
                lre, lim = pl.ds(c * CH, CH), pl.ds(L + c * CH, CH)
                gr, gi = dx_ref[pl.ds(r0, T), lre], dx_ref[pl.ds(r0, T), lim]
                for n, k in enumerate((1, 2, 4)):
                    pr, pi = p_ref[n:n + 1, lre], p_ref[n:n + 1, lim]
                    sr = jnp.where(row < T - k, pltpu.roll(gr, T - k, 0), 0.0)
                    si = jnp.where(row < T - k, pltpu.roll(gi, T - k, 0), 0.0)
                    gr, gi = gr + pr * sr - pi * si, gi + pr * si + pi * sr
                cr, ci = carry[0:1, lre], carry[0:1, lim]
                wr, wi = w_ref[:, lre], w_ref[:, lim]
                gr, gi = gr + wr * cr - wi * ci, gi + wr * ci + wi * cr
                carry[:, lre] = gr
                carry[:, lim] = gi
                g_ref[pl.ds(r0, T), lre] = gr
                g_ref[pl.ds(r0, T), lim] = gi
                xr, xi = x_ref[pl.ds(r0, T), lre], x_ref[pl.ds(r0, T), lim]
                in_blk_r, in_blk_i = x_ref[pl.ds(rp, T), lre], x_ref[pl.ds(rp, T), lim]
                hal_r = jnp.where(first_block, 0.0, xp_ref[:, lre])
                hal_i = jnp.where(first_block, 0.0, xp_ref[:, lim])
                pvr = jnp.where(t == 0, hal_r, in_blk_r)[T - 1:T, :]
                pvi = jnp.where(t == 0, hal_i, in_blk_i)[T - 1:T, :]
                sxr = jnp.where(row >= 1, pltpu.roll(xr, 1, 0), pvr)
                sxi = jnp.where(row >= 1, pltpu.roll(xi, 1, 0), pvi)
                acc[:, lre] += gr * sxr + gi * sxi
                acc[:, lim] += gi * sxr - gr * sxi
            return 0

        lax.fori_loop(0, per, step, 0)

        @pl.when(pid == nblk - 1)
        def _():
            da_ref[...] = jnp.sum(acc[...], axis=0, keepdims=True)

    row_spec = pl.BlockSpec((rt, L2), lambda i: (nblk - 1 - i, 0))
    halo = pl.BlockSpec((T, L2), lambda i: (jnp.maximum((nblk - 1 - i) * per - 1, 0), 0))
    vec = pl.BlockSpec((1, L2), lambda i: (0, 0))
    return pl.pallas_call(
        body, name="s5_scan_bwd", grid=(nblk,), in_specs=[row_spec, row_spec, halo, vec],
        out_specs=[row_spec, vec],
        out_shape=[jax.ShapeDtypeStruct((S, L2), f32), jax.ShapeDtypeStruct((1, L2), f32)],
        scratch_shapes=[pltpu.VMEM((T, L2), f32)] * 4,
        compiler_params=_cp("arbitrary"),
    )(dx, x, x, ab)


def _S5_U_SPEC(rt):
    return pl.BlockSpec((rt, LRU_WIDTH), lambda i: (i, PROJ_S5_U))


def _s5_out(yc, proj, d, wglu, bglu):
    S, W = yc.shape
    rt = _pick(S, ROW_TILE, SUBLANES)

    def body(yc_ref, u_ref, d_ref, w_ref, b_ref, o_ref, y_ref):
        y = yc_ref[...] + d_ref[...] * u_ref[:, 0:W]
        yg = _gelu(y)
        z = _dot(yg.astype(bf16), w_ref[...], NN) + b_ref[...]
        o_ref[...] = yg * _sigmoid(z)
        y_ref[...] = y

    row = pl.BlockSpec((rt, W), lambda i: (i, 0))
    vec = pl.BlockSpec((1, W), lambda i: (0, 0))
    mat = pl.BlockSpec((W, W), lambda i: (0, 0))
    return pl.pallas_call(
        body, name="s5_out", grid=(S // rt,), in_specs=[row, _S5_U_SPEC(rt), vec, mat, vec], out_specs=[row, row],
        out_shape=[jax.ShapeDtypeStruct((S, W), f32)] * 2, compiler_params=_cp("parallel"),
    )(yc, proj, d.reshape(1, W), wglu, bglu.reshape(1, W))


def _s5_out_bwd(dssm, y, proj, d, wglu, bglu):
    S, W = y.shape
    rt = _pick(S, ROW_TILE, SUBLANES)

    def body(do_ref, y_ref, u_ref, d_ref, w_ref, b_ref, dy_ref, du_ref, dw_ref, acc_ref):
        @pl.when(pl.program_id(0) == 0)
        def _():
            dw_ref[...] = jnp.zeros_like(dw_ref)
            acc_ref[...] = jnp.zeros_like(acc_ref)

        do, y = do_ref[...], y_ref[...]
        yg = _gelu(y)
        ygb = yg.astype(bf16)
        sg = _sigmoid(_dot(ygb, w_ref[...], NN) + b_ref[...])
        dz = do * yg * sg * (1.0 - sg)
        dzb = dz.astype(bf16)
        dyg = do * sg + _dot(dzb, w_ref[...], NT)
        dy = dyg * _gelu_grad(y)
        dy_ref[...] = dy
        du_ref[...] = dy * d_ref[...]
        dw_ref[...] += _dot(ygb, dzb, TN)
        acc_ref[0:1, :] += jnp.sum(dz, axis=0, keepdims=True)
        acc_ref[1:2, :] += jnp.sum(dy * u_ref[:, 0:W], axis=0, keepdims=True)

    row = pl.BlockSpec((rt, W), lambda i: (i, 0))
    vec = pl.BlockSpec((1, W), lambda i: (0, 0))
    mat = pl.BlockSpec((W, W), lambda i: (0, 0))
    acc = pl.BlockSpec((SUBLANES, W), lambda i: (0, 0))
    return pl.pallas_call(
        body, name="s5_out_bwd", grid=(S // rt,), in_specs=[row, row, _S5_U_SPEC(rt), vec, mat, vec],
        out_specs=[row, row, mat, acc],
        out_shape=[jax.ShapeDtypeStruct((S, W), f32)] * 2
        + [jax.ShapeDtypeStruct((W, W), f32), jax.ShapeDtypeStruct((SUBLANES, W), f32)],
        compiler_params=_cp("arbitrary"),
    )(dssm, y, proj, d.reshape(1, W), wglu, bglu.reshape(1, W))


MIX_SPLITS = ((0, ATTN_WIDTH), (ATTN_WIDTH, ATTN_WIDTH + LRU_WIDTH), (ATTN_WIDTH + LRU_WIDTH, D_MODEL))


def _mixnorm(attn, lru, ssm, g):
    S = attn.shape[0]
    rt = _pick(S, ROW_TILE, SUBLANES)

    def body(a_ref, l_ref, s_ref, g_ref, o_ref):
        for ref, (lo, hi) in zip((a_ref, l_ref, s_ref), MIX_SPLITS):
            x = ref[...]
            ms = jnp.mean(x * x, axis=-1, keepdims=True)
            o_ref[:, lo:hi] = (x * lax.rsqrt(ms + RMS_EPS) * g_ref[:, lo:hi]).astype(o_ref.dtype)

    rows = [pl.BlockSpec((rt, hi - lo), lambda i: (i, 0)) for lo, hi in MIX_SPLITS]
    return pl.pallas_call(
        body, name="mixnorm", grid=(S // rt,), in_specs=rows + [pl.BlockSpec((1, D_MODEL), lambda i: (0, 0))],
        out_specs=pl.BlockSpec((rt, D_MODEL), lambda i: (i, 0)),
        out_shape=jax.ShapeDtypeStruct((S, D_MODEL), bf16), compiler_params=_cp("parallel"),
    )(attn, lru, ssm, g.reshape(1, D_MODEL))


def _mixnorm_bwd(dmixed, attn, lru, ssm, g):
    S = attn.shape[0]
    rt = _pick(S, ROW_TILE, SUBLANES)

    def body(d_ref, a_ref, l_ref, s_ref, g_ref, da_ref, dl_ref, ds_ref, dlt_ref, acc_ref):
        @pl.when(pl.program_id(0) == 0)
        def _():
            acc_ref[...] = jnp.zeros_like(acc_ref)

        outs = []
        for ref, (lo, hi) in zip((a_ref, l_ref, s_ref), MIX_SPLITS):
            x = ref[...]
            dy = d_ref[:, lo:hi]
            rinv = lax.rsqrt(jnp.mean(x * x, axis=-1, keepdims=True) + RMS_EPS)
            dyg = dy * g_ref[:, lo:hi]
            outs.append(rinv * dyg - x * (rinv * rinv * rinv) * jnp.mean(dyg * x, axis=-1, keepdims=True))
            acc_ref[0:1, lo:hi] += jnp.sum(dy * x * rinv, axis=0, keepdims=True)
        da_ref[...], dl_ref[...], ds_ref[...] = outs
        hi_ = lax.broadcasted_iota(jnp.int32, (ATTN_WIDTH, ATTN_WIDTH), 0) // HEAD_DIM
        hj_ = lax.broadcasted_iota(jnp.int32, (ATTN_WIDTH, ATTN_WIDTH), 1) // HEAD_DIM
        same = jnp.where(hi_ == hj_, 1.0, 0.0).astype(f32)
        dlt_ref[...] = jnp.dot(outs[0] * a_ref[...], same, precision=lax.Precision.HIGHEST, preferred_element_type=f32)

    rows = [pl.BlockSpec((rt, hi - lo), lambda i: (i, 0)) for lo, hi in MIX_SPLITS]
    full = pl.BlockSpec((rt, D_MODEL), lambda i: (i, 0))
    return pl.pallas_call(
        body, name="mixnorm_bwd", grid=(S // rt,),
        in_specs=[full] + rows + [pl.BlockSpec((1, D_MODEL), lambda i: (0, 0))],
        out_specs=rows + [rows[0], pl.BlockSpec((SUBLANES, D_MODEL), lambda i: (0, 0))],
        out_shape=[jax.ShapeDtypeStruct((S, hi - lo), f32) for lo, hi in MIX_SPLITS]
        + [jax.ShapeDtypeStruct((S, ATTN_WIDTH), f32), jax.ShapeDtypeStruct((SUBLANES, D_MODEL), f32)],
        compiler_params=_cp("arbitrary"),
    )(dmixed, attn, lru, ssm, g.reshape(1, D_MODEL))


FFN_COL_TILE = 512


def _ffn_conv(x, prev, w_ref, b_ref, K):
    y = b_ref[...] + w_ref[K - 1:K, :] * x
    for k in range(K - 1):
        y = y + w_ref[k:k + 1, :] * _shift_down(x, prev, K - 1 - k)
    return y


def _ffn_act(up, conv_w, conv_b):
    S, C2 = up.shape
    C = C2 // 2
    K = conv_w.shape[0]
    ct = FFN_COL_TILE
    nct = C // ct
    rt = _pick(S, ROW_TILE, SUBLANES)

    def body(g_ref, gp_ref, v_ref, vp_ref, wg_ref, wv_ref, bg_ref, bv_ref, o_ref):
        first = pl.program_id(1) == 0
        gate = _ffn_conv(g_ref[...], jnp.where(first, 0.0, gp_ref[...]), wg_ref, bg_ref, K)
        val = _ffn_conv(v_ref[...], jnp.where(first, 0.0, vp_ref[...]), wv_ref, bv_ref, K)
        o_ref[...] = (_gelu(gate) * val).astype(o_ref.dtype)

    def specs(off):
        return (pl.BlockSpec((rt, ct), lambda j, i: (i, j + off)), _prev_halo_spec(rt, ct, lambda j, i: j + off))

    def wspec(off, rows):
        return pl.BlockSpec((rows, ct), lambda j, i: (0, j + off))

    g_s, gp_s = specs(0)
    v_s, vp_s = specs(nct)
    return pl.pallas_call(
        body, name="ffn_act", grid=(nct, S // rt),
        in_specs=[g_s, gp_s, v_s, vp_s, wspec(0, K), wspec(nct, K), wspec(0, 1), wspec(nct, 1)],
        out_specs=pl.BlockSpec((rt, ct), lambda j, i: (i, j)),
        out_shape=jax.ShapeDtypeStruct((S, C), bf16), compiler_params=_cp("parallel", "parallel"),
    )(up, up, up, up, conv_w, conv_w, conv_b.reshape(1, C2), conv_b.reshape(1, C2))


def _ffn_act_bwd(dact, up, conv_w, conv_b):
    S, C2 = up.shape
    C = C2 // 2
    K = conv_w.shape[0]
    ct = FFN_COL_TILE
    nct = C // ct
    rt = _pick(S, ROW_TILE, SUBLANES)

    def body(d_ref, g_ref, gp_ref, v_ref, vp_ref, wg_ref, wv_ref, bg_ref, bv_ref, o_ref):
        first = pl.program_id(1) == 0
        gate = _ffn_conv(g_ref[...], jnp.where(first, 0.0, gp_ref[...]), wg_ref, bg_ref, K)
        val = _ffn_conv(v_ref[...], jnp.where(first, 0.0, vp_ref[...]), wv_ref, bv_ref, K)
        d = d_ref[...]
        gl, dgl = _gelu_pair(gate)
        o_ref[0] = d * val * dgl
        o_ref[1] = d * gl

    def specs(off):
        return (pl.BlockSpec((rt, ct), lambda j, i: (i, j + off)), _prev_halo_spec(rt, ct, lambda j, i: j + off))

    def wspec(off, rows):
        return pl.BlockSpec((rows, ct), lambda j, i: (0, j + off))

    g_s, gp_s = specs(0)
    v_s, vp_s = specs(nct)
    return pl.pallas_call(
        body, name="ffn_act_bwd", grid=(nct, S // rt),
        in_specs=[pl.BlockSpec((rt, ct), lambda j, i: (i, j)), g_s, gp_s, v_s, vp_s,
                  wspec(0, K), wspec(nct, K), wspec(0, 1), wspec(nct, 1)],
        out_specs=pl.BlockSpec((2, rt, ct), lambda j, i: (0, i, j)),
        out_shape=jax.ShapeDtypeStruct((2, S, C), f32), compiler_params=_cp("parallel", "parallel"),
    )(dact, up, up, up, up, conv_w, conv_w, conv_b.reshape(1, C2), conv_b.reshape(1, C2))


ANY = pl.BlockSpec(memory_space=pl.ANY)


def _rows_for(cols):
    return max(16, (1 << 17) // cols)


def _chips(x, y):
    return [(1 - x, y), (x, 1 - y), (1 - x, 1 - y)]


class _Gather:
    def __init__(self, shards, split):
        self.shapes = [s.shape for s in shards]
        self.dtypes = [s.dtype for s in shards]
        self.split = list(split)
        self.nt = len(shards)
        self.in_specs = [ANY] * self.nt
        self.out_specs = [ANY] * self.nt
        self.out_shape = [jax.ShapeDtypeStruct((4,) + s, dt) for s, dt in zip(self.shapes, self.dtypes)]
        self.scratch = [pltpu.SemaphoreType.DMA((3, self.nt))] * 4 if self.nt else []

    def _part(self, ref, t, half):
        if not self.split[t]:
            return ref
        r = self.shapes[t][0] // 2
        return ref.at[pl.ds(half * r, r), :]

    def _ici(self, ins, outs, sems, k, t, chip, landing_chip):
        x, y, c = lax.axis_index("x"), lax.axis_index("y"), lax.axis_index("c")
        return pltpu.make_async_remote_copy(
            src_ref=self._part(ins[t], t, c), dst_ref=self._part(outs[t].at[landing_chip], t, c),
            send_sem=sems[0].at[k, t], recv_sem=sems[1].at[k, t], device_id=(chip[0], chip[1], c), device_id_type=MESH)

    def _d2d(self, outs, sems, k, t, q, half):
        x, y, c = lax.axis_index("x"), lax.axis_index("y"), lax.axis_index("c")
        rows = self._part(outs[t].at[q], t, half)
        return pltpu.make_async_remote_copy(
            src_ref=rows, dst_ref=rows, send_sem=sems[2].at[k, t], recv_sem=sems[3].at[k, t],
            device_id=(x, y, 1 - c), device_id_type=MESH)

    def start(self, ins, outs, sems):
        x, y = lax.axis_index("x"), lax.axis_index("y")
        me = 2 * x + y
        for k, chip in enumerate(_chips(x, y)):
            for t in range(self.nt):
                self._ici(ins, outs, sems, k, t, chip, me).start()

    def finish(self, ins, outs, sems):
        x, y, c = lax.axis_index("x"), lax.axis_index("y"), lax.axis_index("c")
        me = 2 * x + y
        chips = _chips(x, y)
        for k, chip in enumerate(chips):
            q = 2 * chip[0] + chip[1]
            for t in range(self.nt):
                self._ici(ins, outs, sems, k, t, chip, q).wait_recv()
                if self.split[t]:
                    self._d2d(outs, sems, k, t, q, c).start()
        for k, chip in enumerate(chips):
            q = 2 * chip[0] + chip[1]
            for t in range(self.nt):
                if self.split[t]:
                    self._d2d(outs, sems, k, t, q, 1 - c).wait_recv()
        for k, chip in enumerate(chips):
            q = 2 * chip[0] + chip[1]
            for t in range(self.nt):
                self._ici(ins, outs, sems, k, t, chip, me).wait_send()
                if self.split[t]:
                    self._d2d(outs, sems, k, t, q, c).wait_send()


def _gather_weights(shards, split):
    g = _Gather(shards, split)
    nt = g.nt

    def body(*refs):
        ins, outs, sems = refs[:nt], refs[nt:2 * nt], refs[2 * nt:]
        g.start(ins, outs, sems)
        g.finish(ins, outs, sems)

    return pl.pallas_call(
        body, name="gather_weights", in_specs=g.in_specs, out_specs=g.out_specs, out_shape=g.out_shape,
        scratch_shapes=g.scratch,
    )(*shards)


class _SiblingSend:
    def __init__(self, gs, dst_c, swap=()):
        self.nt, self.n, self.dst_c = len(gs), len(gs) + len(swap), dst_c
        self.in_specs = [ANY] * self.n
        self.out_specs = [ANY] * self.n
        self.out_shape = [jax.ShapeDtypeStruct(g.shape, g.dtype) for g in list(gs) + list(swap)]
        self.scratch = [pltpu.SemaphoreType.DMA((self.n,))] * 2 if self.n else []

    def _each(self, ins, outs, sems, sender, fn):
        x, y, c = lax.axis_index("x"), lax.axis_index("y"), lax.axis_index("c")

        def cp(t):
            return pltpu.make_async_remote_copy(
                src_ref=ins[t], dst_ref=outs[t], send_sem=sems[0].at[t], recv_sem=sems[1].at[t],
                device_id=(x, y, 1 - c), device_id_type=MESH)

        if self.nt:
            @pl.when((c != self.dst_c) if sender else (c == self.dst_c))
            def _():
                for t in range(self.nt):
                    fn(cp(t))
        for t in range(self.nt, self.n):
            fn(cp(t))

    def start(self, ins, outs, sems):
        self._each(ins, outs, sems, True, lambda cp: cp.start())

    def finish(self, ins, outs, sems):
        self._each(ins, outs, sems, False, lambda cp: cp.wait_recv())
        self._each(ins, outs, sems, True, lambda cp: cp.wait_send())


def _sibling_send(gs, dst_c, swap=()):
    snd = _SiblingSend(gs, dst_c, swap)
    n = snd.n

    def body(*refs):
        ins, outs, sems = refs[:n], refs[n:2 * n], refs[2 * n:]
        snd.start(ins, outs, sems)
        snd.finish(ins, outs, sems)

    res = pl.pallas_call(
        body, name="sibling_send", in_specs=snd.in_specs, out_specs=snd.out_specs, out_shape=snd.out_shape,
        scratch_shapes=snd.scratch,
    )(*gs, *swap)
    return list(res[:snd.nt]), list(res[snd.nt:])


def _owner_flag(owner_c):
    return (lax.axis_index("c") == owner_c).astype(jnp.int32).reshape(1)


def _pair_sum(g, other, name, owner_c):
    R, C = g.shape
    rt = _pick(R, _rows_for(C), 16)

    def body(on_ref, a_ref, o_ref, out_ref):
        out_ref[...] = (a_ref[...] + o_ref[...]).astype(out_ref.dtype)

    row = pl.BlockSpec((rt, C), lambda i, on: (i * on[0], 0))
    return pl.pallas_call(
        body, name=name,
        grid_spec=pltpu.PrefetchScalarGridSpec(num_scalar_prefetch=1, grid=(R // rt,), in_specs=[row, row],
                                               out_specs=row),
        out_shape=jax.ShapeDtypeStruct((R, C), bf16), compiler_params=_cp("arbitrary"),
    )(_owner_flag(owner_c), g, other)


class _ChipExchange:
    def __init__(self, slabs, whole, only_c):
        self.ns, self.nw, self.only_c = len(slabs), len(whole), only_c
        self.n = self.ns + self.nw
        self.in_specs = [ANY] * self.n
        self.out_specs = [ANY] * self.n
        self.out_shape = ([jax.ShapeDtypeStruct(s.shape, s.dtype) for s in slabs]
                          + [jax.ShapeDtypeStruct((4,) + w.shape, w.dtype) for w in whole])
        self.scratch = [pltpu.SemaphoreType.DMA((3, self.n))] * 2 if self.n else []

    def _copy(self, ins, outs, sems, k, t, chip, landing_chip):
        c = lax.axis_index("c")
        src = ins[t].at[2 * chip[0] + chip[1]] if t < self.ns else ins[t]
        return pltpu.make_async_remote_copy(
            src_ref=src, dst_ref=outs[t].at[landing_chip], send_sem=sems[0].at[k, t], recv_sem=sems[1].at[k, t],
            device_id=(chip[0], chip[1], c), device_id_type=MESH)

    def _each(self, fn):
        x, y, c = lax.axis_index("x"), lax.axis_index("y"), lax.axis_index("c")
        chips = _chips(x, y)
        if self.ns:
            @pl.when(c == self.only_c)
            def _():
                for k, chip in enumerate(chips):
                    for t in range(self.ns):
                        fn(k, t, chip)
        for k, chip in enumerate(chips):
            for t in range(self.ns, self.n):
                fn(k, t, chip)

    def start(self, ins, outs, sems):
        me = 2 * lax.axis_index("x") + lax.axis_index("y")
        self._each(lambda k, t, chip: self._copy(ins, outs, sems, k, t, chip, me).start())

    def finish(self, ins, outs, sems):
        me = 2 * lax.axis_index("x") + lax.axis_index("y")
        self._each(lambda k, t, chip: self._copy(ins, outs, sems, k, t, chip, 2 * chip[0] + chip[1]).wait_recv())
        self._each(lambda k, t, chip: self._copy(ins, outs, sems, k, t, chip, me).wait_send())


def _chip_exchange(slabs, whole, only_c):
    ex = _ChipExchange(slabs, whole, only_c)
    n = ex.n

    def body(*refs):
        ins, outs, sems = refs[:n], refs[n:2 * n], refs[2 * n:]
        ex.start(ins, outs, sems)
        ex.finish(ins, outs, sems)

    res = pl.pallas_call(
        body, name="chip_exchange", in_specs=ex.in_specs, out_specs=ex.out_specs, out_shape=ex.out_shape,
        scratch_shapes=ex.scratch,
    )(*slabs, *whole)
    return list(res[:ex.ns]), list(res[ex.ns:])


def _sum_chips(recv, own, name, owner_c=None):
    n, r, C = recv.shape
    rt = _pick(r, _rows_for(C), 16)
    own3 = own.ndim == 3

    def body(on_ref, r_ref, o_ref, out_ref):
        me = 2 * lax.axis_index("x") + lax.axis_index("y")
        acc = None
        for q in range(n):
            term = jnp.where(me == q, o_ref[q] if own3 else o_ref[...], r_ref[q]).astype(f32)
            acc = term if acc is None else acc + term
        out_ref[...] = acc

    blk = pl.BlockSpec((n, rt, C), lambda i, on: (0, i * on[0], 0))
    row = pl.BlockSpec((rt, C), lambda i, on: (i * on[0], 0))
    flag = jnp.ones((1,), jnp.int32) if owner_c is None else _owner_flag(owner_c)
    return pl.pallas_call(
        body, name=name,
        grid_spec=pltpu.PrefetchScalarGridSpec(num_scalar_prefetch=1, grid=(r // rt,),
                                               in_specs=[blk, blk if own3 else row], out_specs=row),
        out_shape=jax.ShapeDtypeStruct((r, C), f32), compiler_params=_cp("arbitrary"),
    )(flag, recv, own)


def _pair_swap(parts0, parts1):
    nt = len(parts0)

    def body(*refs):
        in0, in1, outs = refs[:nt], refs[nt:2 * nt], refs[2 * nt:3 * nt]
        send, recv = refs[3 * nt:]
        x, y, c = lax.axis_index("x"), lax.axis_index("y"), lax.axis_index("c")

        def copies(ins):
            return [pltpu.make_async_remote_copy(
                src_ref=ins[t], dst_ref=outs[t], send_sem=send.at[t], recv_sem=recv.at[t],
                device_id=(x, y, 1 - c), device_id_type=MESH) for t in range(nt)]

        @pl.when(c == 1)
        def _():
            for cp in copies(in0):
                cp.start()

        @pl.when(c == 0)
        def _():
            for cp in copies(in1):
                cp.start()

        for cp in copies(in0):
            cp.wait()

    return pl.pallas_call(
        body, name="pair_swap", in_specs=[ANY] * (2 * nt), out_specs=[ANY] * nt,
        out_shape=[jax.ShapeDtypeStruct(p.shape, p.dtype) for p in parts0],
        scratch_shapes=[pltpu.SemaphoreType.DMA((nt,))] * 2,
    )(*parts0, *parts1)


def _adamw_layers(mine0, mine1, theirs, w, m, v, name):
    L, r, C = w.shape
    rt = _pick(r, _rows_for(C), 16)

    def body(a0_ref, a1_ref, b_ref, w_ref, m_ref, v_ref, g_ref, d_ref, mo_ref, vo_ref):
        layer, c = pl.program_id(0), lax.axis_index("c")
        own = jnp.where(layer == 0, a0_ref[...], a1_ref[...])
        g_ref[...] = jnp.where(layer == 1 - c, own, b_ref[...])
        _adamw_math(g_ref, w_ref, m_ref, v_ref, d_ref, mo_ref, vo_ref)

    flat = pl.BlockSpec((rt, C), lambda l, i: (i, 0))
    lay = pl.BlockSpec((None, rt, C), lambda l, i: (l, i, 0))
    return pl.pallas_call(
        body, name=name, grid=(L, r // rt), in_specs=[flat, flat, flat, lay, lay, lay], out_specs=[lay] * 4,
        out_shape=[jax.ShapeDtypeStruct((L, r, C), f32)] * 4, compiler_params=_cp("parallel", "parallel"),
    )(mine0, mine1, theirs, w, m, v)


def _adamw_math(g_ref, w_ref, m_ref, v_ref, d_ref, mo_ref, vo_ref):
    gg = g_ref[...]
    m_new = ADAM_B1 * m_ref[...] + (1.0 - ADAM_B1) * gg
    v_new = ADAM_B2 * v_ref[...] + (1.0 - ADAM_B2) * (gg * gg)
    m_hat = m_new / (1.0 - ADAM_B1 ** ADAM_STEP)
    v_hat = v_new / (1.0 - ADAM_B2 ** ADAM_STEP)
    d_ref[...] = -ADAM_LR * (m_hat / (jnp.sqrt(v_hat) + ADAM_EPS) + ADAM_WD * w_ref[...])
    mo_ref[...] = m_new
    vo_ref[...] = v_new


FLAT_TILE = 2048


def _add2(a, b, name):
    R = a.shape[0]
    rt = _pick(R, FLAT_TILE, SUBLANES)

    def body(a_ref, b_ref, o_ref):
        o_ref[...] = a_ref[...] + b_ref[...]

    row = pl.BlockSpec((rt, LANES), lambda i: (i, 0))
    return pl.pallas_call(
        body, name=name, grid=(R // rt,), in_specs=[row, row], out_specs=row,
        out_shape=jax.ShapeDtypeStruct((R, LANES), f32), compiler_params=_cp("parallel"),
    )(a, b)


def _adamw(g, w, m, v, name):
    R = g.shape[0]
    rt = _pick(R, FLAT_TILE, SUBLANES)

    def body(g_ref, w_ref, m_ref, v_ref, d_ref, mo_ref, vo_ref):
        gg = g_ref[...]
        m_new = ADAM_B1 * m_ref[...] + (1.0 - ADAM_B1) * gg
        v_new = ADAM_B2 * v_ref[...] + (1.0 - ADAM_B2) * (gg * gg)
        m_hat = m_new / (1.0 - ADAM_B1 ** ADAM_STEP)
        v_hat = v_new / (1.0 - ADAM_B2 ** ADAM_STEP)
        d_ref[...] = -ADAM_LR * (m_hat / (jnp.sqrt(v_hat) + ADAM_EPS) + ADAM_WD * w_ref[...])
        mo_ref[...] = m_new
        vo_ref[...] = v_new

    row = pl.BlockSpec((rt, LANES), lambda i: (i, 0))
    return pl.pallas_call(
        body, name=name, grid=(R // rt,), in_specs=[row] * 4, out_specs=[row] * 3,
        out_shape=[jax.ShapeDtypeStruct((R, LANES), f32)] * 3, compiler_params=_cp("parallel"),
    )(g, w, m, v)


def _pack(arrs, dtype):
    flat = jnp.concatenate([a.astype(dtype).reshape(-1) for a in arrs])
    per = FLAT_TILE * LANES
    flat = jnp.pad(flat, (0, (-flat.shape[0]) % per))
    return flat.reshape(-1, LANES)


def _unpack(buf, shapes):
    flat = buf.reshape(-1)
    out, off = [], 0
    for s in shapes:
        n = math.prod(s)
        out.append(flat[off:off + n].reshape(s))
        off += n
    return out


def _block_diag(w):
    n, a, b = w.shape
    eye = jnp.eye(n, dtype=w.dtype)
    return (w[:, :, None, :] * eye[:, None, :, None]).reshape(n * a, n * b)


def _diag_blocks(m, n):
    a, b = m.shape[0] // n, m.shape[1] // n
    idx = jnp.arange(n)
    return m.reshape(n, a, n, b)[idx, :, idx, :]


BIG = ("w_in", "w_out", "w_up", "w_down", "s5_w_glu")
BIG_COL_SHARDED = {"w_in": True, "w_out": False, "w_up": True, "w_down": False, "s5_w_glu": False}
CONV_SHARDED = ("lru_conv_w", "ffn_conv_w")
SMALL = ("lru_conv_b", "lru_wr", "lru_br", "lru_wi", "lru_bi", "lru_lambda", "s5_a_re", "s5_a_im", "s5_b_re",
         "s5_b_im", "s5_c_re", "s5_c_im", "s5_d", "s5_log_step", "s5_b_glu", "mix_norm_g", "ln1_g", "ln1_b",
         "ffn_conv_b", "ln2_g", "ln2_b")
WEIGHTS = ("w_in", "lru_conv_w", "lru_conv_b", "lru_wr", "lru_br", "lru_wi", "lru_bi", "lru_lambda", "s5_a_re",
           "s5_a_im", "s5_b_re", "s5_b_im", "s5_c_re", "s5_c_im", "s5_d", "s5_log_step", "s5_w_glu", "s5_b_glu",
           "mix_norm_g", "w_out", "ln1_g", "ln1_b", "w_up", "ffn_conv_w", "ffn_conv_b", "w_down", "ln2_g", "ln2_b")


def _assemble(slabs, col_sharded):
    _, L, r, c = slabs.shape
    if col_sharded:
        return slabs.transpose(1, 2, 0, 3).reshape(L, r, 4 * c)
    return slabs.transpose(1, 0, 2, 3).reshape(L, 4 * r, c)


def _s5_prepare(p):
    G = N_S5_GROUPS
    bt_re, bt_im = p["s5_b_re"].transpose(0, 2, 1), p["s5_b_im"].transpose(0, 2, 1)
    ls = p["s5_log_step"].reshape(G, 1)
    ab_re, ab_im, bb_re, bb_im = _s5_params(p["s5_a_re"], p["s5_a_im"], ls, bt_re, bt_im)
    ab = jnp.concatenate([ab_re.reshape(1, S5_LANES), ab_im.reshape(1, S5_LANES)], axis=1)
    bbcat = jnp.concatenate([_block_diag(bb_re), _block_diag(bb_im)], axis=1).astype(bf16)
    ccat = jnp.concatenate([_block_diag(p["s5_c_re"].transpose(0, 2, 1)),
                            -_block_diag(p["s5_c_im"].transpose(0, 2, 1))], axis=0).astype(bf16)
    bbcat_pad = jnp.concatenate([bbcat, jnp.zeros((LRU_WIDTH - S5_WIDTH, 2 * S5_LANES), bf16)], axis=0)
    return dict(bt_re=bt_re, bt_im=bt_im, ls=ls, ab=ab, bbcat=bbcat, bbcat_pad=bbcat_pad, ccat=ccat)


def _layer_fwd(h, p, cos, sin, pending, install):
    sv = {"h": h}
    proj = _mm(h, p["w_in"], "nn", "mm_proj", tn=768)
    sv.update(proj=proj)
    qr, kr = _rope_fwd(proj, cos, sin)
    attn, ltot, gathered = _attn_fwd2(qr, kr, proj, [s for _, _, s in pending], [True] * len(pending))
    install(pending, gathered)
    sv.update(qr=qr, kr=kr, attn=attn, ltot=ltot)
    wr, wi = _block_diag(p["lru_wr"]).astype(bf16), _block_diag(p["lru_wi"]).astype(bf16)
    xc, r, i, log_a, u = _lru_pre(proj, p["lru_conv_w"], p["lru_conv_b"], wr, p["lru_br"], wi, p["lru_bi"],
                                  p["lru_lambda"])
    hl, lru = _lru_scan(log_a, u, proj)
    sv.update(wr=wr, wi=wi, xc=xc, r=r, i=i, log_a=log_a, hl=hl, lru=lru)
    s5 = _s5_prepare(p)
    bu = _mm(proj, s5["bbcat_pad"], "nn", "mm_s5_bu", a_win=(PROJ_S5_U, LRU_WIDTH))
    xs = _s5_scan(bu, s5["ab"])
    yc = _mm(xs, s5["ccat"], "nn", "mm_s5_y")
    ssm, y = _s5_out(yc, proj, p["s5_d"].reshape(-1), p["s5_w_glu"], p["s5_b_glu"])
    sv.update(s5=s5, xs=xs, y=y, ssm=ssm)
    mixed = _mixnorm(attn, lru, ssm, p["mix_norm_g"])
    mix = _mm(mixed, p["w_out"], "nn", "mm_out")
    h1, z1 = _ln_fwd(h, mix, p["ln1_g"], p["ln1_b"], "ln_fwd")
    sv.update(mixed=mixed, z1=z1, h1=h1)
    up = _mm(h1, p["w_up"], "nn", "mm_up", tn=1536)
    act = _ffn_act(up, p["ffn_conv_w"], p["ffn_conv_b"])
    ffn = _mm(act, p["w_down"], "nn", "mm_down")
    h2, z2 = _ln_fwd(h1, ffn, p["ln2_g"], p["ln2_b"], "ln_fwd")
    sv.update(up=up, act=act, z2=z2)
    return h2, sv


def _layer_bwd(dy_a, dy_b, p, sv, cos, sin, relay=None):
    gr = {}
    dz2, acc = _ln_bwd(dy_a, dy_b, sv["z2"], p["ln2_g"], "ln_bwd_top" if dy_a is None else "ln_bwd")
    gr["ln2_g"], gr["ln2_b"] = acc[0], acc[1]
    dact = _mm(dz2, p["w_down"], "nt", "mm_dact")
    gr["w_down"] = _mm(sv["act"], dz2, "tn", "mm_dw_down")
    dupc = _ffn_act_bwd(dact, sv["up"], p["ffn_conv_w"], p["ffn_conv_b"])
    send, relay_c, to_slabs = relay if relay else ((), 0, None)
    dup, acc, from_sibling = _conv_bwd(dupc, sv["up"], p["ffn_conv_w"], "ffn_conv_bwd", col_tile=FFN_COL_TILE,
                                       out_dtype=bf16, send=send, send_dst=relay_c)
    slabs = to_slabs(from_sibling) if relay else ()
    gr["ffn_conv_w"], gr["ffn_conv_b"] = acc[0:3], acc[3]
    dh1 = _mm(dup, p["w_up"], "nt", "mm_dh1", tk=2048)
    gr["w_up"] = _mm(sv["h1"], dup, "tn", "mm_dw_up", tn=1536)
    dz1, acc = _ln_bwd(dz2, dh1, sv["z1"], p["ln1_g"], "ln_bwd")
    gr["ln1_g"], gr["ln1_b"] = acc[0], acc[1]
    dmixed = _mm(dz1, p["w_out"], "nt", "mm_dmixed")
    gr["w_out"] = _mm(sv["mixed"], dz1, "tn", "mm_dw_out")
    dattn, dlru, dssm, delta, acc = _mixnorm_bwd(dmixed, sv["attn"], sv["lru"], sv["ssm"], p["mix_norm_g"])
    gr["mix_norm_g"] = acc[0]
    proj = sv["proj"]
    dqr, dkr, dv, received = _attn_bwd2(sv["qr"], sv["kr"], proj, dattn, sv["ltot"], delta, slabs, relay_c)
    dq, dk = _rope_bwd(dqr, dkr, cos, sin)
    g, dgate = _lru_scan_bwd(dlru, proj, sv["hl"], sv["log_a"])
    dxc, dwr, dwi, acc = _lru_gate_bwd(g, sv["hl"], sv["xc"], sv["r"], sv["i"], sv["log_a"], sv["wr"], sv["wi"],
                                       p["lru_lambda"])
    gr["lru_wr"], gr["lru_wi"] = _diag_blocks(dwr, N_LRU_HEADS), _diag_blocks(dwi, N_LRU_HEADS)
    gr["lru_br"], gr["lru_bi"], gr["lru_lambda"] = acc[0], acc[1], acc[2]
    dxr, acc, _ = _conv_bwd(dxc, proj, p["lru_conv_w"], "lru_conv_bwd", x_col_block=PROJ_LRU_X)
    gr["lru_conv_w"], gr["lru_conv_b"] = acc[0:4], acc[4]
    s5 = sv["s5"]
    G = N_S5_GROUPS
    dy, du_direct, dwglu, acc = _s5_out_bwd(dssm, sv["y"], proj, p["s5_d"].reshape(-1), p["s5_w_glu"],
                                            p["s5_b_glu"])
    gr["s5_w_glu"], gr["s5_b_glu"], gr["s5_d"] = dwglu, acc[0], acc[1].reshape(G, S5_GROUP)
    dxs = _mm(dy, s5["ccat"], "nt", "mm_s5_dx")
    dccat = _mm(sv["xs"], dy, "tn", "mm_s5_dc")
    gr["s5_c_re"] = _diag_blocks(dccat[:S5_LANES], G).transpose(0, 2, 1)
    gr["s5_c_im"] = -_diag_blocks(dccat[S5_LANES:], G).transpose(0, 2, 1)
    gs, dab = _s5_scan_bwd(dxs, sv["xs"], s5["ab"])
    du = _mm(gs, s5["bbcat"], "nt", "mm_s5_du", add=du_direct)
    dbbcat = _mm(proj, gs, "tn", "mm_s5_dbb", a_win=(PROJ_S5_U, LRU_WIDTH))[:S5_WIDTH]
    d_ar, d_ai, d_ls, d_btr, d_bti = _s5_params_bwd(
        p["s5_a_re"], p["s5_a_im"], s5["ls"], s5["bt_re"], s5["bt_im"],
        dab[:, :S5_LANES].reshape(G, S5_STATE), dab[:, S5_LANES:].reshape(G, S5_STATE),
        _diag_blocks(dbbcat[:, :S5_LANES], G), _diag_blocks(dbbcat[:, S5_LANES:], G))
    gr["s5_a_re"], gr["s5_a_im"], gr["s5_log_step"] = d_ar, d_ai, d_ls.reshape(G)
    gr["s5_b_re"], gr["s5_b_im"] = d_btr.transpose(0, 2, 1), d_bti.transpose(0, 2, 1)
    pad = jnp.zeros((du.shape[0], D_IN_PAD - D_IN), f32)
    dproj = jnp.concatenate([dq, dk, dv, dxr, dgate, du, pad], axis=1).astype(bf16)
    gr["w_in"] = _mm(sv["h"], dproj, "tn", "mm_dw_in", tn=768)[:, :D_IN]
    dh = _mm(dproj, p["w_in"], "nt", "mm_dh")
    return (dz1, dh, gr, slabs, received) if relay else (dz1, dh, gr)


def _train_step(d):
    x, target = d["x"][0], d["loss_target"][0]
    S = x.shape[0]
    me = 2 * lax.axis_index("x") + lax.axis_index("y")

    def rows2d(a):
        return a.reshape(a.shape[0] * a.shape[1], a.shape[2])

    params = [{n: d[n][l] for n in SMALL} for l in range(DEPTH)]

    def install(items, gathered):
        for (n, l, mine), g in zip(items, gathered):
            g = lax.dynamic_update_slice_in_dim(g, mine[None], me, axis=0)
            if n in CONV_SHARDED:
                full = _assemble(g.reshape((4,) + d[n].shape), True)
                for k in range(DEPTH):
                    params[k][n] = full[k]
                continue
            full = _assemble(g[:, None], BIG_COL_SHARDED[n])[0]
            if n == "w_in":
                full = jnp.pad(full, ((0, 0), (0, D_IN_PAD - D_IN)))
            params[l][n] = full

    def shard(n, l):
        return (n, l, d[n][l].astype(bf16))

    first = [shard("w_in", 0)] + [(n, None, rows2d(d[n])) for n in CONV_SHARDED]
    install(first, _gather_weights([s for _, _, s in first], [True] + [False] * len(CONV_SHARDED)))
    later = [[shard(n, 0) for n in BIG[1:]] + [shard("w_in", 1)], [shard(n, 1) for n in BIG[1:]]]

    cos, sin = _rope_tables(S)
    h, saved = x, []
    for l in range(DEPTH):
        h, sv = _layer_fwd(h, params[l], cos, sin, later[l], install)
        saved.append(sv)
    dy, loss_acc = _loss_head(h, target)
    loss = lax.psum(loss_acc[0, 0], ("x", "y", "c"))
    def chip_slabs(layer, others):
        chip = [_pair_sum(grads[layer][n], o, "pair_sum_" + n, 1 - layer) for n, o in zip(BIG, others)]
        return [p.reshape(p.shape[0], 4, p.shape[1] // 4).transpose(1, 0, 2) if BIG_COL_SHARDED[n]
                else p.reshape(4, p.shape[0] // 4, p.shape[1]) for n, p in zip(BIG, chip)]

    da, db, grads = None, dy, [None] * DEPTH
    da, db, grads[1] = _layer_bwd(da, db, params[1], saved[1], cos, sin)
    relay = ([grads[1][n] for n in BIG], 0, functools.partial(chip_slabs, 1))
    da, db, grads[0], slabs1, recv1 = _layer_bwd(da, db, params[0], saved[0], cos, sin, relay)
    out = {"loss": loss, "grad_x": _axpy(da, db, "grad_x")[None]}
    G = {n: jnp.stack([grads[l][n] for l in range(DEPTH)]) for n in SMALL + CONV_SHARDED}

    small = SMALL + CONV_SHARDED
    sp = _pack([G[n] for n in small], f32)
    others0, (sp_sibling,) = _sibling_send([grads[0][n] for n in BIG], 1, [sp])
    slabs0 = chip_slabs(0, others0)
    chip_small = _add2(sp, sp_sibling, "pair_sum_small")
    recv0, (recv_small,) = _chip_exchange(slabs0, [chip_small], 1)
    mine0 = [_sum_chips(r, s, "sum_chips_" + n, 1) for n, r, s in zip(BIG, recv0, slabs0)]
    mine1 = [_sum_chips(r, s, "sum_chips_" + n, 0) for n, r, s in zip(BIG, recv1, slabs1)]
    theirs = _pair_swap(mine0, mine1)
    for n, a0, a1, b in zip(BIG, mine0, mine1, theirs):
        upd = _adamw_layers(a0, a1, b, d[n], d["m_" + n], d["v_" + n], "adamw_" + n)
        for pre, u in zip(("grad_", "delta_", "new_m_", "new_v_"), upd):
            out[pre + n] = u

    total = _sum_chips(recv_small, chip_small, "sum_chips_small")
    gs = dict(zip(small, _unpack(total, [G[n].shape for n in small])))
    for n in CONV_SHARDED:
        L, K, C = gs[n].shape
        gs[n] = lax.dynamic_index_in_dim(gs[n].reshape(L, K, 4, C // 4), me, axis=2, keepdims=False)
    small_shapes = [d[n].shape for n in small]
    gsmall = _pack([gs[n] for n in small], f32)
    upd = _adamw(gsmall, _pack([d[n] for n in small], f32), _pack([d["m_" + n] for n in small], f32),
                 _pack([d["v_" + n] for n in small], f32), "adamw_small")
    for pre, buf in zip(("grad_", "delta_", "new_m_", "new_v_"), (gsmall,) + tuple(upd)):
        for n, a in zip(small, _unpack(buf, small_shapes)):
            out[pre + n] = a

    return (out["loss"], out["grad_x"]) + tuple(out[pre + n] for pre in ("grad_", "delta_", "new_m_", "new_v_")
                                                for n in WEIGHTS)


def kernel(
        x, w_in, lru_conv_w, lru_conv_b, lru_wr, lru_br, lru_wi, lru_bi, lru_lambda, s5_a_re, s5_a_im, s5_b_re,
        s5_b_im, s5_c_re, s5_c_im, s5_d, s5_log_step, s5_w_glu, s5_b_glu, mix_norm_g, w_out, ln1_g, ln1_b, w_up,
        ffn_conv_w, ffn_conv_b, w_down, ln2_g, ln2_b, loss_target, m_w_in, m_lru_conv_w, m_lru_conv_b, m_lru_wr,
        m_lru_br, m_lru_wi, m_lru_bi, m_lru_lambda, m_s5_a_re, m_s5_a_im, m_s5_b_re, m_s5_b_im, m_s5_c_re,
        m_s5_c_im, m_s5_d, m_s5_log_step, m_s5_w_glu, m_s5_b_glu, m_mix_norm_g, m_w_out, m_ln1_g, m_ln1_b,
        m_w_up, m_ffn_conv_w, m_ffn_conv_b, m_w_down, m_ln2_g, m_ln2_b, v_w_in, v_lru_conv_w, v_lru_conv_b,
        v_lru_wr, v_lru_br, v_lru_wi, v_lru_bi, v_lru_lambda, v_s5_a_re, v_s5_a_im, v_s5_b_re, v_s5_b_im,
        v_s5_c_re, v_s5_c_im, v_s5_d, v_s5_log_step, v_s5_w_glu, v_s5_b_glu, v_mix_norm_g, v_w_out, v_ln1_g,
        v_ln1_b, v_w_up, v_ffn_conv_w, v_ffn_conv_b, v_w_down, v_ln2_g, v_ln2_b
):
    return _train_step(dict(locals()))
```

```python
import functools
import math

import jax
import jax.numpy as jnp
from jax import lax
from jax.experimental import pallas as pl
from jax.experimental.pallas import tpu as pltpu

f32 = jnp.float32
bf16 = jnp.bfloat16
MESH = pl.DeviceIdType.MESH

D_MODEL = 1024
ATTN_WIDTH = 384
LRU_WIDTH = 384
S5_WIDTH = 256
HEAD_DIM = 64
N_LRU_HEADS = 6
N_S5_GROUPS = 16
S5_GROUP = 16
S5_STATE = 64
S5_LANES = N_S5_GROUPS * S5_STATE
D_FF = 3072
D_IN = 2176
LRU_C = 8.0
ROPE_THETA = 10000.0
DILATIONS = (1, 4, 16)
ATTN_BLOCK = 128
DEPTH = 2
ALPHA = (2 * DEPTH) ** 0.25
LN_EPS = 1e-5
RMS_EPS = 1e-6
ADAM_LR, ADAM_B1, ADAM_B2, ADAM_EPS, ADAM_WD, ADAM_STEP = 0.001, 0.9, 0.999, 1e-08, 0.01, 10

SUBLANES = 8
LANES = 128
VMEM_LIMIT = 56 * 1024 * 1024
ROW_TILE = 512
MM_SINGLE_K = 3072
D_IN_PAD = 2304
NEG = -1e30


def _cp(*sem):
    return pltpu.CompilerParams(dimension_semantics=sem if sem else None, vmem_limit_bytes=VMEM_LIMIT)


def _pick(dim, pref, align=LANES):
    if dim <= pref:
        return dim
    t = (pref // align) * align
    while t >= align:
        if dim % t == 0:
            return t
        t -= align
    return dim


def _gelu(x):
    return jax.nn.gelu(x)


def _gelu_grad(x):
    c = math.sqrt(2.0 / math.pi)
    t = jnp.tanh(c * (x + 0.044715 * x * x * x))
    return 0.5 * (1.0 + t) + 0.5 * x * (1.0 - t * t) * c * (1.0 + 3 * 0.044715 * x * x)


def _gelu_pair(x):
    c = math.sqrt(2.0 / math.pi)
    x2 = x * x
    t = jnp.tanh(c * x * (1.0 + 0.044715 * x2))
    return 0.5 * x * (1.0 + t), 0.5 * (1.0 + t) + 0.5 * x * (1.0 - t * t) * c * (1.0 + 3 * 0.044715 * x2)


def _sigmoid(x):
    return jax.nn.sigmoid(x)


def _expm1(x):
    p = 1.0 + x / 9.0
    for n in (8.0, 7.0, 6.0, 5.0, 4.0, 3.0, 2.0):
        p = 1.0 + (x / n) * p
    return jnp.where(jnp.abs(x) < 0.3, x * p, jnp.exp(x) - 1.0)


def _dot(a, b, dims):
    return lax.dot_general(a, b, (dims, ((), ())), preferred_element_type=f32)


NN = ((1,), (0,))
NT = ((1,), (1,))
TN = ((0,), (0,))


def _mm(a, b, mode, name, out_dtype=f32, tm=1024, tn=1024, tk=1024, add=None, a_win=None):
    if mode == "nn":
        (M, K), N = a.shape, b.shape[1]
    elif mode == "nt":
        (M, K), N = a.shape, b.shape[0]
    else:
        (K, M), N = a.shape, b.shape[1]
    win = 0
    if a_win is not None:
        win, w = a_win
        if mode == "tn":
            M, tm = w, w
        else:
            K = w
    single = mode != "tn" and K <= MM_SINGLE_K
    tm, tn = _pick(M, tm), _pick(N, tn)
    tk = K if single else _pick(K, tk)
    nk = K // tk
    dims = {"nn": NN, "nt": NT, "tn": TN}[mode]

    def body(a_ref, b_ref, *rest):
        prod = _dot(a_ref[...].astype(bf16), b_ref[...].astype(bf16), dims)
        if single:
            o_ref = rest[-1]
            o_ref[...] = (prod if add is None else prod + rest[0][...]).astype(o_ref.dtype)
            return
        o_ref, acc = rest[-2:]
        k = pl.program_id(2)

        @pl.when(k == 0)
        def _():
            acc[...] = prod if add is None else prod + rest[0][...]

        @pl.when(k > 0)
        def _():
            acc[...] += prod

        @pl.when(k == nk - 1)
        def _():
            o_ref[...] = acc[...].astype(o_ref.dtype)

    if mode == "tn":
        a_spec = pl.BlockSpec((tk, tm), lambda i, j, k: (k, i + win))
    else:
        a_spec = pl.BlockSpec((tm, tk), lambda i, j, k: (i, k + win))
    if mode == "nt":
        b_spec = pl.BlockSpec((tn, tk), lambda i, j, k: (j, k))
    else:
        b_spec = pl.BlockSpec((tk, tn), lambda i, j, k: (k, j))
    o_spec = pl.BlockSpec((tm, tn), lambda i, j, k: (i, j))
    return pl.pallas_call(
        body, name=name, grid=(M // tm, N // tn, nk),
        in_specs=[a_spec, b_spec] + ([] if add is None else [o_spec]), out_specs=o_spec,
        out_shape=jax.ShapeDtypeStruct((M, N), out_dtype),
        scratch_shapes=[] if single else [pltpu.VMEM((tm, tn), f32)],
        compiler_params=_cp("parallel", "parallel", "arbitrary"),
    )(*((a, b) if add is None else (a, b, add)))


def _shift_down(cur, prev8, k):
    if k == 0:
        return cur
    T, (R, C) = SUBLANES, cur.shape
    rot = pltpu.roll(cur.reshape(R // T, T, C), k, 1)
    before = jnp.concatenate([pltpu.roll(prev8, k, 0)[None], rot[:-1]], axis=0)
    row = lax.broadcasted_iota(jnp.int32, (R // T, T, C), 1)
    return jnp.where(row < k, before, rot).reshape(R, C)


def _shift_up(cur, next8, k):
    if k == 0:
        return cur
    T, (R, C) = SUBLANES, cur.shape
    rot = pltpu.roll(cur.reshape(R // T, T, C), T - k, 1)
    after = jnp.concatenate([rot[1:], pltpu.roll(next8, T - k, 0)[None]], axis=0)
    row = lax.broadcasted_iota(jnp.int32, (R // T, T, C), 1)
    return jnp.where(row < T - k, rot, after).reshape(R, C)


def _prev_halo_spec(rt, cols, ncolblk_fn):
    per = rt // SUBLANES
    return pl.BlockSpec((SUBLANES, cols), lambda *g: (jnp.maximum(g[-1] * per - 1, 0), ncolblk_fn(*g)))


def _ln_fwd(h, branch, g, b, name):
    S, D = h.shape
    rt = _pick(S, ROW_TILE, SUBLANES)

    def body(h_ref, m_ref, g_ref, b_ref, o_ref, z_ref):
        z = ALPHA * h_ref[...] + m_ref[...]
        mu = jnp.mean(z, axis=-1, keepdims=True)
        zc = z - mu
        var = jnp.mean(zc * zc, axis=-1, keepdims=True)
        o_ref[...] = zc * lax.rsqrt(var + LN_EPS) * g_ref[...] + b_ref[...]
        z_ref[...] = z

    row = pl.BlockSpec((rt, D), lambda i: (i, 0))
    vec = pl.BlockSpec((1, D), lambda i: (0, 0))
    return pl.pallas_call(
        body, name=name, grid=(S // rt,), in_specs=[row, row, vec, vec], out_specs=[row, row],
        out_shape=[jax.ShapeDtypeStruct((S, D), f32)] * 2, compiler_params=_cp("parallel"),
    )(h, branch, g.reshape(1, D), b.reshape(1, D))


def _ln_bwd(dy_a, dy_b, z, g, name):
    S, D = z.shape
    rt = _pick(S, ROW_TILE, SUBLANES)
    two = dy_a is not None

    def body(*refs):
        if two:
            a_ref, b_ref, z_ref, g_ref, dz_ref, acc_ref = refs
            dy = ALPHA * a_ref[...] + b_ref[...]
        else:
            b_ref, z_ref, g_ref, dz_ref, acc_ref = refs
            dy = b_ref[...]
        z = z_ref[...]
        mu = jnp.mean(z, axis=-1, keepdims=True)
        zc = z - mu
        var = jnp.mean(zc * zc, axis=-1, keepdims=True)
        rstd = lax.rsqrt(var + LN_EPS)
        xhat = zc * rstd
        dxh = dy * g_ref[...]
        m1 = jnp.mean(dxh, axis=-1, keepdims=True)
        m2 = jnp.mean(dxh * xhat, axis=-1, keepdims=True)
        dz_ref[...] = rstd * (dxh - m1 - xhat * m2)

        @pl.when(pl.program_id(0) == 0)
        def _():
            acc_ref[...] = jnp.zeros_like(acc_ref)

        acc_ref[0:1, :] += jnp.sum(dy * xhat, axis=0, keepdims=True)
        acc_ref[1:2, :] += jnp.sum(dy, axis=0, keepdims=True)

    row = pl.BlockSpec((rt, D), lambda i: (i, 0))
    vec = pl.BlockSpec((1, D), lambda i: (0, 0))
    acc = pl.BlockSpec((SUBLANES, D), lambda i: (0, 0))
    ins = ([dy_a] if two else []) + [dy_b, z, g.reshape(1, D)]
    return pl.pallas_call(
        body, name=name, grid=(S // rt,), in_specs=[row] * (len(ins) - 1) + [vec], out_specs=[row, acc],
        out_shape=[jax.ShapeDtypeStruct((S, D), f32), jax.ShapeDtypeStruct((SUBLANES, D), f32)],
        compiler_params=_cp("arbitrary"),
    )(*ins)


def _loss_head(y, target):
    S, D = y.shape
    rt = _pick(S, ROW_TILE, SUBLANES)

    def body(y_ref, t_ref, dy_ref, acc_ref):
        e = y_ref[...] - t_ref[...]
        dy_ref[...] = e * (1.0 / D)

        @pl.when(pl.program_id(0) == 0)
        def _():
            acc_ref[...] = jnp.zeros_like(acc_ref)

        part = jnp.sum(jnp.mean(e * e, axis=-1, keepdims=True), axis=0, keepdims=True)
        acc_ref[...] += 0.5 * part

    row = pl.BlockSpec((rt, D), lambda i: (i, 0))
    return pl.pallas_call(
        body, name="loss_head", grid=(S // rt,), in_specs=[row, row],
        out_specs=[row, pl.BlockSpec((1, 1), lambda i: (0, 0))],
        out_shape=[jax.ShapeDtypeStruct((S, D), f32), jax.ShapeDtypeStruct((1, 1), f32)],
        compiler_params=_cp("arbitrary"),
    )(y, target)


def _axpy(a, b, name):
    S, D = a.shape
    rt = _pick(S, ROW_TILE, SUBLANES)

    def body(a_ref, b_ref, o_ref):
        o_ref[...] = ALPHA * a_ref[...] + b_ref[...]

    row = pl.BlockSpec((rt, D), lambda i: (i, 0))
    return pl.pallas_call(
        body, name=name, grid=(S // rt,), in_specs=[row, row], out_specs=row,
        out_shape=jax.ShapeDtypeStruct((S, D), f32), compiler_params=_cp("parallel"),
    )(a, b)


def _rope_tables(S):
    rt = _pick(S, ROW_TILE, SUBLANES)

    def body(c_ref, s_ref):
        pos = (pl.program_id(0) * rt + lax.broadcasted_iota(jnp.int32, (rt, LANES), 0)).astype(f32)
        lane = lax.broadcasted_iota(jnp.int32, (rt, LANES), 1)
        j = (lane % (HEAD_DIM // 2)).astype(f32)
        inv = jnp.exp((-j * 2.0 / HEAD_DIM) * math.log(ROPE_THETA))
        ang = pos * inv
        c = jnp.cos(ang)
        s = jnp.where(lane % HEAD_DIM < HEAD_DIM // 2, -jnp.sin(ang), jnp.sin(ang))
        c_ref[...] = jnp.concatenate([c, c, c], axis=1)
        s_ref[...] = jnp.concatenate([s, s, s], axis=1)

    row = pl.BlockSpec((rt, ATTN_WIDTH), lambda i: (i, 0))
    return pl.pallas_call(
        body, name="rope_tables", grid=(S // rt,), in_specs=[], out_specs=[row, row],
        out_shape=[jax.ShapeDtypeStruct((S, ATTN_WIDTH), f32)] * 2, compiler_params=_cp("parallel"),
    )()


def _swap_halves(x):
    lane = lax.broadcasted_iota(jnp.int32, x.shape, 1)
    half = HEAD_DIM // 2
    return jnp.where(lane % HEAD_DIM < half, pltpu.roll(x, x.shape[1] - half, 1), pltpu.roll(x, half, 1))


def _rope_fwd(proj, cos, sin):
    S, W = proj.shape[0], ATTN_WIDTH
    rt = _pick(S, ROW_TILE, SUBLANES)

    def body(q_ref, k_ref, c_ref, s_ref, qo_ref, ko_ref):
        c, s = c_ref[...], s_ref[...]
        qo_ref[...] = q_ref[...] * c + _swap_halves(q_ref[...]) * s
        ko_ref[...] = k_ref[...] * c + _swap_halves(k_ref[...]) * s

    row = pl.BlockSpec((rt, W), lambda i: (i, 0))
    return pl.pallas_call(
        body, name="rope_fwd", grid=(S // rt,), in_specs=[row, pl.BlockSpec((rt, W), lambda i: (i, 1)), row, row],
        out_specs=[row, row], out_shape=[jax.ShapeDtypeStruct((S, W), f32)] * 2, compiler_params=_cp("parallel"),
    )(proj, proj, cos, sin)


def _rope_bwd(dq, dk, cos, sin):
    S, W = dq.shape
    rt = _pick(S, ROW_TILE, SUBLANES)

    def body(q_ref, k_ref, c_ref, s_ref, qo_ref, ko_ref):
        c, s = c_ref[...], s_ref[...]
        qo_ref[...] = q_ref[...] * c + _swap_halves(q_ref[...] * s)
        ko_ref[...] = k_ref[...] * c + _swap_halves(k_ref[...] * s)

    row = pl.BlockSpec((rt, W), lambda i: (i, 0))
    return pl.pallas_call(
        body, name="rope_bwd", grid=(S // rt,), in_specs=[row] * 4, out_specs=[row] * 2,
        out_shape=[jax.ShapeDtypeStruct((S, W), f32)] * 2, compiler_params=_cp("parallel"),
    )(dq, dk, cos, sin)


def _rows(ref, start, d):
    if d == 1:
        return ref[pl.ds(pl.multiple_of(start, ATTN_BLOCK), ATTN_BLOCK), :]
    return ref[pl.ds(start, ATTN_BLOCK, stride=d), :]


def _set_rows(ref, start, d, val):
    if d == 1:
        ref[pl.ds(pl.multiple_of(start, ATTN_BLOCK), ATTN_BLOCK), :] = val
    else:
        ref[pl.ds(start, ATTN_BLOCK, stride=d), :] = val


def _pair_spec(S, first_block):
    return pl.BlockSpec((S, LANES), lambda p: (0, p + first_block))


def _attn_fwd2(qr, kr, proj, shards=(), split=()):
    S = qr.shape[0]
    B = ATTN_BLOCK
    nb = S // B
    scale = HEAD_DIM ** -0.5

    gather = _Gather(shards, split)
    nt = gather.nt

    def body(*refs):
        q_ref, k_ref, v_ref = refs[:3]
        g_ins = refs[3:3 + nt]
        o_ref, l_ref = refs[3 + nt:5 + nt]
        g_outs = refs[5 + nt:5 + 2 * nt]
        m_s, l_s = refs[5 + 2 * nt:7 + 2 * nt]
        g_sems = refs[7 + 2 * nt:]
        if nt:
            @pl.when(pl.program_id(0) == 0)
            def _():
                gather.start(g_ins, g_outs, g_sems)

        qi = lax.broadcasted_iota(jnp.int32, (B, 2 * B), 0)
        ki = lax.broadcasted_iota(jnp.int32, (B, 2 * B), 1)
        dist = qi + B - ki
        band = (dist >= 0) & (dist <= B)
        for bi, d in enumerate(DILATIONS):
            bpc = nb // d

            def blk(b, carry, bi=bi, d=d, bpc=bpc):
                c, n = b // bpc, b % bpc
                start = c + d * B * n
                pstart = c + d * B * jnp.maximum(n - 1, 0)
                valid = band & ((ki >= B) | (n > 0))
                q = _rows(q_ref, start, d).astype(bf16)
                kcat = jnp.concatenate([_rows(k_ref, pstart, d), _rows(k_ref, start, d)], axis=0).astype(bf16)
                vcat = jnp.concatenate([_rows(v_ref, pstart, d), _rows(v_ref, start, d)], axis=0).astype(bf16)
                if bi > 0:
                    m_old, l_old, a_old = _rows(m_s, start, d), _rows(l_s, start, d), _rows(o_ref, start, d)
                ms, ls, accs = [], [], []
                for h in range(2):
                    sl = slice(h * HEAD_DIM, (h + 1) * HEAD_DIM)
                    c0 = h * HEAD_DIM
                    s = jnp.where(valid, _dot(q[:, sl], kcat[:, sl], NT) * scale, NEG)
                    m = jnp.max(s, axis=1, keepdims=True)
                    if bi > 0:
                        mo = m_old[:, c0:c0 + 1]
                        m = jnp.maximum(m, mo)
                        alpha = jnp.exp(mo - m)
                    p = jnp.exp(s - m)
                    l = jnp.sum(p, axis=1, keepdims=True)
                    acc = _dot(p.astype(bf16), vcat[:, sl], NN)
                    if bi > 0:
                        l = l + alpha * l_old[:, c0:c0 + 1]
                        acc = acc + alpha * a_old[:, sl]
                    ms.append(jnp.broadcast_to(m, (B, HEAD_DIM)))
                    ls.append(jnp.broadcast_to(l, (B, HEAD_DIM)))
                    accs.append(acc)
                _set_rows(m_s, start, d, jnp.concatenate(ms, axis=1))
                _set_rows(l_s, start, d, jnp.concatenate(ls, axis=1))
                _set_rows(o_ref, start, d, jnp.concatenate(accs, axis=1))
                return carry

            lax.fori_loop(0, nb, blk, 0, unroll=4)

        def fin(t, carry):
            rows = pl.ds(pl.multiple_of(t * B, B), B)
            l = l_s[rows, :]
            o_ref[rows, :] = o_ref[rows, :] / l
            l_ref[rows, :] = m_s[rows, :] + jnp.log(l)
            return carry

        lax.fori_loop(0, nb, fin, 0)
        if nt:
            @pl.when(pl.program_id(0) == pl.num_programs(0) - 1)
            def _():
                gather.finish(g_ins, g_outs, g_sems)

    pair = _pair_spec(S, 0)
    res = pl.pallas_call(
        body, name="attn_fwd_gather" if nt else "attn_fwd", grid=(3,),
        in_specs=[pair, pair, _pair_spec(S, 2 * ATTN_WIDTH // LANES)] + gather.in_specs,
        out_specs=[pair, pair] + gather.out_specs,
        out_shape=[jax.ShapeDtypeStruct((S, ATTN_WIDTH), f32)] * 2 + gather.out_shape,
        scratch_shapes=[pltpu.VMEM((S, LANES), f32)] * 2 + gather.scratch,
        compiler_params=_cp("arbitrary"),
    )(qr, kr, proj, *shards)
    return res[0], res[1], list(res[2:])


def _attn_bwd2(qr, kr, proj, dattn, ltot, delta, slabs=(), only_c=0):
    S = qr.shape[0]
    B = ATTN_BLOCK
    nb = S // B
    scale = HEAD_DIM ** -0.5
    ex = _ChipExchange(slabs, (), only_c)
    n = ex.n

    def body(*refs):
        q_ref, k_ref, v_ref, do_ref, l_ref, d_ref = refs[:6]
        x_ins = refs[6:6 + n]
        dq_ref, dk_ref, dv_ref = refs[6 + n:9 + n]
        x_outs = refs[9 + n:9 + 2 * n]
        x_sems = refs[9 + 2 * n:]
        if n:
            @pl.when(pl.program_id(0) == 0)
            def _():
                ex.start(x_ins, x_outs, x_sems)

        qi = lax.broadcasted_iota(jnp.int32, (B, 2 * B), 0)
        ki = lax.broadcasted_iota(jnp.int32, (B, 2 * B), 1)
        dist1 = qi + B - ki
        band1 = (dist1 >= 0) & (dist1 <= B)
        ri = lax.broadcasted_iota(jnp.int32, (2 * B, B), 0)
        ci = lax.broadcasted_iota(jnp.int32, (2 * B, B), 1)
        dist2 = ri - ci
        band2 = (dist2 >= 0) & (dist2 <= B)
        for bi, d in enumerate(DILATIONS):
            bpc = nb // d

            def blk(b, carry, bi=bi, d=d, bpc=bpc):
                c, n = b // bpc, b % bpc
                start = c + d * B * n
                pstart = c + d * B * jnp.maximum(n - 1, 0)
                nstart = c + d * B * jnp.minimum(n + 1, bpc - 1)
                valid1 = band1 & ((ki >= B) | (n > 0))
                valid2 = band2 & ((ri < B) | (n + 1 < bpc))
                q_c, q_n = _rows(q_ref, start, d), _rows(q_ref, nstart, d)
                k_p, k_c = _rows(k_ref, pstart, d), _rows(k_ref, start, d)
                v_p, v_c = _rows(v_ref, pstart, d), _rows(v_ref, start, d)
                do_c, do_n = _rows(do_ref, start, d), _rows(do_ref, nstart, d)
                l_c, l_n = _rows(l_ref, start, d), _rows(l_ref, nstart, d)
                d_c, d_n = _rows(d_ref, start, d), _rows(d_ref, nstart, d)
                qc = q_c.astype(bf16)
                qcat = jnp.concatenate([q_c, q_n], axis=0).astype(bf16)
                kc = k_c.astype(bf16)
                kcat = jnp.concatenate([k_p, k_c], axis=0).astype(bf16)
                vc = v_c.astype(bf16)
                vcat = jnp.concatenate([v_p, v_c], axis=0).astype(bf16)
                doc = do_c.astype(bf16)
                docat = jnp.concatenate([do_c, do_n], axis=0).astype(bf16)
                lcat = jnp.concatenate([l_c, l_n], axis=0)
                dcat = jnp.concatenate([d_c, d_n], axis=0)
                dqs, dks, dvs = [], [], []
                for h in range(2):
                    sl = slice(h * HEAD_DIM, (h + 1) * HEAD_DIM)
                    c0 = h * HEAD_DIM
                    s1 = _dot(qc[:, sl], kcat[:, sl], NT) * scale
                    p1 = jnp.where(valid1, jnp.exp(s1 - l_c[:, c0:c0 + 1]), 0.0)
                    dp1 = _dot(doc[:, sl], vcat[:, sl], NT)
                    ds1 = p1 * (dp1 - d_c[:, c0:c0 + 1]) * scale
                    dqs.append(_dot(ds1.astype(bf16), kcat[:, sl], NN))
                    s2 = _dot(qcat[:, sl], kc[:, sl], NT) * scale
                    p2 = jnp.where(valid2, jnp.exp(s2 - lcat[:, c0:c0 + 1]), 0.0)
                    dvs.append(_dot(p2.astype(bf16), docat[:, sl], TN))
                    dp2 = _dot(docat[:, sl], vc[:, sl], NT)
                    ds2 = p2 * (dp2 - dcat[:, c0:c0 + 1]) * scale
                    dks.append(_dot(ds2.astype(bf16), qcat[:, sl], TN))
                for ref, parts in ((dq_ref, dqs), (dk_ref, dks), (dv_ref, dvs)):
                    new = jnp.concatenate(parts, axis=1)
                    if bi > 0:
                        new = new + _rows(ref, start, d)
                    _set_rows(ref, start, d, new)
                return carry

            lax.fori_loop(0, nb, blk, 0, unroll=4)

        if n:
            @pl.when(pl.program_id(0) == pl.num_programs(0) - 1)
            def _():
                ex.finish(x_ins, x_outs, x_sems)

    pair = _pair_spec(S, 0)
    res = pl.pallas_call(
        body, name="attn_bwd_exchange" if n else "attn_bwd", grid=(3,),
        in_specs=[pair, pair, _pair_spec(S, 2 * ATTN_WIDTH // LANES), pair, pair, pair] + ex.in_specs,
        out_specs=[pair] * 3 + ex.out_specs,
        out_shape=[jax.ShapeDtypeStruct((S, ATTN_WIDTH), f32)] * 3 + ex.out_shape,
        scratch_shapes=ex.scratch, compiler_params=_cp("arbitrary"),
    )(qr, kr, proj, dattn, ltot, delta, *slabs)
    return res[0], res[1], res[2], list(res[3:])


def _softplus_neg(lam):
    return jnp.maximum(-lam, 0.0) + jnp.log1p(jnp.exp(-jnp.abs(lam)))


PROJ_LRU_X, PROJ_LRU_GATE, PROJ_S5_U = 3, 4, 5


def _lru_pre(proj, conv_w, conv_b, wr, br, wi, bi, lam):
    S, W = proj.shape[0], LRU_WIDTH
    rt = _pick(S, ROW_TILE, SUBLANES)
    K = conv_w.shape[0]

    def body(x_ref, xp_ref, cw_ref, cb_ref, wr_ref, br_ref, wi_ref, bi_ref, lam_ref,
             xc_ref, r_ref, i_ref, la_ref, u_ref):
        prev = jnp.where(pl.program_id(0) == 0, 0.0, xp_ref[...])
        x = x_ref[...]
        xc = cb_ref[...] + cw_ref[K - 1:K, :] * x
        for k in range(K - 1):
            xc = xc + cw_ref[k:k + 1, :] * _shift_down(x, prev, K - 1 - k)
        xb = xc.astype(bf16)
        r = _sigmoid(_dot(xb, wr_ref[...], NN) + br_ref[...])
        i = _sigmoid(_dot(xb, wi_ref[...], NN) + bi_ref[...])
        log_a = -LRU_C * r * _softplus_neg(lam_ref[...])
        u = jnp.sqrt(-_expm1(2.0 * log_a)) * (i * xc)
        xc_ref[...], r_ref[...], i_ref[...], la_ref[...], u_ref[...] = xc, r, i, log_a, u

    row = pl.BlockSpec((rt, W), lambda i: (i, 0))
    xrow = pl.BlockSpec((rt, W), lambda i: (i, PROJ_LRU_X))
    halo = _prev_halo_spec(rt, W, lambda i: PROJ_LRU_X)
    vec = pl.BlockSpec((1, W), lambda i: (0, 0))
    return pl.pallas_call(
        body, name="lru_pre", grid=(S // rt,),
        in_specs=[xrow, halo, pl.BlockSpec((K, W), lambda i: (0, 0)), vec,
                  pl.BlockSpec((W, W), lambda i: (0, 0)), vec, pl.BlockSpec((W, W), lambda i: (0, 0)), vec, vec],
        out_specs=[row] * 5, out_shape=[jax.ShapeDtypeStruct((S, W), f32)] * 5, compiler_params=_cp("parallel"),
    )(proj, proj, conv_w, conv_b.reshape(1, W), wr, br.reshape(1, W), wi, bi.reshape(1, W), lam.reshape(1, W))


def _tile_rows(shape):
    return lax.broadcasted_iota(jnp.int32, shape, 0)


def _lru_scan(log_a, u, proj):
    S, W = u.shape
    rt = _pick(S, ROW_TILE, SUBLANES)
    T = SUBLANES

    def body(la_ref, u_ref, g_ref, h_ref, o_ref, carry):
        @pl.when(pl.program_id(0) == 0)
        def _():
            carry[...] = jnp.zeros_like(carry)

        row = _tile_rows((T, W))

        def step(t, hp):
            r0 = pl.multiple_of(t * T, T)
            a = jnp.exp(la_ref[pl.ds(r0, T), :])
            x = u_ref[pl.ds(r0, T), :]
            for k in (1, 2, 4):
                x = x + a * jnp.where(row >= k, pltpu.roll(x, k, 0), 0.0)
                a = a * jnp.where(row >= k, pltpu.roll(a, k, 0), 1.0)
            h = x + a * hp
            h_ref[pl.ds(r0, T), :] = h
            o_ref[pl.ds(r0, T), :] = h * _gelu(g_ref[pl.ds(r0, T), :])
            return h[T - 1:T, :]

        carry[0:1, :] = lax.fori_loop(0, rt // T, step, carry[0:1, :])

    row = pl.BlockSpec((rt, W), lambda i: (i, 0))
    grow = pl.BlockSpec((rt, W), lambda i: (i, PROJ_LRU_GATE))
    return pl.pallas_call(
        body, name="lru_scan", grid=(S // rt,), in_specs=[row, row, grow], out_specs=[row] * 2,
        out_shape=[jax.ShapeDtypeStruct((S, W), f32)] * 2, scratch_shapes=[pltpu.VMEM((T, W), f32)],
        compiler_params=_cp("arbitrary"),
    )(log_a, u, proj)


def _lru_scan_bwd(dlru, proj, h, log_a):
    S, W = h.shape
    rt = _pick(S, ROW_TILE, SUBLANES)
    T = SUBLANES
    nblk = S // rt

    def body(d_ref, g_ref, h_ref, la_ref, go_ref, dg_ref, carry):
        @pl.when(pl.program_id(0) == 0)
        def _():
            carry[...] = jnp.zeros_like(carry)

        row = _tile_rows((T, W))

        def step(j, c):
            gn, an = c
            t = rt // T - 1 - j
            r0 = pl.multiple_of(t * T, T)
            d = d_ref[pl.ds(r0, T), :]
            gate = g_ref[pl.ds(r0, T), :]
            a = jnp.exp(la_ref[pl.ds(r0, T), :])
            dg_ref[pl.ds(r0, T), :] = d * h_ref[pl.ds(r0, T), :] * _gelu_grad(gate)
            x = d * _gelu(gate)
            b = jnp.where(row < T - 1, pltpu.roll(a, T - 1, 0), an)
            for k in (1, 2, 4):
                x = x + b * jnp.where(row < T - k, pltpu.roll(x, T - k, 0), 0.0)
                b = b * jnp.where(row < T - k, pltpu.roll(b, T - k, 0), 1.0)
            g = x + b * gn
            go_ref[pl.ds(r0, T), :] = g
            return g[0:1, :], a[0:1, :]

        gn, an = lax.fori_loop(0, rt // T, step, (carry[0:1, :], carry[1:2, :]))
        carry[0:1, :] = gn
        carry[1:2, :] = an

    row = pl.BlockSpec((rt, W), lambda i: (nblk - 1 - i, 0))
    grow = pl.BlockSpec((rt, W), lambda i: (nblk - 1 - i, PROJ_LRU_GATE))
    return pl.pallas_call(
        body, name="lru_scan_bwd", grid=(nblk,), in_specs=[row, grow, row, row], out_specs=[row] * 2,
        out_shape=[jax.ShapeDtypeStruct((S, W), f32)] * 2, scratch_shapes=[pltpu.VMEM((T, W), f32)],
        compiler_params=_cp("arbitrary"),
    )(dlru, proj, h, log_a)


def _lru_gate_bwd(g, h, xc, r, i, log_a, wr, wi, lam):
    S, W = g.shape
    rt = _pick(S, ROW_TILE, SUBLANES)

    def body(g_ref, h_ref, hp_ref, xc_ref, r_ref, i_ref, la_ref, wr_ref, wi_ref, lam_ref,
             dxc_ref, dwr_ref, dwi_ref, acc_ref):
        @pl.when(pl.program_id(0) == 0)
        def _():
            dwr_ref[...] = jnp.zeros_like(dwr_ref)
            dwi_ref[...] = jnp.zeros_like(dwi_ref)
            acc_ref[...] = jnp.zeros_like(acc_ref)

        prev = jnp.where(pl.program_id(0) == 0, 0.0, hp_ref[...])
        gg, xc, r, i, log_a, lam = g_ref[...], xc_ref[...], r_ref[...], i_ref[...], la_ref[...], lam_ref[...]
        hm1 = _shift_down(h_ref[...], prev, 1)
        a = jnp.exp(log_a)
        s = jnp.sqrt(-_expm1(2.0 * log_a))
        da = gg * hm1
        di = gg * s * xc
        dxc = gg * s * i
        ds = gg * i * xc
        dlog_a = da * a - ds * (a * a / s)
        sp = _softplus_neg(lam)
        dr = dlog_a * (-LRU_C * sp)
        dsp = jnp.sum(dlog_a * (-LRU_C * r), axis=0, keepdims=True)
        dpr = dr * r * (1.0 - r)
        dpi = di * i * (1.0 - i)
        dprb, dpib, xb = dpr.astype(bf16), dpi.astype(bf16), xc.astype(bf16)
        dxc_ref[...] = dxc + _dot(dprb, wr_ref[...], NT) + _dot(dpib, wi_ref[...], NT)
        dwr_ref[...] += _dot(xb, dprb, TN)
        dwi_ref[...] += _dot(xb, dpib, TN)
        acc_ref[0:1, :] += jnp.sum(dpr, axis=0, keepdims=True)
        acc_ref[1:2, :] += jnp.sum(dpi, axis=0, keepdims=True)
        acc_ref[2:3, :] += dsp * (-_sigmoid(-lam))

    row = pl.BlockSpec((rt, W), lambda i: (i, 0))
    halo = _prev_halo_spec(rt, W, lambda i: 0)
    vec = pl.BlockSpec((1, W), lambda i: (0, 0))
    mat = pl.BlockSpec((W, W), lambda i: (0, 0))
    acc = pl.BlockSpec((SUBLANES, W), lambda i: (0, 0))
    return pl.pallas_call(
        body, name="lru_gate_bwd", grid=(S // rt,),
        in_specs=[row, row, halo, row, row, row, row, mat, mat, vec], out_specs=[row, mat, mat, acc],
        out_shape=[jax.ShapeDtypeStruct((S, W), f32), jax.ShapeDtypeStruct((W, W), f32),
                   jax.ShapeDtypeStruct((W, W), f32), jax.ShapeDtypeStruct((SUBLANES, W), f32)],
        compiler_params=_cp("arbitrary"),
    )(g, h, h, xc, r, i, log_a, wr, wi, lam.reshape(1, W))


def _conv_bwd(dy, x, conv_w, name, col_tile=None, out_dtype=f32, x_col_block=0, send=(), send_dst=0):
    if dy.ndim == 2:
        dy = dy[None]
    H, S, Ch = dy.shape
    C = H * Ch
    K = conv_w.shape[0]
    ct = Ch if col_tile is None else col_tile
    nct = Ch // ct
    rt = _pick(S, ROW_TILE, SUBLANES)
    nrt = S // rt
    snd = _SiblingSend(send, send_dst)
    n = snd.n

    def body(*refs):
        dy_ref, dyn_ref, x_ref, w_ref = refs[:4]
        s_ins = refs[4:4 + n]
        dx_ref, acc_ref = refs[4 + n:6 + n]
        s_outs, s_sems = refs[6 + n:6 + 2 * n], refs[6 + 2 * n:]
        i = pl.program_id(2)
        if n:
            @pl.when((pl.program_id(0) == 0) & (pl.program_id(1) == 0) & (i == 0))
            def _():
                snd.start(s_ins, s_outs, s_sems)

        @pl.when(i == 0)
        def _():
            acc_ref[...] = jnp.zeros_like(acc_ref)

        nxt = jnp.where(i == nrt - 1, 0.0, dyn_ref[...])
        dy, x = dy_ref[...], x_ref[...]
        ahead = [dy] + [_shift_up(dy, nxt, j) for j in range(1, K)]
        dx = w_ref[K - 1:K, :] * dy
        for k in range(K - 1):
            dx = dx + w_ref[k:k + 1, :] * ahead[K - 1 - k]
        dx_ref[...] = dx.astype(dx_ref.dtype)
        for k in range(K):
            acc_ref[k:k + 1, :] += jnp.sum(ahead[K - 1 - k] * x, axis=0, keepdims=True)
        acc_ref[K:K + 1, :] += jnp.sum(dy, axis=0, keepdims=True)
        if n:
            @pl.when((pl.program_id(0) == H - 1) & (pl.program_id(1) == nct - 1) & (i == nrt - 1))
            def _():
                snd.finish(s_ins, s_outs, s_sems)

    per, last = rt // SUBLANES, S // SUBLANES - 1
    dy_row = pl.BlockSpec((None, rt, ct), lambda h, j, i: (h, i, j))
    dy_next = pl.BlockSpec((None, SUBLANES, ct), lambda h, j, i: (h, jnp.minimum((i + 1) * per, last), j))
    row = pl.BlockSpec((rt, ct), lambda h, j, i: (i, h * nct + j))
    xrow = pl.BlockSpec((rt, ct), lambda h, j, i: (i, h * nct + j + x_col_block))
    res = pl.pallas_call(
        body, name=name, grid=(H, nct, nrt),
        in_specs=[dy_row, dy_next, xrow, pl.BlockSpec((K, ct), lambda h, j, i: (0, h * nct + j))] + snd.in_specs,
        out_specs=[row, pl.BlockSpec((SUBLANES, ct), lambda h, j, i: (0, h * nct + j))] + snd.out_specs,
        out_shape=[jax.ShapeDtypeStruct((S, C), out_dtype), jax.ShapeDtypeStruct((SUBLANES, C), f32)] + snd.out_shape,
        scratch_shapes=snd.scratch,
        compiler_params=_cp(*(("arbitrary",) * 3 if n else ("parallel", "parallel", "arbitrary"))),
    )(dy, dy, x, conv_w, *send)
    return res[0], res[1], list(res[2:])


def _s5_param_fn(a_re, a_im, ls, bt_re, bt_im):
    step = jnp.exp(ls)
    dt_re, dt_im = step * a_re, step * a_im
    mag = jnp.exp(dt_re)
    ab_re, ab_im = mag * jnp.cos(dt_im), mag * jnp.sin(dt_im)
    z_re, z_im = ab_re - 1.0, ab_im
    den = a_re * a_re + a_im * a_im
    f_re = (z_re * a_re + z_im * a_im) / den
    f_im = (z_im * a_re - z_re * a_im) / den
    bb_re = f_re[:, None, :] * bt_re - f_im[:, None, :] * bt_im
    bb_im = f_re[:, None, :] * bt_im + f_im[:, None, :] * bt_re
    return ab_re, ab_im, bb_re, bb_im


def _s5_params(a_re, a_im, ls, bt_re, bt_im):
    def body(ar, ai, l, br, bi, o_ar, o_ai, o_br, o_bi):
        o_ar[...], o_ai[...], o_br[...], o_bi[...] = _s5_param_fn(ar[...], ai[...], l[...], br[...], bi[...])

    return pl.pallas_call(
        body, name="s5_params",
        out_shape=[jax.ShapeDtypeStruct(a_re.shape, f32)] * 2 + [jax.ShapeDtypeStruct(bt_re.shape, f32)] * 2,
        compiler_params=_cp(),
    )(a_re, a_im, ls, bt_re, bt_im)


def _s5_params_bwd(a_re, a_im, ls, bt_re, bt_im, d_ar, d_ai, d_br, d_bi):
    def body(ar, ai, l, br, bi, c_ar, c_ai, c_br, c_bi, g_ar, g_ai, g_l, g_br, g_bi):
        _, vjp = jax.vjp(_s5_param_fn, ar[...], ai[...], l[...], br[...], bi[...])
        g_ar[...], g_ai[...], g_l[...], g_br[...], g_bi[...] = vjp((c_ar[...], c_ai[...], c_br[...], c_bi[...]))

    return pl.pallas_call(
        body, name="s5_params_bwd",
        out_shape=[jax.ShapeDtypeStruct(a_re.shape, f32)] * 2 + [jax.ShapeDtypeStruct(ls.shape, f32)]
        + [jax.ShapeDtypeStruct(bt_re.shape, f32)] * 2,
        compiler_params=_cp(),
    )(a_re, a_im, ls, bt_re, bt_im, d_ar, d_ai, d_br, d_bi)


S5_CHUNK = 256


def _s5_power_tables(ab_ref, p_ref, w_ref, conj):
    T, L = SUBLANES, S5_LANES
    are = ab_ref[0:1, 0:L]
    aim = ab_ref[0:1, L:2 * L]
    if conj:
        aim = -aim
    pre, pim = are, aim
    for n in range(3):
        p_ref[n:n + 1, 0:L] = pre
        p_ref[n:n + 1, L:2 * L] = pim
        pre, pim = pre * pre - pim * pim, 2.0 * pre * pim
    row = _tile_rows((T, L))
    wre = jnp.zeros((T, L), f32)
    wim = jnp.zeros((T, L), f32)
    pre, pim = are, aim
    for n in range(T):
        tgt = (T - 1 - n) if conj else n
        wre = jnp.where(row == tgt, pre, wre)
        wim = jnp.where(row == tgt, pim, wim)
        pre, pim = pre * are - pim * aim, pre * aim + pim * are
    w_ref[:, 0:L] = wre
    w_ref[:, L:2 * L] = wim


def _s5_scan(bu, ab):
    S, L2 = bu.shape
    L = L2 // 2
    rt = _pick(S, 256, SUBLANES)
    T = SUBLANES
    CH = S5_CHUNK

    def body(bu_ref, ab_ref, x_ref, p_ref, w_ref, carry):
        @pl.when(pl.program_id(0) == 0)
        def _():
            carry[...] = jnp.zeros_like(carry)
            _s5_power_tables(ab_ref, p_ref, w_ref, conj=False)

        row = _tile_rows((T, CH))

        def step(t, _):
            r0 = pl.multiple_of(t * T, T)
            for c in range(L // CH):
                lre, lim = pl.ds(c * CH, CH), pl.ds(L + c * CH, CH)
                xr, xi = bu_ref[pl.ds(r0, T), lre], bu_ref[pl.ds(r0, T), lim]
                for n, k in enumerate((1, 2, 4)):
                    pr, pi = p_ref[n:n + 1, lre], p_ref[n:n + 1, lim]
                    sr = jnp.where(row >= k, pltpu.roll(xr, k, 0), 0.0)
                    si = jnp.where(row >= k, pltpu.roll(xi, k, 0), 0.0)
                    xr, xi = xr + pr * sr - pi * si, xi + pr * si + pi * sr
                cr, ci = carry[T - 1:T, lre], carry[T - 1:T, lim]
                wr, wi = w_ref[:, lre], w_ref[:, lim]
                xr, xi = xr + wr * cr - wi * ci, xi + wr * ci + wi * cr
                carry[:, lre] = xr
                carry[:, lim] = xi
                x_ref[pl.ds(r0, T), lre] = xr
                x_ref[pl.ds(r0, T), lim] = xi
            return 0

        lax.fori_loop(0, rt // T, step, 0)

    row_spec = pl.BlockSpec((rt, L2), lambda i: (i, 0))
    return pl.pallas_call(
        body, name="s5_scan", grid=(S // rt,), in_specs=[row_spec, pl.BlockSpec((1, L2), lambda i: (0, 0))],
        out_specs=row_spec, out_shape=jax.ShapeDtypeStruct((S, L2), f32),
        scratch_shapes=[pltpu.VMEM((T, L2), f32), pltpu.VMEM((T, L2), f32), pltpu.VMEM((T, L2), f32)],
        compiler_params=_cp("arbitrary"),
    )(bu, ab)


def _s5_scan_bwd(dx, x, ab):
    S, L2 = dx.shape
    L = L2 // 2
    rt = _pick(S, 256, SUBLANES)
    T = SUBLANES
    CH = S5_CHUNK
    nblk = S // rt
    per = rt // T

    def body(dx_ref, x_ref, xp_ref, ab_ref, g_ref, da_ref, p_ref, w_ref, carry, acc):
        pid = pl.program_id(0)

        @pl.when(pid == 0)
        def _():
            carry[...] = jnp.zeros_like(carry)
            acc[...] = jnp.zeros_like(acc)
            _s5_power_tables(ab_ref, p_ref, w_ref, conj=True)

        row = _tile_rows((T, CH))
        first_block = pid == nblk - 1

        def step(j, _):
            t = per - 1 - j
            r0 = pl.multiple_of(t * T, T)
            rp = pl.multiple_of(jnp.maximum(t - 1, 0) * T, T)
            for c in range(L // CH):
                lre, lim = pl.ds(c * CH, CH), pl.ds(L + c * CH, CH)
                gr, gi = dx_ref[pl.ds(r0, T), lre], dx_ref[pl.ds(r0, T), lim]
                for n, k in enumerate((1, 2, 4)):
                    pr, pi = p_ref[n:n + 1, lre], p_ref[n:n + 1, lim]
                    sr = jnp.where(row < T - k, pltpu.roll(gr, T - k, 0), 0.0)
                    si = jnp.where(row < T - k, pltpu.roll(gi, T - k, 0), 0.0)
                    gr, gi = gr + pr * sr - pi * si, gi + pr * si + pi * sr
                cr, ci = carry[0:1, lre], carry[0:1, lim]
                wr, wi = w_ref[:, lre], w_ref[:, lim]
                gr, gi = gr + wr * cr - wi * ci, gi + wr * ci + wi * cr
                carry[:, lre] = gr
                carry[:, lim] = gi
                g_ref[pl.ds(r0, T), lre] = gr
                g_ref[pl.ds(r0, T), lim] = gi
                xr, xi = x_ref[pl.ds(r0, T), lre], x_ref[pl.ds(r0, T), lim]
                in_blk_r, in_blk_i = x_ref[pl.ds(rp, T), lre], x_ref[pl.ds(rp, T), lim]
                hal_r = jnp.where(first_block, 0.0, xp_ref[:, lre])
                hal_i = jnp.where(first_block, 0.0, xp_ref[:, lim])
                pvr = jnp.where(t == 0, hal_r, in_blk_r)[T - 1:T, :]
                pvi = jnp.where(t == 0, hal_i, in_blk_i)[T - 1:T, :]
                sxr = jnp.where(row >= 1, pltpu.roll(xr, 1, 0), pvr)
                sxi = jnp.where(row >= 1, pltpu.roll(xi, 1, 0), pvi)
                acc[:, lre] += gr * sxr + gi * sxi
                acc[:, lim] += gi * sxr - gr * sxi
            return 0

        lax.fori_loop(0, per, step, 0)

        @pl.when(pid == nblk - 1)
        def _():
            da_ref[...] = jnp.sum(acc[...], axis=0, keepdims=True)

    row_spec = pl.BlockSpec((rt, L2), lambda i: (nblk - 1 - i, 0))
    halo = pl.BlockSpec((T, L2), lambda i: (jnp.maximum((nblk - 1 - i) * per - 1, 0), 0))
    vec = pl.BlockSpec((1, L2), lambda i: (0, 0))
    return pl.pallas_call(
        body, name="s5_scan_bwd", grid=(nblk,), in_specs=[row_spec, row_spec, halo, vec],
        out_specs=[row_spec, vec],
        out_shape=[jax.ShapeDtypeStruct((S, L2), f32), jax.ShapeDtypeStruct((1, L2), f32)],
        scratch_shapes=[pltpu.VMEM((T, L2), f32)] * 4,
        compiler_params=_cp("arbitrary"),
    )(dx, x, x, ab)


def _S5_U_SPEC(rt):
    return pl.BlockSpec((rt, LRU_WIDTH), lambda i: (i, PROJ_S5_U))


def _s5_out(yc, proj, d, wglu, bglu):
    S, W = yc.shape
    rt = _pick(S, ROW_TILE, SUBLANES)

    def body(yc_ref, u_ref, d_ref, w_ref, b_ref, o_ref, y_ref):
        y = yc_ref[...] + d_ref[...] * u_ref[:, 0:W]
        yg = _gelu(y)
        z = _dot(yg.astype(bf16), w_ref[...], NN) + b_ref[...]
        o_ref[...] = yg * _sigmoid(z)
        y_ref[...] = y

    row = pl.BlockSpec((rt, W), lambda i: (i, 0))
    vec = pl.BlockSpec((1, W), lambda i: (0, 0))
    mat = pl.BlockSpec((W, W), lambda i: (0, 0))
    return pl.pallas_call(
        body, name="s5_out", grid=(S // rt,), in_specs=[row, _S5_U_SPEC(rt), vec, mat, vec], out_specs=[row, row],
        out_shape=[jax.ShapeDtypeStruct((S, W), f32)] * 2, compiler_params=_cp("parallel"),
    )(yc, proj, d.reshape(1, W), wglu, bglu.reshape(1, W))


def _s5_out_bwd(dssm, y, proj, d, wglu, bglu):
    S, W = y.shape
    rt = _pick(S, ROW_TILE, SUBLANES)

    def body(do_ref, y_ref, u_ref, d_ref, w_ref, b_ref, dy_ref, du_ref, dw_ref, acc_ref):
        @pl.when(pl.program_id(0) == 0)
        def _():
            dw_ref[...] = jnp.zeros_like(dw_ref)
            acc_ref[...] = jnp.zeros_like(acc_ref)

        do, y = do_ref[...], y_ref[...]
        yg = _gelu(y)
        ygb = yg.astype(bf16)
        sg = _sigmoid(_dot(ygb, w_ref[...], NN) + b_ref[...])
        dz = do * yg * sg * (1.0 - sg)
        dzb = dz.astype(bf16)
        dyg = do * sg + _dot(dzb, w_ref[...], NT)
        dy = dyg * _gelu_grad(y)
        dy_ref[...] = dy
        du_ref[...] = dy * d_ref[...]
        dw_ref[...] += _dot(ygb, dzb, TN)
        acc_ref[0:1, :] += jnp.sum(dz, axis=0, keepdims=True)
        acc_ref[1:2, :] += jnp.sum(dy * u_ref[:, 0:W], axis=0, keepdims=True)

    row = pl.BlockSpec((rt, W), lambda i: (i, 0))
    vec = pl.BlockSpec((1, W), lambda i: (0, 0))
    mat = pl.BlockSpec((W, W), lambda i: (0, 0))
    acc = pl.BlockSpec((SUBLANES, W), lambda i: (0, 0))
    return pl.pallas_call(
        body, name="s5_out_bwd", grid=(S // rt,), in_specs=[row, row, _S5_U_SPEC(rt), vec, mat, vec],
        out_specs=[row, row, mat, acc],
        out_shape=[jax.ShapeDtypeStruct((S, W), f32)] * 2
        + [jax.ShapeDtypeStruct((W, W), f32), jax.ShapeDtypeStruct((SUBLANES, W), f32)],
        compiler_params=_cp("arbitrary"),
    )(dssm, y, proj, d.reshape(1, W), wglu, bglu.reshape(1, W))


MIX_SPLITS = ((0, ATTN_WIDTH), (ATTN_WIDTH, ATTN_WIDTH + LRU_WIDTH), (ATTN_WIDTH + LRU_WIDTH, D_MODEL))


def _mixnorm(attn, lru, ssm, g):
    S = attn.shape[0]
    rt = _pick(S, ROW_TILE, SUBLANES)

    def body(a_ref, l_ref, s_ref, g_ref, o_ref):
        for ref, (lo, hi) in zip((a_ref, l_ref, s_ref), MIX_SPLITS):
            x = ref[...]
            ms = jnp.mean(x * x, axis=-1, keepdims=True)
            o_ref[:, lo:hi] = (x * lax.rsqrt(ms + RMS_EPS) * g_ref[:, lo:hi]).astype(o_ref.dtype)

    rows = [pl.BlockSpec((rt, hi - lo), lambda i: (i, 0)) for lo, hi in MIX_SPLITS]
    return pl.pallas_call(
        body, name="mixnorm", grid=(S // rt,), in_specs=rows + [pl.BlockSpec((1, D_MODEL), lambda i: (0, 0))],
        out_specs=pl.BlockSpec((rt, D_MODEL), lambda i: (i, 0)),
        out_shape=jax.ShapeDtypeStruct((S, D_MODEL), bf16), compiler_params=_cp("parallel"),
    )(attn, lru, ssm, g.reshape(1, D_MODEL))


def _mixnorm_bwd(dmixed, attn, lru, ssm, g):
    S = attn.shape[0]
    rt = _pick(S, ROW_TILE, SUBLANES)

    def body(d_ref, a_ref, l_ref, s_ref, g_ref, da_ref, dl_ref, ds_ref, dlt_ref, acc_ref):
        @pl.when(pl.program_id(0) == 0)
        def _():
            acc_ref[...] = jnp.zeros_like(acc_ref)

        outs = []
        for ref, (lo, hi) in zip((a_ref, l_ref, s_ref), MIX_SPLITS):
            x = ref[...]
            dy = d_ref[:, lo:hi]
            rinv = lax.rsqrt(jnp.mean(x * x, axis=-1, keepdims=True) + RMS_EPS)
            dyg = dy * g_ref[:, lo:hi]
            outs.append(rinv * dyg - x * (rinv * rinv * rinv) * jnp.mean(dyg * x, axis=-1, keepdims=True))
            acc_ref[0:1, lo:hi] += jnp.sum(dy * x * rinv, axis=0, keepdims=True)
        da_ref[...], dl_ref[...], ds_ref[...] = outs
        hi_ = lax.broadcasted_iota(jnp.int32, (ATTN_WIDTH, ATTN_WIDTH), 0) // HEAD_DIM
        hj_ = lax.broadcasted_iota(jnp.int32, (ATTN_WIDTH, ATTN_WIDTH), 1) // HEAD_DIM
        same = jnp.where(hi_ == hj_, 1.0, 0.0).astype(f32)
        dlt_ref[...] = jnp.dot(outs[0] * a_ref[...], same, precision=lax.Precision.HIGHEST, preferred_element_type=f32)

    rows = [pl.BlockSpec((rt, hi - lo), lambda i: (i, 0)) for lo, hi in MIX_SPLITS]
    full = pl.BlockSpec((rt, D_MODEL), lambda i: (i, 0))
    return pl.pallas_call(
        body, name="mixnorm_bwd", grid=(S // rt,),
        in_specs=[full] + rows + [pl.BlockSpec((1, D_MODEL), lambda i: (0, 0))],
        out_specs=rows + [rows[0], pl.BlockSpec((SUBLANES, D_MODEL), lambda i: (0, 0))],
        out_shape=[jax.ShapeDtypeStruct((S, hi - lo), f32) for lo, hi in MIX_SPLITS]
        + [jax.ShapeDtypeStruct((S, ATTN_WIDTH), f32), jax.ShapeDtypeStruct((SUBLANES, D_MODEL), f32)],
        compiler_params=_cp("arbitrary"),
    )(dmixed, attn, lru, ssm, g.reshape(1, D_MODEL))


FFN_COL_TILE = 512


def _ffn_conv(x, prev, w_ref, b_ref, K):
    y = b_ref[...] + w_ref[K - 1:K, :] * x
    for k in range(K - 1):
        y = y + w_ref[k:k + 1, :] * _shift_down(x, prev, K - 1 - k)
    return y


def _ffn_act(up, conv_w, conv_b):
    S, C2 = up.shape
    C = C2 // 2
    K = conv_w.shape[0]
    ct = FFN_COL_TILE
    nct = C // ct
    rt = _pick(S, ROW_TILE, SUBLANES)

    def body(g_ref, gp_ref, v_ref, vp_ref, wg_ref, wv_ref, bg_ref, bv_ref, o_ref):
        first = pl.program_id(1) == 0
        gate = _ffn_conv(g_ref[...], jnp.where(first, 0.0, gp_ref[...]), wg_ref, bg_ref, K)
        val = _ffn_conv(v_ref[...], jnp.where(first, 0.0, vp_ref[...]), wv_ref, bv_ref, K)
        o_ref[...] = (_gelu(gate) * val).astype(o_ref.dtype)

    def specs(off):
        return (pl.BlockSpec((rt, ct), lambda j, i: (i, j + off)), _prev_halo_spec(rt, ct, lambda j, i: j + off))

    def wspec(off, rows):
        return pl.BlockSpec((rows, ct), lambda j, i: (0, j + off))

    g_s, gp_s = specs(0)
    v_s, vp_s = specs(nct)
    return pl.pallas_call(
        body, name="ffn_act", grid=(nct, S // rt),
        in_specs=[g_s, gp_s, v_s, vp_s, wspec(0, K), wspec(nct, K), wspec(0, 1), wspec(nct, 1)],
        out_specs=pl.BlockSpec((rt, ct), lambda j, i: (i, j)),
        out_shape=jax.ShapeDtypeStruct((S, C), bf16), compiler_params=_cp("parallel", "parallel"),
    )(up, up, up, up, conv_w, conv_w, conv_b.reshape(1, C2), conv_b.reshape(1, C2))


def _ffn_act_bwd(dact, up, conv_w, conv_b):
    S, C2 = up.shape
    C = C2 // 2
    K = conv_w.shape[0]
    ct = FFN_COL_TILE
    nct = C // ct
    rt = _pick(S, ROW_TILE, SUBLANES)

    def body(d_ref, g_ref, gp_ref, v_ref, vp_ref, wg_ref, wv_ref, bg_ref, bv_ref, o_ref):
        first = pl.program_id(1) == 0
        gate = _ffn_conv(g_ref[...], jnp.where(first, 0.0, gp_ref[...]), wg_ref, bg_ref, K)
        val = _ffn_conv(v_ref[...], jnp.where(first, 0.0, vp_ref[...]), wv_ref, bv_ref, K)
        d = d_ref[...]
        gl, dgl = _gelu_pair(gate)
        o_ref[0] = d * val * dgl
        o_ref[1] = d * gl

    def specs(off):
        return (pl.BlockSpec((rt, ct), lambda j, i: (i, j + off)), _prev_halo_spec(rt, ct, lambda j, i: j + off))

    def wspec(off, rows):
        return pl.BlockSpec((rows, ct), lambda j, i: (0, j + off))

    g_s, gp_s = specs(0)
    v_s, vp_s = specs(nct)
    return pl.pallas_call(
        body, name="ffn_act_bwd", grid=(nct, S // rt),
        in_specs=[pl.BlockSpec((rt, ct), lambda j, i: (i, j)), g_s, gp_s, v_s, vp_s,
                  wspec(0, K), wspec(nct, K), wspec(0, 1), wspec(nct, 1)],
        out_specs=pl.BlockSpec((2, rt, ct), lambda j, i: (0, i, j)),
        out_shape=jax.ShapeDtypeStruct((2, S, C), f32), compiler_params=_cp("parallel", "parallel"),
    )(dact, up, up, up, up, conv_w, conv_w, conv_b.reshape(1, C2), conv_b.reshape(1, C2))


ANY = pl.BlockSpec(memory_space=pl.ANY)


def _rows_for(cols):
    return max(16, (1 << 17) // cols)


def _chips(x, y):
    return [(1 - x, y), (x, 1 - y), (1 - x, 1 - y)]


class _Gather:
    def __init__(self, shards, split):
        self.shapes = [s.shape for s in shards]
        self.dtypes = [s.dtype for s in shards]
        self.split = list(split)
        self.nt = len(shards)
        self.in_specs = [ANY] * self.nt
        self.out_specs = [ANY] * self.nt
        self.out_shape = [jax.ShapeDtypeStruct((4,) + s, dt) for s, dt in zip(self.shapes, self.dtypes)]
        self.scratch = [pltpu.SemaphoreType.DMA((3, self.nt))] * 4 if self.nt else []

    def _part(self, ref, t, half):
        if not self.split[t]:
            return ref
        r = self.shapes[t][0] // 2
        return ref.at[pl.ds(half * r, r), :]

    def _ici(self, ins, outs, sems, k, t, chip, landing_chip):
        x, y, c = lax.axis_index("x"), lax.axis_index("y"), lax.axis_index("c")
        return pltpu.make_async_remote_copy(
            src_ref=self._part(ins[t], t, c), dst_ref=self._part(outs[t].at[landing_chip], t, c),
            send_sem=sems[0].at[k, t], recv_sem=sems[1].at[k, t], device_id=(chip[0], chip[1], c), device_id_type=MESH)

    def _d2d(self, outs, sems, k, t, q, half):
        x, y, c = lax.axis_index("x"), lax.axis_index("y"), lax.axis_index("c")
        rows = self._part(outs[t].at[q], t, half)
        return pltpu.make_async_remote_copy(
            src_ref=rows, dst_ref=rows, send_sem=sems[2].at[k, t], recv_sem=sems[3].at[k, t],
            device_id=(x, y, 1 - c), device_id_type=MESH)

    def start(self, ins, outs, sems):
        x, y = lax.axis_index("x"), lax.axis_index("y")
        me = 2 * x + y
        for k, chip in enumerate(_chips(x, y)):
            for t in range(self.nt):
                self._ici(ins, outs, sems, k, t, chip, me).start()

    def finish(self, ins, outs, sems):
        x, y, c = lax.axis_index("x"), lax.axis_index("y"), lax.axis_index("c")
        me = 2 * x + y
        chips = _chips(x, y)
        for k, chip in enumerate(chips):
            q = 2 * chip[0] + chip[1]
            for t in range(self.nt):
                self._ici(ins, outs, sems, k, t, chip, q).wait_recv()
                if self.split[t]:
                    self._d2d(outs, sems, k, t, q, c).start()
        for k, chip in enumerate(chips):
            q = 2 * chip[0] + chip[1]
            for t in range(self.nt):
                if self.split[t]:
                    self._d2d(outs, sems, k, t, q, 1 - c).wait_recv()
        for k, chip in enumerate(chips):
            q = 2 * chip[0] + chip[1]
            for t in range(self.nt):
                self._ici(ins, outs, sems, k, t, chip, me).wait_send()
                if self.split[t]:
                    self._d2d(outs, sems, k, t, q, c).wait_send()


def _gather_weights(shards, split):
    g = _Gather(shards, split)
    nt = g.nt

    def body(*refs):
        ins, outs, sems = refs[:nt], refs[nt:2 * nt], refs[2 * nt:]
        g.start(ins, outs, sems)
        g.finish(ins, outs, sems)

    return pl.pallas_call(
        body, name="gather_weights", in_specs=g.in_specs, out_specs=g.out_specs, out_shape=g.out_shape,
        scratch_shapes=g.scratch,
    )(*shards)


class _SiblingSend:
    def __init__(self, gs, dst_c, swap=()):
        self.nt, self.n, self.dst_c = len(gs), len(gs) + len(swap), dst_c
        self.in_specs = [ANY] * self.n
        self.out_specs = [ANY] * self.n
        self.out_shape = [jax.ShapeDtypeStruct(g.shape, g.dtype) for g in list(gs) + list(swap)]
        self.scratch = [pltpu.SemaphoreType.DMA((self.n,))] * 2 if self.n else []

    def _each(self, ins, outs, sems, sender, fn):
        x, y, c = lax.axis_index("x"), lax.axis_index("y"), lax.axis_index("c")

        def cp(t):
            return pltpu.make_async_remote_copy(
                src_ref=ins[t], dst_ref=outs[t], send_sem=sems[0].at[t], recv_sem=sems[1].at[t],
                device_id=(x, y, 1 - c), device_id_type=MESH)

        if self.nt:
            @pl.when((c != self.dst_c) if sender else (c == self.dst_c))
            def _():
                for t in range(self.nt):
                    fn(cp(t))
        for t in range(self.nt, self.n):
            fn(cp(t))

    def start(self, ins, outs, sems):
        self._each(ins, outs, sems, True, lambda cp: cp.start())

    def finish(self, ins, outs, sems):
        self._each(ins, outs, sems, False, lambda cp: cp.wait_recv())
        self._each(ins, outs, sems, True, lambda cp: cp.wait_send())


def _sibling_send(gs, dst_c, swap=()):
    snd = _SiblingSend(gs, dst_c, swap)
    n = snd.n

    def body(*refs):
        ins, outs, sems = refs[:n], refs[n:2 * n], refs[2 * n:]
        snd.start(ins, outs, sems)
        snd.finish(ins, outs, sems)

    res = pl.pallas_call(
        body, name="sibling_send", in_specs=snd.in_specs, out_specs=snd.out_specs, out_shape=snd.out_shape,
        scratch_shapes=snd.scratch,
    )(*gs, *swap)
    return list(res[:snd.nt]), list(res[snd.nt:])


def _owner_flag(owner_c):
    return (lax.axis_index("c") == owner_c).astype(jnp.int32).reshape(1)


def _pair_sum(g, other, name, owner_c):
    R, C = g.shape
    rt = _pick(R, _rows_for(C), 16)

    def body(on_ref, a_ref, o_ref, out_ref):
        out_ref[...] = (a_ref[...] + o_ref[...]).astype(out_ref.dtype)

    row = pl.BlockSpec((rt, C), lambda i, on: (i * on[0], 0))
    return pl.pallas_call(
        body, name=name,
        grid_spec=pltpu.PrefetchScalarGridSpec(num_scalar_prefetch=1, grid=(R // rt,), in_specs=[row, row],
                                               out_specs=row),
        out_shape=jax.ShapeDtypeStruct((R, C), bf16), compiler_params=_cp("arbitrary"),
    )(_owner_flag(owner_c), g, other)


class _ChipExchange:
    def __init__(self, slabs, whole, only_c):
        self.ns, self.nw, self.only_c = len(slabs), len(whole), only_c
        self.n = self.ns + self.nw
        self.in_specs = [ANY] * self.n
        self.out_specs = [ANY] * self.n
        self.out_shape = ([jax.ShapeDtypeStruct(s.shape, s.dtype) for s in slabs]
                          + [jax.ShapeDtypeStruct((4,) + w.shape, w.dtype) for w in whole])
        self.scratch = [pltpu.SemaphoreType.DMA((3, self.n))] * 2 if self.n else []

    def _copy(self, ins, outs, sems, k, t, chip, landing_chip):
        c = lax.axis_index("c")
        src = ins[t].at[2 * chip[0] + chip[1]] if t < self.ns else ins[t]
        return pltpu.make_async_remote_copy(
            src_ref=src, dst_ref=outs[t].at[landing_chip], send_sem=sems[0].at[k, t], recv_sem=sems[1].at[k, t],
            device_id=(chip[0], chip[1], c), device_id_type=MESH)

    def _each(self, fn):
        x, y, c = lax.axis_index("x"), lax.axis_index("y"), lax.axis_index("c")
        chips = _chips(x, y)
        if self.ns:
            @pl.when(c == self.only_c)
            def _():
                for k, chip in enumerate(chips):
                    for t in range(self.ns):
                        fn(k, t, chip)
        for k, chip in enumerate(chips):
            for t in range(self.ns, self.n):
                fn(k, t, chip)

    def start(self, ins, outs, sems):
        me = 2 * lax.axis_index("x") + lax.axis_index("y")
        self._each(lambda k, t, chip: self._copy(ins, outs, sems, k, t, chip, me).start())

    def finish(self, ins, outs, sems):
        me = 2 * lax.axis_index("x") + lax.axis_index("y")
        self._each(lambda k, t, chip: self._copy(ins, outs, sems, k, t, chip, 2 * chip[0] + chip[1]).wait_recv())
        self._each(lambda k, t, chip: self._copy(ins, outs, sems, k, t, chip, me).wait_send())


def _chip_exchange(slabs, whole, only_c):
    ex = _ChipExchange(slabs, whole, only_c)
    n = ex.n

    def body(*refs):
        ins, outs, sems = refs[:n], refs[n:2 * n], refs[2 * n:]
        ex.start(ins, outs, sems)
        ex.finish(ins, outs, sems)

    res = pl.pallas_call(
        body, name="chip_exchange", in_specs=ex.in_specs, out_specs=ex.out_specs, out_shape=ex.out_shape,
        scratch_shapes=ex.scratch,
    )(*slabs, *whole)
    return list(res[:ex.ns]), list(res[ex.ns:])


def _sum_chips(recv, own, name, owner_c=None):
    n, r, C = recv.shape
    rt = _pick(r, _rows_for(C), 16)
    own3 = own.ndim == 3

    def body(on_ref, r_ref, o_ref, out_ref):
        me = 2 * lax.axis_index("x") + lax.axis_index("y")
        acc = None
        for q in range(n):
            term = jnp.where(me == q, o_ref[q] if own3 else o_ref[...], r_ref[q]).astype(f32)
            acc = term if acc is None else acc + term
        out_ref[...] = acc

    blk = pl.BlockSpec((n, rt, C), lambda i, on: (0, i * on[0], 0))
    row = pl.BlockSpec((rt, C), lambda i, on: (i * on[0], 0))
    flag = jnp.ones((1,), jnp.int32) if owner_c is None else _owner_flag(owner_c)
    return pl.pallas_call(
        body, name=name,
        grid_spec=pltpu.PrefetchScalarGridSpec(num_scalar_prefetch=1, grid=(r // rt,),
                                               in_specs=[blk, blk if own3 else row], out_specs=row),
        out_shape=jax.ShapeDtypeStruct((r, C), f32), compiler_params=_cp("arbitrary"),
    )(flag, recv, own)


def _pair_swap(parts0, parts1):
    nt = len(parts0)

    def body(*refs):
        in0, in1, outs = refs[:nt], refs[nt:2 * nt], refs[2 * nt:3 * nt]
        send, recv = refs[3 * nt:]
        x, y, c = lax.axis_index("x"), lax.axis_index("y"), lax.axis_index("c")

        def copies(ins):
            return [pltpu.make_async_remote_copy(
                src_ref=ins[t], dst_ref=outs[t], send_sem=send.at[t], recv_sem=recv.at[t],
                device_id=(x, y, 1 - c), device_id_type=MESH) for t in range(nt)]

        @pl.when(c == 1)
        def _():
            for cp in copies(in0):
                cp.start()

        @pl.when(c == 0)
        def _():
            for cp in copies(in1):
                cp.start()

        for cp in copies(in0):
            cp.wait()

    return pl.pallas_call(
        body, name="pair_swap", in_specs=[ANY] * (2 * nt), out_specs=[ANY] * nt,
        out_shape=[jax.ShapeDtypeStruct(p.shape, p.dtype) for p in parts0],
        scratch_shapes=[pltpu.SemaphoreType.DMA((nt,))] * 2,
    )(*parts0, *parts1)


def _adamw_layers(mine0, mine1, theirs, w, m, v, name):
    L, r, C = w.shape
    rt = _pick(r, _rows_for(C), 16)

    def body(a0_ref, a1_ref, b_ref, w_ref, m_ref, v_ref, g_ref, d_ref, mo_ref, vo_ref):
        layer, c = pl.program_id(0), lax.axis_index("c")
        own = jnp.where(layer == 0, a0_ref[...], a1_ref[...])
        g_ref[...] = jnp.where(layer == 1 - c, own, b_ref[...])
        _adamw_math(g_ref, w_ref, m_ref, v_ref, d_ref, mo_ref, vo_ref)

    flat = pl.BlockSpec((rt, C), lambda l, i: (i, 0))
    lay = pl.BlockSpec((None, rt, C), lambda l, i: (l, i, 0))
    return pl.pallas_call(
        body, name=name, grid=(L, r // rt), in_specs=[flat, flat, flat, lay, lay, lay], out_specs=[lay] * 4,
        out_shape=[jax.ShapeDtypeStruct((L, r, C), f32)] * 4, compiler_params=_cp("parallel", "parallel"),
    )(mine0, mine1, theirs, w, m, v)


def _adamw_math(g_ref, w_ref, m_ref, v_ref, d_ref, mo_ref, vo_ref):
    gg = g_ref[...]
    m_new = ADAM_B1 * m_ref[...] + (1.0 - ADAM_B1) * gg
    v_new = ADAM_B2 * v_ref[...] + (1.0 - ADAM_B2) * (gg * gg)
    m_hat = m_new / (1.0 - ADAM_B1 ** ADAM_STEP)
    v_hat = v_new / (1.0 - ADAM_B2 ** ADAM_STEP)
    d_ref[...] = -ADAM_LR * (m_hat / (jnp.sqrt(v_hat) + ADAM_EPS) + ADAM_WD * w_ref[...])
    mo_ref[...] = m_new
    vo_ref[...] = v_new


FLAT_TILE = 2048


def _add2(a, b, name):
    R = a.shape[0]
    rt = _pick(R, FLAT_TILE, SUBLANES)

    def body(a_ref, b_ref, o_ref):
        o_ref[...] = a_ref[...] + b_ref[...]

    row = pl.BlockSpec((rt, LANES), lambda i: (i, 0))
    return pl.pallas_call(
        body, name=name, grid=(R // rt,), in_specs=[row, row], out_specs=row,
        out_shape=jax.ShapeDtypeStruct((R, LANES), f32), compiler_params=_cp("parallel"),
    )(a, b)


def _adamw(g, w, m, v, name):
    R = g.shape[0]
    rt = _pick(R, FLAT_TILE, SUBLANES)

    def body(g_ref, w_ref, m_ref, v_ref, d_ref, mo_ref, vo_ref):
        gg = g_ref[...]
        m_new = ADAM_B1 * m_ref[...] + (1.0 - ADAM_B1) * gg
        v_new = ADAM_B2 * v_ref[...] + (1.0 - ADAM_B2) * (gg * gg)
        m_hat = m_new / (1.0 - ADAM_B1 ** ADAM_STEP)
        v_hat = v_new / (1.0 - ADAM_B2 ** ADAM_STEP)
        d_ref[...] = -ADAM_LR * (m_hat / (jnp.sqrt(v_hat) + ADAM_EPS) + ADAM_WD * w_ref[...])
        mo_ref[...] = m_new
        vo_ref[...] = v_new

    row = pl.BlockSpec((rt, LANES), lambda i: (i, 0))
    return pl.pallas_call(
        body, name=name, grid=(R // rt,), in_specs=[row] * 4, out_specs=[row] * 3,
        out_shape=[jax.ShapeDtypeStruct((R, LANES), f32)] * 3, compiler_params=_cp("parallel"),
    )(g, w, m, v)


def _pack(arrs, dtype):
    flat = jnp.concatenate([a.astype(dtype).reshape(-1) for a in arrs])
    per = FLAT_TILE * LANES
    flat = jnp.pad(flat, (0, (-flat.shape[0]) % per))
    return flat.reshape(-1, LANES)


def _unpack(buf, shapes):
    flat = buf.reshape(-1)
    out, off = [], 0
    for s in shapes:
        n = math.prod(s)
        out.append(flat[off:off + n].reshape(s))
        off += n
    return out


def _block_diag(w):
    n, a, b = w.shape
    eye = jnp.eye(n, dtype=w.dtype)
    return (w[:, :, None, :] * eye[:, None, :, None]).reshape(n * a, n * b)


def _diag_blocks(m, n):
    a, b = m.shape[0] // n, m.shape[1] // n
    idx = jnp.arange(n)
    return m.reshape(n, a, n, b)[idx, :, idx, :]


BIG = ("w_in", "w_out", "w_up", "w_down", "s5_w_glu")
BIG_COL_SHARDED = {"w_in": True, "w_out": False, "w_up": True, "w_down": False, "s5_w_glu": False}
CONV_SHARDED = ("lru_conv_w", "ffn_conv_w")
SMALL = ("lru_conv_b", "lru_wr", "lru_br", "lru_wi", "lru_bi", "lru_lambda", "s5_a_re", "s5_a_im", "s5_b_re",
         "s5_b_im", "s5_c_re", "s5_c_im", "s5_d", "s5_log_step", "s5_b_glu", "mix_norm_g", "ln1_g", "ln1_b",
         "ffn_conv_b", "ln2_g", "ln2_b")
WEIGHTS = ("w_in", "lru_conv_w", "lru_conv_b", "lru_wr", "lru_br", "lru_wi", "lru_bi", "lru_lambda", "s5_a_re",
           "s5_a_im", "s5_b_re", "s5_b_im", "s5_c_re", "s5_c_im", "s5_d", "s5_log_step", "s5_w_glu", "s5_b_glu",
           "mix_norm_g", "w_out", "ln1_g", "ln1_b", "w_up", "ffn_conv_w", "ffn_conv_b", "w_down", "ln2_g", "ln2_b")


def _assemble(slabs, col_sharded):
    _, L, r, c = slabs.shape
    if col_sharded:
        return slabs.transpose(1, 2, 0, 3).reshape(L, r, 4 * c)
    return slabs.transpose(1, 0, 2, 3).reshape(L, 4 * r, c)


def _s5_prepare(p):
    G = N_S5_GROUPS
    bt_re, bt_im = p["s5_b_re"].transpose(0, 2, 1), p["s5_b_im"].transpose(0, 2, 1)
    ls = p["s5_log_step"].reshape(G, 1)
    ab_re, ab_im, bb_re, bb_im = _s5_params(p["s5_a_re"], p["s5_a_im"], ls, bt_re, bt_im)
    ab = jnp.concatenate([ab_re.reshape(1, S5_LANES), ab_im.reshape(1, S5_LANES)], axis=1)
    bbcat = jnp.concatenate([_block_diag(bb_re), _block_diag(bb_im)], axis=1).astype(bf16)
    ccat = jnp.concatenate([_block_diag(p["s5_c_re"].transpose(0, 2, 1)),
                            -_block_diag(p["s5_c_im"].transpose(0, 2, 1))], axis=0).astype(bf16)
    bbcat_pad = jnp.concatenate([bbcat, jnp.zeros((LRU_WIDTH - S5_WIDTH, 2 * S5_LANES), bf16)], axis=0)
    return dict(bt_re=bt_re, bt_im=bt_im, ls=ls, ab=ab, bbcat=bbcat, bbcat_pad=bbcat_pad, ccat=ccat)


def _layer_fwd(h, p, cos, sin, pending, install):
    sv = {"h": h}
    proj = _mm(h, p["w_in"], "nn", "mm_proj", tn=768)
    sv.update(proj=proj)
    qr, kr = _rope_fwd(proj, cos, sin)
    attn, ltot, gathered = _attn_fwd2(qr, kr, proj, [s for _, _, s in pending], [True] * len(pending))
    install(pending, gathered)
    sv.update(qr=qr, kr=kr, attn=attn, ltot=ltot)
    wr, wi = _block_diag(p["lru_wr"]).astype(bf16), _block_diag(p["lru_wi"]).astype(bf16)
    xc, r, i, log_a, u = _lru_pre(proj, p["lru_conv_w"], p["lru_conv_b"], wr, p["lru_br"], wi, p["lru_bi"],
                                  p["lru_lambda"])
    hl, lru = _lru_scan(log_a, u, proj)
    sv.update(wr=wr, wi=wi, xc=xc, r=r, i=i, log_a=log_a, hl=hl, lru=lru)
    s5 = _s5_prepare(p)
    bu = _mm(proj, s5["bbcat_pad"], "nn", "mm_s5_bu", a_win=(PROJ_S5_U, LRU_WIDTH))
    xs = _s5_scan(bu, s5["ab"])
    yc = _mm(xs, s5["ccat"], "nn", "mm_s5_y")
    ssm, y = _s5_out(yc, proj, p["s5_d"].reshape(-1), p["s5_w_glu"], p["s5_b_glu"])
    sv.update(s5=s5, xs=xs, y=y, ssm=ssm)
    mixed = _mixnorm(attn, lru, ssm, p["mix_norm_g"])
    mix = _mm(mixed, p["w_out"], "nn", "mm_out")
    h1, z1 = _ln_fwd(h, mix, p["ln1_g"], p["ln1_b"], "ln_fwd")
    sv.update(mixed=mixed, z1=z1, h1=h1)
    up = _mm(h1, p["w_up"], "nn", "mm_up", tn=1536)
    act = _ffn_act(up, p["ffn_conv_w"], p["ffn_conv_b"])
    ffn = _mm(act, p["w_down"], "nn", "mm_down")
    h2, z2 = _ln_fwd(h1, ffn, p["ln2_g"], p["ln2_b"], "ln_fwd")
    sv.update(up=up, act=act, z2=z2)
    return h2, sv


def _layer_bwd(dy_a, dy_b, p, sv, cos, sin, relay=None):
    gr = {}
    dz2, acc = _ln_bwd(dy_a, dy_b, sv["z2"], p["ln2_g"], "ln_bwd_top" if dy_a is None else "ln_bwd")
    gr["ln2_g"], gr["ln2_b"] = acc[0], acc[1]
    dact = _mm(dz2, p["w_down"], "nt", "mm_dact")
    gr["w_down"] = _mm(sv["act"], dz2, "tn", "mm_dw_down")
    dupc = _ffn_act_bwd(dact, sv["up"], p["ffn_conv_w"], p["ffn_conv_b"])
    send, relay_c, to_slabs = relay if relay else ((), 0, None)
    dup, acc, from_sibling = _conv_bwd(dupc, sv["up"], p["ffn_conv_w"], "ffn_conv_bwd", col_tile=FFN_COL_TILE,
                                       out_dtype=bf16, send=send, send_dst=relay_c)
    slabs = to_slabs(from_sibling) if relay else ()
    gr["ffn_conv_w"], gr["ffn_conv_b"] = acc[0:3], acc[3]
    dh1 = _mm(dup, p["w_up"], "nt", "mm_dh1", tk=2048)
    gr["w_up"] = _mm(sv["h1"], dup, "tn", "mm_dw_up", tn=1536)
    dz1, acc = _ln_bwd(dz2, dh1, sv["z1"], p["ln1_g"], "ln_bwd")
    gr["ln1_g"], gr["ln1_b"] = acc[0], acc[1]
    dmixed = _mm(dz1, p["w_out"], "nt", "mm_dmixed")
    gr["w_out"] = _mm(sv["mixed"], dz1, "tn", "mm_dw_out")
    dattn, dlru, dssm, delta, acc = _mixnorm_bwd(dmixed, sv["attn"], sv["lru"], sv["ssm"], p["mix_norm_g"])
    gr["mix_norm_g"] = acc[0]
    proj = sv["proj"]
    dqr, dkr, dv, received = _attn_bwd2(sv["qr"], sv["kr"], proj, dattn, sv["ltot"], delta, slabs, relay_c)
    dq, dk = _rope_bwd(dqr, dkr, cos, sin)
    g, dgate = _lru_scan_bwd(dlru, proj, sv["hl"], sv["log_a"])
    dxc, dwr, dwi, acc = _lru_gate_bwd(g, sv["hl"], sv["xc"], sv["r"], sv["i"], sv["log_a"], sv["wr"], sv["wi"],
                                       p["lru_lambda"])
    gr["lru_wr"], gr["lru_wi"] = _diag_blocks(dwr, N_LRU_HEADS), _diag_blocks(dwi, N_LRU_HEADS)
    gr["lru_br"], gr["lru_bi"], gr["lru_lambda"] = acc[0], acc[1], acc[2]
    dxr, acc, _ = _conv_bwd(dxc, proj, p["lru_conv_w"], "lru_conv_bwd", x_col_block=PROJ_LRU_X)
    gr["lru_conv_w"], gr["lru_conv_b"] = acc[0:4], acc[4]
    s5 = sv["s5"]
    G = N_S5_GROUPS
    dy, du_direct, dwglu, acc = _s5_out_bwd(dssm, sv["y"], proj, p["s5_d"].reshape(-1), p["s5_w_glu"],
                                            p["s5_b_glu"])
    gr["s5_w_glu"], gr["s5_b_glu"], gr["s5_d"] = dwglu, acc[0], acc[1].reshape(G, S5_GROUP)
    dxs = _mm(dy, s5["ccat"], "nt", "mm_s5_dx")
    dccat = _mm(sv["xs"], dy, "tn", "mm_s5_dc")
    gr["s5_c_re"] = _diag_blocks(dccat[:S5_LANES], G).transpose(0, 2, 1)
    gr["s5_c_im"] = -_diag_blocks(dccat[S5_LANES:], G).transpose(0, 2, 1)
    gs, dab = _s5_scan_bwd(dxs, sv["xs"], s5["ab"])
    du = _mm(gs, s5["bbcat"], "nt", "mm_s5_du", add=du_direct)
    dbbcat = _mm(proj, gs, "tn", "mm_s5_dbb", a_win=(PROJ_S5_U, LRU_WIDTH))[:S5_WIDTH]
    d_ar, d_ai, d_ls, d_btr, d_bti = _s5_params_bwd(
        p["s5_a_re"], p["s5_a_im"], s5["ls"], s5["bt_re"], s5["bt_im"],
        dab[:, :S5_LANES].reshape(G, S5_STATE), dab[:, S5_LANES:].reshape(G, S5_STATE),
        _diag_blocks(dbbcat[:, :S5_LANES], G), _diag_blocks(dbbcat[:, S5_LANES:], G))
    gr["s5_a_re"], gr["s5_a_im"], gr["s5_log_step"] = d_ar, d_ai, d_ls.reshape(G)
    gr["s5_b_re"], gr["s5_b_im"] = d_btr.transpose(0, 2, 1), d_bti.transpose(0, 2, 1)
    pad = jnp.zeros((du.shape[0], D_IN_PAD - D_IN), f32)
    dproj = jnp.concatenate([dq, dk, dv, dxr, dgate, du, pad], axis=1).astype(bf16)
    gr["w_in"] = _mm(sv["h"], dproj, "tn", "mm_dw_in", tn=768)[:, :D_IN]
    dh = _mm(dproj, p["w_in"], "nt", "mm_dh")
    return (dz1, dh, gr, slabs, received) if relay else (dz1, dh, gr)


def _train_step(d):
    x, target = d["x"][0], d["loss_target"][0]
    S = x.shape[0]
    me = 2 * lax.axis_index("x") + lax.axis_index("y")

    def rows2d(a):
        return a.reshape(a.shape[0] * a.shape[1], a.shape[2])

    params = [{n: d[n][l] for n in SMALL} for l in range(DEPTH)]

    def install(items, gathered):
        for (n, l, mine), g in zip(items, gathered):
            g = lax.dynamic_update_slice_in_dim(g, mine[None], me, axis=0)
            if n in CONV_SHARDED:
                full = _assemble(g.reshape((4,) + d[n].shape), True)
                for k in range(DEPTH):
                    params[k][n] = full[k]
                continue
            full = _assemble(g[:, None], BIG_COL_SHARDED[n])[0]
            if n == "w_in":
                full = jnp.pad(full, ((0, 0), (0, D_IN_PAD - D_IN)))
            params[l][n] = full

    def shard(n, l):
        return (n, l, d[n][l].astype(bf16))

    first = [shard("w_in", 0)] + [(n, None, rows2d(d[n])) for n in CONV_SHARDED]
    install(first, _gather_weights([s for _, _, s in first], [True] + [False] * len(CONV_SHARDED)))
    later = [[shard(n, 0) for n in BIG[1:]] + [shard("w_in", 1)], [shard(n, 1) for n in BIG[1:]]]

    cos, sin = _rope_tables(S)
    h, saved = x, []
    for l in range(DEPTH):
        h, sv = _layer_fwd(h, params[l], cos, sin, later[l], install)
        saved.append(sv)
    dy, loss_acc = _loss_head(h, target)
    def chip_slabs(layer, others):
        chip = [_pair_sum(grads[layer][n], o, "pair_sum_" + n, 1 - layer) for n, o in zip(BIG, others)]
        return [p.reshape(p.shape[0], 4, p.shape[1] // 4).transpose(1, 0, 2) if BIG_COL_SHARDED[n]
                else p.reshape(4, p.shape[0] // 4, p.shape[1]) for n, p in zip(BIG, chip)]

    da, db, grads = None, dy, [None] * DEPTH
    da, db, grads[1] = _layer_bwd(da, db, params[1], saved[1], cos, sin)
    relay = ([grads[1][n] for n in BIG], 0, functools.partial(chip_slabs, 1))
    da, db, grads[0], slabs1, recv1 = _layer_bwd(da, db, params[0], saved[0], cos, sin, relay)
    out = {"grad_x": _axpy(da, db, "grad_x")[None]}
    G = {n: jnp.stack([grads[l][n] for l in range(DEPTH)]) for n in SMALL + CONV_SHARDED}

    small = SMALL + CONV_SHARDED
    sp = _pack([G[n] for n in small], f32)
    others0, (sp_sibling,) = _sibling_send([grads[0][n] for n in BIG], 1, [sp])
    slabs0 = chip_slabs(0, others0)
    chip_small = _add2(sp, sp_sibling, "pair_sum_small")
    recv0, (recv_small,) = _chip_exchange(slabs0, [chip_small], 1)
    mine0 = [_sum_chips(r, s, "sum_chips_" + n, 1) for n, r, s in zip(BIG, recv0, slabs0)]
    mine1 = [_sum_chips(r, s, "sum_chips_" + n, 0) for n, r, s in zip(BIG, recv1, slabs1)]
    theirs = _pair_swap(mine0, mine1)
    for n, a0, a1, b in zip(BIG, mine0, mine1, theirs):
        upd = _adamw_layers(a0, a1, b, d[n], d["m_" + n], d["v_" + n], "adamw_" + n)
        for pre, u in zip(("grad_", "delta_", "new_m_", "new_v_"), upd):
            out[pre + n] = u

    total = _sum_chips(recv_small, chip_small, "sum_chips_small")
    gs = dict(zip(small, _unpack(total, [G[n].shape for n in small])))
    for n in CONV_SHARDED:
        L, K, C = gs[n].shape
        gs[n] = lax.dynamic_index_in_dim(gs[n].reshape(L, K, 4, C // 4), me, axis=2, keepdims=False)
    small_shapes = [d[n].shape for n in small]
    gsmall = _pack([gs[n] for n in small], f32)
    upd = _adamw(gsmall, _pack([d[n] for n in small], f32), _pack([d["m_" + n] for n in small], f32),
                 _pack([d["v_" + n] for n in small], f32), "adamw_small")
    for pre, buf in zip(("grad_", "delta_", "new_m_", "new_v_"), (gsmall,) + tuple(upd)):
        for n, a in zip(small, _unpack(buf, small_shapes)):
            out[pre + n] = a

    loss_local, _ = lax.optimization_barrier((loss_acc[0, 0], upd[0]))
    out["loss"] = lax.psum(loss_local, ("x", "y", "c"))
    return (out["loss"], out["grad_x"]) + tuple(out[pre + n] for pre in ("grad_", "delta_", "new_m_", "new_v_")
                                                for n in WEIGHTS)


def kernel(
        x, w_in, lru_conv_w, lru_conv_b, lru_wr, lru_br, lru_wi, lru_bi, lru_lambda, s5_a_re, s5_a_im, s5_b_re,
        s5_b_im, s5_c_re, s5_c_im, s5_d, s5_log_step, s5_w_glu, s5_b_glu, mix_norm_g, w_out, ln1_g, ln1_b, w_up,
        ffn_conv_w, ffn_conv_b, w_down, ln2_g, ln2_b, loss_target, m_w_in, m_lru_conv_w, m_lru_conv_b, m_lru_wr,
        m_lru_br, m_lru_wi, m_lru_bi, m_lru_lambda, m_s5_a_re, m_s5_a_im, m_s5_b_re, m_s5_b_im, m_s5_c_re,
        m_s5_c_im, m_s5_d, m_s5_log_step, m_s5_w_glu, m_s5_b_glu, m_mix_norm_g, m_w_out, m_ln1_g, m_ln1_b,
        m_w_up, m_ffn_conv_w, m_ffn_conv_b, m_w_down, m_ln2_g, m_ln2_b, v_w_in, v_lru_conv_w, v_lru_conv_b,
        v_lru_wr, v_lru_br, v_lru_wi, v_lru_bi, v_lru_lambda, v_s5_a_re, v_s5_a_im, v_s5_b_re, v_s5_b_im,
        v_s5_c_re, v_s5_c_im, v_s5_d, v_s5_log_step, v_s5_w_glu, v_s5_b_glu, v_mix_norm_g, v_w_out, v_ln1_g,
        v_ln1_b, v_w_up, v_ffn_conv_w, v_ffn_conv_b, v_w_down, v_ln2_g, v_ln2_b
):
    return _train_step(dict(locals()))
```

```python
import functools
import math

import jax
import jax.numpy as jnp
from jax import lax
from jax.experimental import pallas as pl
from jax.experimental.pallas import tpu as pltpu

f32 = jnp.float32
bf16 = jnp.bfloat16
MESH = pl.DeviceIdType.MESH

D_MODEL = 1024
ATTN_WIDTH = 384
LRU_WIDTH = 384
S5_WIDTH = 256
HEAD_DIM = 64
N_LRU_HEADS = 6
N_S5_GROUPS = 16
S5_GROUP = 16
S5_STATE = 64
S5_LANES = N_S5_GROUPS * S5_STATE
D_FF = 3072
D_IN = 2176
LRU_C = 8.0
ROPE_THETA = 10000.0
DILATIONS = (1, 4, 16)
ATTN_BLOCK = 128
DEPTH = 2
ALPHA = (2 * DEPTH) ** 0.25
LN_EPS = 1e-5
RMS_EPS = 1e-6
ADAM_LR, ADAM_B1, ADAM_B2, ADAM_EPS, ADAM_WD, ADAM_STEP = 0.001, 0.9, 0.999, 1e-08, 0.01, 10

SUBLANES = 8
LANES = 128
VMEM_LIMIT = 56 * 1024 * 1024
ROW_TILE = 512
MM_SINGLE_K = 3072
D_IN_PAD = 2304
NEG = -1e30


def _cp(*sem):
    return pltpu.CompilerParams(dimension_semantics=sem if sem else None, vmem_limit_bytes=VMEM_LIMIT)


def _pick(dim, pref, align=LANES):
    if dim <= pref:
        return dim
    t = (pref // align) * align
    while t >= align:
        if dim % t == 0:
            return t
        t -= align
    return dim


def _gelu(x):
    return jax.nn.gelu(x)


def _gelu_grad(x):
    c = math.sqrt(2.0 / math.pi)
    t = jnp.tanh(c * (x + 0.044715 * x * x * x))
    return 0.5 * (1.0 + t) + 0.5 * x * (1.0 - t * t) * c * (1.0 + 3 * 0.044715 * x * x)


def _gelu_pair(x):
    c = math.sqrt(2.0 / math.pi)
    x2 = x * x
    t = jnp.tanh(c * x * (1.0 + 0.044715 * x2))
    return 0.5 * x * (1.0 + t), 0.5 * (1.0 + t) + 0.5 * x * (1.0 - t * t) * c * (1.0 + 3 * 0.044715 * x2)


def _sigmoid(x):
    return jax.nn.sigmoid(x)


def _expm1(x):
    p = 1.0 + x / 9.0
    for n in (8.0, 7.0, 6.0, 5.0, 4.0, 3.0, 2.0):
        p = 1.0 + (x / n) * p
    return jnp.where(jnp.abs(x) < 0.3, x * p, jnp.exp(x) - 1.0)


def _dot(a, b, dims):
    return lax.dot_general(a, b, (dims, ((), ())), preferred_element_type=f32)


NN = ((1,), (0,))
NT = ((1,), (1,))
TN = ((0,), (0,))


def _mm(a, b, mode, name, out_dtype=f32, tm=1024, tn=1024, tk=1024, add=None, a_win=None):
    if mode == "nn":
        (M, K), N = a.shape, b.shape[1]
    elif mode == "nt":
        (M, K), N = a.shape, b.shape[0]
    else:
        (K, M), N = a.shape, b.shape[1]
    win = 0
    if a_win is not None:
        win, w = a_win
        if mode == "tn":
            M, tm = w, w
        else:
            K = w
    single = mode != "tn" and K <= MM_SINGLE_K
    tm, tn = _pick(M, tm), _pick(N, tn)
    tk = K if single else _pick(K, tk)
    nk = K // tk
    dims = {"nn": NN, "nt": NT, "tn": TN}[mode]

    def body(a_ref, b_ref, *rest):
        prod = _dot(a_ref[...].astype(bf16), b_ref[...].astype(bf16), dims)
        if single:
            o_ref = rest[-1]
            o_ref[...] = (prod if add is None else prod + rest[0][...]).astype(o_ref.dtype)
            return
        o_ref, acc = rest[-2:]
        k = pl.program_id(2)

        @pl.when(k == 0)
        def _():
            acc[...] = prod if add is None else prod + rest[0][...]

        @pl.when(k > 0)
        def _():
            acc[...] += prod

        @pl.when(k == nk - 1)
        def _():
            o_ref[...] = acc[...].astype(o_ref.dtype)

    if mode == "tn":
        a_spec = pl.BlockSpec((tk, tm), lambda i, j, k: (k, i + win))
    else:
        a_spec = pl.BlockSpec((tm, tk), lambda i, j, k: (i, k + win))
    if mode == "nt":
        b_spec = pl.BlockSpec((tn, tk), lambda i, j, k: (j, k))
    else:
        b_spec = pl.BlockSpec((tk, tn), lambda i, j, k: (k, j))
    o_spec = pl.BlockSpec((tm, tn), lambda i, j, k: (i, j))
    return pl.pallas_call(
        body, name=name, grid=(M // tm, N // tn, nk),
        in_specs=[a_spec, b_spec] + ([] if add is None else [o_spec]), out_specs=o_spec,
        out_shape=jax.ShapeDtypeStruct((M, N), out_dtype),
        scratch_shapes=[] if single else [pltpu.VMEM((tm, tn), f32)],
        compiler_params=_cp("parallel", "parallel", "arbitrary"),
    )(*((a, b) if add is None else (a, b, add)))


def _shift_down(cur, prev8, k):
    if k == 0:
        return cur
    T, (R, C) = SUBLANES, cur.shape
    rot = pltpu.roll(cur.reshape(R // T, T, C), k, 1)
    before = jnp.concatenate([pltpu.roll(prev8, k, 0)[None], rot[:-1]], axis=0)
    row = lax.broadcasted_iota(jnp.int32, (R // T, T, C), 1)
    return jnp.where(row < k, before, rot).reshape(R, C)


def _shift_up(cur, next8, k):
    if k == 0:
        return cur
    T, (R, C) = SUBLANES, cur.shape
    rot = pltpu.roll(cur.reshape(R // T, T, C), T - k, 1)
    after = jnp.concatenate([rot[1:], pltpu.roll(next8, T - k, 0)[None]], axis=0)
    row = lax.broadcasted_iota(jnp.int32, (R // T, T, C), 1)
    return jnp.where(row < T - k, rot, after).reshape(R, C)


def _prev_halo_spec(rt, cols, ncolblk_fn):
    per = rt // SUBLANES
    return pl.BlockSpec((SUBLANES, cols), lambda *g: (jnp.maximum(g[-1] * per - 1, 0), ncolblk_fn(*g)))


def _ln_fwd(h, branch, g, b, name):
    S, D = h.shape
    rt = _pick(S, ROW_TILE, SUBLANES)

    def body(h_ref, m_ref, g_ref, b_ref, o_ref, z_ref):
        z = ALPHA * h_ref[...] + m_ref[...]
        mu = jnp.mean(z, axis=-1, keepdims=True)
        zc = z - mu
        var = jnp.mean(zc * zc, axis=-1, keepdims=True)
        o_ref[...] = zc * lax.rsqrt(var + LN_EPS) * g_ref[...] + b_ref[...]
        z_ref[...] = z

    row = pl.BlockSpec((rt, D), lambda i: (i, 0))
    vec = pl.BlockSpec((1, D), lambda i: (0, 0))
    return pl.pallas_call(
        body, name=name, grid=(S // rt,), in_specs=[row, row, vec, vec], out_specs=[row, row],
        out_shape=[jax.ShapeDtypeStruct((S, D), f32)] * 2, compiler_params=_cp("parallel"),
    )(h, branch, g.reshape(1, D), b.reshape(1, D))


def _ln_bwd(dy_a, dy_b, z, g, name):
    S, D = z.shape
    rt = _pick(S, ROW_TILE, SUBLANES)
    two = dy_a is not None

    def body(*refs):
        if two:
            a_ref, b_ref, z_ref, g_ref, dz_ref, acc_ref = refs
            dy = ALPHA * a_ref[...] + b_ref[...]
        else:
            b_ref, z_ref, g_ref, dz_ref, acc_ref = refs
            dy = b_ref[...]
        z = z_ref[...]
        mu = jnp.mean(z, axis=-1, keepdims=True)
        zc = z - mu
        var = jnp.mean(zc * zc, axis=-1, keepdims=True)
        rstd = lax.rsqrt(var + LN_EPS)
        xhat = zc * rstd
        dxh = dy * g_ref[...]
        m1 = jnp.mean(dxh, axis=-1, keepdims=True)
        m2 = jnp.mean(dxh * xhat, axis=-1, keepdims=True)
        dz_ref[...] = rstd * (dxh - m1 - xhat * m2)

        @pl.when(pl.program_id(0) == 0)
        def _():
            acc_ref[...] = jnp.zeros_like(acc_ref)

        acc_ref[0:1, :] += jnp.sum(dy * xhat, axis=0, keepdims=True)
        acc_ref[1:2, :] += jnp.sum(dy, axis=0, keepdims=True)

    row = pl.BlockSpec((rt, D), lambda i: (i, 0))
    vec = pl.BlockSpec((1, D), lambda i: (0, 0))
    acc = pl.BlockSpec((SUBLANES, D), lambda i: (0, 0))
    ins = ([dy_a] if two else []) + [dy_b, z, g.reshape(1, D)]
    return pl.pallas_call(
        body, name=name, grid=(S // rt,), in_specs=[row] * (len(ins) - 1) + [vec], out_specs=[row, acc],
        out_shape=[jax.ShapeDtypeStruct((S, D), f32), jax.ShapeDtypeStruct((SUBLANES, D), f32)],
        compiler_params=_cp("arbitrary"),
    )(*ins)


def _loss_head(y, target):
    S, D = y.shape
    rt = _pick(S, ROW_TILE, SUBLANES)

    def body(y_ref, t_ref, dy_ref, acc_ref):
        e = y_ref[...] - t_ref[...]
        dy_ref[...] = e * (1.0 / D)

        @pl.when(pl.program_id(0) == 0)
        def _():
            acc_ref[...] = jnp.zeros_like(acc_ref)

        part = jnp.sum(jnp.mean(e * e, axis=-1, keepdims=True), axis=0, keepdims=True)
        acc_ref[...] += 0.5 * part

    row = pl.BlockSpec((rt, D), lambda i: (i, 0))
    return pl.pallas_call(
        body, name="loss_head", grid=(S // rt,), in_specs=[row, row],
        out_specs=[row, pl.BlockSpec((1, 1), lambda i: (0, 0))],
        out_shape=[jax.ShapeDtypeStruct((S, D), f32), jax.ShapeDtypeStruct((1, 1), f32)],
        compiler_params=_cp("arbitrary"),
    )(y, target)


def _axpy(a, b, name):
    S, D = a.shape
    rt = _pick(S, ROW_TILE, SUBLANES)

    def body(a_ref, b_ref, o_ref):
        o_ref[...] = ALPHA * a_ref[...] + b_ref[...]

    row = pl.BlockSpec((rt, D), lambda i: (i, 0))
    return pl.pallas_call(
        body, name=name, grid=(S // rt,), in_specs=[row, row], out_specs=row,
        out_shape=jax.ShapeDtypeStruct((S, D), f32), compiler_params=_cp("parallel"),
    )(a, b)


def _rope_tables(S):
    rt = _pick(S, ROW_TILE, SUBLANES)

    def body(c_ref, s_ref):
        pos = (pl.program_id(0) * rt + lax.broadcasted_iota(jnp.int32, (rt, LANES), 0)).astype(f32)
        lane = lax.broadcasted_iota(jnp.int32, (rt, LANES), 1)
        j = (lane % (HEAD_DIM // 2)).astype(f32)
        inv = jnp.exp((-j * 2.0 / HEAD_DIM) * math.log(ROPE_THETA))
        ang = pos * inv
        c = jnp.cos(ang)
        s = jnp.where(lane % HEAD_DIM < HEAD_DIM // 2, -jnp.sin(ang), jnp.sin(ang))
        c_ref[...] = jnp.concatenate([c, c, c], axis=1)
        s_ref[...] = jnp.concatenate([s, s, s], axis=1)

    row = pl.BlockSpec((rt, ATTN_WIDTH), lambda i: (i, 0))
    return pl.pallas_call(
        body, name="rope_tables", grid=(S // rt,), in_specs=[], out_specs=[row, row],
        out_shape=[jax.ShapeDtypeStruct((S, ATTN_WIDTH), f32)] * 2, compiler_params=_cp("parallel"),
    )()


def _swap_halves(x):
    lane = lax.broadcasted_iota(jnp.int32, x.shape, 1)
    half = HEAD_DIM // 2
    return jnp.where(lane % HEAD_DIM < half, pltpu.roll(x, x.shape[1] - half, 1), pltpu.roll(x, half, 1))


def _rope_fwd(proj, cos, sin):
    S, W = proj.shape[0], ATTN_WIDTH
    rt = _pick(S, ROW_TILE, SUBLANES)

    def body(q_ref, k_ref, c_ref, s_ref, qo_ref, ko_ref):
        c, s = c_ref[...], s_ref[...]
        qo_ref[...] = q_ref[...] * c + _swap_halves(q_ref[...]) * s
        ko_ref[...] = k_ref[...] * c + _swap_halves(k_ref[...]) * s

    row = pl.BlockSpec((rt, W), lambda i: (i, 0))
    return pl.pallas_call(
        body, name="rope_fwd", grid=(S // rt,), in_specs=[row, pl.BlockSpec((rt, W), lambda i: (i, 1)), row, row],
        out_specs=[row, row], out_shape=[jax.ShapeDtypeStruct((S, W), f32)] * 2, compiler_params=_cp("parallel"),
    )(proj, proj, cos, sin)


def _rope_bwd(dq, dk, cos, sin):
    S, W = dq.shape
    rt = _pick(S, ROW_TILE, SUBLANES)

    def body(q_ref, k_ref, c_ref, s_ref, qo_ref, ko_ref):
        c, s = c_ref[...], s_ref[...]
        qo_ref[...] = q_ref[...] * c + _swap_halves(q_ref[...] * s)
        ko_ref[...] = k_ref[...] * c + _swap_halves(k_ref[...] * s)

    row = pl.BlockSpec((rt, W), lambda i: (i, 0))
    return pl.pallas_call(
        body, name="rope_bwd", grid=(S // rt,), in_specs=[row] * 4, out_specs=[row] * 2,
        out_shape=[jax.ShapeDtypeStruct((S, W), f32)] * 2, compiler_params=_cp("parallel"),
    )(dq, dk, cos, sin)


def _rows(ref, start, d):
    if d == 1:
        return ref[pl.ds(pl.multiple_of(start, ATTN_BLOCK), ATTN_BLOCK), :]
    return ref[pl.ds(start, ATTN_BLOCK, stride=d), :]


def _set_rows(ref, start, d, val):
    if d == 1:
        ref[pl.ds(pl.multiple_of(start, ATTN_BLOCK), ATTN_BLOCK), :] = val
    else:
        ref[pl.ds(start, ATTN_BLOCK, stride=d), :] = val


def _pair_spec(S, first_block):
    return pl.BlockSpec((S, LANES), lambda p: (0, p + first_block))


def _attn_fwd2(qr, kr, proj, shards=(), split=()):
    S = qr.shape[0]
    B = ATTN_BLOCK
    nb = S // B
    scale = HEAD_DIM ** -0.5

    gather = _Gather(shards, split)
    nt = gather.nt

    def body(*refs):
        q_ref, k_ref, v_ref = refs[:3]
        g_ins = refs[3:3 + nt]
        o_ref, l_ref = refs[3 + nt:5 + nt]
        g_outs = refs[5 + nt:5 + 2 * nt]
        m_s, l_s = refs[5 + 2 * nt:7 + 2 * nt]
        g_sems = refs[7 + 2 * nt:]
        if nt:
            @pl.when(pl.program_id(0) == 0)
            def _():
                gather.start(g_ins, g_outs, g_sems)

        qi = lax.broadcasted_iota(jnp.int32, (B, 2 * B), 0)
        ki = lax.broadcasted_iota(jnp.int32, (B, 2 * B), 1)
        dist = qi + B - ki
        band = (dist >= 0) & (dist <= B)
        for bi, d in enumerate(DILATIONS):
            bpc = nb // d

            def blk(b, carry, bi=bi, d=d, bpc=bpc):
                c, n = b // bpc, b % bpc
                start = c + d * B * n
                pstart = c + d * B * jnp.maximum(n - 1, 0)
                valid = band & ((ki >= B) | (n > 0))
                q = _rows(q_ref, start, d).astype(bf16)
                kcat = jnp.concatenate([_rows(k_ref, pstart, d), _rows(k_ref, start, d)], axis=0).astype(bf16)
                vcat = jnp.concatenate([_rows(v_ref, pstart, d), _rows(v_ref, start, d)], axis=0).astype(bf16)
                if bi > 0:
                    m_old, l_old, a_old = _rows(m_s, start, d), _rows(l_s, start, d), _rows(o_ref, start, d)
                ms, ls, accs = [], [], []
                for h in range(2):
                    sl = slice(h * HEAD_DIM, (h + 1) * HEAD_DIM)
                    c0 = h * HEAD_DIM
                    s = jnp.where(valid, _dot(q[:, sl], kcat[:, sl], NT) * scale, NEG)
                    m = jnp.max(s, axis=1, keepdims=True)
                    if bi > 0:
                        mo = m_old[:, c0:c0 + 1]
                        m = jnp.maximum(m, mo)
                        alpha = jnp.exp(mo - m)
                    p = jnp.exp(s - m)
                    l = jnp.sum(p, axis=1, keepdims=True)
                    acc = _dot(p.astype(bf16), vcat[:, sl], NN)
                    if bi > 0:
                        l = l + alpha * l_old[:, c0:c0 + 1]
                        acc = acc + alpha * a_old[:, sl]
                    ms.append(jnp.broadcast_to(m, (B, HEAD_DIM)))
                    ls.append(jnp.broadcast_to(l, (B, HEAD_DIM)))
                    accs.append(acc)
                _set_rows(m_s, start, d, jnp.concatenate(ms, axis=1))
                _set_rows(l_s, start, d, jnp.concatenate(ls, axis=1))
                _set_rows(o_ref, start, d, jnp.concatenate(accs, axis=1))
                return carry

            lax.fori_loop(0, nb, blk, 0, unroll=4)

        def fin(t, carry):
            rows = pl.ds(pl.multiple_of(t * B, B), B)
            l = l_s[rows, :]
            o_ref[rows, :] = o_ref[rows, :] / l
            l_ref[rows, :] = m_s[rows, :] + jnp.log(l)
            return carry

        lax.fori_loop(0, nb, fin, 0)
        if nt:
            @pl.when(pl.program_id(0) == pl.num_programs(0) - 1)
            def _():
                gather.finish(g_ins, g_outs, g_sems)

    pair = _pair_spec(S, 0)
    res = pl.pallas_call(
        body, name="attn_fwd_gather" if nt else "attn_fwd", grid=(3,),
        in_specs=[pair, pair, _pair_spec(S, 2 * ATTN_WIDTH // LANES)] + gather.in_specs,
        out_specs=[pair, pair] + gather.out_specs,
        out_shape=[jax.ShapeDtypeStruct((S, ATTN_WIDTH), f32)] * 2 + gather.out_shape,
        scratch_shapes=[pltpu.VMEM((S, LANES), f32)] * 2 + gather.scratch,
        compiler_params=_cp("arbitrary"),
    )(qr, kr, proj, *shards)
    return res[0], res[1], list(res[2:])


def _attn_bwd2(qr, kr, proj, dattn, ltot, delta, slabs=(), only_c=0):
    S = qr.shape[0]
    B = ATTN_BLOCK
    nb = S // B
    scale = HEAD_DIM ** -0.5
    ex = _ChipExchange(slabs, (), only_c)
    n = ex.n

    def body(*refs):
        q_ref, k_ref, v_ref, do_ref, l_ref, d_ref = refs[:6]
        x_ins = refs[6:6 + n]
        dq_ref, dk_ref, dv_ref = refs[6 + n:9 + n]
        x_outs = refs[9 + n:9 + 2 * n]
        x_sems = refs[9 + 2 * n:]
        if n:
            @pl.when(pl.program_id(0) == 0)
            def _():
                ex.start(x_ins, x_outs, x_sems)

        qi = lax.broadcasted_iota(jnp.int32, (B, 2 * B), 0)
        ki = lax.broadcasted_iota(jnp.int32, (B, 2 * B), 1)
        dist1 = qi + B - ki
        band1 = (dist1 >= 0) & (dist1 <= B)
        ri = lax.broadcasted_iota(jnp.int32, (2 * B, B), 0)
        ci = lax.broadcasted_iota(jnp.int32, (2 * B, B), 1)
        dist2 = ri - ci
        band2 = (dist2 >= 0) & (dist2 <= B)
        for bi, d in enumerate(DILATIONS):
            bpc = nb // d

            def blk(b, carry, bi=bi, d=d, bpc=bpc):
                c, n = b // bpc, b % bpc
                start = c + d * B * n
                pstart = c + d * B * jnp.maximum(n - 1, 0)
                nstart = c + d * B * jnp.minimum(n + 1, bpc - 1)
                valid1 = band1 & ((ki >= B) | (n > 0))
                valid2 = band2 & ((ri < B) | (n + 1 < bpc))
                q_c, q_n = _rows(q_ref, start, d), _rows(q_ref, nstart, d)
                k_p, k_c = _rows(k_ref, pstart, d), _rows(k_ref, start, d)
                v_p, v_c = _rows(v_ref, pstart, d), _rows(v_ref, start, d)
                do_c, do_n = _rows(do_ref, start, d), _rows(do_ref, nstart, d)
                l_c, l_n = _rows(l_ref, start, d), _rows(l_ref, nstart, d)
                d_c, d_n = _rows(d_ref, start, d), _rows(d_ref, nstart, d)
                qc = q_c.astype(bf16)
                qcat = jnp.concatenate([q_c, q_n], axis=0).astype(bf16)
                kc = k_c.astype(bf16)
                kcat = jnp.concatenate([k_p, k_c], axis=0).astype(bf16)
                vc = v_c.astype(bf16)
                vcat = jnp.concatenate([v_p, v_c], axis=0).astype(bf16)
                doc = do_c.astype(bf16)
                docat = jnp.concatenate([do_c, do_n], axis=0).astype(bf16)
                lcat = jnp.concatenate([l_c, l_n], axis=0)
                dcat = jnp.concatenate([d_c, d_n], axis=0)
                dqs, dks, dvs = [], [], []
                for h in range(2):
                    sl = slice(h * HEAD_DIM, (h + 1) * HEAD_DIM)
                    c0 = h * HEAD_DIM
                    s1 = _dot(qc[:, sl], kcat[:, sl], NT) * scale
                    p1 = jnp.where(valid1, jnp.exp(s1 - l_c[:, c0:c0 + 1]), 0.0)
                    dp1 = _dot(doc[:, sl], vcat[:, sl], NT)
                    ds1 = p1 * (dp1 - d_c[:, c0:c0 + 1]) * scale
                    dqs.append(_dot(ds1.astype(bf16), kcat[:, sl], NN))
                    s2 = _dot(qcat[:, sl], kc[:, sl], NT) * scale
                    p2 = jnp.where(valid2, jnp.exp(s2 - lcat[:, c0:c0 + 1]), 0.0)
                    dvs.append(_dot(p2.astype(bf16), docat[:, sl], TN))
                    dp2 = _dot(docat[:, sl], vc[:, sl], NT)
                    ds2 = p2 * (dp2 - dcat[:, c0:c0 + 1]) * scale
                    dks.append(_dot(ds2.astype(bf16), qcat[:, sl], TN))
                for ref, parts in ((dq_ref, dqs), (dk_ref, dks), (dv_ref, dvs)):
                    new = jnp.concatenate(parts, axis=1)
                    if bi > 0:
                        new = new + _rows(ref, start, d)
                    _set_rows(ref, start, d, new)
                return carry

            lax.fori_loop(0, nb, blk, 0, unroll=4)

        if n:
            @pl.when(pl.program_id(0) == pl.num_programs(0) - 1)
            def _():
                ex.finish(x_ins, x_outs, x_sems)

    pair = _pair_spec(S, 0)
    res = pl.pallas_call(
        body, name="attn_bwd_exchange" if n else "attn_bwd", grid=(3,),
        in_specs=[pair, pair, _pair_spec(S, 2 * ATTN_WIDTH // LANES), pair, pair, pair] + ex.in_specs,
        out_specs=[pair] * 3 + ex.out_specs,
        out_shape=[jax.ShapeDtypeStruct((S, ATTN_WIDTH), f32)] * 3 + ex.out_shape,
        scratch_shapes=ex.scratch, compiler_params=_cp("arbitrary"),
    )(qr, kr, proj, dattn, ltot, delta, *slabs)
    return res[0], res[1], res[2], list(res[3:])


def _softplus_neg(lam):
    return jnp.maximum(-lam, 0.0) + jnp.log1p(jnp.exp(-jnp.abs(lam)))


PROJ_LRU_X, PROJ_LRU_GATE, PROJ_S5_U = 3, 4, 5


def _lru_pre(proj, conv_w, conv_b, wr, br, wi, bi, lam):
    S, W = proj.shape[0], LRU_WIDTH
    rt = _pick(S, ROW_TILE, SUBLANES)
    K = conv_w.shape[0]

    def body(x_ref, xp_ref, cw_ref, cb_ref, wr_ref, br_ref, wi_ref, bi_ref, lam_ref,
             xc_ref, r_ref, i_ref, la_ref, u_ref):
        prev = jnp.where(pl.program_id(0) == 0, 0.0, xp_ref[...])
        x = x_ref[...]
        xc = cb_ref[...] + cw_ref[K - 1:K, :] * x
        for k in range(K - 1):
            xc = xc + cw_ref[k:k + 1, :] * _shift_down(x, prev, K - 1 - k)
        xb = xc.astype(bf16)
        r = _sigmoid(_dot(xb, wr_ref[...], NN) + br_ref[...])
        i = _sigmoid(_dot(xb, wi_ref[...], NN) + bi_ref[...])
        log_a = -LRU_C * r * _softplus_neg(lam_ref[...])
        u = jnp.sqrt(-_expm1(2.0 * log_a)) * (i * xc)
        xc_ref[...], r_ref[...], i_ref[...], la_ref[...], u_ref[...] = xc, r, i, log_a, u

    row = pl.BlockSpec((rt, W), lambda i: (i, 0))
    xrow = pl.BlockSpec((rt, W), lambda i: (i, PROJ_LRU_X))
    halo = _prev_halo_spec(rt, W, lambda i: PROJ_LRU_X)
    vec = pl.BlockSpec((1, W), lambda i: (0, 0))
    return pl.pallas_call(
        body, name="lru_pre", grid=(S // rt,),
        in_specs=[xrow, halo, pl.BlockSpec((K, W), lambda i: (0, 0)), vec,
                  pl.BlockSpec((W, W), lambda i: (0, 0)), vec, pl.BlockSpec((W, W), lambda i: (0, 0)), vec, vec],
        out_specs=[row] * 5, out_shape=[jax.ShapeDtypeStruct((S, W), f32)] * 5, compiler_params=_cp("parallel"),
    )(proj, proj, conv_w, conv_b.reshape(1, W), wr, br.reshape(1, W), wi, bi.reshape(1, W), lam.reshape(1, W))


def _tile_rows(shape):
    return lax.broadcasted_iota(jnp.int32, shape, 0)


def _lru_scan(log_a, u, proj):
    S, W = u.shape
    rt = _pick(S, ROW_TILE, SUBLANES)
    T = SUBLANES

    def body(la_ref, u_ref, g_ref, h_ref, o_ref, carry):
        @pl.when(pl.program_id(0) == 0)
        def _():
            carry[...] = jnp.zeros_like(carry)

        row = _tile_rows((T, W))

        def step(t, hp):
            r0 = pl.multiple_of(t * T, T)
            a = jnp.exp(la_ref[pl.ds(r0, T), :])
            x = u_ref[pl.ds(r0, T), :]
            for k in (1, 2, 4):
                x = x + a * jnp.where(row >= k, pltpu.roll(x, k, 0), 0.0)
                a = a * jnp.where(row >= k, pltpu.roll(a, k, 0), 1.0)
            h = x + a * hp
            h_ref[pl.ds(r0, T), :] = h
            o_ref[pl.ds(r0, T), :] = h * _gelu(g_ref[pl.ds(r0, T), :])
            return h[T - 1:T, :]

        carry[0:1, :] = lax.fori_loop(0, rt // T, step, carry[0:1, :])

    row = pl.BlockSpec((rt, W), lambda i: (i, 0))
    grow = pl.BlockSpec((rt, W), lambda i: (i, PROJ_LRU_GATE))
    return pl.pallas_call(
        body, name="lru_scan", grid=(S // rt,), in_specs=[row, row, grow], out_specs=[row] * 2,
        out_shape=[jax.ShapeDtypeStruct((S, W), f32)] * 2, scratch_shapes=[pltpu.VMEM((T, W), f32)],
        compiler_params=_cp("arbitrary"),
    )(log_a, u, proj)


def _lru_scan_bwd(dlru, proj, h, log_a):
    S, W = h.shape
    rt = _pick(S, ROW_TILE, SUBLANES)
    T = SUBLANES
    nblk = S // rt

    def body(d_ref, g_ref, h_ref, la_ref, go_ref, dg_ref, carry):
        @pl.when(pl.program_id(0) == 0)
        def _():
            carry[...] = jnp.zeros_like(carry)

        row = _tile_rows((T, W))

        def step(j, c):
            gn, an = c
            t = rt // T - 1 - j
            r0 = pl.multiple_of(t * T, T)
            d = d_ref[pl.ds(r0, T), :]
            gate = g_ref[pl.ds(r0, T), :]
            a = jnp.exp(la_ref[pl.ds(r0, T), :])
            dg_ref[pl.ds(r0, T), :] = d * h_ref[pl.ds(r0, T), :] * _gelu_grad(gate)
            x = d * _gelu(gate)
            b = jnp.where(row < T - 1, pltpu.roll(a, T - 1, 0), an)
            for k in (1, 2, 4):
                x = x + b * jnp.where(row < T - k, pltpu.roll(x, T - k, 0), 0.0)
                b = b * jnp.where(row < T - k, pltpu.roll(b, T - k, 0), 1.0)
            g = x + b * gn
            go_ref[pl.ds(r0, T), :] = g
            return g[0:1, :], a[0:1, :]

        gn, an = lax.fori_loop(0, rt // T, step, (carry[0:1, :], carry[1:2, :]))
        carry[0:1, :] = gn
        carry[1:2, :] = an

    row = pl.BlockSpec((rt, W), lambda i: (nblk - 1 - i, 0))
    grow = pl.BlockSpec((rt, W), lambda i: (nblk - 1 - i, PROJ_LRU_GATE))
    return pl.pallas_call(
        body, name="lru_scan_bwd", grid=(nblk,), in_specs=[row, grow, row, row], out_specs=[row] * 2,
        out_shape=[jax.ShapeDtypeStruct((S, W), f32)] * 2, scratch_shapes=[pltpu.VMEM((T, W), f32)],
        compiler_params=_cp("arbitrary"),
    )(dlru, proj, h, log_a)


def _lru_gate_bwd(g, h, xc, r, i, log_a, wr, wi, lam):
    S, W = g.shape
    rt = _pick(S, ROW_TILE, SUBLANES)

    def body(g_ref, h_ref, hp_ref, xc_ref, r_ref, i_ref, la_ref, wr_ref, wi_ref, lam_ref,
             dxc_ref, dwr_ref, dwi_ref, acc_ref):
        @pl.when(pl.program_id(0) == 0)
        def _():
            dwr_ref[...] = jnp.zeros_like(dwr_ref)
            dwi_ref[...] = jnp.zeros_like(dwi_ref)
            acc_ref[...] = jnp.zeros_like(acc_ref)

        prev = jnp.where(pl.program_id(0) == 0, 0.0, hp_ref[...])
        gg, xc, r, i, log_a, lam = g_ref[...], xc_ref[...], r_ref[...], i_ref[...], la_ref[...], lam_ref[...]
        hm1 = _shift_down(h_ref[...], prev, 1)
        a = jnp.exp(log_a)
        s = jnp.sqrt(-_expm1(2.0 * log_a))
        da = gg * hm1
        di = gg * s * xc
        dxc = gg * s * i
        ds = gg * i * xc
        dlog_a = da * a - ds * (a * a / s)
        sp = _softplus_neg(lam)
        dr = dlog_a * (-LRU_C * sp)
        dsp = jnp.sum(dlog_a * (-LRU_C * r), axis=0, keepdims=True)
        dpr = dr * r * (1.0 - r)
        dpi = di * i * (1.0 - i)
        dprb, dpib, xb = dpr.astype(bf16), dpi.astype(bf16), xc.astype(bf16)
        dxc_ref[...] = dxc + _dot(dprb, wr_ref[...], NT) + _dot(dpib, wi_ref[...], NT)
        dwr_ref[...] += _dot(xb, dprb, TN)
        dwi_ref[...] += _dot(xb, dpib, TN)
        acc_ref[0:1, :] += jnp.sum(dpr, axis=0, keepdims=True)
        acc_ref[1:2, :] += jnp.sum(dpi, axis=0, keepdims=True)
        acc_ref[2:3, :] += dsp * (-_sigmoid(-lam))

    row = pl.BlockSpec((rt, W), lambda i: (i, 0))
    halo = _prev_halo_spec(rt, W, lambda i: 0)
    vec = pl.BlockSpec((1, W), lambda i: (0, 0))
    mat = pl.BlockSpec((W, W), lambda i: (0, 0))
    acc = pl.BlockSpec((SUBLANES, W), lambda i: (0, 0))
    return pl.pallas_call(
        body, name="lru_gate_bwd", grid=(S // rt,),
        in_specs=[row, row, halo, row, row, row, row, mat, mat, vec], out_specs=[row, mat, mat, acc],
        out_shape=[jax.ShapeDtypeStruct((S, W), f32), jax.ShapeDtypeStruct((W, W), f32),
                   jax.ShapeDtypeStruct((W, W), f32), jax.ShapeDtypeStruct((SUBLANES, W), f32)],
        compiler_params=_cp("arbitrary"),
    )(g, h, h, xc, r, i, log_a, wr, wi, lam.reshape(1, W))


def _conv_bwd(dy, x, conv_w, name, col_tile=None, out_dtype=f32, x_col_block=0, send=(), send_dst=0):
    if dy.ndim == 2:
        dy = dy[None]
    H, S, Ch = dy.shape
    C = H * Ch
    K = conv_w.shape[0]
    ct = Ch if col_tile is None else col_tile
    nct = Ch // ct
    rt = _pick(S, ROW_TILE, SUBLANES)
    nrt = S // rt
    snd = _SiblingSend(send, send_dst)
    n = snd.n

    def body(*refs):
        dy_ref, dyn_ref, x_ref, w_ref = refs[:4]
        s_ins = refs[4:4 + n]
        dx_ref, acc_ref = refs[4 + n:6 + n]
        s_outs, s_sems = refs[6 + n:6 + 2 * n], refs[6 + 2 * n:]
        i = pl.program_id(2)
        if n:
            @pl.when((pl.program_id(0) == 0) & (pl.program_id(1) == 0) & (i == 0))
            def _():
                snd.start(s_ins, s_outs, s_sems)

        @pl.when(i == 0)
        def _():
            acc_ref[...] = jnp.zeros_like(acc_ref)

        nxt = jnp.where(i == nrt - 1, 0.0, dyn_ref[...])
        dy, x = dy_ref[...], x_ref[...]
        ahead = [dy] + [_shift_up(dy, nxt, j) for j in range(1, K)]
        dx = w_ref[K - 1:K, :] * dy
        for k in range(K - 1):
            dx = dx + w_ref[k:k + 1, :] * ahead[K - 1 - k]
        dx_ref[...] = dx.astype(dx_ref.dtype)
        for k in range(K):
            acc_ref[k:k + 1, :] += jnp.sum(ahead[K - 1 - k] * x, axis=0, keepdims=True)
        acc_ref[K:K + 1, :] += jnp.sum(dy, axis=0, keepdims=True)
        if n:
            @pl.when((pl.program_id(0) == H - 1) & (pl.program_id(1) == nct - 1) & (i == nrt - 1))
            def _():
                snd.finish(s_ins, s_outs, s_sems)

    per, last = rt // SUBLANES, S // SUBLANES - 1
    dy_row = pl.BlockSpec((None, rt, ct), lambda h, j, i: (h, i, j))
    dy_next = pl.BlockSpec((None, SUBLANES, ct), lambda h, j, i: (h, jnp.minimum((i + 1) * per, last), j))
    row = pl.BlockSpec((rt, ct), lambda h, j, i: (i, h * nct + j))
    xrow = pl.BlockSpec((rt, ct), lambda h, j, i: (i, h * nct + j + x_col_block))
    res = pl.pallas_call(
        body, name=name, grid=(H, nct, nrt),
        in_specs=[dy_row, dy_next, xrow, pl.BlockSpec((K, ct), lambda h, j, i: (0, h * nct + j))] + snd.in_specs,
        out_specs=[row, pl.BlockSpec((SUBLANES, ct), lambda h, j, i: (0, h * nct + j))] + snd.out_specs,
        out_shape=[jax.ShapeDtypeStruct((S, C), out_dtype), jax.ShapeDtypeStruct((SUBLANES, C), f32)] + snd.out_shape,
        scratch_shapes=snd.scratch,
        compiler_params=_cp(*(("arbitrary",) * 3 if n else ("parallel", "parallel", "arbitrary"))),
    )(dy, dy, x, conv_w, *send)
    return res[0], res[1], list(res[2:])


def _s5_param_fn(a_re, a_im, ls, bt_re, bt_im):
    step = jnp.exp(ls)
    dt_re, dt_im = step * a_re, step * a_im
    mag = jnp.exp(dt_re)
    ab_re, ab_im = mag * jnp.cos(dt_im), mag * jnp.sin(dt_im)
    z_re, z_im = ab_re - 1.0, ab_im
    den = a_re * a_re + a_im * a_im
    f_re = (z_re * a_re + z_im * a_im) / den
    f_im = (z_im * a_re - z_re * a_im) / den
    bb_re = f_re[:, None, :] * bt_re - f_im[:, None, :] * bt_im
    bb_im = f_re[:, None, :] * bt_im + f_im[:, None, :] * bt_re
    return ab_re, ab_im, bb_re, bb_im


def _s5_params(a_re, a_im, ls, bt_re, bt_im):
    def body(ar, ai, l, br, bi, o_ar, o_ai, o_br, o_bi):
        o_ar[...], o_ai[...], o_br[...], o_bi[...] = _s5_param_fn(ar[...], ai[...], l[...], br[...], bi[...])

    return pl.pallas_call(
        body, name="s5_params",
        out_shape=[jax.ShapeDtypeStruct(a_re.shape, f32)] * 2 + [jax.ShapeDtypeStruct(bt_re.shape, f32)] * 2,
        compiler_params=_cp(),
    )(a_re, a_im, ls, bt_re, bt_im)


def _s5_params_bwd(a_re, a_im, ls, bt_re, bt_im, d_ar, d_ai, d_br, d_bi):
    def body(ar, ai, l, br, bi, c_ar, c_ai, c_br, c_bi, g_ar, g_ai, g_l, g_br, g_bi):
        _, vjp = jax.vjp(_s5_param_fn, ar[...], ai[...], l[...], br[...], bi[...])
        g_ar[...], g_ai[...], g_l[...], g_br[...], g_bi[...] = vjp((c_ar[...], c_ai[...], c_br[...], c_bi[...]))

    return pl.pallas_call(
        body, name="s5_params_bwd",
        out_shape=[jax.ShapeDtypeStruct(a_re.shape, f32)] * 2 + [jax.ShapeDtypeStruct(ls.shape, f32)]
        + [jax.ShapeDtypeStruct(bt_re.shape, f32)] * 2,
        compiler_params=_cp(),
    )(a_re, a_im, ls, bt_re, bt_im, d_ar, d_ai, d_br, d_bi)


S5_CHUNK = 256


def _s5_power_tables(ab_ref, p_ref, w_ref, conj):
    T, L = SUBLANES, S5_LANES
    are = ab_ref[0:1, 0:L]
    aim = ab_ref[0:1, L:2 * L]
    if conj:
        aim = -aim
    pre, pim = are, aim
    for n in range(3):
        p_ref[n:n + 1, 0:L] = pre
        p_ref[n:n + 1, L:2 * L] = pim
        pre, pim = pre * pre - pim * pim, 2.0 * pre * pim
    row = _tile_rows((T, L))
    wre = jnp.zeros((T, L), f32)
    wim = jnp.zeros((T, L), f32)
    pre, pim = are, aim
    for n in range(T):
        tgt = (T - 1 - n) if conj else n
        wre = jnp.where(row == tgt, pre, wre)
        wim = jnp.where(row == tgt, pim, wim)
        pre, pim = pre * are - pim * aim, pre * aim + pim * are
    w_ref[:, 0:L] = wre
    w_ref[:, L:2 * L] = wim


def _s5_scan(bu, ab):
    S, L2 = bu.shape
    L = L2 // 2
    rt = _pick(S, 256, SUBLANES)
    T = SUBLANES
    CH = S5_CHUNK

    def body(bu_ref, ab_ref, x_ref, p_ref, w_ref, carry):
        @pl.when(pl.program_id(0) == 0)
        def _():
            carry[...] = jnp.zeros_like(carry)
            _s5_power_tables(ab_ref, p_ref, w_ref, conj=False)

        row = _tile_rows((T, CH))

        def step(t, _):
            r0 = pl.multiple_of(t * T, T)
            for c in range(L // CH):
                lre, lim = pl.ds(c * CH, CH), pl.ds(L + c * CH, CH)
                xr, xi = bu_ref[pl.ds(r0, T), lre], bu_ref[pl.ds(r0, T), lim]
                for n, k in enumerate((1, 2, 4)):
                    pr, pi = p_ref[n:n + 1, lre], p_ref[n:n + 1, lim]
                    sr = jnp.where(row >= k, pltpu.roll(xr, k, 0), 0.0)
                    si = jnp.where(row >= k, pltpu.roll(xi, k, 0), 0.0)
                    xr, xi = xr + pr * sr - pi * si, xi + pr * si + pi * sr
                cr, ci = carry[T - 1:T, lre], carry[T - 1:T, lim]
                wr, wi = w_ref[:, lre], w_ref[:, lim]
                xr, xi = xr + wr * cr - wi * ci, xi + wr * ci + wi * cr
                carry[:, lre] = xr
                carry[:, lim] = xi
                x_ref[pl.ds(r0, T), lre] = xr
                x_ref[pl.ds(r0, T), lim] = xi
            return 0

        lax.fori_loop(0, rt // T, step, 0)

    row_spec = pl.BlockSpec((rt, L2), lambda i: (i, 0))
    return pl.pallas_call(
        body, name="s5_scan", grid=(S // rt,), in_specs=[row_spec, pl.BlockSpec((1, L2), lambda i: (0, 0))],
        out_specs=row_spec, out_shape=jax.ShapeDtypeStruct((S, L2), f32),
        scratch_shapes=[pltpu.VMEM((T, L2), f32), pltpu.VMEM((T, L2), f32), pltpu.VMEM((T, L2), f32)],
        compiler_params=_cp("arbitrary"),
    )(bu, ab)


def _s5_scan_bwd(dx, x, ab):
    S, L2 = dx.shape
    L = L2 // 2
    rt = _pick(S, 256, SUBLANES)
    T = SUBLANES
    CH = S5_CHUNK
    nblk = S // rt
    per = rt // T

    def body(dx_ref, x_ref, xp_ref, ab_ref, g_ref, da_ref, p_ref, w_ref, carry, acc):
        pid = pl.program_id(0)

        @pl.when(pid == 0)
        def _():
            carry[...] = jnp.zeros_like(carry)
            acc[...] = jnp.zeros_like(acc)
            _s5_power_tables(ab_ref, p_ref, w_ref, conj=True)

        row = _tile_rows((T, CH))
        first_block = pid == nblk - 1

        def step(j, _):
            t = per - 1 - j
            r0 = pl.multiple_of(t * T, T)
            rp = pl.multiple_of(jnp.maximum(t - 1, 0) * T, T)
            for c in range(L // CH):
                lre, lim = pl.ds(c * CH, CH), pl.ds(L + c * CH, CH)
                gr, gi = dx_ref[pl.ds(r0, T), lre], dx_ref[pl.ds(r0, T), lim]
                for n, k in enumerate((1, 2, 4)):
                    pr, pi = p_ref[n:n + 1, lre], p_ref[n:n + 1, lim]
                    sr = jnp.where(row < T - k, pltpu.roll(gr, T - k, 0), 0.0)
                    si = jnp.where(row < T - k, pltpu.roll(gi, T - k, 0), 0.0)
                    gr, gi = gr + pr * sr - pi * si, gi + pr * si + pi * sr
                cr, ci = carry[0:1, lre], carry[0:1, lim]
                wr, wi = w_ref[:, lre], w_ref[:, lim]
                gr, gi = gr + wr * cr - wi * ci, gi + wr * ci + wi * cr
                carry[:, lre] = gr
                carry[:, lim] = gi
                g_ref[pl.ds(r0, T), lre] = gr
                g_ref[pl.ds(r0, T), lim] = gi
                xr, xi = x_ref[pl.ds(r0, T), lre], x_ref[pl.ds(r0, T), lim]
                in_blk_r, in_blk_i = x_ref[pl.ds(rp, T), lre], x_ref[pl.ds(rp, T), lim]
                hal_r = jnp.where(first_block, 0.0, xp_ref[:, lre])
                hal_i = jnp.where(first_block, 0.0, xp_ref[:, lim])
                pvr = jnp.where(t == 0, hal_r, in_blk_r)[T - 1:T, :]
                pvi = jnp.where(t == 0, hal_i, in_blk_i)[T - 1:T, :]
                sxr = jnp.where(row >= 1, pltpu.roll(xr, 1, 0), pvr)
                sxi = jnp.where(row >= 1, pltpu.roll(xi, 1, 0), pvi)
                acc[:, lre] += gr * sxr + gi * sxi
                acc[:, lim] += gi * sxr - gr * sxi
            return 0

        lax.fori_loop(0, per, step, 0)

        @pl.when(pid == nblk - 1)
        def _():
            da_ref[...] = jnp.sum(acc[...], axis=0, keepdims=True)

    row_spec = pl.BlockSpec((rt, L2), lambda i: (nblk - 1 - i, 0))
    halo = pl.BlockSpec((T, L2), lambda i: (jnp.maximum((nblk - 1 - i) * per - 1, 0), 0))
    vec = pl.BlockSpec((1, L2), lambda i: (0, 0))
    return pl.pallas_call(
        body, name="s5_scan_bwd", grid=(nblk,), in_specs=[row_spec, row_spec, halo, vec],
        out_specs=[row_spec, vec],
        out_shape=[jax.ShapeDtypeStruct((S, L2), f32), jax.ShapeDtypeStruct((1, L2), f32)],
        scratch_shapes=[pltpu.VMEM((T, L2), f32)] * 4,
        compiler_params=_cp("arbitrary"),
    )(dx, x, x, ab)


def _S5_U_SPEC(rt):
    return pl.BlockSpec((rt, LRU_WIDTH), lambda i: (i, PROJ_S5_U))


def _s5_out(yc, proj, d, wglu, bglu):
    S, W = yc.shape
    rt = _pick(S, ROW_TILE, SUBLANES)

    def body(yc_ref, u_ref, d_ref, w_ref, b_ref, o_ref, y_ref):
        y = yc_ref[...] + d_ref[...] * u_ref[:, 0:W]
        yg = _gelu(y)
        z = _dot(yg.astype(bf16), w_ref[...], NN) + b_ref[...]
        o_ref[...] = yg * _sigmoid(z)
        y_ref[...] = y

    row = pl.BlockSpec((rt, W), lambda i: (i, 0))
    vec = pl.BlockSpec((1, W), lambda i: (0, 0))
    mat = pl.BlockSpec((W, W), lambda i: (0, 0))
    return pl.pallas_call(
        body, name="s5_out", grid=(S // rt,), in_specs=[row, _S5_U_SPEC(rt), vec, mat, vec], out_specs=[row, row],
        out_shape=[jax.ShapeDtypeStruct((S, W), f32)] * 2, compiler_params=_cp("parallel"),
    )(yc, proj, d.reshape(1, W), wglu, bglu.reshape(1, W))


def _s5_out_bwd(dssm, y, proj, d, wglu, bglu):
    S, W = y.shape
    rt = _pick(S, ROW_TILE, SUBLANES)

    def body(do_ref, y_ref, u_ref, d_ref, w_ref, b_ref, dy_ref, du_ref, dw_ref, acc_ref):
        @pl.when(pl.program_id(0) == 0)
        def _():
            dw_ref[...] = jnp.zeros_like(dw_ref)
            acc_ref[...] = jnp.zeros_like(acc_ref)

        do, y = do_ref[...], y_ref[...]
        yg = _gelu(y)
        ygb = yg.astype(bf16)
        sg = _sigmoid(_dot(ygb, w_ref[...], NN) + b_ref[...])
        dz = do * yg * sg * (1.0 - sg)
        dzb = dz.astype(bf16)
        dyg = do * sg + _dot(dzb, w_ref[...], NT)
        dy = dyg * _gelu_grad(y)
        dy_ref[...] = dy
        du_ref[...] = dy * d_ref[...]
        dw_ref[...] += _dot(ygb, dzb, TN)
        acc_ref[0:1, :] += jnp.sum(dz, axis=0, keepdims=True)
        acc_ref[1:2, :] += jnp.sum(dy * u_ref[:, 0:W], axis=0, keepdims=True)

    row = pl.BlockSpec((rt, W), lambda i: (i, 0))
    vec = pl.BlockSpec((1, W), lambda i: (0, 0))
    mat = pl.BlockSpec((W, W), lambda i: (0, 0))
    acc = pl.BlockSpec((SUBLANES, W), lambda i: (0, 0))
    return pl.pallas_call(
        body, name="s5_out_bwd", grid=(S // rt,), in_specs=[row, row, _S5_U_SPEC(rt), vec, mat, vec],
        out_specs=[row, row, mat, acc],
        out_shape=[jax.ShapeDtypeStruct((S, W), f32)] * 2
        + [jax.ShapeDtypeStruct((W, W), f32), jax.ShapeDtypeStruct((SUBLANES, W), f32)],
        compiler_params=_cp("arbitrary"),
    )(dssm, y, proj, d.reshape(1, W), wglu, bglu.reshape(1, W))


MIX_SPLITS = ((0, ATTN_WIDTH), (ATTN_WIDTH, ATTN_WIDTH + LRU_WIDTH), (ATTN_WIDTH + LRU_WIDTH, D_MODEL))


def _mixnorm(attn, lru, ssm, g):
    S = attn.shape[0]
    rt = _pick(S, ROW_TILE, SUBLANES)

    def body(a_ref, l_ref, s_ref, g_ref, o_ref):
        for ref, (lo, hi) in zip((a_ref, l_ref, s_ref), MIX_SPLITS):
            x = ref[...]
            ms = jnp.mean(x * x, axis=-1, keepdims=True)
            o_ref[:, lo:hi] = (x * lax.rsqrt(ms + RMS_EPS) * g_ref[:, lo:hi]).astype(o_ref.dtype)

    rows = [pl.BlockSpec((rt, hi - lo), lambda i: (i, 0)) for lo, hi in MIX_SPLITS]
    return pl.pallas_call(
        body, name="mixnorm", grid=(S // rt,), in_specs=rows + [pl.BlockSpec((1, D_MODEL), lambda i: (0, 0))],
        out_specs=pl.BlockSpec((rt, D_MODEL), lambda i: (i, 0)),
        out_shape=jax.ShapeDtypeStruct((S, D_MODEL), bf16), compiler_params=_cp("parallel"),
    )(attn, lru, ssm, g.reshape(1, D_MODEL))


def _mixnorm_bwd(dmixed, attn, lru, ssm, g):
    S = attn.shape[0]
    rt = _pick(S, ROW_TILE, SUBLANES)

    def body(d_ref, a_ref, l_ref, s_ref, g_ref, da_ref, dl_ref, ds_ref, dlt_ref, acc_ref):
        @pl.when(pl.program_id(0) == 0)
        def _():
            acc_ref[...] = jnp.zeros_like(acc_ref)

        outs = []
        for ref, (lo, hi) in zip((a_ref, l_ref, s_ref), MIX_SPLITS):
            x = ref[...]
            dy = d_ref[:, lo:hi]
            rinv = lax.rsqrt(jnp.mean(x * x, axis=-1, keepdims=True) + RMS_EPS)
            dyg = dy * g_ref[:, lo:hi]
            outs.append(rinv * dyg - x * (rinv * rinv * rinv) * jnp.mean(dyg * x, axis=-1, keepdims=True))
            acc_ref[0:1, lo:hi] += jnp.sum(dy * x * rinv, axis=0, keepdims=True)
        da_ref[...], dl_ref[...], ds_ref[...] = outs
        hi_ = lax.broadcasted_iota(jnp.int32, (ATTN_WIDTH, ATTN_WIDTH), 0) // HEAD_DIM
        hj_ = lax.broadcasted_iota(jnp.int32, (ATTN_WIDTH, ATTN_WIDTH), 1) // HEAD_DIM
        same = jnp.where(hi_ == hj_, 1.0, 0.0).astype(f32)
        dlt_ref[...] = jnp.dot(outs[0] * a_ref[...], same, precision=lax.Precision.HIGHEST, preferred_element_type=f32)

    rows = [pl.BlockSpec((rt, hi - lo), lambda i: (i, 0)) for lo, hi in MIX_SPLITS]
    full = pl.BlockSpec((rt, D_MODEL), lambda i: (i, 0))
    return pl.pallas_call(
        body, name="mixnorm_bwd", grid=(S // rt,),
        in_specs=[full] + rows + [pl.BlockSpec((1, D_MODEL), lambda i: (0, 0))],
        out_specs=rows + [rows[0], pl.BlockSpec((SUBLANES, D_MODEL), lambda i: (0, 0))],
        out_shape=[jax.ShapeDtypeStruct((S, hi - lo), f32) for lo, hi in MIX_SPLITS]
        + [jax.ShapeDtypeStruct((S, ATTN_WIDTH), f32), jax.ShapeDtypeStruct((SUBLANES, D_MODEL), f32)],
        compiler_params=_cp("arbitrary"),
    )(dmixed, attn, lru, ssm, g.reshape(1, D_MODEL))


FFN_COL_TILE = 1024


def _ffn_conv(x, prev, w_ref, b_ref, K):
    y = b_ref[...] + w_ref[K - 1:K, :] * x
    for k in range(K - 1):
        y = y + w_ref[k:k + 1, :] * _shift_down(x, prev, K - 1 - k)
    return y


def _ffn_act(up, conv_w, conv_b):
    S, C2 = up.shape
    C = C2 // 2
    K = conv_w.shape[0]
    ct = FFN_COL_TILE
    nct = C // ct
    rt = _pick(S, ROW_TILE, SUBLANES)

    def body(g_ref, gp_ref, v_ref, vp_ref, wg_ref, wv_ref, bg_ref, bv_ref, o_ref):
        first = pl.program_id(1) == 0
        gate = _ffn_conv(g_ref[...], jnp.where(first, 0.0, gp_ref[...]), wg_ref, bg_ref, K)
        val = _ffn_conv(v_ref[...], jnp.where(first, 0.0, vp_ref[...]), wv_ref, bv_ref, K)
        o_ref[...] = (_gelu(gate) * val).astype(o_ref.dtype)

    def specs(off):
        return (pl.BlockSpec((rt, ct), lambda j, i: (i, j + off)), _prev_halo_spec(rt, ct, lambda j, i: j + off))

    def wspec(off, rows):
        return pl.BlockSpec((rows, ct), lambda j, i: (0, j + off))

    g_s, gp_s = specs(0)
    v_s, vp_s = specs(nct)
    return pl.pallas_call(
        body, name="ffn_act", grid=(nct, S // rt),
        in_specs=[g_s, gp_s, v_s, vp_s, wspec(0, K), wspec(nct, K), wspec(0, 1), wspec(nct, 1)],
        out_specs=pl.BlockSpec((rt, ct), lambda j, i: (i, j)),
        out_shape=jax.ShapeDtypeStruct((S, C), bf16), compiler_params=_cp("parallel", "parallel"),
    )(up, up, up, up, conv_w, conv_w, conv_b.reshape(1, C2), conv_b.reshape(1, C2))


def _ffn_act_bwd(dact, up, conv_w, conv_b):
    S, C2 = up.shape
    C = C2 // 2
    K = conv_w.shape[0]
    ct = FFN_COL_TILE
    nct = C // ct
    rt = _pick(S, ROW_TILE, SUBLANES)

    def body(d_ref, g_ref, gp_ref, v_ref, vp_ref, wg_ref, wv_ref, bg_ref, bv_ref, o_ref):
        first = pl.program_id(1) == 0
        gate = _ffn_conv(g_ref[...], jnp.where(first, 0.0, gp_ref[...]), wg_ref, bg_ref, K)
        val = _ffn_conv(v_ref[...], jnp.where(first, 0.0, vp_ref[...]), wv_ref, bv_ref, K)
        d = d_ref[...]
        gl, dgl = _gelu_pair(gate)
        o_ref[0] = d * val * dgl
        o_ref[1] = d * gl

    def specs(off):
        return (pl.BlockSpec((rt, ct), lambda j, i: (i, j + off)), _prev_halo_spec(rt, ct, lambda j, i: j + off))

    def wspec(off, rows):
        return pl.BlockSpec((rows, ct), lambda j, i: (0, j + off))

    g_s, gp_s = specs(0)
    v_s, vp_s = specs(nct)
    return pl.pallas_call(
        body, name="ffn_act_bwd", grid=(nct, S // rt),
        in_specs=[pl.BlockSpec((rt, ct), lambda j, i: (i, j)), g_s, gp_s, v_s, vp_s,
                  wspec(0, K), wspec(nct, K), wspec(0, 1), wspec(nct, 1)],
        out_specs=pl.BlockSpec((2, rt, ct), lambda j, i: (0, i, j)),
        out_shape=jax.ShapeDtypeStruct((2, S, C), f32), compiler_params=_cp("parallel", "parallel"),
    )(dact, up, up, up, up, conv_w, conv_w, conv_b.reshape(1, C2), conv_b.reshape(1, C2))


ANY = pl.BlockSpec(memory_space=pl.ANY)


def _rows_for(cols):
    return max(16, (1 << 17) // cols)


def _chips(x, y):
    return [(1 - x, y), (x, 1 - y), (1 - x, 1 - y)]


class _Gather:
    def __init__(self, shards, split):
        self.shapes = [s.shape for s in shards]
        self.dtypes = [s.dtype for s in shards]
        self.split = list(split)
        self.nt = len(shards)
        self.in_specs = [ANY] * self.nt
        self.out_specs = [ANY] * self.nt
        self.out_shape = [jax.ShapeDtypeStruct((4,) + s, dt) for s, dt in zip(self.shapes, self.dtypes)]
        self.scratch = [pltpu.SemaphoreType.DMA((3, self.nt))] * 4 if self.nt else []

    def _part(self, ref, t, half):
        if not self.split[t]:
            return ref
        r = self.shapes[t][0] // 2
        return ref.at[pl.ds(half * r, r), :]

    def _ici(self, ins, outs, sems, k, t, chip, landing_chip):
        x, y, c = lax.axis_index("x"), lax.axis_index("y"), lax.axis_index("c")
        return pltpu.make_async_remote_copy(
            src_ref=self._part(ins[t], t, c), dst_ref=self._part(outs[t].at[landing_chip], t, c),
            send_sem=sems[0].at[k, t], recv_sem=sems[1].at[k, t], device_id=(chip[0], chip[1], c), device_id_type=MESH)

    def _d2d(self, outs, sems, k, t, q, half):
        x, y, c = lax.axis_index("x"), lax.axis_index("y"), lax.axis_index("c")
        rows = self._part(outs[t].at[q], t, half)
        return pltpu.make_async_remote_copy(
            src_ref=rows, dst_ref=rows, send_sem=sems[2].at[k, t], recv_sem=sems[3].at[k, t],
            device_id=(x, y, 1 - c), device_id_type=MESH)

    def start(self, ins, outs, sems):
        x, y = lax.axis_index("x"), lax.axis_index("y")
        me = 2 * x + y
        for k, chip in enumerate(_chips(x, y)):
            for t in range(self.nt):
                self._ici(ins, outs, sems, k, t, chip, me).start()

    def finish(self, ins, outs, sems):
        x, y, c = lax.axis_index("x"), lax.axis_index("y"), lax.axis_index("c")
        me = 2 * x + y
        chips = _chips(x, y)
        for k, chip in enumerate(chips):
            q = 2 * chip[0] + chip[1]
            for t in range(self.nt):
                self._ici(ins, outs, sems, k, t, chip, q).wait_recv()
                if self.split[t]:
                    self._d2d(outs, sems, k, t, q, c).start()
        for k, chip in enumerate(chips):
            q = 2 * chip[0] + chip[1]
            for t in range(self.nt):
                if self.split[t]:
                    self._d2d(outs, sems, k, t, q, 1 - c).wait_recv()
        for k, chip in enumerate(chips):
            q = 2 * chip[0] + chip[1]
            for t in range(self.nt):
                self._ici(ins, outs, sems, k, t, chip, me).wait_send()
                if self.split[t]:
                    self._d2d(outs, sems, k, t, q, c).wait_send()


def _gather_weights(shards, split):
    g = _Gather(shards, split)
    nt = g.nt

    def body(*refs):
        ins, outs, sems = refs[:nt], refs[nt:2 * nt], refs[2 * nt:]
        g.start(ins, outs, sems)
        g.finish(ins, outs, sems)

    return pl.pallas_call(
        body, name="gather_weights", in_specs=g.in_specs, out_specs=g.out_specs, out_shape=g.out_shape,
        scratch_shapes=g.scratch,
    )(*shards)


class _SiblingSend:
    def __init__(self, gs, dst_c, swap=()):
        self.nt, self.n = len(gs), len(gs) + len(swap)
        self.dst_c = list(dst_c) if isinstance(dst_c, (list, tuple)) else [dst_c] * self.nt
        self.in_specs = [ANY] * self.n
        self.out_specs = [ANY] * self.n
        self.out_shape = [jax.ShapeDtypeStruct(g.shape, g.dtype) for g in list(gs) + list(swap)]
        self.scratch = [pltpu.SemaphoreType.DMA((self.n,))] * 2 if self.n else []

    def _each(self, ins, outs, sems, sender, fn):
        x, y, c = lax.axis_index("x"), lax.axis_index("y"), lax.axis_index("c")

        def cp(t):
            return pltpu.make_async_remote_copy(
                src_ref=ins[t], dst_ref=outs[t], send_sem=sems[0].at[t], recv_sem=sems[1].at[t],
                device_id=(x, y, 1 - c), device_id_type=MESH)

        for dst in (0, 1):
            which = [t for t in range(self.nt) if self.dst_c[t] == dst]
            if which:
                @pl.when((c != dst) if sender else (c == dst))
                def _(which=which):
                    for t in which:
                        fn(cp(t))
        for t in range(self.nt, self.n):
            fn(cp(t))

    def start(self, ins, outs, sems):
        self._each(ins, outs, sems, True, lambda cp: cp.start())

    def finish(self, ins, outs, sems):
        self._each(ins, outs, sems, False, lambda cp: cp.wait_recv())
        self._each(ins, outs, sems, True, lambda cp: cp.wait_send())


def _sibling_send(gs, dst_c, swap=()):
    snd = _SiblingSend(gs, dst_c, swap)
    n = snd.n

    def body(*refs):
        ins, outs, sems = refs[:n], refs[n:2 * n], refs[2 * n:]
        snd.start(ins, outs, sems)
        snd.finish(ins, outs, sems)

    res = pl.pallas_call(
        body, name="sibling_send", in_specs=snd.in_specs, out_specs=snd.out_specs, out_shape=snd.out_shape,
        scratch_shapes=snd.scratch,
    )(*gs, *swap)
    return list(res[:snd.nt]), list(res[snd.nt:])


def _owner_flag(owner_c):
    return (lax.axis_index("c") == owner_c).astype(jnp.int32).reshape(1)


def _pair_sum(g, other, name, owner_c):
    R, C = g.shape
    rt = _pick(R, _rows_for(C), 16)

    def body(on_ref, a_ref, o_ref, out_ref):
        out_ref[...] = (a_ref[...] + o_ref[...]).astype(out_ref.dtype)

    row = pl.BlockSpec((rt, C), lambda i, on: (i * on[0], 0))
    return pl.pallas_call(
        body, name=name,
        grid_spec=pltpu.PrefetchScalarGridSpec(num_scalar_prefetch=1, grid=(R // rt,), in_specs=[row, row],
                                               out_specs=row),
        out_shape=jax.ShapeDtypeStruct((R, C), bf16), compiler_params=_cp("arbitrary"),
    )(_owner_flag(owner_c), g, other)


class _ChipExchange:
    def __init__(self, slabs, whole, only_c):
        self.ns, self.nw = len(slabs), len(whole)
        self.only_c = list(only_c) if isinstance(only_c, (list, tuple)) else [only_c] * self.ns
        self.n = self.ns + self.nw
        self.in_specs = [ANY] * self.n
        self.out_specs = [ANY] * self.n
        self.out_shape = ([jax.ShapeDtypeStruct(s.shape, s.dtype) for s in slabs]
                          + [jax.ShapeDtypeStruct((4,) + w.shape, w.dtype) for w in whole])
        self.scratch = [pltpu.SemaphoreType.DMA((3, self.n))] * 2 if self.n else []

    def _copy(self, ins, outs, sems, k, t, chip, landing_chip):
        c = lax.axis_index("c")
        src = ins[t].at[2 * chip[0] + chip[1]] if t < self.ns else ins[t]
        return pltpu.make_async_remote_copy(
            src_ref=src, dst_ref=outs[t].at[landing_chip], send_sem=sems[0].at[k, t], recv_sem=sems[1].at[k, t],
            device_id=(chip[0], chip[1], c), device_id_type=MESH)

    def _each(self, fn):
        x, y, c = lax.axis_index("x"), lax.axis_index("y"), lax.axis_index("c")
        chips = _chips(x, y)
        for owner in (0, 1):
            which = [t for t in range(self.ns) if self.only_c[t] == owner]
            if which:
                @pl.when(c == owner)
                def _(which=which):
                    for k, chip in enumerate(chips):
                        for t in which:
                            fn(k, t, chip)
        for k, chip in enumerate(chips):
            for t in range(self.ns, self.n):
                fn(k, t, chip)

    def start(self, ins, outs, sems):
        me = 2 * lax.axis_index("x") + lax.axis_index("y")
        self._each(lambda k, t, chip: self._copy(ins, outs, sems, k, t, chip, me).start())

    def finish(self, ins, outs, sems):
        me = 2 * lax.axis_index("x") + lax.axis_index("y")
        self._each(lambda k, t, chip: self._copy(ins, outs, sems, k, t, chip, 2 * chip[0] + chip[1]).wait_recv())
        self._each(lambda k, t, chip: self._copy(ins, outs, sems, k, t, chip, me).wait_send())


def _chip_exchange(slabs, whole, only_c):
    ex = _ChipExchange(slabs, whole, only_c)
    n = ex.n

    def body(*refs):
        ins, outs, sems = refs[:n], refs[n:2 * n], refs[2 * n:]
        ex.start(ins, outs, sems)
        ex.finish(ins, outs, sems)

    res = pl.pallas_call(
        body, name="chip_exchange", in_specs=ex.in_specs, out_specs=ex.out_specs, out_shape=ex.out_shape,
        scratch_shapes=ex.scratch,
    )(*slabs, *whole)
    return list(res[:ex.ns]), list(res[ex.ns:])


def _sum_chips(recv, own, name, owner_c=None):
    n, r, C = recv.shape
    rt = _pick(r, _rows_for(C), 16)
    own3 = own.ndim == 3

    def body(on_ref, r_ref, o_ref, out_ref):
        me = 2 * lax.axis_index("x") + lax.axis_index("y")
        acc = None
        for q in range(n):
            term = jnp.where(me == q, o_ref[q] if own3 else o_ref[...], r_ref[q]).astype(f32)
            acc = term if acc is None else acc + term
        out_ref[...] = acc

    blk = pl.BlockSpec((n, rt, C), lambda i, on: (0, i * on[0], 0))
    row = pl.BlockSpec((rt, C), lambda i, on: (i * on[0], 0))
    flag = jnp.ones((1,), jnp.int32) if owner_c is None else _owner_flag(owner_c)
    return pl.pallas_call(
        body, name=name,
        grid_spec=pltpu.PrefetchScalarGridSpec(num_scalar_prefetch=1, grid=(r // rt,),
                                               in_specs=[blk, blk if own3 else row], out_specs=row),
        out_shape=jax.ShapeDtypeStruct((r, C), f32), compiler_params=_cp("arbitrary"),
    )(flag, recv, own)


def _adamw_layers(mine, theirs, owners, w, m, v, name):
    L, r, C = w.shape
    rt = _pick(r, _rows_for(C), 16)

    def body(a0_ref, a1_ref, b0_ref, b1_ref, w_ref, m_ref, v_ref, g_ref, d_ref, mo_ref, vo_ref):
        layer, c = pl.program_id(0), lax.axis_index("c")
        g0 = jnp.where(c == owners[0], a0_ref[...], b0_ref[...])
        g1 = jnp.where(c == owners[1], a1_ref[...], b1_ref[...])
        g_ref[...] = jnp.where(layer == 0, g0, g1)
        _adamw_math(g_ref, w_ref, m_ref, v_ref, d_ref, mo_ref, vo_ref)

    flat = pl.BlockSpec((rt, C), lambda l, i: (i, 0))
    lay = pl.BlockSpec((None, rt, C), lambda l, i: (l, i, 0))
    return pl.pallas_call(
        body, name=name, grid=(L, r // rt), in_specs=[flat] * 4 + [lay] * 3, out_specs=[lay] * 4,
        out_shape=[jax.ShapeDtypeStruct((L, r, C), f32)] * 4, compiler_params=_cp("parallel", "parallel"),
    )(mine[0], mine[1], theirs[0], theirs[1], w, m, v)


def _adamw_math(g_ref, w_ref, m_ref, v_ref, d_ref, mo_ref, vo_ref):
    gg = g_ref[...]
    m_new = ADAM_B1 * m_ref[...] + (1.0 - ADAM_B1) * gg
    v_new = ADAM_B2 * v_ref[...] + (1.0 - ADAM_B2) * (gg * gg)
    m_hat = m_new / (1.0 - ADAM_B1 ** ADAM_STEP)
    v_hat = v_new / (1.0 - ADAM_B2 ** ADAM_STEP)
    d_ref[...] = -ADAM_LR * (m_hat / (jnp.sqrt(v_hat) + ADAM_EPS) + ADAM_WD * w_ref[...])
    mo_ref[...] = m_new
    vo_ref[...] = v_new


FLAT_TILE = 2048


def _add2(a, b, name):
    R = a.shape[0]
    rt = _pick(R, FLAT_TILE, SUBLANES)

    def body(a_ref, b_ref, o_ref):
        o_ref[...] = a_ref[...] + b_ref[...]

    row = pl.BlockSpec((rt, LANES), lambda i: (i, 0))
    return pl.pallas_call(
        body, name=name, grid=(R // rt,), in_specs=[row, row], out_specs=row,
        out_shape=jax.ShapeDtypeStruct((R, LANES), f32), compiler_params=_cp("parallel"),
    )(a, b)


def _adamw(g, w, m, v, name):
    R = g.shape[0]
    rt = _pick(R, FLAT_TILE, SUBLANES)

    def body(g_ref, w_ref, m_ref, v_ref, d_ref, mo_ref, vo_ref):
        gg = g_ref[...]
        m_new = ADAM_B1 * m_ref[...] + (1.0 - ADAM_B1) * gg
        v_new = ADAM_B2 * v_ref[...] + (1.0 - ADAM_B2) * (gg * gg)
        m_hat = m_new / (1.0 - ADAM_B1 ** ADAM_STEP)
        v_hat = v_new / (1.0 - ADAM_B2 ** ADAM_STEP)
        d_ref[...] = -ADAM_LR * (m_hat / (jnp.sqrt(v_hat) + ADAM_EPS) + ADAM_WD * w_ref[...])
        mo_ref[...] = m_new
        vo_ref[...] = v_new

    row = pl.BlockSpec((rt, LANES), lambda i: (i, 0))
    return pl.pallas_call(
        body, name=name, grid=(R // rt,), in_specs=[row] * 4, out_specs=[row] * 3,
        out_shape=[jax.ShapeDtypeStruct((R, LANES), f32)] * 3, compiler_params=_cp("parallel"),
    )(g, w, m, v)


def _pack(arrs, dtype):
    flat = jnp.concatenate([a.astype(dtype).reshape(-1) for a in arrs])
    per = FLAT_TILE * LANES
    flat = jnp.pad(flat, (0, (-flat.shape[0]) % per))
    return flat.reshape(-1, LANES)


def _unpack(buf, shapes):
    flat = buf.reshape(-1)
    out, off = [], 0
    for s in shapes:
        n = math.prod(s)
        out.append(flat[off:off + n].reshape(s))
        off += n
    return out


def _block_diag(w):
    n, a, b = w.shape
    eye = jnp.eye(n, dtype=w.dtype)
    return (w[:, :, None, :] * eye[:, None, :, None]).reshape(n * a, n * b)


def _diag_blocks(m, n):
    a, b = m.shape[0] // n, m.shape[1] // n
    idx = jnp.arange(n)
    return m.reshape(n, a, n, b)[idx, :, idx, :]


BIG = ("w_in", "w_out", "w_up", "w_down", "s5_w_glu")
BIG_COL_SHARDED = {"w_in": True, "w_out": False, "w_up": True, "w_down": False, "s5_w_glu": False}
CONV_SHARDED = ("lru_conv_w", "ffn_conv_w")
SMALL = ("lru_conv_b", "lru_wr", "lru_br", "lru_wi", "lru_bi", "lru_lambda", "s5_a_re", "s5_a_im", "s5_b_re",
         "s5_b_im", "s5_c_re", "s5_c_im", "s5_d", "s5_log_step", "s5_b_glu", "mix_norm_g", "ln1_g", "ln1_b",
         "ffn_conv_b", "ln2_g", "ln2_b")
WEIGHTS = ("w_in", "lru_conv_w", "lru_conv_b", "lru_wr", "lru_br", "lru_wi", "lru_bi", "lru_lambda", "s5_a_re",
           "s5_a_im", "s5_b_re", "s5_b_im", "s5_c_re", "s5_c_im", "s5_d", "s5_log_step", "s5_w_glu", "s5_b_glu",
           "mix_norm_g", "w_out", "ln1_g", "ln1_b", "w_up", "ffn_conv_w", "ffn_conv_b", "w_down", "ln2_g", "ln2_b")


def _assemble(slabs, col_sharded):
    _, L, r, c = slabs.shape
    if col_sharded:
        return slabs.transpose(1, 2, 0, 3).reshape(L, r, 4 * c)
    return slabs.transpose(1, 0, 2, 3).reshape(L, 4 * r, c)


def _s5_prepare(p):
    G = N_S5_GROUPS
    bt_re, bt_im = p["s5_b_re"].transpose(0, 2, 1), p["s5_b_im"].transpose(0, 2, 1)
    ls = p["s5_log_step"].reshape(G, 1)
    ab_re, ab_im, bb_re, bb_im = _s5_params(p["s5_a_re"], p["s5_a_im"], ls, bt_re, bt_im)
    ab = jnp.concatenate([ab_re.reshape(1, S5_LANES), ab_im.reshape(1, S5_LANES)], axis=1)
    bbcat = jnp.concatenate([_block_diag(bb_re), _block_diag(bb_im)], axis=1).astype(bf16)
    ccat = jnp.concatenate([_block_diag(p["s5_c_re"].transpose(0, 2, 1)),
                            -_block_diag(p["s5_c_im"].transpose(0, 2, 1))], axis=0).astype(bf16)
    bbcat_pad = jnp.concatenate([bbcat, jnp.zeros((LRU_WIDTH - S5_WIDTH, 2 * S5_LANES), bf16)], axis=0)
    return dict(bt_re=bt_re, bt_im=bt_im, ls=ls, ab=ab, bbcat=bbcat, bbcat_pad=bbcat_pad, ccat=ccat)


def _layer_fwd(h, p, cos, sin, pending, install):
    sv = {"h": h}
    proj = _mm(h, p["w_in"], "nn", "mm_proj", tn=768)
    sv.update(proj=proj)
    qr, kr = _rope_fwd(proj, cos, sin)
    attn, ltot, gathered = _attn_fwd2(qr, kr, proj, [s for _, _, s in pending], [True] * len(pending))
    install(pending, gathered)
    sv.update(qr=qr, kr=kr, attn=attn, ltot=ltot)
    wr, wi = _block_diag(p["lru_wr"]).astype(bf16), _block_diag(p["lru_wi"]).astype(bf16)
    xc, r, i, log_a, u = _lru_pre(proj, p["lru_conv_w"], p["lru_conv_b"], wr, p["lru_br"], wi, p["lru_bi"],
                                  p["lru_lambda"])
    hl, lru = _lru_scan(log_a, u, proj)
    sv.update(wr=wr, wi=wi, xc=xc, r=r, i=i, log_a=log_a, hl=hl, lru=lru)
    s5 = _s5_prepare(p)
    bu = _mm(proj, s5["bbcat_pad"], "nn", "mm_s5_bu", a_win=(PROJ_S5_U, LRU_WIDTH))
    xs = _s5_scan(bu, s5["ab"])
    yc = _mm(xs, s5["ccat"], "nn", "mm_s5_y")
    ssm, y = _s5_out(yc, proj, p["s5_d"].reshape(-1), p["s5_w_glu"], p["s5_b_glu"])
    sv.update(s5=s5, xs=xs, y=y, ssm=ssm)
    mixed = _mixnorm(attn, lru, ssm, p["mix_norm_g"])
    mix = _mm(mixed, p["w_out"], "nn", "mm_out")
    h1, z1 = _ln_fwd(h, mix, p["ln1_g"], p["ln1_b"], "ln_fwd")
    sv.update(mixed=mixed, z1=z1, h1=h1)
    up = _mm(h1, p["w_up"], "nn", "mm_up", tn=1536)
    act = _ffn_act(up, p["ffn_conv_w"], p["ffn_conv_b"])
    ffn = _mm(act, p["w_down"], "nn", "mm_down")
    h2, z2 = _ln_fwd(h1, ffn, p["ln2_g"], p["ln2_b"], "ln_fwd")
    sv.update(up=up, act=act, z2=z2)
    return h2, sv


def _layer_bwd(dy_a, dy_b, p, sv, cos, sin, relay=None):
    gr = {}
    dz2, acc = _ln_bwd(dy_a, dy_b, sv["z2"], p["ln2_g"], "ln_bwd_top" if dy_a is None else "ln_bwd")
    gr["ln2_g"], gr["ln2_b"] = acc[0], acc[1]
    dact = _mm(dz2, p["w_down"], "nt", "mm_dact")
    gr["w_down"] = _mm(sv["act"], dz2, "tn", "mm_dw_down")
    dupc = _ffn_act_bwd(dact, sv["up"], p["ffn_conv_w"], p["ffn_conv_b"])
    send, relay_c, to_slabs = relay if relay else ((), 0, None)
    dup, acc, from_sibling = _conv_bwd(dupc, sv["up"], p["ffn_conv_w"], "ffn_conv_bwd", col_tile=FFN_COL_TILE,
                                       out_dtype=bf16, send=send, send_dst=relay_c)
    slabs = to_slabs(from_sibling) if relay else ()
    gr["ffn_conv_w"], gr["ffn_conv_b"] = acc[0:3], acc[3]
    dh1 = _mm(dup, p["w_up"], "nt", "mm_dh1", tk=2048)
    gr["w_up"] = _mm(sv["h1"], dup, "tn", "mm_dw_up", tn=1536)
    dz1, acc = _ln_bwd(dz2, dh1, sv["z1"], p["ln1_g"], "ln_bwd")
    gr["ln1_g"], gr["ln1_b"] = acc[0], acc[1]
    dmixed = _mm(dz1, p["w_out"], "nt", "mm_dmixed")
    gr["w_out"] = _mm(sv["mixed"], dz1, "tn", "mm_dw_out")
    dattn, dlru, dssm, delta, acc = _mixnorm_bwd(dmixed, sv["attn"], sv["lru"], sv["ssm"], p["mix_norm_g"])
    gr["mix_norm_g"] = acc[0]
    proj = sv["proj"]
    dqr, dkr, dv, received = _attn_bwd2(sv["qr"], sv["kr"], proj, dattn, sv["ltot"], delta, slabs, relay_c)
    dq, dk = _rope_bwd(dqr, dkr, cos, sin)
    g, dgate = _lru_scan_bwd(dlru, proj, sv["hl"], sv["log_a"])
    dxc, dwr, dwi, acc = _lru_gate_bwd(g, sv["hl"], sv["xc"], sv["r"], sv["i"], sv["log_a"], sv["wr"], sv["wi"],
                                       p["lru_lambda"])
    gr["lru_wr"], gr["lru_wi"] = _diag_blocks(dwr, N_LRU_HEADS), _diag_blocks(dwi, N_LRU_HEADS)
    gr["lru_br"], gr["lru_bi"], gr["lru_lambda"] = acc[0], acc[1], acc[2]
    dxr, acc, _ = _conv_bwd(dxc, proj, p["lru_conv_w"], "lru_conv_bwd", x_col_block=PROJ_LRU_X)
    gr["lru_conv_w"], gr["lru_conv_b"] = acc[0:4], acc[4]
    s5 = sv["s5"]
    G = N_S5_GROUPS
    dy, du_direct, dwglu, acc = _s5_out_bwd(dssm, sv["y"], proj, p["s5_d"].reshape(-1), p["s5_w_glu"],
                                            p["s5_b_glu"])
    gr["s5_w_glu"], gr["s5_b_glu"], gr["s5_d"] = dwglu, acc[0], acc[1].reshape(G, S5_GROUP)
    dxs = _mm(dy, s5["ccat"], "nt", "mm_s5_dx")
    dccat = _mm(sv["xs"], dy, "tn", "mm_s5_dc")
    gr["s5_c_re"] = _diag_blocks(dccat[:S5_LANES], G).transpose(0, 2, 1)
    gr["s5_c_im"] = -_diag_blocks(dccat[S5_LANES:], G).transpose(0, 2, 1)
    gs, dab = _s5_scan_bwd(dxs, sv["xs"], s5["ab"])
    du = _mm(gs, s5["bbcat"], "nt", "mm_s5_du", add=du_direct)
    dbbcat = _mm(proj, gs, "tn", "mm_s5_dbb", a_win=(PROJ_S5_U, LRU_WIDTH))[:S5_WIDTH]
    d_ar, d_ai, d_ls, d_btr, d_bti = _s5_params_bwd(
        p["s5_a_re"], p["s5_a_im"], s5["ls"], s5["bt_re"], s5["bt_im"],
        dab[:, :S5_LANES].reshape(G, S5_STATE), dab[:, S5_LANES:].reshape(G, S5_STATE),
        _diag_blocks(dbbcat[:, :S5_LANES], G), _diag_blocks(dbbcat[:, S5_LANES:], G))
    gr["s5_a_re"], gr["s5_a_im"], gr["s5_log_step"] = d_ar, d_ai, d_ls.reshape(G)
    gr["s5_b_re"], gr["s5_b_im"] = d_btr.transpose(0, 2, 1), d_bti.transpose(0, 2, 1)
    pad = jnp.zeros((du.shape[0], D_IN_PAD - D_IN), f32)
    dproj = jnp.concatenate([dq, dk, dv, dxr, dgate, du, pad], axis=1).astype(bf16)
    gr["w_in"] = _mm(sv["h"], dproj, "tn", "mm_dw_in", tn=768)[:, :D_IN]
    dh = _mm(dproj, p["w_in"], "nt", "mm_dh")
    return (dz1, dh, gr, slabs, received) if relay else (dz1, dh, gr)


def _train_step(d):
    x, target = d["x"][0], d["loss_target"][0]
    S = x.shape[0]
    me = 2 * lax.axis_index("x") + lax.axis_index("y")

    def rows2d(a):
        return a.reshape(a.shape[0] * a.shape[1], a.shape[2])

    params = [{n: d[n][l] for n in SMALL} for l in range(DEPTH)]

    def install(items, gathered):
        for (n, l, mine), g in zip(items, gathered):
            g = lax.dynamic_update_slice_in_dim(g, mine[None], me, axis=0)
            if n in CONV_SHARDED:
                full = _assemble(g.reshape((4,) + d[n].shape), True)
                for k in range(DEPTH):
                    params[k][n] = full[k]
                continue
            full = _assemble(g[:, None], BIG_COL_SHARDED[n])[0]
            if n == "w_in":
                full = jnp.pad(full, ((0, 0), (0, D_IN_PAD - D_IN)))
            params[l][n] = full

    def shard(n, l):
        return (n, l, d[n][l].astype(bf16))

    first = [shard("w_in", 0)] + [(n, None, rows2d(d[n])) for n in CONV_SHARDED]
    install(first, _gather_weights([s for _, _, s in first], [True] + [False] * len(CONV_SHARDED)))
    later = [[shard(n, 0) for n in BIG[1:]] + [shard("w_in", 1)], [shard(n, 1) for n in BIG[1:]]]

    cos, sin = _rope_tables(S)
    h, saved = x, []
    for l in range(DEPTH):
        h, sv = _layer_fwd(h, params[l], cos, sin, later[l], install)
        saved.append(sv)
    dy, loss_acc = _loss_head(h, target)
    def chip_slabs(layer, owners, others):
        chip = [_pair_sum(grads[layer][n], o, "pair_sum_" + n, own) for n, o, own in zip(BIG, others, owners)]
        return [p.reshape(p.shape[0], 4, p.shape[1] // 4).transpose(1, 0, 2) if BIG_COL_SHARDED[n]
                else p.reshape(4, p.shape[0] // 4, p.shape[1]) for n, p in zip(BIG, chip)]

    own1 = [0] * len(BIG)
    own0 = [1 if n == "w_up" else 0 for n in BIG]
    da, db, grads = None, dy, [None] * DEPTH
    da, db, grads[1] = _layer_bwd(da, db, params[1], saved[1], cos, sin)
    relay = ([grads[1][n] for n in BIG], 0, functools.partial(chip_slabs, 1, own1))
    da, db, grads[0], slabs1, recv1 = _layer_bwd(da, db, params[0], saved[0], cos, sin, relay)
    out = {"grad_x": _axpy(da, db, "grad_x")[None]}
    G = {n: jnp.stack([grads[l][n] for l in range(DEPTH)]) for n in SMALL + CONV_SHARDED}

    small = SMALL + CONV_SHARDED
    sp = _pack([G[n] for n in small], f32)
    others0, (sp_sibling,) = _sibling_send([grads[0][n] for n in BIG], own0, [sp])
    slabs0 = chip_slabs(0, own0, others0)
    chip_small = _add2(sp, sp_sibling, "pair_sum_small")
    recv0, (recv_small,) = _chip_exchange(slabs0, [chip_small], own0)
    mine0 = [_sum_chips(r, s, "sum_chips_" + n, o) for n, r, s, o in zip(BIG, recv0, slabs0, own0)]
    mine1 = [_sum_chips(r, s, "sum_chips_" + n, o) for n, r, s, o in zip(BIG, recv1, slabs1, own1)]
    sent, _ = _sibling_send(mine0 + mine1, [1 - o for o in own0 + own1])
    theirs0, theirs1 = sent[:len(BIG)], sent[len(BIG):]
    for i, n in enumerate(BIG):
        upd = _adamw_layers((mine0[i], mine1[i]), (theirs0[i], theirs1[i]), (own0[i], own1[i]),
                            d[n], d["m_" + n], d["v_" + n], "adamw_" + n)
        for pre, u in zip(("grad_", "delta_", "new_m_", "new_v_"), upd):
            out[pre + n] = u

    total = _sum_chips(recv_small, chip_small, "sum_chips_small")
    gs = dict(zip(small, _unpack(total, [G[n].shape for n in small])))
    for n in CONV_SHARDED:
        L, K, C = gs[n].shape
        gs[n] = lax.dynamic_index_in_dim(gs[n].reshape(L, K, 4, C // 4), me, axis=2, keepdims=False)
    small_shapes = [d[n].shape for n in small]
    gsmall = _pack([gs[n] for n in small], f32)
    upd = _adamw(gsmall, _pack([d[n] for n in small], f32), _pack([d["m_" + n] for n in small], f32),
                 _pack([d["v_" + n] for n in small], f32), "adamw_small")
    for pre, buf in zip(("grad_", "delta_", "new_m_", "new_v_"), (gsmall,) + tuple(upd)):
        for n, a in zip(small, _unpack(buf, small_shapes)):
            out[pre + n] = a

    loss_local, _ = lax.optimization_barrier((loss_acc[0, 0], upd[0]))
    out["loss"] = lax.psum(loss_local, ("x", "y", "c"))
    return (out["loss"], out["grad_x"]) + tuple(out[pre + n] for pre in ("grad_", "delta_", "new_m_", "new_v_")
                                                for n in WEIGHTS)


def kernel(
        x, w_in, lru_conv_w, lru_conv_b, lru_wr, lru_br, lru_wi, lru_bi, lru_lambda, s5_a_re, s5_a_im, s5_b_re,
        s5_b_im, s5_c_re, s5_c_im, s5_d, s5_log_step, s5_w_glu, s5_b_glu, mix_norm_g, w_out, ln1_g, ln1_b, w_up,
        ffn_conv_w, ffn_conv_b, w_down, ln2_g, ln2_b, loss_target, m_w_in, m_lru_conv_w, m_lru_conv_b, m_lru_wr,
        m_lru_br, m_lru_wi, m_lru_bi, m_lru_lambda, m_s5_a_re, m_s5_a_im, m_s5_b_re, m_s5_b_im, m_s5_c_re,
        m_s5_c_im, m_s5_d, m_s5_log_step, m_s5_w_glu, m_s5_b_glu, m_mix_norm_g, m_w_out, m_ln1_g, m_ln1_b,
        m_w_up, m_ffn_conv_w, m_ffn_conv_b, m_w_down, m_ln2_g, m_ln2_b, v_w_in, v_lru_conv_w, v_lru_conv_b,
        v_lru_wr, v_lru_br, v_lru_wi, v_lru_bi, v_lru_lambda, v_s5_a_re, v_s5_a_im, v_s5_b_re, v_s5_b_im,
        v_s5_c_re, v_s5_c_im, v_s5_d, v_s5_log_step, v_s5_w_glu, v_s5_b_glu, v_mix_norm_g, v_w_out, v_ln1_g,
        v_ln1_b, v_w_up, v_ffn_conv_w, v_ffn_conv_b, v_w_down, v_ln2_g, v_ln2_b
):
    return _train_step(dict(locals()))
```

```python
import functools
import math

import jax
import jax.numpy as jnp
from jax import lax
from jax.experimental import pallas as pl
from jax.experimental.pallas import tpu as pltpu

f32 = jnp.float32
bf16 = jnp.bfloat16
MESH = pl.DeviceIdType.MESH

D_MODEL = 1024
ATTN_WIDTH = 384
LRU_WIDTH = 384
S5_WIDTH = 256
HEAD_DIM = 64
N_LRU_HEADS = 6
N_S5_GROUPS = 16
S5_GROUP = 16
S5_STATE = 64
S5_LANES = N_S5_GROUPS * S5_STATE
D_FF = 3072
D_IN = 2176
LRU_C = 8.0
ROPE_THETA = 10000.0
DILATIONS = (1, 4, 16)
ATTN_BLOCK = 128
DEPTH = 2
ALPHA = (2 * DEPTH) ** 0.25
LN_EPS = 1e-5
RMS_EPS = 1e-6
ADAM_LR, ADAM_B1, ADAM_B2, ADAM_EPS, ADAM_WD, ADAM_STEP = 0.001, 0.9, 0.999, 1e-08, 0.01, 10

SUBLANES = 8
LANES = 128
VMEM_LIMIT = 56 * 1024 * 1024
ROW_TILE = 512
MM_SINGLE_K = 3072
D_IN_PAD = 2304
NEG = -1e30


def _cp(*sem):
    return pltpu.CompilerParams(dimension_semantics=sem if sem else None, vmem_limit_bytes=VMEM_LIMIT)


def _pick(dim, pref, align=LANES):
    if dim <= pref:
        return dim
    t = (pref // align) * align
    while t >= align:
        if dim % t == 0:
            return t
        t -= align
    return dim


def _gelu(x):
    return jax.nn.gelu(x)


def _gelu_grad(x):
    c = math.sqrt(2.0 / math.pi)
    t = jnp.tanh(c * (x + 0.044715 * x * x * x))
    return 0.5 * (1.0 + t) + 0.5 * x * (1.0 - t * t) * c * (1.0 + 3 * 0.044715 * x * x)


def _gelu_pair(x):
    c = math.sqrt(2.0 / math.pi)
    x2 = x * x
    t = jnp.tanh(c * x * (1.0 + 0.044715 * x2))
    return 0.5 * x * (1.0 + t), 0.5 * (1.0 + t) + 0.5 * x * (1.0 - t * t) * c * (1.0 + 3 * 0.044715 * x2)


def _sigmoid(x):
    return jax.nn.sigmoid(x)


def _expm1(x):
    p = 1.0 + x / 9.0
    for n in (8.0, 7.0, 6.0, 5.0, 4.0, 3.0, 2.0):
        p = 1.0 + (x / n) * p
    return jnp.where(jnp.abs(x) < 0.3, x * p, jnp.exp(x) - 1.0)


def _dot(a, b, dims):
    return lax.dot_general(a, b, (dims, ((), ())), preferred_element_type=f32)


NN = ((1,), (0,))
NT = ((1,), (1,))
TN = ((0,), (0,))


def _mm(a, b, mode, name, out_dtype=f32, tm=1024, tn=1024, tk=1024, add=None, a_win=None):
    if mode == "nn":
        (M, K), N = a.shape, b.shape[1]
    elif mode == "nt":
        (M, K), N = a.shape, b.shape[0]
    else:
        (K, M), N = a.shape, b.shape[1]
    win = 0
    if a_win is not None:
        win, w = a_win
        if mode == "tn":
            M, tm = w, w
        else:
            K = w
    single = mode != "tn" and K <= MM_SINGLE_K
    tm, tn = _pick(M, tm), _pick(N, tn)
    tk = K if single else _pick(K, tk)
    nk = K // tk
    dims = {"nn": NN, "nt": NT, "tn": TN}[mode]

    def body(a_ref, b_ref, *rest):
        prod = _dot(a_ref[...].astype(bf16), b_ref[...].astype(bf16), dims)
        if single:
            o_ref = rest[-1]
            o_ref[...] = (prod if add is None else prod + rest[0][...]).astype(o_ref.dtype)
            return
        o_ref, acc = rest[-2:]
        k = pl.program_id(2)

        @pl.when(k == 0)
        def _():
            acc[...] = prod if add is None else prod + rest[0][...]

        @pl.when(k > 0)
        def _():
            acc[...] += prod

        @pl.when(k == nk - 1)
        def _():
            o_ref[...] = acc[...].astype(o_ref.dtype)

    if mode == "tn":
        a_spec = pl.BlockSpec((tk, tm), lambda i, j, k: (k, i + win))
    else:
        a_spec = pl.BlockSpec((tm, tk), lambda i, j, k: (i, k + win))
    if mode == "nt":
        b_spec = pl.BlockSpec((tn, tk), lambda i, j, k: (j, k))
    else:
        b_spec = pl.BlockSpec((tk, tn), lambda i, j, k: (k, j))
    o_spec = pl.BlockSpec((tm, tn), lambda i, j, k: (i, j))
    return pl.pallas_call(
        body, name=name, grid=(M // tm, N // tn, nk),
        in_specs=[a_spec, b_spec] + ([] if add is None else [o_spec]), out_specs=o_spec,
        out_shape=jax.ShapeDtypeStruct((M, N), out_dtype),
        scratch_shapes=[] if single else [pltpu.VMEM((tm, tn), f32)],
        compiler_params=_cp("parallel", "parallel", "arbitrary"),
    )(*((a, b) if add is None else (a, b, add)))


def _shift_down(cur, prev8, k):
    if k == 0:
        return cur
    T, (R, C) = SUBLANES, cur.shape
    rot = pltpu.roll(cur.reshape(R // T, T, C), k, 1)
    before = jnp.concatenate([pltpu.roll(prev8, k, 0)[None], rot[:-1]], axis=0)
    row = lax.broadcasted_iota(jnp.int32, (R // T, T, C), 1)
    return jnp.where(row < k, before, rot).reshape(R, C)


def _shift_up(cur, next8, k):
    if k == 0:
        return cur
    T, (R, C) = SUBLANES, cur.shape
    rot = pltpu.roll(cur.reshape(R // T, T, C), T - k, 1)
    after = jnp.concatenate([rot[1:], pltpu.roll(next8, T - k, 0)[None]], axis=0)
    row = lax.broadcasted_iota(jnp.int32, (R // T, T, C), 1)
    return jnp.where(row < T - k, rot, after).reshape(R, C)


def _prev_halo_spec(rt, cols, ncolblk_fn):
    per = rt // SUBLANES
    return pl.BlockSpec((SUBLANES, cols), lambda *g: (jnp.maximum(g[-1] * per - 1, 0), ncolblk_fn(*g)))


def _ln_fwd(h, branch, g, b, name):
    S, D = h.shape
    rt = _pick(S, ROW_TILE, SUBLANES)

    def body(h_ref, m_ref, g_ref, b_ref, o_ref, z_ref):
        z = ALPHA * h_ref[...] + m_ref[...]
        mu = jnp.mean(z, axis=-1, keepdims=True)
        zc = z - mu
        var = jnp.mean(zc * zc, axis=-1, keepdims=True)
        o_ref[...] = zc * lax.rsqrt(var + LN_EPS) * g_ref[...] + b_ref[...]
        z_ref[...] = z

    row = pl.BlockSpec((rt, D), lambda i: (i, 0))
    vec = pl.BlockSpec((1, D), lambda i: (0, 0))
    return pl.pallas_call(
        body, name=name, grid=(S // rt,), in_specs=[row, row, vec, vec], out_specs=[row, row],
        out_shape=[jax.ShapeDtypeStruct((S, D), f32)] * 2, compiler_params=_cp("parallel"),
    )(h, branch, g.reshape(1, D), b.reshape(1, D))


def _ln_bwd(dy_a, dy_b, z, g, name):
    S, D = z.shape
    rt = _pick(S, ROW_TILE, SUBLANES)
    two = dy_a is not None

    def body(*refs):
        if two:
            a_ref, b_ref, z_ref, g_ref, dz_ref, acc_ref = refs
            dy = ALPHA * a_ref[...] + b_ref[...]
        else:
            b_ref, z_ref, g_ref, dz_ref, acc_ref = refs
            dy = b_ref[...]
        z = z_ref[...]
        mu = jnp.mean(z, axis=-1, keepdims=True)
        zc = z - mu
        var = jnp.mean(zc * zc, axis=-1, keepdims=True)
        rstd = lax.rsqrt(var + LN_EPS)
        xhat = zc * rstd
        dxh = dy * g_ref[...]
        m1 = jnp.mean(dxh, axis=-1, keepdims=True)
        m2 = jnp.mean(dxh * xhat, axis=-1, keepdims=True)
        dz_ref[...] = rstd * (dxh - m1 - xhat * m2)

        @pl.when(pl.program_id(0) == 0)
        def _():
            acc_ref[...] = jnp.zeros_like(acc_ref)

        acc_ref[0:1, :] += jnp.sum(dy * xhat, axis=0, keepdims=True)
        acc_ref[1:2, :] += jnp.sum(dy, axis=0, keepdims=True)

    row = pl.BlockSpec((rt, D), lambda i: (i, 0))
    vec = pl.BlockSpec((1, D), lambda i: (0, 0))
    acc = pl.BlockSpec((SUBLANES, D), lambda i: (0, 0))
    ins = ([dy_a] if two else []) + [dy_b, z, g.reshape(1, D)]
    return pl.pallas_call(
        body, name=name, grid=(S // rt,), in_specs=[row] * (len(ins) - 1) + [vec], out_specs=[row, acc],
        out_shape=[jax.ShapeDtypeStruct((S, D), f32), jax.ShapeDtypeStruct((SUBLANES, D), f32)],
        compiler_params=_cp("arbitrary"),
    )(*ins)


def _loss_head(y, target):
    S, D = y.shape
    rt = _pick(S, ROW_TILE, SUBLANES)

    def body(y_ref, t_ref, dy_ref, acc_ref):
        e = y_ref[...] - t_ref[...]
        dy_ref[...] = e * (1.0 / D)

        @pl.when(pl.program_id(0) == 0)
        def _():
            acc_ref[...] = jnp.zeros_like(acc_ref)

        part = jnp.sum(jnp.mean(e * e, axis=-1, keepdims=True), axis=0, keepdims=True)
        acc_ref[...] += 0.5 * part

    row = pl.BlockSpec((rt, D), lambda i: (i, 0))
    return pl.pallas_call(
        body, name="loss_head", grid=(S // rt,), in_specs=[row, row],
        out_specs=[row, pl.BlockSpec((1, 1), lambda i: (0, 0))],
        out_shape=[jax.ShapeDtypeStruct((S, D), f32), jax.ShapeDtypeStruct((1, 1), f32)],
        compiler_params=_cp("arbitrary"),
    )(y, target)


def _axpy(a, b, name):
    S, D = a.shape
    rt = _pick(S, ROW_TILE, SUBLANES)

    def body(a_ref, b_ref, o_ref):
        o_ref[...] = ALPHA * a_ref[...] + b_ref[...]

    row = pl.BlockSpec((rt, D), lambda i: (i, 0))
    return pl.pallas_call(
        body, name=name, grid=(S // rt,), in_specs=[row, row], out_specs=row,
        out_shape=jax.ShapeDtypeStruct((S, D), f32), compiler_params=_cp("parallel"),
    )(a, b)


def _rope_tables(S):
    rt = _pick(S, ROW_TILE, SUBLANES)

    def body(c_ref, s_ref):
        pos = (pl.program_id(0) * rt + lax.broadcasted_iota(jnp.int32, (rt, LANES), 0)).astype(f32)
        lane = lax.broadcasted_iota(jnp.int32, (rt, LANES), 1)
        j = (lane % (HEAD_DIM // 2)).astype(f32)
        inv = jnp.exp((-j * 2.0 / HEAD_DIM) * math.log(ROPE_THETA))
        ang = pos * inv
        c = jnp.cos(ang)
        s = jnp.where(lane % HEAD_DIM < HEAD_DIM // 2, -jnp.sin(ang), jnp.sin(ang))
        c_ref[...] = jnp.concatenate([c, c, c], axis=1)
        s_ref[...] = jnp.concatenate([s, s, s], axis=1)

    row = pl.BlockSpec((rt, ATTN_WIDTH), lambda i: (i, 0))
    return pl.pallas_call(
        body, name="rope_tables", grid=(S // rt,), in_specs=[], out_specs=[row, row],
        out_shape=[jax.ShapeDtypeStruct((S, ATTN_WIDTH), f32)] * 2, compiler_params=_cp("parallel"),
    )()


def _swap_halves(x):
    lane = lax.broadcasted_iota(jnp.int32, x.shape, 1)
    half = HEAD_DIM // 2
    return jnp.where(lane % HEAD_DIM < half, pltpu.roll(x, x.shape[1] - half, 1), pltpu.roll(x, half, 1))


def _rope_fwd(proj, cos, sin):
    S, W = proj.shape[0], ATTN_WIDTH
    rt = _pick(S, ROW_TILE, SUBLANES)

    def body(q_ref, k_ref, c_ref, s_ref, qo_ref, ko_ref):
        c, s = c_ref[...], s_ref[...]
        qo_ref[...] = q_ref[...] * c + _swap_halves(q_ref[...]) * s
        ko_ref[...] = k_ref[...] * c + _swap_halves(k_ref[...]) * s

    row = pl.BlockSpec((rt, W), lambda i: (i, 0))
    return pl.pallas_call(
        body, name="rope_fwd", grid=(S // rt,), in_specs=[row, pl.BlockSpec((rt, W), lambda i: (i, 1)), row, row],
        out_specs=[row, row], out_shape=[jax.ShapeDtypeStruct((S, W), f32)] * 2, compiler_params=_cp("parallel"),
    )(proj, proj, cos, sin)


def _rope_bwd(dq, dk, cos, sin):
    S, W = dq.shape
    rt = _pick(S, ROW_TILE, SUBLANES)

    def body(q_ref, k_ref, c_ref, s_ref, qo_ref, ko_ref):
        c, s = c_ref[...], s_ref[...]
        qo_ref[...] = q_ref[...] * c + _swap_halves(q_ref[...] * s)
        ko_ref[...] = k_ref[...] * c + _swap_halves(k_ref[...] * s)

    row = pl.BlockSpec((rt, W), lambda i: (i, 0))
    return pl.pallas_call(
        body, name="rope_bwd", grid=(S // rt,), in_specs=[row] * 4, out_specs=[row] * 2,
        out_shape=[jax.ShapeDtypeStruct((S, W), f32)] * 2, compiler_params=_cp("parallel"),
    )(dq, dk, cos, sin)


def _rows(ref, start, d):
    if d == 1:
        return ref[pl.ds(pl.multiple_of(start, ATTN_BLOCK), ATTN_BLOCK), :]
    return ref[pl.ds(start, ATTN_BLOCK, stride=d), :]


def _set_rows(ref, start, d, val):
    if d == 1:
        ref[pl.ds(pl.multiple_of(start, ATTN_BLOCK), ATTN_BLOCK), :] = val
    else:
        ref[pl.ds(start, ATTN_BLOCK, stride=d), :] = val


def _pair_spec(S, first_block):
    return pl.BlockSpec((S, LANES), lambda p: (0, p + first_block))


def _attn_fwd2(qr, kr, proj, shards=(), split=()):
    S = qr.shape[0]
    B = ATTN_BLOCK
    nb = S // B
    scale = HEAD_DIM ** -0.5

    gather = _Gather(shards, split)
    nt = gather.nt

    def body(*refs):
        q_ref, k_ref, v_ref = refs[:3]
        g_ins = refs[3:3 + nt]
        o_ref, l_ref = refs[3 + nt:5 + nt]
        g_outs = refs[5 + nt:5 + 2 * nt]
        m_s, l_s = refs[5 + 2 * nt:7 + 2 * nt]
        g_sems = refs[7 + 2 * nt:]
        if nt:
            @pl.when(pl.program_id(0) == 0)
            def _():
                gather.start(g_ins, g_outs, g_sems)

        qi = lax.broadcasted_iota(jnp.int32, (B, 2 * B), 0)
        ki = lax.broadcasted_iota(jnp.int32, (B, 2 * B), 1)
        dist = qi + B - ki
        band = (dist >= 0) & (dist <= B)
        for bi, d in enumerate(DILATIONS):
            bpc = nb // d

            def blk(b, carry, bi=bi, d=d, bpc=bpc):
                c, n = b // bpc, b % bpc
                start = c + d * B * n
                pstart = c + d * B * jnp.maximum(n - 1, 0)
                valid = band & ((ki >= B) | (n > 0))
                q = _rows(q_ref, start, d).astype(bf16)
                kcat = jnp.concatenate([_rows(k_ref, pstart, d), _rows(k_ref, start, d)], axis=0).astype(bf16)
                vcat = jnp.concatenate([_rows(v_ref, pstart, d), _rows(v_ref, start, d)], axis=0).astype(bf16)
                if bi > 0:
                    m_old, l_old, a_old = _rows(m_s, start, d), _rows(l_s, start, d), _rows(o_ref, start, d)
                ms, ls, accs = [], [], []
                for h in range(2):
                    sl = slice(h * HEAD_DIM, (h + 1) * HEAD_DIM)
                    c0 = h * HEAD_DIM
                    s = jnp.where(valid, _dot(q[:, sl], kcat[:, sl], NT) * scale, NEG)
                    m = jnp.max(s, axis=1, keepdims=True)
                    if bi > 0:
                        mo = m_old[:, c0:c0 + 1]
                        m = jnp.maximum(m, mo)
                        alpha = jnp.exp(mo - m)
                    p = jnp.exp(s - m)
                    l = jnp.sum(p, axis=1, keepdims=True)
                    acc = _dot(p.astype(bf16), vcat[:, sl], NN)
                    if bi > 0:
                        l = l + alpha * l_old[:, c0:c0 + 1]
                        acc = acc + alpha * a_old[:, sl]
                    ms.append(jnp.broadcast_to(m, (B, HEAD_DIM)))
                    ls.append(jnp.broadcast_to(l, (B, HEAD_DIM)))
                    accs.append(acc)
                _set_rows(m_s, start, d, jnp.concatenate(ms, axis=1))
                _set_rows(l_s, start, d, jnp.concatenate(ls, axis=1))
                _set_rows(o_ref, start, d, jnp.concatenate(accs, axis=1))
                return carry

            lax.fori_loop(0, nb, blk, 0, unroll=4)

        def fin(t, carry):
            rows = pl.ds(pl.multiple_of(t * B, B), B)
            l = l_s[rows, :]
            o_ref[rows, :] = o_ref[rows, :] / l
            l_ref[rows, :] = m_s[rows, :] + jnp.log(l)
            return carry

        lax.fori_loop(0, nb, fin, 0)
        if nt:
            @pl.when(pl.program_id(0) == pl.num_programs(0) - 1)
            def _():
                gather.finish(g_ins, g_outs, g_sems)

    pair = _pair_spec(S, 0)
    res = pl.pallas_call(
        body, name="attn_fwd_gather" if nt else "attn_fwd", grid=(3,),
        in_specs=[pair, pair, _pair_spec(S, 2 * ATTN_WIDTH // LANES)] + gather.in_specs,
        out_specs=[pair, pair] + gather.out_specs,
        out_shape=[jax.ShapeDtypeStruct((S, ATTN_WIDTH), f32)] * 2 + gather.out_shape,
        scratch_shapes=[pltpu.VMEM((S, LANES), f32)] * 2 + gather.scratch,
        compiler_params=_cp("arbitrary"),
    )(qr, kr, proj, *shards)
    return res[0], res[1], list(res[2:])


def _attn_bwd2(qr, kr, proj, dattn, ltot, delta, slabs=(), only_c=0):
    S = qr.shape[0]
    B = ATTN_BLOCK
    nb = S // B
    scale = HEAD_DIM ** -0.5
    ex = _ChipExchange(slabs, (), only_c)
    n = ex.n

    def body(*refs):
        q_ref, k_ref, v_ref, do_ref, l_ref, d_ref = refs[:6]
        x_ins = refs[6:6 + n]
        dq_ref, dk_ref, dv_ref = refs[6 + n:9 + n]
        x_outs = refs[9 + n:9 + 2 * n]
        x_sems = refs[9 + 2 * n:]
        if n:
            @pl.when(pl.program_id(0) == 0)
            def _():
                ex.start(x_ins, x_outs, x_sems)

        qi = lax.broadcasted_iota(jnp.int32, (B, 2 * B), 0)
        ki = lax.broadcasted_iota(jnp.int32, (B, 2 * B), 1)
        dist1 = qi + B - ki
        band1 = (dist1 >= 0) & (dist1 <= B)
        ri = lax.broadcasted_iota(jnp.int32, (2 * B, B), 0)
        ci = lax.broadcasted_iota(jnp.int32, (2 * B, B), 1)
        dist2 = ri - ci
        band2 = (dist2 >= 0) & (dist2 <= B)
        for bi, d in enumerate(DILATIONS):
            bpc = nb // d

            def blk(b, carry, bi=bi, d=d, bpc=bpc):
                c, n = b // bpc, b % bpc
                start = c + d * B * n
                pstart = c + d * B * jnp.maximum(n - 1, 0)
                nstart = c + d * B * jnp.minimum(n + 1, bpc - 1)
                valid1 = band1 & ((ki >= B) | (n > 0))
                valid2 = band2 & ((ri < B) | (n + 1 < bpc))
                q_c, q_n = _rows(q_ref, start, d), _rows(q_ref, nstart, d)
                k_p, k_c = _rows(k_ref, pstart, d), _rows(k_ref, start, d)
                v_p, v_c = _rows(v_ref, pstart, d), _rows(v_ref, start, d)
                do_c, do_n = _rows(do_ref, start, d), _rows(do_ref, nstart, d)
                l_c, l_n = _rows(l_ref, start, d), _rows(l_ref, nstart, d)
                d_c, d_n = _rows(d_ref, start, d), _rows(d_ref, nstart, d)
                qc = q_c.astype(bf16)
                qcat = jnp.concatenate([q_c, q_n], axis=0).astype(bf16)
                kc = k_c.astype(bf16)
                kcat = jnp.concatenate([k_p, k_c], axis=0).astype(bf16)
                vc = v_c.astype(bf16)
                vcat = jnp.concatenate([v_p, v_c], axis=0).astype(bf16)
                doc = do_c.astype(bf16)
                docat = jnp.concatenate([do_c, do_n], axis=0).astype(bf16)
                lcat = jnp.concatenate([l_c, l_n], axis=0)
                dcat = jnp.concatenate([d_c, d_n], axis=0)
                dqs, dks, dvs = [], [], []
                for h in range(2):
                    sl = slice(h * HEAD_DIM, (h + 1) * HEAD_DIM)
                    c0 = h * HEAD_DIM
                    s1 = _dot(qc[:, sl], kcat[:, sl], NT) * scale
                    p1 = jnp.where(valid1, jnp.exp(s1 - l_c[:, c0:c0 + 1]), 0.0)
                    dp1 = _dot(doc[:, sl], vcat[:, sl], NT)
                    ds1 = p1 * (dp1 - d_c[:, c0:c0 + 1]) * scale
                    dqs.append(_dot(ds1.astype(bf16), kcat[:, sl], NN))
                    s2 = _dot(qcat[:, sl], kc[:, sl], NT) * scale
                    p2 = jnp.where(valid2, jnp.exp(s2 - lcat[:, c0:c0 + 1]), 0.0)
                    dvs.append(_dot(p2.astype(bf16), docat[:, sl], TN))
                    dp2 = _dot(docat[:, sl], vc[:, sl], NT)
                    ds2 = p2 * (dp2 - dcat[:, c0:c0 + 1]) * scale
                    dks.append(_dot(ds2.astype(bf16), qcat[:, sl], TN))
                for ref, parts in ((dq_ref, dqs), (dk_ref, dks), (dv_ref, dvs)):
                    new = jnp.concatenate(parts, axis=1)
                    if bi > 0:
                        new = new + _rows(ref, start, d)
                    _set_rows(ref, start, d, new)
                return carry

            lax.fori_loop(0, nb, blk, 0, unroll=4)

        if n:
            @pl.when(pl.program_id(0) == pl.num_programs(0) - 1)
            def _():
                ex.finish(x_ins, x_outs, x_sems)

    pair = _pair_spec(S, 0)
    res = pl.pallas_call(
        body, name="attn_bwd_exchange" if n else "attn_bwd", grid=(3,),
        in_specs=[pair, pair, _pair_spec(S, 2 * ATTN_WIDTH // LANES), pair, pair, pair] + ex.in_specs,
        out_specs=[pair] * 3 + ex.out_specs,
        out_shape=[jax.ShapeDtypeStruct((S, ATTN_WIDTH), f32)] * 3 + ex.out_shape,
        scratch_shapes=ex.scratch, compiler_params=_cp("arbitrary"),
    )(qr, kr, proj, dattn, ltot, delta, *slabs)
    return res[0], res[1], res[2], list(res[3:])


def _softplus_neg(lam):
    return jnp.maximum(-lam, 0.0) + jnp.log1p(jnp.exp(-jnp.abs(lam)))


PROJ_LRU_X, PROJ_LRU_GATE, PROJ_S5_U = 3, 4, 5


def _lru_pre(proj, conv_w, conv_b, wr, br, wi, bi, lam):
    S, W = proj.shape[0], LRU_WIDTH
    rt = _pick(S, ROW_TILE, SUBLANES)
    K = conv_w.shape[0]

    def body(x_ref, xp_ref, cw_ref, cb_ref, wr_ref, br_ref, wi_ref, bi_ref, lam_ref,
             xc_ref, r_ref, i_ref, la_ref, u_ref):
        prev = jnp.where(pl.program_id(0) == 0, 0.0, xp_ref[...])
        x = x_ref[...]
        xc = cb_ref[...] + cw_ref[K - 1:K, :] * x
        for k in range(K - 1):
            xc = xc + cw_ref[k:k + 1, :] * _shift_down(x, prev, K - 1 - k)
        xb = xc.astype(bf16)
        r = _sigmoid(_dot(xb, wr_ref[...], NN) + br_ref[...])
        i = _sigmoid(_dot(xb, wi_ref[...], NN) + bi_ref[...])
        log_a = -LRU_C * r * _softplus_neg(lam_ref[...])
        u = jnp.sqrt(-_expm1(2.0 * log_a)) * (i * xc)
        xc_ref[...], r_ref[...], i_ref[...], la_ref[...], u_ref[...] = xc, r, i, log_a, u

    row = pl.BlockSpec((rt, W), lambda i: (i, 0))
    xrow = pl.BlockSpec((rt, W), lambda i: (i, PROJ_LRU_X))
    halo = _prev_halo_spec(rt, W, lambda i: PROJ_LRU_X)
    vec = pl.BlockSpec((1, W), lambda i: (0, 0))
    return pl.pallas_call(
        body, name="lru_pre", grid=(S // rt,),
        in_specs=[xrow, halo, pl.BlockSpec((K, W), lambda i: (0, 0)), vec,
                  pl.BlockSpec((W, W), lambda i: (0, 0)), vec, pl.BlockSpec((W, W), lambda i: (0, 0)), vec, vec],
        out_specs=[row] * 5, out_shape=[jax.ShapeDtypeStruct((S, W), f32)] * 5, compiler_params=_cp("parallel"),
    )(proj, proj, conv_w, conv_b.reshape(1, W), wr, br.reshape(1, W), wi, bi.reshape(1, W), lam.reshape(1, W))


def _tile_rows(shape):
    return lax.broadcasted_iota(jnp.int32, shape, 0)


def _lru_scan(log_a, u, proj):
    S, W = u.shape
    rt = _pick(S, ROW_TILE, SUBLANES)
    T = SUBLANES

    def body(la_ref, u_ref, g_ref, h_ref, o_ref, carry):
        @pl.when(pl.program_id(0) == 0)
        def _():
            carry[...] = jnp.zeros_like(carry)

        row = _tile_rows((T, W))

        def step(t, hp):
            r0 = pl.multiple_of(t * T, T)
            a = jnp.exp(la_ref[pl.ds(r0, T), :])
            x = u_ref[pl.ds(r0, T), :]
            for k in (1, 2, 4):
                x = x + a * jnp.where(row >= k, pltpu.roll(x, k, 0), 0.0)
                a = a * jnp.where(row >= k, pltpu.roll(a, k, 0), 1.0)
            h = x + a * hp
            h_ref[pl.ds(r0, T), :] = h
            o_ref[pl.ds(r0, T), :] = h * _gelu(g_ref[pl.ds(r0, T), :])
            return h[T - 1:T, :]

        carry[0:1, :] = lax.fori_loop(0, rt // T, step, carry[0:1, :])

    row = pl.BlockSpec((rt, W), lambda i: (i, 0))
    grow = pl.BlockSpec((rt, W), lambda i: (i, PROJ_LRU_GATE))
    return pl.pallas_call(
        body, name="lru_scan", grid=(S // rt,), in_specs=[row, row, grow], out_specs=[row] * 2,
        out_shape=[jax.ShapeDtypeStruct((S, W), f32)] * 2, scratch_shapes=[pltpu.VMEM((T, W), f32)],
        compiler_params=_cp("arbitrary"),
    )(log_a, u, proj)


def _lru_scan_bwd(dlru, proj, h, log_a):
    S, W = h.shape
    rt = _pick(S, ROW_TILE, SUBLANES)
    T = SUBLANES
    nblk = S // rt

    def body(d_ref, g_ref, h_ref, la_ref, go_ref, dg_ref, carry):
        @pl.when(pl.program_id(0) == 0)
        def _():
            carry[...] = jnp.zeros_like(carry)

        row = _tile_rows((T, W))

        def step(j, c):
            gn, an = c
            t = rt // T - 1 - j
            r0 = pl.multiple_of(t * T, T)
            d = d_ref[pl.ds(r0, T), :]
            gate = g_ref[pl.ds(r0, T), :]
            a = jnp.exp(la_ref[pl.ds(r0, T), :])
            dg_ref[pl.ds(r0, T), :] = d * h_ref[pl.ds(r0, T), :] * _gelu_grad(gate)
            x = d * _gelu(gate)
            b = jnp.where(row < T - 1, pltpu.roll(a, T - 1, 0), an)
            for k in (1, 2, 4):
                x = x + b * jnp.where(row < T - k, pltpu.roll(x, T - k, 0), 0.0)
                b = b * jnp.where(row < T - k, pltpu.roll(b, T - k, 0), 1.0)
            g = x + b * gn
            go_ref[pl.ds(r0, T), :] = g
            return g[0:1, :], a[0:1, :]

        gn, an = lax.fori_loop(0, rt // T, step, (carry[0:1, :], carry[1:2, :]))
        carry[0:1, :] = gn
        carry[1:2, :] = an

    row = pl.BlockSpec((rt, W), lambda i: (nblk - 1 - i, 0))
    grow = pl.BlockSpec((rt, W), lambda i: (nblk - 1 - i, PROJ_LRU_GATE))
    return pl.pallas_call(
        body, name="lru_scan_bwd", grid=(nblk,), in_specs=[row, grow, row, row], out_specs=[row] * 2,
        out_shape=[jax.ShapeDtypeStruct((S, W), f32)] * 2, scratch_shapes=[pltpu.VMEM((T, W), f32)],
        compiler_params=_cp("arbitrary"),
    )(dlru, proj, h, log_a)


def _lru_gate_bwd(g, h, xc, r, i, log_a, wr, wi, lam):
    S, W = g.shape
    rt = _pick(S, ROW_TILE, SUBLANES)

    def body(g_ref, h_ref, hp_ref, xc_ref, r_ref, i_ref, la_ref, wr_ref, wi_ref, lam_ref,
             dxc_ref, dwr_ref, dwi_ref, acc_ref):
        @pl.when(pl.program_id(0) == 0)
        def _():
            dwr_ref[...] = jnp.zeros_like(dwr_ref)
            dwi_ref[...] = jnp.zeros_like(dwi_ref)
            acc_ref[...] = jnp.zeros_like(acc_ref)

        prev = jnp.where(pl.program_id(0) == 0, 0.0, hp_ref[...])
        gg, xc, r, i, log_a, lam = g_ref[...], xc_ref[...], r_ref[...], i_ref[...], la_ref[...], lam_ref[...]
        hm1 = _shift_down(h_ref[...], prev, 1)
        a = jnp.exp(log_a)
        s = jnp.sqrt(-_expm1(2.0 * log_a))
        da = gg * hm1
        di = gg * s * xc
        dxc = gg * s * i
        ds = gg * i * xc
        dlog_a = da * a - ds * (a * a / s)
        sp = _softplus_neg(lam)
        dr = dlog_a * (-LRU_C * sp)
        dsp = jnp.sum(dlog_a * (-LRU_C * r), axis=0, keepdims=True)
        dpr = dr * r * (1.0 - r)
        dpi = di * i * (1.0 - i)
        dprb, dpib, xb = dpr.astype(bf16), dpi.astype(bf16), xc.astype(bf16)
        dxc_ref[...] = dxc + _dot(dprb, wr_ref[...], NT) + _dot(dpib, wi_ref[...], NT)
        dwr_ref[...] += _dot(xb, dprb, TN)
        dwi_ref[...] += _dot(xb, dpib, TN)
        acc_ref[0:1, :] += jnp.sum(dpr, axis=0, keepdims=True)
        acc_ref[1:2, :] += jnp.sum(dpi, axis=0, keepdims=True)
        acc_ref[2:3, :] += dsp * (-_sigmoid(-lam))

    row = pl.BlockSpec((rt, W), lambda i: (i, 0))
    halo = _prev_halo_spec(rt, W, lambda i: 0)
    vec = pl.BlockSpec((1, W), lambda i: (0, 0))
    mat = pl.BlockSpec((W, W), lambda i: (0, 0))
    acc = pl.BlockSpec((SUBLANES, W), lambda i: (0, 0))
    return pl.pallas_call(
        body, name="lru_gate_bwd", grid=(S // rt,),
        in_specs=[row, row, halo, row, row, row, row, mat, mat, vec], out_specs=[row, mat, mat, acc],
        out_shape=[jax.ShapeDtypeStruct((S, W), f32), jax.ShapeDtypeStruct((W, W), f32),
                   jax.ShapeDtypeStruct((W, W), f32), jax.ShapeDtypeStruct((SUBLANES, W), f32)],
        compiler_params=_cp("arbitrary"),
    )(g, h, h, xc, r, i, log_a, wr, wi, lam.reshape(1, W))


def _conv_bwd(dy, x, conv_w, name, col_tile=None, out_dtype=f32, x_col_block=0, send=(), send_dst=0):
    if dy.ndim == 2:
        dy = dy[None]
    H, S, Ch = dy.shape
    C = H * Ch
    K = conv_w.shape[0]
    ct = Ch if col_tile is None else col_tile
    nct = Ch // ct
    rt = _pick(S, ROW_TILE, SUBLANES)
    nrt = S // rt
    snd = _SiblingSend(send, send_dst)
    n = snd.n

    def body(*refs):
        dy_ref, dyn_ref, x_ref, w_ref = refs[:4]
        s_ins = refs[4:4 + n]
        dx_ref, acc_ref = refs[4 + n:6 + n]
        s_outs, s_sems = refs[6 + n:6 + 2 * n], refs[6 + 2 * n:]
        i = pl.program_id(2)
        if n:
            @pl.when((pl.program_id(0) == 0) & (pl.program_id(1) == 0) & (i == 0))
            def _():
                snd.start(s_ins, s_outs, s_sems)

        @pl.when(i == 0)
        def _():
            acc_ref[...] = jnp.zeros_like(acc_ref)

        nxt = jnp.where(i == nrt - 1, 0.0, dyn_ref[...])
        dy, x = dy_ref[...], x_ref[...]
        ahead = [dy] + [_shift_up(dy, nxt, j) for j in range(1, K)]
        dx = w_ref[K - 1:K, :] * dy
        for k in range(K - 1):
            dx = dx + w_ref[k:k + 1, :] * ahead[K - 1 - k]
        dx_ref[...] = dx.astype(dx_ref.dtype)
        for k in range(K):
            acc_ref[k:k + 1, :] += jnp.sum(ahead[K - 1 - k] * x, axis=0, keepdims=True)
        acc_ref[K:K + 1, :] += jnp.sum(dy, axis=0, keepdims=True)
        if n:
            @pl.when((pl.program_id(0) == H - 1) & (pl.program_id(1) == nct - 1) & (i == nrt - 1))
            def _():
                snd.finish(s_ins, s_outs, s_sems)

    per, last = rt // SUBLANES, S // SUBLANES - 1
    dy_row = pl.BlockSpec((None, rt, ct), lambda h, j, i: (h, i, j))
    dy_next = pl.BlockSpec((None, SUBLANES, ct), lambda h, j, i: (h, jnp.minimum((i + 1) * per, last), j))
    row = pl.BlockSpec((rt, ct), lambda h, j, i: (i, h * nct + j))
    xrow = pl.BlockSpec((rt, ct), lambda h, j, i: (i, h * nct + j + x_col_block))
    res = pl.pallas_call(
        body, name=name, grid=(H, nct, nrt),
        in_specs=[dy_row, dy_next, xrow, pl.BlockSpec((K, ct), lambda h, j, i: (0, h * nct + j))] + snd.in_specs,
        out_specs=[row, pl.BlockSpec((SUBLANES, ct), lambda h, j, i: (0, h * nct + j))] + snd.out_specs,
        out_shape=[jax.ShapeDtypeStruct((S, C), out_dtype), jax.ShapeDtypeStruct((SUBLANES, C), f32)] + snd.out_shape,
        scratch_shapes=snd.scratch,
        compiler_params=_cp(*(("arbitrary",) * 3 if n else ("parallel", "parallel", "arbitrary"))),
    )(dy, dy, x, conv_w, *send)
    return res[0], res[1], list(res[2:])


def _s5_param_fn(a_re, a_im, ls, bt_re, bt_im):
    step = jnp.exp(ls)
    dt_re, dt_im = step * a_re, step * a_im
    mag = jnp.exp(dt_re)
    ab_re, ab_im = mag * jnp.cos(dt_im), mag * jnp.sin(dt_im)
    z_re, z_im = ab_re - 1.0, ab_im
    den = a_re * a_re + a_im * a_im
    f_re = (z_re * a_re + z_im * a_im) / den
    f_im = (z_im * a_re - z_re * a_im) / den
    bb_re = f_re[:, None, :] * bt_re - f_im[:, None, :] * bt_im
    bb_im = f_re[:, None, :] * bt_im + f_im[:, None, :] * bt_re
    return ab_re, ab_im, bb_re, bb_im


def _s5_params(a_re, a_im, ls, bt_re, bt_im):
    def body(ar, ai, l, br, bi, o_ar, o_ai, o_br, o_bi):
        o_ar[...], o_ai[...], o_br[...], o_bi[...] = _s5_param_fn(ar[...], ai[...], l[...], br[...], bi[...])

    return pl.pallas_call(
        body, name="s5_params",
        out_shape=[jax.ShapeDtypeStruct(a_re.shape, f32)] * 2 + [jax.ShapeDtypeStruct(bt_re.shape, f32)] * 2,
        compiler_params=_cp(),
    )(a_re, a_im, ls, bt_re, bt_im)


def _s5_params_bwd(a_re, a_im, ls, bt_re, bt_im, d_ar, d_ai, d_br, d_bi):
    def body(ar, ai, l, br, bi, c_ar, c_ai, c_br, c_bi, g_ar, g_ai, g_l, g_br, g_bi):
        _, vjp = jax.vjp(_s5_param_fn, ar[...], ai[...], l[...], br[...], bi[...])
        g_ar[...], g_ai[...], g_l[...], g_br[...], g_bi[...] = vjp((c_ar[...], c_ai[...], c_br[...], c_bi[...]))

    return pl.pallas_call(
        body, name="s5_params_bwd",
        out_shape=[jax.ShapeDtypeStruct(a_re.shape, f32)] * 2 + [jax.ShapeDtypeStruct(ls.shape, f32)]
        + [jax.ShapeDtypeStruct(bt_re.shape, f32)] * 2,
        compiler_params=_cp(),
    )(a_re, a_im, ls, bt_re, bt_im, d_ar, d_ai, d_br, d_bi)


S5_CHUNK = 256


def _s5_power_tables(ab_ref, p_ref, w_ref, conj):
    T, L = SUBLANES, S5_LANES
    are = ab_ref[0:1, 0:L]
    aim = ab_ref[0:1, L:2 * L]
    if conj:
        aim = -aim
    pre, pim = are, aim
    for n in range(3):
        p_ref[n:n + 1, 0:L] = pre
        p_ref[n:n + 1, L:2 * L] = pim
        pre, pim = pre * pre - pim * pim, 2.0 * pre * pim
    row = _tile_rows((T, L))
    wre = jnp.zeros((T, L), f32)
    wim = jnp.zeros((T, L), f32)
    pre, pim = are, aim
    for n in range(T):
        tgt = (T - 1 - n) if conj else n
        wre = jnp.where(row == tgt, pre, wre)
        wim = jnp.where(row == tgt, pim, wim)
        pre, pim = pre * are - pim * aim, pre * aim + pim * are
    w_ref[:, 0:L] = wre
    w_ref[:, L:2 * L] = wim


def _s5_scan(bu, ab):
    S, L2 = bu.shape
    L = L2 // 2
    rt = _pick(S, 256, SUBLANES)
    T = SUBLANES
    CH = S5_CHUNK

    def body(bu_ref, ab_ref, x_ref, p_ref, w_ref, carry):
        @pl.when(pl.program_id(0) == 0)
        def _():
            carry[...] = jnp.zeros_like(carry)
            _s5_power_tables(ab_ref, p_ref, w_ref, conj=False)

        row = _tile_rows((T, CH))

        def step(t, _):
            r0 = pl.multiple_of(t * T, T)
            for c in range(L // CH):
                lre, lim = pl.ds(c * CH, CH), pl.ds(L + c * CH, CH)
                xr, xi = bu_ref[pl.ds(r0, T), lre], bu_ref[pl.ds(r0, T), lim]
                for n, k in enumerate((1, 2, 4)):
                    pr, pi = p_ref[n:n + 1, lre], p_ref[n:n + 1, lim]
                    sr = jnp.where(row >= k, pltpu.roll(xr, k, 0), 0.0)
                    si = jnp.where(row >= k, pltpu.roll(xi, k, 0), 0.0)
                    xr, xi = xr + pr * sr - pi * si, xi + pr * si + pi * sr
                cr, ci = carry[T - 1:T, lre], carry[T - 1:T, lim]
                wr, wi = w_ref[:, lre], w_ref[:, lim]
                xr, xi = xr + wr * cr - wi * ci, xi + wr * ci + wi * cr
                carry[:, lre] = xr
                carry[:, lim] = xi
                x_ref[pl.ds(r0, T), lre] = xr
                x_ref[pl.ds(r0, T), lim] = xi
            return 0

        lax.fori_loop(0, rt // T, step, 0)

    row_spec = pl.BlockSpec((rt, L2), lambda i: (i, 0))
    return pl.pallas_call(
        body, name="s5_scan", grid=(S // rt,), in_specs=[row_spec, pl.BlockSpec((1, L2), lambda i: (0, 0))],
        out_specs=row_spec, out_shape=jax.ShapeDtypeStruct((S, L2), f32),
        scratch_shapes=[pltpu.VMEM((T, L2), f32), pltpu.VMEM((T, L2), f32), pltpu.VMEM((T, L2), f32)],
        compiler_params=_cp("arbitrary"),
    )(bu, ab)


def _s5_scan_bwd(dx, x, ab):
    S, L2 = dx.shape
    L = L2 // 2
    rt = _pick(S, 256, SUBLANES)
    T = SUBLANES
    CH = S5_CHUNK
    nblk = S // rt
    per = rt // T

    def body(dx_ref, x_ref, xp_ref, ab_ref, g_ref, da_ref, p_ref, w_ref, carry, acc):
        pid = pl.program_id(0)

        @pl.when(pid == 0)
        def _():
            carry[...] = jnp.zeros_like(carry)
            acc[...] = jnp.zeros_like(acc)
            _s5_power_tables(ab_ref, p_ref, w_ref, conj=True)

        row = _tile_rows((T, CH))
        first_block = pid == nblk - 1

        def step(j, _):
            t = per - 1 - j
            r0 = pl.multiple_of(t * T, T)
            rp = pl.multiple_of(jnp.maximum(t - 1, 0) * T, T)
            for c in range(L // CH):
                lre, lim = pl.ds(c * CH, CH), pl.ds(L + c * CH, CH)
                gr, gi = dx_ref[pl.ds(r0, T), lre], dx_ref[pl.ds(r0, T), lim]
                for n, k in enumerate((1, 2, 4)):
                    pr, pi = p_ref[n:n + 1, lre], p_ref[n:n + 1, lim]
                    sr = jnp.where(row < T - k, pltpu.roll(gr, T - k, 0), 0.0)
                    si = jnp.where(row < T - k, pltpu.roll(gi, T - k, 0), 0.0)
                    gr, gi = gr + pr * sr - pi * si, gi + pr * si + pi * sr
                cr, ci = carry[0:1, lre], carry[0:1, lim]
                wr, wi = w_ref[:, lre], w_ref[:, lim]
                gr, gi = gr + wr * cr - wi * ci, gi + wr * ci + wi * cr
                carry[:, lre] = gr
                carry[:, lim] = gi
                g_ref[pl.ds(r0, T), lre] = gr
                g_ref[pl.ds(r0, T), lim] = gi
                xr, xi = x_ref[pl.ds(r0, T), lre], x_ref[pl.ds(r0, T), lim]
                in_blk_r, in_blk_i = x_ref[pl.ds(rp, T), lre], x_ref[pl.ds(rp, T), lim]
                hal_r = jnp.where(first_block, 0.0, xp_ref[:, lre])
                hal_i = jnp.where(first_block, 0.0, xp_ref[:, lim])
                pvr = jnp.where(t == 0, hal_r, in_blk_r)[T - 1:T, :]
                pvi = jnp.where(t == 0, hal_i, in_blk_i)[T - 1:T, :]
                sxr = jnp.where(row >= 1, pltpu.roll(xr, 1, 0), pvr)
                sxi = jnp.where(row >= 1, pltpu.roll(xi, 1, 0), pvi)
                acc[:, lre] += gr * sxr + gi * sxi
                acc[:, lim] += gi * sxr - gr * sxi
            return 0

        lax.fori_loop(0, per, step, 0)

        @pl.when(pid == nblk - 1)
        def _():
            da_ref[...] = jnp.sum(acc[...], axis=0, keepdims=True)

    row_spec = pl.BlockSpec((rt, L2), lambda i: (nblk - 1 - i, 0))
    halo = pl.BlockSpec((T, L2), lambda i: (jnp.maximum((nblk - 1 - i) * per - 1, 0), 0))
    vec = pl.BlockSpec((1, L2), lambda i: (0, 0))
    return pl.pallas_call(
        body, name="s5_scan_bwd", grid=(nblk,), in_specs=[row_spec, row_spec, halo, vec],
        out_specs=[row_spec, vec],
        out_shape=[jax.ShapeDtypeStruct((S, L2), f32), jax.ShapeDtypeStruct((1, L2), f32)],
        scratch_shapes=[pltpu.VMEM((T, L2), f32)] * 4,
        compiler_params=_cp("arbitrary"),
    )(dx, x, x, ab)


def _S5_U_SPEC(rt):
    return pl.BlockSpec((rt, LRU_WIDTH), lambda i: (i, PROJ_S5_U))


def _s5_out(yc, proj, d, wglu, bglu):
    S, W = yc.shape
    rt = _pick(S, ROW_TILE, SUBLANES)

    def body(yc_ref, u_ref, d_ref, w_ref, b_ref, o_ref, y_ref):
        y = yc_ref[...] + d_ref[...] * u_ref[:, 0:W]
        yg = _gelu(y)
        z = _dot(yg.astype(bf16), w_ref[...], NN) + b_ref[...]
        o_ref[...] = yg * _sigmoid(z)
        y_ref[...] = y

    row = pl.BlockSpec((rt, W), lambda i: (i, 0))
    vec = pl.BlockSpec((1, W), lambda i: (0, 0))
    mat = pl.BlockSpec((W, W), lambda i: (0, 0))
    return pl.pallas_call(
        body, name="s5_out", grid=(S // rt,), in_specs=[row, _S5_U_SPEC(rt), vec, mat, vec], out_specs=[row, row],
        out_shape=[jax.ShapeDtypeStruct((S, W), f32)] * 2, compiler_params=_cp("parallel"),
    )(yc, proj, d.reshape(1, W), wglu, bglu.reshape(1, W))


def _s5_out_bwd(dssm, y, proj, d, wglu, bglu):
    S, W = y.shape
    rt = _pick(S, ROW_TILE, SUBLANES)

    def body(do_ref, y_ref, u_ref, d_ref, w_ref, b_ref, dy_ref, du_ref, dw_ref, acc_ref):
        @pl.when(pl.program_id(0) == 0)
        def _():
            dw_ref[...] = jnp.zeros_like(dw_ref)
            acc_ref[...] = jnp.zeros_like(acc_ref)

        do, y = do_ref[...], y_ref[...]
        yg = _gelu(y)
        ygb = yg.astype(bf16)
        sg = _sigmoid(_dot(ygb, w_ref[...], NN) + b_ref[...])
        dz = do * yg * sg * (1.0 - sg)
        dzb = dz.astype(bf16)
        dyg = do * sg + _dot(dzb, w_ref[...], NT)
        dy = dyg * _gelu_grad(y)
        dy_ref[...] = dy
        du_ref[...] = dy * d_ref[...]
        dw_ref[...] += _dot(ygb, dzb, TN)
        acc_ref[0:1, :] += jnp.sum(dz, axis=0, keepdims=True)
        acc_ref[1:2, :] += jnp.sum(dy * u_ref[:, 0:W], axis=0, keepdims=True)

    row = pl.BlockSpec((rt, W), lambda i: (i, 0))
    vec = pl.BlockSpec((1, W), lambda i: (0, 0))
    mat = pl.BlockSpec((W, W), lambda i: (0, 0))
    acc = pl.BlockSpec((SUBLANES, W), lambda i: (0, 0))
    return pl.pallas_call(
        body, name="s5_out_bwd", grid=(S // rt,), in_specs=[row, row, _S5_U_SPEC(rt), vec, mat, vec],
        out_specs=[row, row, mat, acc],
        out_shape=[jax.ShapeDtypeStruct((S, W), f32)] * 2
        + [jax.ShapeDtypeStruct((W, W), f32), jax.ShapeDtypeStruct((SUBLANES, W), f32)],
        compiler_params=_cp("arbitrary"),
    )(dssm, y, proj, d.reshape(1, W), wglu, bglu.reshape(1, W))


MIX_SPLITS = ((0, ATTN_WIDTH), (ATTN_WIDTH, ATTN_WIDTH + LRU_WIDTH), (ATTN_WIDTH + LRU_WIDTH, D_MODEL))


def _mixnorm(attn, lru, ssm, g):
    S = attn.shape[0]
    rt = _pick(S, ROW_TILE, SUBLANES)

    def body(a_ref, l_ref, s_ref, g_ref, o_ref):
        for ref, (lo, hi) in zip((a_ref, l_ref, s_ref), MIX_SPLITS):
            x = ref[...]
            ms = jnp.mean(x * x, axis=-1, keepdims=True)
            o_ref[:, lo:hi] = (x * lax.rsqrt(ms + RMS_EPS) * g_ref[:, lo:hi]).astype(o_ref.dtype)

    rows = [pl.BlockSpec((rt, hi - lo), lambda i: (i, 0)) for lo, hi in MIX_SPLITS]
    return pl.pallas_call(
        body, name="mixnorm", grid=(S // rt,), in_specs=rows + [pl.BlockSpec((1, D_MODEL), lambda i: (0, 0))],
        out_specs=pl.BlockSpec((rt, D_MODEL), lambda i: (i, 0)),
        out_shape=jax.ShapeDtypeStruct((S, D_MODEL), bf16), compiler_params=_cp("parallel"),
    )(attn, lru, ssm, g.reshape(1, D_MODEL))


def _mixnorm_bwd(dmixed, attn, lru, ssm, g):
    S = attn.shape[0]
    rt = _pick(S, ROW_TILE, SUBLANES)

    def body(d_ref, a_ref, l_ref, s_ref, g_ref, da_ref, dl_ref, ds_ref, dlt_ref, acc_ref):
        @pl.when(pl.program_id(0) == 0)
        def _():
            acc_ref[...] = jnp.zeros_like(acc_ref)

        outs = []
        for ref, (lo, hi) in zip((a_ref, l_ref, s_ref), MIX_SPLITS):
            x = ref[...]
            dy = d_ref[:, lo:hi]
            rinv = lax.rsqrt(jnp.mean(x * x, axis=-1, keepdims=True) + RMS_EPS)
            dyg = dy * g_ref[:, lo:hi]
            outs.append(rinv * dyg - x * (rinv * rinv * rinv) * jnp.mean(dyg * x, axis=-1, keepdims=True))
            acc_ref[0:1, lo:hi] += jnp.sum(dy * x * rinv, axis=0, keepdims=True)
        da_ref[...], dl_ref[...], ds_ref[...] = outs
        hi_ = lax.broadcasted_iota(jnp.int32, (ATTN_WIDTH, ATTN_WIDTH), 0) // HEAD_DIM
        hj_ = lax.broadcasted_iota(jnp.int32, (ATTN_WIDTH, ATTN_WIDTH), 1) // HEAD_DIM
        same = jnp.where(hi_ == hj_, 1.0, 0.0).astype(f32)
        dlt_ref[...] = jnp.dot(outs[0] * a_ref[...], same, precision=lax.Precision.HIGHEST, preferred_element_type=f32)

    rows = [pl.BlockSpec((rt, hi - lo), lambda i: (i, 0)) for lo, hi in MIX_SPLITS]
    full = pl.BlockSpec((rt, D_MODEL), lambda i: (i, 0))
    return pl.pallas_call(
        body, name="mixnorm_bwd", grid=(S // rt,),
        in_specs=[full] + rows + [pl.BlockSpec((1, D_MODEL), lambda i: (0, 0))],
        out_specs=rows + [rows[0], pl.BlockSpec((SUBLANES, D_MODEL), lambda i: (0, 0))],
        out_shape=[jax.ShapeDtypeStruct((S, hi - lo), f32) for lo, hi in MIX_SPLITS]
        + [jax.ShapeDtypeStruct((S, ATTN_WIDTH), f32), jax.ShapeDtypeStruct((SUBLANES, D_MODEL), f32)],
        compiler_params=_cp("arbitrary"),
    )(dmixed, attn, lru, ssm, g.reshape(1, D_MODEL))


FFN_COL_TILE = 1536


def _ffn_conv(x, prev, w_ref, b_ref, K):
    y = b_ref[...] + w_ref[K - 1:K, :] * x
    for k in range(K - 1):
        y = y + w_ref[k:k + 1, :] * _shift_down(x, prev, K - 1 - k)
    return y


def _ffn_act(up, conv_w, conv_b):
    S, C2 = up.shape
    C = C2 // 2
    K = conv_w.shape[0]
    ct = FFN_COL_TILE
    nct = C // ct
    rt = _pick(S, ROW_TILE, SUBLANES)

    def body(g_ref, gp_ref, v_ref, vp_ref, wg_ref, wv_ref, bg_ref, bv_ref, o_ref):
        first = pl.program_id(1) == 0
        gate = _ffn_conv(g_ref[...], jnp.where(first, 0.0, gp_ref[...]), wg_ref, bg_ref, K)
        val = _ffn_conv(v_ref[...], jnp.where(first, 0.0, vp_ref[...]), wv_ref, bv_ref, K)
        o_ref[...] = (_gelu(gate) * val).astype(o_ref.dtype)

    def specs(off):
        return (pl.BlockSpec((rt, ct), lambda j, i: (i, j + off)), _prev_halo_spec(rt, ct, lambda j, i: j + off))

    def wspec(off, rows):
        return pl.BlockSpec((rows, ct), lambda j, i: (0, j + off))

    g_s, gp_s = specs(0)
    v_s, vp_s = specs(nct)
    return pl.pallas_call(
        body, name="ffn_act", grid=(nct, S // rt),
        in_specs=[g_s, gp_s, v_s, vp_s, wspec(0, K), wspec(nct, K), wspec(0, 1), wspec(nct, 1)],
        out_specs=pl.BlockSpec((rt, ct), lambda j, i: (i, j)),
        out_shape=jax.ShapeDtypeStruct((S, C), bf16), compiler_params=_cp("parallel", "parallel"),
    )(up, up, up, up, conv_w, conv_w, conv_b.reshape(1, C2), conv_b.reshape(1, C2))


def _ffn_act_bwd(dact, up, conv_w, conv_b):
    S, C2 = up.shape
    C = C2 // 2
    K = conv_w.shape[0]
    ct = FFN_COL_TILE
    nct = C // ct
    rt = _pick(S, ROW_TILE, SUBLANES)

    def body(d_ref, g_ref, gp_ref, v_ref, vp_ref, wg_ref, wv_ref, bg_ref, bv_ref, o_ref):
        first = pl.program_id(1) == 0
        gate = _ffn_conv(g_ref[...], jnp.where(first, 0.0, gp_ref[...]), wg_ref, bg_ref, K)
        val = _ffn_conv(v_ref[...], jnp.where(first, 0.0, vp_ref[...]), wv_ref, bv_ref, K)
        d = d_ref[...]
        gl, dgl = _gelu_pair(gate)
        o_ref[0] = d * val * dgl
        o_ref[1] = d * gl

    def specs(off):
        return (pl.BlockSpec((rt, ct), lambda j, i: (i, j + off)), _prev_halo_spec(rt, ct, lambda j, i: j + off))

    def wspec(off, rows):
        return pl.BlockSpec((rows, ct), lambda j, i: (0, j + off))

    g_s, gp_s = specs(0)
    v_s, vp_s = specs(nct)
    return pl.pallas_call(
        body, name="ffn_act_bwd", grid=(nct, S // rt),
        in_specs=[pl.BlockSpec((rt, ct), lambda j, i: (i, j)), g_s, gp_s, v_s, vp_s,
                  wspec(0, K), wspec(nct, K), wspec(0, 1), wspec(nct, 1)],
        out_specs=pl.BlockSpec((2, rt, ct), lambda j, i: (0, i, j)),
        out_shape=jax.ShapeDtypeStruct((2, S, C), f32), compiler_params=_cp("parallel", "parallel"),
    )(dact, up, up, up, up, conv_w, conv_w, conv_b.reshape(1, C2), conv_b.reshape(1, C2))


ANY = pl.BlockSpec(memory_space=pl.ANY)


def _rows_for(cols):
    return max(16, (1 << 19) // cols)


def _chips(x, y):
    return [(1 - x, y), (x, 1 - y), (1 - x, 1 - y)]


class _Gather:
    def __init__(self, shards, split):
        self.shapes = [s.shape for s in shards]
        self.dtypes = [s.dtype for s in shards]
        self.split = list(split)
        self.nt = len(shards)
        self.in_specs = [ANY] * self.nt
        self.out_specs = [ANY] * self.nt
        self.out_shape = [jax.ShapeDtypeStruct((4,) + s, dt) for s, dt in zip(self.shapes, self.dtypes)]
        self.scratch = [pltpu.SemaphoreType.DMA((3, self.nt))] * 4 if self.nt else []

    def _part(self, ref, t, half):
        if not self.split[t]:
            return ref
        r = self.shapes[t][0] // 2
        return ref.at[pl.ds(half * r, r), :]

    def _ici(self, ins, outs, sems, k, t, chip, landing_chip):
        x, y, c = lax.axis_index("x"), lax.axis_index("y"), lax.axis_index("c")
        return pltpu.make_async_remote_copy(
            src_ref=self._part(ins[t], t, c), dst_ref=self._part(outs[t].at[landing_chip], t, c),
            send_sem=sems[0].at[k, t], recv_sem=sems[1].at[k, t], device_id=(chip[0], chip[1], c), device_id_type=MESH)

    def _d2d(self, outs, sems, k, t, q, half):
        x, y, c = lax.axis_index("x"), lax.axis_index("y"), lax.axis_index("c")
        rows = self._part(outs[t].at[q], t, half)
        return pltpu.make_async_remote_copy(
            src_ref=rows, dst_ref=rows, send_sem=sems[2].at[k, t], recv_sem=sems[3].at[k, t],
            device_id=(x, y, 1 - c), device_id_type=MESH)

    def start(self, ins, outs, sems):
        x, y = lax.axis_index("x"), lax.axis_index("y")
        me = 2 * x + y
        for k, chip in enumerate(_chips(x, y)):
            for t in range(self.nt):
                self._ici(ins, outs, sems, k, t, chip, me).start()

    def finish(self, ins, outs, sems):
        x, y, c = lax.axis_index("x"), lax.axis_index("y"), lax.axis_index("c")
        me = 2 * x + y
        chips = _chips(x, y)
        for k, chip in enumerate(chips):
            q = 2 * chip[0] + chip[1]
            for t in range(self.nt):
                self._ici(ins, outs, sems, k, t, chip, q).wait_recv()
                if self.split[t]:
                    self._d2d(outs, sems, k, t, q, c).start()
        for k, chip in enumerate(chips):
            q = 2 * chip[0] + chip[1]
            for t in range(self.nt):
                if self.split[t]:
                    self._d2d(outs, sems, k, t, q, 1 - c).wait_recv()
        for k, chip in enumerate(chips):
            q = 2 * chip[0] + chip[1]
            for t in range(self.nt):
                self._ici(ins, outs, sems, k, t, chip, me).wait_send()
                if self.split[t]:
                    self._d2d(outs, sems, k, t, q, c).wait_send()


def _gather_weights(shards, split):
    g = _Gather(shards, split)
    nt = g.nt

    def body(*refs):
        ins, outs, sems = refs[:nt], refs[nt:2 * nt], refs[2 * nt:]
        g.start(ins, outs, sems)
        g.finish(ins, outs, sems)

    return pl.pallas_call(
        body, name="gather_weights", in_specs=g.in_specs, out_specs=g.out_specs, out_shape=g.out_shape,
        scratch_shapes=g.scratch,
    )(*shards)


class _SiblingSend:
    def __init__(self, gs, dst_c, swap=()):
        self.nt, self.n = len(gs), len(gs) + len(swap)
        self.dst_c = list(dst_c) if isinstance(dst_c, (list, tuple)) else [dst_c] * self.nt
        self.in_specs = [ANY] * self.n
        self.out_specs = [ANY] * self.n
        self.out_shape = [jax.ShapeDtypeStruct(g.shape, g.dtype) for g in list(gs) + list(swap)]
        self.scratch = [pltpu.SemaphoreType.DMA((self.n,))] * 2 if self.n else []

    def _each(self, ins, outs, sems, sender, fn):
        x, y, c = lax.axis_index("x"), lax.axis_index("y"), lax.axis_index("c")

        def cp(t):
            return pltpu.make_async_remote_copy(
                src_ref=ins[t], dst_ref=outs[t], send_sem=sems[0].at[t], recv_sem=sems[1].at[t],
                device_id=(x, y, 1 - c), device_id_type=MESH)

        for dst in (0, 1):
            which = [t for t in range(self.nt) if self.dst_c[t] == dst]
            if which:
                @pl.when((c != dst) if sender else (c == dst))
                def _(which=which):
                    for t in which:
                        fn(cp(t))
        for t in range(self.nt, self.n):
            fn(cp(t))

    def start(self, ins, outs, sems):
        self._each(ins, outs, sems, True, lambda cp: cp.start())

    def finish(self, ins, outs, sems):
        self._each(ins, outs, sems, False, lambda cp: cp.wait_recv())
        self._each(ins, outs, sems, True, lambda cp: cp.wait_send())


def _sibling_send(gs, dst_c, swap=()):
    snd = _SiblingSend(gs, dst_c, swap)
    n = snd.n

    def body(*refs):
        ins, outs, sems = refs[:n], refs[n:2 * n], refs[2 * n:]
        snd.start(ins, outs, sems)
        snd.finish(ins, outs, sems)

    res = pl.pallas_call(
        body, name="sibling_send", in_specs=snd.in_specs, out_specs=snd.out_specs, out_shape=snd.out_shape,
        scratch_shapes=snd.scratch,
    )(*gs, *swap)
    return list(res[:snd.nt]), list(res[snd.nt:])


def _owner_flag(owner_c):
    return (lax.axis_index("c") == owner_c).astype(jnp.int32).reshape(1)


def _pair_sum(g, other, name, owner_c, col_slabs=False):
    R, C = g.shape
    cb = C // 4 if col_slabs else C
    rt = _pick(R, _rows_for(cb), 16)

    def body(on_ref, a_ref, o_ref, out_ref):
        out_ref[...] = (a_ref[...] + o_ref[...]).astype(out_ref.dtype)

    row = pl.BlockSpec((rt, cb), lambda q, i, on: (i * on[0], q * on[0]))
    if col_slabs:
        out_spec = pl.BlockSpec((None, rt, cb), lambda q, i, on: (q * on[0], i * on[0], 0))
        out_shape = jax.ShapeDtypeStruct((4, R, cb), bf16)
    else:
        out_spec, out_shape = row, jax.ShapeDtypeStruct((R, C), bf16)
    return pl.pallas_call(
        body, name=name,
        grid_spec=pltpu.PrefetchScalarGridSpec(num_scalar_prefetch=1, grid=(C // cb, R // rt),
                                               in_specs=[row, row], out_specs=out_spec),
        out_shape=out_shape, compiler_params=_cp("arbitrary", "arbitrary"),
    )(_owner_flag(owner_c), g, other)


class _ChipExchange:
    def __init__(self, slabs, whole, only_c):
        self.ns, self.nw = len(slabs), len(whole)
        self.only_c = list(only_c) if isinstance(only_c, (list, tuple)) else [only_c] * self.ns
        self.n = self.ns + self.nw
        self.in_specs = [ANY] * self.n
        self.out_specs = [ANY] * self.n
        self.out_shape = ([jax.ShapeDtypeStruct(s.shape, s.dtype) for s in slabs]
                          + [jax.ShapeDtypeStruct((4,) + w.shape, w.dtype) for w in whole])
        self.scratch = [pltpu.SemaphoreType.DMA((3, self.n))] * 2 if self.n else []

    def _copy(self, ins, outs, sems, k, t, chip, landing_chip):
        c = lax.axis_index("c")
        src = ins[t].at[2 * chip[0] + chip[1]] if t < self.ns else ins[t]
        return pltpu.make_async_remote_copy(
            src_ref=src, dst_ref=outs[t].at[landing_chip], send_sem=sems[0].at[k, t], recv_sem=sems[1].at[k, t],
            device_id=(chip[0], chip[1], c), device_id_type=MESH)

    def _each(self, fn):
        x, y, c = lax.axis_index("x"), lax.axis_index("y"), lax.axis_index("c")
        chips = _chips(x, y)
        for owner in (0, 1):
            which = [t for t in range(self.ns) if self.only_c[t] == owner]
            if which:
                @pl.when(c == owner)
                def _(which=which):
                    for k, chip in enumerate(chips):
                        for t in which:
                            fn(k, t, chip)
        for k, chip in enumerate(chips):
            for t in range(self.ns, self.n):
                fn(k, t, chip)

    def start(self, ins, outs, sems):
        me = 2 * lax.axis_index("x") + lax.axis_index("y")
        self._each(lambda k, t, chip: self._copy(ins, outs, sems, k, t, chip, me).start())

    def finish(self, ins, outs, sems):
        me = 2 * lax.axis_index("x") + lax.axis_index("y")
        self._each(lambda k, t, chip: self._copy(ins, outs, sems, k, t, chip, 2 * chip[0] + chip[1]).wait_recv())
        self._each(lambda k, t, chip: self._copy(ins, outs, sems, k, t, chip, me).wait_send())


def _chip_exchange(slabs, whole, only_c):
    ex = _ChipExchange(slabs, whole, only_c)
    n = ex.n

    def body(*refs):
        ins, outs, sems = refs[:n], refs[n:2 * n], refs[2 * n:]
        ex.start(ins, outs, sems)
        ex.finish(ins, outs, sems)

    res = pl.pallas_call(
        body, name="chip_exchange", in_specs=ex.in_specs, out_specs=ex.out_specs, out_shape=ex.out_shape,
        scratch_shapes=ex.scratch,
    )(*slabs, *whole)
    return list(res[:ex.ns]), list(res[ex.ns:])


def _sum_chips(recv, own, name, owner_c=None):
    n, r, C = recv.shape
    rt = _pick(r, _rows_for(C), 16)
    own3 = own.ndim == 3

    def body(on_ref, r_ref, o_ref, out_ref):
        me = 2 * lax.axis_index("x") + lax.axis_index("y")
        acc = None
        for q in range(n):
            term = jnp.where(me == q, o_ref[q] if own3 else o_ref[...], r_ref[q]).astype(f32)
            acc = term if acc is None else acc + term
        out_ref[...] = acc

    blk = pl.BlockSpec((n, rt, C), lambda i, on: (0, i * on[0], 0))
    row = pl.BlockSpec((rt, C), lambda i, on: (i * on[0], 0))
    flag = jnp.ones((1,), jnp.int32) if owner_c is None else _owner_flag(owner_c)
    return pl.pallas_call(
        body, name=name,
        grid_spec=pltpu.PrefetchScalarGridSpec(num_scalar_prefetch=1, grid=(r // rt,),
                                               in_specs=[blk, blk if own3 else row], out_specs=row),
        out_shape=jax.ShapeDtypeStruct((r, C), f32), compiler_params=_cp("arbitrary"),
    )(flag, recv, own)


def _adamw_layers(mine, theirs, owners, w, m, v, name):
    L, r, C = w.shape
    rt = _pick(r, _rows_for(C), 16)

    def body(a0_ref, a1_ref, b0_ref, b1_ref, w_ref, m_ref, v_ref, g_ref, d_ref, mo_ref, vo_ref):
        layer, c = pl.program_id(0), lax.axis_index("c")
        g0 = jnp.where(c == owners[0], a0_ref[...], b0_ref[...])
        g1 = jnp.where(c == owners[1], a1_ref[...], b1_ref[...])
        g_ref[...] = jnp.where(layer == 0, g0, g1)
        _adamw_math(g_ref, w_ref, m_ref, v_ref, d_ref, mo_ref, vo_ref)

    flat = pl.BlockSpec((rt, C), lambda l, i: (i, 0))
    lay = pl.BlockSpec((None, rt, C), lambda l, i: (l, i, 0))
    return pl.pallas_call(
        body, name=name, grid=(L, r // rt), in_specs=[flat] * 4 + [lay] * 3, out_specs=[lay] * 4,
        out_shape=[jax.ShapeDtypeStruct((L, r, C), f32)] * 4, compiler_params=_cp("parallel", "parallel"),
    )(mine[0], mine[1], theirs[0], theirs[1], w, m, v)


def _adamw_math(g_ref, w_ref, m_ref, v_ref, d_ref, mo_ref, vo_ref):
    gg = g_ref[...]
    m_new = ADAM_B1 * m_ref[...] + (1.0 - ADAM_B1) * gg
    v_new = ADAM_B2 * v_ref[...] + (1.0 - ADAM_B2) * (gg * gg)
    m_hat = m_new / (1.0 - ADAM_B1 ** ADAM_STEP)
    v_hat = v_new / (1.0 - ADAM_B2 ** ADAM_STEP)
    d_ref[...] = -ADAM_LR * (m_hat / (jnp.sqrt(v_hat) + ADAM_EPS) + ADAM_WD * w_ref[...])
    mo_ref[...] = m_new
    vo_ref[...] = v_new


FLAT_TILE = 2048


def _add2(a, b, name):
    R = a.shape[0]
    rt = _pick(R, FLAT_TILE, SUBLANES)

    def body(a_ref, b_ref, o_ref):
        o_ref[...] = a_ref[...] + b_ref[...]

    row = pl.BlockSpec((rt, LANES), lambda i: (i, 0))
    return pl.pallas_call(
        body, name=name, grid=(R // rt,), in_specs=[row, row], out_specs=row,
        out_shape=jax.ShapeDtypeStruct((R, LANES), f32), compiler_params=_cp("parallel"),
    )(a, b)


def _adamw(g, w, m, v, name):
    R = g.shape[0]
    rt = _pick(R, FLAT_TILE, SUBLANES)

    def body(g_ref, w_ref, m_ref, v_ref, d_ref, mo_ref, vo_ref):
        gg = g_ref[...]
        m_new = ADAM_B1 * m_ref[...] + (1.0 - ADAM_B1) * gg
        v_new = ADAM_B2 * v_ref[...] + (1.0 - ADAM_B2) * (gg * gg)
        m_hat = m_new / (1.0 - ADAM_B1 ** ADAM_STEP)
        v_hat = v_new / (1.0 - ADAM_B2 ** ADAM_STEP)
        d_ref[...] = -ADAM_LR * (m_hat / (jnp.sqrt(v_hat) + ADAM_EPS) + ADAM_WD * w_ref[...])
        mo_ref[...] = m_new
        vo_ref[...] = v_new

    row = pl.BlockSpec((rt, LANES), lambda i: (i, 0))
    return pl.pallas_call(
        body, name=name, grid=(R // rt,), in_specs=[row] * 4, out_specs=[row] * 3,
        out_shape=[jax.ShapeDtypeStruct((R, LANES), f32)] * 3, compiler_params=_cp("parallel"),
    )(g, w, m, v)


def _pack(arrs, dtype):
    flat = jnp.concatenate([a.astype(dtype).reshape(-1) for a in arrs])
    per = FLAT_TILE * LANES
    flat = jnp.pad(flat, (0, (-flat.shape[0]) % per))
    return flat.reshape(-1, LANES)


def _unpack(buf, shapes):
    flat = buf.reshape(-1)
    out, off = [], 0
    for s in shapes:
        n = math.prod(s)
        out.append(flat[off:off + n].reshape(s))
        off += n
    return out


def _block_diag(w):
    n, a, b = w.shape
    eye = jnp.eye(n, dtype=w.dtype)
    return (w[:, :, None, :] * eye[:, None, :, None]).reshape(n * a, n * b)


def _diag_blocks(m, n):
    a, b = m.shape[0] // n, m.shape[1] // n
    idx = jnp.arange(n)
    return m.reshape(n, a, n, b)[idx, :, idx, :]


BIG = ("w_in", "w_out", "w_up", "w_down", "s5_w_glu")
BIG_COL_SHARDED = {"w_in": True, "w_out": False, "w_up": True, "w_down": False, "s5_w_glu": False}
CONV_SHARDED = ("lru_conv_w", "ffn_conv_w")
SMALL = ("lru_conv_b", "lru_wr", "lru_br", "lru_wi", "lru_bi", "lru_lambda", "s5_a_re", "s5_a_im", "s5_b_re",
         "s5_b_im", "s5_c_re", "s5_c_im", "s5_d", "s5_log_step", "s5_b_glu", "mix_norm_g", "ln1_g", "ln1_b",
         "ffn_conv_b", "ln2_g", "ln2_b")
WEIGHTS = ("w_in", "lru_conv_w", "lru_conv_b", "lru_wr", "lru_br", "lru_wi", "lru_bi", "lru_lambda", "s5_a_re",
           "s5_a_im", "s5_b_re", "s5_b_im", "s5_c_re", "s5_c_im", "s5_d", "s5_log_step", "s5_w_glu", "s5_b_glu",
           "mix_norm_g", "w_out", "ln1_g", "ln1_b", "w_up", "ffn_conv_w", "ffn_conv_b", "w_down", "ln2_g", "ln2_b")


def _assemble(slabs, col_sharded):
    _, L, r, c = slabs.shape
    if col_sharded:
        return slabs.transpose(1, 2, 0, 3).reshape(L, r, 4 * c)
    return slabs.transpose(1, 0, 2, 3).reshape(L, 4 * r, c)


def _s5_prepare(p):
    G = N_S5_GROUPS
    bt_re, bt_im = p["s5_b_re"].transpose(0, 2, 1), p["s5_b_im"].transpose(0, 2, 1)
    ls = p["s5_log_step"].reshape(G, 1)
    ab_re, ab_im, bb_re, bb_im = _s5_params(p["s5_a_re"], p["s5_a_im"], ls, bt_re, bt_im)
    ab = jnp.concatenate([ab_re.reshape(1, S5_LANES), ab_im.reshape(1, S5_LANES)], axis=1)
    bbcat = jnp.concatenate([_block_diag(bb_re), _block_diag(bb_im)], axis=1).astype(bf16)
    ccat = jnp.concatenate([_block_diag(p["s5_c_re"].transpose(0, 2, 1)),
                            -_block_diag(p["s5_c_im"].transpose(0, 2, 1))], axis=0).astype(bf16)
    bbcat_pad = jnp.concatenate([bbcat, jnp.zeros((LRU_WIDTH - S5_WIDTH, 2 * S5_LANES), bf16)], axis=0)
    return dict(bt_re=bt_re, bt_im=bt_im, ls=ls, ab=ab, bbcat=bbcat, bbcat_pad=bbcat_pad, ccat=ccat)


def _layer_fwd(h, p, cos, sin, pending, install):
    sv = {"h": h}
    proj = _mm(h, p["w_in"], "nn", "mm_proj", tn=768)
    sv.update(proj=proj)
    qr, kr = _rope_fwd(proj, cos, sin)
    attn, ltot, gathered = _attn_fwd2(qr, kr, proj, [s for _, _, s in pending], [True] * len(pending))
    install(pending, gathered)
    sv.update(qr=qr, kr=kr, attn=attn, ltot=ltot)
    wr, wi = _block_diag(p["lru_wr"]).astype(bf16), _block_diag(p["lru_wi"]).astype(bf16)
    xc, r, i, log_a, u = _lru_pre(proj, p["lru_conv_w"], p["lru_conv_b"], wr, p["lru_br"], wi, p["lru_bi"],
                                  p["lru_lambda"])
    hl, lru = _lru_scan(log_a, u, proj)
    sv.update(wr=wr, wi=wi, xc=xc, r=r, i=i, log_a=log_a, hl=hl, lru=lru)
    s5 = _s5_prepare(p)
    bu = _mm(proj, s5["bbcat_pad"], "nn", "mm_s5_bu", a_win=(PROJ_S5_U, LRU_WIDTH))
    xs = _s5_scan(bu, s5["ab"])
    yc = _mm(xs, s5["ccat"], "nn", "mm_s5_y")
    ssm, y = _s5_out(yc, proj, p["s5_d"].reshape(-1), p["s5_w_glu"], p["s5_b_glu"])
    sv.update(s5=s5, xs=xs, y=y, ssm=ssm)
    mixed = _mixnorm(attn, lru, ssm, p["mix_norm_g"])
    mix = _mm(mixed, p["w_out"], "nn", "mm_out")
    h1, z1 = _ln_fwd(h, mix, p["ln1_g"], p["ln1_b"], "ln_fwd")
    sv.update(mixed=mixed, z1=z1, h1=h1)
    up = _mm(h1, p["w_up"], "nn", "mm_up", tn=1536)
    act = _ffn_act(up, p["ffn_conv_w"], p["ffn_conv_b"])
    ffn = _mm(act, p["w_down"], "nn", "mm_down")
    h2, z2 = _ln_fwd(h1, ffn, p["ln2_g"], p["ln2_b"], "ln_fwd")
    sv.update(up=up, act=act, z2=z2)
    return h2, sv


def _layer_bwd(dy_a, dy_b, p, sv, cos, sin, relay=None):
    gr = {}
    dz2, acc = _ln_bwd(dy_a, dy_b, sv["z2"], p["ln2_g"], "ln_bwd_top" if dy_a is None else "ln_bwd")
    gr["ln2_g"], gr["ln2_b"] = acc[0], acc[1]
    dact = _mm(dz2, p["w_down"], "nt", "mm_dact")
    gr["w_down"] = _mm(sv["act"], dz2, "tn", "mm_dw_down")
    dupc = _ffn_act_bwd(dact, sv["up"], p["ffn_conv_w"], p["ffn_conv_b"])
    send, relay_c, to_slabs = relay if relay else ((), 0, None)
    dup, acc, from_sibling = _conv_bwd(dupc, sv["up"], p["ffn_conv_w"], "ffn_conv_bwd", col_tile=FFN_COL_TILE,
                                       out_dtype=bf16, send=send, send_dst=relay_c)
    slabs = to_slabs(from_sibling) if relay else ()
    gr["ffn_conv_w"], gr["ffn_conv_b"] = acc[0:3], acc[3]
    dh1 = _mm(dup, p["w_up"], "nt", "mm_dh1", tk=2048)
    gr["w_up"] = _mm(sv["h1"], dup, "tn", "mm_dw_up", tn=1536)
    dz1, acc = _ln_bwd(dz2, dh1, sv["z1"], p["ln1_g"], "ln_bwd")
    gr["ln1_g"], gr["ln1_b"] = acc[0], acc[1]
    dmixed = _mm(dz1, p["w_out"], "nt", "mm_dmixed")
    gr["w_out"] = _mm(sv["mixed"], dz1, "tn", "mm_dw_out")
    dattn, dlru, dssm, delta, acc = _mixnorm_bwd(dmixed, sv["attn"], sv["lru"], sv["ssm"], p["mix_norm_g"])
    gr["mix_norm_g"] = acc[0]
    proj = sv["proj"]
    dqr, dkr, dv, received = _attn_bwd2(sv["qr"], sv["kr"], proj, dattn, sv["ltot"], delta, slabs, relay_c)
    dq, dk = _rope_bwd(dqr, dkr, cos, sin)
    g, dgate = _lru_scan_bwd(dlru, proj, sv["hl"], sv["log_a"])
    dxc, dwr, dwi, acc = _lru_gate_bwd(g, sv["hl"], sv["xc"], sv["r"], sv["i"], sv["log_a"], sv["wr"], sv["wi"],
                                       p["lru_lambda"])
    gr["lru_wr"], gr["lru_wi"] = _diag_blocks(dwr, N_LRU_HEADS), _diag_blocks(dwi, N_LRU_HEADS)
    gr["lru_br"], gr["lru_bi"], gr["lru_lambda"] = acc[0], acc[1], acc[2]
    dxr, acc, _ = _conv_bwd(dxc, proj, p["lru_conv_w"], "lru_conv_bwd", x_col_block=PROJ_LRU_X)
    gr["lru_conv_w"], gr["lru_conv_b"] = acc[0:4], acc[4]
    s5 = sv["s5"]
    G = N_S5_GROUPS
    dy, du_direct, dwglu, acc = _s5_out_bwd(dssm, sv["y"], proj, p["s5_d"].reshape(-1), p["s5_w_glu"],
                                            p["s5_b_glu"])
    gr["s5_w_glu"], gr["s5_b_glu"], gr["s5_d"] = dwglu, acc[0], acc[1].reshape(G, S5_GROUP)
    dxs = _mm(dy, s5["ccat"], "nt", "mm_s5_dx")
    dccat = _mm(sv["xs"], dy, "tn", "mm_s5_dc")
    gr["s5_c_re"] = _diag_blocks(dccat[:S5_LANES], G).transpose(0, 2, 1)
    gr["s5_c_im"] = -_diag_blocks(dccat[S5_LANES:], G).transpose(0, 2, 1)
    gs, dab = _s5_scan_bwd(dxs, sv["xs"], s5["ab"])
    du = _mm(gs, s5["bbcat"], "nt", "mm_s5_du", add=du_direct)
    dbbcat = _mm(proj, gs, "tn", "mm_s5_dbb", a_win=(PROJ_S5_U, LRU_WIDTH))[:S5_WIDTH]
    d_ar, d_ai, d_ls, d_btr, d_bti = _s5_params_bwd(
        p["s5_a_re"], p["s5_a_im"], s5["ls"], s5["bt_re"], s5["bt_im"],
        dab[:, :S5_LANES].reshape(G, S5_STATE), dab[:, S5_LANES:].reshape(G, S5_STATE),
        _diag_blocks(dbbcat[:, :S5_LANES], G), _diag_blocks(dbbcat[:, S5_LANES:], G))
    gr["s5_a_re"], gr["s5_a_im"], gr["s5_log_step"] = d_ar, d_ai, d_ls.reshape(G)
    gr["s5_b_re"], gr["s5_b_im"] = d_btr.transpose(0, 2, 1), d_bti.transpose(0, 2, 1)
    pad = jnp.zeros((du.shape[0], D_IN_PAD - D_IN), f32)
    dproj = jnp.concatenate([dq, dk, dv, dxr, dgate, du, pad], axis=1).astype(bf16)
    gr["w_in"] = _mm(sv["h"], dproj, "tn", "mm_dw_in", tn=768)[:, :D_IN]
    dh = _mm(dproj, p["w_in"], "nt", "mm_dh")
    return (dz1, dh, gr, slabs, received) if relay else (dz1, dh, gr)


def _train_step(d):
    x, target = d["x"][0], d["loss_target"][0]
    S = x.shape[0]
    me = 2 * lax.axis_index("x") + lax.axis_index("y")

    def rows2d(a):
        return a.reshape(a.shape[0] * a.shape[1], a.shape[2])

    params = [{n: d[n][l] for n in SMALL} for l in range(DEPTH)]

    def install(items, gathered):
        for (n, l, mine), g in zip(items, gathered):
            g = lax.dynamic_update_slice_in_dim(g, mine[None], me, axis=0)
            if n in CONV_SHARDED:
                full = _assemble(g.reshape((4,) + d[n].shape), True)
                for k in range(DEPTH):
                    params[k][n] = full[k]
                continue
            full = _assemble(g[:, None], BIG_COL_SHARDED[n])[0]
            if n == "w_in":
                full = jnp.pad(full, ((0, 0), (0, D_IN_PAD - D_IN)))
            params[l][n] = full

    def shard(n, l):
        return (n, l, d[n][l].astype(bf16))

    first = [shard("w_in", 0)] + [(n, None, rows2d(d[n])) for n in CONV_SHARDED]
    install(first, _gather_weights([s for _, _, s in first], [True] + [False] * len(CONV_SHARDED)))
    later = [[shard(n, 0) for n in BIG[1:]] + [shard("w_in", 1)], [shard(n, 1) for n in BIG[1:]]]

    cos, sin = _rope_tables(S)
    h, saved = x, []
    for l in range(DEPTH):
        h, sv = _layer_fwd(h, params[l], cos, sin, later[l], install)
        saved.append(sv)
    dy, loss_acc = _loss_head(h, target)
    def chip_slabs(layer, owners, others):
        out = []
        for n, o, own in zip(BIG, others, owners):
            g = grads[layer][n]
            aligned = BIG_COL_SHARDED[n] and (g.shape[1] // 4) % LANES == 0
            p = _pair_sum(g, o, "pair_sum_" + n, own, col_slabs=aligned)
            if BIG_COL_SHARDED[n] and not aligned:
                p = p.reshape(p.shape[0], 4, p.shape[1] // 4).transpose(1, 0, 2)
            elif not BIG_COL_SHARDED[n]:
                p = p.reshape(4, p.shape[0] // 4, p.shape[1])
            out.append(p)
        return out

    own1 = [0] * len(BIG)
    own0 = [1 if n == "w_up" else 0 for n in BIG]
    da, db, grads = None, dy, [None] * DEPTH
    da, db, grads[1] = _layer_bwd(da, db, params[1], saved[1], cos, sin)
    relay = ([grads[1][n] for n in BIG], 0, functools.partial(chip_slabs, 1, own1))
    da, db, grads[0], slabs1, recv1 = _layer_bwd(da, db, params[0], saved[0], cos, sin, relay)
    out = {"grad_x": _axpy(da, db, "grad_x")[None]}
    G = {n: jnp.stack([grads[l][n] for l in range(DEPTH)]) for n in SMALL + CONV_SHARDED}

    small = SMALL + CONV_SHARDED
    sp = _pack([G[n] for n in small], f32)
    others0, (sp_sibling,) = _sibling_send([grads[0][n] for n in BIG], own0, [sp])
    slabs0 = chip_slabs(0, own0, others0)
    chip_small = _add2(sp, sp_sibling, "pair_sum_small")
    recv0, (recv_small,) = _chip_exchange(slabs0, [chip_small], own0)
    mine0 = [_sum_chips(r, s, "sum_chips_" + n, o) for n, r, s, o in zip(BIG, recv0, slabs0, own0)]
    mine1 = [_sum_chips(r, s, "sum_chips_" + n, o) for n, r, s, o in zip(BIG, recv1, slabs1, own1)]
    sent, _ = _sibling_send(mine0 + mine1, [1 - o for o in own0 + own1])
    theirs0, theirs1 = sent[:len(BIG)], sent[len(BIG):]
    for i, n in enumerate(BIG):
        upd = _adamw_layers((mine0[i], mine1[i]), (theirs0[i], theirs1[i]), (own0[i], own1[i]),
                            d[n], d["m_" + n], d["v_" + n], "adamw_" + n)
        for pre, u in zip(("grad_", "delta_", "new_m_", "new_v_"), upd):
            out[pre + n] = u

    total = _sum_chips(recv_small, chip_small, "sum_chips_small")
    gs = dict(zip(small, _unpack(total, [G[n].shape for n in small])))
    for n in CONV_SHARDED:
        L, K, C = gs[n].shape
        gs[n] = lax.dynamic_index_in_dim(gs[n].reshape(L, K, 4, C // 4), me, axis=2, keepdims=False)
    small_shapes = [d[n].shape for n in small]
    gsmall = _pack([gs[n] for n in small], f32)
    upd = _adamw(gsmall, _pack([d[n] for n in small], f32), _pack([d["m_" + n] for n in small], f32),
                 _pack([d["v_" + n] for n in small], f32), "adamw_small")
    for pre, buf in zip(("grad_", "delta_", "new_m_", "new_v_"), (gsmall,) + tuple(upd)):
        for n, a in zip(small, _unpack(buf, small_shapes)):
            out[pre + n] = a

    loss_local, _ = lax.optimization_barrier((loss_acc[0, 0], upd[0]))
    out["loss"] = lax.psum(loss_local, ("x", "y", "c"))
    return (out["loss"], out["grad_x"]) + tuple(out[pre + n] for pre in ("grad_", "delta_", "new_m_", "new_v_")
                                                for n in WEIGHTS)


def kernel(
        x, w_in, lru_conv_w, lru_conv_b, lru_wr, lru_br, lru_wi, lru_bi, lru_lambda, s5_a_re, s5_a_im, s5_b_re,
        s5_b_im, s5_c_re, s5_c_im, s5_d, s5_log_step, s5_w_glu, s5_b_glu, mix_norm_g, w_out, ln1_g, ln1_b, w_up,
        ffn_conv_w, ffn_conv_b, w_down, ln2_g, ln2_b, loss_target, m_w_in, m_lru_conv_w, m_lru_conv_b, m_lru_wr,
        m_lru_br, m_lru_wi, m_lru_bi, m_lru_lambda, m_s5_a_re, m_s5_a_im, m_s5_b_re, m_s5_b_im, m_s5_c_re,
        m_s5_c_im, m_s5_d, m_s5_log_step, m_s5_w_glu, m_s5_b_glu, m_mix_norm_g, m_w_out, m_ln1_g, m_ln1_b,
        m_w_up, m_ffn_conv_w, m_ffn_conv_b, m_w_down, m_ln2_g, m_ln2_b, v_w_in, v_lru_conv_w, v_lru_conv_b,
        v_lru_wr, v_lru_br, v_lru_wi, v_lru_bi, v_lru_lambda, v_s5_a_re, v_s5_a_im, v_s5_b_re, v_s5_b_im,
        v_s5_c_re, v_s5_c_im, v_s5_d, v_s5_log_step, v_s5_w_glu, v_s5_b_glu, v_mix_norm_g, v_w_out, v_ln1_g,
        v_ln1_b, v_w_up, v_ffn_conv_w, v_ffn_conv_b, v_w_down, v_ln2_g, v_ln2_b
):
    return _train_step(dict(locals()))
```

```python
import functools
import math

import jax
import jax.numpy as jnp
from jax import lax
from jax.experimental import pallas as pl
from jax.experimental.pallas import tpu as pltpu

f32 = jnp.float32
bf16 = jnp.bfloat16
MESH = pl.DeviceIdType.MESH

D_MODEL = 1024
ATTN_WIDTH = 384
LRU_WIDTH = 384
S5_WIDTH = 256
HEAD_DIM = 64
N_LRU_HEADS = 6
N_S5_GROUPS = 16
S5_GROUP = 16
S5_STATE = 64
S5_LANES = N_S5_GROUPS * S5_STATE
D_FF = 3072
D_IN = 2176
LRU_C = 8.0
ROPE_THETA = 10000.0
DILATIONS = (1, 4, 16)
ATTN_BLOCK = 128
DEPTH = 2
ALPHA = (2 * DEPTH) ** 0.25
LN_EPS = 1e-5
RMS_EPS = 1e-6
ADAM_LR, ADAM_B1, ADAM_B2, ADAM_EPS, ADAM_WD, ADAM_STEP = 0.001, 0.9, 0.999, 1e-08, 0.01, 10

SUBLANES = 8
LANES = 128
VMEM_LIMIT = 56 * 1024 * 1024
ROW_TILE = 512
MM_SINGLE_K = 3072
D_IN_PAD = 2304
NEG = -1e30


def _cp(*sem):
    return pltpu.CompilerParams(dimension_semantics=sem if sem else None, vmem_limit_bytes=VMEM_LIMIT)


def _pick(dim, pref, align=LANES):
    if dim <= pref:
        return dim
    t = (pref // align) * align
    while t >= align:
        if dim % t == 0:
            return t
        t -= align
    return dim


def _gelu(x):
    return jax.nn.gelu(x)


def _gelu_grad(x):
    c = math.sqrt(2.0 / math.pi)
    t = jnp.tanh(c * (x + 0.044715 * x * x * x))
    return 0.5 * (1.0 + t) + 0.5 * x * (1.0 - t * t) * c * (1.0 + 3 * 0.044715 * x * x)


def _gelu_pair(x):
    c = math.sqrt(2.0 / math.pi)
    x2 = x * x
    t = jnp.tanh(c * x * (1.0 + 0.044715 * x2))
    return 0.5 * x * (1.0 + t), 0.5 * (1.0 + t) + 0.5 * x * (1.0 - t * t) * c * (1.0 + 3 * 0.044715 * x2)


def _sigmoid(x):
    return jax.nn.sigmoid(x)


def _expm1(x):
    p = 1.0 + x / 9.0
    for n in (8.0, 7.0, 6.0, 5.0, 4.0, 3.0, 2.0):
        p = 1.0 + (x / n) * p
    return jnp.where(jnp.abs(x) < 0.3, x * p, jnp.exp(x) - 1.0)


def _dot(a, b, dims):
    return lax.dot_general(a, b, (dims, ((), ())), preferred_element_type=f32)


NN = ((1,), (0,))
NT = ((1,), (1,))
TN = ((0,), (0,))


def _mm(a, b, mode, name, out_dtype=f32, tm=1024, tn=1024, tk=1024, add=None, a_win=None):
    if mode == "nn":
        (M, K), N = a.shape, b.shape[1]
    elif mode == "nt":
        (M, K), N = a.shape, b.shape[0]
    else:
        (K, M), N = a.shape, b.shape[1]
    win = 0
    if a_win is not None:
        win, w = a_win
        if mode == "tn":
            M, tm = w, w
        else:
            K = w
    single = mode != "tn" and K <= MM_SINGLE_K
    tm, tn = _pick(M, tm), _pick(N, tn)
    tk = K if single else _pick(K, tk)
    nk = K // tk
    dims = {"nn": NN, "nt": NT, "tn": TN}[mode]

    def body(a_ref, b_ref, *rest):
        prod = _dot(a_ref[...].astype(bf16), b_ref[...].astype(bf16), dims)
        if single:
            o_ref = rest[-1]
            o_ref[...] = (prod if add is None else prod + rest[0][...]).astype(o_ref.dtype)
            return
        o_ref, acc = rest[-2:]
        k = pl.program_id(2)

        @pl.when(k == 0)
        def _():
            acc[...] = prod if add is None else prod + rest[0][...]

        @pl.when(k > 0)
        def _():
            acc[...] += prod

        @pl.when(k == nk - 1)
        def _():
            o_ref[...] = acc[...].astype(o_ref.dtype)

    if mode == "tn":
        a_spec = pl.BlockSpec((tk, tm), lambda i, j, k: (k, i + win))
    else:
        a_spec = pl.BlockSpec((tm, tk), lambda i, j, k: (i, k + win))
    if mode == "nt":
        b_spec = pl.BlockSpec((tn, tk), lambda i, j, k: (j, k))
    else:
        b_spec = pl.BlockSpec((tk, tn), lambda i, j, k: (k, j))
    o_spec = pl.BlockSpec((tm, tn), lambda i, j, k: (i, j))
    return pl.pallas_call(
        body, name=name, grid=(M // tm, N // tn, nk),
        in_specs=[a_spec, b_spec] + ([] if add is None else [o_spec]), out_specs=o_spec,
        out_shape=jax.ShapeDtypeStruct((M, N), out_dtype),
        scratch_shapes=[] if single else [pltpu.VMEM((tm, tn), f32)],
        compiler_params=_cp("parallel", "parallel", "arbitrary"),
    )(*((a, b) if add is None else (a, b, add)))


def _shift_down(cur, prev8, k):
    if k == 0:
        return cur
    T, (R, C) = SUBLANES, cur.shape
    rot = pltpu.roll(cur.reshape(R // T, T, C), k, 1)
    before = jnp.concatenate([pltpu.roll(prev8, k, 0)[None], rot[:-1]], axis=0)
    row = lax.broadcasted_iota(jnp.int32, (R // T, T, C), 1)
    return jnp.where(row < k, before, rot).reshape(R, C)


def _shift_up(cur, next8, k):
    if k == 0:
        return cur
    T, (R, C) = SUBLANES, cur.shape
    rot = pltpu.roll(cur.reshape(R // T, T, C), T - k, 1)
    after = jnp.concatenate([rot[1:], pltpu.roll(next8, T - k, 0)[None]], axis=0)
    row = lax.broadcasted_iota(jnp.int32, (R // T, T, C), 1)
    return jnp.where(row < T - k, rot, after).reshape(R, C)


def _prev_halo_spec(rt, cols, ncolblk_fn):
    per = rt // SUBLANES
    return pl.BlockSpec((SUBLANES, cols), lambda *g: (jnp.maximum(g[-1] * per - 1, 0), ncolblk_fn(*g)))


def _ln_fwd(h, branch, g, b, name):
    S, D = h.shape
    rt = _pick(S, ROW_TILE, SUBLANES)

    def body(h_ref, m_ref, g_ref, b_ref, o_ref, z_ref):
        z = ALPHA * h_ref[...] + m_ref[...]
        mu = jnp.mean(z, axis=-1, keepdims=True)
        zc = z - mu
        var = jnp.mean(zc * zc, axis=-1, keepdims=True)
        o_ref[...] = zc * lax.rsqrt(var + LN_EPS) * g_ref[...] + b_ref[...]
        z_ref[...] = z

    row = pl.BlockSpec((rt, D), lambda i: (i, 0))
    vec = pl.BlockSpec((1, D), lambda i: (0, 0))
    return pl.pallas_call(
        body, name=name, grid=(S // rt,), in_specs=[row, row, vec, vec], out_specs=[row, row],
        out_shape=[jax.ShapeDtypeStruct((S, D), f32)] * 2, compiler_params=_cp("parallel"),
    )(h, branch, g.reshape(1, D), b.reshape(1, D))


def _ln_bwd(dy_a, dy_b, z, g, name):
    S, D = z.shape
    rt = _pick(S, ROW_TILE, SUBLANES)
    two = dy_a is not None

    def body(*refs):
        if two:
            a_ref, b_ref, z_ref, g_ref, dz_ref, acc_ref = refs
            dy = ALPHA * a_ref[...] + b_ref[...]
        else:
            b_ref, z_ref, g_ref, dz_ref, acc_ref = refs
            dy = b_ref[...]
        z = z_ref[...]
        mu = jnp.mean(z, axis=-1, keepdims=True)
        zc = z - mu
        var = jnp.mean(zc * zc, axis=-1, keepdims=True)
        rstd = lax.rsqrt(var + LN_EPS)
        xhat = zc * rstd
        dxh = dy * g_ref[...]
        m1 = jnp.mean(dxh, axis=-1, keepdims=True)
        m2 = jnp.mean(dxh * xhat, axis=-1, keepdims=True)
        dz_ref[...] = rstd * (dxh - m1 - xhat * m2)

        @pl.when(pl.program_id(0) == 0)
        def _():
            acc_ref[...] = jnp.zeros_like(acc_ref)

        acc_ref[0:1, :] += jnp.sum(dy * xhat, axis=0, keepdims=True)
        acc_ref[1:2, :] += jnp.sum(dy, axis=0, keepdims=True)

    row = pl.BlockSpec((rt, D), lambda i: (i, 0))
    vec = pl.BlockSpec((1, D), lambda i: (0, 0))
    acc = pl.BlockSpec((SUBLANES, D), lambda i: (0, 0))
    ins = ([dy_a] if two else []) + [dy_b, z, g.reshape(1, D)]
    return pl.pallas_call(
        body, name=name, grid=(S // rt,), in_specs=[row] * (len(ins) - 1) + [vec], out_specs=[row, acc],
        out_shape=[jax.ShapeDtypeStruct((S, D), f32), jax.ShapeDtypeStruct((SUBLANES, D), f32)],
        compiler_params=_cp("arbitrary"),
    )(*ins)


def _loss_head(y, target):
    S, D = y.shape
    rt = _pick(S, ROW_TILE, SUBLANES)

    def body(y_ref, t_ref, dy_ref, acc_ref):
        e = y_ref[...] - t_ref[...]
        dy_ref[...] = e * (1.0 / D)

        @pl.when(pl.program_id(0) == 0)
        def _():
            acc_ref[...] = jnp.zeros_like(acc_ref)

        part = jnp.sum(jnp.mean(e * e, axis=-1, keepdims=True), axis=0, keepdims=True)
        acc_ref[...] += 0.5 * part

    row = pl.BlockSpec((rt, D), lambda i: (i, 0))
    return pl.pallas_call(
        body, name="loss_head", grid=(S // rt,), in_specs=[row, row],
        out_specs=[row, pl.BlockSpec((1, 1), lambda i: (0, 0))],
        out_shape=[jax.ShapeDtypeStruct((S, D), f32), jax.ShapeDtypeStruct((1, 1), f32)],
        compiler_params=_cp("arbitrary"),
    )(y, target)


def _axpy(a, b, name):
    S, D = a.shape
    rt = _pick(S, ROW_TILE, SUBLANES)

    def body(a_ref, b_ref, o_ref):
        o_ref[...] = ALPHA * a_ref[...] + b_ref[...]

    row = pl.BlockSpec((rt, D), lambda i: (i, 0))
    return pl.pallas_call(
        body, name=name, grid=(S // rt,), in_specs=[row, row], out_specs=row,
        out_shape=jax.ShapeDtypeStruct((S, D), f32), compiler_params=_cp("parallel"),
    )(a, b)


def _rope_tables(S):
    rt = _pick(S, ROW_TILE, SUBLANES)

    def body(c_ref, s_ref):
        pos = (pl.program_id(0) * rt + lax.broadcasted_iota(jnp.int32, (rt, LANES), 0)).astype(f32)
        lane = lax.broadcasted_iota(jnp.int32, (rt, LANES), 1)
        j = (lane % (HEAD_DIM // 2)).astype(f32)
        inv = jnp.exp((-j * 2.0 / HEAD_DIM) * math.log(ROPE_THETA))
        ang = pos * inv
        c = jnp.cos(ang)
        s = jnp.where(lane % HEAD_DIM < HEAD_DIM // 2, -jnp.sin(ang), jnp.sin(ang))
        c_ref[...] = jnp.concatenate([c, c, c], axis=1)
        s_ref[...] = jnp.concatenate([s, s, s], axis=1)

    row = pl.BlockSpec((rt, ATTN_WIDTH), lambda i: (i, 0))
    return pl.pallas_call(
        body, name="rope_tables", grid=(S // rt,), in_specs=[], out_specs=[row, row],
        out_shape=[jax.ShapeDtypeStruct((S, ATTN_WIDTH), f32)] * 2, compiler_params=_cp("parallel"),
    )()


def _swap_halves(x):
    lane = lax.broadcasted_iota(jnp.int32, x.shape, 1)
    half = HEAD_DIM // 2
    return jnp.where(lane % HEAD_DIM < half, pltpu.roll(x, x.shape[1] - half, 1), pltpu.roll(x, half, 1))


def _rope_fwd(proj, cos, sin):
    S, W = proj.shape[0], ATTN_WIDTH
    rt = _pick(S, ROW_TILE, SUBLANES)

    def body(q_ref, k_ref, c_ref, s_ref, qo_ref, ko_ref):
        c, s = c_ref[...], s_ref[...]
        qo_ref[...] = q_ref[...] * c + _swap_halves(q_ref[...]) * s
        ko_ref[...] = k_ref[...] * c + _swap_halves(k_ref[...]) * s

    row = pl.BlockSpec((rt, W), lambda i: (i, 0))
    return pl.pallas_call(
        body, name="rope_fwd", grid=(S // rt,), in_specs=[row, pl.BlockSpec((rt, W), lambda i: (i, 1)), row, row],
        out_specs=[row, row], out_shape=[jax.ShapeDtypeStruct((S, W), f32)] * 2, compiler_params=_cp("parallel"),
    )(proj, proj, cos, sin)


def _rope_bwd(dq, dk, cos, sin):
    S, W = dq.shape
    rt = _pick(S, ROW_TILE, SUBLANES)

    def body(q_ref, k_ref, c_ref, s_ref, qo_ref, ko_ref):
        c, s = c_ref[...], s_ref[...]
        qo_ref[...] = q_ref[...] * c + _swap_halves(q_ref[...] * s)
        ko_ref[...] = k_ref[...] * c + _swap_halves(k_ref[...] * s)

    row = pl.BlockSpec((rt, W), lambda i: (i, 0))
    return pl.pallas_call(
        body, name="rope_bwd", grid=(S // rt,), in_specs=[row] * 4, out_specs=[row] * 2,
        out_shape=[jax.ShapeDtypeStruct((S, W), f32)] * 2, compiler_params=_cp("parallel"),
    )(dq, dk, cos, sin)


def _rows(ref, start, d):
    if d == 1:
        return ref[pl.ds(pl.multiple_of(start, ATTN_BLOCK), ATTN_BLOCK), :]
    return ref[pl.ds(start, ATTN_BLOCK, stride=d), :]


def _set_rows(ref, start, d, val):
    if d == 1:
        ref[pl.ds(pl.multiple_of(start, ATTN_BLOCK), ATTN_BLOCK), :] = val
    else:
        ref[pl.ds(start, ATTN_BLOCK, stride=d), :] = val


def _pair_spec(S, first_block):
    return pl.BlockSpec((S, LANES), lambda p: (0, p + first_block))


def _attn_fwd2(qr, kr, proj, shards=(), split=()):
    S = qr.shape[0]
    B = ATTN_BLOCK
    nb = S // B
    scale = HEAD_DIM ** -0.5

    gather = _Gather(shards, split)
    nt = gather.nt

    def body(*refs):
        q_ref, k_ref, v_ref = refs[:3]
        g_ins = refs[3:3 + nt]
        o_ref, l_ref = refs[3 + nt:5 + nt]
        g_outs = refs[5 + nt:5 + 2 * nt]
        m_s, l_s = refs[5 + 2 * nt:7 + 2 * nt]
        g_sems = refs[7 + 2 * nt:]
        if nt:
            @pl.when(pl.program_id(0) == 0)
            def _():
                gather.start(g_ins, g_outs, g_sems)

        qi = lax.broadcasted_iota(jnp.int32, (B, 2 * B), 0)
        ki = lax.broadcasted_iota(jnp.int32, (B, 2 * B), 1)
        dist = qi + B - ki
        band = (dist >= 0) & (dist <= B)
        for bi, d in enumerate(DILATIONS):
            bpc = nb // d

            def blk(b, carry, bi=bi, d=d, bpc=bpc):
                c, n = b // bpc, b % bpc
                start = c + d * B * n
                pstart = c + d * B * jnp.maximum(n - 1, 0)
                valid = band & ((ki >= B) | (n > 0))
                q = _rows(q_ref, start, d).astype(bf16)
                kcat = jnp.concatenate([_rows(k_ref, pstart, d), _rows(k_ref, start, d)], axis=0).astype(bf16)
                vcat = jnp.concatenate([_rows(v_ref, pstart, d), _rows(v_ref, start, d)], axis=0).astype(bf16)
                if bi > 0:
                    m_old, l_old, a_old = _rows(m_s, start, d), _rows(l_s, start, d), _rows(o_ref, start, d)
                ms, ls, accs = [], [], []
                for h in range(2):
                    sl = slice(h * HEAD_DIM, (h + 1) * HEAD_DIM)
                    c0 = h * HEAD_DIM
                    s = jnp.where(valid, _dot(q[:, sl], kcat[:, sl], NT) * scale, NEG)
                    m = jnp.max(s, axis=1, keepdims=True)
                    if bi > 0:
                        mo = m_old[:, c0:c0 + 1]
                        m = jnp.maximum(m, mo)
                        alpha = jnp.exp(mo - m)
                    p = jnp.exp(s - m)
                    l = jnp.sum(p, axis=1, keepdims=True)
                    acc = _dot(p.astype(bf16), vcat[:, sl], NN)
                    if bi > 0:
                        l = l + alpha * l_old[:, c0:c0 + 1]
                        acc = acc + alpha * a_old[:, sl]
                    ms.append(jnp.broadcast_to(m, (B, HEAD_DIM)))
                    ls.append(jnp.broadcast_to(l, (B, HEAD_DIM)))
                    accs.append(acc)
                _set_rows(m_s, start, d, jnp.concatenate(ms, axis=1))
                _set_rows(l_s, start, d, jnp.concatenate(ls, axis=1))
                _set_rows(o_ref, start, d, jnp.concatenate(accs, axis=1))
                return carry

            lax.fori_loop(0, nb, blk, 0, unroll=4)

        def fin(t, carry):
            rows = pl.ds(pl.multiple_of(t * B, B), B)
            l = l_s[rows, :]
            o_ref[rows, :] = o_ref[rows, :] / l
            l_ref[rows, :] = m_s[rows, :] + jnp.log(l)
            return carry

        lax.fori_loop(0, nb, fin, 0)
        if nt:
            @pl.when(pl.program_id(0) == pl.num_programs(0) - 1)
            def _():
                gather.finish(g_ins, g_outs, g_sems)

    pair = _pair_spec(S, 0)
    res = pl.pallas_call(
        body, name="attn_fwd_gather" if nt else "attn_fwd", grid=(3,),
        in_specs=[pair, pair, _pair_spec(S, 2 * ATTN_WIDTH // LANES)] + gather.in_specs,
        out_specs=[pair, pair] + gather.out_specs,
        out_shape=[jax.ShapeDtypeStruct((S, ATTN_WIDTH), f32)] * 2 + gather.out_shape,
        scratch_shapes=[pltpu.VMEM((S, LANES), f32)] * 2 + gather.scratch,
        compiler_params=_cp("arbitrary"),
    )(qr, kr, proj, *shards)
    return res[0], res[1], list(res[2:])


def _attn_bwd2(qr, kr, proj, dattn, ltot, delta, slabs=(), only_c=0, send=(), send_dst=0):
    S = qr.shape[0]
    B = ATTN_BLOCK
    nb = S // B
    scale = HEAD_DIM ** -0.5
    ex = _ChipExchange(slabs, (), only_c)
    snd = _SiblingSend(send, send_dst)
    n, ns = ex.n, snd.n
    hosted = n + ns

    def body(*refs):
        q_ref, k_ref, v_ref, do_ref, l_ref, d_ref = refs[:6]
        x_ins, s_ins = refs[6:6 + n], refs[6 + n:6 + hosted]
        dq_ref, dk_ref, dv_ref = refs[6 + hosted:9 + hosted]
        x_outs, s_outs = refs[9 + hosted:9 + hosted + n], refs[9 + hosted + n:9 + 2 * hosted]
        sems = refs[9 + 2 * hosted:]
        x_sems, s_sems = sems[:len(ex.scratch)], sems[len(ex.scratch):]
        if hosted:
            @pl.when(pl.program_id(0) == 0)
            def _():
                if n:
                    ex.start(x_ins, x_outs, x_sems)
                if ns:
                    snd.start(s_ins, s_outs, s_sems)

        qi = lax.broadcasted_iota(jnp.int32, (B, 2 * B), 0)
        ki = lax.broadcasted_iota(jnp.int32, (B, 2 * B), 1)
        dist1 = qi + B - ki
        band1 = (dist1 >= 0) & (dist1 <= B)
        ri = lax.broadcasted_iota(jnp.int32, (2 * B, B), 0)
        ci = lax.broadcasted_iota(jnp.int32, (2 * B, B), 1)
        dist2 = ri - ci
        band2 = (dist2 >= 0) & (dist2 <= B)
        for bi, d in enumerate(DILATIONS):
            bpc = nb // d

            def blk(b, carry, bi=bi, d=d, bpc=bpc):
                c, n = b // bpc, b % bpc
                start = c + d * B * n
                pstart = c + d * B * jnp.maximum(n - 1, 0)
                nstart = c + d * B * jnp.minimum(n + 1, bpc - 1)
                valid1 = band1 & ((ki >= B) | (n > 0))
                valid2 = band2 & ((ri < B) | (n + 1 < bpc))
                q_c, q_n = _rows(q_ref, start, d), _rows(q_ref, nstart, d)
                k_p, k_c = _rows(k_ref, pstart, d), _rows(k_ref, start, d)
                v_p, v_c = _rows(v_ref, pstart, d), _rows(v_ref, start, d)
                do_c, do_n = _rows(do_ref, start, d), _rows(do_ref, nstart, d)
                l_c, l_n = _rows(l_ref, start, d), _rows(l_ref, nstart, d)
                d_c, d_n = _rows(d_ref, start, d), _rows(d_ref, nstart, d)
                qc = q_c.astype(bf16)
                qcat = jnp.concatenate([q_c, q_n], axis=0).astype(bf16)
                kc = k_c.astype(bf16)
                kcat = jnp.concatenate([k_p, k_c], axis=0).astype(bf16)
                vc = v_c.astype(bf16)
                vcat = jnp.concatenate([v_p, v_c], axis=0).astype(bf16)
                doc = do_c.astype(bf16)
                docat = jnp.concatenate([do_c, do_n], axis=0).astype(bf16)
                lcat = jnp.concatenate([l_c, l_n], axis=0)
                dcat = jnp.concatenate([d_c, d_n], axis=0)
                dqs, dks, dvs = [], [], []
                for h in range(2):
                    sl = slice(h * HEAD_DIM, (h + 1) * HEAD_DIM)
                    c0 = h * HEAD_DIM
                    s1 = _dot(qc[:, sl], kcat[:, sl], NT) * scale
                    p1 = jnp.where(valid1, jnp.exp(s1 - l_c[:, c0:c0 + 1]), 0.0)
                    dp1 = _dot(doc[:, sl], vcat[:, sl], NT)
                    ds1 = p1 * (dp1 - d_c[:, c0:c0 + 1]) * scale
                    dqs.append(_dot(ds1.astype(bf16), kcat[:, sl], NN))
                    s2 = _dot(qcat[:, sl], kc[:, sl], NT) * scale
                    p2 = jnp.where(valid2, jnp.exp(s2 - lcat[:, c0:c0 + 1]), 0.0)
                    dvs.append(_dot(p2.astype(bf16), docat[:, sl], TN))
                    dp2 = _dot(docat[:, sl], vc[:, sl], NT)
                    ds2 = p2 * (dp2 - dcat[:, c0:c0 + 1]) * scale
                    dks.append(_dot(ds2.astype(bf16), qcat[:, sl], TN))
                for ref, parts in ((dq_ref, dqs), (dk_ref, dks), (dv_ref, dvs)):
                    new = jnp.concatenate(parts, axis=1)
                    if bi > 0:
                        new = new + _rows(ref, start, d)
                    _set_rows(ref, start, d, new)
                return carry

            lax.fori_loop(0, nb, blk, 0, unroll=4)

        if hosted:
            @pl.when(pl.program_id(0) == pl.num_programs(0) - 1)
            def _():
                if ns:
                    snd.finish(s_ins, s_outs, s_sems)
                if n:
                    ex.finish(x_ins, x_outs, x_sems)

    pair = _pair_spec(S, 0)
    res = pl.pallas_call(
        body, name="attn_bwd_exchange" if hosted else "attn_bwd", grid=(3,),
        in_specs=[pair, pair, _pair_spec(S, 2 * ATTN_WIDTH // LANES), pair, pair, pair] + ex.in_specs + snd.in_specs,
        out_specs=[pair] * 3 + ex.out_specs + snd.out_specs,
        out_shape=[jax.ShapeDtypeStruct((S, ATTN_WIDTH), f32)] * 3 + ex.out_shape + snd.out_shape,
        scratch_shapes=ex.scratch + snd.scratch, compiler_params=_cp("arbitrary"),
    )(qr, kr, proj, dattn, ltot, delta, *slabs, *send)
    return res[0], res[1], res[2], list(res[3:3 + n]), list(res[3 + n:])


def _softplus_neg(lam):
    return jnp.maximum(-lam, 0.0) + jnp.log1p(jnp.exp(-jnp.abs(lam)))


PROJ_LRU_X, PROJ_LRU_GATE, PROJ_S5_U = 3, 4, 5
EARLY_OWNER = 1


def _lru_pre(proj, conv_w, conv_b, wr, br, wi, bi, lam):
    S, W = proj.shape[0], LRU_WIDTH
    rt = _pick(S, ROW_TILE, SUBLANES)
    K = conv_w.shape[0]

    def body(x_ref, xp_ref, cw_ref, cb_ref, wr_ref, br_ref, wi_ref, bi_ref, lam_ref,
             xc_ref, r_ref, i_ref, la_ref, u_ref):
        prev = jnp.where(pl.program_id(0) == 0, 0.0, xp_ref[...])
        x = x_ref[...]
        xc = cb_ref[...] + cw_ref[K - 1:K, :] * x
        for k in range(K - 1):
            xc = xc + cw_ref[k:k + 1, :] * _shift_down(x, prev, K - 1 - k)
        xb = xc.astype(bf16)
        r = _sigmoid(_dot(xb, wr_ref[...], NN) + br_ref[...])
        i = _sigmoid(_dot(xb, wi_ref[...], NN) + bi_ref[...])
        log_a = -LRU_C * r * _softplus_neg(lam_ref[...])
        u = jnp.sqrt(-_expm1(2.0 * log_a)) * (i * xc)
        xc_ref[...], r_ref[...], i_ref[...], la_ref[...], u_ref[...] = xc, r, i, log_a, u

    row = pl.BlockSpec((rt, W), lambda i: (i, 0))
    xrow = pl.BlockSpec((rt, W), lambda i: (i, PROJ_LRU_X))
    halo = _prev_halo_spec(rt, W, lambda i: PROJ_LRU_X)
    vec = pl.BlockSpec((1, W), lambda i: (0, 0))
    return pl.pallas_call(
        body, name="lru_pre", grid=(S // rt,),
        in_specs=[xrow, halo, pl.BlockSpec((K, W), lambda i: (0, 0)), vec,
                  pl.BlockSpec((W, W), lambda i: (0, 0)), vec, pl.BlockSpec((W, W), lambda i: (0, 0)), vec, vec],
        out_specs=[row] * 5, out_shape=[jax.ShapeDtypeStruct((S, W), f32)] * 5, compiler_params=_cp("parallel"),
    )(proj, proj, conv_w, conv_b.reshape(1, W), wr, br.reshape(1, W), wi, bi.reshape(1, W), lam.reshape(1, W))


def _tile_rows(shape):
    return lax.broadcasted_iota(jnp.int32, shape, 0)


def _lru_scan(log_a, u, proj):
    S, W = u.shape
    rt = _pick(S, ROW_TILE, SUBLANES)
    T = SUBLANES

    def body(la_ref, u_ref, g_ref, h_ref, o_ref, carry):
        @pl.when(pl.program_id(0) == 0)
        def _():
            carry[...] = jnp.zeros_like(carry)

        row = _tile_rows((T, W))

        def step(t, hp):
            r0 = pl.multiple_of(t * T, T)
            a = jnp.exp(la_ref[pl.ds(r0, T), :])
            x = u_ref[pl.ds(r0, T), :]
            for k in (1, 2, 4):
                x = x + a * jnp.where(row >= k, pltpu.roll(x, k, 0), 0.0)
                a = a * jnp.where(row >= k, pltpu.roll(a, k, 0), 1.0)
            h = x + a * hp
            h_ref[pl.ds(r0, T), :] = h
            o_ref[pl.ds(r0, T), :] = h * _gelu(g_ref[pl.ds(r0, T), :])
            return h[T - 1:T, :]

        carry[0:1, :] = lax.fori_loop(0, rt // T, step, carry[0:1, :])

    row = pl.BlockSpec((rt, W), lambda i: (i, 0))
    grow = pl.BlockSpec((rt, W), lambda i: (i, PROJ_LRU_GATE))
    return pl.pallas_call(
        body, name="lru_scan", grid=(S // rt,), in_specs=[row, row, grow], out_specs=[row] * 2,
        out_shape=[jax.ShapeDtypeStruct((S, W), f32)] * 2, scratch_shapes=[pltpu.VMEM((T, W), f32)],
        compiler_params=_cp("arbitrary"),
    )(log_a, u, proj)


def _lru_scan_bwd(dlru, proj, h, log_a):
    S, W = h.shape
    rt = _pick(S, ROW_TILE, SUBLANES)
    T = SUBLANES
    nblk = S // rt

    def body(d_ref, g_ref, h_ref, la_ref, go_ref, dg_ref, carry):
        @pl.when(pl.program_id(0) == 0)
        def _():
            carry[...] = jnp.zeros_like(carry)

        row = _tile_rows((T, W))

        def step(j, c):
            gn, an = c
            t = rt // T - 1 - j
            r0 = pl.multiple_of(t * T, T)
            d = d_ref[pl.ds(r0, T), :]
            gate = g_ref[pl.ds(r0, T), :]
            a = jnp.exp(la_ref[pl.ds(r0, T), :])
            dg_ref[pl.ds(r0, T), :] = d * h_ref[pl.ds(r0, T), :] * _gelu_grad(gate)
            x = d * _gelu(gate)
            b = jnp.where(row < T - 1, pltpu.roll(a, T - 1, 0), an)
            for k in (1, 2, 4):
                x = x + b * jnp.where(row < T - k, pltpu.roll(x, T - k, 0), 0.0)
                b = b * jnp.where(row < T - k, pltpu.roll(b, T - k, 0), 1.0)
            g = x + b * gn
            go_ref[pl.ds(r0, T), :] = g
            return g[0:1, :], a[0:1, :]

        gn, an = lax.fori_loop(0, rt // T, step, (carry[0:1, :], carry[1:2, :]))
        carry[0:1, :] = gn
        carry[1:2, :] = an

    row = pl.BlockSpec((rt, W), lambda i: (nblk - 1 - i, 0))
    grow = pl.BlockSpec((rt, W), lambda i: (nblk - 1 - i, PROJ_LRU_GATE))
    return pl.pallas_call(
        body, name="lru_scan_bwd", grid=(nblk,), in_specs=[row, grow, row, row], out_specs=[row] * 2,
        out_shape=[jax.ShapeDtypeStruct((S, W), f32)] * 2, scratch_shapes=[pltpu.VMEM((T, W), f32)],
        compiler_params=_cp("arbitrary"),
    )(dlru, proj, h, log_a)


def _lru_gate_bwd(g, h, xc, r, i, log_a, wr, wi, lam):
    S, W = g.shape
    rt = _pick(S, ROW_TILE, SUBLANES)

    def body(g_ref, h_ref, hp_ref, xc_ref, r_ref, i_ref, la_ref, wr_ref, wi_ref, lam_ref,
             dxc_ref, dwr_ref, dwi_ref, acc_ref):
        @pl.when(pl.program_id(0) == 0)
        def _():
            dwr_ref[...] = jnp.zeros_like(dwr_ref)
            dwi_ref[...] = jnp.zeros_like(dwi_ref)
            acc_ref[...] = jnp.zeros_like(acc_ref)

        prev = jnp.where(pl.program_id(0) == 0, 0.0, hp_ref[...])
        gg, xc, r, i, log_a, lam = g_ref[...], xc_ref[...], r_ref[...], i_ref[...], la_ref[...], lam_ref[...]
        hm1 = _shift_down(h_ref[...], prev, 1)
        a = jnp.exp(log_a)
        s = jnp.sqrt(-_expm1(2.0 * log_a))
        da = gg * hm1
        di = gg * s * xc
        dxc = gg * s * i
        ds = gg * i * xc
        dlog_a = da * a - ds * (a * a / s)
        sp = _softplus_neg(lam)
        dr = dlog_a * (-LRU_C * sp)
        dsp = jnp.sum(dlog_a * (-LRU_C * r), axis=0, keepdims=True)
        dpr = dr * r * (1.0 - r)
        dpi = di * i * (1.0 - i)
        dprb, dpib, xb = dpr.astype(bf16), dpi.astype(bf16), xc.astype(bf16)
        dxc_ref[...] = dxc + _dot(dprb, wr_ref[...], NT) + _dot(dpib, wi_ref[...], NT)
        dwr_ref[...] += _dot(xb, dprb, TN)
        dwi_ref[...] += _dot(xb, dpib, TN)
        acc_ref[0:1, :] += jnp.sum(dpr, axis=0, keepdims=True)
        acc_ref[1:2, :] += jnp.sum(dpi, axis=0, keepdims=True)
        acc_ref[2:3, :] += dsp * (-_sigmoid(-lam))

    row = pl.BlockSpec((rt, W), lambda i: (i, 0))
    halo = _prev_halo_spec(rt, W, lambda i: 0)
    vec = pl.BlockSpec((1, W), lambda i: (0, 0))
    mat = pl.BlockSpec((W, W), lambda i: (0, 0))
    acc = pl.BlockSpec((SUBLANES, W), lambda i: (0, 0))
    return pl.pallas_call(
        body, name="lru_gate_bwd", grid=(S // rt,),
        in_specs=[row, row, halo, row, row, row, row, mat, mat, vec], out_specs=[row, mat, mat, acc],
        out_shape=[jax.ShapeDtypeStruct((S, W), f32), jax.ShapeDtypeStruct((W, W), f32),
                   jax.ShapeDtypeStruct((W, W), f32), jax.ShapeDtypeStruct((SUBLANES, W), f32)],
        compiler_params=_cp("arbitrary"),
    )(g, h, h, xc, r, i, log_a, wr, wi, lam.reshape(1, W))


def _conv_bwd(dy, x, conv_w, name, col_tile=None, out_dtype=f32, x_col_block=0, send=(), send_dst=0):
    if dy.ndim == 2:
        dy = dy[None]
    H, S, Ch = dy.shape
    C = H * Ch
    K = conv_w.shape[0]
    ct = Ch if col_tile is None else col_tile
    nct = Ch // ct
    rt = _pick(S, ROW_TILE, SUBLANES)
    nrt = S // rt
    snd = _SiblingSend(send, send_dst)
    n = snd.n

    def body(*refs):
        dy_ref, dyn_ref, x_ref, w_ref = refs[:4]
        s_ins = refs[4:4 + n]
        dx_ref, acc_ref = refs[4 + n:6 + n]
        s_outs, s_sems = refs[6 + n:6 + 2 * n], refs[6 + 2 * n:]
        i = pl.program_id(2)
        if n:
            @pl.when((pl.program_id(0) == 0) & (pl.program_id(1) == 0) & (i == 0))
            def _():
                snd.start(s_ins, s_outs, s_sems)

        @pl.when(i == 0)
        def _():
            acc_ref[...] = jnp.zeros_like(acc_ref)

        nxt = jnp.where(i == nrt - 1, 0.0, dyn_ref[...])
        dy, x = dy_ref[...], x_ref[...]
        ahead = [dy] + [_shift_up(dy, nxt, j) for j in range(1, K)]
        dx = w_ref[K - 1:K, :] * dy
        for k in range(K - 1):
            dx = dx + w_ref[k:k + 1, :] * ahead[K - 1 - k]
        dx_ref[...] = dx.astype(dx_ref.dtype)
        for k in range(K):
            acc_ref[k:k + 1, :] += jnp.sum(ahead[K - 1 - k] * x, axis=0, keepdims=True)
        acc_ref[K:K + 1, :] += jnp.sum(dy, axis=0, keepdims=True)
        if n:
            @pl.when((pl.program_id(0) == H - 1) & (pl.program_id(1) == nct - 1) & (i == nrt - 1))
            def _():
                snd.finish(s_ins, s_outs, s_sems)

    per, last = rt // SUBLANES, S // SUBLANES - 1
    dy_row = pl.BlockSpec((None, rt, ct), lambda h, j, i: (h, i, j))
    dy_next = pl.BlockSpec((None, SUBLANES, ct), lambda h, j, i: (h, jnp.minimum((i + 1) * per, last), j))
    row = pl.BlockSpec((rt, ct), lambda h, j, i: (i, h * nct + j))
    xrow = pl.BlockSpec((rt, ct), lambda h, j, i: (i, h * nct + j + x_col_block))
    res = pl.pallas_call(
        body, name=name, grid=(H, nct, nrt),
        in_specs=[dy_row, dy_next, xrow, pl.BlockSpec((K, ct), lambda h, j, i: (0, h * nct + j))] + snd.in_specs,
        out_specs=[row, pl.BlockSpec((SUBLANES, ct), lambda h, j, i: (0, h * nct + j))] + snd.out_specs,
        out_shape=[jax.ShapeDtypeStruct((S, C), out_dtype), jax.ShapeDtypeStruct((SUBLANES, C), f32)] + snd.out_shape,
        scratch_shapes=snd.scratch,
        compiler_params=_cp(*(("arbitrary",) * 3 if n else ("parallel", "parallel", "arbitrary"))),
    )(dy, dy, x, conv_w, *send)
    return res[0], res[1], list(res[2:])


def _s5_param_fn(a_re, a_im, ls, bt_re, bt_im):
    step = jnp.exp(ls)
    dt_re, dt_im = step * a_re, step * a_im
    mag = jnp.exp(dt_re)
    ab_re, ab_im = mag * jnp.cos(dt_im), mag * jnp.sin(dt_im)
    z_re, z_im = ab_re - 1.0, ab_im
    den = a_re * a_re + a_im * a_im
    f_re = (z_re * a_re + z_im * a_im) / den
    f_im = (z_im * a_re - z_re * a_im) / den
    bb_re = f_re[:, None, :] * bt_re - f_im[:, None, :] * bt_im
    bb_im = f_re[:, None, :] * bt_im + f_im[:, None, :] * bt_re
    return ab_re, ab_im, bb_re, bb_im


def _s5_params(a_re, a_im, ls, bt_re, bt_im):
    def body(ar, ai, l, br, bi, o_ar, o_ai, o_br, o_bi):
        o_ar[...], o_ai[...], o_br[...], o_bi[...] = _s5_param_fn(ar[...], ai[...], l[...], br[...], bi[...])

    return pl.pallas_call(
        body, name="s5_params",
        out_shape=[jax.ShapeDtypeStruct(a_re.shape, f32)] * 2 + [jax.ShapeDtypeStruct(bt_re.shape, f32)] * 2,
        compiler_params=_cp(),
    )(a_re, a_im, ls, bt_re, bt_im)


def _s5_params_bwd(a_re, a_im, ls, bt_re, bt_im, d_ar, d_ai, d_br, d_bi):
    def body(ar, ai, l, br, bi, c_ar, c_ai, c_br, c_bi, g_ar, g_ai, g_l, g_br, g_bi):
        _, vjp = jax.vjp(_s5_param_fn, ar[...], ai[...], l[...], br[...], bi[...])
        g_ar[...], g_ai[...], g_l[...], g_br[...], g_bi[...] = vjp((c_ar[...], c_ai[...], c_br[...], c_bi[...]))

    return pl.pallas_call(
        body, name="s5_params_bwd",
        out_shape=[jax.ShapeDtypeStruct(a_re.shape, f32)] * 2 + [jax.ShapeDtypeStruct(ls.shape, f32)]
        + [jax.ShapeDtypeStruct(bt_re.shape, f32)] * 2,
        compiler_params=_cp(),
    )(a_re, a_im, ls, bt_re, bt_im, d_ar, d_ai, d_br, d_bi)


S5_CHUNK = 256


def _s5_power_tables(ab_ref, p_ref, w_ref, conj):
    T, L = SUBLANES, S5_LANES
    are = ab_ref[0:1, 0:L]
    aim = ab_ref[0:1, L:2 * L]
    if conj:
        aim = -aim
    pre, pim = are, aim
    for n in range(3):
        p_ref[n:n + 1, 0:L] = pre
        p_ref[n:n + 1, L:2 * L] = pim
        pre, pim = pre * pre - pim * pim, 2.0 * pre * pim
    row = _tile_rows((T, L))
    wre = jnp.zeros((T, L), f32)
    wim = jnp.zeros((T, L), f32)
    pre, pim = are, aim
    for n in range(T):
        tgt = (T - 1 - n) if conj else n
        wre = jnp.where(row == tgt, pre, wre)
        wim = jnp.where(row == tgt, pim, wim)
        pre, pim = pre * are - pim * aim, pre * aim + pim * are
    w_ref[:, 0:L] = wre
    w_ref[:, L:2 * L] = wim


def _s5_scan(bu, ab):
    S, L2 = bu.shape
    L = L2 // 2
    rt = _pick(S, 256, SUBLANES)
    T = SUBLANES
    CH = S5_CHUNK

    def body(bu_ref, ab_ref, x_ref, p_ref, w_ref, carry):
        @pl.when(pl.program_id(0) == 0)
        def _():
            carry[...] = jnp.zeros_like(carry)
            _s5_power_tables(ab_ref, p_ref, w_ref, conj=False)

        row = _tile_rows((T, CH))

        def step(t, _):
            r0 = pl.multiple_of(t * T, T)
            for c in range(L // CH):
                lre, lim = pl.ds(c * CH, CH), pl.ds(L + c * CH, CH)
                xr, xi = bu_ref[pl.ds(r0, T), lre], bu_ref[pl.ds(r0, T), lim]
                for n, k in enumerate((1, 2, 4)):
                    pr, pi = p_ref[n:n + 1, lre], p_ref[n:n + 1, lim]
                    sr = jnp.where(row >= k, pltpu.roll(xr, k, 0), 0.0)
                    si = jnp.where(row >= k, pltpu.roll(xi, k, 0), 0.0)
                    xr, xi = xr + pr * sr - pi * si, xi + pr * si + pi * sr
                cr, ci = carry[T - 1:T, lre], carry[T - 1:T, lim]
                wr, wi = w_ref[:, lre], w_ref[:, lim]
                xr, xi = xr + wr * cr - wi * ci, xi + wr * ci + wi * cr
                carry[:, lre] = xr
                carry[:, lim] = xi
                x_ref[pl.ds(r0, T), lre] = xr
                x_ref[pl.ds(r0, T), lim] = xi
            return 0

        lax.fori_loop(0, rt // T, step, 0)

    row_spec = pl.BlockSpec((rt, L2), lambda i: (i, 0))
    return pl.pallas_call(
        body, name="s5_scan", grid=(S // rt,), in_specs=[row_spec, pl.BlockSpec((1, L2), lambda i: (0, 0))],
        out_specs=row_spec, out_shape=jax.ShapeDtypeStruct((S, L2), f32),
        scratch_shapes=[pltpu.VMEM((T, L2), f32), pltpu.VMEM((T, L2), f32), pltpu.VMEM((T, L2), f32)],
        compiler_params=_cp("arbitrary"),
    )(bu, ab)


def _s5_scan_bwd(dx, x, ab):
    S, L2 = dx.shape
    L = L2 // 2
    rt = _pick(S, 256, SUBLANES)
    T = SUBLANES
    CH = S5_CHUNK
    nblk = S // rt
    per = rt // T

    def body(dx_ref, x_ref, xp_ref, ab_ref, g_ref, da_ref, p_ref, w_ref, carry, acc):
        pid = pl.program_id(0)

        @pl.when(pid == 0)
        def _():
            carry[...] = jnp.zeros_like(carry)
            acc[...] = jnp.zeros_like(acc)
            _s5_power_tables(ab_ref, p_ref, w_ref, conj=True)

        row = _tile_rows((T, CH))
        first_block = pid == nblk - 1

        def step(j, _):
            t = per - 1 - j
            r0 = pl.multiple_of(t * T, T)
            rp = pl.multiple_of(jnp.maximum(t - 1, 0) * T, T)
            for c in range(L // CH):
                lre, lim = pl.ds(c * CH, CH), pl.ds(L + c * CH, CH)
                gr, gi = dx_ref[pl.ds(r0, T), lre], dx_ref[pl.ds(r0, T), lim]
                for n, k in enumerate((1, 2, 4)):
                    pr, pi = p_ref[n:n + 1, lre], p_ref[n:n + 1, lim]
                    sr = jnp.where(row < T - k, pltpu.roll(gr, T - k, 0), 0.0)
                    si = jnp.where(row < T - k, pltpu.roll(gi, T - k, 0), 0.0)
                    gr, gi = gr + pr * sr - pi * si, gi + pr * si + pi * sr
                cr, ci = carry[0:1, lre], carry[0:1, lim]
                wr, wi = w_ref[:, lre], w_ref[:, lim]
                gr, gi = gr + wr * cr - wi * ci, gi + wr * ci + wi * cr
                carry[:, lre] = gr
                carry[:, lim] = gi
                g_ref[pl.ds(r0, T), lre] = gr
                g_ref[pl.ds(r0, T), lim] = gi
                xr, xi = x_ref[pl.ds(r0, T), lre], x_ref[pl.ds(r0, T), lim]
                in_blk_r, in_blk_i = x_ref[pl.ds(rp, T), lre], x_ref[pl.ds(rp, T), lim]
                hal_r = jnp.where(first_block, 0.0, xp_ref[:, lre])
                hal_i = jnp.where(first_block, 0.0, xp_ref[:, lim])
                pvr = jnp.where(t == 0, hal_r, in_blk_r)[T - 1:T, :]
                pvi = jnp.where(t == 0, hal_i, in_blk_i)[T - 1:T, :]
                sxr = jnp.where(row >= 1, pltpu.roll(xr, 1, 0), pvr)
                sxi = jnp.where(row >= 1, pltpu.roll(xi, 1, 0), pvi)
                acc[:, lre] += gr * sxr + gi * sxi
                acc[:, lim] += gi * sxr - gr * sxi
            return 0

        lax.fori_loop(0, per, step, 0)

        @pl.when(pid == nblk - 1)
        def _():
            da_ref[...] = jnp.sum(acc[...], axis=0, keepdims=True)

    row_spec = pl.BlockSpec((rt, L2), lambda i: (nblk - 1 - i, 0))
    halo = pl.BlockSpec((T, L2), lambda i: (jnp.maximum((nblk - 1 - i) * per - 1, 0), 0))
    vec = pl.BlockSpec((1, L2), lambda i: (0, 0))
    return pl.pallas_call(
        body, name="s5_scan_bwd", grid=(nblk,), in_specs=[row_spec, row_spec, halo, vec],
        out_specs=[row_spec, vec],
        out_shape=[jax.ShapeDtypeStruct((S, L2), f32), jax.ShapeDtypeStruct((1, L2), f32)],
        scratch_shapes=[pltpu.VMEM((T, L2), f32)] * 4,
        compiler_params=_cp("arbitrary"),
    )(dx, x, x, ab)


def _S5_U_SPEC(rt):
    return pl.BlockSpec((rt, LRU_WIDTH), lambda i: (i, PROJ_S5_U))


def _s5_out(yc, proj, d, wglu, bglu):
    S, W = yc.shape
    rt = _pick(S, ROW_TILE, SUBLANES)

    def body(yc_ref, u_ref, d_ref, w_ref, b_ref, o_ref, y_ref):
        y = yc_ref[...] + d_ref[...] * u_ref[:, 0:W]
        yg = _gelu(y)
        z = _dot(yg.astype(bf16), w_ref[...], NN) + b_ref[...]
        o_ref[...] = yg * _sigmoid(z)
        y_ref[...] = y

    row = pl.BlockSpec((rt, W), lambda i: (i, 0))
    vec = pl.BlockSpec((1, W), lambda i: (0, 0))
    mat = pl.BlockSpec((W, W), lambda i: (0, 0))
    return pl.pallas_call(
        body, name="s5_out", grid=(S // rt,), in_specs=[row, _S5_U_SPEC(rt), vec, mat, vec], out_specs=[row, row],
        out_shape=[jax.ShapeDtypeStruct((S, W), f32)] * 2, compiler_params=_cp("parallel"),
    )(yc, proj, d.reshape(1, W), wglu, bglu.reshape(1, W))


def _s5_out_bwd(dssm, y, proj, d, wglu, bglu):
    S, W = y.shape
    rt = _pick(S, ROW_TILE, SUBLANES)

    def body(do_ref, y_ref, u_ref, d_ref, w_ref, b_ref, dy_ref, du_ref, dw_ref, acc_ref):
        @pl.when(pl.program_id(0) == 0)
        def _():
            dw_ref[...] = jnp.zeros_like(dw_ref)
            acc_ref[...] = jnp.zeros_like(acc_ref)

        do, y = do_ref[...], y_ref[...]
        yg = _gelu(y)
        ygb = yg.astype(bf16)
        sg = _sigmoid(_dot(ygb, w_ref[...], NN) + b_ref[...])
        dz = do * yg * sg * (1.0 - sg)
        dzb = dz.astype(bf16)
        dyg = do * sg + _dot(dzb, w_ref[...], NT)
        dy = dyg * _gelu_grad(y)
        dy_ref[...] = dy
        du_ref[...] = dy * d_ref[...]
        dw_ref[...] += _dot(ygb, dzb, TN)
        acc_ref[0:1, :] += jnp.sum(dz, axis=0, keepdims=True)
        acc_ref[1:2, :] += jnp.sum(dy * u_ref[:, 0:W], axis=0, keepdims=True)

    row = pl.BlockSpec((rt, W), lambda i: (i, 0))
    vec = pl.BlockSpec((1, W), lambda i: (0, 0))
    mat = pl.BlockSpec((W, W), lambda i: (0, 0))
    acc = pl.BlockSpec((SUBLANES, W), lambda i: (0, 0))
    return pl.pallas_call(
        body, name="s5_out_bwd", grid=(S // rt,), in_specs=[row, row, _S5_U_SPEC(rt), vec, mat, vec],
        out_specs=[row, row, mat, acc],
        out_shape=[jax.ShapeDtypeStruct((S, W), f32)] * 2
        + [jax.ShapeDtypeStruct((W, W), f32), jax.ShapeDtypeStruct((SUBLANES, W), f32)],
        compiler_params=_cp("arbitrary"),
    )(dssm, y, proj, d.reshape(1, W), wglu, bglu.reshape(1, W))


MIX_SPLITS = ((0, ATTN_WIDTH), (ATTN_WIDTH, ATTN_WIDTH + LRU_WIDTH), (ATTN_WIDTH + LRU_WIDTH, D_MODEL))


def _mixnorm(attn, lru, ssm, g):
    S = attn.shape[0]
    rt = _pick(S, ROW_TILE, SUBLANES)

    def body(a_ref, l_ref, s_ref, g_ref, o_ref):
        for ref, (lo, hi) in zip((a_ref, l_ref, s_ref), MIX_SPLITS):
            x = ref[...]
            ms = jnp.mean(x * x, axis=-1, keepdims=True)
            o_ref[:, lo:hi] = (x * lax.rsqrt(ms + RMS_EPS) * g_ref[:, lo:hi]).astype(o_ref.dtype)

    rows = [pl.BlockSpec((rt, hi - lo), lambda i: (i, 0)) for lo, hi in MIX_SPLITS]
    return pl.pallas_call(
        body, name="mixnorm", grid=(S // rt,), in_specs=rows + [pl.BlockSpec((1, D_MODEL), lambda i: (0, 0))],
        out_specs=pl.BlockSpec((rt, D_MODEL), lambda i: (i, 0)),
        out_shape=jax.ShapeDtypeStruct((S, D_MODEL), bf16), compiler_params=_cp("parallel"),
    )(attn, lru, ssm, g.reshape(1, D_MODEL))


def _mixnorm_bwd(dmixed, attn, lru, ssm, g):
    S = attn.shape[0]
    rt = _pick(S, ROW_TILE, SUBLANES)

    def body(d_ref, a_ref, l_ref, s_ref, g_ref, da_ref, dl_ref, ds_ref, dlt_ref, acc_ref):
        @pl.when(pl.program_id(0) == 0)
        def _():
            acc_ref[...] = jnp.zeros_like(acc_ref)

        outs = []
        for ref, (lo, hi) in zip((a_ref, l_ref, s_ref), MIX_SPLITS):
            x = ref[...]
            dy = d_ref[:, lo:hi]
            rinv = lax.rsqrt(jnp.mean(x * x, axis=-1, keepdims=True) + RMS_EPS)
            dyg = dy * g_ref[:, lo:hi]
            outs.append(rinv * dyg - x * (rinv * rinv * rinv) * jnp.mean(dyg * x, axis=-1, keepdims=True))
            acc_ref[0:1, lo:hi] += jnp.sum(dy * x * rinv, axis=0, keepdims=True)
        da_ref[...], dl_ref[...], ds_ref[...] = outs
        hi_ = lax.broadcasted_iota(jnp.int32, (ATTN_WIDTH, ATTN_WIDTH), 0) // HEAD_DIM
        hj_ = lax.broadcasted_iota(jnp.int32, (ATTN_WIDTH, ATTN_WIDTH), 1) // HEAD_DIM
        same = jnp.where(hi_ == hj_, 1.0, 0.0).astype(f32)
        dlt_ref[...] = jnp.dot(outs[0] * a_ref[...], same, precision=lax.Precision.HIGHEST, preferred_element_type=f32)

    rows = [pl.BlockSpec((rt, hi - lo), lambda i: (i, 0)) for lo, hi in MIX_SPLITS]
    full = pl.BlockSpec((rt, D_MODEL), lambda i: (i, 0))
    return pl.pallas_call(
        body, name="mixnorm_bwd", grid=(S // rt,),
        in_specs=[full] + rows + [pl.BlockSpec((1, D_MODEL), lambda i: (0, 0))],
        out_specs=rows + [rows[0], pl.BlockSpec((SUBLANES, D_MODEL), lambda i: (0, 0))],
        out_shape=[jax.ShapeDtypeStruct((S, hi - lo), f32) for lo, hi in MIX_SPLITS]
        + [jax.ShapeDtypeStruct((S, ATTN_WIDTH), f32), jax.ShapeDtypeStruct((SUBLANES, D_MODEL), f32)],
        compiler_params=_cp("arbitrary"),
    )(dmixed, attn, lru, ssm, g.reshape(1, D_MODEL))


FFN_COL_TILE = 1536


def _ffn_conv(x, prev, w_ref, b_ref, K):
    y = b_ref[...] + w_ref[K - 1:K, :] * x
    for k in range(K - 1):
        y = y + w_ref[k:k + 1, :] * _shift_down(x, prev, K - 1 - k)
    return y


def _ffn_act(up, conv_w, conv_b):
    S, C2 = up.shape
    C = C2 // 2
    K = conv_w.shape[0]
    ct = FFN_COL_TILE
    nct = C // ct
    rt = _pick(S, ROW_TILE, SUBLANES)

    def body(g_ref, gp_ref, v_ref, vp_ref, wg_ref, wv_ref, bg_ref, bv_ref, o_ref):
        first = pl.program_id(1) == 0
        gate = _ffn_conv(g_ref[...], jnp.where(first, 0.0, gp_ref[...]), wg_ref, bg_ref, K)
        val = _ffn_conv(v_ref[...], jnp.where(first, 0.0, vp_ref[...]), wv_ref, bv_ref, K)
        o_ref[...] = (_gelu(gate) * val).astype(o_ref.dtype)

    def specs(off):
        return (pl.BlockSpec((rt, ct), lambda j, i: (i, j + off)), _prev_halo_spec(rt, ct, lambda j, i: j + off))

    def wspec(off, rows):
        return pl.BlockSpec((rows, ct), lambda j, i: (0, j + off))

    g_s, gp_s = specs(0)
    v_s, vp_s = specs(nct)
    return pl.pallas_call(
        body, name="ffn_act", grid=(nct, S // rt),
        in_specs=[g_s, gp_s, v_s, vp_s, wspec(0, K), wspec(nct, K), wspec(0, 1), wspec(nct, 1)],
        out_specs=pl.BlockSpec((rt, ct), lambda j, i: (i, j)),
        out_shape=jax.ShapeDtypeStruct((S, C), bf16), compiler_params=_cp("parallel", "parallel"),
    )(up, up, up, up, conv_w, conv_w, conv_b.reshape(1, C2), conv_b.reshape(1, C2))


def _ffn_act_bwd(dact, up, conv_w, conv_b):
    S, C2 = up.shape
    C = C2 // 2
    K = conv_w.shape[0]
    ct = FFN_COL_TILE
    nct = C // ct
    rt = _pick(S, ROW_TILE, SUBLANES)

    def body(d_ref, g_ref, gp_ref, v_ref, vp_ref, wg_ref, wv_ref, bg_ref, bv_ref, o_ref):
        first = pl.program_id(1) == 0
        gate = _ffn_conv(g_ref[...], jnp.where(first, 0.0, gp_ref[...]), wg_ref, bg_ref, K)
        val = _ffn_conv(v_ref[...], jnp.where(first, 0.0, vp_ref[...]), wv_ref, bv_ref, K)
        d = d_ref[...]
        gl, dgl = _gelu_pair(gate)
        o_ref[0] = d * val * dgl
        o_ref[1] = d * gl

    def specs(off):
        return (pl.BlockSpec((rt, ct), lambda j, i: (i, j + off)), _prev_halo_spec(rt, ct, lambda j, i: j + off))

    def wspec(off, rows):
        return pl.BlockSpec((rows, ct), lambda j, i: (0, j + off))

    g_s, gp_s = specs(0)
    v_s, vp_s = specs(nct)
    return pl.pallas_call(
        body, name="ffn_act_bwd", grid=(nct, S // rt),
        in_specs=[pl.BlockSpec((rt, ct), lambda j, i: (i, j)), g_s, gp_s, v_s, vp_s,
                  wspec(0, K), wspec(nct, K), wspec(0, 1), wspec(nct, 1)],
        out_specs=pl.BlockSpec((2, rt, ct), lambda j, i: (0, i, j)),
        out_shape=jax.ShapeDtypeStruct((2, S, C), f32), compiler_params=_cp("parallel", "parallel"),
    )(dact, up, up, up, up, conv_w, conv_w, conv_b.reshape(1, C2), conv_b.reshape(1, C2))


ANY = pl.BlockSpec(memory_space=pl.ANY)


def _rows_for(cols):
    return max(16, (1 << 19) // cols)


def _chips(x, y):
    return [(1 - x, y), (x, 1 - y), (1 - x, 1 - y)]


class _Gather:
    def __init__(self, shards, split):
        self.shapes = [s.shape for s in shards]
        self.dtypes = [s.dtype for s in shards]
        self.split = list(split)
        self.nt = len(shards)
        self.in_specs = [ANY] * self.nt
        self.out_specs = [ANY] * self.nt
        self.out_shape = [jax.ShapeDtypeStruct((4,) + s, dt) for s, dt in zip(self.shapes, self.dtypes)]
        self.scratch = [pltpu.SemaphoreType.DMA((3, self.nt))] * 4 if self.nt else []

    def _part(self, ref, t, half):
        if not self.split[t]:
            return ref
        r = self.shapes[t][0] // 2
        return ref.at[pl.ds(half * r, r), :]

    def _ici(self, ins, outs, sems, k, t, chip, landing_chip):
        x, y, c = lax.axis_index("x"), lax.axis_index("y"), lax.axis_index("c")
        return pltpu.make_async_remote_copy(
            src_ref=self._part(ins[t], t, c), dst_ref=self._part(outs[t].at[landing_chip], t, c),
            send_sem=sems[0].at[k, t], recv_sem=sems[1].at[k, t], device_id=(chip[0], chip[1], c), device_id_type=MESH)

    def _d2d(self, outs, sems, k, t, q, half):
        x, y, c = lax.axis_index("x"), lax.axis_index("y"), lax.axis_index("c")
        rows = self._part(outs[t].at[q], t, half)
        return pltpu.make_async_remote_copy(
            src_ref=rows, dst_ref=rows, send_sem=sems[2].at[k, t], recv_sem=sems[3].at[k, t],
            device_id=(x, y, 1 - c), device_id_type=MESH)

    def start(self, ins, outs, sems):
        x, y = lax.axis_index("x"), lax.axis_index("y")
        me = 2 * x + y
        for k, chip in enumerate(_chips(x, y)):
            for t in range(self.nt):
                self._ici(ins, outs, sems, k, t, chip, me).start()

    def finish(self, ins, outs, sems):
        x, y, c = lax.axis_index("x"), lax.axis_index("y"), lax.axis_index("c")
        me = 2 * x + y
        chips = _chips(x, y)
        for k, chip in enumerate(chips):
            q = 2 * chip[0] + chip[1]
            for t in range(self.nt):
                self._ici(ins, outs, sems, k, t, chip, q).wait_recv()
                if self.split[t]:
                    self._d2d(outs, sems, k, t, q, c).start()
        for k, chip in enumerate(chips):
            q = 2 * chip[0] + chip[1]
            for t in range(self.nt):
                if self.split[t]:
                    self._d2d(outs, sems, k, t, q, 1 - c).wait_recv()
        for k, chip in enumerate(chips):
            q = 2 * chip[0] + chip[1]
            for t in range(self.nt):
                self._ici(ins, outs, sems, k, t, chip, me).wait_send()
                if self.split[t]:
                    self._d2d(outs, sems, k, t, q, c).wait_send()


def _gather_weights(shards, split):
    g = _Gather(shards, split)
    nt = g.nt

    def body(*refs):
        ins, outs, sems = refs[:nt], refs[nt:2 * nt], refs[2 * nt:]
        g.start(ins, outs, sems)
        g.finish(ins, outs, sems)

    return pl.pallas_call(
        body, name="gather_weights", in_specs=g.in_specs, out_specs=g.out_specs, out_shape=g.out_shape,
        scratch_shapes=g.scratch,
    )(*shards)


class _SiblingSend:
    def __init__(self, gs, dst_c, swap=()):
        self.nt, self.n = len(gs), len(gs) + len(swap)
        self.dst_c = list(dst_c) if isinstance(dst_c, (list, tuple)) else [dst_c] * self.nt
        self.in_specs = [ANY] * self.n
        self.out_specs = [ANY] * self.n
        self.out_shape = [jax.ShapeDtypeStruct(g.shape, g.dtype) for g in list(gs) + list(swap)]
        self.scratch = [pltpu.SemaphoreType.DMA((self.n,))] * 2 if self.n else []

    def _each(self, ins, outs, sems, sender, fn):
        x, y, c = lax.axis_index("x"), lax.axis_index("y"), lax.axis_index("c")

        def cp(t):
            return pltpu.make_async_remote_copy(
                src_ref=ins[t], dst_ref=outs[t], send_sem=sems[0].at[t], recv_sem=sems[1].at[t],
                device_id=(x, y, 1 - c), device_id_type=MESH)

        for dst in (0, 1):
            which = [t for t in range(self.nt) if self.dst_c[t] == dst]
            if which:
                @pl.when((c != dst) if sender else (c == dst))
                def _(which=which):
                    for t in which:
                        fn(cp(t))
        for t in range(self.nt, self.n):
            fn(cp(t))

    def start(self, ins, outs, sems):
        self._each(ins, outs, sems, True, lambda cp: cp.start())

    def finish(self, ins, outs, sems):
        self._each(ins, outs, sems, False, lambda cp: cp.wait_recv())
        self._each(ins, outs, sems, True, lambda cp: cp.wait_send())


def _sibling_send(gs, dst_c, swap=()):
    snd = _SiblingSend(gs, dst_c, swap)
    n = snd.n

    def body(*refs):
        ins, outs, sems = refs[:n], refs[n:2 * n], refs[2 * n:]
        snd.start(ins, outs, sems)
        snd.finish(ins, outs, sems)

    res = pl.pallas_call(
        body, name="sibling_send", in_specs=snd.in_specs, out_specs=snd.out_specs, out_shape=snd.out_shape,
        scratch_shapes=snd.scratch,
    )(*gs, *swap)
    return list(res[:snd.nt]), list(res[snd.nt:])


def _owner_flag(owner_c):
    return (lax.axis_index("c") == owner_c).astype(jnp.int32).reshape(1)


def _pair_sum(g, other, name, owner_c, col_slabs=False):
    R, C = g.shape
    cb = C // 4 if col_slabs else C
    rt = _pick(R, _rows_for(cb), 16)

    def body(on_ref, a_ref, o_ref, out_ref):
        out_ref[...] = (a_ref[...] + o_ref[...]).astype(out_ref.dtype)

    row = pl.BlockSpec((rt, cb), lambda q, i, on: (i * on[0], q * on[0]))
    if col_slabs:
        out_spec = pl.BlockSpec((None, rt, cb), lambda q, i, on: (q * on[0], i * on[0], 0))
        out_shape = jax.ShapeDtypeStruct((4, R, cb), bf16)
    else:
        out_spec, out_shape = row, jax.ShapeDtypeStruct((R, C), bf16)
    return pl.pallas_call(
        body, name=name,
        grid_spec=pltpu.PrefetchScalarGridSpec(num_scalar_prefetch=1, grid=(C // cb, R // rt),
                                               in_specs=[row, row], out_specs=out_spec),
        out_shape=out_shape, compiler_params=_cp("arbitrary", "arbitrary"),
    )(_owner_flag(owner_c), g, other)


class _ChipExchange:
    def __init__(self, slabs, whole, only_c):
        self.ns, self.nw = len(slabs), len(whole)
        self.only_c = list(only_c) if isinstance(only_c, (list, tuple)) else [only_c] * self.ns
        self.n = self.ns + self.nw
        self.in_specs = [ANY] * self.n
        self.out_specs = [ANY] * self.n
        self.out_shape = ([jax.ShapeDtypeStruct(s.shape, s.dtype) for s in slabs]
                          + [jax.ShapeDtypeStruct((4,) + w.shape, w.dtype) for w in whole])
        self.scratch = [pltpu.SemaphoreType.DMA((3, self.n))] * 2 if self.n else []

    def _copy(self, ins, outs, sems, k, t, chip, landing_chip):
        c = lax.axis_index("c")
        src = ins[t].at[2 * chip[0] + chip[1]] if t < self.ns else ins[t]
        return pltpu.make_async_remote_copy(
            src_ref=src, dst_ref=outs[t].at[landing_chip], send_sem=sems[0].at[k, t], recv_sem=sems[1].at[k, t],
            device_id=(chip[0], chip[1], c), device_id_type=MESH)

    def _each(self, fn):
        x, y, c = lax.axis_index("x"), lax.axis_index("y"), lax.axis_index("c")
        chips = _chips(x, y)
        for owner in (0, 1):
            which = [t for t in range(self.ns) if self.only_c[t] == owner]
            if which:
                @pl.when(c == owner)
                def _(which=which):
                    for k, chip in enumerate(chips):
                        for t in which:
                            fn(k, t, chip)
        for k, chip in enumerate(chips):
            for t in range(self.ns, self.n):
                fn(k, t, chip)

    def start(self, ins, outs, sems):
        me = 2 * lax.axis_index("x") + lax.axis_index("y")
        self._each(lambda k, t, chip: self._copy(ins, outs, sems, k, t, chip, me).start())

    def finish(self, ins, outs, sems):
        me = 2 * lax.axis_index("x") + lax.axis_index("y")
        self._each(lambda k, t, chip: self._copy(ins, outs, sems, k, t, chip, 2 * chip[0] + chip[1]).wait_recv())
        self._each(lambda k, t, chip: self._copy(ins, outs, sems, k, t, chip, me).wait_send())


def _chip_exchange(slabs, whole, only_c):
    ex = _ChipExchange(slabs, whole, only_c)
    n = ex.n

    def body(*refs):
        ins, outs, sems = refs[:n], refs[n:2 * n], refs[2 * n:]
        ex.start(ins, outs, sems)
        ex.finish(ins, outs, sems)

    res = pl.pallas_call(
        body, name="chip_exchange", in_specs=ex.in_specs, out_specs=ex.out_specs, out_shape=ex.out_shape,
        scratch_shapes=ex.scratch,
    )(*slabs, *whole)
    return list(res[:ex.ns]), list(res[ex.ns:])


def _sum_chips(recv, own, name, owner_c=None):
    n, r, C = recv.shape
    rt = _pick(r, _rows_for(C), 16)
    own3 = own.ndim == 3

    def body(on_ref, r_ref, o_ref, out_ref):
        me = 2 * lax.axis_index("x") + lax.axis_index("y")
        acc = None
        for q in range(n):
            term = jnp.where(me == q, o_ref[q] if own3 else o_ref[...], r_ref[q]).astype(f32)
            acc = term if acc is None else acc + term
        out_ref[...] = acc

    blk = pl.BlockSpec((n, rt, C), lambda i, on: (0, i * on[0], 0))
    row = pl.BlockSpec((rt, C), lambda i, on: (i * on[0], 0))
    flag = jnp.ones((1,), jnp.int32) if owner_c is None else _owner_flag(owner_c)
    return pl.pallas_call(
        body, name=name,
        grid_spec=pltpu.PrefetchScalarGridSpec(num_scalar_prefetch=1, grid=(r // rt,),
                                               in_specs=[blk, blk if own3 else row], out_specs=row),
        out_shape=jax.ShapeDtypeStruct((r, C), f32), compiler_params=_cp("arbitrary"),
    )(flag, recv, own)


def _adamw_layers(mine, theirs, owners, w, m, v, name):
    L, r, C = w.shape
    rt = _pick(r, _rows_for(C), 16)

    def body(a0_ref, a1_ref, b0_ref, b1_ref, w_ref, m_ref, v_ref, g_ref, d_ref, mo_ref, vo_ref):
        layer, c = pl.program_id(0), lax.axis_index("c")
        g0 = jnp.where(c == owners[0], a0_ref[...], b0_ref[...])
        g1 = jnp.where(c == owners[1], a1_ref[...], b1_ref[...])
        g_ref[...] = jnp.where(layer == 0, g0, g1)
        _adamw_math(g_ref, w_ref, m_ref, v_ref, d_ref, mo_ref, vo_ref)

    flat = pl.BlockSpec((rt, C), lambda l, i: (i, 0))
    lay = pl.BlockSpec((None, rt, C), lambda l, i: (l, i, 0))
    return pl.pallas_call(
        body, name=name, grid=(L, r // rt), in_specs=[flat] * 4 + [lay] * 3, out_specs=[lay] * 4,
        out_shape=[jax.ShapeDtypeStruct((L, r, C), f32)] * 4, compiler_params=_cp("parallel", "parallel"),
    )(mine[0], mine[1], theirs[0], theirs[1], w, m, v)


def _adamw_math(g_ref, w_ref, m_ref, v_ref, d_ref, mo_ref, vo_ref):
    gg = g_ref[...]
    m_new = ADAM_B1 * m_ref[...] + (1.0 - ADAM_B1) * gg
    v_new = ADAM_B2 * v_ref[...] + (1.0 - ADAM_B2) * (gg * gg)
    m_hat = m_new / (1.0 - ADAM_B1 ** ADAM_STEP)
    v_hat = v_new / (1.0 - ADAM_B2 ** ADAM_STEP)
    d_ref[...] = -ADAM_LR * (m_hat / (jnp.sqrt(v_hat) + ADAM_EPS) + ADAM_WD * w_ref[...])
    mo_ref[...] = m_new
    vo_ref[...] = v_new


FLAT_TILE = 2048


def _add2(a, b, name):
    R = a.shape[0]
    rt = _pick(R, FLAT_TILE, SUBLANES)

    def body(a_ref, b_ref, o_ref):
        o_ref[...] = a_ref[...] + b_ref[...]

    row = pl.BlockSpec((rt, LANES), lambda i: (i, 0))
    return pl.pallas_call(
        body, name=name, grid=(R // rt,), in_specs=[row, row], out_specs=row,
        out_shape=jax.ShapeDtypeStruct((R, LANES), f32), compiler_params=_cp("parallel"),
    )(a, b)


def _adamw(g, w, m, v, name):
    R = g.shape[0]
    rt = _pick(R, FLAT_TILE, SUBLANES)

    def body(g_ref, w_ref, m_ref, v_ref, d_ref, mo_ref, vo_ref):
        gg = g_ref[...]
        m_new = ADAM_B1 * m_ref[...] + (1.0 - ADAM_B1) * gg
        v_new = ADAM_B2 * v_ref[...] + (1.0 - ADAM_B2) * (gg * gg)
        m_hat = m_new / (1.0 - ADAM_B1 ** ADAM_STEP)
        v_hat = v_new / (1.0 - ADAM_B2 ** ADAM_STEP)
        d_ref[...] = -ADAM_LR * (m_hat / (jnp.sqrt(v_hat) + ADAM_EPS) + ADAM_WD * w_ref[...])
        mo_ref[...] = m_new
        vo_ref[...] = v_new

    row = pl.BlockSpec((rt, LANES), lambda i: (i, 0))
    return pl.pallas_call(
        body, name=name, grid=(R // rt,), in_specs=[row] * 4, out_specs=[row] * 3,
        out_shape=[jax.ShapeDtypeStruct((R, LANES), f32)] * 3, compiler_params=_cp("parallel"),
    )(g, w, m, v)


def _pack(arrs, dtype):
    flat = jnp.concatenate([a.astype(dtype).reshape(-1) for a in arrs])
    per = FLAT_TILE * LANES
    flat = jnp.pad(flat, (0, (-flat.shape[0]) % per))
    return flat.reshape(-1, LANES)


def _unpack(buf, shapes):
    flat = buf.reshape(-1)
    out, off = [], 0
    for s in shapes:
        n = math.prod(s)
        out.append(flat[off:off + n].reshape(s))
        off += n
    return out


def _block_diag(w):
    n, a, b = w.shape
    eye = jnp.eye(n, dtype=w.dtype)
    return (w[:, :, None, :] * eye[:, None, :, None]).reshape(n * a, n * b)


def _diag_blocks(m, n):
    a, b = m.shape[0] // n, m.shape[1] // n
    idx = jnp.arange(n)
    return m.reshape(n, a, n, b)[idx, :, idx, :]


BIG = ("w_in", "w_out", "w_up", "w_down", "s5_w_glu")
BIG_COL_SHARDED = {"w_in": True, "w_out": False, "w_up": True, "w_down": False, "s5_w_glu": False}
CONV_SHARDED = ("lru_conv_w", "ffn_conv_w")
SMALL = ("lru_conv_b", "lru_wr", "lru_br", "lru_wi", "lru_bi", "lru_lambda", "s5_a_re", "s5_a_im", "s5_b_re",
         "s5_b_im", "s5_c_re", "s5_c_im", "s5_d", "s5_log_step", "s5_b_glu", "mix_norm_g", "ln1_g", "ln1_b",
         "ffn_conv_b", "ln2_g", "ln2_b")
WEIGHTS = ("w_in", "lru_conv_w", "lru_conv_b", "lru_wr", "lru_br", "lru_wi", "lru_bi", "lru_lambda", "s5_a_re",
           "s5_a_im", "s5_b_re", "s5_b_im", "s5_c_re", "s5_c_im", "s5_d", "s5_log_step", "s5_w_glu", "s5_b_glu",
           "mix_norm_g", "w_out", "ln1_g", "ln1_b", "w_up", "ffn_conv_w", "ffn_conv_b", "w_down", "ln2_g", "ln2_b")


def _assemble(slabs, col_sharded):
    _, L, r, c = slabs.shape
    if col_sharded:
        return slabs.transpose(1, 2, 0, 3).reshape(L, r, 4 * c)
    return slabs.transpose(1, 0, 2, 3).reshape(L, 4 * r, c)


def _s5_prepare(p):
    G = N_S5_GROUPS
    bt_re, bt_im = p["s5_b_re"].transpose(0, 2, 1), p["s5_b_im"].transpose(0, 2, 1)
    ls = p["s5_log_step"].reshape(G, 1)
    ab_re, ab_im, bb_re, bb_im = _s5_params(p["s5_a_re"], p["s5_a_im"], ls, bt_re, bt_im)
    ab = jnp.concatenate([ab_re.reshape(1, S5_LANES), ab_im.reshape(1, S5_LANES)], axis=1)
    bbcat = jnp.concatenate([_block_diag(bb_re), _block_diag(bb_im)], axis=1).astype(bf16)
    ccat = jnp.concatenate([_block_diag(p["s5_c_re"].transpose(0, 2, 1)),
                            -_block_diag(p["s5_c_im"].transpose(0, 2, 1))], axis=0).astype(bf16)
    bbcat_pad = jnp.concatenate([bbcat, jnp.zeros((LRU_WIDTH - S5_WIDTH, 2 * S5_LANES), bf16)], axis=0)
    return dict(bt_re=bt_re, bt_im=bt_im, ls=ls, ab=ab, bbcat=bbcat, bbcat_pad=bbcat_pad, ccat=ccat)


def _layer_fwd(h, p, cos, sin, pending, install):
    sv = {"h": h}
    proj = _mm(h, p["w_in"], "nn", "mm_proj", tn=768)
    sv.update(proj=proj)
    qr, kr = _rope_fwd(proj, cos, sin)
    attn, ltot, gathered = _attn_fwd2(qr, kr, proj, [s for _, _, s in pending], [True] * len(pending))
    install(pending, gathered)
    sv.update(qr=qr, kr=kr, attn=attn, ltot=ltot)
    wr, wi = _block_diag(p["lru_wr"]).astype(bf16), _block_diag(p["lru_wi"]).astype(bf16)
    xc, r, i, log_a, u = _lru_pre(proj, p["lru_conv_w"], p["lru_conv_b"], wr, p["lru_br"], wi, p["lru_bi"],
                                  p["lru_lambda"])
    hl, lru = _lru_scan(log_a, u, proj)
    sv.update(wr=wr, wi=wi, xc=xc, r=r, i=i, log_a=log_a, hl=hl, lru=lru)
    s5 = _s5_prepare(p)
    bu = _mm(proj, s5["bbcat_pad"], "nn", "mm_s5_bu", a_win=(PROJ_S5_U, LRU_WIDTH))
    xs = _s5_scan(bu, s5["ab"])
    yc = _mm(xs, s5["ccat"], "nn", "mm_s5_y")
    ssm, y = _s5_out(yc, proj, p["s5_d"].reshape(-1), p["s5_w_glu"], p["s5_b_glu"])
    sv.update(s5=s5, xs=xs, y=y, ssm=ssm)
    mixed = _mixnorm(attn, lru, ssm, p["mix_norm_g"])
    mix = _mm(mixed, p["w_out"], "nn", "mm_out")
    h1, z1 = _ln_fwd(h, mix, p["ln1_g"], p["ln1_b"], "ln_fwd")
    sv.update(mixed=mixed, z1=z1, h1=h1)
    up = _mm(h1, p["w_up"], "nn", "mm_up", tn=1536)
    act = _ffn_act(up, p["ffn_conv_w"], p["ffn_conv_b"])
    ffn = _mm(act, p["w_down"], "nn", "mm_down")
    h2, z2 = _ln_fwd(h1, ffn, p["ln2_g"], p["ln2_b"], "ln_fwd")
    sv.update(up=up, act=act, z2=z2)
    return h2, sv


def _layer_bwd(dy_a, dy_b, p, sv, cos, sin, relay=None):
    gr = {}
    dz2, acc = _ln_bwd(dy_a, dy_b, sv["z2"], p["ln2_g"], "ln_bwd_top" if dy_a is None else "ln_bwd")
    gr["ln2_g"], gr["ln2_b"] = acc[0], acc[1]
    dact = _mm(dz2, p["w_down"], "nt", "mm_dact")
    gr["w_down"] = _mm(sv["act"], dz2, "tn", "mm_dw_down")
    dupc = _ffn_act_bwd(dact, sv["up"], p["ffn_conv_w"], p["ffn_conv_b"])
    others, others_dst, to_slabs = relay if relay else ((), 0, None)
    send = list(others) + ([gr["w_down"]] if relay else [])
    send_dst = [others_dst] * len(others) + ([EARLY_OWNER] if relay else [])
    dup, acc, from_sibling = _conv_bwd(dupc, sv["up"], p["ffn_conv_w"], "ffn_conv_bwd", col_tile=FFN_COL_TILE,
                                       out_dtype=bf16, send=send, send_dst=send_dst)
    slabs, owners = to_slabs(from_sibling[:-1], gr["w_down"], from_sibling[-1]) if relay else ((), 0)
    gr["ffn_conv_w"], gr["ffn_conv_b"] = acc[0:3], acc[3]
    dh1 = _mm(dup, p["w_up"], "nt", "mm_dh1", tk=2048)
    gr["w_up"] = _mm(sv["h1"], dup, "tn", "mm_dw_up", tn=1536)
    dz1, acc = _ln_bwd(dz2, dh1, sv["z1"], p["ln1_g"], "ln_bwd")
    gr["ln1_g"], gr["ln1_b"] = acc[0], acc[1]
    dmixed = _mm(dz1, p["w_out"], "nt", "mm_dmixed")
    gr["w_out"] = _mm(sv["mixed"], dz1, "tn", "mm_dw_out")
    dattn, dlru, dssm, delta, acc = _mixnorm_bwd(dmixed, sv["attn"], sv["lru"], sv["ssm"], p["mix_norm_g"])
    gr["mix_norm_g"] = acc[0]
    proj = sv["proj"]
    late = [gr["w_up"]] if relay else []
    dqr, dkr, dv, received, late_sibling = _attn_bwd2(sv["qr"], sv["kr"], proj, dattn, sv["ltot"], delta, slabs,
                                                      owners, late, EARLY_OWNER)
    dq, dk = _rope_bwd(dqr, dkr, cos, sin)
    g, dgate = _lru_scan_bwd(dlru, proj, sv["hl"], sv["log_a"])
    dxc, dwr, dwi, acc = _lru_gate_bwd(g, sv["hl"], sv["xc"], sv["r"], sv["i"], sv["log_a"], sv["wr"], sv["wi"],
                                       p["lru_lambda"])
    gr["lru_wr"], gr["lru_wi"] = _diag_blocks(dwr, N_LRU_HEADS), _diag_blocks(dwi, N_LRU_HEADS)
    gr["lru_br"], gr["lru_bi"], gr["lru_lambda"] = acc[0], acc[1], acc[2]
    dxr, acc, _ = _conv_bwd(dxc, proj, p["lru_conv_w"], "lru_conv_bwd", x_col_block=PROJ_LRU_X)
    gr["lru_conv_w"], gr["lru_conv_b"] = acc[0:4], acc[4]
    s5 = sv["s5"]
    G = N_S5_GROUPS
    dy, du_direct, dwglu, acc = _s5_out_bwd(dssm, sv["y"], proj, p["s5_d"].reshape(-1), p["s5_w_glu"],
                                            p["s5_b_glu"])
    gr["s5_w_glu"], gr["s5_b_glu"], gr["s5_d"] = dwglu, acc[0], acc[1].reshape(G, S5_GROUP)
    dxs = _mm(dy, s5["ccat"], "nt", "mm_s5_dx")
    dccat = _mm(sv["xs"], dy, "tn", "mm_s5_dc")
    gr["s5_c_re"] = _diag_blocks(dccat[:S5_LANES], G).transpose(0, 2, 1)
    gr["s5_c_im"] = -_diag_blocks(dccat[S5_LANES:], G).transpose(0, 2, 1)
    gs, dab = _s5_scan_bwd(dxs, sv["xs"], s5["ab"])
    du = _mm(gs, s5["bbcat"], "nt", "mm_s5_du", add=du_direct)
    dbbcat = _mm(proj, gs, "tn", "mm_s5_dbb", a_win=(PROJ_S5_U, LRU_WIDTH))[:S5_WIDTH]
    d_ar, d_ai, d_ls, d_btr, d_bti = _s5_params_bwd(
        p["s5_a_re"], p["s5_a_im"], s5["ls"], s5["bt_re"], s5["bt_im"],
        dab[:, :S5_LANES].reshape(G, S5_STATE), dab[:, S5_LANES:].reshape(G, S5_STATE),
        _diag_blocks(dbbcat[:, :S5_LANES], G), _diag_blocks(dbbcat[:, S5_LANES:], G))
    gr["s5_a_re"], gr["s5_a_im"], gr["s5_log_step"] = d_ar, d_ai, d_ls.reshape(G)
    gr["s5_b_re"], gr["s5_b_im"] = d_btr.transpose(0, 2, 1), d_bti.transpose(0, 2, 1)
    pad = jnp.zeros((du.shape[0], D_IN_PAD - D_IN), f32)
    dproj = jnp.concatenate([dq, dk, dv, dxr, dgate, du, pad], axis=1).astype(bf16)
    gr["w_in"] = _mm(sv["h"], dproj, "tn", "mm_dw_in", tn=768)[:, :D_IN]
    dh = _mm(dproj, p["w_in"], "nt", "mm_dh")
    return (dz1, dh, gr, slabs, received, late_sibling) if relay else (dz1, dh, gr)


def _train_step(d):
    x, target = d["x"][0], d["loss_target"][0]
    S = x.shape[0]
    me = 2 * lax.axis_index("x") + lax.axis_index("y")

    def rows2d(a):
        return a.reshape(a.shape[0] * a.shape[1], a.shape[2])

    params = [{n: d[n][l] for n in SMALL} for l in range(DEPTH)]

    def install(items, gathered):
        for (n, l, mine), g in zip(items, gathered):
            g = lax.dynamic_update_slice_in_dim(g, mine[None], me, axis=0)
            if n in CONV_SHARDED:
                full = _assemble(g.reshape((4,) + d[n].shape), True)
                for k in range(DEPTH):
                    params[k][n] = full[k]
                continue
            full = _assemble(g[:, None], BIG_COL_SHARDED[n])[0]
            if n == "w_in":
                full = jnp.pad(full, ((0, 0), (0, D_IN_PAD - D_IN)))
            params[l][n] = full

    def shard(n, l):
        return (n, l, d[n][l].astype(bf16))

    first = [shard("w_in", 0)] + [(n, None, rows2d(d[n])) for n in CONV_SHARDED]
    install(first, _gather_weights([s for _, _, s in first], [True] + [False] * len(CONV_SHARDED)))
    later = [[shard(n, 0) for n in BIG[1:]] + [shard("w_in", 1)], [shard(n, 1) for n in BIG[1:]]]

    cos, sin = _rope_tables(S)
    h, saved = x, []
    for l in range(DEPTH):
        h, sv = _layer_fwd(h, params[l], cos, sin, later[l], install)
        saved.append(sv)
    dy, loss_acc = _loss_head(h, target)
    def slab(n, g, other, owner):
        aligned = BIG_COL_SHARDED[n] and (g.shape[1] // 4) % LANES == 0
        p = _pair_sum(g, other, "pair_sum_" + n, owner, col_slabs=aligned)
        if BIG_COL_SHARDED[n] and not aligned:
            return p.reshape(p.shape[0], 4, p.shape[1] // 4).transpose(1, 0, 2)
        return p if aligned else p.reshape(4, p.shape[0] // 4, p.shape[1])

    own1 = {n: 1 - EARLY_OWNER for n in BIG}
    own0 = {n: (EARLY_OWNER if n in ("w_down", "w_up") else 1 - EARLY_OWNER) for n in BIG}

    def hidden_slabs(others1, w_down0, w_down0_sibling):
        slabs = [slab(n, grads[1][n], o, own1[n]) for n, o in zip(BIG, others1)]
        slabs.append(slab("w_down", w_down0, w_down0_sibling, own0["w_down"]))
        return slabs, [own1[n] for n in BIG] + [own0["w_down"]]

    da, db, grads = None, dy, [None] * DEPTH
    da, db, grads[1] = _layer_bwd(da, db, params[1], saved[1], cos, sin)
    relay = ([grads[1][n] for n in BIG], 1 - EARLY_OWNER, hidden_slabs)
    da, db, grads[0], hslabs, hrecv, (w_up0_sibling,) = _layer_bwd(da, db, params[0], saved[0], cos, sin, relay)
    out = {"grad_x": _axpy(da, db, "grad_x")[None]}
    G = {n: jnp.stack([grads[l][n] for l in range(DEPTH)]) for n in SMALL + CONV_SHARDED}

    small = SMALL + CONV_SHARDED
    sp = _pack([G[n] for n in small], f32)
    tail = [n for n in BIG if n != "w_down"]
    rest = [n for n in tail if n != "w_up"]
    others_rest, (sp_sibling,) = _sibling_send([grads[0][n] for n in rest], [own0[n] for n in rest], [sp])
    sibling0 = dict(zip(rest, others_rest), w_up=w_up0_sibling)
    tslabs = [slab(n, grads[0][n], sibling0[n], own0[n]) for n in tail]
    chip_small = _add2(sp, sp_sibling, "pair_sum_small")
    trecv, (recv_small,) = _chip_exchange(tslabs, [chip_small], [own0[n] for n in tail])
    mine0 = {n: _sum_chips(r, s, "sum_chips_" + n, own0[n]) for n, r, s in zip(tail, trecv, tslabs)}
    mine0["w_down"] = _sum_chips(hrecv[-1], hslabs[-1], "sum_chips_w_down", own0["w_down"])
    mine1 = {n: _sum_chips(r, s, "sum_chips_" + n, own1[n]) for n, r, s in zip(BIG, hrecv, hslabs)}
    sent, _ = _sibling_send([mine0[n] for n in BIG] + [mine1[n] for n in BIG],
                            [1 - own0[n] for n in BIG] + [1 - own1[n] for n in BIG])
    theirs0, theirs1 = dict(zip(BIG, sent[:len(BIG)])), dict(zip(BIG, sent[len(BIG):]))
    for n in BIG:
        upd = _adamw_layers((mine0[n], mine1[n]), (theirs0[n], theirs1[n]), (own0[n], own1[n]),
                            d[n], d["m_" + n], d["v_" + n], "adamw_" + n)
        for pre, u in zip(("grad_", "delta_", "new_m_", "new_v_"), upd):
            out[pre + n] = u

    total = _sum_chips(recv_small, chip_small, "sum_chips_small")
    gs = dict(zip(small, _unpack(total, [G[n].shape for n in small])))
    for n in CONV_SHARDED:
        L, K, C = gs[n].shape
        gs[n] = lax.dynamic_index_in_dim(gs[n].reshape(L, K, 4, C // 4), me, axis=2, keepdims=False)
    small_shapes = [d[n].shape for n in small]
    gsmall = _pack([gs[n] for n in small], f32)
    upd = _adamw(gsmall, _pack([d[n] for n in small], f32), _pack([d["m_" + n] for n in small], f32),
                 _pack([d["v_" + n] for n in small], f32), "adamw_small")
    for pre, buf in zip(("grad_", "delta_", "new_m_", "new_v_"), (gsmall,) + tuple(upd)):
        for n, a in zip(small, _unpack(buf, small_shapes)):
            out[pre + n] = a

    loss_local, _ = lax.optimization_barrier((loss_acc[0, 0], upd[0]))
    out["loss"] = lax.psum(loss_local, ("x", "y", "c"))
    return (out["loss"], out["grad_x"]) + tuple(out[pre + n] for pre in ("grad_", "delta_", "new_m_", "new_v_")
                                                for n in WEIGHTS)


def kernel(
        x, w_in, lru_conv_w, lru_conv_b, lru_wr, lru_br, lru_wi, lru_bi, lru_lambda, s5_a_re, s5_a_im, s5_b_re,
        s5_b_im, s5_c_re, s5_c_im, s5_d, s5_log_step, s5_w_glu, s5_b_glu, mix_norm_g, w_out, ln1_g, ln1_b, w_up,
        ffn_conv_w, ffn_conv_b, w_down, ln2_g, ln2_b, loss_target, m_w_in, m_lru_conv_w, m_lru_conv_b, m_lru_wr,
        m_lru_br, m_lru_wi, m_lru_bi, m_lru_lambda, m_s5_a_re, m_s5_a_im, m_s5_b_re, m_s5_b_im, m_s5_c_re,
        m_s5_c_im, m_s5_d, m_s5_log_step, m_s5_w_glu, m_s5_b_glu, m_mix_norm_g, m_w_out, m_ln1_g, m_ln1_b,
        m_w_up, m_ffn_conv_w, m_ffn_conv_b, m_w_down, m_ln2_g, m_ln2_b, v_w_in, v_lru_conv_w, v_lru_conv_b,
        v_lru_wr, v_lru_br, v_lru_wi, v_lru_bi, v_lru_lambda, v_s5_a_re, v_s5_a_im, v_s5_b_re, v_s5_b_im,
        v_s5_c_re, v_s5_c_im, v_s5_d, v_s5_log_step, v_s5_w_glu, v_s5_b_glu, v_mix_norm_g, v_w_out, v_ln1_g,
        v_ln1_b, v_w_up, v_ffn_conv_w, v_ffn_conv_b, v_w_down, v_ln2_g, v_ln2_b
):
    return _train_step(dict(locals()))
```

```python
import functools
import math

import jax
import jax.numpy as jnp
from jax import lax
from jax.experimental import pallas as pl
from jax.experimental.pallas import tpu as pltpu

f32 = jnp.float32
bf16 = jnp.bfloat16
MESH = pl.DeviceIdType.MESH

D_MODEL = 1024
ATTN_WIDTH = 384
LRU_WIDTH = 384
S5_WIDTH = 256
HEAD_DIM = 64
N_LRU_HEADS = 6
N_S5_GROUPS = 16
S5_GROUP = 16
S5_STATE = 64
S5_LANES = N_S5_GROUPS * S5_STATE
D_FF = 3072
D_IN = 2176
LRU_C = 8.0
ROPE_THETA = 10000.0
DILATIONS = (1, 4, 16)
ATTN_BLOCK = 128
DEPTH = 2
ALPHA = (2 * DEPTH) ** 0.25
LN_EPS = 1e-5
RMS_EPS = 1e-6
ADAM_LR, ADAM_B1, ADAM_B2, ADAM_EPS, ADAM_WD, ADAM_STEP = 0.001, 0.9, 0.999, 1e-08, 0.01, 10

SUBLANES = 8
LANES = 128
VMEM_LIMIT = 56 * 1024 * 1024
ROW_TILE = 512
MM_SINGLE_K = 3072
D_IN_PAD = 2304
NEG = -1e30


def _cp(*sem):
    return pltpu.CompilerParams(dimension_semantics=sem if sem else None, vmem_limit_bytes=VMEM_LIMIT)


def _pick(dim, pref, align=LANES):
    if dim <= pref:
        return dim
    t = (pref // align) * align
    while t >= align:
        if dim % t == 0:
            return t
        t -= align
    return dim


def _gelu(x):
    return jax.nn.gelu(x)


def _gelu_grad(x):
    c = math.sqrt(2.0 / math.pi)
    t = jnp.tanh(c * (x + 0.044715 * x * x * x))
    return 0.5 * (1.0 + t) + 0.5 * x * (1.0 - t * t) * c * (1.0 + 3 * 0.044715 * x * x)


def _gelu_pair(x):
    c = math.sqrt(2.0 / math.pi)
    x2 = x * x
    t = jnp.tanh(c * x * (1.0 + 0.044715 * x2))
    return 0.5 * x * (1.0 + t), 0.5 * (1.0 + t) + 0.5 * x * (1.0 - t * t) * c * (1.0 + 3 * 0.044715 * x2)


def _sigmoid(x):
    return jax.nn.sigmoid(x)


def _expm1(x):
    p = 1.0 + x / 9.0
    for n in (8.0, 7.0, 6.0, 5.0, 4.0, 3.0, 2.0):
        p = 1.0 + (x / n) * p
    return jnp.where(jnp.abs(x) < 0.3, x * p, jnp.exp(x) - 1.0)


def _dot(a, b, dims):
    return lax.dot_general(a, b, (dims, ((), ())), preferred_element_type=f32)


NN = ((1,), (0,))
NT = ((1,), (1,))
TN = ((0,), (0,))


def _mm(a, b, mode, name, out_dtype=f32, tm=1024, tn=1024, tk=1024, add=None, a_win=None):
    if mode == "nn":
        (M, K), N = a.shape, b.shape[1]
    elif mode == "nt":
        (M, K), N = a.shape, b.shape[0]
    else:
        (K, M), N = a.shape, b.shape[1]
    win = 0
    if a_win is not None:
        win, w = a_win
        if mode == "tn":
            M, tm = w, w
        else:
            K = w
    single = mode != "tn" and K <= MM_SINGLE_K
    tm, tn = _pick(M, tm), _pick(N, tn)
    tk = K if single else _pick(K, tk)
    nk = K // tk
    dims = {"nn": NN, "nt": NT, "tn": TN}[mode]

    def body(a_ref, b_ref, *rest):
        prod = _dot(a_ref[...].astype(bf16), b_ref[...].astype(bf16), dims)
        if single:
            o_ref = rest[-1]
            o_ref[...] = (prod if add is None else prod + rest[0][...]).astype(o_ref.dtype)
            return
        o_ref, acc = rest[-2:]
        k = pl.program_id(2)

        @pl.when(k == 0)
        def _():
            acc[...] = prod if add is None else prod + rest[0][...]

        @pl.when(k > 0)
        def _():
            acc[...] += prod

        @pl.when(k == nk - 1)
        def _():
            o_ref[...] = acc[...].astype(o_ref.dtype)

    if mode == "tn":
        a_spec = pl.BlockSpec((tk, tm), lambda i, j, k: (k, i + win))
    else:
        a_spec = pl.BlockSpec((tm, tk), lambda i, j, k: (i, k + win))
    if mode == "nt":
        b_spec = pl.BlockSpec((tn, tk), lambda i, j, k: (j, k))
    else:
        b_spec = pl.BlockSpec((tk, tn), lambda i, j, k: (k, j))
    o_spec = pl.BlockSpec((tm, tn), lambda i, j, k: (i, j))
    return pl.pallas_call(
        body, name=name, grid=(M // tm, N // tn, nk),
        in_specs=[a_spec, b_spec] + ([] if add is None else [o_spec]), out_specs=o_spec,
        out_shape=jax.ShapeDtypeStruct((M, N), out_dtype),
        scratch_shapes=[] if single else [pltpu.VMEM((tm, tn), f32)],
        compiler_params=_cp("parallel", "parallel", "arbitrary"),
    )(*((a, b) if add is None else (a, b, add)))


def _shift_down(cur, prev8, k):
    if k == 0:
        return cur
    T, (R, C) = SUBLANES, cur.shape
    rot = pltpu.roll(cur.reshape(R // T, T, C), k, 1)
    before = jnp.concatenate([pltpu.roll(prev8, k, 0)[None], rot[:-1]], axis=0)
    row = lax.broadcasted_iota(jnp.int32, (R // T, T, C), 1)
    return jnp.where(row < k, before, rot).reshape(R, C)


def _shift_up(cur, next8, k):
    if k == 0:
        return cur
    T, (R, C) = SUBLANES, cur.shape
    rot = pltpu.roll(cur.reshape(R // T, T, C), T - k, 1)
    after = jnp.concatenate([rot[1:], pltpu.roll(next8, T - k, 0)[None]], axis=0)
    row = lax.broadcasted_iota(jnp.int32, (R // T, T, C), 1)
    return jnp.where(row < T - k, rot, after).reshape(R, C)


def _prev_halo_spec(rt, cols, ncolblk_fn):
    per = rt // SUBLANES
    return pl.BlockSpec((SUBLANES, cols), lambda *g: (jnp.maximum(g[-1] * per - 1, 0), ncolblk_fn(*g)))


def _ln_fwd(h, branch, g, b, name):
    S, D = h.shape
    rt = _pick(S, ROW_TILE, SUBLANES)

    def body(h_ref, m_ref, g_ref, b_ref, o_ref, z_ref):
        z = ALPHA * h_ref[...] + m_ref[...]
        mu = jnp.mean(z, axis=-1, keepdims=True)
        zc = z - mu
        var = jnp.mean(zc * zc, axis=-1, keepdims=True)
        o_ref[...] = zc * lax.rsqrt(var + LN_EPS) * g_ref[...] + b_ref[...]
        z_ref[...] = z

    row = pl.BlockSpec((rt, D), lambda i: (i, 0))
    vec = pl.BlockSpec((1, D), lambda i: (0, 0))
    return pl.pallas_call(
        body, name=name, grid=(S // rt,), in_specs=[row, row, vec, vec], out_specs=[row, row],
        out_shape=[jax.ShapeDtypeStruct((S, D), f32)] * 2, compiler_params=_cp("parallel"),
    )(h, branch, g.reshape(1, D), b.reshape(1, D))


def _ln_bwd(dy_a, dy_b, z, g, name):
    S, D = z.shape
    rt = _pick(S, ROW_TILE, SUBLANES)
    two = dy_a is not None

    def body(*refs):
        if two:
            a_ref, b_ref, z_ref, g_ref, dz_ref, acc_ref = refs
            dy = ALPHA * a_ref[...] + b_ref[...]
        else:
            b_ref, z_ref, g_ref, dz_ref, acc_ref = refs
            dy = b_ref[...]
        z = z_ref[...]
        mu = jnp.mean(z, axis=-1, keepdims=True)
        zc = z - mu
        var = jnp.mean(zc * zc, axis=-1, keepdims=True)
        rstd = lax.rsqrt(var + LN_EPS)
        xhat = zc * rstd
        dxh = dy * g_ref[...]
        m1 = jnp.mean(dxh, axis=-1, keepdims=True)
        m2 = jnp.mean(dxh * xhat, axis=-1, keepdims=True)
        dz_ref[...] = rstd * (dxh - m1 - xhat * m2)

        @pl.when(pl.program_id(0) == 0)
        def _():
            acc_ref[...] = jnp.zeros_like(acc_ref)

        acc_ref[0:1, :] += jnp.sum(dy * xhat, axis=0, keepdims=True)
        acc_ref[1:2, :] += jnp.sum(dy, axis=0, keepdims=True)

    row = pl.BlockSpec((rt, D), lambda i: (i, 0))
    vec = pl.BlockSpec((1, D), lambda i: (0, 0))
    acc = pl.BlockSpec((SUBLANES, D), lambda i: (0, 0))
    ins = ([dy_a] if two else []) + [dy_b, z, g.reshape(1, D)]
    return pl.pallas_call(
        body, name=name, grid=(S // rt,), in_specs=[row] * (len(ins) - 1) + [vec], out_specs=[row, acc],
        out_shape=[jax.ShapeDtypeStruct((S, D), f32), jax.ShapeDtypeStruct((SUBLANES, D), f32)],
        compiler_params=_cp("arbitrary"),
    )(*ins)


def _loss_head(y, target):
    S, D = y.shape
    rt = _pick(S, ROW_TILE, SUBLANES)

    def body(y_ref, t_ref, dy_ref, acc_ref):
        e = y_ref[...] - t_ref[...]
        dy_ref[...] = e * (1.0 / D)

        @pl.when(pl.program_id(0) == 0)
        def _():
            acc_ref[...] = jnp.zeros_like(acc_ref)

        part = jnp.sum(jnp.mean(e * e, axis=-1, keepdims=True), axis=0, keepdims=True)
        acc_ref[...] += 0.5 * part

    row = pl.BlockSpec((rt, D), lambda i: (i, 0))
    return pl.pallas_call(
        body, name="loss_head", grid=(S // rt,), in_specs=[row, row],
        out_specs=[row, pl.BlockSpec((1, 1), lambda i: (0, 0))],
        out_shape=[jax.ShapeDtypeStruct((S, D), f32), jax.ShapeDtypeStruct((1, 1), f32)],
        compiler_params=_cp("arbitrary"),
    )(y, target)


def _axpy(a, b, name):
    S, D = a.shape
    rt = _pick(S, ROW_TILE, SUBLANES)

    def body(a_ref, b_ref, o_ref):
        o_ref[...] = ALPHA * a_ref[...] + b_ref[...]

    row = pl.BlockSpec((rt, D), lambda i: (i, 0))
    return pl.pallas_call(
        body, name=name, grid=(S // rt,), in_specs=[row, row], out_specs=row,
        out_shape=jax.ShapeDtypeStruct((S, D), f32), compiler_params=_cp("parallel"),
    )(a, b)


def _rope_tables(S):
    rt = _pick(S, ROW_TILE, SUBLANES)

    def body(c_ref, s_ref):
        pos = (pl.program_id(0) * rt + lax.broadcasted_iota(jnp.int32, (rt, LANES), 0)).astype(f32)
        lane = lax.broadcasted_iota(jnp.int32, (rt, LANES), 1)
        j = (lane % (HEAD_DIM // 2)).astype(f32)
        inv = jnp.exp((-j * 2.0 / HEAD_DIM) * math.log(ROPE_THETA))
        ang = pos * inv
        c = jnp.cos(ang)
        s = jnp.where(lane % HEAD_DIM < HEAD_DIM // 2, -jnp.sin(ang), jnp.sin(ang))
        c_ref[...] = jnp.concatenate([c, c, c], axis=1)
        s_ref[...] = jnp.concatenate([s, s, s], axis=1)

    row = pl.BlockSpec((rt, ATTN_WIDTH), lambda i: (i, 0))
    return pl.pallas_call(
        body, name="rope_tables", grid=(S // rt,), in_specs=[], out_specs=[row, row],
        out_shape=[jax.ShapeDtypeStruct((S, ATTN_WIDTH), f32)] * 2, compiler_params=_cp("parallel"),
    )()


def _swap_halves(x):
    lane = lax.broadcasted_iota(jnp.int32, x.shape, 1)
    half = HEAD_DIM // 2
    return jnp.where(lane % HEAD_DIM < half, pltpu.roll(x, x.shape[1] - half, 1), pltpu.roll(x, half, 1))


def _rope_fwd(proj, cos, sin):
    S, W = proj.shape[0], ATTN_WIDTH
    rt = _pick(S, ROW_TILE, SUBLANES)

    def body(q_ref, k_ref, c_ref, s_ref, qo_ref, ko_ref):
        c, s = c_ref[...], s_ref[...]
        qo_ref[...] = q_ref[...] * c + _swap_halves(q_ref[...]) * s
        ko_ref[...] = k_ref[...] * c + _swap_halves(k_ref[...]) * s

    row = pl.BlockSpec((rt, W), lambda i: (i, 0))
    return pl.pallas_call(
        body, name="rope_fwd", grid=(S // rt,), in_specs=[row, pl.BlockSpec((rt, W), lambda i: (i, 1)), row, row],
        out_specs=[row, row], out_shape=[jax.ShapeDtypeStruct((S, W), f32)] * 2, compiler_params=_cp("parallel"),
    )(proj, proj, cos, sin)


def _rope_bwd(dq, dk, cos, sin):
    S, W = dq.shape
    rt = _pick(S, ROW_TILE, SUBLANES)

    def body(q_ref, k_ref, c_ref, s_ref, qo_ref, ko_ref):
        c, s = c_ref[...], s_ref[...]
        qo_ref[...] = q_ref[...] * c + _swap_halves(q_ref[...] * s)
        ko_ref[...] = k_ref[...] * c + _swap_halves(k_ref[...] * s)

    row = pl.BlockSpec((rt, W), lambda i: (i, 0))
    return pl.pallas_call(
        body, name="rope_bwd", grid=(S // rt,), in_specs=[row] * 4, out_specs=[row] * 2,
        out_shape=[jax.ShapeDtypeStruct((S, W), f32)] * 2, compiler_params=_cp("parallel"),
    )(dq, dk, cos, sin)


def _rows(ref, start, d):
    if d == 1:
        return ref[pl.ds(pl.multiple_of(start, ATTN_BLOCK), ATTN_BLOCK), :]
    return ref[pl.ds(start, ATTN_BLOCK, stride=d), :]


def _set_rows(ref, start, d, val):
    if d == 1:
        ref[pl.ds(pl.multiple_of(start, ATTN_BLOCK), ATTN_BLOCK), :] = val
    else:
        ref[pl.ds(start, ATTN_BLOCK, stride=d), :] = val


def _pair_spec(S, first_block):
    return pl.BlockSpec((S, LANES), lambda p: (0, p + first_block))


def _attn_fwd2(qr, kr, proj, shards=(), split=()):
    S = qr.shape[0]
    B = ATTN_BLOCK
    nb = S // B
    scale = HEAD_DIM ** -0.5

    gather = _Gather(shards, split)
    nt = gather.nt

    def body(*refs):
        q_ref, k_ref, v_ref = refs[:3]
        g_ins = refs[3:3 + nt]
        o_ref, l_ref = refs[3 + nt:5 + nt]
        g_outs = refs[5 + nt:5 + 2 * nt]
        m_s, l_s = refs[5 + 2 * nt:7 + 2 * nt]
        g_sems = refs[7 + 2 * nt:]
        if nt:
            @pl.when(pl.program_id(0) == 0)
            def _():
                gather.start(g_ins, g_outs, g_sems)

        qi = lax.broadcasted_iota(jnp.int32, (B, 2 * B), 0)
        ki = lax.broadcasted_iota(jnp.int32, (B, 2 * B), 1)
        dist = qi + B - ki
        band = (dist >= 0) & (dist <= B)
        for bi, d in enumerate(DILATIONS):
            bpc = nb // d

            def blk(b, carry, bi=bi, d=d, bpc=bpc):
                c, n = b // bpc, b % bpc
                start = c + d * B * n
                pstart = c + d * B * jnp.maximum(n - 1, 0)
                valid = band & ((ki >= B) | (n > 0))
                q = _rows(q_ref, start, d).astype(bf16)
                kcat = jnp.concatenate([_rows(k_ref, pstart, d), _rows(k_ref, start, d)], axis=0).astype(bf16)
                vcat = jnp.concatenate([_rows(v_ref, pstart, d), _rows(v_ref, start, d)], axis=0).astype(bf16)
                if bi > 0:
                    m_old, l_old, a_old = _rows(m_s, start, d), _rows(l_s, start, d), _rows(o_ref, start, d)
                ms, ls, accs = [], [], []
                for h in range(2):
                    sl = slice(h * HEAD_DIM, (h + 1) * HEAD_DIM)
                    c0 = h * HEAD_DIM
                    s = jnp.where(valid, _dot(q[:, sl], kcat[:, sl], NT) * scale, NEG)
                    m = jnp.max(s, axis=1, keepdims=True)
                    if bi > 0:
                        mo = m_old[:, c0:c0 + 1]
                        m = jnp.maximum(m, mo)
                        alpha = jnp.exp(mo - m)
                    p = jnp.exp(s - m)
                    l = jnp.sum(p, axis=1, keepdims=True)
                    acc = _dot(p.astype(bf16), vcat[:, sl], NN)
                    if bi > 0:
                        l = l + alpha * l_old[:, c0:c0 + 1]
                        acc = acc + alpha * a_old[:, sl]
                    ms.append(jnp.broadcast_to(m, (B, HEAD_DIM)))
                    ls.append(jnp.broadcast_to(l, (B, HEAD_DIM)))
                    accs.append(acc)
                _set_rows(m_s, start, d, jnp.concatenate(ms, axis=1))
                _set_rows(l_s, start, d, jnp.concatenate(ls, axis=1))
                _set_rows(o_ref, start, d, jnp.concatenate(accs, axis=1))
                return carry

            lax.fori_loop(0, nb, blk, 0, unroll=4)

        def fin(t, carry):
            rows = pl.ds(pl.multiple_of(t * B, B), B)
            l = l_s[rows, :]
            o_ref[rows, :] = o_ref[rows, :] / l
            l_ref[rows, :] = m_s[rows, :] + jnp.log(l)
            return carry

        lax.fori_loop(0, nb, fin, 0)
        if nt:
            @pl.when(pl.program_id(0) == pl.num_programs(0) - 1)
            def _():
                gather.finish(g_ins, g_outs, g_sems)

    pair = _pair_spec(S, 0)
    res = pl.pallas_call(
        body, name="attn_fwd_gather" if nt else "attn_fwd", grid=(3,),
        in_specs=[pair, pair, _pair_spec(S, 2 * ATTN_WIDTH // LANES)] + gather.in_specs,
        out_specs=[pair, pair] + gather.out_specs,
        out_shape=[jax.ShapeDtypeStruct((S, ATTN_WIDTH), f32)] * 2 + gather.out_shape,
        scratch_shapes=[pltpu.VMEM((S, LANES), f32)] * 2 + gather.scratch,
        compiler_params=_cp("arbitrary"),
    )(qr, kr, proj, *shards)
    return res[0], res[1], list(res[2:])


def _attn_bwd2(qr, kr, proj, dattn, ltot, delta, slabs=(), only_c=0, send=(), send_dst=0):
    S = qr.shape[0]
    B = ATTN_BLOCK
    nb = S // B
    scale = HEAD_DIM ** -0.5
    ex = _ChipExchange(slabs, (), only_c)
    snd = _SiblingSend(send, send_dst)
    n, ns = ex.n, snd.n
    hosted = n + ns

    def body(*refs):
        q_ref, k_ref, v_ref, do_ref, l_ref, d_ref = refs[:6]
        x_ins, s_ins = refs[6:6 + n], refs[6 + n:6 + hosted]
        dq_ref, dk_ref, dv_ref = refs[6 + hosted:9 + hosted]
        x_outs, s_outs = refs[9 + hosted:9 + hosted + n], refs[9 + hosted + n:9 + 2 * hosted]
        sems = refs[9 + 2 * hosted:]
        x_sems, s_sems = sems[:len(ex.scratch)], sems[len(ex.scratch):]
        if hosted:
            @pl.when(pl.program_id(0) == 0)
            def _():
                if n:
                    ex.start(x_ins, x_outs, x_sems)
                if ns:
                    snd.start(s_ins, s_outs, s_sems)

        qi = lax.broadcasted_iota(jnp.int32, (B, 2 * B), 0)
        ki = lax.broadcasted_iota(jnp.int32, (B, 2 * B), 1)
        dist1 = qi + B - ki
        band1 = (dist1 >= 0) & (dist1 <= B)
        ri = lax.broadcasted_iota(jnp.int32, (2 * B, B), 0)
        ci = lax.broadcasted_iota(jnp.int32, (2 * B, B), 1)
        dist2 = ri - ci
        band2 = (dist2 >= 0) & (dist2 <= B)
        for bi, d in enumerate(DILATIONS):
            bpc = nb // d

            def blk(b, carry, bi=bi, d=d, bpc=bpc):
                c, n = b // bpc, b % bpc
                start = c + d * B * n
                pstart = c + d * B * jnp.maximum(n - 1, 0)
                nstart = c + d * B * jnp.minimum(n + 1, bpc - 1)
                valid1 = band1 & ((ki >= B) | (n > 0))
                valid2 = band2 & ((ri < B) | (n + 1 < bpc))
                q_c, q_n = _rows(q_ref, start, d), _rows(q_ref, nstart, d)
                k_p, k_c = _rows(k_ref, pstart, d), _rows(k_ref, start, d)
                v_p, v_c = _rows(v_ref, pstart, d), _rows(v_ref, start, d)
                do_c, do_n = _rows(do_ref, start, d), _rows(do_ref, nstart, d)
                l_c, l_n = _rows(l_ref, start, d), _rows(l_ref, nstart, d)
                d_c, d_n = _rows(d_ref, start, d), _rows(d_ref, nstart, d)
                qc = q_c.astype(bf16)
                qcat = jnp.concatenate([q_c, q_n], axis=0).astype(bf16)
                kc = k_c.astype(bf16)
                kcat = jnp.concatenate([k_p, k_c], axis=0).astype(bf16)
                vc = v_c.astype(bf16)
                vcat = jnp.concatenate([v_p, v_c], axis=0).astype(bf16)
                doc = do_c.astype(bf16)
                docat = jnp.concatenate([do_c, do_n], axis=0).astype(bf16)
                lcat = jnp.concatenate([l_c, l_n], axis=0)
                dcat = jnp.concatenate([d_c, d_n], axis=0)
                dqs, dks, dvs = [], [], []
                for h in range(2):
                    sl = slice(h * HEAD_DIM, (h + 1) * HEAD_DIM)
                    c0 = h * HEAD_DIM
                    s1 = _dot(qc[:, sl], kcat[:, sl], NT) * scale
                    p1 = jnp.where(valid1, jnp.exp(s1 - l_c[:, c0:c0 + 1]), 0.0)
                    dp1 = _dot(doc[:, sl], vcat[:, sl], NT)
                    ds1 = p1 * (dp1 - d_c[:, c0:c0 + 1]) * scale
                    dqs.append(_dot(ds1.astype(bf16), kcat[:, sl], NN))
                    s2 = _dot(qcat[:, sl], kc[:, sl], NT) * scale
                    p2 = jnp.where(valid2, jnp.exp(s2 - lcat[:, c0:c0 + 1]), 0.0)
                    dvs.append(_dot(p2.astype(bf16), docat[:, sl], TN))
                    dp2 = _dot(docat[:, sl], vc[:, sl], NT)
                    ds2 = p2 * (dp2 - dcat[:, c0:c0 + 1]) * scale
                    dks.append(_dot(ds2.astype(bf16), qcat[:, sl], TN))
                for ref, parts in ((dq_ref, dqs), (dk_ref, dks), (dv_ref, dvs)):
                    new = jnp.concatenate(parts, axis=1)
                    if bi > 0:
                        new = new + _rows(ref, start, d)
                    _set_rows(ref, start, d, new)
                return carry

            lax.fori_loop(0, nb, blk, 0, unroll=4)

        if hosted:
            @pl.when(pl.program_id(0) == pl.num_programs(0) - 1)
            def _():
                if ns:
                    snd.finish(s_ins, s_outs, s_sems)
                if n:
                    ex.finish(x_ins, x_outs, x_sems)

    pair = _pair_spec(S, 0)
    res = pl.pallas_call(
        body, name="attn_bwd_exchange" if hosted else "attn_bwd", grid=(3,),
        in_specs=[pair, pair, _pair_spec(S, 2 * ATTN_WIDTH // LANES), pair, pair, pair] + ex.in_specs + snd.in_specs,
        out_specs=[pair] * 3 + ex.out_specs + snd.out_specs,
        out_shape=[jax.ShapeDtypeStruct((S, ATTN_WIDTH), f32)] * 3 + ex.out_shape + snd.out_shape,
        scratch_shapes=ex.scratch + snd.scratch, compiler_params=_cp("arbitrary"),
    )(qr, kr, proj, dattn, ltot, delta, *slabs, *send)
    return res[0], res[1], res[2], list(res[3:3 + n]), list(res[3 + n:])


def _softplus_neg(lam):
    return jnp.maximum(-lam, 0.0) + jnp.log1p(jnp.exp(-jnp.abs(lam)))


PROJ_LRU_X, PROJ_LRU_GATE, PROJ_S5_U = 3, 4, 5
EARLY_OWNER = 1


def _lru_pre(proj, conv_w, conv_b, wr, br, wi, bi, lam):
    S, W = proj.shape[0], LRU_WIDTH
    rt = _pick(S, ROW_TILE, SUBLANES)
    K = conv_w.shape[0]

    def body(x_ref, xp_ref, cw_ref, cb_ref, wr_ref, br_ref, wi_ref, bi_ref, lam_ref,
             xc_ref, r_ref, i_ref, la_ref, u_ref):
        prev = jnp.where(pl.program_id(0) == 0, 0.0, xp_ref[...])
        x = x_ref[...]
        xc = cb_ref[...] + cw_ref[K - 1:K, :] * x
        for k in range(K - 1):
            xc = xc + cw_ref[k:k + 1, :] * _shift_down(x, prev, K - 1 - k)
        xb = xc.astype(bf16)
        r = _sigmoid(_dot(xb, wr_ref[...], NN) + br_ref[...])
        i = _sigmoid(_dot(xb, wi_ref[...], NN) + bi_ref[...])
        log_a = -LRU_C * r * _softplus_neg(lam_ref[...])
        u = jnp.sqrt(-_expm1(2.0 * log_a)) * (i * xc)
        xc_ref[...], r_ref[...], i_ref[...], la_ref[...], u_ref[...] = xc, r, i, log_a, u

    row = pl.BlockSpec((rt, W), lambda i: (i, 0))
    xrow = pl.BlockSpec((rt, W), lambda i: (i, PROJ_LRU_X))
    halo = _prev_halo_spec(rt, W, lambda i: PROJ_LRU_X)
    vec = pl.BlockSpec((1, W), lambda i: (0, 0))
    return pl.pallas_call(
        body, name="lru_pre", grid=(S // rt,),
        in_specs=[xrow, halo, pl.BlockSpec((K, W), lambda i: (0, 0)), vec,
                  pl.BlockSpec((W, W), lambda i: (0, 0)), vec, pl.BlockSpec((W, W), lambda i: (0, 0)), vec, vec],
        out_specs=[row] * 5, out_shape=[jax.ShapeDtypeStruct((S, W), f32)] * 5, compiler_params=_cp("parallel"),
    )(proj, proj, conv_w, conv_b.reshape(1, W), wr, br.reshape(1, W), wi, bi.reshape(1, W), lam.reshape(1, W))


def _tile_rows(shape):
    return lax.broadcasted_iota(jnp.int32, shape, 0)


def _lru_scan(log_a, u, proj):
    S, W = u.shape
    rt = _pick(S, ROW_TILE, SUBLANES)
    T = SUBLANES

    def body(la_ref, u_ref, g_ref, h_ref, o_ref, carry):
        @pl.when(pl.program_id(0) == 0)
        def _():
            carry[...] = jnp.zeros_like(carry)

        row = _tile_rows((T, W))

        def step(t, hp):
            r0 = pl.multiple_of(t * T, T)
            a = jnp.exp(la_ref[pl.ds(r0, T), :])
            x = u_ref[pl.ds(r0, T), :]
            for k in (1, 2, 4):
                x = x + a * jnp.where(row >= k, pltpu.roll(x, k, 0), 0.0)
                a = a * jnp.where(row >= k, pltpu.roll(a, k, 0), 1.0)
            h = x + a * hp
            h_ref[pl.ds(r0, T), :] = h
            o_ref[pl.ds(r0, T), :] = h * _gelu(g_ref[pl.ds(r0, T), :])
            return h[T - 1:T, :]

        carry[0:1, :] = lax.fori_loop(0, rt // T, step, carry[0:1, :])

    row = pl.BlockSpec((rt, W), lambda i: (i, 0))
    grow = pl.BlockSpec((rt, W), lambda i: (i, PROJ_LRU_GATE))
    return pl.pallas_call(
        body, name="lru_scan", grid=(S // rt,), in_specs=[row, row, grow], out_specs=[row] * 2,
        out_shape=[jax.ShapeDtypeStruct((S, W), f32)] * 2, scratch_shapes=[pltpu.VMEM((T, W), f32)],
        compiler_params=_cp("arbitrary"),
    )(log_a, u, proj)


def _lru_scan_bwd(dlru, proj, h, log_a):
    S, W = h.shape
    rt = _pick(S, ROW_TILE, SUBLANES)
    T = SUBLANES
    nblk = S // rt

    def body(d_ref, g_ref, h_ref, la_ref, go_ref, dg_ref, carry):
        @pl.when(pl.program_id(0) == 0)
        def _():
            carry[...] = jnp.zeros_like(carry)

        row = _tile_rows((T, W))

        def step(j, c):
            gn, an = c
            t = rt // T - 1 - j
            r0 = pl.multiple_of(t * T, T)
            d = d_ref[pl.ds(r0, T), :]
            gate = g_ref[pl.ds(r0, T), :]
            a = jnp.exp(la_ref[pl.ds(r0, T), :])
            dg_ref[pl.ds(r0, T), :] = d * h_ref[pl.ds(r0, T), :] * _gelu_grad(gate)
            x = d * _gelu(gate)
            b = jnp.where(row < T - 1, pltpu.roll(a, T - 1, 0), an)
            for k in (1, 2, 4):
                x = x + b * jnp.where(row < T - k, pltpu.roll(x, T - k, 0), 0.0)
                b = b * jnp.where(row < T - k, pltpu.roll(b, T - k, 0), 1.0)
            g = x + b * gn
            go_ref[pl.ds(r0, T), :] = g
            return g[0:1, :], a[0:1, :]

        gn, an = lax.fori_loop(0, rt // T, step, (carry[0:1, :], carry[1:2, :]))
        carry[0:1, :] = gn
        carry[1:2, :] = an

    row = pl.BlockSpec((rt, W), lambda i: (nblk - 1 - i, 0))
    grow = pl.BlockSpec((rt, W), lambda i: (nblk - 1 - i, PROJ_LRU_GATE))
    return pl.pallas_call(
        body, name="lru_scan_bwd", grid=(nblk,), in_specs=[row, grow, row, row], out_specs=[row] * 2,
        out_shape=[jax.ShapeDtypeStruct((S, W), f32)] * 2, scratch_shapes=[pltpu.VMEM((T, W), f32)],
        compiler_params=_cp("arbitrary"),
    )(dlru, proj, h, log_a)


def _lru_gate_bwd(g, h, xc, r, i, log_a, wr, wi, lam):
    S, W = g.shape
    rt = _pick(S, ROW_TILE, SUBLANES)

    def body(g_ref, h_ref, hp_ref, xc_ref, r_ref, i_ref, la_ref, wr_ref, wi_ref, lam_ref,
             dxc_ref, dwr_ref, dwi_ref, acc_ref):
        @pl.when(pl.program_id(0) == 0)
        def _():
            dwr_ref[...] = jnp.zeros_like(dwr_ref)
            dwi_ref[...] = jnp.zeros_like(dwi_ref)
            acc_ref[...] = jnp.zeros_like(acc_ref)

        prev = jnp.where(pl.program_id(0) == 0, 0.0, hp_ref[...])
        gg, xc, r, i, log_a, lam = g_ref[...], xc_ref[...], r_ref[...], i_ref[...], la_ref[...], lam_ref[...]
        hm1 = _shift_down(h_ref[...], prev, 1)
        a = jnp.exp(log_a)
        s = jnp.sqrt(-_expm1(2.0 * log_a))
        da = gg * hm1
        di = gg * s * xc
        dxc = gg * s * i
        ds = gg * i * xc
        dlog_a = da * a - ds * (a * a / s)
        sp = _softplus_neg(lam)
        dr = dlog_a * (-LRU_C * sp)
        dsp = jnp.sum(dlog_a * (-LRU_C * r), axis=0, keepdims=True)
        dpr = dr * r * (1.0 - r)
        dpi = di * i * (1.0 - i)
        dprb, dpib, xb = dpr.astype(bf16), dpi.astype(bf16), xc.astype(bf16)
        dxc_ref[...] = dxc + _dot(dprb, wr_ref[...], NT) + _dot(dpib, wi_ref[...], NT)
        dwr_ref[...] += _dot(xb, dprb, TN)
        dwi_ref[...] += _dot(xb, dpib, TN)
        acc_ref[0:1, :] += jnp.sum(dpr, axis=0, keepdims=True)
        acc_ref[1:2, :] += jnp.sum(dpi, axis=0, keepdims=True)
        acc_ref[2:3, :] += dsp * (-_sigmoid(-lam))

    row = pl.BlockSpec((rt, W), lambda i: (i, 0))
    halo = _prev_halo_spec(rt, W, lambda i: 0)
    vec = pl.BlockSpec((1, W), lambda i: (0, 0))
    mat = pl.BlockSpec((W, W), lambda i: (0, 0))
    acc = pl.BlockSpec((SUBLANES, W), lambda i: (0, 0))
    return pl.pallas_call(
        body, name="lru_gate_bwd", grid=(S // rt,),
        in_specs=[row, row, halo, row, row, row, row, mat, mat, vec], out_specs=[row, mat, mat, acc],
        out_shape=[jax.ShapeDtypeStruct((S, W), f32), jax.ShapeDtypeStruct((W, W), f32),
                   jax.ShapeDtypeStruct((W, W), f32), jax.ShapeDtypeStruct((SUBLANES, W), f32)],
        compiler_params=_cp("arbitrary"),
    )(g, h, h, xc, r, i, log_a, wr, wi, lam.reshape(1, W))


def _conv_bwd(dy, x, conv_w, name, col_tile=None, out_dtype=f32, x_col_block=0, send=(), send_dst=0):
    if dy.ndim == 2:
        dy = dy[None]
    H, S, Ch = dy.shape
    C = H * Ch
    K = conv_w.shape[0]
    ct = Ch if col_tile is None else col_tile
    nct = Ch // ct
    rt = _pick(S, ROW_TILE, SUBLANES)
    nrt = S // rt
    snd = _SiblingSend(send, send_dst)
    n = snd.n

    def body(*refs):
        dy_ref, dyn_ref, x_ref, w_ref = refs[:4]
        s_ins = refs[4:4 + n]
        dx_ref, acc_ref = refs[4 + n:6 + n]
        s_outs, s_sems = refs[6 + n:6 + 2 * n], refs[6 + 2 * n:]
        i = pl.program_id(2)
        if n:
            @pl.when((pl.program_id(0) == 0) & (pl.program_id(1) == 0) & (i == 0))
            def _():
                snd.start(s_ins, s_outs, s_sems)

        @pl.when(i == 0)
        def _():
            acc_ref[...] = jnp.zeros_like(acc_ref)

        nxt = jnp.where(i == nrt - 1, 0.0, dyn_ref[...])
        dy, x = dy_ref[...], x_ref[...]
        ahead = [dy] + [_shift_up(dy, nxt, j) for j in range(1, K)]
        dx = w_ref[K - 1:K, :] * dy
        for k in range(K - 1):
            dx = dx + w_ref[k:k + 1, :] * ahead[K - 1 - k]
        dx_ref[...] = dx.astype(dx_ref.dtype)
        for k in range(K):
            acc_ref[k:k + 1, :] += jnp.sum(ahead[K - 1 - k] * x, axis=0, keepdims=True)
        acc_ref[K:K + 1, :] += jnp.sum(dy, axis=0, keepdims=True)
        if n:
            @pl.when((pl.program_id(0) == H - 1) & (pl.program_id(1) == nct - 1) & (i == nrt - 1))
            def _():
                snd.finish(s_ins, s_outs, s_sems)

    per, last = rt // SUBLANES, S // SUBLANES - 1
    dy_row = pl.BlockSpec((None, rt, ct), lambda h, j, i: (h, i, j))
    dy_next = pl.BlockSpec((None, SUBLANES, ct), lambda h, j, i: (h, jnp.minimum((i + 1) * per, last), j))
    row = pl.BlockSpec((rt, ct), lambda h, j, i: (i, h * nct + j))
    xrow = pl.BlockSpec((rt, ct), lambda h, j, i: (i, h * nct + j + x_col_block))
    res = pl.pallas_call(
        body, name=name, grid=(H, nct, nrt),
        in_specs=[dy_row, dy_next, xrow, pl.BlockSpec((K, ct), lambda h, j, i: (0, h * nct + j))] + snd.in_specs,
        out_specs=[row, pl.BlockSpec((SUBLANES, ct), lambda h, j, i: (0, h * nct + j))] + snd.out_specs,
        out_shape=[jax.ShapeDtypeStruct((S, C), out_dtype), jax.ShapeDtypeStruct((SUBLANES, C), f32)] + snd.out_shape,
        scratch_shapes=snd.scratch,
        compiler_params=_cp(*(("arbitrary",) * 3 if n else ("parallel", "parallel", "arbitrary"))),
    )(dy, dy, x, conv_w, *send)
    return res[0], res[1], list(res[2:])


def _s5_param_fn(a_re, a_im, ls, bt_re, bt_im):
    step = jnp.exp(ls)
    dt_re, dt_im = step * a_re, step * a_im
    mag = jnp.exp(dt_re)
    ab_re, ab_im = mag * jnp.cos(dt_im), mag * jnp.sin(dt_im)
    z_re, z_im = ab_re - 1.0, ab_im
    den = a_re * a_re + a_im * a_im
    f_re = (z_re * a_re + z_im * a_im) / den
    f_im = (z_im * a_re - z_re * a_im) / den
    bb_re = f_re[:, None, :] * bt_re - f_im[:, None, :] * bt_im
    bb_im = f_re[:, None, :] * bt_im + f_im[:, None, :] * bt_re
    return ab_re, ab_im, bb_re, bb_im


def _s5_params(a_re, a_im, ls, bt_re, bt_im):
    def body(ar, ai, l, br, bi, o_ar, o_ai, o_br, o_bi):
        o_ar[...], o_ai[...], o_br[...], o_bi[...] = _s5_param_fn(ar[...], ai[...], l[...], br[...], bi[...])

    return pl.pallas_call(
        body, name="s5_params",
        out_shape=[jax.ShapeDtypeStruct(a_re.shape, f32)] * 2 + [jax.ShapeDtypeStruct(bt_re.shape, f32)] * 2,
        compiler_params=_cp(),
    )(a_re, a_im, ls, bt_re, bt_im)


def _s5_params_bwd(a_re, a_im, ls, bt_re, bt_im, d_ar, d_ai, d_br, d_bi):
    def body(ar, ai, l, br, bi, c_ar, c_ai, c_br, c_bi, g_ar, g_ai, g_l, g_br, g_bi):
        _, vjp = jax.vjp(_s5_param_fn, ar[...], ai[...], l[...], br[...], bi[...])
        g_ar[...], g_ai[...], g_l[...], g_br[...], g_bi[...] = vjp((c_ar[...], c_ai[...], c_br[...], c_bi[...]))

    return pl.pallas_call(
        body, name="s5_params_bwd",
        out_shape=[jax.ShapeDtypeStruct(a_re.shape, f32)] * 2 + [jax.ShapeDtypeStruct(ls.shape, f32)]
        + [jax.ShapeDtypeStruct(bt_re.shape, f32)] * 2,
        compiler_params=_cp(),
    )(a_re, a_im, ls, bt_re, bt_im, d_ar, d_ai, d_br, d_bi)


S5_CHUNK = 256


def _s5_power_tables(ab_ref, p_ref, w_ref, conj):
    T, L = SUBLANES, S5_LANES
    are = ab_ref[0:1, 0:L]
    aim = ab_ref[0:1, L:2 * L]
    if conj:
        aim = -aim
    pre, pim = are, aim
    for n in range(3):
        p_ref[n:n + 1, 0:L] = pre
        p_ref[n:n + 1, L:2 * L] = pim
        pre, pim = pre * pre - pim * pim, 2.0 * pre * pim
    row = _tile_rows((T, L))
    wre = jnp.zeros((T, L), f32)
    wim = jnp.zeros((T, L), f32)
    pre, pim = are, aim
    for n in range(T):
        tgt = (T - 1 - n) if conj else n
        wre = jnp.where(row == tgt, pre, wre)
        wim = jnp.where(row == tgt, pim, wim)
        pre, pim = pre * are - pim * aim, pre * aim + pim * are
    w_ref[:, 0:L] = wre
    w_ref[:, L:2 * L] = wim


def _s5_scan(bu, ab):
    S, L2 = bu.shape
    L = L2 // 2
    rt = _pick(S, 256, SUBLANES)
    T = SUBLANES
    CH = S5_CHUNK

    def body(bu_ref, ab_ref, x_ref, p_ref, w_ref, carry):
        @pl.when(pl.program_id(0) == 0)
        def _():
            carry[...] = jnp.zeros_like(carry)
            _s5_power_tables(ab_ref, p_ref, w_ref, conj=False)

        row = _tile_rows((T, CH))

        def step(t, _):
            r0 = pl.multiple_of(t * T, T)
            for c in range(L // CH):
                lre, lim = pl.ds(c * CH, CH), pl.ds(L + c * CH, CH)
                xr, xi = bu_ref[pl.ds(r0, T), lre], bu_ref[pl.ds(r0, T), lim]
                for n, k in enumerate((1, 2, 4)):
                    pr, pi = p_ref[n:n + 1, lre], p_ref[n:n + 1, lim]
                    sr = jnp.where(row >= k, pltpu.roll(xr, k, 0), 0.0)
                    si = jnp.where(row >= k, pltpu.roll(xi, k, 0), 0.0)
                    xr, xi = xr + pr * sr - pi * si, xi + pr * si + pi * sr
                cr, ci = carry[T - 1:T, lre], carry[T - 1:T, lim]
                wr, wi = w_ref[:, lre], w_ref[:, lim]
                xr, xi = xr + wr * cr - wi * ci, xi + wr * ci + wi * cr
                carry[:, lre] = xr
                carry[:, lim] = xi
                x_ref[pl.ds(r0, T), lre] = xr
                x_ref[pl.ds(r0, T), lim] = xi
            return 0

        lax.fori_loop(0, rt // T, step, 0)

    row_spec = pl.BlockSpec((rt, L2), lambda i: (i, 0))
    return pl.pallas_call(
        body, name="s5_scan", grid=(S // rt,), in_specs=[row_spec, pl.BlockSpec((1, L2), lambda i: (0, 0))],
        out_specs=row_spec, out_shape=jax.ShapeDtypeStruct((S, L2), f32),
        scratch_shapes=[pltpu.VMEM((T, L2), f32), pltpu.VMEM((T, L2), f32), pltpu.VMEM((T, L2), f32)],
        compiler_params=_cp("arbitrary"),
    )(bu, ab)


def _s5_scan_bwd(dx, x, ab):
    S, L2 = dx.shape
    L = L2 // 2
    rt = _pick(S, 256, SUBLANES)
    T = SUBLANES
    CH = S5_CHUNK
    nblk = S // rt
    per = rt // T

    def body(dx_ref, x_ref, xp_ref, ab_ref, g_ref, da_ref, p_ref, w_ref, carry, acc):
        pid = pl.program_id(0)

        @pl.when(pid == 0)
        def _():
            carry[...] = jnp.zeros_like(carry)
            acc[...] = jnp.zeros_like(acc)
            _s5_power_tables(ab_ref, p_ref, w_ref, conj=True)

        row = _tile_rows((T, CH))
        first_block = pid == nblk - 1

        def step(j, _):
            t = per - 1 - j
            r0 = pl.multiple_of(t * T, T)
            rp = pl.multiple_of(jnp.maximum(t - 1, 0) * T, T)
            for c in range(L // CH):
                lre, lim = pl.ds(c * CH, CH), pl.ds(L + c * CH, CH)
                gr, gi = dx_ref[pl.ds(r0, T), lre], dx_ref[pl.ds(r0, T), lim]
                for n, k in enumerate((1, 2, 4)):
                    pr, pi = p_ref[n:n + 1, lre], p_ref[n:n + 1, lim]
                    sr = jnp.where(row < T - k, pltpu.roll(gr, T - k, 0), 0.0)
                    si = jnp.where(row < T - k, pltpu.roll(gi, T - k, 0), 0.0)
                    gr, gi = gr + pr * sr - pi * si, gi + pr * si + pi * sr
                cr, ci = carry[0:1, lre], carry[0:1, lim]
                wr, wi = w_ref[:, lre], w_ref[:, lim]
                gr, gi = gr + wr * cr - wi * ci, gi + wr * ci + wi * cr
                carry[:, lre] = gr
                carry[:, lim] = gi
                g_ref[pl.ds(r0, T), lre] = gr
                g_ref[pl.ds(r0, T), lim] = gi
                xr, xi = x_ref[pl.ds(r0, T), lre], x_ref[pl.ds(r0, T), lim]
                in_blk_r, in_blk_i = x_ref[pl.ds(rp, T), lre], x_ref[pl.ds(rp, T), lim]
                hal_r = jnp.where(first_block, 0.0, xp_ref[:, lre])
                hal_i = jnp.where(first_block, 0.0, xp_ref[:, lim])
                pvr = jnp.where(t == 0, hal_r, in_blk_r)[T - 1:T, :]
                pvi = jnp.where(t == 0, hal_i, in_blk_i)[T - 1:T, :]
                sxr = jnp.where(row >= 1, pltpu.roll(xr, 1, 0), pvr)
                sxi = jnp.where(row >= 1, pltpu.roll(xi, 1, 0), pvi)
                acc[:, lre] += gr * sxr + gi * sxi
                acc[:, lim] += gi * sxr - gr * sxi
            return 0

        lax.fori_loop(0, per, step, 0)

        @pl.when(pid == nblk - 1)
        def _():
            da_ref[...] = jnp.sum(acc[...], axis=0, keepdims=True)

    row_spec = pl.BlockSpec((rt, L2), lambda i: (nblk - 1 - i, 0))
    halo = pl.BlockSpec((T, L2), lambda i: (jnp.maximum((nblk - 1 - i) * per - 1, 0), 0))
    vec = pl.BlockSpec((1, L2), lambda i: (0, 0))
    return pl.pallas_call(
        body, name="s5_scan_bwd", grid=(nblk,), in_specs=[row_spec, row_spec, halo, vec],
        out_specs=[row_spec, vec],
        out_shape=[jax.ShapeDtypeStruct((S, L2), f32), jax.ShapeDtypeStruct((1, L2), f32)],
        scratch_shapes=[pltpu.VMEM((T, L2), f32)] * 4,
        compiler_params=_cp("arbitrary"),
    )(dx, x, x, ab)


def _S5_U_SPEC(rt):
    return pl.BlockSpec((rt, LRU_WIDTH), lambda i: (i, PROJ_S5_U))


def _s5_out(yc, proj, d, wglu, bglu):
    S, W = yc.shape
    rt = _pick(S, ROW_TILE, SUBLANES)

    def body(yc_ref, u_ref, d_ref, w_ref, b_ref, o_ref, y_ref):
        y = yc_ref[...] + d_ref[...] * u_ref[:, 0:W]
        yg = _gelu(y)
        z = _dot(yg.astype(bf16), w_ref[...], NN) + b_ref[...]
        o_ref[...] = yg * _sigmoid(z)
        y_ref[...] = y

    row = pl.BlockSpec((rt, W), lambda i: (i, 0))
    vec = pl.BlockSpec((1, W), lambda i: (0, 0))
    mat = pl.BlockSpec((W, W), lambda i: (0, 0))
    return pl.pallas_call(
        body, name="s5_out", grid=(S // rt,), in_specs=[row, _S5_U_SPEC(rt), vec, mat, vec], out_specs=[row, row],
        out_shape=[jax.ShapeDtypeStruct((S, W), f32)] * 2, compiler_params=_cp("parallel"),
    )(yc, proj, d.reshape(1, W), wglu, bglu.reshape(1, W))


def _s5_out_bwd(dssm, y, proj, d, wglu, bglu):
    S, W = y.shape
    rt = _pick(S, ROW_TILE, SUBLANES)

    def body(do_ref, y_ref, u_ref, d_ref, w_ref, b_ref, dy_ref, du_ref, dw_ref, acc_ref):
        @pl.when(pl.program_id(0) == 0)
        def _():
            dw_ref[...] = jnp.zeros_like(dw_ref)
            acc_ref[...] = jnp.zeros_like(acc_ref)

        do, y = do_ref[...], y_ref[...]
        yg = _gelu(y)
        ygb = yg.astype(bf16)
        sg = _sigmoid(_dot(ygb, w_ref[...], NN) + b_ref[...])
        dz = do * yg * sg * (1.0 - sg)
        dzb = dz.astype(bf16)
        dyg = do * sg + _dot(dzb, w_ref[...], NT)
        dy = dyg * _gelu_grad(y)
        dy_ref[...] = dy
        du_ref[...] = dy * d_ref[...]
        dw_ref[...] += _dot(ygb, dzb, TN)
        acc_ref[0:1, :] += jnp.sum(dz, axis=0, keepdims=True)
        acc_ref[1:2, :] += jnp.sum(dy * u_ref[:, 0:W], axis=0, keepdims=True)

    row = pl.BlockSpec((rt, W), lambda i: (i, 0))
    vec = pl.BlockSpec((1, W), lambda i: (0, 0))
    mat = pl.BlockSpec((W, W), lambda i: (0, 0))
    acc = pl.BlockSpec((SUBLANES, W), lambda i: (0, 0))
    return pl.pallas_call(
        body, name="s5_out_bwd", grid=(S // rt,), in_specs=[row, row, _S5_U_SPEC(rt), vec, mat, vec],
        out_specs=[row, row, mat, acc],
        out_shape=[jax.ShapeDtypeStruct((S, W), f32)] * 2
        + [jax.ShapeDtypeStruct((W, W), f32), jax.ShapeDtypeStruct((SUBLANES, W), f32)],
        compiler_params=_cp("arbitrary"),
    )(dssm, y, proj, d.reshape(1, W), wglu, bglu.reshape(1, W))


MIX_SPLITS = ((0, ATTN_WIDTH), (ATTN_WIDTH, ATTN_WIDTH + LRU_WIDTH), (ATTN_WIDTH + LRU_WIDTH, D_MODEL))


def _mixnorm(attn, lru, ssm, g):
    S = attn.shape[0]
    rt = _pick(S, ROW_TILE, SUBLANES)

    def body(a_ref, l_ref, s_ref, g_ref, o_ref):
        for ref, (lo, hi) in zip((a_ref, l_ref, s_ref), MIX_SPLITS):
            x = ref[...]
            ms = jnp.mean(x * x, axis=-1, keepdims=True)
            o_ref[:, lo:hi] = (x * lax.rsqrt(ms + RMS_EPS) * g_ref[:, lo:hi]).astype(o_ref.dtype)

    rows = [pl.BlockSpec((rt, hi - lo), lambda i: (i, 0)) for lo, hi in MIX_SPLITS]
    return pl.pallas_call(
        body, name="mixnorm", grid=(S // rt,), in_specs=rows + [pl.BlockSpec((1, D_MODEL), lambda i: (0, 0))],
        out_specs=pl.BlockSpec((rt, D_MODEL), lambda i: (i, 0)),
        out_shape=jax.ShapeDtypeStruct((S, D_MODEL), bf16), compiler_params=_cp("parallel"),
    )(attn, lru, ssm, g.reshape(1, D_MODEL))


def _mixnorm_bwd(dmixed, attn, lru, ssm, g):
    S = attn.shape[0]
    rt = _pick(S, ROW_TILE, SUBLANES)

    def body(d_ref, a_ref, l_ref, s_ref, g_ref, da_ref, dl_ref, ds_ref, dlt_ref, acc_ref):
        @pl.when(pl.program_id(0) == 0)
        def _():
            acc_ref[...] = jnp.zeros_like(acc_ref)

        outs = []
        for ref, (lo, hi) in zip((a_ref, l_ref, s_ref), MIX_SPLITS):
            x = ref[...]
            dy = d_ref[:, lo:hi]
            rinv = lax.rsqrt(jnp.mean(x * x, axis=-1, keepdims=True) + RMS_EPS)
            dyg = dy * g_ref[:, lo:hi]
            outs.append(rinv * dyg - x * (rinv * rinv * rinv) * jnp.mean(dyg * x, axis=-1, keepdims=True))
            acc_ref[0:1, lo:hi] += jnp.sum(dy * x * rinv, axis=0, keepdims=True)
        da_ref[...], dl_ref[...], ds_ref[...] = outs
        hi_ = lax.broadcasted_iota(jnp.int32, (ATTN_WIDTH, ATTN_WIDTH), 0) // HEAD_DIM
        hj_ = lax.broadcasted_iota(jnp.int32, (ATTN_WIDTH, ATTN_WIDTH), 1) // HEAD_DIM
        same = jnp.where(hi_ == hj_, 1.0, 0.0).astype(f32)
        dlt_ref[...] = jnp.dot(outs[0] * a_ref[...], same, precision=lax.Precision.HIGHEST, preferred_element_type=f32)

    rows = [pl.BlockSpec((rt, hi - lo), lambda i: (i, 0)) for lo, hi in MIX_SPLITS]
    full = pl.BlockSpec((rt, D_MODEL), lambda i: (i, 0))
    return pl.pallas_call(
        body, name="mixnorm_bwd", grid=(S // rt,),
        in_specs=[full] + rows + [pl.BlockSpec((1, D_MODEL), lambda i: (0, 0))],
        out_specs=rows + [rows[0], pl.BlockSpec((SUBLANES, D_MODEL), lambda i: (0, 0))],
        out_shape=[jax.ShapeDtypeStruct((S, hi - lo), f32) for lo, hi in MIX_SPLITS]
        + [jax.ShapeDtypeStruct((S, ATTN_WIDTH), f32), jax.ShapeDtypeStruct((SUBLANES, D_MODEL), f32)],
        compiler_params=_cp("arbitrary"),
    )(dmixed, attn, lru, ssm, g.reshape(1, D_MODEL))


FFN_COL_TILE = 1536


def _ffn_conv(x, prev, w_ref, b_ref, K):
    y = b_ref[...] + w_ref[K - 1:K, :] * x
    for k in range(K - 1):
        y = y + w_ref[k:k + 1, :] * _shift_down(x, prev, K - 1 - k)
    return y


def _ffn_act(up, conv_w, conv_b):
    S, C2 = up.shape
    C = C2 // 2
    K = conv_w.shape[0]
    ct = FFN_COL_TILE
    nct = C // ct
    rt = _pick(S, ROW_TILE, SUBLANES)

    def body(g_ref, gp_ref, v_ref, vp_ref, wg_ref, wv_ref, bg_ref, bv_ref, o_ref):
        first = pl.program_id(1) == 0
        gate = _ffn_conv(g_ref[...], jnp.where(first, 0.0, gp_ref[...]), wg_ref, bg_ref, K)
        val = _ffn_conv(v_ref[...], jnp.where(first, 0.0, vp_ref[...]), wv_ref, bv_ref, K)
        o_ref[...] = (_gelu(gate) * val).astype(o_ref.dtype)

    def specs(off):
        return (pl.BlockSpec((rt, ct), lambda j, i: (i, j + off)), _prev_halo_spec(rt, ct, lambda j, i: j + off))

    def wspec(off, rows):
        return pl.BlockSpec((rows, ct), lambda j, i: (0, j + off))

    g_s, gp_s = specs(0)
    v_s, vp_s = specs(nct)
    return pl.pallas_call(
        body, name="ffn_act", grid=(nct, S // rt),
        in_specs=[g_s, gp_s, v_s, vp_s, wspec(0, K), wspec(nct, K), wspec(0, 1), wspec(nct, 1)],
        out_specs=pl.BlockSpec((rt, ct), lambda j, i: (i, j)),
        out_shape=jax.ShapeDtypeStruct((S, C), bf16), compiler_params=_cp("parallel", "parallel"),
    )(up, up, up, up, conv_w, conv_w, conv_b.reshape(1, C2), conv_b.reshape(1, C2))


def _ffn_act_bwd(dact, up, conv_w, conv_b):
    S, C2 = up.shape
    C = C2 // 2
    K = conv_w.shape[0]
    ct = FFN_COL_TILE
    nct = C // ct
    rt = _pick(S, ROW_TILE, SUBLANES)

    def body(d_ref, g_ref, gp_ref, v_ref, vp_ref, wg_ref, wv_ref, bg_ref, bv_ref, o_ref):
        first = pl.program_id(1) == 0
        gate = _ffn_conv(g_ref[...], jnp.where(first, 0.0, gp_ref[...]), wg_ref, bg_ref, K)
        val = _ffn_conv(v_ref[...], jnp.where(first, 0.0, vp_ref[...]), wv_ref, bv_ref, K)
        d = d_ref[...]
        gl, dgl = _gelu_pair(gate)
        o_ref[0] = d * val * dgl
        o_ref[1] = d * gl

    def specs(off):
        return (pl.BlockSpec((rt, ct), lambda j, i: (i, j + off)), _prev_halo_spec(rt, ct, lambda j, i: j + off))

    def wspec(off, rows):
        return pl.BlockSpec((rows, ct), lambda j, i: (0, j + off))

    g_s, gp_s = specs(0)
    v_s, vp_s = specs(nct)
    return pl.pallas_call(
        body, name="ffn_act_bwd", grid=(nct, S // rt),
        in_specs=[pl.BlockSpec((rt, ct), lambda j, i: (i, j)), g_s, gp_s, v_s, vp_s,
                  wspec(0, K), wspec(nct, K), wspec(0, 1), wspec(nct, 1)],
        out_specs=pl.BlockSpec((2, rt, ct), lambda j, i: (0, i, j)),
        out_shape=jax.ShapeDtypeStruct((2, S, C), f32), compiler_params=_cp("parallel", "parallel"),
    )(dact, up, up, up, up, conv_w, conv_w, conv_b.reshape(1, C2), conv_b.reshape(1, C2))


ANY = pl.BlockSpec(memory_space=pl.ANY)


def _rows_for(cols):
    return max(16, (1 << 19) // cols)


def _chips(x, y):
    return [(1 - x, y), (x, 1 - y), (1 - x, 1 - y)]


class _Gather:
    def __init__(self, shards, split):
        self.shapes = [s.shape for s in shards]
        self.dtypes = [s.dtype for s in shards]
        self.split = list(split)
        self.nt = len(shards)
        self.in_specs = [ANY] * self.nt
        self.out_specs = [ANY] * self.nt
        self.out_shape = [jax.ShapeDtypeStruct((4,) + s, dt) for s, dt in zip(self.shapes, self.dtypes)]
        self.scratch = [pltpu.SemaphoreType.DMA((3, self.nt))] * 4 if self.nt else []

    def _part(self, ref, t, half):
        if not self.split[t]:
            return ref
        r = self.shapes[t][0] // 2
        return ref.at[pl.ds(half * r, r), :]

    def _ici(self, ins, outs, sems, k, t, chip, landing_chip):
        x, y, c = lax.axis_index("x"), lax.axis_index("y"), lax.axis_index("c")
        return pltpu.make_async_remote_copy(
            src_ref=self._part(ins[t], t, c), dst_ref=self._part(outs[t].at[landing_chip], t, c),
            send_sem=sems[0].at[k, t], recv_sem=sems[1].at[k, t], device_id=(chip[0], chip[1], c), device_id_type=MESH)

    def _d2d(self, outs, sems, k, t, q, half):
        x, y, c = lax.axis_index("x"), lax.axis_index("y"), lax.axis_index("c")
        rows = self._part(outs[t].at[q], t, half)
        return pltpu.make_async_remote_copy(
            src_ref=rows, dst_ref=rows, send_sem=sems[2].at[k, t], recv_sem=sems[3].at[k, t],
            device_id=(x, y, 1 - c), device_id_type=MESH)

    def start(self, ins, outs, sems):
        x, y = lax.axis_index("x"), lax.axis_index("y")
        me = 2 * x + y
        for k, chip in enumerate(_chips(x, y)):
            for t in range(self.nt):
                self._ici(ins, outs, sems, k, t, chip, me).start()

    def finish(self, ins, outs, sems):
        x, y, c = lax.axis_index("x"), lax.axis_index("y"), lax.axis_index("c")
        me = 2 * x + y
        chips = _chips(x, y)
        for k, chip in enumerate(chips):
            q = 2 * chip[0] + chip[1]
            for t in range(self.nt):
                self._ici(ins, outs, sems, k, t, chip, q).wait_recv()
                if self.split[t]:
                    self._d2d(outs, sems, k, t, q, c).start()
        for k, chip in enumerate(chips):
            q = 2 * chip[0] + chip[1]
            for t in range(self.nt):
                if self.split[t]:
                    self._d2d(outs, sems, k, t, q, 1 - c).wait_recv()
        for k, chip in enumerate(chips):
            q = 2 * chip[0] + chip[1]
            for t in range(self.nt):
                self._ici(ins, outs, sems, k, t, chip, me).wait_send()
                if self.split[t]:
                    self._d2d(outs, sems, k, t, q, c).wait_send()


def _gather_weights(shards, split):
    g = _Gather(shards, split)
    nt = g.nt

    def body(*refs):
        ins, outs, sems = refs[:nt], refs[nt:2 * nt], refs[2 * nt:]
        g.start(ins, outs, sems)
        g.finish(ins, outs, sems)

    return pl.pallas_call(
        body, name="gather_weights", in_specs=g.in_specs, out_specs=g.out_specs, out_shape=g.out_shape,
        scratch_shapes=g.scratch,
    )(*shards)


class _SiblingSend:
    def __init__(self, gs, dst_c, swap=()):
        self.nt, self.n = len(gs), len(gs) + len(swap)
        self.dst_c = list(dst_c) if isinstance(dst_c, (list, tuple)) else [dst_c] * self.nt
        self.in_specs = [ANY] * self.n
        self.out_specs = [ANY] * self.n
        self.out_shape = [jax.ShapeDtypeStruct(g.shape, g.dtype) for g in list(gs) + list(swap)]
        self.scratch = [pltpu.SemaphoreType.DMA((self.n,))] * 2 if self.n else []

    def _each(self, ins, outs, sems, sender, fn):
        x, y, c = lax.axis_index("x"), lax.axis_index("y"), lax.axis_index("c")

        def cp(t):
            return pltpu.make_async_remote_copy(
                src_ref=ins[t], dst_ref=outs[t], send_sem=sems[0].at[t], recv_sem=sems[1].at[t],
                device_id=(x, y, 1 - c), device_id_type=MESH)

        for dst in (0, 1):
            which = [t for t in range(self.nt) if self.dst_c[t] == dst]
            if which:
                @pl.when((c != dst) if sender else (c == dst))
                def _(which=which):
                    for t in which:
                        fn(cp(t))
        for t in range(self.nt, self.n):
            fn(cp(t))

    def start(self, ins, outs, sems):
        self._each(ins, outs, sems, True, lambda cp: cp.start())

    def finish(self, ins, outs, sems):
        self._each(ins, outs, sems, False, lambda cp: cp.wait_recv())
        self._each(ins, outs, sems, True, lambda cp: cp.wait_send())


def _sibling_send(gs, dst_c, swap=()):
    snd = _SiblingSend(gs, dst_c, swap)
    n = snd.n

    def body(*refs):
        ins, outs, sems = refs[:n], refs[n:2 * n], refs[2 * n:]
        snd.start(ins, outs, sems)
        snd.finish(ins, outs, sems)

    res = pl.pallas_call(
        body, name="sibling_send", in_specs=snd.in_specs, out_specs=snd.out_specs, out_shape=snd.out_shape,
        scratch_shapes=snd.scratch,
    )(*gs, *swap)
    return list(res[:snd.nt]), list(res[snd.nt:])


def _owner_flag(owner_c):
    return (lax.axis_index("c") == owner_c).astype(jnp.int32).reshape(1)


def _pair_sum(g, other, name, owner_c, col_slabs=False):
    R, C = g.shape
    cb = C // 4 if col_slabs else C
    rt = _pick(R, _rows_for(cb), 16)

    def body(on_ref, a_ref, o_ref, out_ref):
        out_ref[...] = (a_ref[...] + o_ref[...]).astype(out_ref.dtype)

    row = pl.BlockSpec((rt, cb), lambda q, i, on: (i * on[0], q * on[0]))
    if col_slabs:
        out_spec = pl.BlockSpec((None, rt, cb), lambda q, i, on: (q * on[0], i * on[0], 0))
        out_shape = jax.ShapeDtypeStruct((4, R, cb), bf16)
    else:
        out_spec, out_shape = row, jax.ShapeDtypeStruct((R, C), bf16)
    return pl.pallas_call(
        body, name=name,
        grid_spec=pltpu.PrefetchScalarGridSpec(num_scalar_prefetch=1, grid=(C // cb, R // rt),
                                               in_specs=[row, row], out_specs=out_spec),
        out_shape=out_shape, compiler_params=_cp("arbitrary", "arbitrary"),
    )(_owner_flag(owner_c), g, other)


class _ChipExchange:
    def __init__(self, slabs, whole, only_c):
        self.ns, self.nw = len(slabs), len(whole)
        self.only_c = list(only_c) if isinstance(only_c, (list, tuple)) else [only_c] * self.ns
        self.n = self.ns + self.nw
        self.in_specs = [ANY] * self.n
        self.out_specs = [ANY] * self.n
        self.out_shape = ([jax.ShapeDtypeStruct(s.shape, s.dtype) for s in slabs]
                          + [jax.ShapeDtypeStruct((4,) + w.shape, w.dtype) for w in whole])
        self.scratch = [pltpu.SemaphoreType.DMA((3, self.n))] * 2 if self.n else []

    def _copy(self, ins, outs, sems, k, t, chip, landing_chip):
        c = lax.axis_index("c")
        src = ins[t].at[2 * chip[0] + chip[1]] if t < self.ns else ins[t]
        return pltpu.make_async_remote_copy(
            src_ref=src, dst_ref=outs[t].at[landing_chip], send_sem=sems[0].at[k, t], recv_sem=sems[1].at[k, t],
            device_id=(chip[0], chip[1], c), device_id_type=MESH)

    def _each(self, fn):
        x, y, c = lax.axis_index("x"), lax.axis_index("y"), lax.axis_index("c")
        chips = _chips(x, y)
        for owner in (0, 1):
            which = [t for t in range(self.ns) if self.only_c[t] == owner]
            if which:
                @pl.when(c == owner)
                def _(which=which):
                    for k, chip in enumerate(chips):
                        for t in which:
                            fn(k, t, chip)
        for k, chip in enumerate(chips):
            for t in range(self.ns, self.n):
                fn(k, t, chip)

    def start(self, ins, outs, sems):
        me = 2 * lax.axis_index("x") + lax.axis_index("y")
        self._each(lambda k, t, chip: self._copy(ins, outs, sems, k, t, chip, me).start())

    def finish(self, ins, outs, sems):
        me = 2 * lax.axis_index("x") + lax.axis_index("y")
        self._each(lambda k, t, chip: self._copy(ins, outs, sems, k, t, chip, 2 * chip[0] + chip[1]).wait_recv())
        self._each(lambda k, t, chip: self._copy(ins, outs, sems, k, t, chip, me).wait_send())


def _chip_exchange(slabs, whole, only_c):
    ex = _ChipExchange(slabs, whole, only_c)
    n = ex.n

    def body(*refs):
        ins, outs, sems = refs[:n], refs[n:2 * n], refs[2 * n:]
        ex.start(ins, outs, sems)
        ex.finish(ins, outs, sems)

    res = pl.pallas_call(
        body, name="chip_exchange", in_specs=ex.in_specs, out_specs=ex.out_specs, out_shape=ex.out_shape,
        scratch_shapes=ex.scratch,
    )(*slabs, *whole)
    return list(res[:ex.ns]), list(res[ex.ns:])


def _sum_chips(recv, own, name, owner_c=None):
    n, r, C = recv.shape
    rt = _pick(r, _rows_for(C), 16)
    own3 = own.ndim == 3

    def body(on_ref, r_ref, o_ref, out_ref):
        me = 2 * lax.axis_index("x") + lax.axis_index("y")
        acc = None
        for q in range(n):
            term = jnp.where(me == q, o_ref[q] if own3 else o_ref[...], r_ref[q]).astype(f32)
            acc = term if acc is None else acc + term
        out_ref[...] = acc

    blk = pl.BlockSpec((n, rt, C), lambda i, on: (0, i * on[0], 0))
    row = pl.BlockSpec((rt, C), lambda i, on: (i * on[0], 0))
    flag = jnp.ones((1,), jnp.int32) if owner_c is None else _owner_flag(owner_c)
    return pl.pallas_call(
        body, name=name,
        grid_spec=pltpu.PrefetchScalarGridSpec(num_scalar_prefetch=1, grid=(r // rt,),
                                               in_specs=[blk, blk if own3 else row], out_specs=row),
        out_shape=jax.ShapeDtypeStruct((r, C), f32), compiler_params=_cp("arbitrary"),
    )(flag, recv, own)


def _adamw_layers(mine, theirs, owners, w, m, v, name):
    L, r, C = w.shape
    rt = _pick(r, _rows_for(C), 16)

    def body(a0_ref, a1_ref, b0_ref, b1_ref, w_ref, m_ref, v_ref, g_ref, d_ref, mo_ref, vo_ref):
        layer, c = pl.program_id(0), lax.axis_index("c")
        g0 = jnp.where(c == owners[0], a0_ref[...], b0_ref[...])
        g1 = jnp.where(c == owners[1], a1_ref[...], b1_ref[...])
        g_ref[...] = jnp.where(layer == 0, g0, g1)
        _adamw_math(g_ref, w_ref, m_ref, v_ref, d_ref, mo_ref, vo_ref)

    flat = pl.BlockSpec((rt, C), lambda l, i: (i, 0))
    lay = pl.BlockSpec((None, rt, C), lambda l, i: (l, i, 0))
    return pl.pallas_call(
        body, name=name, grid=(L, r // rt), in_specs=[flat] * 4 + [lay] * 3, out_specs=[lay] * 4,
        out_shape=[jax.ShapeDtypeStruct((L, r, C), f32)] * 4, compiler_params=_cp("parallel", "parallel"),
    )(mine[0], mine[1], theirs[0], theirs[1], w, m, v)


def _adamw_math(g_ref, w_ref, m_ref, v_ref, d_ref, mo_ref, vo_ref):
    gg = g_ref[...]
    m_new = ADAM_B1 * m_ref[...] + (1.0 - ADAM_B1) * gg
    v_new = ADAM_B2 * v_ref[...] + (1.0 - ADAM_B2) * (gg * gg)
    m_hat = m_new / (1.0 - ADAM_B1 ** ADAM_STEP)
    v_hat = v_new / (1.0 - ADAM_B2 ** ADAM_STEP)
    d_ref[...] = -ADAM_LR * (m_hat / (jnp.sqrt(v_hat) + ADAM_EPS) + ADAM_WD * w_ref[...])
    mo_ref[...] = m_new
    vo_ref[...] = v_new


FLAT_TILE = 2048


def _add2(a, b, name):
    R = a.shape[0]
    rt = _pick(R, FLAT_TILE, SUBLANES)

    def body(a_ref, b_ref, o_ref):
        o_ref[...] = a_ref[...] + b_ref[...]

    row = pl.BlockSpec((rt, LANES), lambda i: (i, 0))
    return pl.pallas_call(
        body, name=name, grid=(R // rt,), in_specs=[row, row], out_specs=row,
        out_shape=jax.ShapeDtypeStruct((R, LANES), f32), compiler_params=_cp("parallel"),
    )(a, b)


def _adamw(g, w, m, v, name):
    R = g.shape[0]
    rt = _pick(R, FLAT_TILE, SUBLANES)

    def body(g_ref, w_ref, m_ref, v_ref, d_ref, mo_ref, vo_ref):
        _adamw_math(g_ref, w_ref, m_ref, v_ref, d_ref, mo_ref, vo_ref)

    row = pl.BlockSpec((rt, LANES), lambda i: (i, 0))
    return pl.pallas_call(
        body, name=name, grid=(R // rt,), in_specs=[row] * 4, out_specs=[row] * 3,
        out_shape=[jax.ShapeDtypeStruct((R, LANES), f32)] * 3, compiler_params=_cp("parallel"),
    )(g, w, m, v)


def _adamw_whole(g, w, m, v, name):
    def body(g_ref, w_ref, m_ref, v_ref, d_ref, mo_ref, vo_ref):
        _adamw_math(g_ref, w_ref, m_ref, v_ref, d_ref, mo_ref, vo_ref)

    return pl.pallas_call(
        body, name=name, out_shape=[jax.ShapeDtypeStruct(w.shape, f32)] * 3, compiler_params=_cp(),
    )(g, w, m, v)


def _pack(arrs, dtype, rows=None):
    flat = jnp.concatenate([a.astype(dtype).reshape(-1) for a in arrs])
    per = FLAT_TILE * LANES
    total = rows * LANES if rows else flat.shape[0] + (-flat.shape[0]) % per
    flat = jnp.pad(flat, (0, total - flat.shape[0]))
    return flat.reshape(-1, LANES)


def _unpack(buf, shapes):
    flat = buf.reshape(-1)
    out, off = [], 0
    for s in shapes:
        n = math.prod(s)
        out.append(flat[off:off + n].reshape(s))
        off += n
    return out


def _block_diag(w):
    n, a, b = w.shape
    eye = jnp.eye(n, dtype=w.dtype)
    return (w[:, :, None, :] * eye[:, None, :, None]).reshape(n * a, n * b)


def _diag_blocks(m, n):
    a, b = m.shape[0] // n, m.shape[1] // n
    idx = jnp.arange(n)
    return m.reshape(n, a, n, b)[idx, :, idx, :]


BIG = ("w_in", "w_out", "w_up", "w_down", "s5_w_glu")
BIG_COL_SHARDED = {"w_in": True, "w_out": False, "w_up": True, "w_down": False, "s5_w_glu": False}
CONV_SHARDED = ("lru_conv_w", "ffn_conv_w")
SMALL = ("lru_conv_b", "lru_wr", "lru_br", "lru_wi", "lru_bi", "lru_lambda", "s5_a_re", "s5_a_im", "s5_b_re",
         "s5_b_im", "s5_c_re", "s5_c_im", "s5_d", "s5_log_step", "s5_b_glu", "mix_norm_g", "ln1_g", "ln1_b",
         "ffn_conv_b", "ln2_g", "ln2_b")
WEIGHTS = ("w_in", "lru_conv_w", "lru_conv_b", "lru_wr", "lru_br", "lru_wi", "lru_bi", "lru_lambda", "s5_a_re",
           "s5_a_im", "s5_b_re", "s5_b_im", "s5_c_re", "s5_c_im", "s5_d", "s5_log_step", "s5_w_glu", "s5_b_glu",
           "mix_norm_g", "w_out", "ln1_g", "ln1_b", "w_up", "ffn_conv_w", "ffn_conv_b", "w_down", "ln2_g", "ln2_b")


def _assemble(slabs, col_sharded):
    _, L, r, c = slabs.shape
    if col_sharded:
        return slabs.transpose(1, 2, 0, 3).reshape(L, r, 4 * c)
    return slabs.transpose(1, 0, 2, 3).reshape(L, 4 * r, c)


def _s5_prepare(p):
    G = N_S5_GROUPS
    bt_re, bt_im = p["s5_b_re"].transpose(0, 2, 1), p["s5_b_im"].transpose(0, 2, 1)
    ls = p["s5_log_step"].reshape(G, 1)
    ab_re, ab_im, bb_re, bb_im = _s5_params(p["s5_a_re"], p["s5_a_im"], ls, bt_re, bt_im)
    ab = jnp.concatenate([ab_re.reshape(1, S5_LANES), ab_im.reshape(1, S5_LANES)], axis=1)
    bbcat = jnp.concatenate([_block_diag(bb_re), _block_diag(bb_im)], axis=1).astype(bf16)
    ccat = jnp.concatenate([_block_diag(p["s5_c_re"].transpose(0, 2, 1)),
                            -_block_diag(p["s5_c_im"].transpose(0, 2, 1))], axis=0).astype(bf16)
    bbcat_pad = jnp.concatenate([bbcat, jnp.zeros((LRU_WIDTH - S5_WIDTH, 2 * S5_LANES), bf16)], axis=0)
    return dict(bt_re=bt_re, bt_im=bt_im, ls=ls, ab=ab, bbcat=bbcat, bbcat_pad=bbcat_pad, ccat=ccat)


def _layer_fwd(h, p, cos, sin, pending, install):
    sv = {"h": h}
    proj = _mm(h, p["w_in"], "nn", "mm_proj", tn=768)
    sv.update(proj=proj)
    qr, kr = _rope_fwd(proj, cos, sin)
    attn, ltot, gathered = _attn_fwd2(qr, kr, proj, [s for _, _, s in pending], [True] * len(pending))
    install(pending, gathered)
    sv.update(qr=qr, kr=kr, attn=attn, ltot=ltot)
    wr, wi = _block_diag(p["lru_wr"]).astype(bf16), _block_diag(p["lru_wi"]).astype(bf16)
    xc, r, i, log_a, u = _lru_pre(proj, p["lru_conv_w"], p["lru_conv_b"], wr, p["lru_br"], wi, p["lru_bi"],
                                  p["lru_lambda"])
    hl, lru = _lru_scan(log_a, u, proj)
    sv.update(wr=wr, wi=wi, xc=xc, r=r, i=i, log_a=log_a, hl=hl, lru=lru)
    s5 = _s5_prepare(p)
    bu = _mm(proj, s5["bbcat_pad"], "nn", "mm_s5_bu", a_win=(PROJ_S5_U, LRU_WIDTH))
    xs = _s5_scan(bu, s5["ab"])
    yc = _mm(xs, s5["ccat"], "nn", "mm_s5_y")
    ssm, y = _s5_out(yc, proj, p["s5_d"].reshape(-1), p["s5_w_glu"], p["s5_b_glu"])
    sv.update(s5=s5, xs=xs, y=y, ssm=ssm)
    mixed = _mixnorm(attn, lru, ssm, p["mix_norm_g"])
    mix = _mm(mixed, p["w_out"], "nn", "mm_out")
    h1, z1 = _ln_fwd(h, mix, p["ln1_g"], p["ln1_b"], "ln_fwd")
    sv.update(mixed=mixed, z1=z1, h1=h1)
    up = _mm(h1, p["w_up"], "nn", "mm_up", tn=1536)
    act = _ffn_act(up, p["ffn_conv_w"], p["ffn_conv_b"])
    ffn = _mm(act, p["w_down"], "nn", "mm_down")
    h2, z2 = _ln_fwd(h1, ffn, p["ln2_g"], p["ln2_b"], "ln_fwd")
    sv.update(up=up, act=act, z2=z2)
    return h2, sv


def _layer_bwd(dy_a, dy_b, p, sv, cos, sin, relay=None):
    gr = {}
    dz2, acc = _ln_bwd(dy_a, dy_b, sv["z2"], p["ln2_g"], "ln_bwd_top" if dy_a is None else "ln_bwd")
    gr["ln2_g"], gr["ln2_b"] = acc[0], acc[1]
    dact = _mm(dz2, p["w_down"], "nt", "mm_dact")
    gr["w_down"] = _mm(sv["act"], dz2, "tn", "mm_dw_down")
    dupc = _ffn_act_bwd(dact, sv["up"], p["ffn_conv_w"], p["ffn_conv_b"])
    others, others_dst, to_slabs = relay if relay else ((), 0, None)
    send = list(others) + ([gr["w_down"]] if relay else [])
    send_dst = [others_dst] * len(others) + ([EARLY_OWNER] if relay else [])
    dup, acc, from_sibling = _conv_bwd(dupc, sv["up"], p["ffn_conv_w"], "ffn_conv_bwd", col_tile=FFN_COL_TILE,
                                       out_dtype=bf16, send=send, send_dst=send_dst)
    slabs, owners = to_slabs(from_sibling[:-1], gr["w_down"], from_sibling[-1]) if relay else ((), 0)
    gr["ffn_conv_w"], gr["ffn_conv_b"] = acc[0:3], acc[3]
    dh1 = _mm(dup, p["w_up"], "nt", "mm_dh1", tk=2048)
    gr["w_up"] = _mm(sv["h1"], dup, "tn", "mm_dw_up", tn=1536)
    dz1, acc = _ln_bwd(dz2, dh1, sv["z1"], p["ln1_g"], "ln_bwd")
    gr["ln1_g"], gr["ln1_b"] = acc[0], acc[1]
    dmixed = _mm(dz1, p["w_out"], "nt", "mm_dmixed")
    gr["w_out"] = _mm(sv["mixed"], dz1, "tn", "mm_dw_out")
    dattn, dlru, dssm, delta, acc = _mixnorm_bwd(dmixed, sv["attn"], sv["lru"], sv["ssm"], p["mix_norm_g"])
    gr["mix_norm_g"] = acc[0]
    proj = sv["proj"]
    late = [gr["w_up"]] if relay else []
    dqr, dkr, dv, received, late_sibling = _attn_bwd2(sv["qr"], sv["kr"], proj, dattn, sv["ltot"], delta, slabs,
                                                      owners, late, EARLY_OWNER)
    dq, dk = _rope_bwd(dqr, dkr, cos, sin)
    g, dgate = _lru_scan_bwd(dlru, proj, sv["hl"], sv["log_a"])
    dxc, dwr, dwi, acc = _lru_gate_bwd(g, sv["hl"], sv["xc"], sv["r"], sv["i"], sv["log_a"], sv["wr"], sv["wi"],
                                       p["lru_lambda"])
    gr["lru_wr"], gr["lru_wi"] = _diag_blocks(dwr, N_LRU_HEADS), _diag_blocks(dwi, N_LRU_HEADS)
    gr["lru_br"], gr["lru_bi"], gr["lru_lambda"] = acc[0], acc[1], acc[2]
    dxr, acc, _ = _conv_bwd(dxc, proj, p["lru_conv_w"], "lru_conv_bwd", x_col_block=PROJ_LRU_X)
    gr["lru_conv_w"], gr["lru_conv_b"] = acc[0:4], acc[4]
    s5 = sv["s5"]
    G = N_S5_GROUPS
    dy, du_direct, dwglu, acc = _s5_out_bwd(dssm, sv["y"], proj, p["s5_d"].reshape(-1), p["s5_w_glu"],
                                            p["s5_b_glu"])
    gr["s5_w_glu"], gr["s5_b_glu"], gr["s5_d"] = dwglu, acc[0], acc[1].reshape(G, S5_GROUP)
    dxs = _mm(dy, s5["ccat"], "nt", "mm_s5_dx")
    dccat = _mm(sv["xs"], dy, "tn", "mm_s5_dc")
    gr["s5_c_re"] = _diag_blocks(dccat[:S5_LANES], G).transpose(0, 2, 1)
    gr["s5_c_im"] = -_diag_blocks(dccat[S5_LANES:], G).transpose(0, 2, 1)
    gs, dab = _s5_scan_bwd(dxs, sv["xs"], s5["ab"])
    du = _mm(gs, s5["bbcat"], "nt", "mm_s5_du", add=du_direct)
    dbbcat = _mm(proj, gs, "tn", "mm_s5_dbb", a_win=(PROJ_S5_U, LRU_WIDTH))[:S5_WIDTH]
    d_ar, d_ai, d_ls, d_btr, d_bti = _s5_params_bwd(
        p["s5_a_re"], p["s5_a_im"], s5["ls"], s5["bt_re"], s5["bt_im"],
        dab[:, :S5_LANES].reshape(G, S5_STATE), dab[:, S5_LANES:].reshape(G, S5_STATE),
        _diag_blocks(dbbcat[:, :S5_LANES], G), _diag_blocks(dbbcat[:, S5_LANES:], G))
    gr["s5_a_re"], gr["s5_a_im"], gr["s5_log_step"] = d_ar, d_ai, d_ls.reshape(G)
    gr["s5_b_re"], gr["s5_b_im"] = d_btr.transpose(0, 2, 1), d_bti.transpose(0, 2, 1)
    pad = jnp.zeros((du.shape[0], D_IN_PAD - D_IN), f32)
    dproj = jnp.concatenate([dq, dk, dv, dxr, dgate, du, pad], axis=1).astype(bf16)
    gr["w_in"] = _mm(sv["h"], dproj, "tn", "mm_dw_in", tn=768)[:, :D_IN]
    dh = _mm(dproj, p["w_in"], "nt", "mm_dh")
    return (dz1, dh, gr, slabs, received, late_sibling) if relay else (dz1, dh, gr)


def _train_step(d):
    x, target = d["x"][0], d["loss_target"][0]
    S = x.shape[0]
    me = 2 * lax.axis_index("x") + lax.axis_index("y")

    def rows2d(a):
        return a.reshape(a.shape[0] * a.shape[1], a.shape[2])

    params = [{n: d[n][l] for n in SMALL} for l in range(DEPTH)]

    def install(items, gathered):
        for (n, l, mine), g in zip(items, gathered):
            g = lax.dynamic_update_slice_in_dim(g, mine[None], me, axis=0)
            if n in CONV_SHARDED:
                full = _assemble(g.reshape((4,) + d[n].shape), True)
                for k in range(DEPTH):
                    params[k][n] = full[k]
                continue
            full = _assemble(g[:, None], BIG_COL_SHARDED[n])[0]
            if n == "w_in":
                full = jnp.pad(full, ((0, 0), (0, D_IN_PAD - D_IN)))
            params[l][n] = full

    def shard(n, l):
        return (n, l, d[n][l].astype(bf16))

    first = [shard("w_in", 0)] + [(n, None, rows2d(d[n])) for n in CONV_SHARDED]
    install(first, _gather_weights([s for _, _, s in first], [True] + [False] * len(CONV_SHARDED)))
    later = [[shard(n, 0) for n in BIG[1:]] + [shard("w_in", 1)], [shard(n, 1) for n in BIG[1:]]]

    cos, sin = _rope_tables(S)
    h, saved = x, []
    for l in range(DEPTH):
        h, sv = _layer_fwd(h, params[l], cos, sin, later[l], install)
        saved.append(sv)
    dy, loss_acc = _loss_head(h, target)
    def slab(n, g, other, owner):
        aligned = BIG_COL_SHARDED[n] and (g.shape[1] // 4) % LANES == 0
        p = _pair_sum(g, other, "pair_sum_" + n, owner, col_slabs=aligned)
        if BIG_COL_SHARDED[n] and not aligned:
            return p.reshape(p.shape[0], 4, p.shape[1] // 4).transpose(1, 0, 2)
        return p if aligned else p.reshape(4, p.shape[0] // 4, p.shape[1])

    own1 = {n: 1 - EARLY_OWNER for n in BIG}
    own0 = {n: (EARLY_OWNER if n in ("w_down", "w_up") else 1 - EARLY_OWNER) for n in BIG}

    def hidden_slabs(others1, w_down0, w_down0_sibling):
        slabs = [slab(n, grads[1][n], o, own1[n]) for n, o in zip(BIG, others1)]
        slabs.append(slab("w_down", w_down0, w_down0_sibling, own0["w_down"]))
        return slabs, [own1[n] for n in BIG] + [own0["w_down"]]

    da, db, grads = None, dy, [None] * DEPTH
    da, db, grads[1] = _layer_bwd(da, db, params[1], saved[1], cos, sin)
    relay = ([grads[1][n] for n in BIG], 1 - EARLY_OWNER, hidden_slabs)
    da, db, grads[0], hslabs, hrecv, (w_up0_sibling,) = _layer_bwd(da, db, params[0], saved[0], cos, sin, relay)
    out = {"grad_x": _axpy(da, db, "grad_x")[None]}

    small = SMALL + CONV_SHARDED
    sp = _pack([grads[l][n] for n in small for l in range(DEPTH)], f32)
    tail = [n for n in BIG if n != "w_down"]
    rest = [n for n in tail if n != "w_up"]
    others_rest, (sp_sibling,) = _sibling_send([grads[0][n] for n in rest], [own0[n] for n in rest], [sp])
    sibling0 = dict(zip(rest, others_rest), w_up=w_up0_sibling)
    tslabs = [slab(n, grads[0][n], sibling0[n], own0[n]) for n in tail]
    chip_small = _add2(sp, sp_sibling, "pair_sum_small")
    trecv, (recv_small,) = _chip_exchange(tslabs, [chip_small], [own0[n] for n in tail])
    mine0 = {n: _sum_chips(r, s, "sum_chips_" + n, own0[n]) for n, r, s in zip(tail, trecv, tslabs)}
    mine0["w_down"] = _sum_chips(hrecv[-1], hslabs[-1], "sum_chips_w_down", own0["w_down"])
    mine1 = {n: _sum_chips(r, s, "sum_chips_" + n, own1[n]) for n, r, s in zip(BIG, hrecv, hslabs)}
    sent, _ = _sibling_send([mine0[n] for n in BIG] + [mine1[n] for n in BIG],
                            [1 - own0[n] for n in BIG] + [1 - own1[n] for n in BIG])
    theirs0, theirs1 = dict(zip(BIG, sent[:len(BIG)])), dict(zip(BIG, sent[len(BIG):]))
    for n in BIG:
        upd = _adamw_layers((mine0[n], mine1[n]), (theirs0[n], theirs1[n]), (own0[n], own1[n]),
                            d[n], d["m_" + n], d["v_" + n], "adamw_" + n)
        for pre, u in zip(("grad_", "delta_", "new_m_", "new_v_"), upd):
            out[pre + n] = u

    total = _sum_chips(recv_small, chip_small, "sum_chips_small")
    rows = total.shape[0]
    upd = _adamw(total, _pack([d[n] for n in SMALL], f32, rows), _pack([d["m_" + n] for n in SMALL], f32, rows),
                 _pack([d["v_" + n] for n in SMALL], f32, rows), "adamw_small")
    small_shapes = [d[n].shape for n in SMALL]
    for pre, buf in zip(("grad_", "delta_", "new_m_", "new_v_"), (total,) + tuple(upd)):
        for n, a in zip(SMALL, _unpack(buf, small_shapes)):
            out[pre + n] = a
    conv_full = _unpack(total, small_shapes + [(DEPTH,) + grads[0][n].shape for n in CONV_SHARDED])
    for n, g in zip(CONV_SHARDED, conv_full[len(SMALL):]):
        L, K, C = g.shape
        g = lax.dynamic_index_in_dim(g.reshape(L, K, 4, C // 4), me, axis=2, keepdims=False)
        out["grad_" + n] = g
        for pre, u in zip(("delta_", "new_m_", "new_v_"), _adamw_whole(g, d[n], d["m_" + n], d["v_" + n], "adamw_" + n)):
            out[pre + n] = u

    loss_local, _ = lax.optimization_barrier((loss_acc[0, 0], upd[0]))
    out["loss"] = lax.psum(loss_local, ("x", "y", "c"))
    return (out["loss"], out["grad_x"]) + tuple(out[pre + n] for pre in ("grad_", "delta_", "new_m_", "new_v_")
                                                for n in WEIGHTS)


def kernel(
        x, w_in, lru_conv_w, lru_conv_b, lru_wr, lru_br, lru_wi, lru_bi, lru_lambda, s5_a_re, s5_a_im, s5_b_re,
        s5_b_im, s5_c_re, s5_c_im, s5_d, s5_log_step, s5_w_glu, s5_b_glu, mix_norm_g, w_out, ln1_g, ln1_b, w_up,
        ffn_conv_w, ffn_conv_b, w_down, ln2_g, ln2_b, loss_target, m_w_in, m_lru_conv_w, m_lru_conv_b, m_lru_wr,
        m_lru_br, m_lru_wi, m_lru_bi, m_lru_lambda, m_s5_a_re, m_s5_a_im, m_s5_b_re, m_s5_b_im, m_s5_c_re,
        m_s5_c_im, m_s5_d, m_s5_log_step, m_s5_w_glu, m_s5_b_glu, m_mix_norm_g, m_w_out, m_ln1_g, m_ln1_b,
        m_w_up, m_ffn_conv_w, m_ffn_conv_b, m_w_down, m_ln2_g, m_ln2_b, v_w_in, v_lru_conv_w, v_lru_conv_b,
        v_lru_wr, v_lru_br, v_lru_wi, v_lru_bi, v_lru_lambda, v_s5_a_re, v_s5_a_im, v_s5_b_re, v_s5_b_im,
        v_s5_c_re, v_s5_c_im, v_s5_d, v_s5_log_step, v_s5_w_glu, v_s5_b_glu, v_mix_norm_g, v_w_out, v_ln1_g,
        v_ln1_b, v_w_up, v_ffn_conv_w, v_ffn_conv_b, v_w_down, v_ln2_g, v_ln2_b
):
    return _train_step(dict(locals()))
```

```python
import math

import jax
import jax.numpy as jnp
from jax import lax
from jax.experimental import pallas as pl
from jax.experimental.pallas import tpu as pltpu

f32 = jnp.float32
bf16 = jnp.bfloat16
MESH = pl.DeviceIdType.MESH

D_MODEL = 1024
ATTN_WIDTH = 384
LRU_WIDTH = 384
S5_WIDTH = 256
HEAD_DIM = 64
N_LRU_HEADS = 6
N_S5_GROUPS = 16
S5_GROUP = 16
S5_STATE = 64
S5_LANES = N_S5_GROUPS * S5_STATE
D_FF = 3072
D_IN = 2176
LRU_C = 8.0
ROPE_THETA = 10000.0
DILATIONS = (1, 4, 16)
ATTN_BLOCK = 128
DEPTH = 2
ALPHA = (2 * DEPTH) ** 0.25
LN_EPS = 1e-5
RMS_EPS = 1e-6
ADAM_LR, ADAM_B1, ADAM_B2, ADAM_EPS, ADAM_WD, ADAM_STEP = 0.001, 0.9, 0.999, 1e-08, 0.01, 10

SUBLANES = 8
LANES = 128
VMEM_LIMIT = 56 * 1024 * 1024
ROW_TILE = 512
MM_SINGLE_K = 3072
D_IN_PAD = 2304
NEG = -1e30


def _cp(*sem):
    return pltpu.CompilerParams(dimension_semantics=sem if sem else None, vmem_limit_bytes=VMEM_LIMIT)


def _pick(dim, pref, align=LANES):
    if dim <= pref:
        return dim
    t = (pref // align) * align
    while t >= align:
        if dim % t == 0:
            return t
        t -= align
    return dim


def _gelu(x):
    return jax.nn.gelu(x)


def _gelu_grad(x):
    c = math.sqrt(2.0 / math.pi)
    t = jnp.tanh(c * (x + 0.044715 * x * x * x))
    return 0.5 * (1.0 + t) + 0.5 * x * (1.0 - t * t) * c * (1.0 + 3 * 0.044715 * x * x)


def _gelu_pair(x):
    c = math.sqrt(2.0 / math.pi)
    x2 = x * x
    t = jnp.tanh(c * x * (1.0 + 0.044715 * x2))
    return 0.5 * x * (1.0 + t), 0.5 * (1.0 + t) + 0.5 * x * (1.0 - t * t) * c * (1.0 + 3 * 0.044715 * x2)


def _sigmoid(x):
    return jax.nn.sigmoid(x)


def _expm1(x):
    p = 1.0 + x / 9.0
    for n in (8.0, 7.0, 6.0, 5.0, 4.0, 3.0, 2.0):
        p = 1.0 + (x / n) * p
    return jnp.where(jnp.abs(x) < 0.3, x * p, jnp.exp(x) - 1.0)


def _dot(a, b, dims):
    return lax.dot_general(a, b, (dims, ((), ())), preferred_element_type=f32)


NN = ((1,), (0,))
NT = ((1,), (1,))
TN = ((0,), (0,))


def _mm(a, b, mode, name, out_dtype=f32, tm=1024, tn=1024, tk=1024, add=None, a_win=None):
    if mode == "nn":
        (M, K), N = a.shape, b.shape[1]
    elif mode == "nt":
        (M, K), N = a.shape, b.shape[0]
    else:
        (K, M), N = a.shape, b.shape[1]
    win = 0
    if a_win is not None:
        win, w = a_win
        if mode == "tn":
            M, tm = w, w
        else:
            K = w
    single = mode != "tn" and K <= MM_SINGLE_K
    tm, tn = _pick(M, tm), _pick(N, tn)
    tk = K if single else _pick(K, tk)
    nk = K // tk
    dims = {"nn": NN, "nt": NT, "tn": TN}[mode]

    def body(a_ref, b_ref, *rest):
        prod = _dot(a_ref[...].astype(bf16), b_ref[...].astype(bf16), dims)
        if single:
            o_ref = rest[-1]
            o_ref[...] = (prod if add is None else prod + rest[0][...]).astype(o_ref.dtype)
            return
        o_ref, acc = rest[-2:]
        k = pl.program_id(2)

        @pl.when(k == 0)
        def _():
            acc[...] = prod if add is None else prod + rest[0][...]

        @pl.when(k > 0)
        def _():
            acc[...] += prod

        @pl.when(k == nk - 1)
        def _():
            o_ref[...] = acc[...].astype(o_ref.dtype)

    if mode == "tn":
        a_spec = pl.BlockSpec((tk, tm), lambda i, j, k: (k, i + win))
    else:
        a_spec = pl.BlockSpec((tm, tk), lambda i, j, k: (i, k + win))
    if mode == "nt":
        b_spec = pl.BlockSpec((tn, tk), lambda i, j, k: (j, k))
    else:
        b_spec = pl.BlockSpec((tk, tn), lambda i, j, k: (k, j))
    o_spec = pl.BlockSpec((tm, tn), lambda i, j, k: (i, j))
    return pl.pallas_call(
        body, name=name, grid=(M // tm, N // tn, nk),
        in_specs=[a_spec, b_spec] + ([] if add is None else [o_spec]), out_specs=o_spec,
        out_shape=jax.ShapeDtypeStruct((M, N), out_dtype),
        scratch_shapes=[] if single else [pltpu.VMEM((tm, tn), f32)],
        compiler_params=_cp("parallel", "parallel", "arbitrary"),
    )(*((a, b) if add is None else (a, b, add)))


def _shift_down(cur, prev8, k):
    if k == 0:
        return cur
    T, (R, C) = SUBLANES, cur.shape
    rot = pltpu.roll(cur.reshape(R // T, T, C), k, 1)
    before = jnp.concatenate([pltpu.roll(prev8, k, 0)[None], rot[:-1]], axis=0)
    row = lax.broadcasted_iota(jnp.int32, (R // T, T, C), 1)
    return jnp.where(row < k, before, rot).reshape(R, C)


def _shift_up(cur, next8, k):
    if k == 0:
        return cur
    T, (R, C) = SUBLANES, cur.shape
    rot = pltpu.roll(cur.reshape(R // T, T, C), T - k, 1)
    after = jnp.concatenate([rot[1:], pltpu.roll(next8, T - k, 0)[None]], axis=0)
    row = lax.broadcasted_iota(jnp.int32, (R // T, T, C), 1)
    return jnp.where(row < T - k, rot, after).reshape(R, C)


def _prev_halo_spec(rt, cols, ncolblk_fn):
    per = rt // SUBLANES
    return pl.BlockSpec((SUBLANES, cols), lambda *g: (jnp.maximum(g[-1] * per - 1, 0), ncolblk_fn(*g)))


def _ln_fwd(h, branch, g, b, name):
    S, D = h.shape
    rt = _pick(S, ROW_TILE, SUBLANES)

    def body(h_ref, m_ref, g_ref, b_ref, o_ref, z_ref):
        z = ALPHA * h_ref[...] + m_ref[...]
        mu = jnp.mean(z, axis=-1, keepdims=True)
        zc = z - mu
        var = jnp.mean(zc * zc, axis=-1, keepdims=True)
        o_ref[...] = zc * lax.rsqrt(var + LN_EPS) * g_ref[...] + b_ref[...]
        z_ref[...] = z

    row = pl.BlockSpec((rt, D), lambda i: (i, 0))
    vec = pl.BlockSpec((1, D), lambda i: (0, 0))
    return pl.pallas_call(
        body, name=name, grid=(S // rt,), in_specs=[row, row, vec, vec], out_specs=[row, row],
        out_shape=[jax.ShapeDtypeStruct((S, D), f32)] * 2, compiler_params=_cp("parallel"),
    )(h, branch, g.reshape(1, D), b.reshape(1, D))


def _ln_bwd(dy_a, dy_b, z, g, name):
    S, D = z.shape
    rt = _pick(S, ROW_TILE, SUBLANES)
    two = dy_a is not None

    def body(*refs):
        if two:
            a_ref, b_ref, z_ref, g_ref, dz_ref, acc_ref = refs
            dy = ALPHA * a_ref[...] + b_ref[...]
        else:
            b_ref, z_ref, g_ref, dz_ref, acc_ref = refs
            dy = b_ref[...]
        z = z_ref[...]
        mu = jnp.mean(z, axis=-1, keepdims=True)
        zc = z - mu
        var = jnp.mean(zc * zc, axis=-1, keepdims=True)
        rstd = lax.rsqrt(var + LN_EPS)
        xhat = zc * rstd
        dxh = dy * g_ref[...]
        m1 = jnp.mean(dxh, axis=-1, keepdims=True)
        m2 = jnp.mean(dxh * xhat, axis=-1, keepdims=True)
        dz_ref[...] = rstd * (dxh - m1 - xhat * m2)

        @pl.when(pl.program_id(0) == 0)
        def _():
            acc_ref[...] = jnp.zeros_like(acc_ref)

        acc_ref[0:1, :] += jnp.sum(dy * xhat, axis=0, keepdims=True)
        acc_ref[1:2, :] += jnp.sum(dy, axis=0, keepdims=True)

    row = pl.BlockSpec((rt, D), lambda i: (i, 0))
    vec = pl.BlockSpec((1, D), lambda i: (0, 0))
    acc = pl.BlockSpec((SUBLANES, D), lambda i: (0, 0))
    ins = ([dy_a] if two else []) + [dy_b, z, g.reshape(1, D)]
    return pl.pallas_call(
        body, name=name, grid=(S // rt,), in_specs=[row] * (len(ins) - 1) + [vec], out_specs=[row, acc],
        out_shape=[jax.ShapeDtypeStruct((S, D), f32), jax.ShapeDtypeStruct((SUBLANES, D), f32)],
        compiler_params=_cp("arbitrary"),
    )(*ins)


def _loss_head(y, target):
    S, D = y.shape
    rt = _pick(S, ROW_TILE, SUBLANES)

    def body(y_ref, t_ref, dy_ref, acc_ref):
        e = y_ref[...] - t_ref[...]
        dy_ref[...] = e * (1.0 / D)

        @pl.when(pl.program_id(0) == 0)
        def _():
            acc_ref[...] = jnp.zeros_like(acc_ref)

        part = jnp.sum(jnp.mean(e * e, axis=-1, keepdims=True), axis=0, keepdims=True)
        acc_ref[...] += 0.5 * part

    row = pl.BlockSpec((rt, D), lambda i: (i, 0))
    return pl.pallas_call(
        body, name="loss_head", grid=(S // rt,), in_specs=[row, row],
        out_specs=[row, pl.BlockSpec((1, 1), lambda i: (0, 0))],
        out_shape=[jax.ShapeDtypeStruct((S, D), f32), jax.ShapeDtypeStruct((1, 1), f32)],
        compiler_params=_cp("arbitrary"),
    )(y, target)


def _axpy(a, b, name):
    S, D = a.shape
    rt = _pick(S, ROW_TILE, SUBLANES)

    def body(a_ref, b_ref, o_ref):
        o_ref[...] = ALPHA * a_ref[...] + b_ref[...]

    row = pl.BlockSpec((rt, D), lambda i: (i, 0))
    return pl.pallas_call(
        body, name=name, grid=(S // rt,), in_specs=[row, row], out_specs=row,
        out_shape=jax.ShapeDtypeStruct((S, D), f32), compiler_params=_cp("parallel"),
    )(a, b)


def _rope_tables(S):
    rt = _pick(S, ROW_TILE, SUBLANES)

    def body(c_ref, s_ref):
        pos = (pl.program_id(0) * rt + lax.broadcasted_iota(jnp.int32, (rt, LANES), 0)).astype(f32)
        lane = lax.broadcasted_iota(jnp.int32, (rt, LANES), 1)
        j = (lane % (HEAD_DIM // 2)).astype(f32)
        inv = jnp.exp((-j * 2.0 / HEAD_DIM) * math.log(ROPE_THETA))
        ang = pos * inv
        c = jnp.cos(ang)
        s = jnp.where(lane % HEAD_DIM < HEAD_DIM // 2, -jnp.sin(ang), jnp.sin(ang))
        c_ref[...] = jnp.concatenate([c, c, c], axis=1)
        s_ref[...] = jnp.concatenate([s, s, s], axis=1)

    row = pl.BlockSpec((rt, ATTN_WIDTH), lambda i: (i, 0))
    return pl.pallas_call(
        body, name="rope_tables", grid=(S // rt,), in_specs=[], out_specs=[row, row],
        out_shape=[jax.ShapeDtypeStruct((S, ATTN_WIDTH), f32)] * 2, compiler_params=_cp("parallel"),
    )()


def _swap_halves(x):
    lane = lax.broadcasted_iota(jnp.int32, x.shape, 1)
    half = HEAD_DIM // 2
    return jnp.where(lane % HEAD_DIM < half, pltpu.roll(x, x.shape[1] - half, 1), pltpu.roll(x, half, 1))


def _rope_fwd(proj, cos, sin):
    S, W = proj.shape[0], ATTN_WIDTH
    rt = _pick(S, ROW_TILE, SUBLANES)

    def body(q_ref, k_ref, c_ref, s_ref, qo_ref, ko_ref):
        c, s = c_ref[...], s_ref[...]
        qo_ref[...] = q_ref[...] * c + _swap_halves(q_ref[...]) * s
        ko_ref[...] = k_ref[...] * c + _swap_halves(k_ref[...]) * s

    row = pl.BlockSpec((rt, W), lambda i: (i, 0))
    return pl.pallas_call(
        body, name="rope_fwd", grid=(S // rt,), in_specs=[row, pl.BlockSpec((rt, W), lambda i: (i, 1)), row, row],
        out_specs=[row, row], out_shape=[jax.ShapeDtypeStruct((S, W), f32)] * 2, compiler_params=_cp("parallel"),
    )(proj, proj, cos, sin)


def _rope_bwd(dq, dk, cos, sin):
    S, W = dq.shape
    rt = _pick(S, ROW_TILE, SUBLANES)

    def body(q_ref, k_ref, c_ref, s_ref, qo_ref, ko_ref):
        c, s = c_ref[...], s_ref[...]
        qo_ref[...] = q_ref[...] * c + _swap_halves(q_ref[...] * s)
        ko_ref[...] = k_ref[...] * c + _swap_halves(k_ref[...] * s)

    row = pl.BlockSpec((rt, W), lambda i: (i, 0))
    return pl.pallas_call(
        body, name="rope_bwd", grid=(S // rt,), in_specs=[row] * 4, out_specs=[row] * 2,
        out_shape=[jax.ShapeDtypeStruct((S, W), f32)] * 2, compiler_params=_cp("parallel"),
    )(dq, dk, cos, sin)


def _rows(ref, start, d):
    if d == 1:
        return ref[pl.ds(pl.multiple_of(start, ATTN_BLOCK), ATTN_BLOCK), :]
    return ref[pl.ds(start, ATTN_BLOCK, stride=d), :]


def _set_rows(ref, start, d, val):
    if d == 1:
        ref[pl.ds(pl.multiple_of(start, ATTN_BLOCK), ATTN_BLOCK), :] = val
    else:
        ref[pl.ds(start, ATTN_BLOCK, stride=d), :] = val


def _pair_spec(S, first_block):
    return pl.BlockSpec((S, LANES), lambda p: (0, p + first_block))


def _attn_fwd2(qr, kr, proj, shards=(), split=()):
    S = qr.shape[0]
    B = ATTN_BLOCK
    nb = S // B
    scale = HEAD_DIM ** -0.5

    gather = _Gather(shards, split)
    nt = gather.nt

    def body(*refs):
        q_ref, k_ref, v_ref = refs[:3]
        g_ins = refs[3:3 + nt]
        o_ref, l_ref = refs[3 + nt:5 + nt]
        g_outs = refs[5 + nt:5 + 2 * nt]
        m_s, l_s = refs[5 + 2 * nt:7 + 2 * nt]
        g_sems = refs[7 + 2 * nt:]
        if nt:
            @pl.when(pl.program_id(0) == 0)
            def _():
                gather.start(g_ins, g_outs, g_sems)

        qi = lax.broadcasted_iota(jnp.int32, (B, 2 * B), 0)
        ki = lax.broadcasted_iota(jnp.int32, (B, 2 * B), 1)
        dist = qi + B - ki
        band = (dist >= 0) & (dist <= B)
        for bi, d in enumerate(DILATIONS):
            bpc = nb // d

            def blk(b, carry, bi=bi, d=d, bpc=bpc):
                c, n = b // bpc, b % bpc
                start = c + d * B * n
                pstart = c + d * B * jnp.maximum(n - 1, 0)
                valid = band & ((ki >= B) | (n > 0))
                q = _rows(q_ref, start, d).astype(bf16)
                kcat = jnp.concatenate([_rows(k_ref, pstart, d), _rows(k_ref, start, d)], axis=0).astype(bf16)
                vcat = jnp.concatenate([_rows(v_ref, pstart, d), _rows(v_ref, start, d)], axis=0).astype(bf16)
                if bi > 0:
                    m_old, l_old, a_old = _rows(m_s, start, d), _rows(l_s, start, d), _rows(o_ref, start, d)
                ms, ls, accs = [], [], []
                for h in range(2):
                    sl = slice(h * HEAD_DIM, (h + 1) * HEAD_DIM)
                    c0 = h * HEAD_DIM
                    s = jnp.where(valid, _dot(q[:, sl], kcat[:, sl], NT) * scale, NEG)
                    m = jnp.max(s, axis=1, keepdims=True)
                    if bi > 0:
                        mo = m_old[:, c0:c0 + 1]
                        m = jnp.maximum(m, mo)
                        alpha = jnp.exp(mo - m)
                    p = jnp.exp(s - m)
                    l = jnp.sum(p, axis=1, keepdims=True)
                    acc = _dot(p.astype(bf16), vcat[:, sl], NN)
                    if bi > 0:
                        l = l + alpha * l_old[:, c0:c0 + 1]
                        acc = acc + alpha * a_old[:, sl]
                    ms.append(jnp.broadcast_to(m, (B, HEAD_DIM)))
                    ls.append(jnp.broadcast_to(l, (B, HEAD_DIM)))
                    accs.append(acc)
                _set_rows(m_s, start, d, jnp.concatenate(ms, axis=1))
                _set_rows(l_s, start, d, jnp.concatenate(ls, axis=1))
                _set_rows(o_ref, start, d, jnp.concatenate(accs, axis=1))
                return carry

            lax.fori_loop(0, nb, blk, 0, unroll=4)

        def fin(t, carry):
            rows = pl.ds(pl.multiple_of(t * B, B), B)
            l = l_s[rows, :]
            o_ref[rows, :] = o_ref[rows, :] / l
            l_ref[rows, :] = m_s[rows, :] + jnp.log(l)
            return carry

        lax.fori_loop(0, nb, fin, 0)
        if nt:
            @pl.when(pl.program_id(0) == pl.num_programs(0) - 1)
            def _():
                gather.finish(g_ins, g_outs, g_sems)

    pair = _pair_spec(S, 0)
    res = pl.pallas_call(
        body, name="attn_fwd_gather" if nt else "attn_fwd", grid=(3,),
        in_specs=[pair, pair, _pair_spec(S, 2 * ATTN_WIDTH // LANES)] + gather.in_specs,
        out_specs=[pair, pair] + gather.out_specs,
        out_shape=[jax.ShapeDtypeStruct((S, ATTN_WIDTH), f32)] * 2 + gather.out_shape,
        scratch_shapes=[pltpu.VMEM((S, LANES), f32)] * 2 + gather.scratch,
        compiler_params=_cp("arbitrary"),
    )(qr, kr, proj, *shards)
    return res[0], res[1], list(res[2:])


def _attn_bwd2(qr, kr, proj, dattn, ltot, delta, slabs=(), only_c=0, send=(), send_dst=0):
    S = qr.shape[0]
    B = ATTN_BLOCK
    nb = S // B
    scale = HEAD_DIM ** -0.5
    ex = _ChipExchange(slabs, (), only_c)
    snd = _SiblingSend(send, send_dst)
    n, ns = ex.n, snd.n
    hosted = n + ns

    def body(*refs):
        q_ref, k_ref, v_ref, do_ref, l_ref, d_ref = refs[:6]
        x_ins, s_ins = refs[6:6 + n], refs[6 + n:6 + hosted]
        dq_ref, dk_ref, dv_ref = refs[6 + hosted:9 + hosted]
        x_outs, s_outs = refs[9 + hosted:9 + hosted + n], refs[9 + hosted + n:9 + 2 * hosted]
        sems = refs[9 + 2 * hosted:]
        x_sems, s_sems = sems[:len(ex.scratch)], sems[len(ex.scratch):]
        if hosted:
            @pl.when(pl.program_id(0) == 0)
            def _():
                if n:
                    ex.start(x_ins, x_outs, x_sems)
                if ns:
                    snd.start(s_ins, s_outs, s_sems)

        qi = lax.broadcasted_iota(jnp.int32, (B, 2 * B), 0)
        ki = lax.broadcasted_iota(jnp.int32, (B, 2 * B), 1)
        dist1 = qi + B - ki
        band1 = (dist1 >= 0) & (dist1 <= B)
        ri = lax.broadcasted_iota(jnp.int32, (2 * B, B), 0)
        ci = lax.broadcasted_iota(jnp.int32, (2 * B, B), 1)
        dist2 = ri - ci
        band2 = (dist2 >= 0) & (dist2 <= B)
        for bi, d in enumerate(DILATIONS):
            bpc = nb // d

            def blk(b, carry, bi=bi, d=d, bpc=bpc):
                c, n = b // bpc, b % bpc
                start = c + d * B * n
                pstart = c + d * B * jnp.maximum(n - 1, 0)
                nstart = c + d * B * jnp.minimum(n + 1, bpc - 1)
                valid1 = band1 & ((ki >= B) | (n > 0))
                valid2 = band2 & ((ri < B) | (n + 1 < bpc))
                q_c, q_n = _rows(q_ref, start, d), _rows(q_ref, nstart, d)
                k_p, k_c = _rows(k_ref, pstart, d), _rows(k_ref, start, d)
                v_p, v_c = _rows(v_ref, pstart, d), _rows(v_ref, start, d)
                do_c, do_n = _rows(do_ref, start, d), _rows(do_ref, nstart, d)
                l_c, l_n = _rows(l_ref, start, d), _rows(l_ref, nstart, d)
                d_c, d_n = _rows(d_ref, start, d), _rows(d_ref, nstart, d)
                qc = q_c.astype(bf16)
                qcat = jnp.concatenate([q_c, q_n], axis=0).astype(bf16)
                kc = k_c.astype(bf16)
                kcat = jnp.concatenate([k_p, k_c], axis=0).astype(bf16)
                vc = v_c.astype(bf16)
                vcat = jnp.concatenate([v_p, v_c], axis=0).astype(bf16)
                doc = do_c.astype(bf16)
                docat = jnp.concatenate([do_c, do_n], axis=0).astype(bf16)
                lcat = jnp.concatenate([l_c, l_n], axis=0)
                dcat = jnp.concatenate([d_c, d_n], axis=0)
                dqs, dks, dvs = [], [], []
                for h in range(2):
                    sl = slice(h * HEAD_DIM, (h + 1) * HEAD_DIM)
                    c0 = h * HEAD_DIM
                    s1 = _dot(qc[:, sl], kcat[:, sl], NT) * scale
                    p1 = jnp.where(valid1, jnp.exp(s1 - l_c[:, c0:c0 + 1]), 0.0)
                    dp1 = _dot(doc[:, sl], vcat[:, sl], NT)
                    ds1 = p1 * (dp1 - d_c[:, c0:c0 + 1]) * scale
                    dqs.append(_dot(ds1.astype(bf16), kcat[:, sl], NN))
                    s2 = _dot(qcat[:, sl], kc[:, sl], NT) * scale
                    p2 = jnp.where(valid2, jnp.exp(s2 - lcat[:, c0:c0 + 1]), 0.0)
                    dvs.append(_dot(p2.astype(bf16), docat[:, sl], TN))
                    dp2 = _dot(docat[:, sl], vc[:, sl], NT)
                    ds2 = p2 * (dp2 - dcat[:, c0:c0 + 1]) * scale
                    dks.append(_dot(ds2.astype(bf16), qcat[:, sl], TN))
                for ref, parts in ((dq_ref, dqs), (dk_ref, dks), (dv_ref, dvs)):
                    new = jnp.concatenate(parts, axis=1)
                    if bi > 0:
                        new = new + _rows(ref, start, d)
                    _set_rows(ref, start, d, new)
                return carry

            lax.fori_loop(0, nb, blk, 0, unroll=4)

        if hosted:
            @pl.when(pl.program_id(0) == pl.num_programs(0) - 1)
            def _():
                if ns:
                    snd.finish(s_ins, s_outs, s_sems)
                if n:
                    ex.finish(x_ins, x_outs, x_sems)

    pair = _pair_spec(S, 0)
    res = pl.pallas_call(
        body, name="attn_bwd_exchange" if hosted else "attn_bwd", grid=(3,),
        in_specs=[pair, pair, _pair_spec(S, 2 * ATTN_WIDTH // LANES), pair, pair, pair] + ex.in_specs + snd.in_specs,
        out_specs=[pair] * 3 + ex.out_specs + snd.out_specs,
        out_shape=[jax.ShapeDtypeStruct((S, ATTN_WIDTH), f32)] * 3 + ex.out_shape + snd.out_shape,
        scratch_shapes=ex.scratch + snd.scratch, compiler_params=_cp("arbitrary"),
    )(qr, kr, proj, dattn, ltot, delta, *slabs, *send)
    return res[0], res[1], res[2], list(res[3:3 + n]), list(res[3 + n:])


def _softplus_neg(lam):
    return jnp.maximum(-lam, 0.0) + jnp.log1p(jnp.exp(-jnp.abs(lam)))


PROJ_LRU_X, PROJ_LRU_GATE, PROJ_S5_U = 3, 4, 5
EARLY_OWNER = 1


def _lru_pre(proj, conv_w, conv_b, wr, br, wi, bi, lam):
    S, W = proj.shape[0], LRU_WIDTH
    rt = _pick(S, ROW_TILE, SUBLANES)
    K = conv_w.shape[0]

    def body(x_ref, xp_ref, cw_ref, cb_ref, wr_ref, br_ref, wi_ref, bi_ref, lam_ref,
             xc_ref, r_ref, i_ref, la_ref, u_ref):
        prev = jnp.where(pl.program_id(0) == 0, 0.0, xp_ref[...])
        x = x_ref[...]
        xc = cb_ref[...] + cw_ref[K - 1:K, :] * x
        for k in range(K - 1):
            xc = xc + cw_ref[k:k + 1, :] * _shift_down(x, prev, K - 1 - k)
        xb = xc.astype(bf16)
        r = _sigmoid(_dot(xb, wr_ref[...], NN) + br_ref[...])
        i = _sigmoid(_dot(xb, wi_ref[...], NN) + bi_ref[...])
        log_a = -LRU_C * r * _softplus_neg(lam_ref[...])
        u = jnp.sqrt(-_expm1(2.0 * log_a)) * (i * xc)
        xc_ref[...], r_ref[...], i_ref[...], la_ref[...], u_ref[...] = xc, r, i, log_a, u

    row = pl.BlockSpec((rt, W), lambda i: (i, 0))
    xrow = pl.BlockSpec((rt, W), lambda i: (i, PROJ_LRU_X))
    halo = _prev_halo_spec(rt, W, lambda i: PROJ_LRU_X)
    vec = pl.BlockSpec((1, W), lambda i: (0, 0))
    return pl.pallas_call(
        body, name="lru_pre", grid=(S // rt,),
        in_specs=[xrow, halo, pl.BlockSpec((K, W), lambda i: (0, 0)), vec,
                  pl.BlockSpec((W, W), lambda i: (0, 0)), vec, pl.BlockSpec((W, W), lambda i: (0, 0)), vec, vec],
        out_specs=[row] * 5, out_shape=[jax.ShapeDtypeStruct((S, W), f32)] * 5, compiler_params=_cp("parallel"),
    )(proj, proj, conv_w, conv_b.reshape(1, W), wr, br.reshape(1, W), wi, bi.reshape(1, W), lam.reshape(1, W))


def _tile_rows(shape):
    return lax.broadcasted_iota(jnp.int32, shape, 0)


def _lru_scan(log_a, u, proj):
    S, W = u.shape
    rt = _pick(S, ROW_TILE, SUBLANES)
    T = SUBLANES

    def body(la_ref, u_ref, g_ref, h_ref, o_ref, carry):
        @pl.when(pl.program_id(0) == 0)
        def _():
            carry[...] = jnp.zeros_like(carry)

        row = _tile_rows((T, W))

        def step(t, hp):
            r0 = pl.multiple_of(t * T, T)
            a = jnp.exp(la_ref[pl.ds(r0, T), :])
            x = u_ref[pl.ds(r0, T), :]
            for k in (1, 2, 4):
                x = x + a * jnp.where(row >= k, pltpu.roll(x, k, 0), 0.0)
                a = a * jnp.where(row >= k, pltpu.roll(a, k, 0), 1.0)
            h = x + a * hp
            h_ref[pl.ds(r0, T), :] = h
            o_ref[pl.ds(r0, T), :] = h * _gelu(g_ref[pl.ds(r0, T), :])
            return h[T - 1:T, :]

        carry[0:1, :] = lax.fori_loop(0, rt // T, step, carry[0:1, :])

    row = pl.BlockSpec((rt, W), lambda i: (i, 0))
    grow = pl.BlockSpec((rt, W), lambda i: (i, PROJ_LRU_GATE))
    return pl.pallas_call(
        body, name="lru_scan", grid=(S // rt,), in_specs=[row, row, grow], out_specs=[row] * 2,
        out_shape=[jax.ShapeDtypeStruct((S, W), f32)] * 2, scratch_shapes=[pltpu.VMEM((T, W), f32)],
        compiler_params=_cp("arbitrary"),
    )(log_a, u, proj)


def _lru_scan_bwd(dlru, proj, h, log_a):
    S, W = h.shape
    rt = _pick(S, ROW_TILE, SUBLANES)
    T = SUBLANES
    nblk = S // rt

    def body(d_ref, g_ref, h_ref, la_ref, go_ref, dg_ref, carry):
        @pl.when(pl.program_id(0) == 0)
        def _():
            carry[...] = jnp.zeros_like(carry)

        row = _tile_rows((T, W))

        def step(j, c):
            gn, an = c
            t = rt // T - 1 - j
            r0 = pl.multiple_of(t * T, T)
            d = d_ref[pl.ds(r0, T), :]
            gate = g_ref[pl.ds(r0, T), :]
            a = jnp.exp(la_ref[pl.ds(r0, T), :])
            dg_ref[pl.ds(r0, T), :] = d * h_ref[pl.ds(r0, T), :] * _gelu_grad(gate)
            x = d * _gelu(gate)
            b = jnp.where(row < T - 1, pltpu.roll(a, T - 1, 0), an)
            for k in (1, 2, 4):
                x = x + b * jnp.where(row < T - k, pltpu.roll(x, T - k, 0), 0.0)
                b = b * jnp.where(row < T - k, pltpu.roll(b, T - k, 0), 1.0)
            g = x + b * gn
            go_ref[pl.ds(r0, T), :] = g
            return g[0:1, :], a[0:1, :]

        gn, an = lax.fori_loop(0, rt // T, step, (carry[0:1, :], carry[1:2, :]))
        carry[0:1, :] = gn
        carry[1:2, :] = an

    row = pl.BlockSpec((rt, W), lambda i: (nblk - 1 - i, 0))
    grow = pl.BlockSpec((rt, W), lambda i: (nblk - 1 - i, PROJ_LRU_GATE))
    return pl.pallas_call(
        body, name="lru_scan_bwd", grid=(nblk,), in_specs=[row, grow, row, row], out_specs=[row] * 2,
        out_shape=[jax.ShapeDtypeStruct((S, W), f32)] * 2, scratch_shapes=[pltpu.VMEM((T, W), f32)],
        compiler_params=_cp("arbitrary"),
    )(dlru, proj, h, log_a)


def _lru_gate_bwd(g, h, xc, r, i, log_a, wr, wi, lam):
    S, W = g.shape
    rt = _pick(S, ROW_TILE, SUBLANES)

    def body(g_ref, h_ref, hp_ref, xc_ref, r_ref, i_ref, la_ref, wr_ref, wi_ref, lam_ref,
             dxc_ref, dwr_ref, dwi_ref, acc_ref):
        @pl.when(pl.program_id(0) == 0)
        def _():
            dwr_ref[...] = jnp.zeros_like(dwr_ref)
            dwi_ref[...] = jnp.zeros_like(dwi_ref)
            acc_ref[...] = jnp.zeros_like(acc_ref)

        prev = jnp.where(pl.program_id(0) == 0, 0.0, hp_ref[...])
        gg, xc, r, i, log_a, lam = g_ref[...], xc_ref[...], r_ref[...], i_ref[...], la_ref[...], lam_ref[...]
        hm1 = _shift_down(h_ref[...], prev, 1)
        a = jnp.exp(log_a)
        s = jnp.sqrt(-_expm1(2.0 * log_a))
        da = gg * hm1
        di = gg * s * xc
        dxc = gg * s * i
        ds = gg * i * xc
        dlog_a = da * a - ds * (a * a / s)
        sp = _softplus_neg(lam)
        dr = dlog_a * (-LRU_C * sp)
        dsp = jnp.sum(dlog_a * (-LRU_C * r), axis=0, keepdims=True)
        dpr = dr * r * (1.0 - r)
        dpi = di * i * (1.0 - i)
        dprb, dpib, xb = dpr.astype(bf16), dpi.astype(bf16), xc.astype(bf16)
        dxc_ref[...] = dxc + _dot(dprb, wr_ref[...], NT) + _dot(dpib, wi_ref[...], NT)
        dwr_ref[...] += _dot(xb, dprb, TN)
        dwi_ref[...] += _dot(xb, dpib, TN)
        acc_ref[0:1, :] += jnp.sum(dpr, axis=0, keepdims=True)
        acc_ref[1:2, :] += jnp.sum(dpi, axis=0, keepdims=True)
        acc_ref[2:3, :] += dsp * (-_sigmoid(-lam))

    row = pl.BlockSpec((rt, W), lambda i: (i, 0))
    halo = _prev_halo_spec(rt, W, lambda i: 0)
    vec = pl.BlockSpec((1, W), lambda i: (0, 0))
    mat = pl.BlockSpec((W, W), lambda i: (0, 0))
    acc = pl.BlockSpec((SUBLANES, W), lambda i: (0, 0))
    return pl.pallas_call(
        body, name="lru_gate_bwd", grid=(S // rt,),
        in_specs=[row, row, halo, row, row, row, row, mat, mat, vec], out_specs=[row, mat, mat, acc],
        out_shape=[jax.ShapeDtypeStruct((S, W), f32), jax.ShapeDtypeStruct((W, W), f32),
                   jax.ShapeDtypeStruct((W, W), f32), jax.ShapeDtypeStruct((SUBLANES, W), f32)],
        compiler_params=_cp("arbitrary"),
    )(g, h, h, xc, r, i, log_a, wr, wi, lam.reshape(1, W))


def _conv_bwd(dy, x, conv_w, name, col_tile=None, out_dtype=f32, x_col_block=0, send=(), send_dst=0):
    if dy.ndim == 2:
        dy = dy[None]
    H, S, Ch = dy.shape
    C = H * Ch
    K = conv_w.shape[0]
    ct = Ch if col_tile is None else col_tile
    nct = Ch // ct
    rt = _pick(S, ROW_TILE, SUBLANES)
    nrt = S // rt
    snd = _SiblingSend(send, send_dst)
    n = snd.n

    def body(*refs):
        dy_ref, dyn_ref, x_ref, w_ref = refs[:4]
        s_ins = refs[4:4 + n]
        dx_ref, acc_ref = refs[4 + n:6 + n]
        s_outs, s_sems = refs[6 + n:6 + 2 * n], refs[6 + 2 * n:]
        i = pl.program_id(2)
        if n:
            @pl.when((pl.program_id(0) == 0) & (pl.program_id(1) == 0) & (i == 0))
            def _():
                snd.start(s_ins, s_outs, s_sems)

        @pl.when(i == 0)
        def _():
            acc_ref[...] = jnp.zeros_like(acc_ref)

        nxt = jnp.where(i == nrt - 1, 0.0, dyn_ref[...])
        dy, x = dy_ref[...], x_ref[...]
        ahead = [dy] + [_shift_up(dy, nxt, j) for j in range(1, K)]
        dx = w_ref[K - 1:K, :] * dy
        for k in range(K - 1):
            dx = dx + w_ref[k:k + 1, :] * ahead[K - 1 - k]
        dx_ref[...] = dx.astype(dx_ref.dtype)
        for k in range(K):
            acc_ref[k:k + 1, :] += jnp.sum(ahead[K - 1 - k] * x, axis=0, keepdims=True)
        acc_ref[K:K + 1, :] += jnp.sum(dy, axis=0, keepdims=True)
        if n:
            @pl.when((pl.program_id(0) == H - 1) & (pl.program_id(1) == nct - 1) & (i == nrt - 1))
            def _():
                snd.finish(s_ins, s_outs, s_sems)

    per, last = rt // SUBLANES, S // SUBLANES - 1
    dy_row = pl.BlockSpec((None, rt, ct), lambda h, j, i: (h, i, j))
    dy_next = pl.BlockSpec((None, SUBLANES, ct), lambda h, j, i: (h, jnp.minimum((i + 1) * per, last), j))
    row = pl.BlockSpec((rt, ct), lambda h, j, i: (i, h * nct + j))
    xrow = pl.BlockSpec((rt, ct), lambda h, j, i: (i, h * nct + j + x_col_block))
    res = pl.pallas_call(
        body, name=name, grid=(H, nct, nrt),
        in_specs=[dy_row, dy_next, xrow, pl.BlockSpec((K, ct), lambda h, j, i: (0, h * nct + j))] + snd.in_specs,
        out_specs=[row, pl.BlockSpec((SUBLANES, ct), lambda h, j, i: (0, h * nct + j))] + snd.out_specs,
        out_shape=[jax.ShapeDtypeStruct((S, C), out_dtype), jax.ShapeDtypeStruct((SUBLANES, C), f32)] + snd.out_shape,
        scratch_shapes=snd.scratch,
        compiler_params=_cp(*(("arbitrary",) * 3 if n else ("parallel", "parallel", "arbitrary"))),
    )(dy, dy, x, conv_w, *send)
    return res[0], res[1], list(res[2:])


def _s5_param_fn(a_re, a_im, ls, bt_re, bt_im):
    step = jnp.exp(ls)
    dt_re, dt_im = step * a_re, step * a_im
    mag = jnp.exp(dt_re)
    ab_re, ab_im = mag * jnp.cos(dt_im), mag * jnp.sin(dt_im)
    z_re, z_im = ab_re - 1.0, ab_im
    den = a_re * a_re + a_im * a_im
    f_re = (z_re * a_re + z_im * a_im) / den
    f_im = (z_im * a_re - z_re * a_im) / den
    bb_re = f_re[:, None, :] * bt_re - f_im[:, None, :] * bt_im
    bb_im = f_re[:, None, :] * bt_im + f_im[:, None, :] * bt_re
    return ab_re, ab_im, bb_re, bb_im


def _s5_params(a_re, a_im, ls, bt_re, bt_im):
    def body(ar, ai, l, br, bi, o_ar, o_ai, o_br, o_bi):
        o_ar[...], o_ai[...], o_br[...], o_bi[...] = _s5_param_fn(ar[...], ai[...], l[...], br[...], bi[...])

    return pl.pallas_call(
        body, name="s5_params",
        out_shape=[jax.ShapeDtypeStruct(a_re.shape, f32)] * 2 + [jax.ShapeDtypeStruct(bt_re.shape, f32)] * 2,
        compiler_params=_cp(),
    )(a_re, a_im, ls, bt_re, bt_im)


def _s5_params_bwd(a_re, a_im, ls, bt_re, bt_im, d_ar, d_ai, d_br, d_bi):
    def body(ar, ai, l, br, bi, c_ar, c_ai, c_br, c_bi, g_ar, g_ai, g_l, g_br, g_bi):
        _, vjp = jax.vjp(_s5_param_fn, ar[...], ai[...], l[...], br[...], bi[...])
        g_ar[...], g_ai[...], g_l[...], g_br[...], g_bi[...] = vjp((c_ar[...], c_ai[...], c_br[...], c_bi[...]))

    return pl.pallas_call(
        body, name="s5_params_bwd",
        out_shape=[jax.ShapeDtypeStruct(a_re.shape, f32)] * 2 + [jax.ShapeDtypeStruct(ls.shape, f32)]
        + [jax.ShapeDtypeStruct(bt_re.shape, f32)] * 2,
        compiler_params=_cp(),
    )(a_re, a_im, ls, bt_re, bt_im, d_ar, d_ai, d_br, d_bi)


S5_CHUNK = 256


def _s5_power_tables(ab_ref, p_ref, w_ref, conj):
    T, L = SUBLANES, S5_LANES
    are = ab_ref[0:1, 0:L]
    aim = ab_ref[0:1, L:2 * L]
    if conj:
        aim = -aim
    pre, pim = are, aim
    for n in range(3):
        p_ref[n:n + 1, 0:L] = pre
        p_ref[n:n + 1, L:2 * L] = pim
        pre, pim = pre * pre - pim * pim, 2.0 * pre * pim
    row = _tile_rows((T, L))
    wre = jnp.zeros((T, L), f32)
    wim = jnp.zeros((T, L), f32)
    pre, pim = are, aim
    for n in range(T):
        tgt = (T - 1 - n) if conj else n
        wre = jnp.where(row == tgt, pre, wre)
        wim = jnp.where(row == tgt, pim, wim)
        pre, pim = pre * are - pim * aim, pre * aim + pim * are
    w_ref[:, 0:L] = wre
    w_ref[:, L:2 * L] = wim


def _s5_scan(bu, ab):
    S, L2 = bu.shape
    L = L2 // 2
    rt = _pick(S, 256, SUBLANES)
    T = SUBLANES
    CH = S5_CHUNK

    def body(bu_ref, ab_ref, x_ref, p_ref, w_ref, carry):
        @pl.when(pl.program_id(0) == 0)
        def _():
            carry[...] = jnp.zeros_like(carry)
            _s5_power_tables(ab_ref, p_ref, w_ref, conj=False)

        row = _tile_rows((T, CH))

        def step(t, _):
            r0 = pl.multiple_of(t * T, T)
            for c in range(L // CH):
                lre, lim = pl.ds(c * CH, CH), pl.ds(L + c * CH, CH)
                xr, xi = bu_ref[pl.ds(r0, T), lre], bu_ref[pl.ds(r0, T), lim]
                for n, k in enumerate((1, 2, 4)):
                    pr, pi = p_ref[n:n + 1, lre], p_ref[n:n + 1, lim]
                    sr = jnp.where(row >= k, pltpu.roll(xr, k, 0), 0.0)
                    si = jnp.where(row >= k, pltpu.roll(xi, k, 0), 0.0)
                    xr, xi = xr + pr * sr - pi * si, xi + pr * si + pi * sr
                cr, ci = carry[T - 1:T, lre], carry[T - 1:T, lim]
                wr, wi = w_ref[:, lre], w_ref[:, lim]
                xr, xi = xr + wr * cr - wi * ci, xi + wr * ci + wi * cr
                carry[:, lre] = xr
                carry[:, lim] = xi
                x_ref[pl.ds(r0, T), lre] = xr
                x_ref[pl.ds(r0, T), lim] = xi
            return 0

        lax.fori_loop(0, rt // T, step, 0)

    row_spec = pl.BlockSpec((rt, L2), lambda i: (i, 0))
    return pl.pallas_call(
        body, name="s5_scan", grid=(S // rt,), in_specs=[row_spec, pl.BlockSpec((1, L2), lambda i: (0, 0))],
        out_specs=row_spec, out_shape=jax.ShapeDtypeStruct((S, L2), f32),
        scratch_shapes=[pltpu.VMEM((T, L2), f32), pltpu.VMEM((T, L2), f32), pltpu.VMEM((T, L2), f32)],
        compiler_params=_cp("arbitrary"),
    )(bu, ab)


def _s5_scan_bwd(dx, x, ab):
    S, L2 = dx.shape
    L = L2 // 2
    rt = _pick(S, 256, SUBLANES)
    T = SUBLANES
    CH = S5_CHUNK
    nblk = S // rt
    per = rt // T

    def body(dx_ref, x_ref, xp_ref, ab_ref, g_ref, da_ref, p_ref, w_ref, carry, acc):
        pid = pl.program_id(0)

        @pl.when(pid == 0)
        def _():
            carry[...] = jnp.zeros_like(carry)
            acc[...] = jnp.zeros_like(acc)
            _s5_power_tables(ab_ref, p_ref, w_ref, conj=True)

        row = _tile_rows((T, CH))
        first_block = pid == nblk - 1

        def step(j, _):
            t = per - 1 - j
            r0 = pl.multiple_of(t * T, T)
            rp = pl.multiple_of(jnp.maximum(t - 1, 0) * T, T)
            for c in range(L // CH):
                lre, lim = pl.ds(c * CH, CH), pl.ds(L + c * CH, CH)
                gr, gi = dx_ref[pl.ds(r0, T), lre], dx_ref[pl.ds(r0, T), lim]
                for n, k in enumerate((1, 2, 4)):
                    pr, pi = p_ref[n:n + 1, lre], p_ref[n:n + 1, lim]
                    sr = jnp.where(row < T - k, pltpu.roll(gr, T - k, 0), 0.0)
                    si = jnp.where(row < T - k, pltpu.roll(gi, T - k, 0), 0.0)
                    gr, gi = gr + pr * sr - pi * si, gi + pr * si + pi * sr
                cr, ci = carry[0:1, lre], carry[0:1, lim]
                wr, wi = w_ref[:, lre], w_ref[:, lim]
                gr, gi = gr + wr * cr - wi * ci, gi + wr * ci + wi * cr
                carry[:, lre] = gr
                carry[:, lim] = gi
                g_ref[pl.ds(r0, T), lre] = gr
                g_ref[pl.ds(r0, T), lim] = gi
                xr, xi = x_ref[pl.ds(r0, T), lre], x_ref[pl.ds(r0, T), lim]
                in_blk_r, in_blk_i = x_ref[pl.ds(rp, T), lre], x_ref[pl.ds(rp, T), lim]
                hal_r = jnp.where(first_block, 0.0, xp_ref[:, lre])
                hal_i = jnp.where(first_block, 0.0, xp_ref[:, lim])
                pvr = jnp.where(t == 0, hal_r, in_blk_r)[T - 1:T, :]
                pvi = jnp.where(t == 0, hal_i, in_blk_i)[T - 1:T, :]
                sxr = jnp.where(row >= 1, pltpu.roll(xr, 1, 0), pvr)
                sxi = jnp.where(row >= 1, pltpu.roll(xi, 1, 0), pvi)
                acc[:, lre] += gr * sxr + gi * sxi
                acc[:, lim] += gi * sxr - gr * sxi
            return 0

        lax.fori_loop(0, per, step, 0)

        @pl.when(pid == nblk - 1)
        def _():
            da_ref[...] = jnp.sum(acc[...], axis=0, keepdims=True)

    row_spec = pl.BlockSpec((rt, L2), lambda i: (nblk - 1 - i, 0))
    halo = pl.BlockSpec((T, L2), lambda i: (jnp.maximum((nblk - 1 - i) * per - 1, 0), 0))
    vec = pl.BlockSpec((1, L2), lambda i: (0, 0))
    return pl.pallas_call(
        body, name="s5_scan_bwd", grid=(nblk,), in_specs=[row_spec, row_spec, halo, vec],
        out_specs=[row_spec, vec],
        out_shape=[jax.ShapeDtypeStruct((S, L2), f32), jax.ShapeDtypeStruct((1, L2), f32)],
        scratch_shapes=[pltpu.VMEM((T, L2), f32)] * 4,
        compiler_params=_cp("arbitrary"),
    )(dx, x, x, ab)


def _S5_U_SPEC(rt):
    return pl.BlockSpec((rt, LRU_WIDTH), lambda i: (i, PROJ_S5_U))


def _s5_out(yc, proj, d, wglu, bglu):
    S, W = yc.shape
    rt = _pick(S, ROW_TILE, SUBLANES)

    def body(yc_ref, u_ref, d_ref, w_ref, b_ref, o_ref, y_ref):
        y = yc_ref[...] + d_ref[...] * u_ref[:, 0:W]
        yg = _gelu(y)
        z = _dot(yg.astype(bf16), w_ref[...], NN) + b_ref[...]
        o_ref[...] = yg * _sigmoid(z)
        y_ref[...] = y

    row = pl.BlockSpec((rt, W), lambda i: (i, 0))
    vec = pl.BlockSpec((1, W), lambda i: (0, 0))
    mat = pl.BlockSpec((W, W), lambda i: (0, 0))
    return pl.pallas_call(
        body, name="s5_out", grid=(S // rt,), in_specs=[row, _S5_U_SPEC(rt), vec, mat, vec], out_specs=[row, row],
        out_shape=[jax.ShapeDtypeStruct((S, W), f32)] * 2, compiler_params=_cp("parallel"),
    )(yc, proj, d.reshape(1, W), wglu, bglu.reshape(1, W))


def _s5_out_bwd(dssm, y, proj, d, wglu, bglu):
    S, W = y.shape
    rt = _pick(S, ROW_TILE, SUBLANES)

    def body(do_ref, y_ref, u_ref, d_ref, w_ref, b_ref, dy_ref, du_ref, dw_ref, acc_ref):
        @pl.when(pl.program_id(0) == 0)
        def _():
            dw_ref[...] = jnp.zeros_like(dw_ref)
            acc_ref[...] = jnp.zeros_like(acc_ref)

        do, y = do_ref[...], y_ref[...]
        yg = _gelu(y)
        ygb = yg.astype(bf16)
        sg = _sigmoid(_dot(ygb, w_ref[...], NN) + b_ref[...])
        dz = do * yg * sg * (1.0 - sg)
        dzb = dz.astype(bf16)
        dyg = do * sg + _dot(dzb, w_ref[...], NT)
        dy = dyg * _gelu_grad(y)
        dy_ref[...] = dy
        du_ref[...] = dy * d_ref[...]
        dw_ref[...] += _dot(ygb, dzb, TN)
        acc_ref[0:1, :] += jnp.sum(dz, axis=0, keepdims=True)
        acc_ref[1:2, :] += jnp.sum(dy * u_ref[:, 0:W], axis=0, keepdims=True)

    row = pl.BlockSpec((rt, W), lambda i: (i, 0))
    vec = pl.BlockSpec((1, W), lambda i: (0, 0))
    mat = pl.BlockSpec((W, W), lambda i: (0, 0))
    acc = pl.BlockSpec((SUBLANES, W), lambda i: (0, 0))
    return pl.pallas_call(
        body, name="s5_out_bwd", grid=(S // rt,), in_specs=[row, row, _S5_U_SPEC(rt), vec, mat, vec],
        out_specs=[row, row, mat, acc],
        out_shape=[jax.ShapeDtypeStruct((S, W), f32)] * 2
        + [jax.ShapeDtypeStruct((W, W), f32), jax.ShapeDtypeStruct((SUBLANES, W), f32)],
        compiler_params=_cp("arbitrary"),
    )(dssm, y, proj, d.reshape(1, W), wglu, bglu.reshape(1, W))


MIX_SPLITS = ((0, ATTN_WIDTH), (ATTN_WIDTH, ATTN_WIDTH + LRU_WIDTH), (ATTN_WIDTH + LRU_WIDTH, D_MODEL))


def _mixnorm(attn, lru, ssm, g):
    S = attn.shape[0]
    rt = _pick(S, ROW_TILE, SUBLANES)

    def body(a_ref, l_ref, s_ref, g_ref, o_ref):
        for ref, (lo, hi) in zip((a_ref, l_ref, s_ref), MIX_SPLITS):
            x = ref[...]
            ms = jnp.mean(x * x, axis=-1, keepdims=True)
            o_ref[:, lo:hi] = (x * lax.rsqrt(ms + RMS_EPS) * g_ref[:, lo:hi]).astype(o_ref.dtype)

    rows = [pl.BlockSpec((rt, hi - lo), lambda i: (i, 0)) for lo, hi in MIX_SPLITS]
    return pl.pallas_call(
        body, name="mixnorm", grid=(S // rt,), in_specs=rows + [pl.BlockSpec((1, D_MODEL), lambda i: (0, 0))],
        out_specs=pl.BlockSpec((rt, D_MODEL), lambda i: (i, 0)),
        out_shape=jax.ShapeDtypeStruct((S, D_MODEL), bf16), compiler_params=_cp("parallel"),
    )(attn, lru, ssm, g.reshape(1, D_MODEL))


def _mixnorm_bwd(dmixed, attn, lru, ssm, g):
    S = attn.shape[0]
    rt = _pick(S, ROW_TILE, SUBLANES)

    def body(d_ref, a_ref, l_ref, s_ref, g_ref, da_ref, dl_ref, ds_ref, dlt_ref, acc_ref):
        @pl.when(pl.program_id(0) == 0)
        def _():
            acc_ref[...] = jnp.zeros_like(acc_ref)

        outs = []
        for ref, (lo, hi) in zip((a_ref, l_ref, s_ref), MIX_SPLITS):
            x = ref[...]
            dy = d_ref[:, lo:hi]
            rinv = lax.rsqrt(jnp.mean(x * x, axis=-1, keepdims=True) + RMS_EPS)
            dyg = dy * g_ref[:, lo:hi]
            outs.append(rinv * dyg - x * (rinv * rinv * rinv) * jnp.mean(dyg * x, axis=-1, keepdims=True))
            acc_ref[0:1, lo:hi] += jnp.sum(dy * x * rinv, axis=0, keepdims=True)
        da_ref[...], dl_ref[...], ds_ref[...] = outs
        hi_ = lax.broadcasted_iota(jnp.int32, (ATTN_WIDTH, ATTN_WIDTH), 0) // HEAD_DIM
        hj_ = lax.broadcasted_iota(jnp.int32, (ATTN_WIDTH, ATTN_WIDTH), 1) // HEAD_DIM
        same = jnp.where(hi_ == hj_, 1.0, 0.0).astype(f32)
        dlt_ref[...] = jnp.dot(outs[0] * a_ref[...], same, precision=lax.Precision.HIGHEST, preferred_element_type=f32)

    rows = [pl.BlockSpec((rt, hi - lo), lambda i: (i, 0)) for lo, hi in MIX_SPLITS]
    full = pl.BlockSpec((rt, D_MODEL), lambda i: (i, 0))
    return pl.pallas_call(
        body, name="mixnorm_bwd", grid=(S // rt,),
        in_specs=[full] + rows + [pl.BlockSpec((1, D_MODEL), lambda i: (0, 0))],
        out_specs=rows + [rows[0], pl.BlockSpec((SUBLANES, D_MODEL), lambda i: (0, 0))],
        out_shape=[jax.ShapeDtypeStruct((S, hi - lo), f32) for lo, hi in MIX_SPLITS]
        + [jax.ShapeDtypeStruct((S, ATTN_WIDTH), f32), jax.ShapeDtypeStruct((SUBLANES, D_MODEL), f32)],
        compiler_params=_cp("arbitrary"),
    )(dmixed, attn, lru, ssm, g.reshape(1, D_MODEL))


FFN_COL_TILE = 1536


def _ffn_conv(x, prev, w_ref, b_ref, K):
    y = b_ref[...] + w_ref[K - 1:K, :] * x
    for k in range(K - 1):
        y = y + w_ref[k:k + 1, :] * _shift_down(x, prev, K - 1 - k)
    return y


def _ffn_act(up, conv_w, conv_b):
    S, C2 = up.shape
    C = C2 // 2
    K = conv_w.shape[0]
    ct = FFN_COL_TILE
    nct = C // ct
    rt = _pick(S, ROW_TILE, SUBLANES)

    def body(g_ref, gp_ref, v_ref, vp_ref, wg_ref, wv_ref, bg_ref, bv_ref, o_ref):
        first = pl.program_id(1) == 0
        gate = _ffn_conv(g_ref[...], jnp.where(first, 0.0, gp_ref[...]), wg_ref, bg_ref, K)
        val = _ffn_conv(v_ref[...], jnp.where(first, 0.0, vp_ref[...]), wv_ref, bv_ref, K)
        o_ref[...] = (_gelu(gate) * val).astype(o_ref.dtype)

    def specs(off):
        return (pl.BlockSpec((rt, ct), lambda j, i: (i, j + off)), _prev_halo_spec(rt, ct, lambda j, i: j + off))

    def wspec(off, rows):
        return pl.BlockSpec((rows, ct), lambda j, i: (0, j + off))

    g_s, gp_s = specs(0)
    v_s, vp_s = specs(nct)
    return pl.pallas_call(
        body, name="ffn_act", grid=(nct, S // rt),
        in_specs=[g_s, gp_s, v_s, vp_s, wspec(0, K), wspec(nct, K), wspec(0, 1), wspec(nct, 1)],
        out_specs=pl.BlockSpec((rt, ct), lambda j, i: (i, j)),
        out_shape=jax.ShapeDtypeStruct((S, C), bf16), compiler_params=_cp("parallel", "parallel"),
    )(up, up, up, up, conv_w, conv_w, conv_b.reshape(1, C2), conv_b.reshape(1, C2))


def _ffn_act_bwd(dact, up, conv_w, conv_b):
    S, C2 = up.shape
    C = C2 // 2
    K = conv_w.shape[0]
    ct = FFN_COL_TILE
    nct = C // ct
    rt = _pick(S, ROW_TILE, SUBLANES)

    def body(d_ref, g_ref, gp_ref, v_ref, vp_ref, wg_ref, wv_ref, bg_ref, bv_ref, o_ref):
        first = pl.program_id(1) == 0
        gate = _ffn_conv(g_ref[...], jnp.where(first, 0.0, gp_ref[...]), wg_ref, bg_ref, K)
        val = _ffn_conv(v_ref[...], jnp.where(first, 0.0, vp_ref[...]), wv_ref, bv_ref, K)
        d = d_ref[...]
        gl, dgl = _gelu_pair(gate)
        o_ref[0] = d * val * dgl
        o_ref[1] = d * gl

    def specs(off):
        return (pl.BlockSpec((rt, ct), lambda j, i: (i, j + off)), _prev_halo_spec(rt, ct, lambda j, i: j + off))

    def wspec(off, rows):
        return pl.BlockSpec((rows, ct), lambda j, i: (0, j + off))

    g_s, gp_s = specs(0)
    v_s, vp_s = specs(nct)
    return pl.pallas_call(
        body, name="ffn_act_bwd", grid=(nct, S // rt),
        in_specs=[pl.BlockSpec((rt, ct), lambda j, i: (i, j)), g_s, gp_s, v_s, vp_s,
                  wspec(0, K), wspec(nct, K), wspec(0, 1), wspec(nct, 1)],
        out_specs=pl.BlockSpec((2, rt, ct), lambda j, i: (0, i, j)),
        out_shape=jax.ShapeDtypeStruct((2, S, C), f32), compiler_params=_cp("parallel", "parallel"),
    )(dact, up, up, up, up, conv_w, conv_w, conv_b.reshape(1, C2), conv_b.reshape(1, C2))


ANY = pl.BlockSpec(memory_space=pl.ANY)


def _rows_for(cols):
    return max(16, (1 << 19) // cols)


def _chips(x, y):
    return [(1 - x, y), (x, 1 - y), (1 - x, 1 - y)]


class _Gather:
    def __init__(self, shards, split):
        self.shapes = [s.shape for s in shards]
        self.dtypes = [s.dtype for s in shards]
        self.split = list(split)
        self.nt = len(shards)
        self.in_specs = [ANY] * self.nt
        self.out_specs = [ANY] * self.nt
        self.out_shape = [jax.ShapeDtypeStruct((4,) + s, dt) for s, dt in zip(self.shapes, self.dtypes)]
        self.scratch = [pltpu.SemaphoreType.DMA((3, self.nt))] * 4 if self.nt else []

    def _part(self, ref, t, half):
        if not self.split[t]:
            return ref
        r = self.shapes[t][0] // 2
        return ref.at[pl.ds(half * r, r), :]

    def _ici(self, ins, outs, sems, k, t, chip, landing_chip):
        x, y, c = lax.axis_index("x"), lax.axis_index("y"), lax.axis_index("c")
        return pltpu.make_async_remote_copy(
            src_ref=self._part(ins[t], t, c), dst_ref=self._part(outs[t].at[landing_chip], t, c),
            send_sem=sems[0].at[k, t], recv_sem=sems[1].at[k, t], device_id=(chip[0], chip[1], c), device_id_type=MESH)

    def _d2d(self, outs, sems, k, t, q, half):
        x, y, c = lax.axis_index("x"), lax.axis_index("y"), lax.axis_index("c")
        rows = self._part(outs[t].at[q], t, half)
        return pltpu.make_async_remote_copy(
            src_ref=rows, dst_ref=rows, send_sem=sems[2].at[k, t], recv_sem=sems[3].at[k, t],
            device_id=(x, y, 1 - c), device_id_type=MESH)

    def start(self, ins, outs, sems):
        x, y = lax.axis_index("x"), lax.axis_index("y")
        me = 2 * x + y
        for k, chip in enumerate(_chips(x, y)):
            for t in range(self.nt):
                self._ici(ins, outs, sems, k, t, chip, me).start()

    def finish(self, ins, outs, sems):
        x, y, c = lax.axis_index("x"), lax.axis_index("y"), lax.axis_index("c")
        me = 2 * x + y
        chips = _chips(x, y)
        for k, chip in enumerate(chips):
            q = 2 * chip[0] + chip[1]
            for t in range(self.nt):
                self._ici(ins, outs, sems, k, t, chip, q).wait_recv()
                if self.split[t]:
                    self._d2d(outs, sems, k, t, q, c).start()
        for k, chip in enumerate(chips):
            q = 2 * chip[0] + chip[1]
            for t in range(self.nt):
                if self.split[t]:
                    self._d2d(outs, sems, k, t, q, 1 - c).wait_recv()
        for k, chip in enumerate(chips):
            q = 2 * chip[0] + chip[1]
            for t in range(self.nt):
                self._ici(ins, outs, sems, k, t, chip, me).wait_send()
                if self.split[t]:
                    self._d2d(outs, sems, k, t, q, c).wait_send()


def _gather_weights(shards, split):
    g = _Gather(shards, split)
    nt = g.nt

    def body(*refs):
        ins, outs, sems = refs[:nt], refs[nt:2 * nt], refs[2 * nt:]
        g.start(ins, outs, sems)
        g.finish(ins, outs, sems)

    return pl.pallas_call(
        body, name="gather_weights", in_specs=g.in_specs, out_specs=g.out_specs, out_shape=g.out_shape,
        scratch_shapes=g.scratch,
    )(*shards)


class _SiblingSend:
    def __init__(self, gs, dst_c, swap=()):
        self.nt, self.n = len(gs), len(gs) + len(swap)
        self.dst_c = list(dst_c) if isinstance(dst_c, (list, tuple)) else [dst_c] * self.nt
        self.in_specs = [ANY] * self.n
        self.out_specs = [ANY] * self.n
        self.out_shape = [jax.ShapeDtypeStruct(g.shape, g.dtype) for g in list(gs) + list(swap)]
        self.scratch = [pltpu.SemaphoreType.DMA((self.n,))] * 2 if self.n else []

    def _each(self, ins, outs, sems, sender, fn):
        x, y, c = lax.axis_index("x"), lax.axis_index("y"), lax.axis_index("c")

        def cp(t):
            return pltpu.make_async_remote_copy(
                src_ref=ins[t], dst_ref=outs[t], send_sem=sems[0].at[t], recv_sem=sems[1].at[t],
                device_id=(x, y, 1 - c), device_id_type=MESH)

        for dst in (0, 1):
            which = [t for t in range(self.nt) if self.dst_c[t] == dst]
            if which:
                @pl.when((c != dst) if sender else (c == dst))
                def _(which=which):
                    for t in which:
                        fn(cp(t))
        for t in range(self.nt, self.n):
            fn(cp(t))

    def start(self, ins, outs, sems):
        self._each(ins, outs, sems, True, lambda cp: cp.start())

    def finish(self, ins, outs, sems):
        self._each(ins, outs, sems, False, lambda cp: cp.wait_recv())
        self._each(ins, outs, sems, True, lambda cp: cp.wait_send())


def _sibling_send(gs, dst_c, swap=()):
    snd = _SiblingSend(gs, dst_c, swap)
    n = snd.n

    def body(*refs):
        ins, outs, sems = refs[:n], refs[n:2 * n], refs[2 * n:]
        snd.start(ins, outs, sems)
        snd.finish(ins, outs, sems)

    res = pl.pallas_call(
        body, name="sibling_send", in_specs=snd.in_specs, out_specs=snd.out_specs, out_shape=snd.out_shape,
        scratch_shapes=snd.scratch,
    )(*gs, *swap)
    return list(res[:snd.nt]), list(res[snd.nt:])


def _owner_flag(owner_c):
    return (lax.axis_index("c") == owner_c).astype(jnp.int32).reshape(1)


def _pair_sum(g, other, name, owner_c, col_slabs=False):
    R, C = g.shape
    cb = C // 4 if col_slabs else C
    rt = _pick(R, _rows_for(cb), 16)

    def body(on_ref, a_ref, o_ref, out_ref):
        out_ref[...] = (a_ref[...] + o_ref[...]).astype(out_ref.dtype)

    row = pl.BlockSpec((rt, cb), lambda q, i, on: (i * on[0], q * on[0]))
    if col_slabs:
        out_spec = pl.BlockSpec((None, rt, cb), lambda q, i, on: (q * on[0], i * on[0], 0))
        out_shape = jax.ShapeDtypeStruct((4, R, cb), bf16)
    else:
        out_spec, out_shape = row, jax.ShapeDtypeStruct((R, C), bf16)
    return pl.pallas_call(
        body, name=name,
        grid_spec=pltpu.PrefetchScalarGridSpec(num_scalar_prefetch=1, grid=(C // cb, R // rt),
                                               in_specs=[row, row], out_specs=out_spec),
        out_shape=out_shape, compiler_params=_cp("arbitrary", "arbitrary"),
    )(_owner_flag(owner_c), g, other)


class _ChipExchange:
    NORTH = 1

    def __init__(self, slabs, whole, only_c):
        self.ns, self.nw = len(slabs), len(whole)
        self.only_c = list(only_c) if isinstance(only_c, (list, tuple)) else [only_c] * self.ns
        self.only_c += [self.NORTH] * self.nw
        self.n = self.ns + self.nw
        self.in_specs = [ANY] * self.n
        self.out_specs = [ANY] * self.n
        self.out_shape = ([jax.ShapeDtypeStruct(s.shape, s.dtype) for s in slabs]
                          + [jax.ShapeDtypeStruct((4,) + w.shape, w.dtype) for w in whole])
        self.scratch = [pltpu.SemaphoreType.DMA((3, self.n))] * 2 if self.n else []
        self.scratch += [pltpu.SemaphoreType.DMA((3, self.nw))] * 2 if self.nw else []

    def _copy(self, ins, outs, sems, k, t, chip, landing_chip):
        c = lax.axis_index("c")
        src = ins[t].at[2 * chip[0] + chip[1]] if t < self.ns else ins[t]
        return pltpu.make_async_remote_copy(
            src_ref=src, dst_ref=outs[t].at[landing_chip], send_sem=sems[0].at[k, t], recv_sem=sems[1].at[k, t],
            device_id=(chip[0], chip[1], c), device_id_type=MESH)

    def _pass_on(self, outs, sems, k, t, chip):
        x, y, c = lax.axis_index("x"), lax.axis_index("y"), lax.axis_index("c")
        rows = outs[t].at[2 * chip[0] + chip[1]]
        return pltpu.make_async_remote_copy(
            src_ref=rows, dst_ref=rows, send_sem=sems[2].at[k, t - self.ns], recv_sem=sems[3].at[k, t - self.ns],
            device_id=(x, y, 1 - c), device_id_type=MESH)

    def _each(self, fn, north=True):
        x, y, c = lax.axis_index("x"), lax.axis_index("y"), lax.axis_index("c")
        chips = _chips(x, y)
        groups = {}
        if north:
            for t in range(self.n):
                groups.setdefault(self.only_c[t], []).append(t)
        elif self.nw:
            groups[1 - self.NORTH] = list(range(self.ns, self.n))
        for owner, which in groups.items():
            @pl.when(c == owner)
            def _(which=which):
                for k, chip in enumerate(chips):
                    for t in which:
                        fn(k, t, chip)

    def start(self, ins, outs, sems):
        me = 2 * lax.axis_index("x") + lax.axis_index("y")
        self._each(lambda k, t, chip: self._copy(ins, outs, sems, k, t, chip, me).start())

    def finish(self, ins, outs, sems):
        me = 2 * lax.axis_index("x") + lax.axis_index("y")

        def landed(k, t, chip):
            self._copy(ins, outs, sems, k, t, chip, 2 * chip[0] + chip[1]).wait_recv()
            if t >= self.ns:
                self._pass_on(outs, sems, k, t, chip).start()

        def sent(k, t, chip):
            self._copy(ins, outs, sems, k, t, chip, me).wait_send()
            if t >= self.ns:
                self._pass_on(outs, sems, k, t, chip).wait_send()

        self._each(landed)
        self._each(lambda k, t, chip: self._pass_on(outs, sems, k, t, chip).wait_recv(), north=False)
        self._each(sent)


def _chip_exchange(slabs, whole, only_c):
    ex = _ChipExchange(slabs, whole, only_c)
    n = ex.n

    def body(*refs):
        ins, outs, sems = refs[:n], refs[n:2 * n], refs[2 * n:]
        ex.start(ins, outs, sems)
        ex.finish(ins, outs, sems)

    res = pl.pallas_call(
        body, name="chip_exchange", in_specs=ex.in_specs, out_specs=ex.out_specs, out_shape=ex.out_shape,
        scratch_shapes=ex.scratch,
    )(*slabs, *whole)
    return list(res[:ex.ns]), list(res[ex.ns:])


def _sum_chips(recv, own, name, owner_c=None):
    n, r, C = recv.shape
    rt = _pick(r, _rows_for(C), 16)
    own3 = own.ndim == 3

    def body(on_ref, r_ref, o_ref, out_ref):
        me = 2 * lax.axis_index("x") + lax.axis_index("y")
        acc = None
        for q in range(n):
            term = jnp.where(me == q, o_ref[q] if own3 else o_ref[...], r_ref[q]).astype(f32)
            acc = term if acc is None else acc + term
        out_ref[...] = acc

    blk = pl.BlockSpec((n, rt, C), lambda i, on: (0, i * on[0], 0))
    row = pl.BlockSpec((rt, C), lambda i, on: (i * on[0], 0))
    flag = jnp.ones((1,), jnp.int32) if owner_c is None else _owner_flag(owner_c)
    return pl.pallas_call(
        body, name=name,
        grid_spec=pltpu.PrefetchScalarGridSpec(num_scalar_prefetch=1, grid=(r // rt,),
                                               in_specs=[blk, blk if own3 else row], out_specs=row),
        out_shape=jax.ShapeDtypeStruct((r, C), f32), compiler_params=_cp("arbitrary"),
    )(flag, recv, own)


def _adamw_layers(mine, theirs, owners, w, m, v, name):
    L, r, C = w.shape
    rt = _pick(r, _rows_for(C), 16)

    def body(a0_ref, a1_ref, b0_ref, b1_ref, w_ref, m_ref, v_ref, g_ref, d_ref, mo_ref, vo_ref):
        layer, c = pl.program_id(0), lax.axis_index("c")
        g0 = jnp.where(c == owners[0], a0_ref[...], b0_ref[...])
        g1 = jnp.where(c == owners[1], a1_ref[...], b1_ref[...])
        g_ref[...] = jnp.where(layer == 0, g0, g1)
        _adamw_math(g_ref, w_ref, m_ref, v_ref, d_ref, mo_ref, vo_ref)

    flat = pl.BlockSpec((rt, C), lambda l, i: (i, 0))
    lay = pl.BlockSpec((None, rt, C), lambda l, i: (l, i, 0))
    return pl.pallas_call(
        body, name=name, grid=(L, r // rt), in_specs=[flat] * 4 + [lay] * 3, out_specs=[lay] * 4,
        out_shape=[jax.ShapeDtypeStruct((L, r, C), f32)] * 4, compiler_params=_cp("parallel", "parallel"),
    )(mine[0], mine[1], theirs[0], theirs[1], w, m, v)


def _adamw_math(g_ref, w_ref, m_ref, v_ref, d_ref, mo_ref, vo_ref):
    gg = g_ref[...]
    m_new = ADAM_B1 * m_ref[...] + (1.0 - ADAM_B1) * gg
    v_new = ADAM_B2 * v_ref[...] + (1.0 - ADAM_B2) * (gg * gg)
    m_hat = m_new / (1.0 - ADAM_B1 ** ADAM_STEP)
    v_hat = v_new / (1.0 - ADAM_B2 ** ADAM_STEP)
    d_ref[...] = -ADAM_LR * (m_hat / (jnp.sqrt(v_hat) + ADAM_EPS) + ADAM_WD * w_ref[...])
    mo_ref[...] = m_new
    vo_ref[...] = v_new


FLAT_TILE = 2048


def _add2(a, b, name):
    R = a.shape[0]
    rt = _pick(R, FLAT_TILE, SUBLANES)

    def body(a_ref, b_ref, o_ref):
        o_ref[...] = a_ref[...] + b_ref[...]

    row = pl.BlockSpec((rt, LANES), lambda i: (i, 0))
    return pl.pallas_call(
        body, name=name, grid=(R // rt,), in_specs=[row, row], out_specs=row,
        out_shape=jax.ShapeDtypeStruct((R, LANES), f32), compiler_params=_cp("parallel"),
    )(a, b)


def _adamw(g, w, m, v, name):
    R = g.shape[0]
    rt = _pick(R, FLAT_TILE, SUBLANES)

    def body(g_ref, w_ref, m_ref, v_ref, d_ref, mo_ref, vo_ref):
        _adamw_math(g_ref, w_ref, m_ref, v_ref, d_ref, mo_ref, vo_ref)

    row = pl.BlockSpec((rt, LANES), lambda i: (i, 0))
    return pl.pallas_call(
        body, name=name, grid=(R // rt,), in_specs=[row] * 4, out_specs=[row] * 3,
        out_shape=[jax.ShapeDtypeStruct((R, LANES), f32)] * 3, compiler_params=_cp("parallel"),
    )(g, w, m, v)


def _adamw_whole(g, w, m, v, name):
    def body(g_ref, w_ref, m_ref, v_ref, d_ref, mo_ref, vo_ref):
        _adamw_math(g_ref, w_ref, m_ref, v_ref, d_ref, mo_ref, vo_ref)

    return pl.pallas_call(
        body, name=name, out_shape=[jax.ShapeDtypeStruct(w.shape, f32)] * 3, compiler_params=_cp(),
    )(g, w, m, v)


def _pack(arrs, dtype, rows=None):
    flat = jnp.concatenate([a.astype(dtype).reshape(-1) for a in arrs])
    per = FLAT_TILE * LANES
    total = rows * LANES if rows else flat.shape[0] + (-flat.shape[0]) % per
    flat = jnp.pad(flat, (0, total - flat.shape[0]))
    return flat.reshape(-1, LANES)


def _unpack(buf, shapes):
    flat = buf.reshape(-1)
    out, off = [], 0
    for s in shapes:
        n = math.prod(s)
        out.append(flat[off:off + n].reshape(s))
        off += n
    return out


def _block_diag(w):
    n, a, b = w.shape
    eye = jnp.eye(n, dtype=w.dtype)
    return (w[:, :, None, :] * eye[:, None, :, None]).reshape(n * a, n * b)


def _diag_blocks(m, n):
    a, b = m.shape[0] // n, m.shape[1] // n
    idx = jnp.arange(n)
    return m.reshape(n, a, n, b)[idx, :, idx, :]


BIG = ("w_in", "w_out", "w_up", "w_down", "s5_w_glu")
BIG_COL_SHARDED = {"w_in": True, "w_out": False, "w_up": True, "w_down": False, "s5_w_glu": False}
CONV_SHARDED = ("lru_conv_w", "ffn_conv_w")
SMALL = ("lru_conv_b", "lru_wr", "lru_br", "lru_wi", "lru_bi", "lru_lambda", "s5_a_re", "s5_a_im", "s5_b_re",
         "s5_b_im", "s5_c_re", "s5_c_im", "s5_d", "s5_log_step", "s5_b_glu", "mix_norm_g", "ln1_g", "ln1_b",
         "ffn_conv_b", "ln2_g", "ln2_b")
WEIGHTS = ("w_in", "lru_conv_w", "lru_conv_b", "lru_wr", "lru_br", "lru_wi", "lru_bi", "lru_lambda", "s5_a_re",
           "s5_a_im", "s5_b_re", "s5_b_im", "s5_c_re", "s5_c_im", "s5_d", "s5_log_step", "s5_w_glu", "s5_b_glu",
           "mix_norm_g", "w_out", "ln1_g", "ln1_b", "w_up", "ffn_conv_w", "ffn_conv_b", "w_down", "ln2_g", "ln2_b")


def _assemble(slabs, col_sharded):
    _, L, r, c = slabs.shape
    if col_sharded:
        return slabs.transpose(1, 2, 0, 3).reshape(L, r, 4 * c)
    return slabs.transpose(1, 0, 2, 3).reshape(L, 4 * r, c)


def _s5_prepare(p):
    G = N_S5_GROUPS
    bt_re, bt_im = p["s5_b_re"].transpose(0, 2, 1), p["s5_b_im"].transpose(0, 2, 1)
    ls = p["s5_log_step"].reshape(G, 1)
    ab_re, ab_im, bb_re, bb_im = _s5_params(p["s5_a_re"], p["s5_a_im"], ls, bt_re, bt_im)
    ab = jnp.concatenate([ab_re.reshape(1, S5_LANES), ab_im.reshape(1, S5_LANES)], axis=1)
    bbcat = jnp.concatenate([_block_diag(bb_re), _block_diag(bb_im)], axis=1).astype(bf16)
    ccat = jnp.concatenate([_block_diag(p["s5_c_re"].transpose(0, 2, 1)),
                            -_block_diag(p["s5_c_im"].transpose(0, 2, 1))], axis=0).astype(bf16)
    bbcat_pad = jnp.concatenate([bbcat, jnp.zeros((LRU_WIDTH - S5_WIDTH, 2 * S5_LANES), bf16)], axis=0)
    return dict(bt_re=bt_re, bt_im=bt_im, ls=ls, ab=ab, bbcat=bbcat, bbcat_pad=bbcat_pad, ccat=ccat)


def _layer_fwd(h, p, cos, sin, pending, install):
    sv = {"h": h}
    proj = _mm(h, p["w_in"], "nn", "mm_proj", tn=D_IN_PAD)
    sv.update(proj=proj)
    qr, kr = _rope_fwd(proj, cos, sin)
    attn, ltot, gathered = _attn_fwd2(qr, kr, proj, [s for _, _, s in pending], [True] * len(pending))
    install(pending, gathered)
    sv.update(qr=qr, kr=kr, attn=attn, ltot=ltot)
    wr, wi = _block_diag(p["lru_wr"]).astype(bf16), _block_diag(p["lru_wi"]).astype(bf16)
    xc, r, i, log_a, u = _lru_pre(proj, p["lru_conv_w"], p["lru_conv_b"], wr, p["lru_br"], wi, p["lru_bi"],
                                  p["lru_lambda"])
    hl, lru = _lru_scan(log_a, u, proj)
    sv.update(wr=wr, wi=wi, xc=xc, r=r, i=i, log_a=log_a, hl=hl, lru=lru)
    s5 = _s5_prepare(p)
    bu = _mm(proj, s5["bbcat_pad"], "nn", "mm_s5_bu", a_win=(PROJ_S5_U, LRU_WIDTH))
    xs = _s5_scan(bu, s5["ab"])
    yc = _mm(xs, s5["ccat"], "nn", "mm_s5_y")
    ssm, y = _s5_out(yc, proj, p["s5_d"].reshape(-1), p["s5_w_glu"], p["s5_b_glu"])
    sv.update(s5=s5, xs=xs, y=y, ssm=ssm)
    mixed = _mixnorm(attn, lru, ssm, p["mix_norm_g"])
    mix = _mm(mixed, p["w_out"], "nn", "mm_out")
    h1, z1 = _ln_fwd(h, mix, p["ln1_g"], p["ln1_b"], "ln_fwd")
    sv.update(mixed=mixed, z1=z1, h1=h1)
    up = _mm(h1, p["w_up"], "nn", "mm_up", tn=1536)
    act = _ffn_act(up, p["ffn_conv_w"], p["ffn_conv_b"])
    ffn = _mm(act, p["w_down"], "nn", "mm_down")
    h2, z2 = _ln_fwd(h1, ffn, p["ln2_g"], p["ln2_b"], "ln_fwd")
    sv.update(up=up, act=act, z2=z2)
    return h2, sv


def _layer_bwd(dy_a, dy_b, p, sv, cos, sin, relay=None):
    gr = {}
    dz2, acc = _ln_bwd(dy_a, dy_b, sv["z2"], p["ln2_g"], "ln_bwd_top" if dy_a is None else "ln_bwd")
    gr["ln2_g"], gr["ln2_b"] = acc[0], acc[1]
    dact = _mm(dz2, p["w_down"], "nt", "mm_dact")
    gr["w_down"] = _mm(sv["act"], dz2, "tn", "mm_dw_down")
    dupc = _ffn_act_bwd(dact, sv["up"], p["ffn_conv_w"], p["ffn_conv_b"])
    others, others_dst, to_slabs = relay if relay else ((), 0, None)
    send = list(others) + ([gr["w_down"]] if relay else [])
    send_dst = [others_dst] * len(others) + ([EARLY_OWNER] if relay else [])
    dup, acc, from_sibling = _conv_bwd(dupc, sv["up"], p["ffn_conv_w"], "ffn_conv_bwd", col_tile=FFN_COL_TILE,
                                       out_dtype=bf16, send=send, send_dst=send_dst)
    slabs, owners = to_slabs(from_sibling[:-1], gr["w_down"], from_sibling[-1]) if relay else ((), 0)
    gr["ffn_conv_w"], gr["ffn_conv_b"] = acc[0:3], acc[3]
    dh1 = _mm(dup, p["w_up"], "nt", "mm_dh1", tk=2048)
    gr["w_up"] = _mm(sv["h1"], dup, "tn", "mm_dw_up", tn=1536)
    dz1, acc = _ln_bwd(dz2, dh1, sv["z1"], p["ln1_g"], "ln_bwd")
    gr["ln1_g"], gr["ln1_b"] = acc[0], acc[1]
    dmixed = _mm(dz1, p["w_out"], "nt", "mm_dmixed")
    gr["w_out"] = _mm(sv["mixed"], dz1, "tn", "mm_dw_out")
    dattn, dlru, dssm, delta, acc = _mixnorm_bwd(dmixed, sv["attn"], sv["lru"], sv["ssm"], p["mix_norm_g"])
    gr["mix_norm_g"] = acc[0]
    proj = sv["proj"]
    late = [gr["w_up"]] if relay else []
    dqr, dkr, dv, received, late_sibling = _attn_bwd2(sv["qr"], sv["kr"], proj, dattn, sv["ltot"], delta, slabs,
                                                      owners, late, EARLY_OWNER)
    dq, dk = _rope_bwd(dqr, dkr, cos, sin)
    g, dgate = _lru_scan_bwd(dlru, proj, sv["hl"], sv["log_a"])
    dxc, dwr, dwi, acc = _lru_gate_bwd(g, sv["hl"], sv["xc"], sv["r"], sv["i"], sv["log_a"], sv["wr"], sv["wi"],
                                       p["lru_lambda"])
    gr["lru_wr"], gr["lru_wi"] = _diag_blocks(dwr, N_LRU_HEADS), _diag_blocks(dwi, N_LRU_HEADS)
    gr["lru_br"], gr["lru_bi"], gr["lru_lambda"] = acc[0], acc[1], acc[2]
    dxr, acc, _ = _conv_bwd(dxc, proj, p["lru_conv_w"], "lru_conv_bwd", x_col_block=PROJ_LRU_X)
    gr["lru_conv_w"], gr["lru_conv_b"] = acc[0:4], acc[4]
    s5 = sv["s5"]
    G = N_S5_GROUPS
    dy, du_direct, dwglu, acc = _s5_out_bwd(dssm, sv["y"], proj, p["s5_d"].reshape(-1), p["s5_w_glu"],
                                            p["s5_b_glu"])
    gr["s5_w_glu"], gr["s5_b_glu"], gr["s5_d"] = dwglu, acc[0], acc[1].reshape(G, S5_GROUP)
    dxs = _mm(dy, s5["ccat"], "nt", "mm_s5_dx")
    dccat = _mm(sv["xs"], dy, "tn", "mm_s5_dc")
    gr["s5_c_re"] = _diag_blocks(dccat[:S5_LANES], G).transpose(0, 2, 1)
    gr["s5_c_im"] = -_diag_blocks(dccat[S5_LANES:], G).transpose(0, 2, 1)
    gs, dab = _s5_scan_bwd(dxs, sv["xs"], s5["ab"])
    du = _mm(gs, s5["bbcat"], "nt", "mm_s5_du", add=du_direct)
    dbbcat = _mm(proj, gs, "tn", "mm_s5_dbb", a_win=(PROJ_S5_U, LRU_WIDTH))[:S5_WIDTH]
    d_ar, d_ai, d_ls, d_btr, d_bti = _s5_params_bwd(
        p["s5_a_re"], p["s5_a_im"], s5["ls"], s5["bt_re"], s5["bt_im"],
        dab[:, :S5_LANES].reshape(G, S5_STATE), dab[:, S5_LANES:].reshape(G, S5_STATE),
        _diag_blocks(dbbcat[:, :S5_LANES], G), _diag_blocks(dbbcat[:, S5_LANES:], G))
    gr["s5_a_re"], gr["s5_a_im"], gr["s5_log_step"] = d_ar, d_ai, d_ls.reshape(G)
    gr["s5_b_re"], gr["s5_b_im"] = d_btr.transpose(0, 2, 1), d_bti.transpose(0, 2, 1)
    pad = jnp.zeros((du.shape[0], D_IN_PAD - D_IN), f32)
    dproj = jnp.concatenate([dq, dk, dv, dxr, dgate, du, pad], axis=1).astype(bf16)
    gr["w_in"] = _mm(sv["h"], dproj, "tn", "mm_dw_in", tn=768)[:, :D_IN]
    dh = _mm(dproj, p["w_in"], "nt", "mm_dh")
    return (dz1, dh, gr, slabs, received, late_sibling) if relay else (dz1, dh, gr)


def _train_step(d):
    x, target = d["x"][0], d["loss_target"][0]
    S = x.shape[0]
    me = 2 * lax.axis_index("x") + lax.axis_index("y")

    def rows2d(a):
        return a.reshape(a.shape[0] * a.shape[1], a.shape[2])

    params = [{n: d[n][l] for n in SMALL} for l in range(DEPTH)]

    def install(items, gathered):
        for (n, l, mine), g in zip(items, gathered):
            g = lax.dynamic_update_slice_in_dim(g, mine[None], me, axis=0)
            if n in CONV_SHARDED:
                full = _assemble(g.reshape((4,) + d[n].shape), True)
                for k in range(DEPTH):
                    params[k][n] = full[k]
                continue
            full = _assemble(g[:, None], BIG_COL_SHARDED[n])[0]
            if n == "w_in":
                full = jnp.pad(full, ((0, 0), (0, D_IN_PAD - D_IN)))
            params[l][n] = full

    def shard(n, l):
        return (n, l, d[n][l].astype(bf16))

    first = [shard("w_in", 0)] + [(n, None, rows2d(d[n])) for n in CONV_SHARDED]
    install(first, _gather_weights([s for _, _, s in first], [True] + [False] * len(CONV_SHARDED)))
    later = [[shard(n, 0) for n in BIG[1:]] + [shard("w_in", 1)], [shard(n, 1) for n in BIG[1:]]]

    cos, sin = _rope_tables(S)
    h, saved = x, []
    for l in range(DEPTH):
        h, sv = _layer_fwd(h, params[l], cos, sin, later[l], install)
        saved.append(sv)
    dy, loss_acc = _loss_head(h, target)
    def slab(n, g, other, owner):
        aligned = BIG_COL_SHARDED[n] and (g.shape[1] // 4) % LANES == 0
        p = _pair_sum(g, other, "pair_sum_" + n, owner, col_slabs=aligned)
        if BIG_COL_SHARDED[n] and not aligned:
            return p.reshape(p.shape[0], 4, p.shape[1] // 4).transpose(1, 0, 2)
        return p if aligned else p.reshape(4, p.shape[0] // 4, p.shape[1])

    own1 = {n: 1 - EARLY_OWNER for n in BIG}
    own0 = {n: (EARLY_OWNER if n in ("w_down", "w_up") else 1 - EARLY_OWNER) for n in BIG}

    def hidden_slabs(others1, w_down0, w_down0_sibling):
        slabs = [slab(n, grads[1][n], o, own1[n]) for n, o in zip(BIG, others1)]
        slabs.append(slab("w_down", w_down0, w_down0_sibling, own0["w_down"]))
        return slabs, [own1[n] for n in BIG] + [own0["w_down"]]

    da, db, grads = None, dy, [None] * DEPTH
    da, db, grads[1] = _layer_bwd(da, db, params[1], saved[1], cos, sin)
    relay = ([grads[1][n] for n in BIG], 1 - EARLY_OWNER, hidden_slabs)
    da, db, grads[0], hslabs, hrecv, (w_up0_sibling,) = _layer_bwd(da, db, params[0], saved[0], cos, sin, relay)
    out = {"grad_x": _axpy(da, db, "grad_x")[None]}

    small = SMALL + CONV_SHARDED
    sp = _pack([grads[l][n] for n in small for l in range(DEPTH)], f32)
    tail = [n for n in BIG if n != "w_down"]
    rest = [n for n in tail if n != "w_up"]
    others_rest, (sp_sibling,) = _sibling_send([grads[0][n] for n in rest], [own0[n] for n in rest], [sp])
    sibling0 = dict(zip(rest, others_rest), w_up=w_up0_sibling)
    tslabs = [slab(n, grads[0][n], sibling0[n], own0[n]) for n in tail]
    chip_small = _add2(sp, sp_sibling, "pair_sum_small")
    trecv, (recv_small,) = _chip_exchange(tslabs, [chip_small], [own0[n] for n in tail])
    mine0 = {n: _sum_chips(r, s, "sum_chips_" + n, own0[n]) for n, r, s in zip(tail, trecv, tslabs)}
    mine0["w_down"] = _sum_chips(hrecv[-1], hslabs[-1], "sum_chips_w_down", own0["w_down"])
    mine1 = {n: _sum_chips(r, s, "sum_chips_" + n, own1[n]) for n, r, s in zip(BIG, hrecv, hslabs)}
    sent, _ = _sibling_send([mine0[n] for n in BIG] + [mine1[n] for n in BIG],
                            [1 - own0[n] for n in BIG] + [1 - own1[n] for n in BIG])
    theirs0, theirs1 = dict(zip(BIG, sent[:len(BIG)])), dict(zip(BIG, sent[len(BIG):]))
    for n in BIG:
        upd = _adamw_layers((mine0[n], mine1[n]), (theirs0[n], theirs1[n]), (own0[n], own1[n]),
                            d[n], d["m_" + n], d["v_" + n], "adamw_" + n)
        for pre, u in zip(("grad_", "delta_", "new_m_", "new_v_"), upd):
            out[pre + n] = u

    total = _sum_chips(recv_small, chip_small, "sum_chips_small")
    rows = total.shape[0]
    upd = _adamw(total, _pack([d[n] for n in SMALL], f32, rows), _pack([d["m_" + n] for n in SMALL], f32, rows),
                 _pack([d["v_" + n] for n in SMALL], f32, rows), "adamw_small")
    small_shapes = [d[n].shape for n in SMALL]
    for pre, buf in zip(("grad_", "delta_", "new_m_", "new_v_"), (total,) + tuple(upd)):
        for n, a in zip(SMALL, _unpack(buf, small_shapes)):
            out[pre + n] = a
    conv_full = _unpack(total, small_shapes + [(DEPTH,) + grads[0][n].shape for n in CONV_SHARDED])
    for n, g in zip(CONV_SHARDED, conv_full[len(SMALL):]):
        L, K, C = g.shape
        g = lax.dynamic_index_in_dim(g.reshape(L, K, 4, C // 4), me, axis=2, keepdims=False)
        out["grad_" + n] = g
        for pre, u in zip(("delta_", "new_m_", "new_v_"), _adamw_whole(g, d[n], d["m_" + n], d["v_" + n], "adamw_" + n)):
            out[pre + n] = u

    loss_local, _ = lax.optimization_barrier((loss_acc[0, 0], upd[0]))
    out["loss"] = lax.psum(loss_local, ("x", "y", "c"))
    return (out["loss"], out["grad_x"]) + tuple(out[pre + n] for pre in ("grad_", "delta_", "new_m_", "new_v_")
                                                for n in WEIGHTS)


def kernel(
        x, w_in, lru_conv_w, lru_conv_b, lru_wr, lru_br, lru_wi, lru_bi, lru_lambda, s5_a_re, s5_a_im, s5_b_re,
        s5_b_im, s5_c_re, s5_c_im, s5_d, s5_log_step, s5_w_glu, s5_b_glu, mix_norm_g, w_out, ln1_g, ln1_b, w_up,
        ffn_conv_w, ffn_conv_b, w_down, ln2_g, ln2_b, loss_target, m_w_in, m_lru_conv_w, m_lru_conv_b, m_lru_wr,
        m_lru_br, m_lru_wi, m_lru_bi, m_lru_lambda, m_s5_a_re, m_s5_a_im, m_s5_b_re, m_s5_b_im, m_s5_c_re,
        m_s5_c_im, m_s5_d, m_s5_log_step, m_s5_w_glu, m_s5_b_glu, m_mix_norm_g, m_w_out, m_ln1_g, m_ln1_b,
        m_w_up, m_ffn_conv_w, m_ffn_conv_b, m_w_down, m_ln2_g, m_ln2_b, v_w_in, v_lru_conv_w, v_lru_conv_b,
        v_lru_wr, v_lru_br, v_lru_wi, v_lru_bi, v_lru_lambda, v_s5_a_re, v_s5_a_im, v_s5_b_re, v_s5_b_im,
        v_s5_c_re, v_s5_c_im, v_s5_d, v_s5_log_step, v_s5_w_glu, v_s5_b_glu, v_mix_norm_g, v_w_out, v_ln1_g,
        v_ln1_b, v_w_up, v_ffn_conv_w, v_ffn_conv_b, v_w_down, v_ln2_g, v_ln2_b
):
    return _train_step(dict(locals()))
```

```python
import math

import jax
import jax.numpy as jnp
from jax import lax
from jax.experimental import pallas as pl
from jax.experimental.pallas import tpu as pltpu

f32 = jnp.float32
bf16 = jnp.bfloat16
MESH = pl.DeviceIdType.MESH

D_MODEL = 1024
ATTN_WIDTH = 384
LRU_WIDTH = 384
S5_WIDTH = 256
HEAD_DIM = 64
N_LRU_HEADS = 6
N_S5_GROUPS = 16
S5_GROUP = 16
S5_STATE = 64
S5_LANES = N_S5_GROUPS * S5_STATE
D_FF = 3072
D_IN = 2176
LRU_C = 8.0
ROPE_THETA = 10000.0
DILATIONS = (1, 4, 16)
ATTN_BLOCK = 128
DEPTH = 2
ALPHA = (2 * DEPTH) ** 0.25
LN_EPS = 1e-5
RMS_EPS = 1e-6
ADAM_LR, ADAM_B1, ADAM_B2, ADAM_EPS, ADAM_WD, ADAM_STEP = 0.001, 0.9, 0.999, 1e-08, 0.01, 10

SUBLANES = 8
LANES = 128
VMEM_LIMIT = 56 * 1024 * 1024
ROW_TILE = 512
MM_SINGLE_K = 3072
D_IN_PAD = 2304
NEG = -1e30


def _cp(*sem):
    return pltpu.CompilerParams(dimension_semantics=sem if sem else None, vmem_limit_bytes=VMEM_LIMIT)


def _pick(dim, pref, align=LANES):
    if dim <= pref:
        return dim
    t = (pref // align) * align
    while t >= align:
        if dim % t == 0:
            return t
        t -= align
    return dim


def _gelu(x):
    return jax.nn.gelu(x)


def _gelu_grad(x):
    c = math.sqrt(2.0 / math.pi)
    t = jnp.tanh(c * (x + 0.044715 * x * x * x))
    return 0.5 * (1.0 + t) + 0.5 * x * (1.0 - t * t) * c * (1.0 + 3 * 0.044715 * x * x)


def _gelu_pair(x):
    c = math.sqrt(2.0 / math.pi)
    x2 = x * x
    t = jnp.tanh(c * x * (1.0 + 0.044715 * x2))
    return 0.5 * x * (1.0 + t), 0.5 * (1.0 + t) + 0.5 * x * (1.0 - t * t) * c * (1.0 + 3 * 0.044715 * x2)


def _sigmoid(x):
    return jax.nn.sigmoid(x)


def _expm1(x):
    p = 1.0 + x / 9.0
    for n in (8.0, 7.0, 6.0, 5.0, 4.0, 3.0, 2.0):
        p = 1.0 + (x / n) * p
    return jnp.where(jnp.abs(x) < 0.3, x * p, jnp.exp(x) - 1.0)


def _dot(a, b, dims):
    return lax.dot_general(a, b, (dims, ((), ())), preferred_element_type=f32)


NN = ((1,), (0,))
NT = ((1,), (1,))
TN = ((0,), (0,))


def _mm(a, b, mode, name, out_dtype=f32, tm=1024, tn=1024, tk=1024, add=None, a_win=None):
    if mode == "nn":
        (M, K), N = a.shape, b.shape[1]
    elif mode == "nt":
        (M, K), N = a.shape, b.shape[0]
    else:
        (K, M), N = a.shape, b.shape[1]
    win = 0
    if a_win is not None:
        win, w = a_win
        if mode == "tn":
            M, tm = w, w
        else:
            K = w
    single = mode != "tn" and K <= MM_SINGLE_K
    tm, tn = _pick(M, tm), _pick(N, tn)
    tk = K if single else _pick(K, tk)
    nk = K // tk
    dims = {"nn": NN, "nt": NT, "tn": TN}[mode]

    def body(a_ref, b_ref, *rest):
        prod = _dot(a_ref[...].astype(bf16), b_ref[...].astype(bf16), dims)
        if single:
            o_ref = rest[-1]
            o_ref[...] = (prod if add is None else prod + rest[0][...]).astype(o_ref.dtype)
            return
        o_ref, acc = rest[-2:]
        k = pl.program_id(2)

        @pl.when(k == 0)
        def _():
            acc[...] = prod if add is None else prod + rest[0][...]

        @pl.when(k > 0)
        def _():
            acc[...] += prod

        @pl.when(k == nk - 1)
        def _():
            o_ref[...] = acc[...].astype(o_ref.dtype)

    if mode == "tn":
        a_spec = pl.BlockSpec((tk, tm), lambda i, j, k: (k, i + win))
    else:
        a_spec = pl.BlockSpec((tm, tk), lambda i, j, k: (i, k + win))
    if mode == "nt":
        b_spec = pl.BlockSpec((tn, tk), lambda i, j, k: (j, k))
    else:
        b_spec = pl.BlockSpec((tk, tn), lambda i, j, k: (k, j))
    o_spec = pl.BlockSpec((tm, tn), lambda i, j, k: (i, j))
    return pl.pallas_call(
        body, name=name, grid=(M // tm, N // tn, nk),
        in_specs=[a_spec, b_spec] + ([] if add is None else [o_spec]), out_specs=o_spec,
        out_shape=jax.ShapeDtypeStruct((M, N), out_dtype),
        scratch_shapes=[] if single else [pltpu.VMEM((tm, tn), f32)],
        compiler_params=_cp("parallel", "parallel", "arbitrary"),
    )(*((a, b) if add is None else (a, b, add)))


def _shift_down(cur, prev8, k):
    if k == 0:
        return cur
    T, (R, C) = SUBLANES, cur.shape
    rot = pltpu.roll(cur.reshape(R // T, T, C), k, 1)
    before = jnp.concatenate([pltpu.roll(prev8, k, 0)[None], rot[:-1]], axis=0)
    row = lax.broadcasted_iota(jnp.int32, (R // T, T, C), 1)
    return jnp.where(row < k, before, rot).reshape(R, C)


def _shift_up(cur, next8, k):
    if k == 0:
        return cur
    T, (R, C) = SUBLANES, cur.shape
    rot = pltpu.roll(cur.reshape(R // T, T, C), T - k, 1)
    after = jnp.concatenate([rot[1:], pltpu.roll(next8, T - k, 0)[None]], axis=0)
    row = lax.broadcasted_iota(jnp.int32, (R // T, T, C), 1)
    return jnp.where(row < T - k, rot, after).reshape(R, C)


def _prev_halo_spec(rt, cols, ncolblk_fn):
    per = rt // SUBLANES
    return pl.BlockSpec((SUBLANES, cols), lambda *g: (jnp.maximum(g[-1] * per - 1, 0), ncolblk_fn(*g)))


def _ln_fwd(h, branch, g, b, name):
    S, D = h.shape
    rt = _pick(S, ROW_TILE, SUBLANES)

    def body(h_ref, m_ref, g_ref, b_ref, o_ref, z_ref):
        z = ALPHA * h_ref[...] + m_ref[...]
        mu = jnp.mean(z, axis=-1, keepdims=True)
        zc = z - mu
        var = jnp.mean(zc * zc, axis=-1, keepdims=True)
        o_ref[...] = zc * lax.rsqrt(var + LN_EPS) * g_ref[...] + b_ref[...]
        z_ref[...] = z

    row = pl.BlockSpec((rt, D), lambda i: (i, 0))
    vec = pl.BlockSpec((1, D), lambda i: (0, 0))
    return pl.pallas_call(
        body, name=name, grid=(S // rt,), in_specs=[row, row, vec, vec], out_specs=[row, row],
        out_shape=[jax.ShapeDtypeStruct((S, D), f32)] * 2, compiler_params=_cp("parallel"),
    )(h, branch, g.reshape(1, D), b.reshape(1, D))


def _ln_bwd(dy_a, dy_b, z, g, name):
    S, D = z.shape
    rt = _pick(S, ROW_TILE, SUBLANES)
    two = dy_a is not None

    def body(*refs):
        if two:
            a_ref, b_ref, z_ref, g_ref, dz_ref, acc_ref = refs
            dy = ALPHA * a_ref[...] + b_ref[...]
        else:
            b_ref, z_ref, g_ref, dz_ref, acc_ref = refs
            dy = b_ref[...]
        z = z_ref[...]
        mu = jnp.mean(z, axis=-1, keepdims=True)
        zc = z - mu
        var = jnp.mean(zc * zc, axis=-1, keepdims=True)
        rstd = lax.rsqrt(var + LN_EPS)
        xhat = zc * rstd
        dxh = dy * g_ref[...]
        m1 = jnp.mean(dxh, axis=-1, keepdims=True)
        m2 = jnp.mean(dxh * xhat, axis=-1, keepdims=True)
        dz_ref[...] = rstd * (dxh - m1 - xhat * m2)

        @pl.when(pl.program_id(0) == 0)
        def _():
            acc_ref[...] = jnp.zeros_like(acc_ref)

        acc_ref[0:1, :] += jnp.sum(dy * xhat, axis=0, keepdims=True)
        acc_ref[1:2, :] += jnp.sum(dy, axis=0, keepdims=True)

    row = pl.BlockSpec((rt, D), lambda i: (i, 0))
    vec = pl.BlockSpec((1, D), lambda i: (0, 0))
    acc = pl.BlockSpec((SUBLANES, D), lambda i: (0, 0))
    ins = ([dy_a] if two else []) + [dy_b, z, g.reshape(1, D)]
    return pl.pallas_call(
        body, name=name, grid=(S // rt,), in_specs=[row] * (len(ins) - 1) + [vec], out_specs=[row, acc],
        out_shape=[jax.ShapeDtypeStruct((S, D), f32), jax.ShapeDtypeStruct((SUBLANES, D), f32)],
        compiler_params=_cp("arbitrary"),
    )(*ins)


def _loss_head(y, target):
    S, D = y.shape
    rt = _pick(S, ROW_TILE, SUBLANES)

    def body(y_ref, t_ref, dy_ref, acc_ref):
        e = y_ref[...] - t_ref[...]
        dy_ref[...] = e * (1.0 / D)

        @pl.when(pl.program_id(0) == 0)
        def _():
            acc_ref[...] = jnp.zeros_like(acc_ref)

        part = jnp.sum(jnp.mean(e * e, axis=-1, keepdims=True), axis=0, keepdims=True)
        acc_ref[...] += 0.5 * part

    row = pl.BlockSpec((rt, D), lambda i: (i, 0))
    return pl.pallas_call(
        body, name="loss_head", grid=(S // rt,), in_specs=[row, row],
        out_specs=[row, pl.BlockSpec((1, 1), lambda i: (0, 0))],
        out_shape=[jax.ShapeDtypeStruct((S, D), f32), jax.ShapeDtypeStruct((1, 1), f32)],
        compiler_params=_cp("arbitrary"),
    )(y, target)


def _axpy(a, b, name):
    S, D = a.shape
    rt = _pick(S, ROW_TILE, SUBLANES)

    def body(a_ref, b_ref, o_ref):
        o_ref[...] = ALPHA * a_ref[...] + b_ref[...]

    row = pl.BlockSpec((rt, D), lambda i: (i, 0))
    return pl.pallas_call(
        body, name=name, grid=(S // rt,), in_specs=[row, row], out_specs=row,
        out_shape=jax.ShapeDtypeStruct((S, D), f32), compiler_params=_cp("parallel"),
    )(a, b)


def _rope_tables(S):
    rt = _pick(S, ROW_TILE, SUBLANES)

    def body(c_ref, s_ref):
        pos = (pl.program_id(0) * rt + lax.broadcasted_iota(jnp.int32, (rt, LANES), 0)).astype(f32)
        lane = lax.broadcasted_iota(jnp.int32, (rt, LANES), 1)
        j = (lane % (HEAD_DIM // 2)).astype(f32)
        inv = jnp.exp((-j * 2.0 / HEAD_DIM) * math.log(ROPE_THETA))
        ang = pos * inv
        c = jnp.cos(ang)
        s = jnp.where(lane % HEAD_DIM < HEAD_DIM // 2, -jnp.sin(ang), jnp.sin(ang))
        c_ref[...] = jnp.concatenate([c, c, c], axis=1)
        s_ref[...] = jnp.concatenate([s, s, s], axis=1)

    row = pl.BlockSpec((rt, ATTN_WIDTH), lambda i: (i, 0))
    return pl.pallas_call(
        body, name="rope_tables", grid=(S // rt,), in_specs=[], out_specs=[row, row],
        out_shape=[jax.ShapeDtypeStruct((S, ATTN_WIDTH), f32)] * 2, compiler_params=_cp("parallel"),
    )()


def _swap_halves(x):
    lane = lax.broadcasted_iota(jnp.int32, x.shape, 1)
    half = HEAD_DIM // 2
    return jnp.where(lane % HEAD_DIM < half, pltpu.roll(x, x.shape[1] - half, 1), pltpu.roll(x, half, 1))


def _rope_fwd(proj, cos, sin):
    S, W = proj.shape[0], ATTN_WIDTH
    rt = _pick(S, ROW_TILE, SUBLANES)

    def body(q_ref, k_ref, c_ref, s_ref, qo_ref, ko_ref):
        c, s = c_ref[...], s_ref[...]
        qo_ref[...] = q_ref[...] * c + _swap_halves(q_ref[...]) * s
        ko_ref[...] = k_ref[...] * c + _swap_halves(k_ref[...]) * s

    row = pl.BlockSpec((rt, W), lambda i: (i, 0))
    return pl.pallas_call(
        body, name="rope_fwd", grid=(S // rt,), in_specs=[row, pl.BlockSpec((rt, W), lambda i: (i, 1)), row, row],
        out_specs=[row, row], out_shape=[jax.ShapeDtypeStruct((S, W), f32)] * 2, compiler_params=_cp("parallel"),
    )(proj, proj, cos, sin)


def _rope_bwd(dq, dk, cos, sin):
    S, W = dq.shape
    rt = _pick(S, ROW_TILE, SUBLANES)

    def body(q_ref, k_ref, c_ref, s_ref, qo_ref, ko_ref):
        c, s = c_ref[...], s_ref[...]
        qo_ref[...] = q_ref[...] * c + _swap_halves(q_ref[...] * s)
        ko_ref[...] = k_ref[...] * c + _swap_halves(k_ref[...] * s)

    row = pl.BlockSpec((rt, W), lambda i: (i, 0))
    return pl.pallas_call(
        body, name="rope_bwd", grid=(S // rt,), in_specs=[row] * 4, out_specs=[row] * 2,
        out_shape=[jax.ShapeDtypeStruct((S, W), f32)] * 2, compiler_params=_cp("parallel"),
    )(dq, dk, cos, sin)


def _rows(ref, start, d):
    if d == 1:
        return ref[pl.ds(pl.multiple_of(start, ATTN_BLOCK), ATTN_BLOCK), :]
    return ref[pl.ds(start, ATTN_BLOCK, stride=d), :]


def _set_rows(ref, start, d, val):
    if d == 1:
        ref[pl.ds(pl.multiple_of(start, ATTN_BLOCK), ATTN_BLOCK), :] = val
    else:
        ref[pl.ds(start, ATTN_BLOCK, stride=d), :] = val


def _pair_spec(S, first_block):
    return pl.BlockSpec((S, LANES), lambda p: (0, p + first_block))


def _attn_fwd2(qr, kr, proj, shards=(), split=()):
    S = qr.shape[0]
    B = ATTN_BLOCK
    nb = S // B
    scale = HEAD_DIM ** -0.5

    gather = _Gather(shards, split)
    nt = gather.nt

    def body(*refs):
        q_ref, k_ref, v_ref = refs[:3]
        g_ins = refs[3:3 + nt]
        o_ref, l_ref = refs[3 + nt:5 + nt]
        g_outs = refs[5 + nt:5 + 2 * nt]
        m_s, l_s = refs[5 + 2 * nt:7 + 2 * nt]
        g_sems = refs[7 + 2 * nt:]
        if nt:
            @pl.when(pl.program_id(0) == 0)
            def _():
                gather.start(g_ins, g_outs, g_sems)

        qi = lax.broadcasted_iota(jnp.int32, (B, 2 * B), 0)
        ki = lax.broadcasted_iota(jnp.int32, (B, 2 * B), 1)
        dist = qi + B - ki
        band = (dist >= 0) & (dist <= B)
        for bi, d in enumerate(DILATIONS):
            bpc = nb // d

            def blk(b, carry, bi=bi, d=d, bpc=bpc):
                c, n = b // bpc, b % bpc
                start = c + d * B * n
                pstart = c + d * B * jnp.maximum(n - 1, 0)
                valid = band & ((ki >= B) | (n > 0))
                q = _rows(q_ref, start, d).astype(bf16)
                kcat = jnp.concatenate([_rows(k_ref, pstart, d), _rows(k_ref, start, d)], axis=0).astype(bf16)
                vcat = jnp.concatenate([_rows(v_ref, pstart, d), _rows(v_ref, start, d)], axis=0).astype(bf16)
                if bi > 0:
                    m_old, l_old, a_old = _rows(m_s, start, d), _rows(l_s, start, d), _rows(o_ref, start, d)
                ms, ls, accs = [], [], []
                for h in range(2):
                    sl = slice(h * HEAD_DIM, (h + 1) * HEAD_DIM)
                    c0 = h * HEAD_DIM
                    s = jnp.where(valid, _dot(q[:, sl], kcat[:, sl], NT) * scale, NEG)
                    m = jnp.max(s, axis=1, keepdims=True)
                    if bi > 0:
                        mo = m_old[:, c0:c0 + 1]
                        m = jnp.maximum(m, mo)
                        alpha = jnp.exp(mo - m)
                    p = jnp.exp(s - m)
                    l = jnp.sum(p, axis=1, keepdims=True)
                    acc = _dot(p.astype(bf16), vcat[:, sl], NN)
                    if bi > 0:
                        l = l + alpha * l_old[:, c0:c0 + 1]
                        acc = acc + alpha * a_old[:, sl]
                    ms.append(jnp.broadcast_to(m, (B, HEAD_DIM)))
                    ls.append(jnp.broadcast_to(l, (B, HEAD_DIM)))
                    accs.append(acc)
                _set_rows(m_s, start, d, jnp.concatenate(ms, axis=1))
                _set_rows(l_s, start, d, jnp.concatenate(ls, axis=1))
                _set_rows(o_ref, start, d, jnp.concatenate(accs, axis=1))
                return carry

            lax.fori_loop(0, nb, blk, 0, unroll=4)

        def fin(t, carry):
            rows = pl.ds(pl.multiple_of(t * B, B), B)
            l = l_s[rows, :]
            o_ref[rows, :] = o_ref[rows, :] / l
            l_ref[rows, :] = m_s[rows, :] + jnp.log(l)
            return carry

        lax.fori_loop(0, nb, fin, 0)
        if nt:
            @pl.when(pl.program_id(0) == pl.num_programs(0) - 1)
            def _():
                gather.finish(g_ins, g_outs, g_sems)

    pair = _pair_spec(S, 0)
    res = pl.pallas_call(
        body, name="attn_fwd_gather" if nt else "attn_fwd", grid=(3,),
        in_specs=[pair, pair, _pair_spec(S, 2 * ATTN_WIDTH // LANES)] + gather.in_specs,
        out_specs=[pair, pair] + gather.out_specs,
        out_shape=[jax.ShapeDtypeStruct((S, ATTN_WIDTH), f32)] * 2 + gather.out_shape,
        scratch_shapes=[pltpu.VMEM((S, LANES), f32)] * 2 + gather.scratch,
        compiler_params=_cp("arbitrary"),
    )(qr, kr, proj, *shards)
    return res[0], res[1], list(res[2:])


def _attn_bwd2(qr, kr, proj, dattn, ltot, delta, slabs=(), only_c=0, send=(), send_dst=0):
    S = qr.shape[0]
    B = ATTN_BLOCK
    nb = S // B
    scale = HEAD_DIM ** -0.5
    ex = _ChipExchange(slabs, (), only_c)
    snd = _SiblingSend(send, send_dst)
    n, ns = ex.n, snd.n
    hosted = n + ns

    def body(*refs):
        q_ref, k_ref, v_ref, do_ref, l_ref, d_ref = refs[:6]
        x_ins, s_ins = refs[6:6 + n], refs[6 + n:6 + hosted]
        dq_ref, dk_ref, dv_ref = refs[6 + hosted:9 + hosted]
        x_outs, s_outs = refs[9 + hosted:9 + hosted + n], refs[9 + hosted + n:9 + 2 * hosted]
        sems = refs[9 + 2 * hosted:]
        x_sems, s_sems = sems[:len(ex.scratch)], sems[len(ex.scratch):]
        if hosted:
            @pl.when(pl.program_id(0) == 0)
            def _():
                if n:
                    ex.start(x_ins, x_outs, x_sems)
                if ns:
                    snd.start(s_ins, s_outs, s_sems)

        qi = lax.broadcasted_iota(jnp.int32, (B, 2 * B), 0)
        ki = lax.broadcasted_iota(jnp.int32, (B, 2 * B), 1)
        dist1 = qi + B - ki
        band1 = (dist1 >= 0) & (dist1 <= B)
        ri = lax.broadcasted_iota(jnp.int32, (2 * B, B), 0)
        ci = lax.broadcasted_iota(jnp.int32, (2 * B, B), 1)
        dist2 = ri - ci
        band2 = (dist2 >= 0) & (dist2 <= B)
        for bi, d in enumerate(DILATIONS):
            bpc = nb // d

            def blk(b, carry, bi=bi, d=d, bpc=bpc):
                c, n = b // bpc, b % bpc
                start = c + d * B * n
                pstart = c + d * B * jnp.maximum(n - 1, 0)
                nstart = c + d * B * jnp.minimum(n + 1, bpc - 1)
                valid1 = band1 & ((ki >= B) | (n > 0))
                valid2 = band2 & ((ri < B) | (n + 1 < bpc))
                q_c, q_n = _rows(q_ref, start, d), _rows(q_ref, nstart, d)
                k_p, k_c = _rows(k_ref, pstart, d), _rows(k_ref, start, d)
                v_p, v_c = _rows(v_ref, pstart, d), _rows(v_ref, start, d)
                do_c, do_n = _rows(do_ref, start, d), _rows(do_ref, nstart, d)
                l_c, l_n = _rows(l_ref, start, d), _rows(l_ref, nstart, d)
                d_c, d_n = _rows(d_ref, start, d), _rows(d_ref, nstart, d)
                qc = q_c.astype(bf16)
                qcat = jnp.concatenate([q_c, q_n], axis=0).astype(bf16)
                kc = k_c.astype(bf16)
                kcat = jnp.concatenate([k_p, k_c], axis=0).astype(bf16)
                vc = v_c.astype(bf16)
                vcat = jnp.concatenate([v_p, v_c], axis=0).astype(bf16)
                doc = do_c.astype(bf16)
                docat = jnp.concatenate([do_c, do_n], axis=0).astype(bf16)
                lcat = jnp.concatenate([l_c, l_n], axis=0)
                dcat = jnp.concatenate([d_c, d_n], axis=0)
                dqs, dks, dvs = [], [], []
                for h in range(2):
                    sl = slice(h * HEAD_DIM, (h + 1) * HEAD_DIM)
                    c0 = h * HEAD_DIM
                    s1 = _dot(qc[:, sl], kcat[:, sl], NT) * scale
                    p1 = jnp.where(valid1, jnp.exp(s1 - l_c[:, c0:c0 + 1]), 0.0)
                    dp1 = _dot(doc[:, sl], vcat[:, sl], NT)
                    ds1 = p1 * (dp1 - d_c[:, c0:c0 + 1]) * scale
                    dqs.append(_dot(ds1.astype(bf16), kcat[:, sl], NN))
                    s2 = _dot(qcat[:, sl], kc[:, sl], NT) * scale
                    p2 = jnp.where(valid2, jnp.exp(s2 - lcat[:, c0:c0 + 1]), 0.0)
                    dvs.append(_dot(p2.astype(bf16), docat[:, sl], TN))
                    dp2 = _dot(docat[:, sl], vc[:, sl], NT)
                    ds2 = p2 * (dp2 - dcat[:, c0:c0 + 1]) * scale
                    dks.append(_dot(ds2.astype(bf16), qcat[:, sl], TN))
                for ref, parts in ((dq_ref, dqs), (dk_ref, dks), (dv_ref, dvs)):
                    new = jnp.concatenate(parts, axis=1)
                    if bi > 0:
                        new = new + _rows(ref, start, d)
                    _set_rows(ref, start, d, new)
                return carry

            lax.fori_loop(0, nb, blk, 0, unroll=4)

        if hosted:
            @pl.when(pl.program_id(0) == pl.num_programs(0) - 1)
            def _():
                if ns:
                    snd.finish(s_ins, s_outs, s_sems)
                if n:
                    ex.finish(x_ins, x_outs, x_sems)

    pair = _pair_spec(S, 0)
    res = pl.pallas_call(
        body, name="attn_bwd_exchange" if hosted else "attn_bwd", grid=(3,),
        in_specs=[pair, pair, _pair_spec(S, 2 * ATTN_WIDTH // LANES), pair, pair, pair] + ex.in_specs + snd.in_specs,
        out_specs=[pair] * 3 + ex.out_specs + snd.out_specs,
        out_shape=[jax.ShapeDtypeStruct((S, ATTN_WIDTH), f32)] * 3 + ex.out_shape + snd.out_shape,
        scratch_shapes=ex.scratch + snd.scratch, compiler_params=_cp("arbitrary"),
    )(qr, kr, proj, dattn, ltot, delta, *slabs, *send)
    return res[0], res[1], res[2], list(res[3:3 + n]), list(res[3 + n:])


def _softplus_neg(lam):
    return jnp.maximum(-lam, 0.0) + jnp.log1p(jnp.exp(-jnp.abs(lam)))


PROJ_LRU_X, PROJ_LRU_GATE, PROJ_S5_U = 3, 4, 5
EARLY_OWNER = 1


def _lru_pre(proj, conv_w, conv_b, wr, br, wi, bi, lam):
    S, W = proj.shape[0], LRU_WIDTH
    rt = _pick(S, ROW_TILE, SUBLANES)
    K = conv_w.shape[0]

    def body(x_ref, xp_ref, cw_ref, cb_ref, wr_ref, br_ref, wi_ref, bi_ref, lam_ref,
             xc_ref, r_ref, i_ref, la_ref, u_ref):
        prev = jnp.where(pl.program_id(0) == 0, 0.0, xp_ref[...])
        x = x_ref[...]
        xc = cb_ref[...] + cw_ref[K - 1:K, :] * x
        for k in range(K - 1):
            xc = xc + cw_ref[k:k + 1, :] * _shift_down(x, prev, K - 1 - k)
        xb = xc.astype(bf16)
        r = _sigmoid(_dot(xb, wr_ref[...], NN) + br_ref[...])
        i = _sigmoid(_dot(xb, wi_ref[...], NN) + bi_ref[...])
        log_a = -LRU_C * r * _softplus_neg(lam_ref[...])
        u = jnp.sqrt(-_expm1(2.0 * log_a)) * (i * xc)
        xc_ref[...], r_ref[...], i_ref[...], la_ref[...], u_ref[...] = xc, r, i, log_a, u

    row = pl.BlockSpec((rt, W), lambda i: (i, 0))
    xrow = pl.BlockSpec((rt, W), lambda i: (i, PROJ_LRU_X))
    halo = _prev_halo_spec(rt, W, lambda i: PROJ_LRU_X)
    vec = pl.BlockSpec((1, W), lambda i: (0, 0))
    return pl.pallas_call(
        body, name="lru_pre", grid=(S // rt,),
        in_specs=[xrow, halo, pl.BlockSpec((K, W), lambda i: (0, 0)), vec,
                  pl.BlockSpec((W, W), lambda i: (0, 0)), vec, pl.BlockSpec((W, W), lambda i: (0, 0)), vec, vec],
        out_specs=[row] * 5, out_shape=[jax.ShapeDtypeStruct((S, W), f32)] * 5, compiler_params=_cp("parallel"),
    )(proj, proj, conv_w, conv_b.reshape(1, W), wr, br.reshape(1, W), wi, bi.reshape(1, W), lam.reshape(1, W))


def _tile_rows(shape):
    return lax.broadcasted_iota(jnp.int32, shape, 0)


def _lru_scan(log_a, u, proj):
    S, W = u.shape
    rt = _pick(S, ROW_TILE, SUBLANES)
    T = SUBLANES

    def body(la_ref, u_ref, g_ref, h_ref, o_ref, carry):
        @pl.when(pl.program_id(0) == 0)
        def _():
            carry[...] = jnp.zeros_like(carry)

        row = _tile_rows((T, W))

        def step(t, hp):
            r0 = pl.multiple_of(t * T, T)
            a = jnp.exp(la_ref[pl.ds(r0, T), :])
            x = u_ref[pl.ds(r0, T), :]
            for k in (1, 2, 4):
                x = x + a * jnp.where(row >= k, pltpu.roll(x, k, 0), 0.0)
                a = a * jnp.where(row >= k, pltpu.roll(a, k, 0), 1.0)
            h = x + a * hp
            h_ref[pl.ds(r0, T), :] = h
            o_ref[pl.ds(r0, T), :] = h * _gelu(g_ref[pl.ds(r0, T), :])
            return h[T - 1:T, :]

        carry[0:1, :] = lax.fori_loop(0, rt // T, step, carry[0:1, :])

    row = pl.BlockSpec((rt, W), lambda i: (i, 0))
    grow = pl.BlockSpec((rt, W), lambda i: (i, PROJ_LRU_GATE))
    return pl.pallas_call(
        body, name="lru_scan", grid=(S // rt,), in_specs=[row, row, grow], out_specs=[row] * 2,
        out_shape=[jax.ShapeDtypeStruct((S, W), f32)] * 2, scratch_shapes=[pltpu.VMEM((T, W), f32)],
        compiler_params=_cp("arbitrary"),
    )(log_a, u, proj)


def _lru_scan_bwd(dlru, proj, h, log_a):
    S, W = h.shape
    rt = _pick(S, ROW_TILE, SUBLANES)
    T = SUBLANES
    nblk = S // rt

    def body(d_ref, g_ref, h_ref, la_ref, go_ref, dg_ref, carry):
        @pl.when(pl.program_id(0) == 0)
        def _():
            carry[...] = jnp.zeros_like(carry)

        row = _tile_rows((T, W))

        def step(j, c):
            gn, an = c
            t = rt // T - 1 - j
            r0 = pl.multiple_of(t * T, T)
            d = d_ref[pl.ds(r0, T), :]
            gate = g_ref[pl.ds(r0, T), :]
            a = jnp.exp(la_ref[pl.ds(r0, T), :])
            dg_ref[pl.ds(r0, T), :] = d * h_ref[pl.ds(r0, T), :] * _gelu_grad(gate)
            x = d * _gelu(gate)
            b = jnp.where(row < T - 1, pltpu.roll(a, T - 1, 0), an)
            for k in (1, 2, 4):
                x = x + b * jnp.where(row < T - k, pltpu.roll(x, T - k, 0), 0.0)
                b = b * jnp.where(row < T - k, pltpu.roll(b, T - k, 0), 1.0)
            g = x + b * gn
            go_ref[pl.ds(r0, T), :] = g
            return g[0:1, :], a[0:1, :]

        gn, an = lax.fori_loop(0, rt // T, step, (carry[0:1, :], carry[1:2, :]))
        carry[0:1, :] = gn
        carry[1:2, :] = an

    row = pl.BlockSpec((rt, W), lambda i: (nblk - 1 - i, 0))
    grow = pl.BlockSpec((rt, W), lambda i: (nblk - 1 - i, PROJ_LRU_GATE))
    return pl.pallas_call(
        body, name="lru_scan_bwd", grid=(nblk,), in_specs=[row, grow, row, row], out_specs=[row] * 2,
        out_shape=[jax.ShapeDtypeStruct((S, W), f32)] * 2, scratch_shapes=[pltpu.VMEM((T, W), f32)],
        compiler_params=_cp("arbitrary"),
    )(dlru, proj, h, log_a)


def _lru_gate_bwd(g, h, xc, r, i, log_a, wr, wi, lam):
    S, W = g.shape
    rt = _pick(S, ROW_TILE, SUBLANES)

    def body(g_ref, h_ref, hp_ref, xc_ref, r_ref, i_ref, la_ref, wr_ref, wi_ref, lam_ref,
             dxc_ref, dwr_ref, dwi_ref, acc_ref):
        @pl.when(pl.program_id(0) == 0)
        def _():
            dwr_ref[...] = jnp.zeros_like(dwr_ref)
            dwi_ref[...] = jnp.zeros_like(dwi_ref)
            acc_ref[...] = jnp.zeros_like(acc_ref)

        prev = jnp.where(pl.program_id(0) == 0, 0.0, hp_ref[...])
        gg, xc, r, i, log_a, lam = g_ref[...], xc_ref[...], r_ref[...], i_ref[...], la_ref[...], lam_ref[...]
        hm1 = _shift_down(h_ref[...], prev, 1)
        a = jnp.exp(log_a)
        s = jnp.sqrt(-_expm1(2.0 * log_a))
        da = gg * hm1
        di = gg * s * xc
        dxc = gg * s * i
        ds = gg * i * xc
        dlog_a = da * a - ds * (a * a / s)
        sp = _softplus_neg(lam)
        dr = dlog_a * (-LRU_C * sp)
        dsp = jnp.sum(dlog_a * (-LRU_C * r), axis=0, keepdims=True)
        dpr = dr * r * (1.0 - r)
        dpi = di * i * (1.0 - i)
        dprb, dpib, xb = dpr.astype(bf16), dpi.astype(bf16), xc.astype(bf16)
        dxc_ref[...] = dxc + _dot(dprb, wr_ref[...], NT) + _dot(dpib, wi_ref[...], NT)
        dwr_ref[...] += _dot(xb, dprb, TN)
        dwi_ref[...] += _dot(xb, dpib, TN)
        acc_ref[0:1, :] += jnp.sum(dpr, axis=0, keepdims=True)
        acc_ref[1:2, :] += jnp.sum(dpi, axis=0, keepdims=True)
        acc_ref[2:3, :] += dsp * (-_sigmoid(-lam))

    row = pl.BlockSpec((rt, W), lambda i: (i, 0))
    halo = _prev_halo_spec(rt, W, lambda i: 0)
    vec = pl.BlockSpec((1, W), lambda i: (0, 0))
    mat = pl.BlockSpec((W, W), lambda i: (0, 0))
    acc = pl.BlockSpec((SUBLANES, W), lambda i: (0, 0))
    return pl.pallas_call(
        body, name="lru_gate_bwd", grid=(S // rt,),
        in_specs=[row, row, halo, row, row, row, row, mat, mat, vec], out_specs=[row, mat, mat, acc],
        out_shape=[jax.ShapeDtypeStruct((S, W), f32), jax.ShapeDtypeStruct((W, W), f32),
                   jax.ShapeDtypeStruct((W, W), f32), jax.ShapeDtypeStruct((SUBLANES, W), f32)],
        compiler_params=_cp("arbitrary"),
    )(g, h, h, xc, r, i, log_a, wr, wi, lam.reshape(1, W))


def _conv_bwd(dy, x, conv_w, name, col_tile=None, out_dtype=f32, x_col_block=0, send=(), send_dst=0):
    if dy.ndim == 2:
        dy = dy[None]
    H, S, Ch = dy.shape
    C = H * Ch
    K = conv_w.shape[0]
    ct = Ch if col_tile is None else col_tile
    nct = Ch // ct
    rt = _pick(S, ROW_TILE, SUBLANES)
    nrt = S // rt
    snd = _SiblingSend(send, send_dst)
    n = snd.n

    def body(*refs):
        dy_ref, dyn_ref, x_ref, w_ref = refs[:4]
        s_ins = refs[4:4 + n]
        dx_ref, acc_ref = refs[4 + n:6 + n]
        s_outs, s_sems = refs[6 + n:6 + 2 * n], refs[6 + 2 * n:]
        i = pl.program_id(2)
        if n:
            @pl.when((pl.program_id(0) == 0) & (pl.program_id(1) == 0) & (i == 0))
            def _():
                snd.start(s_ins, s_outs, s_sems)

        @pl.when(i == 0)
        def _():
            acc_ref[...] = jnp.zeros_like(acc_ref)

        nxt = jnp.where(i == nrt - 1, 0.0, dyn_ref[...].astype(f32)[0:SUBLANES])
        dy, x = dy_ref[...].astype(f32), x_ref[...]
        ahead = [dy] + [_shift_up(dy, nxt, j) for j in range(1, K)]
        dx = w_ref[K - 1:K, :] * dy
        for k in range(K - 1):
            dx = dx + w_ref[k:k + 1, :] * ahead[K - 1 - k]
        dx_ref[...] = dx.astype(dx_ref.dtype)
        for k in range(K):
            acc_ref[k:k + 1, :] += jnp.sum(ahead[K - 1 - k] * x, axis=0, keepdims=True)
        acc_ref[K:K + 1, :] += jnp.sum(dy, axis=0, keepdims=True)
        if n:
            @pl.when((pl.program_id(0) == H - 1) & (pl.program_id(1) == nct - 1) & (i == nrt - 1))
            def _():
                snd.finish(s_ins, s_outs, s_sems)

    halo = SUBLANES * (4 // dy.dtype.itemsize)
    per, last = rt // halo, S // halo - 1
    dy_row = pl.BlockSpec((None, rt, ct), lambda h, j, i: (h, i, j))
    dy_next = pl.BlockSpec((None, halo, ct), lambda h, j, i: (h, jnp.minimum((i + 1) * per, last), j))
    row = pl.BlockSpec((rt, ct), lambda h, j, i: (i, h * nct + j))
    xrow = pl.BlockSpec((rt, ct), lambda h, j, i: (i, h * nct + j + x_col_block))
    res = pl.pallas_call(
        body, name=name, grid=(H, nct, nrt),
        in_specs=[dy_row, dy_next, xrow, pl.BlockSpec((K, ct), lambda h, j, i: (0, h * nct + j))] + snd.in_specs,
        out_specs=[row, pl.BlockSpec((SUBLANES, ct), lambda h, j, i: (0, h * nct + j))] + snd.out_specs,
        out_shape=[jax.ShapeDtypeStruct((S, C), out_dtype), jax.ShapeDtypeStruct((SUBLANES, C), f32)] + snd.out_shape,
        scratch_shapes=snd.scratch,
        compiler_params=_cp(*(("arbitrary",) * 3 if n else ("parallel", "parallel", "arbitrary"))),
    )(dy, dy, x, conv_w, *send)
    return res[0], res[1], list(res[2:])


def _s5_param_fn(a_re, a_im, ls, bt_re, bt_im):
    step = jnp.exp(ls)
    dt_re, dt_im = step * a_re, step * a_im
    mag = jnp.exp(dt_re)
    ab_re, ab_im = mag * jnp.cos(dt_im), mag * jnp.sin(dt_im)
    z_re, z_im = ab_re - 1.0, ab_im
    den = a_re * a_re + a_im * a_im
    f_re = (z_re * a_re + z_im * a_im) / den
    f_im = (z_im * a_re - z_re * a_im) / den
    bb_re = f_re[:, None, :] * bt_re - f_im[:, None, :] * bt_im
    bb_im = f_re[:, None, :] * bt_im + f_im[:, None, :] * bt_re
    return ab_re, ab_im, bb_re, bb_im


def _s5_params(a_re, a_im, ls, bt_re, bt_im):
    def body(ar, ai, l, br, bi, o_ar, o_ai, o_br, o_bi):
        o_ar[...], o_ai[...], o_br[...], o_bi[...] = _s5_param_fn(ar[...], ai[...], l[...], br[...], bi[...])

    return pl.pallas_call(
        body, name="s5_params",
        out_shape=[jax.ShapeDtypeStruct(a_re.shape, f32)] * 2 + [jax.ShapeDtypeStruct(bt_re.shape, f32)] * 2,
        compiler_params=_cp(),
    )(a_re, a_im, ls, bt_re, bt_im)


def _s5_params_bwd(a_re, a_im, ls, bt_re, bt_im, d_ar, d_ai, d_br, d_bi):
    def body(ar, ai, l, br, bi, c_ar, c_ai, c_br, c_bi, g_ar, g_ai, g_l, g_br, g_bi):
        _, vjp = jax.vjp(_s5_param_fn, ar[...], ai[...], l[...], br[...], bi[...])
        g_ar[...], g_ai[...], g_l[...], g_br[...], g_bi[...] = vjp((c_ar[...], c_ai[...], c_br[...], c_bi[...]))

    return pl.pallas_call(
        body, name="s5_params_bwd",
        out_shape=[jax.ShapeDtypeStruct(a_re.shape, f32)] * 2 + [jax.ShapeDtypeStruct(ls.shape, f32)]
        + [jax.ShapeDtypeStruct(bt_re.shape, f32)] * 2,
        compiler_params=_cp(),
    )(a_re, a_im, ls, bt_re, bt_im, d_ar, d_ai, d_br, d_bi)


S5_CHUNK = 256


def _s5_power_tables(ab_ref, p_ref, w_ref, conj):
    T, L = SUBLANES, S5_LANES
    are = ab_ref[0:1, 0:L]
    aim = ab_ref[0:1, L:2 * L]
    if conj:
        aim = -aim
    pre, pim = are, aim
    for n in range(3):
        p_ref[n:n + 1, 0:L] = pre
        p_ref[n:n + 1, L:2 * L] = pim
        pre, pim = pre * pre - pim * pim, 2.0 * pre * pim
    row = _tile_rows((T, L))
    wre = jnp.zeros((T, L), f32)
    wim = jnp.zeros((T, L), f32)
    pre, pim = are, aim
    for n in range(T):
        tgt = (T - 1 - n) if conj else n
        wre = jnp.where(row == tgt, pre, wre)
        wim = jnp.where(row == tgt, pim, wim)
        pre, pim = pre * are - pim * aim, pre * aim + pim * are
    w_ref[:, 0:L] = wre
    w_ref[:, L:2 * L] = wim


def _s5_scan(bu, ab):
    S, L2 = bu.shape
    L = L2 // 2
    rt = _pick(S, 256, SUBLANES)
    T = SUBLANES
    CH = S5_CHUNK

    def body(bu_ref, ab_ref, x_ref, p_ref, w_ref, carry):
        @pl.when(pl.program_id(0) == 0)
        def _():
            carry[...] = jnp.zeros_like(carry)
            _s5_power_tables(ab_ref, p_ref, w_ref, conj=False)

        row = _tile_rows((T, CH))

        def step(t, _):
            r0 = pl.multiple_of(t * T, T)
            for c in range(L // CH):
                lre, lim = pl.ds(c * CH, CH), pl.ds(L + c * CH, CH)
                xr, xi = bu_ref[pl.ds(r0, T), lre], bu_ref[pl.ds(r0, T), lim]
                for n, k in enumerate((1, 2, 4)):
                    pr, pi = p_ref[n:n + 1, lre], p_ref[n:n + 1, lim]
                    sr = jnp.where(row >= k, pltpu.roll(xr, k, 0), 0.0)
                    si = jnp.where(row >= k, pltpu.roll(xi, k, 0), 0.0)
                    xr, xi = xr + pr * sr - pi * si, xi + pr * si + pi * sr
                cr, ci = carry[T - 1:T, lre], carry[T - 1:T, lim]
                wr, wi = w_ref[:, lre], w_ref[:, lim]
                xr, xi = xr + wr * cr - wi * ci, xi + wr * ci + wi * cr
                carry[:, lre] = xr
                carry[:, lim] = xi
                x_ref[pl.ds(r0, T), lre] = xr
                x_ref[pl.ds(r0, T), lim] = xi
            return 0

        lax.fori_loop(0, rt // T, step, 0)

    row_spec = pl.BlockSpec((rt, L2), lambda i: (i, 0))
    return pl.pallas_call(
        body, name="s5_scan", grid=(S // rt,), in_specs=[row_spec, pl.BlockSpec((1, L2), lambda i: (0, 0))],
        out_specs=row_spec, out_shape=jax.ShapeDtypeStruct((S, L2), f32),
        scratch_shapes=[pltpu.VMEM((T, L2), f32), pltpu.VMEM((T, L2), f32), pltpu.VMEM((T, L2), f32)],
        compiler_params=_cp("arbitrary"),
    )(bu, ab)


def _s5_scan_bwd(dx, x, ab):
    S, L2 = dx.shape
    L = L2 // 2
    rt = _pick(S, 256, SUBLANES)
    T = SUBLANES
    CH = S5_CHUNK
    nblk = S // rt
    per = rt // T

    def body(dx_ref, x_ref, xp_ref, ab_ref, g_ref, da_ref, p_ref, w_ref, carry, acc):
        pid = pl.program_id(0)

        @pl.when(pid == 0)
        def _():
            carry[...] = jnp.zeros_like(carry)
            acc[...] = jnp.zeros_like(acc)
            _s5_power_tables(ab_ref, p_ref, w_ref, conj=True)

        row = _tile_rows((T, CH))
        first_block = pid == nblk - 1

        def step(j, _):
            t = per - 1 - j
            r0 = pl.multiple_of(t * T, T)
            rp = pl.multiple_of(jnp.maximum(t - 1, 0) * T, T)
            for c in range(L // CH):
                lre, lim = pl.ds(c * CH, CH), pl.ds(L + c * CH, CH)
                gr, gi = dx_ref[pl.ds(r0, T), lre], dx_ref[pl.ds(r0, T), lim]
                for n, k in enumerate((1, 2, 4)):
                    pr, pi = p_ref[n:n + 1, lre], p_ref[n:n + 1, lim]
                    sr = jnp.where(row < T - k, pltpu.roll(gr, T - k, 0), 0.0)
                    si = jnp.where(row < T - k, pltpu.roll(gi, T - k, 0), 0.0)
                    gr, gi = gr + pr * sr - pi * si, gi + pr * si + pi * sr
                cr, ci = carry[0:1, lre], carry[0:1, lim]
                wr, wi = w_ref[:, lre], w_ref[:, lim]
                gr, gi = gr + wr * cr - wi * ci, gi + wr * ci + wi * cr
                carry[:, lre] = gr
                carry[:, lim] = gi
                g_ref[pl.ds(r0, T), lre] = gr
                g_ref[pl.ds(r0, T), lim] = gi
                xr, xi = x_ref[pl.ds(r0, T), lre], x_ref[pl.ds(r0, T), lim]
                in_blk_r, in_blk_i = x_ref[pl.ds(rp, T), lre], x_ref[pl.ds(rp, T), lim]
                hal_r = jnp.where(first_block, 0.0, xp_ref[:, lre])
                hal_i = jnp.where(first_block, 0.0, xp_ref[:, lim])
                pvr = jnp.where(t == 0, hal_r, in_blk_r)[T - 1:T, :]
                pvi = jnp.where(t == 0, hal_i, in_blk_i)[T - 1:T, :]
                sxr = jnp.where(row >= 1, pltpu.roll(xr, 1, 0), pvr)
                sxi = jnp.where(row >= 1, pltpu.roll(xi, 1, 0), pvi)
                acc[:, lre] += gr * sxr + gi * sxi
                acc[:, lim] += gi * sxr - gr * sxi
            return 0

        lax.fori_loop(0, per, step, 0)

        @pl.when(pid == nblk - 1)
        def _():
            da_ref[...] = jnp.sum(acc[...], axis=0, keepdims=True)

    row_spec = pl.BlockSpec((rt, L2), lambda i: (nblk - 1 - i, 0))
    halo = pl.BlockSpec((T, L2), lambda i: (jnp.maximum((nblk - 1 - i) * per - 1, 0), 0))
    vec = pl.BlockSpec((1, L2), lambda i: (0, 0))
    return pl.pallas_call(
        body, name="s5_scan_bwd", grid=(nblk,), in_specs=[row_spec, row_spec, halo, vec],
        out_specs=[row_spec, vec],
        out_shape=[jax.ShapeDtypeStruct((S, L2), f32), jax.ShapeDtypeStruct((1, L2), f32)],
        scratch_shapes=[pltpu.VMEM((T, L2), f32)] * 4,
        compiler_params=_cp("arbitrary"),
    )(dx, x, x, ab)


def _S5_U_SPEC(rt):
    return pl.BlockSpec((rt, LRU_WIDTH), lambda i: (i, PROJ_S5_U))


def _s5_out(yc, proj, d, wglu, bglu):
    S, W = yc.shape
    rt = _pick(S, ROW_TILE, SUBLANES)

    def body(yc_ref, u_ref, d_ref, w_ref, b_ref, o_ref, y_ref):
        y = yc_ref[...] + d_ref[...] * u_ref[:, 0:W]
        yg = _gelu(y)
        z = _dot(yg.astype(bf16), w_ref[...], NN) + b_ref[...]
        o_ref[...] = yg * _sigmoid(z)
        y_ref[...] = y

    row = pl.BlockSpec((rt, W), lambda i: (i, 0))
    vec = pl.BlockSpec((1, W), lambda i: (0, 0))
    mat = pl.BlockSpec((W, W), lambda i: (0, 0))
    return pl.pallas_call(
        body, name="s5_out", grid=(S // rt,), in_specs=[row, _S5_U_SPEC(rt), vec, mat, vec], out_specs=[row, row],
        out_shape=[jax.ShapeDtypeStruct((S, W), f32)] * 2, compiler_params=_cp("parallel"),
    )(yc, proj, d.reshape(1, W), wglu, bglu.reshape(1, W))


def _s5_out_bwd(dssm, y, proj, d, wglu, bglu):
    S, W = y.shape
    rt = _pick(S, ROW_TILE, SUBLANES)

    def body(do_ref, y_ref, u_ref, d_ref, w_ref, b_ref, dy_ref, du_ref, dw_ref, acc_ref):
        @pl.when(pl.program_id(0) == 0)
        def _():
            dw_ref[...] = jnp.zeros_like(dw_ref)
            acc_ref[...] = jnp.zeros_like(acc_ref)

        do, y = do_ref[...], y_ref[...]
        yg = _gelu(y)
        ygb = yg.astype(bf16)
        sg = _sigmoid(_dot(ygb, w_ref[...], NN) + b_ref[...])
        dz = do * yg * sg * (1.0 - sg)
        dzb = dz.astype(bf16)
        dyg = do * sg + _dot(dzb, w_ref[...], NT)
        dy = dyg * _gelu_grad(y)
        dy_ref[...] = dy
        du_ref[...] = dy * d_ref[...]
        dw_ref[...] += _dot(ygb, dzb, TN)
        acc_ref[0:1, :] += jnp.sum(dz, axis=0, keepdims=True)
        acc_ref[1:2, :] += jnp.sum(dy * u_ref[:, 0:W], axis=0, keepdims=True)

    row = pl.BlockSpec((rt, W), lambda i: (i, 0))
    vec = pl.BlockSpec((1, W), lambda i: (0, 0))
    mat = pl.BlockSpec((W, W), lambda i: (0, 0))
    acc = pl.BlockSpec((SUBLANES, W), lambda i: (0, 0))
    return pl.pallas_call(
        body, name="s5_out_bwd", grid=(S // rt,), in_specs=[row, row, _S5_U_SPEC(rt), vec, mat, vec],
        out_specs=[row, row, mat, acc],
        out_shape=[jax.ShapeDtypeStruct((S, W), f32)] * 2
        + [jax.ShapeDtypeStruct((W, W), f32), jax.ShapeDtypeStruct((SUBLANES, W), f32)],
        compiler_params=_cp("arbitrary"),
    )(dssm, y, proj, d.reshape(1, W), wglu, bglu.reshape(1, W))


MIX_SPLITS = ((0, ATTN_WIDTH), (ATTN_WIDTH, ATTN_WIDTH + LRU_WIDTH), (ATTN_WIDTH + LRU_WIDTH, D_MODEL))


def _mixnorm(attn, lru, ssm, g):
    S = attn.shape[0]
    rt = _pick(S, ROW_TILE, SUBLANES)

    def body(a_ref, l_ref, s_ref, g_ref, o_ref):
        for ref, (lo, hi) in zip((a_ref, l_ref, s_ref), MIX_SPLITS):
            x = ref[...]
            ms = jnp.mean(x * x, axis=-1, keepdims=True)
            o_ref[:, lo:hi] = (x * lax.rsqrt(ms + RMS_EPS) * g_ref[:, lo:hi]).astype(o_ref.dtype)

    rows = [pl.BlockSpec((rt, hi - lo), lambda i: (i, 0)) for lo, hi in MIX_SPLITS]
    return pl.pallas_call(
        body, name="mixnorm", grid=(S // rt,), in_specs=rows + [pl.BlockSpec((1, D_MODEL), lambda i: (0, 0))],
        out_specs=pl.BlockSpec((rt, D_MODEL), lambda i: (i, 0)),
        out_shape=jax.ShapeDtypeStruct((S, D_MODEL), bf16), compiler_params=_cp("parallel"),
    )(attn, lru, ssm, g.reshape(1, D_MODEL))


def _mixnorm_bwd(dmixed, attn, lru, ssm, g):
    S = attn.shape[0]
    rt = _pick(S, ROW_TILE, SUBLANES)

    def body(d_ref, a_ref, l_ref, s_ref, g_ref, da_ref, dl_ref, ds_ref, dlt_ref, acc_ref):
        @pl.when(pl.program_id(0) == 0)
        def _():
            acc_ref[...] = jnp.zeros_like(acc_ref)

        outs = []
        for ref, (lo, hi) in zip((a_ref, l_ref, s_ref), MIX_SPLITS):
            x = ref[...]
            dy = d_ref[:, lo:hi]
            rinv = lax.rsqrt(jnp.mean(x * x, axis=-1, keepdims=True) + RMS_EPS)
            dyg = dy * g_ref[:, lo:hi]
            outs.append(rinv * dyg - x * (rinv * rinv * rinv) * jnp.mean(dyg * x, axis=-1, keepdims=True))
            acc_ref[0:1, lo:hi] += jnp.sum(dy * x * rinv, axis=0, keepdims=True)
        da_ref[...], dl_ref[...], ds_ref[...] = outs
        hi_ = lax.broadcasted_iota(jnp.int32, (ATTN_WIDTH, ATTN_WIDTH), 0) // HEAD_DIM
        hj_ = lax.broadcasted_iota(jnp.int32, (ATTN_WIDTH, ATTN_WIDTH), 1) // HEAD_DIM
        same = jnp.where(hi_ == hj_, 1.0, 0.0).astype(f32)
        dlt_ref[...] = jnp.dot(outs[0] * a_ref[...], same, precision=lax.Precision.HIGHEST, preferred_element_type=f32)

    rows = [pl.BlockSpec((rt, hi - lo), lambda i: (i, 0)) for lo, hi in MIX_SPLITS]
    full = pl.BlockSpec((rt, D_MODEL), lambda i: (i, 0))
    return pl.pallas_call(
        body, name="mixnorm_bwd", grid=(S // rt,),
        in_specs=[full] + rows + [pl.BlockSpec((1, D_MODEL), lambda i: (0, 0))],
        out_specs=rows + [rows[0], pl.BlockSpec((SUBLANES, D_MODEL), lambda i: (0, 0))],
        out_shape=[jax.ShapeDtypeStruct((S, hi - lo), f32) for lo, hi in MIX_SPLITS]
        + [jax.ShapeDtypeStruct((S, ATTN_WIDTH), f32), jax.ShapeDtypeStruct((SUBLANES, D_MODEL), f32)],
        compiler_params=_cp("arbitrary"),
    )(dmixed, attn, lru, ssm, g.reshape(1, D_MODEL))


FFN_COL_TILE = 1536


def _ffn_conv(x, prev, w_ref, b_ref, K):
    y = b_ref[...] + w_ref[K - 1:K, :] * x
    for k in range(K - 1):
        y = y + w_ref[k:k + 1, :] * _shift_down(x, prev, K - 1 - k)
    return y


def _ffn_act(up, conv_w, conv_b):
    S, C2 = up.shape
    C = C2 // 2
    K = conv_w.shape[0]
    ct = FFN_COL_TILE
    nct = C // ct
    rt = _pick(S, ROW_TILE, SUBLANES)

    def body(g_ref, gp_ref, v_ref, vp_ref, wg_ref, wv_ref, bg_ref, bv_ref, o_ref):
        first = pl.program_id(1) == 0
        gate = _ffn_conv(g_ref[...], jnp.where(first, 0.0, gp_ref[...]), wg_ref, bg_ref, K)
        val = _ffn_conv(v_ref[...], jnp.where(first, 0.0, vp_ref[...]), wv_ref, bv_ref, K)
        o_ref[...] = (_gelu(gate) * val).astype(o_ref.dtype)

    def specs(off):
        return (pl.BlockSpec((rt, ct), lambda j, i: (i, j + off)), _prev_halo_spec(rt, ct, lambda j, i: j + off))

    def wspec(off, rows):
        return pl.BlockSpec((rows, ct), lambda j, i: (0, j + off))

    g_s, gp_s = specs(0)
    v_s, vp_s = specs(nct)
    return pl.pallas_call(
        body, name="ffn_act", grid=(nct, S // rt),
        in_specs=[g_s, gp_s, v_s, vp_s, wspec(0, K), wspec(nct, K), wspec(0, 1), wspec(nct, 1)],
        out_specs=pl.BlockSpec((rt, ct), lambda j, i: (i, j)),
        out_shape=jax.ShapeDtypeStruct((S, C), bf16), compiler_params=_cp("parallel", "parallel"),
    )(up, up, up, up, conv_w, conv_w, conv_b.reshape(1, C2), conv_b.reshape(1, C2))


def _ffn_act_bwd(dact, up, conv_w, conv_b):
    S, C2 = up.shape
    C = C2 // 2
    K = conv_w.shape[0]
    ct = FFN_COL_TILE
    nct = C // ct
    rt = _pick(S, ROW_TILE, SUBLANES)

    def body(d_ref, g_ref, gp_ref, v_ref, vp_ref, wg_ref, wv_ref, bg_ref, bv_ref, o_ref):
        first = pl.program_id(1) == 0
        gate = _ffn_conv(g_ref[...], jnp.where(first, 0.0, gp_ref[...]), wg_ref, bg_ref, K)
        val = _ffn_conv(v_ref[...], jnp.where(first, 0.0, vp_ref[...]), wv_ref, bv_ref, K)
        d = d_ref[...].astype(f32)
        gl, dgl = _gelu_pair(gate)
        o_ref[0] = (d * val * dgl).astype(o_ref.dtype)
        o_ref[1] = (d * gl).astype(o_ref.dtype)

    def specs(off):
        return (pl.BlockSpec((rt, ct), lambda j, i: (i, j + off)), _prev_halo_spec(rt, ct, lambda j, i: j + off))

    def wspec(off, rows):
        return pl.BlockSpec((rows, ct), lambda j, i: (0, j + off))

    g_s, gp_s = specs(0)
    v_s, vp_s = specs(nct)
    return pl.pallas_call(
        body, name="ffn_act_bwd", grid=(nct, S // rt),
        in_specs=[pl.BlockSpec((rt, ct), lambda j, i: (i, j)), g_s, gp_s, v_s, vp_s,
                  wspec(0, K), wspec(nct, K), wspec(0, 1), wspec(nct, 1)],
        out_specs=pl.BlockSpec((2, rt, ct), lambda j, i: (0, i, j)),
        out_shape=jax.ShapeDtypeStruct((2, S, C), bf16), compiler_params=_cp("parallel", "parallel"),
    )(dact, up, up, up, up, conv_w, conv_w, conv_b.reshape(1, C2), conv_b.reshape(1, C2))


ANY = pl.BlockSpec(memory_space=pl.ANY)


def _rows_for(cols):
    return max(16, (1 << 19) // cols)


def _chips(x, y):
    return [(1 - x, y), (x, 1 - y), (1 - x, 1 - y)]


class _Gather:
    def __init__(self, shards, split):
        self.shapes = [s.shape for s in shards]
        self.dtypes = [s.dtype for s in shards]
        self.split = list(split)
        self.nt = len(shards)
        self.in_specs = [ANY] * self.nt
        self.out_specs = [ANY] * self.nt
        self.out_shape = [jax.ShapeDtypeStruct((4,) + s, dt) for s, dt in zip(self.shapes, self.dtypes)]
        self.scratch = [pltpu.SemaphoreType.DMA((3, self.nt))] * 4 if self.nt else []

    def _part(self, ref, t, half):
        if not self.split[t]:
            return ref
        r = self.shapes[t][0] // 2
        return ref.at[pl.ds(half * r, r), :]

    def _ici(self, ins, outs, sems, k, t, chip, landing_chip):
        x, y, c = lax.axis_index("x"), lax.axis_index("y"), lax.axis_index("c")
        return pltpu.make_async_remote_copy(
            src_ref=self._part(ins[t], t, c), dst_ref=self._part(outs[t].at[landing_chip], t, c),
            send_sem=sems[0].at[k, t], recv_sem=sems[1].at[k, t], device_id=(chip[0], chip[1], c), device_id_type=MESH)

    def _d2d(self, outs, sems, k, t, q, half):
        x, y, c = lax.axis_index("x"), lax.axis_index("y"), lax.axis_index("c")
        rows = self._part(outs[t].at[q], t, half)
        return pltpu.make_async_remote_copy(
            src_ref=rows, dst_ref=rows, send_sem=sems[2].at[k, t], recv_sem=sems[3].at[k, t],
            device_id=(x, y, 1 - c), device_id_type=MESH)

    def start(self, ins, outs, sems):
        x, y = lax.axis_index("x"), lax.axis_index("y")
        me = 2 * x + y
        for k, chip in enumerate(_chips(x, y)):
            for t in range(self.nt):
                self._ici(ins, outs, sems, k, t, chip, me).start()

    def finish(self, ins, outs, sems):
        x, y, c = lax.axis_index("x"), lax.axis_index("y"), lax.axis_index("c")
        me = 2 * x + y
        chips = _chips(x, y)
        for k, chip in enumerate(chips):
            q = 2 * chip[0] + chip[1]
            for t in range(self.nt):
                self._ici(ins, outs, sems, k, t, chip, q).wait_recv()
                if self.split[t]:
                    self._d2d(outs, sems, k, t, q, c).start()
        for k, chip in enumerate(chips):
            q = 2 * chip[0] + chip[1]
            for t in range(self.nt):
                if self.split[t]:
                    self._d2d(outs, sems, k, t, q, 1 - c).wait_recv()
        for k, chip in enumerate(chips):
            q = 2 * chip[0] + chip[1]
            for t in range(self.nt):
                self._ici(ins, outs, sems, k, t, chip, me).wait_send()
                if self.split[t]:
                    self._d2d(outs, sems, k, t, q, c).wait_send()


def _gather_weights(shards, split):
    g = _Gather(shards, split)
    nt = g.nt

    def body(*refs):
        ins, outs, sems = refs[:nt], refs[nt:2 * nt], refs[2 * nt:]
        g.start(ins, outs, sems)
        g.finish(ins, outs, sems)

    return pl.pallas_call(
        body, name="gather_weights", in_specs=g.in_specs, out_specs=g.out_specs, out_shape=g.out_shape,
        scratch_shapes=g.scratch,
    )(*shards)


class _SiblingSend:
    def __init__(self, gs, dst_c, swap=()):
        self.nt, self.n = len(gs), len(gs) + len(swap)
        self.dst_c = list(dst_c) if isinstance(dst_c, (list, tuple)) else [dst_c] * self.nt
        self.in_specs = [ANY] * self.n
        self.out_specs = [ANY] * self.n
        self.out_shape = [jax.ShapeDtypeStruct(g.shape, g.dtype) for g in list(gs) + list(swap)]
        self.scratch = [pltpu.SemaphoreType.DMA((self.n,))] * 2 if self.n else []

    def _each(self, ins, outs, sems, sender, fn):
        x, y, c = lax.axis_index("x"), lax.axis_index("y"), lax.axis_index("c")

        def cp(t):
            return pltpu.make_async_remote_copy(
                src_ref=ins[t], dst_ref=outs[t], send_sem=sems[0].at[t], recv_sem=sems[1].at[t],
                device_id=(x, y, 1 - c), device_id_type=MESH)

        for dst in (0, 1):
            which = [t for t in range(self.nt) if self.dst_c[t] == dst]
            if which:
                @pl.when((c != dst) if sender else (c == dst))
                def _(which=which):
                    for t in which:
                        fn(cp(t))
        for t in range(self.nt, self.n):
            fn(cp(t))

    def start(self, ins, outs, sems):
        self._each(ins, outs, sems, True, lambda cp: cp.start())

    def finish(self, ins, outs, sems):
        self._each(ins, outs, sems, False, lambda cp: cp.wait_recv())
        self._each(ins, outs, sems, True, lambda cp: cp.wait_send())


def _sibling_send(gs, dst_c, swap=()):
    snd = _SiblingSend(gs, dst_c, swap)
    n = snd.n

    def body(*refs):
        ins, outs, sems = refs[:n], refs[n:2 * n], refs[2 * n:]
        snd.start(ins, outs, sems)
        snd.finish(ins, outs, sems)

    res = pl.pallas_call(
        body, name="sibling_send", in_specs=snd.in_specs, out_specs=snd.out_specs, out_shape=snd.out_shape,
        scratch_shapes=snd.scratch,
    )(*gs, *swap)
    return list(res[:snd.nt]), list(res[snd.nt:])


def _owner_flag(owner_c):
    return (lax.axis_index("c") == owner_c).astype(jnp.int32).reshape(1)


def _pair_sum(g, other, name, owner_c, col_slabs=False):
    R, C = g.shape
    cb = C // 4 if col_slabs else C
    rt = _pick(R, _rows_for(cb), 16)

    def body(on_ref, a_ref, o_ref, out_ref):
        out_ref[...] = (a_ref[...] + o_ref[...]).astype(out_ref.dtype)

    row = pl.BlockSpec((rt, cb), lambda q, i, on: (i * on[0], q * on[0]))
    if col_slabs:
        out_spec = pl.BlockSpec((None, rt, cb), lambda q, i, on: (q * on[0], i * on[0], 0))
        out_shape = jax.ShapeDtypeStruct((4, R, cb), bf16)
    else:
        out_spec, out_shape = row, jax.ShapeDtypeStruct((R, C), bf16)
    return pl.pallas_call(
        body, name=name,
        grid_spec=pltpu.PrefetchScalarGridSpec(num_scalar_prefetch=1, grid=(C // cb, R // rt),
                                               in_specs=[row, row], out_specs=out_spec),
        out_shape=out_shape, compiler_params=_cp("arbitrary", "arbitrary"),
    )(_owner_flag(owner_c), g, other)


class _ChipExchange:
    NORTH = 1

    def __init__(self, slabs, whole, only_c):
        self.ns, self.nw = len(slabs), len(whole)
        self.only_c = list(only_c) if isinstance(only_c, (list, tuple)) else [only_c] * self.ns
        self.only_c += [self.NORTH] * self.nw
        self.n = self.ns + self.nw
        self.in_specs = [ANY] * self.n
        self.out_specs = [ANY] * self.n
        self.out_shape = ([jax.ShapeDtypeStruct(s.shape, s.dtype) for s in slabs]
                          + [jax.ShapeDtypeStruct((4,) + w.shape, w.dtype) for w in whole])
        self.scratch = [pltpu.SemaphoreType.DMA((3, self.n))] * 2 if self.n else []
        self.scratch += [pltpu.SemaphoreType.DMA((3, self.nw))] * 2 if self.nw else []

    def _copy(self, ins, outs, sems, k, t, chip, landing_chip):
        c = lax.axis_index("c")
        src = ins[t].at[2 * chip[0] + chip[1]] if t < self.ns else ins[t]
        return pltpu.make_async_remote_copy(
            src_ref=src, dst_ref=outs[t].at[landing_chip], send_sem=sems[0].at[k, t], recv_sem=sems[1].at[k, t],
            device_id=(chip[0], chip[1], c), device_id_type=MESH)

    def _pass_on(self, outs, sems, k, t, chip):
        x, y, c = lax.axis_index("x"), lax.axis_index("y"), lax.axis_index("c")
        rows = outs[t].at[2 * chip[0] + chip[1]]
        return pltpu.make_async_remote_copy(
            src_ref=rows, dst_ref=rows, send_sem=sems[2].at[k, t - self.ns], recv_sem=sems[3].at[k, t - self.ns],
            device_id=(x, y, 1 - c), device_id_type=MESH)

    def _each(self, fn, north=True):
        x, y, c = lax.axis_index("x"), lax.axis_index("y"), lax.axis_index("c")
        chips = _chips(x, y)
        groups = {}
        if north:
            for t in range(self.n):
                groups.setdefault(self.only_c[t], []).append(t)
        elif self.nw:
            groups[1 - self.NORTH] = list(range(self.ns, self.n))
        for owner, which in groups.items():
            @pl.when(c == owner)
            def _(which=which):
                for k, chip in enumerate(chips):
                    for t in which:
                        fn(k, t, chip)

    def start(self, ins, outs, sems):
        me = 2 * lax.axis_index("x") + lax.axis_index("y")
        self._each(lambda k, t, chip: self._copy(ins, outs, sems, k, t, chip, me).start())

    def finish(self, ins, outs, sems):
        me = 2 * lax.axis_index("x") + lax.axis_index("y")

        def landed(k, t, chip):
            self._copy(ins, outs, sems, k, t, chip, 2 * chip[0] + chip[1]).wait_recv()
            if t >= self.ns:
                self._pass_on(outs, sems, k, t, chip).start()

        def sent(k, t, chip):
            self._copy(ins, outs, sems, k, t, chip, me).wait_send()
            if t >= self.ns:
                self._pass_on(outs, sems, k, t, chip).wait_send()

        self._each(landed)
        self._each(lambda k, t, chip: self._pass_on(outs, sems, k, t, chip).wait_recv(), north=False)
        self._each(sent)


def _chip_exchange(slabs, whole, only_c):
    ex = _ChipExchange(slabs, whole, only_c)
    n = ex.n

    def body(*refs):
        ins, outs, sems = refs[:n], refs[n:2 * n], refs[2 * n:]
        ex.start(ins, outs, sems)
        ex.finish(ins, outs, sems)

    res = pl.pallas_call(
        body, name="chip_exchange", in_specs=ex.in_specs, out_specs=ex.out_specs, out_shape=ex.out_shape,
        scratch_shapes=ex.scratch,
    )(*slabs, *whole)
    return list(res[:ex.ns]), list(res[ex.ns:])


def _sum_chips(recv, own, name, owner_c=None):
    n, r, C = recv.shape
    rt = _pick(r, _rows_for(C), 16)
    own3 = own.ndim == 3

    def body(on_ref, r_ref, o_ref, out_ref):
        me = 2 * lax.axis_index("x") + lax.axis_index("y")
        acc = None
        for q in range(n):
            term = jnp.where(me == q, o_ref[q] if own3 else o_ref[...], r_ref[q]).astype(f32)
            acc = term if acc is None else acc + term
        out_ref[...] = acc

    blk = pl.BlockSpec((n, rt, C), lambda i, on: (0, i * on[0], 0))
    row = pl.BlockSpec((rt, C), lambda i, on: (i * on[0], 0))
    flag = jnp.ones((1,), jnp.int32) if owner_c is None else _owner_flag(owner_c)
    return pl.pallas_call(
        body, name=name,
        grid_spec=pltpu.PrefetchScalarGridSpec(num_scalar_prefetch=1, grid=(r // rt,),
                                               in_specs=[blk, blk if own3 else row], out_specs=row),
        out_shape=jax.ShapeDtypeStruct((r, C), f32), compiler_params=_cp("arbitrary"),
    )(flag, recv, own)


def _adamw_layers(mine, theirs, owners, w, m, v, name):
    L, r, C = w.shape
    rt = _pick(r, _rows_for(C), 16)

    def body(a0_ref, a1_ref, b0_ref, b1_ref, w_ref, m_ref, v_ref, g_ref, d_ref, mo_ref, vo_ref):
        layer, c = pl.program_id(0), lax.axis_index("c")
        g0 = jnp.where(c == owners[0], a0_ref[...], b0_ref[...])
        g1 = jnp.where(c == owners[1], a1_ref[...], b1_ref[...])
        g_ref[...] = jnp.where(layer == 0, g0, g1)
        _adamw_math(g_ref, w_ref, m_ref, v_ref, d_ref, mo_ref, vo_ref)

    flat = pl.BlockSpec((rt, C), lambda l, i: (i, 0))
    lay = pl.BlockSpec((None, rt, C), lambda l, i: (l, i, 0))
    return pl.pallas_call(
        body, name=name, grid=(L, r // rt), in_specs=[flat] * 4 + [lay] * 3, out_specs=[lay] * 4,
        out_shape=[jax.ShapeDtypeStruct((L, r, C), f32)] * 4, compiler_params=_cp("parallel", "parallel"),
    )(mine[0], mine[1], theirs[0], theirs[1], w, m, v)


def _adamw_math(g_ref, w_ref, m_ref, v_ref, d_ref, mo_ref, vo_ref):
    gg = g_ref[...]
    m_new = ADAM_B1 * m_ref[...] + (1.0 - ADAM_B1) * gg
    v_new = ADAM_B2 * v_ref[...] + (1.0 - ADAM_B2) * (gg * gg)
    m_hat = m_new / (1.0 - ADAM_B1 ** ADAM_STEP)
    v_hat = v_new / (1.0 - ADAM_B2 ** ADAM_STEP)
    d_ref[...] = -ADAM_LR * (m_hat / (jnp.sqrt(v_hat) + ADAM_EPS) + ADAM_WD * w_ref[...])
    mo_ref[...] = m_new
    vo_ref[...] = v_new


FLAT_TILE = 2048


def _add2(a, b, name):
    R = a.shape[0]
    rt = _pick(R, FLAT_TILE, SUBLANES)

    def body(a_ref, b_ref, o_ref):
        o_ref[...] = a_ref[...] + b_ref[...]

    row = pl.BlockSpec((rt, LANES), lambda i: (i, 0))
    return pl.pallas_call(
        body, name=name, grid=(R // rt,), in_specs=[row, row], out_specs=row,
        out_shape=jax.ShapeDtypeStruct((R, LANES), f32), compiler_params=_cp("parallel"),
    )(a, b)


def _adamw(g, w, m, v, name):
    R = g.shape[0]
    rt = _pick(R, FLAT_TILE, SUBLANES)

    def body(g_ref, w_ref, m_ref, v_ref, d_ref, mo_ref, vo_ref):
        _adamw_math(g_ref, w_ref, m_ref, v_ref, d_ref, mo_ref, vo_ref)

    row = pl.BlockSpec((rt, LANES), lambda i: (i, 0))
    return pl.pallas_call(
        body, name=name, grid=(R // rt,), in_specs=[row] * 4, out_specs=[row] * 3,
        out_shape=[jax.ShapeDtypeStruct((R, LANES), f32)] * 3, compiler_params=_cp("parallel"),
    )(g, w, m, v)


def _adamw_whole(g, w, m, v, name):
    def body(g_ref, w_ref, m_ref, v_ref, d_ref, mo_ref, vo_ref):
        _adamw_math(g_ref, w_ref, m_ref, v_ref, d_ref, mo_ref, vo_ref)

    return pl.pallas_call(
        body, name=name, out_shape=[jax.ShapeDtypeStruct(w.shape, f32)] * 3, compiler_params=_cp(),
    )(g, w, m, v)


def _pack(arrs, dtype, rows=None):
    flat = jnp.concatenate([a.astype(dtype).reshape(-1) for a in arrs])
    per = FLAT_TILE * LANES
    total = rows * LANES if rows else flat.shape[0] + (-flat.shape[0]) % per
    flat = jnp.pad(flat, (0, total - flat.shape[0]))
    return flat.reshape(-1, LANES)


def _unpack(buf, shapes):
    flat = buf.reshape(-1)
    out, off = [], 0
    for s in shapes:
        n = math.prod(s)
        out.append(flat[off:off + n].reshape(s))
        off += n
    return out


def _block_diag(w):
    n, a, b = w.shape
    eye = jnp.eye(n, dtype=w.dtype)
    return (w[:, :, None, :] * eye[:, None, :, None]).reshape(n * a, n * b)


def _diag_blocks(m, n):
    a, b = m.shape[0] // n, m.shape[1] // n
    idx = jnp.arange(n)
    return m.reshape(n, a, n, b)[idx, :, idx, :]


BIG = ("w_in", "w_out", "w_up", "w_down", "s5_w_glu")
BIG_COL_SHARDED = {"w_in": True, "w_out": False, "w_up": True, "w_down": False, "s5_w_glu": False}
CONV_SHARDED = ("lru_conv_w", "ffn_conv_w")
SMALL = ("lru_conv_b", "lru_wr", "lru_br", "lru_wi", "lru_bi", "lru_lambda", "s5_a_re", "s5_a_im", "s5_b_re",
         "s5_b_im", "s5_c_re", "s5_c_im", "s5_d", "s5_log_step", "s5_b_glu", "mix_norm_g", "ln1_g", "ln1_b",
         "ffn_conv_b", "ln2_g", "ln2_b")
WEIGHTS = ("w_in", "lru_conv_w", "lru_conv_b", "lru_wr", "lru_br", "lru_wi", "lru_bi", "lru_lambda", "s5_a_re",
           "s5_a_im", "s5_b_re", "s5_b_im", "s5_c_re", "s5_c_im", "s5_d", "s5_log_step", "s5_w_glu", "s5_b_glu",
           "mix_norm_g", "w_out", "ln1_g", "ln1_b", "w_up", "ffn_conv_w", "ffn_conv_b", "w_down", "ln2_g", "ln2_b")


def _assemble(slabs, col_sharded):
    _, L, r, c = slabs.shape
    if col_sharded:
        return slabs.transpose(1, 2, 0, 3).reshape(L, r, 4 * c)
    return slabs.transpose(1, 0, 2, 3).reshape(L, 4 * r, c)


def _s5_prepare(p):
    G = N_S5_GROUPS
    bt_re, bt_im = p["s5_b_re"].transpose(0, 2, 1), p["s5_b_im"].transpose(0, 2, 1)
    ls = p["s5_log_step"].reshape(G, 1)
    ab_re, ab_im, bb_re, bb_im = _s5_params(p["s5_a_re"], p["s5_a_im"], ls, bt_re, bt_im)
    ab = jnp.concatenate([ab_re.reshape(1, S5_LANES), ab_im.reshape(1, S5_LANES)], axis=1)
    bbcat = jnp.concatenate([_block_diag(bb_re), _block_diag(bb_im)], axis=1).astype(bf16)
    ccat = jnp.concatenate([_block_diag(p["s5_c_re"].transpose(0, 2, 1)),
                            -_block_diag(p["s5_c_im"].transpose(0, 2, 1))], axis=0).astype(bf16)
    bbcat_pad = jnp.concatenate([bbcat, jnp.zeros((LRU_WIDTH - S5_WIDTH, 2 * S5_LANES), bf16)], axis=0)
    return dict(bt_re=bt_re, bt_im=bt_im, ls=ls, ab=ab, bbcat=bbcat, bbcat_pad=bbcat_pad, ccat=ccat)


def _layer_fwd(h, p, cos, sin, pending, install):
    sv = {"h": h}
    proj = _mm(h, p["w_in"], "nn", "mm_proj", tn=D_IN_PAD)
    sv.update(proj=proj)
    qr, kr = _rope_fwd(proj, cos, sin)
    attn, ltot, gathered = _attn_fwd2(qr, kr, proj, [s for _, _, s in pending], [True] * len(pending))
    install(pending, gathered)
    sv.update(qr=qr, kr=kr, attn=attn, ltot=ltot)
    wr, wi = _block_diag(p["lru_wr"]).astype(bf16), _block_diag(p["lru_wi"]).astype(bf16)
    xc, r, i, log_a, u = _lru_pre(proj, p["lru_conv_w"], p["lru_conv_b"], wr, p["lru_br"], wi, p["lru_bi"],
                                  p["lru_lambda"])
    hl, lru = _lru_scan(log_a, u, proj)
    sv.update(wr=wr, wi=wi, xc=xc, r=r, i=i, log_a=log_a, hl=hl, lru=lru)
    s5 = _s5_prepare(p)
    bu = _mm(proj, s5["bbcat_pad"], "nn", "mm_s5_bu", a_win=(PROJ_S5_U, LRU_WIDTH))
    xs = _s5_scan(bu, s5["ab"])
    yc = _mm(xs, s5["ccat"], "nn", "mm_s5_y")
    ssm, y = _s5_out(yc, proj, p["s5_d"].reshape(-1), p["s5_w_glu"], p["s5_b_glu"])
    sv.update(s5=s5, xs=xs, y=y, ssm=ssm)
    mixed = _mixnorm(attn, lru, ssm, p["mix_norm_g"])
    mix = _mm(mixed, p["w_out"], "nn", "mm_out")
    h1, z1 = _ln_fwd(h, mix, p["ln1_g"], p["ln1_b"], "ln_fwd")
    sv.update(mixed=mixed, z1=z1, h1=h1)
    up = _mm(h1, p["w_up"], "nn", "mm_up", tn=1536)
    act = _ffn_act(up, p["ffn_conv_w"], p["ffn_conv_b"])
    ffn = _mm(act, p["w_down"], "nn", "mm_down")
    h2, z2 = _ln_fwd(h1, ffn, p["ln2_g"], p["ln2_b"], "ln_fwd")
    sv.update(up=up, act=act, z2=z2)
    return h2, sv


def _layer_bwd(dy_a, dy_b, p, sv, cos, sin, relay=None):
    gr = {}
    dz2, acc = _ln_bwd(dy_a, dy_b, sv["z2"], p["ln2_g"], "ln_bwd_top" if dy_a is None else "ln_bwd")
    gr["ln2_g"], gr["ln2_b"] = acc[0], acc[1]
    dact = _mm(dz2, p["w_down"], "nt", "mm_dact", out_dtype=bf16)
    gr["w_down"] = _mm(sv["act"], dz2, "tn", "mm_dw_down")
    dupc = _ffn_act_bwd(dact, sv["up"], p["ffn_conv_w"], p["ffn_conv_b"])
    others, others_dst, to_slabs = relay if relay else ((), 0, None)
    send = list(others) + ([gr["w_down"]] if relay else [])
    send_dst = [others_dst] * len(others) + ([EARLY_OWNER] if relay else [])
    dup, acc, from_sibling = _conv_bwd(dupc, sv["up"], p["ffn_conv_w"], "ffn_conv_bwd", col_tile=FFN_COL_TILE,
                                       out_dtype=bf16, send=send, send_dst=send_dst)
    slabs, owners = to_slabs(from_sibling[:-1], gr["w_down"], from_sibling[-1]) if relay else ((), 0)
    gr["ffn_conv_w"], gr["ffn_conv_b"] = acc[0:3], acc[3]
    dh1 = _mm(dup, p["w_up"], "nt", "mm_dh1", tk=2048)
    gr["w_up"] = _mm(sv["h1"], dup, "tn", "mm_dw_up", tn=1536)
    dz1, acc = _ln_bwd(dz2, dh1, sv["z1"], p["ln1_g"], "ln_bwd")
    gr["ln1_g"], gr["ln1_b"] = acc[0], acc[1]
    dmixed = _mm(dz1, p["w_out"], "nt", "mm_dmixed")
    gr["w_out"] = _mm(sv["mixed"], dz1, "tn", "mm_dw_out")
    dattn, dlru, dssm, delta, acc = _mixnorm_bwd(dmixed, sv["attn"], sv["lru"], sv["ssm"], p["mix_norm_g"])
    gr["mix_norm_g"] = acc[0]
    proj = sv["proj"]
    late = [gr["w_up"]] if relay else []
    dqr, dkr, dv, received, late_sibling = _attn_bwd2(sv["qr"], sv["kr"], proj, dattn, sv["ltot"], delta, slabs,
                                                      owners, late, EARLY_OWNER)
    dq, dk = _rope_bwd(dqr, dkr, cos, sin)
    g, dgate = _lru_scan_bwd(dlru, proj, sv["hl"], sv["log_a"])
    dxc, dwr, dwi, acc = _lru_gate_bwd(g, sv["hl"], sv["xc"], sv["r"], sv["i"], sv["log_a"], sv["wr"], sv["wi"],
                                       p["lru_lambda"])
    gr["lru_wr"], gr["lru_wi"] = _diag_blocks(dwr, N_LRU_HEADS), _diag_blocks(dwi, N_LRU_HEADS)
    gr["lru_br"], gr["lru_bi"], gr["lru_lambda"] = acc[0], acc[1], acc[2]
    dxr, acc, _ = _conv_bwd(dxc, proj, p["lru_conv_w"], "lru_conv_bwd", x_col_block=PROJ_LRU_X)
    gr["lru_conv_w"], gr["lru_conv_b"] = acc[0:4], acc[4]
    s5 = sv["s5"]
    G = N_S5_GROUPS
    dy, du_direct, dwglu, acc = _s5_out_bwd(dssm, sv["y"], proj, p["s5_d"].reshape(-1), p["s5_w_glu"],
                                            p["s5_b_glu"])
    gr["s5_w_glu"], gr["s5_b_glu"], gr["s5_d"] = dwglu, acc[0], acc[1].reshape(G, S5_GROUP)
    dxs = _mm(dy, s5["ccat"], "nt", "mm_s5_dx")
    dccat = _mm(sv["xs"], dy, "tn", "mm_s5_dc")
    gr["s5_c_re"] = _diag_blocks(dccat[:S5_LANES], G).transpose(0, 2, 1)
    gr["s5_c_im"] = -_diag_blocks(dccat[S5_LANES:], G).transpose(0, 2, 1)
    gs, dab = _s5_scan_bwd(dxs, sv["xs"], s5["ab"])
    du = _mm(gs, s5["bbcat"], "nt", "mm_s5_du", add=du_direct)
    dbbcat = _mm(proj, gs, "tn", "mm_s5_dbb", a_win=(PROJ_S5_U, LRU_WIDTH))[:S5_WIDTH]
    d_ar, d_ai, d_ls, d_btr, d_bti = _s5_params_bwd(
        p["s5_a_re"], p["s5_a_im"], s5["ls"], s5["bt_re"], s5["bt_im"],
        dab[:, :S5_LANES].reshape(G, S5_STATE), dab[:, S5_LANES:].reshape(G, S5_STATE),
        _diag_blocks(dbbcat[:, :S5_LANES], G), _diag_blocks(dbbcat[:, S5_LANES:], G))
    gr["s5_a_re"], gr["s5_a_im"], gr["s5_log_step"] = d_ar, d_ai, d_ls.reshape(G)
    gr["s5_b_re"], gr["s5_b_im"] = d_btr.transpose(0, 2, 1), d_bti.transpose(0, 2, 1)
    pad = jnp.zeros((du.shape[0], D_IN_PAD - D_IN), f32)
    dproj = jnp.concatenate([dq, dk, dv, dxr, dgate, du, pad], axis=1).astype(bf16)
    gr["w_in"] = _mm(sv["h"], dproj, "tn", "mm_dw_in", tn=768)[:, :D_IN]
    dh = _mm(dproj, p["w_in"], "nt", "mm_dh")
    return (dz1, dh, gr, slabs, received, late_sibling) if relay else (dz1, dh, gr)


def _train_step(d):
    x, target = d["x"][0], d["loss_target"][0]
    S = x.shape[0]
    me = 2 * lax.axis_index("x") + lax.axis_index("y")

    def rows2d(a):
        return a.reshape(a.shape[0] * a.shape[1], a.shape[2])

    params = [{n: d[n][l] for n in SMALL} for l in range(DEPTH)]

    def install(items, gathered):
        for (n, l, mine), g in zip(items, gathered):
            g = lax.dynamic_update_slice_in_dim(g, mine[None], me, axis=0)
            if n in CONV_SHARDED:
                full = _assemble(g.reshape((4,) + d[n].shape), True)
                for k in range(DEPTH):
                    params[k][n] = full[k]
                continue
            full = _assemble(g[:, None], BIG_COL_SHARDED[n])[0]
            if n == "w_in":
                full = jnp.pad(full, ((0, 0), (0, D_IN_PAD - D_IN)))
            params[l][n] = full

    def shard(n, l):
        return (n, l, d[n][l].astype(bf16))

    first = [shard("w_in", 0)] + [(n, None, rows2d(d[n])) for n in CONV_SHARDED]
    install(first, _gather_weights([s for _, _, s in first], [True] + [False] * len(CONV_SHARDED)))
    later = [[shard(n, 0) for n in BIG[1:]] + [shard("w_in", 1)], [shard(n, 1) for n in BIG[1:]]]

    cos, sin = _rope_tables(S)
    h, saved = x, []
    for l in range(DEPTH):
        h, sv = _layer_fwd(h, params[l], cos, sin, later[l], install)
        saved.append(sv)
    dy, loss_acc = _loss_head(h, target)
    def slab(n, g, other, owner):
        aligned = BIG_COL_SHARDED[n] and (g.shape[1] // 4) % LANES == 0
        p = _pair_sum(g, other, "pair_sum_" + n, owner, col_slabs=aligned)
        if BIG_COL_SHARDED[n] and not aligned:
            return p.reshape(p.shape[0], 4, p.shape[1] // 4).transpose(1, 0, 2)
        return p if aligned else p.reshape(4, p.shape[0] // 4, p.shape[1])

    own1 = {n: 1 - EARLY_OWNER for n in BIG}
    own0 = {n: (EARLY_OWNER if n in ("w_down", "w_up") else 1 - EARLY_OWNER) for n in BIG}

    def hidden_slabs(others1, w_down0, w_down0_sibling):
        slabs = [slab(n, grads[1][n], o, own1[n]) for n, o in zip(BIG, others1)]
        slabs.append(slab("w_down", w_down0, w_down0_sibling, own0["w_down"]))
        return slabs, [own1[n] for n in BIG] + [own0["w_down"]]

    da, db, grads = None, dy, [None] * DEPTH
    da, db, grads[1] = _layer_bwd(da, db, params[1], saved[1], cos, sin)
    relay = ([grads[1][n] for n in BIG], 1 - EARLY_OWNER, hidden_slabs)
    da, db, grads[0], hslabs, hrecv, (w_up0_sibling,) = _layer_bwd(da, db, params[0], saved[0], cos, sin, relay)
    out = {"grad_x": _axpy(da, db, "grad_x")[None]}

    small = SMALL + CONV_SHARDED
    sp = _pack([grads[l][n] for n in small for l in range(DEPTH)], f32)
    tail = [n for n in BIG if n != "w_down"]
    rest = [n for n in tail if n != "w_up"]
    others_rest, (sp_sibling,) = _sibling_send([grads[0][n] for n in rest], [own0[n] for n in rest], [sp])
    sibling0 = dict(zip(rest, others_rest), w_up=w_up0_sibling)
    tslabs = [slab(n, grads[0][n], sibling0[n], own0[n]) for n in tail]
    chip_small = _add2(sp, sp_sibling, "pair_sum_small")
    trecv, (recv_small,) = _chip_exchange(tslabs, [chip_small], [own0[n] for n in tail])
    mine0 = {n: _sum_chips(r, s, "sum_chips_" + n, own0[n]) for n, r, s in zip(tail, trecv, tslabs)}
    mine0["w_down"] = _sum_chips(hrecv[-1], hslabs[-1], "sum_chips_w_down", own0["w_down"])
    mine1 = {n: _sum_chips(r, s, "sum_chips_" + n, own1[n]) for n, r, s in zip(BIG, hrecv, hslabs)}
    sent, _ = _sibling_send([mine0[n] for n in BIG] + [mine1[n] for n in BIG],
                            [1 - own0[n] for n in BIG] + [1 - own1[n] for n in BIG])
    theirs0, theirs1 = dict(zip(BIG, sent[:len(BIG)])), dict(zip(BIG, sent[len(BIG):]))
    for n in BIG:
        upd = _adamw_layers((mine0[n], mine1[n]), (theirs0[n], theirs1[n]), (own0[n], own1[n]),
                            d[n], d["m_" + n], d["v_" + n], "adamw_" + n)
        for pre, u in zip(("grad_", "delta_", "new_m_", "new_v_"), upd):
            out[pre + n] = u

    total = _sum_chips(recv_small, chip_small, "sum_chips_small")
    rows = total.shape[0]
    upd = _adamw(total, _pack([d[n] for n in SMALL], f32, rows), _pack([d["m_" + n] for n in SMALL], f32, rows),
                 _pack([d["v_" + n] for n in SMALL], f32, rows), "adamw_small")
    small_shapes = [d[n].shape for n in SMALL]
    for pre, buf in zip(("grad_", "delta_", "new_m_", "new_v_"), (total,) + tuple(upd)):
        for n, a in zip(SMALL, _unpack(buf, small_shapes)):
            out[pre + n] = a
    conv_full = _unpack(total, small_shapes + [(DEPTH,) + grads[0][n].shape for n in CONV_SHARDED])
    for n, g in zip(CONV_SHARDED, conv_full[len(SMALL):]):
        L, K, C = g.shape
        g = lax.dynamic_index_in_dim(g.reshape(L, K, 4, C // 4), me, axis=2, keepdims=False)
        out["grad_" + n] = g
        for pre, u in zip(("delta_", "new_m_", "new_v_"), _adamw_whole(g, d[n], d["m_" + n], d["v_" + n], "adamw_" + n)):
            out[pre + n] = u

    loss_local, _ = lax.optimization_barrier((loss_acc[0, 0], upd[0]))
    out["loss"] = lax.psum(loss_local, ("x", "y", "c"))
    return (out["loss"], out["grad_x"]) + tuple(out[pre + n] for pre in ("grad_", "delta_", "new_m_", "new_v_")
                                                for n in WEIGHTS)


def kernel(
        x, w_in, lru_conv_w, lru_conv_b, lru_wr, lru_br, lru_wi, lru_bi, lru_lambda, s5_a_re, s5_a_im, s5_b_re,
        s5_b_im, s5_c_re, s5_c_im, s5_d, s5_log_step, s5_w_glu, s5_b_glu, mix_norm_g, w_out, ln1_g, ln1_b, w_up,
        ffn_conv_w, ffn_conv_b, w_down, ln2_g, ln2_b, loss_target, m_w_in, m_lru_conv_w, m_lru_conv_b, m_lru_wr,
        m_lru_br, m_lru_wi, m_lru_bi, m_lru_lambda, m_s5_a_re, m_s5_a_im, m_s5_b_re, m_s5_b_im, m_s5_c_re,
        m_s5_c_im, m_s5_d, m_s5_log_step, m_s5_w_glu, m_s5_b_glu, m_mix_norm_g, m_w_out, m_ln1_g, m_ln1_b,
        m_w_up, m_ffn_conv_w, m_ffn_conv_b, m_w_down, m_ln2_g, m_ln2_b, v_w_in, v_lru_conv_w, v_lru_conv_b,
        v_lru_wr, v_lru_br, v_lru_wi, v_lru_bi, v_lru_lambda, v_s5_a_re, v_s5_a_im, v_s5_b_re, v_s5_b_im,
        v_s5_c_re, v_s5_c_im, v_s5_d, v_s5_log_step, v_s5_w_glu, v_s5_b_glu, v_mix_norm_g, v_w_out, v_ln1_g,
        v_ln1_b, v_w_up, v_ffn_conv_w, v_ffn_conv_b, v_w_down, v_ln2_g, v_ln2_b
):
    return _train_step(dict(locals()))
```

```python
import math

import jax
import jax.numpy as jnp
from jax import lax
from jax.experimental import pallas as pl
from jax.experimental.pallas import tpu as pltpu

f32 = jnp.float32
bf16 = jnp.bfloat16
MESH = pl.DeviceIdType.MESH

D_MODEL = 1024
ATTN_WIDTH = 384
LRU_WIDTH = 384
S5_WIDTH = 256
HEAD_DIM = 64
N_LRU_HEADS = 6
N_S5_GROUPS = 16
S5_GROUP = 16
S5_STATE = 64
S5_LANES = N_S5_GROUPS * S5_STATE
D_FF = 3072
D_IN = 2176
LRU_C = 8.0
ROPE_THETA = 10000.0
DILATIONS = (1, 4, 16)
ATTN_BLOCK = 128
DEPTH = 2
ALPHA = (2 * DEPTH) ** 0.25
LN_EPS = 1e-5
RMS_EPS = 1e-6
ADAM_LR, ADAM_B1, ADAM_B2, ADAM_EPS, ADAM_WD, ADAM_STEP = 0.001, 0.9, 0.999, 1e-08, 0.01, 10

SUBLANES = 8
LANES = 128
VMEM_LIMIT = 56 * 1024 * 1024
ROW_TILE = 512
MM_SINGLE_K = 3072
D_IN_PAD = 2304
NEG = -1e30


def _cp(*sem):
    return pltpu.CompilerParams(dimension_semantics=sem if sem else None, vmem_limit_bytes=VMEM_LIMIT)


def _pick(dim, pref, align=LANES):
    if dim <= pref:
        return dim
    t = (pref // align) * align
    while t >= align:
        if dim % t == 0:
            return t
        t -= align
    return dim


def _gelu(x):
    return jax.nn.gelu(x)


def _gelu_grad(x):
    c = math.sqrt(2.0 / math.pi)
    t = jnp.tanh(c * (x + 0.044715 * x * x * x))
    return 0.5 * (1.0 + t) + 0.5 * x * (1.0 - t * t) * c * (1.0 + 3 * 0.044715 * x * x)


def _gelu_pair(x):
    c = math.sqrt(2.0 / math.pi)
    x2 = x * x
    t = jnp.tanh(c * x * (1.0 + 0.044715 * x2))
    return 0.5 * x * (1.0 + t), 0.5 * (1.0 + t) + 0.5 * x * (1.0 - t * t) * c * (1.0 + 3 * 0.044715 * x2)


def _sigmoid(x):
    return jax.nn.sigmoid(x)


def _expm1(x):
    p = 1.0 + x / 9.0
    for n in (8.0, 7.0, 6.0, 5.0, 4.0, 3.0, 2.0):
        p = 1.0 + (x / n) * p
    return jnp.where(jnp.abs(x) < 0.3, x * p, jnp.exp(x) - 1.0)


def _dot(a, b, dims):
    return lax.dot_general(a, b, (dims, ((), ())), preferred_element_type=f32)


NN = ((1,), (0,))
NT = ((1,), (1,))
TN = ((0,), (0,))


def _mm(a, b, mode, name, out_dtype=f32, tm=1024, tn=1024, tk=1024, add=None, a_win=None):
    if mode == "nn":
        (M, K), N = a.shape, b.shape[1]
    elif mode == "nt":
        (M, K), N = a.shape, b.shape[0]
    else:
        (K, M), N = a.shape, b.shape[1]
    win = 0
    if a_win is not None:
        win, w = a_win
        if mode == "tn":
            M, tm = w, w
        else:
            K = w
    single = mode != "tn" and K <= MM_SINGLE_K
    tm, tn = _pick(M, tm), _pick(N, tn)
    tk = K if single else _pick(K, tk)
    nk = K // tk
    dims = {"nn": NN, "nt": NT, "tn": TN}[mode]

    def body(a_ref, b_ref, *rest):
        prod = _dot(a_ref[...].astype(bf16), b_ref[...].astype(bf16), dims)
        if single:
            o_ref = rest[-1]
            o_ref[...] = (prod if add is None else prod + rest[0][...]).astype(o_ref.dtype)
            return
        o_ref, acc = rest[-2:]
        k = pl.program_id(2)

        @pl.when(k == 0)
        def _():
            acc[...] = prod if add is None else prod + rest[0][...]

        @pl.when(k > 0)
        def _():
            acc[...] += prod

        @pl.when(k == nk - 1)
        def _():
            o_ref[...] = acc[...].astype(o_ref.dtype)

    if mode == "tn":
        a_spec = pl.BlockSpec((tk, tm), lambda i, j, k: (k, i + win))
    else:
        a_spec = pl.BlockSpec((tm, tk), lambda i, j, k: (i, k + win))
    if mode == "nt":
        b_spec = pl.BlockSpec((tn, tk), lambda i, j, k: (j, k))
    else:
        b_spec = pl.BlockSpec((tk, tn), lambda i, j, k: (k, j))
    o_spec = pl.BlockSpec((tm, tn), lambda i, j, k: (i, j))
    return pl.pallas_call(
        body, name=name, grid=(M // tm, N // tn, nk),
        in_specs=[a_spec, b_spec] + ([] if add is None else [o_spec]), out_specs=o_spec,
        out_shape=jax.ShapeDtypeStruct((M, N), out_dtype),
        scratch_shapes=[] if single else [pltpu.VMEM((tm, tn), f32)],
        compiler_params=_cp("parallel", "parallel", "arbitrary"),
    )(*((a, b) if add is None else (a, b, add)))


def _shift_down(cur, prev8, k):
    if k == 0:
        return cur
    T, (R, C) = SUBLANES, cur.shape
    rot = pltpu.roll(cur.reshape(R // T, T, C), k, 1)
    before = jnp.concatenate([pltpu.roll(prev8, k, 0)[None], rot[:-1]], axis=0)
    row = lax.broadcasted_iota(jnp.int32, (R // T, T, C), 1)
    return jnp.where(row < k, before, rot).reshape(R, C)


def _shift_up(cur, next8, k):
    if k == 0:
        return cur
    T, (R, C) = SUBLANES, cur.shape
    rot = pltpu.roll(cur.reshape(R // T, T, C), T - k, 1)
    after = jnp.concatenate([rot[1:], pltpu.roll(next8, T - k, 0)[None]], axis=0)
    row = lax.broadcasted_iota(jnp.int32, (R // T, T, C), 1)
    return jnp.where(row < T - k, rot, after).reshape(R, C)


def _prev_halo_spec(rt, cols, ncolblk_fn):
    per = rt // SUBLANES
    return pl.BlockSpec((SUBLANES, cols), lambda *g: (jnp.maximum(g[-1] * per - 1, 0), ncolblk_fn(*g)))


def _ln_fwd(h, branch, g, b, name):
    S, D = h.shape
    rt = _pick(S, ROW_TILE, SUBLANES)

    def body(h_ref, m_ref, g_ref, b_ref, o_ref, z_ref):
        z = ALPHA * h_ref[...] + m_ref[...]
        mu = jnp.mean(z, axis=-1, keepdims=True)
        zc = z - mu
        var = jnp.mean(zc * zc, axis=-1, keepdims=True)
        o_ref[...] = zc * lax.rsqrt(var + LN_EPS) * g_ref[...] + b_ref[...]
        z_ref[...] = z

    row = pl.BlockSpec((rt, D), lambda i: (i, 0))
    vec = pl.BlockSpec((1, D), lambda i: (0, 0))
    return pl.pallas_call(
        body, name=name, grid=(S // rt,), in_specs=[row, row, vec, vec], out_specs=[row, row],
        out_shape=[jax.ShapeDtypeStruct((S, D), f32)] * 2, compiler_params=_cp("parallel"),
    )(h, branch, g.reshape(1, D), b.reshape(1, D))


def _ln_bwd(dy_a, dy_b, z, g, name):
    S, D = z.shape
    rt = _pick(S, ROW_TILE, SUBLANES)
    two = dy_a is not None

    def body(*refs):
        if two:
            a_ref, b_ref, z_ref, g_ref, dz_ref, acc_ref = refs
            dy = ALPHA * a_ref[...] + b_ref[...]
        else:
            b_ref, z_ref, g_ref, dz_ref, acc_ref = refs
            dy = b_ref[...]
        z = z_ref[...]
        mu = jnp.mean(z, axis=-1, keepdims=True)
        zc = z - mu
        var = jnp.mean(zc * zc, axis=-1, keepdims=True)
        rstd = lax.rsqrt(var + LN_EPS)
        xhat = zc * rstd
        dxh = dy * g_ref[...]
        m1 = jnp.mean(dxh, axis=-1, keepdims=True)
        m2 = jnp.mean(dxh * xhat, axis=-1, keepdims=True)
        dz_ref[...] = rstd * (dxh - m1 - xhat * m2)

        @pl.when(pl.program_id(0) == 0)
        def _():
            acc_ref[...] = jnp.zeros_like(acc_ref)

        acc_ref[0:1, :] += jnp.sum(dy * xhat, axis=0, keepdims=True)
        acc_ref[1:2, :] += jnp.sum(dy, axis=0, keepdims=True)

    row = pl.BlockSpec((rt, D), lambda i: (i, 0))
    vec = pl.BlockSpec((1, D), lambda i: (0, 0))
    acc = pl.BlockSpec((SUBLANES, D), lambda i: (0, 0))
    ins = ([dy_a] if two else []) + [dy_b, z, g.reshape(1, D)]
    return pl.pallas_call(
        body, name=name, grid=(S // rt,), in_specs=[row] * (len(ins) - 1) + [vec], out_specs=[row, acc],
        out_shape=[jax.ShapeDtypeStruct((S, D), f32), jax.ShapeDtypeStruct((SUBLANES, D), f32)],
        compiler_params=_cp("arbitrary"),
    )(*ins)


def _loss_head(y, target):
    S, D = y.shape
    rt = _pick(S, ROW_TILE, SUBLANES)

    def body(y_ref, t_ref, dy_ref, acc_ref):
        e = y_ref[...] - t_ref[...]
        dy_ref[...] = e * (1.0 / D)

        @pl.when(pl.program_id(0) == 0)
        def _():
            acc_ref[...] = jnp.zeros_like(acc_ref)

        part = jnp.sum(jnp.mean(e * e, axis=-1, keepdims=True), axis=0, keepdims=True)
        acc_ref[...] += 0.5 * part

    row = pl.BlockSpec((rt, D), lambda i: (i, 0))
    return pl.pallas_call(
        body, name="loss_head", grid=(S // rt,), in_specs=[row, row],
        out_specs=[row, pl.BlockSpec((1, 1), lambda i: (0, 0))],
        out_shape=[jax.ShapeDtypeStruct((S, D), f32), jax.ShapeDtypeStruct((1, 1), f32)],
        compiler_params=_cp("arbitrary"),
    )(y, target)


def _axpy(a, b, name):
    S, D = a.shape
    rt = _pick(S, ROW_TILE, SUBLANES)

    def body(a_ref, b_ref, o_ref):
        o_ref[...] = ALPHA * a_ref[...] + b_ref[...]

    row = pl.BlockSpec((rt, D), lambda i: (i, 0))
    return pl.pallas_call(
        body, name=name, grid=(S // rt,), in_specs=[row, row], out_specs=row,
        out_shape=jax.ShapeDtypeStruct((S, D), f32), compiler_params=_cp("parallel"),
    )(a, b)


def _rope_tables(S):
    rt = _pick(S, ROW_TILE, SUBLANES)

    def body(c_ref, s_ref):
        pos = (pl.program_id(0) * rt + lax.broadcasted_iota(jnp.int32, (rt, LANES), 0)).astype(f32)
        lane = lax.broadcasted_iota(jnp.int32, (rt, LANES), 1)
        j = (lane % (HEAD_DIM // 2)).astype(f32)
        inv = jnp.exp((-j * 2.0 / HEAD_DIM) * math.log(ROPE_THETA))
        ang = pos * inv
        c = jnp.cos(ang)
        s = jnp.where(lane % HEAD_DIM < HEAD_DIM // 2, -jnp.sin(ang), jnp.sin(ang))
        c_ref[...] = jnp.concatenate([c, c, c], axis=1)
        s_ref[...] = jnp.concatenate([s, s, s], axis=1)

    row = pl.BlockSpec((rt, ATTN_WIDTH), lambda i: (i, 0))
    return pl.pallas_call(
        body, name="rope_tables", grid=(S // rt,), in_specs=[], out_specs=[row, row],
        out_shape=[jax.ShapeDtypeStruct((S, ATTN_WIDTH), f32)] * 2, compiler_params=_cp("parallel"),
    )()


def _swap_halves(x):
    lane = lax.broadcasted_iota(jnp.int32, x.shape, 1)
    half = HEAD_DIM // 2
    return jnp.where(lane % HEAD_DIM < half, pltpu.roll(x, x.shape[1] - half, 1), pltpu.roll(x, half, 1))


def _rope_fwd(proj, cos, sin):
    S, W = proj.shape[0], ATTN_WIDTH
    rt = _pick(S, ROW_TILE, SUBLANES)

    def body(q_ref, k_ref, c_ref, s_ref, qo_ref, ko_ref):
        c, s = c_ref[...], s_ref[...]
        qo_ref[...] = q_ref[...] * c + _swap_halves(q_ref[...]) * s
        ko_ref[...] = k_ref[...] * c + _swap_halves(k_ref[...]) * s

    row = pl.BlockSpec((rt, W), lambda i: (i, 0))
    return pl.pallas_call(
        body, name="rope_fwd", grid=(S // rt,), in_specs=[row, pl.BlockSpec((rt, W), lambda i: (i, 1)), row, row],
        out_specs=[row, row], out_shape=[jax.ShapeDtypeStruct((S, W), f32)] * 2, compiler_params=_cp("parallel"),
    )(proj, proj, cos, sin)


def _rope_bwd(dq, dk, cos, sin):
    S, W = dq.shape
    rt = _pick(S, ROW_TILE, SUBLANES)

    def body(q_ref, k_ref, c_ref, s_ref, qo_ref, ko_ref):
        c, s = c_ref[...], s_ref[...]
        qo_ref[...] = q_ref[...] * c + _swap_halves(q_ref[...] * s)
        ko_ref[...] = k_ref[...] * c + _swap_halves(k_ref[...] * s)

    row = pl.BlockSpec((rt, W), lambda i: (i, 0))
    return pl.pallas_call(
        body, name="rope_bwd", grid=(S // rt,), in_specs=[row] * 4, out_specs=[row] * 2,
        out_shape=[jax.ShapeDtypeStruct((S, W), f32)] * 2, compiler_params=_cp("parallel"),
    )(dq, dk, cos, sin)


def _rows(ref, start, d):
    if d == 1:
        return ref[pl.ds(pl.multiple_of(start, ATTN_BLOCK), ATTN_BLOCK), :]
    return ref[pl.ds(start, ATTN_BLOCK, stride=d), :]


def _set_rows(ref, start, d, val):
    if d == 1:
        ref[pl.ds(pl.multiple_of(start, ATTN_BLOCK), ATTN_BLOCK), :] = val
    else:
        ref[pl.ds(start, ATTN_BLOCK, stride=d), :] = val


def _pair_spec(S, first_block):
    return pl.BlockSpec((S, LANES), lambda p: (0, p + first_block))


def _attn_fwd2(qr, kr, proj, shards=(), split=()):
    S = qr.shape[0]
    B = ATTN_BLOCK
    nb = S // B
    scale = HEAD_DIM ** -0.5

    gather = _Gather(shards, split)
    nt = gather.nt

    def body(*refs):
        q_ref, k_ref, v_ref = refs[:3]
        g_ins = refs[3:3 + nt]
        o_ref, l_ref = refs[3 + nt:5 + nt]
        g_outs = refs[5 + nt:5 + 2 * nt]
        m_s, l_s = refs[5 + 2 * nt:7 + 2 * nt]
        g_sems = refs[7 + 2 * nt:]
        if nt:
            @pl.when(pl.program_id(0) == 0)
            def _():
                gather.start(g_ins, g_outs, g_sems)

        qi = lax.broadcasted_iota(jnp.int32, (B, 2 * B), 0)
        ki = lax.broadcasted_iota(jnp.int32, (B, 2 * B), 1)
        dist = qi + B - ki
        band = (dist >= 0) & (dist <= B)
        for bi, d in enumerate(DILATIONS):
            bpc = nb // d

            def blk(b, carry, bi=bi, d=d, bpc=bpc):
                c, n = b // bpc, b % bpc
                start = c + d * B * n
                pstart = c + d * B * jnp.maximum(n - 1, 0)
                valid = band & ((ki >= B) | (n > 0))
                q = _rows(q_ref, start, d).astype(bf16)
                kcat = jnp.concatenate([_rows(k_ref, pstart, d), _rows(k_ref, start, d)], axis=0).astype(bf16)
                vcat = jnp.concatenate([_rows(v_ref, pstart, d), _rows(v_ref, start, d)], axis=0).astype(bf16)
                if bi > 0:
                    m_old, l_old, a_old = _rows(m_s, start, d), _rows(l_s, start, d), _rows(o_ref, start, d)
                ms, ls, accs = [], [], []
                for h in range(2):
                    sl = slice(h * HEAD_DIM, (h + 1) * HEAD_DIM)
                    c0 = h * HEAD_DIM
                    s = jnp.where(valid, _dot(q[:, sl], kcat[:, sl], NT) * scale, NEG)
                    m = jnp.max(s, axis=1, keepdims=True)
                    if bi > 0:
                        mo = m_old[:, c0:c0 + 1]
                        m = jnp.maximum(m, mo)
                        alpha = jnp.exp(mo - m)
                    p = jnp.exp(s - m)
                    l = jnp.sum(p, axis=1, keepdims=True)
                    acc = _dot(p.astype(bf16), vcat[:, sl], NN)
                    if bi > 0:
                        l = l + alpha * l_old[:, c0:c0 + 1]
                        acc = acc + alpha * a_old[:, sl]
                    ms.append(jnp.broadcast_to(m, (B, HEAD_DIM)))
                    ls.append(jnp.broadcast_to(l, (B, HEAD_DIM)))
                    accs.append(acc)
                _set_rows(m_s, start, d, jnp.concatenate(ms, axis=1))
                _set_rows(l_s, start, d, jnp.concatenate(ls, axis=1))
                _set_rows(o_ref, start, d, jnp.concatenate(accs, axis=1))
                return carry

            lax.fori_loop(0, nb, blk, 0, unroll=4)

        def fin(t, carry):
            rows = pl.ds(pl.multiple_of(t * B, B), B)
            l = l_s[rows, :]
            o_ref[rows, :] = o_ref[rows, :] / l
            l_ref[rows, :] = m_s[rows, :] + jnp.log(l)
            return carry

        lax.fori_loop(0, nb, fin, 0)
        if nt:
            @pl.when(pl.program_id(0) == pl.num_programs(0) - 1)
            def _():
                gather.finish(g_ins, g_outs, g_sems)

    pair = _pair_spec(S, 0)
    res = pl.pallas_call(
        body, name="attn_fwd_gather" if nt else "attn_fwd", grid=(3,),
        in_specs=[pair, pair, _pair_spec(S, 2 * ATTN_WIDTH // LANES)] + gather.in_specs,
        out_specs=[pair, pair] + gather.out_specs,
        out_shape=[jax.ShapeDtypeStruct((S, ATTN_WIDTH), f32)] * 2 + gather.out_shape,
        scratch_shapes=[pltpu.VMEM((S, LANES), f32)] * 2 + gather.scratch,
        compiler_params=_cp("arbitrary"),
    )(qr, kr, proj, *shards)
    return res[0], res[1], list(res[2:])


def _attn_bwd2(qr, kr, proj, dattn, ltot, delta, slabs=(), only_c=0, send=(), send_dst=0):
    S = qr.shape[0]
    B = ATTN_BLOCK
    nb = S // B
    scale = HEAD_DIM ** -0.5
    ex = _ChipExchange(slabs, (), only_c)
    snd = _SiblingSend(send, send_dst)
    n, ns = ex.n, snd.n
    hosted = n + ns

    def body(*refs):
        q_ref, k_ref, v_ref, do_ref, l_ref, d_ref = refs[:6]
        x_ins, s_ins = refs[6:6 + n], refs[6 + n:6 + hosted]
        dq_ref, dk_ref, dv_ref = refs[6 + hosted:9 + hosted]
        x_outs, s_outs = refs[9 + hosted:9 + hosted + n], refs[9 + hosted + n:9 + 2 * hosted]
        sems = refs[9 + 2 * hosted:]
        x_sems, s_sems = sems[:len(ex.scratch)], sems[len(ex.scratch):]
        if hosted:
            @pl.when(pl.program_id(0) == 0)
            def _():
                if n:
                    ex.start(x_ins, x_outs, x_sems)
                if ns:
                    snd.start(s_ins, s_outs, s_sems)

        qi = lax.broadcasted_iota(jnp.int32, (B, 2 * B), 0)
        ki = lax.broadcasted_iota(jnp.int32, (B, 2 * B), 1)
        dist1 = qi + B - ki
        band1 = (dist1 >= 0) & (dist1 <= B)
        ri = lax.broadcasted_iota(jnp.int32, (2 * B, B), 0)
        ci = lax.broadcasted_iota(jnp.int32, (2 * B, B), 1)
        dist2 = ri - ci
        band2 = (dist2 >= 0) & (dist2 <= B)
        for bi, d in enumerate(DILATIONS):
            bpc = nb // d

            def blk(b, carry, bi=bi, d=d, bpc=bpc):
                c, n = b // bpc, b % bpc
                start = c + d * B * n
                pstart = c + d * B * jnp.maximum(n - 1, 0)
                nstart = c + d * B * jnp.minimum(n + 1, bpc - 1)
                valid1 = band1 & ((ki >= B) | (n > 0))
                valid2 = band2 & ((ri < B) | (n + 1 < bpc))
                q_c, q_n = _rows(q_ref, start, d), _rows(q_ref, nstart, d)
                k_p, k_c = _rows(k_ref, pstart, d), _rows(k_ref, start, d)
                v_p, v_c = _rows(v_ref, pstart, d), _rows(v_ref, start, d)
                do_c, do_n = _rows(do_ref, start, d), _rows(do_ref, nstart, d)
                l_c, l_n = _rows(l_ref, start, d), _rows(l_ref, nstart, d)
                d_c, d_n = _rows(d_ref, start, d), _rows(d_ref, nstart, d)
                qc = q_c.astype(bf16)
                qcat = jnp.concatenate([q_c, q_n], axis=0).astype(bf16)
                kc = k_c.astype(bf16)
                kcat = jnp.concatenate([k_p, k_c], axis=0).astype(bf16)
                vc = v_c.astype(bf16)
                vcat = jnp.concatenate([v_p, v_c], axis=0).astype(bf16)
                doc = do_c.astype(bf16)
                docat = jnp.concatenate([do_c, do_n], axis=0).astype(bf16)
                lcat = jnp.concatenate([l_c, l_n], axis=0)
                dcat = jnp.concatenate([d_c, d_n], axis=0)
                dqs, dks, dvs = [], [], []
                for h in range(2):
                    sl = slice(h * HEAD_DIM, (h + 1) * HEAD_DIM)
                    c0 = h * HEAD_DIM
                    s1 = _dot(qc[:, sl], kcat[:, sl], NT) * scale
                    p1 = jnp.where(valid1, jnp.exp(s1 - l_c[:, c0:c0 + 1]), 0.0)
                    dp1 = _dot(doc[:, sl], vcat[:, sl], NT)
                    ds1 = p1 * (dp1 - d_c[:, c0:c0 + 1]) * scale
                    dqs.append(_dot(ds1.astype(bf16), kcat[:, sl], NN))
                    s2 = _dot(qcat[:, sl], kc[:, sl], NT) * scale
                    p2 = jnp.where(valid2, jnp.exp(s2 - lcat[:, c0:c0 + 1]), 0.0)
                    dvs.append(_dot(p2.astype(bf16), docat[:, sl], TN))
                    dp2 = _dot(docat[:, sl], vc[:, sl], NT)
                    ds2 = p2 * (dp2 - dcat[:, c0:c0 + 1]) * scale
                    dks.append(_dot(ds2.astype(bf16), qcat[:, sl], TN))
                for ref, parts in ((dq_ref, dqs), (dk_ref, dks), (dv_ref, dvs)):
                    new = jnp.concatenate(parts, axis=1)
                    if bi > 0:
                        new = new + _rows(ref, start, d)
                    _set_rows(ref, start, d, new)
                return carry

            lax.fori_loop(0, nb, blk, 0, unroll=4)

        if hosted:
            @pl.when(pl.program_id(0) == pl.num_programs(0) - 1)
            def _():
                if ns:
                    snd.finish(s_ins, s_outs, s_sems)
                if n:
                    ex.finish(x_ins, x_outs, x_sems)

    pair = _pair_spec(S, 0)
    res = pl.pallas_call(
        body, name="attn_bwd_exchange" if hosted else "attn_bwd", grid=(3,),
        in_specs=[pair, pair, _pair_spec(S, 2 * ATTN_WIDTH // LANES), pair, pair, pair] + ex.in_specs + snd.in_specs,
        out_specs=[pair] * 3 + ex.out_specs + snd.out_specs,
        out_shape=[jax.ShapeDtypeStruct((S, ATTN_WIDTH), f32)] * 3 + ex.out_shape + snd.out_shape,
        scratch_shapes=ex.scratch + snd.scratch, compiler_params=_cp("arbitrary"),
    )(qr, kr, proj, dattn, ltot, delta, *slabs, *send)
    return res[0], res[1], res[2], list(res[3:3 + n]), list(res[3 + n:])


def _softplus_neg(lam):
    return jnp.maximum(-lam, 0.0) + jnp.log1p(jnp.exp(-jnp.abs(lam)))


PROJ_LRU_X, PROJ_LRU_GATE, PROJ_S5_U = 3, 4, 5
EARLY_OWNER = 1


def _lru_pre(proj, conv_w, conv_b, wr, br, wi, bi, lam):
    S, W = proj.shape[0], LRU_WIDTH
    rt = _pick(S, ROW_TILE, SUBLANES)
    K = conv_w.shape[0]

    def body(x_ref, xp_ref, cw_ref, cb_ref, wr_ref, br_ref, wi_ref, bi_ref, lam_ref,
             xc_ref, r_ref, i_ref, la_ref, u_ref):
        prev = jnp.where(pl.program_id(0) == 0, 0.0, xp_ref[...])
        x = x_ref[...]
        xc = cb_ref[...] + cw_ref[K - 1:K, :] * x
        for k in range(K - 1):
            xc = xc + cw_ref[k:k + 1, :] * _shift_down(x, prev, K - 1 - k)
        xb = xc.astype(bf16)
        r = _sigmoid(_dot(xb, wr_ref[...], NN) + br_ref[...])
        i = _sigmoid(_dot(xb, wi_ref[...], NN) + bi_ref[...])
        log_a = -LRU_C * r * _softplus_neg(lam_ref[...])
        u = jnp.sqrt(-_expm1(2.0 * log_a)) * (i * xc)
        xc_ref[...], r_ref[...], i_ref[...], la_ref[...], u_ref[...] = xc, r, i, log_a, u

    row = pl.BlockSpec((rt, W), lambda i: (i, 0))
    xrow = pl.BlockSpec((rt, W), lambda i: (i, PROJ_LRU_X))
    halo = _prev_halo_spec(rt, W, lambda i: PROJ_LRU_X)
    vec = pl.BlockSpec((1, W), lambda i: (0, 0))
    return pl.pallas_call(
        body, name="lru_pre", grid=(S // rt,),
        in_specs=[xrow, halo, pl.BlockSpec((K, W), lambda i: (0, 0)), vec,
                  pl.BlockSpec((W, W), lambda i: (0, 0)), vec, pl.BlockSpec((W, W), lambda i: (0, 0)), vec, vec],
        out_specs=[row] * 5, out_shape=[jax.ShapeDtypeStruct((S, W), f32)] * 5, compiler_params=_cp("parallel"),
    )(proj, proj, conv_w, conv_b.reshape(1, W), wr, br.reshape(1, W), wi, bi.reshape(1, W), lam.reshape(1, W))


def _tile_rows(shape):
    return lax.broadcasted_iota(jnp.int32, shape, 0)


def _lru_scan(log_a, u, proj):
    S, W = u.shape
    rt = _pick(S, ROW_TILE, SUBLANES)
    T = SUBLANES

    def body(la_ref, u_ref, g_ref, h_ref, o_ref, carry):
        @pl.when(pl.program_id(0) == 0)
        def _():
            carry[...] = jnp.zeros_like(carry)

        row = _tile_rows((T, W))

        def step(t, hp):
            r0 = pl.multiple_of(t * T, T)
            a = jnp.exp(la_ref[pl.ds(r0, T), :])
            x = u_ref[pl.ds(r0, T), :]
            for k in (1, 2, 4):
                x = x + a * jnp.where(row >= k, pltpu.roll(x, k, 0), 0.0)
                a = a * jnp.where(row >= k, pltpu.roll(a, k, 0), 1.0)
            h = x + a * hp
            h_ref[pl.ds(r0, T), :] = h
            o_ref[pl.ds(r0, T), :] = h * _gelu(g_ref[pl.ds(r0, T), :])
            return h[T - 1:T, :]

        carry[0:1, :] = lax.fori_loop(0, rt // T, step, carry[0:1, :])

    row = pl.BlockSpec((rt, W), lambda i: (i, 0))
    grow = pl.BlockSpec((rt, W), lambda i: (i, PROJ_LRU_GATE))
    return pl.pallas_call(
        body, name="lru_scan", grid=(S // rt,), in_specs=[row, row, grow], out_specs=[row] * 2,
        out_shape=[jax.ShapeDtypeStruct((S, W), f32)] * 2, scratch_shapes=[pltpu.VMEM((T, W), f32)],
        compiler_params=_cp("arbitrary"),
    )(log_a, u, proj)


def _lru_scan_bwd(dlru, proj, h, log_a):
    S, W = h.shape
    rt = _pick(S, ROW_TILE, SUBLANES)
    T = SUBLANES
    nblk = S // rt

    def body(d_ref, g_ref, h_ref, la_ref, go_ref, dg_ref, carry):
        @pl.when(pl.program_id(0) == 0)
        def _():
            carry[...] = jnp.zeros_like(carry)

        row = _tile_rows((T, W))

        def step(j, c):
            gn, an = c
            t = rt // T - 1 - j
            r0 = pl.multiple_of(t * T, T)
            d = d_ref[pl.ds(r0, T), :]
            gate = g_ref[pl.ds(r0, T), :]
            a = jnp.exp(la_ref[pl.ds(r0, T), :])
            dg_ref[pl.ds(r0, T), :] = d * h_ref[pl.ds(r0, T), :] * _gelu_grad(gate)
            x = d * _gelu(gate)
            b = jnp.where(row < T - 1, pltpu.roll(a, T - 1, 0), an)
            for k in (1, 2, 4):
                x = x + b * jnp.where(row < T - k, pltpu.roll(x, T - k, 0), 0.0)
                b = b * jnp.where(row < T - k, pltpu.roll(b, T - k, 0), 1.0)
            g = x + b * gn
            go_ref[pl.ds(r0, T), :] = g
            return g[0:1, :], a[0:1, :]

        gn, an = lax.fori_loop(0, rt // T, step, (carry[0:1, :], carry[1:2, :]))
        carry[0:1, :] = gn
        carry[1:2, :] = an

    row = pl.BlockSpec((rt, W), lambda i: (nblk - 1 - i, 0))
    grow = pl.BlockSpec((rt, W), lambda i: (nblk - 1 - i, PROJ_LRU_GATE))
    return pl.pallas_call(
        body, name="lru_scan_bwd", grid=(nblk,), in_specs=[row, grow, row, row], out_specs=[row] * 2,
        out_shape=[jax.ShapeDtypeStruct((S, W), f32)] * 2, scratch_shapes=[pltpu.VMEM((T, W), f32)],
        compiler_params=_cp("arbitrary"),
    )(dlru, proj, h, log_a)


def _lru_gate_bwd(g, h, xc, r, i, log_a, wr, wi, lam):
    S, W = g.shape
    rt = _pick(S, ROW_TILE, SUBLANES)

    def body(g_ref, h_ref, hp_ref, xc_ref, r_ref, i_ref, la_ref, wr_ref, wi_ref, lam_ref,
             dxc_ref, dwr_ref, dwi_ref, acc_ref):
        @pl.when(pl.program_id(0) == 0)
        def _():
            dwr_ref[...] = jnp.zeros_like(dwr_ref)
            dwi_ref[...] = jnp.zeros_like(dwi_ref)
            acc_ref[...] = jnp.zeros_like(acc_ref)

        prev = jnp.where(pl.program_id(0) == 0, 0.0, hp_ref[...])
        gg, xc, r, i, log_a, lam = g_ref[...], xc_ref[...], r_ref[...], i_ref[...], la_ref[...], lam_ref[...]
        hm1 = _shift_down(h_ref[...], prev, 1)
        a = jnp.exp(log_a)
        s = jnp.sqrt(-_expm1(2.0 * log_a))
        da = gg * hm1
        di = gg * s * xc
        dxc = gg * s * i
        ds = gg * i * xc
        dlog_a = da * a - ds * (a * a / s)
        sp = _softplus_neg(lam)
        dr = dlog_a * (-LRU_C * sp)
        dsp = jnp.sum(dlog_a * (-LRU_C * r), axis=0, keepdims=True)
        dpr = dr * r * (1.0 - r)
        dpi = di * i * (1.0 - i)
        dprb, dpib, xb = dpr.astype(bf16), dpi.astype(bf16), xc.astype(bf16)
        dxc_ref[...] = dxc + _dot(dprb, wr_ref[...], NT) + _dot(dpib, wi_ref[...], NT)
        dwr_ref[...] += _dot(xb, dprb, TN)
        dwi_ref[...] += _dot(xb, dpib, TN)
        acc_ref[0:1, :] += jnp.sum(dpr, axis=0, keepdims=True)
        acc_ref[1:2, :] += jnp.sum(dpi, axis=0, keepdims=True)
        acc_ref[2:3, :] += dsp * (-_sigmoid(-lam))

    row = pl.BlockSpec((rt, W), lambda i: (i, 0))
    halo = _prev_halo_spec(rt, W, lambda i: 0)
    vec = pl.BlockSpec((1, W), lambda i: (0, 0))
    mat = pl.BlockSpec((W, W), lambda i: (0, 0))
    acc = pl.BlockSpec((SUBLANES, W), lambda i: (0, 0))
    return pl.pallas_call(
        body, name="lru_gate_bwd", grid=(S // rt,),
        in_specs=[row, row, halo, row, row, row, row, mat, mat, vec], out_specs=[row, mat, mat, acc],
        out_shape=[jax.ShapeDtypeStruct((S, W), f32), jax.ShapeDtypeStruct((W, W), f32),
                   jax.ShapeDtypeStruct((W, W), f32), jax.ShapeDtypeStruct((SUBLANES, W), f32)],
        compiler_params=_cp("arbitrary"),
    )(g, h, h, xc, r, i, log_a, wr, wi, lam.reshape(1, W))


def _conv_bwd(dy, x, conv_w, name, col_tile=None, out_dtype=f32, x_col_block=0, send=(), send_dst=0):
    if dy.ndim == 2:
        dy = dy[None]
    H, S, Ch = dy.shape
    C = H * Ch
    K = conv_w.shape[0]
    ct = Ch if col_tile is None else col_tile
    nct = Ch // ct
    rt = _pick(S, ROW_TILE, SUBLANES)
    nrt = S // rt
    snd = _SiblingSend(send, send_dst)
    n = snd.n

    def body(*refs):
        dy_ref, dyn_ref, x_ref, w_ref = refs[:4]
        s_ins = refs[4:4 + n]
        dx_ref, acc_ref = refs[4 + n:6 + n]
        s_outs, s_sems = refs[6 + n:6 + 2 * n], refs[6 + 2 * n:]
        i = pl.program_id(2)
        if n:
            @pl.when((pl.program_id(0) == 0) & (pl.program_id(1) == 0) & (i == 0))
            def _():
                snd.start(s_ins, s_outs, s_sems)

        @pl.when(i == 0)
        def _():
            acc_ref[...] = jnp.zeros_like(acc_ref)

        nxt = jnp.where(i == nrt - 1, 0.0, dyn_ref[...].astype(f32)[0:SUBLANES])
        dy, x = dy_ref[...].astype(f32), x_ref[...]
        ahead = [dy] + [_shift_up(dy, nxt, j) for j in range(1, K)]
        dx = w_ref[K - 1:K, :] * dy
        for k in range(K - 1):
            dx = dx + w_ref[k:k + 1, :] * ahead[K - 1 - k]
        dx_ref[...] = dx.astype(dx_ref.dtype)
        for k in range(K):
            acc_ref[k:k + 1, :] += jnp.sum(ahead[K - 1 - k] * x, axis=0, keepdims=True)
        acc_ref[K:K + 1, :] += jnp.sum(dy, axis=0, keepdims=True)
        if n:
            @pl.when((pl.program_id(0) == H - 1) & (pl.program_id(1) == nct - 1) & (i == nrt - 1))
            def _():
                snd.finish(s_ins, s_outs, s_sems)

    halo = SUBLANES * (4 // dy.dtype.itemsize)
    per, last = rt // halo, S // halo - 1
    dy_row = pl.BlockSpec((None, rt, ct), lambda h, j, i: (h, i, j))
    dy_next = pl.BlockSpec((None, halo, ct), lambda h, j, i: (h, jnp.minimum((i + 1) * per, last), j))
    row = pl.BlockSpec((rt, ct), lambda h, j, i: (i, h * nct + j))
    xrow = pl.BlockSpec((rt, ct), lambda h, j, i: (i, h * nct + j + x_col_block))
    res = pl.pallas_call(
        body, name=name, grid=(H, nct, nrt),
        in_specs=[dy_row, dy_next, xrow, pl.BlockSpec((K, ct), lambda h, j, i: (0, h * nct + j))] + snd.in_specs,
        out_specs=[row, pl.BlockSpec((SUBLANES, ct), lambda h, j, i: (0, h * nct + j))] + snd.out_specs,
        out_shape=[jax.ShapeDtypeStruct((S, C), out_dtype), jax.ShapeDtypeStruct((SUBLANES, C), f32)] + snd.out_shape,
        scratch_shapes=snd.scratch,
        compiler_params=_cp(*(("arbitrary",) * 3 if n else ("parallel", "parallel", "arbitrary"))),
    )(dy, dy, x, conv_w, *send)
    return res[0], res[1], list(res[2:])


def _s5_param_fn(a_re, a_im, ls, bt_re, bt_im):
    step = jnp.exp(ls)
    dt_re, dt_im = step * a_re, step * a_im
    mag = jnp.exp(dt_re)
    ab_re, ab_im = mag * jnp.cos(dt_im), mag * jnp.sin(dt_im)
    z_re, z_im = ab_re - 1.0, ab_im
    den = a_re * a_re + a_im * a_im
    f_re = (z_re * a_re + z_im * a_im) / den
    f_im = (z_im * a_re - z_re * a_im) / den
    bb_re = f_re[:, None, :] * bt_re - f_im[:, None, :] * bt_im
    bb_im = f_re[:, None, :] * bt_im + f_im[:, None, :] * bt_re
    return ab_re, ab_im, bb_re, bb_im


def _s5_params(a_re, a_im, ls, bt_re, bt_im):
    def body(ar, ai, l, br, bi, o_ar, o_ai, o_br, o_bi):
        o_ar[...], o_ai[...], o_br[...], o_bi[...] = _s5_param_fn(ar[...], ai[...], l[...], br[...], bi[...])

    return pl.pallas_call(
        body, name="s5_params",
        out_shape=[jax.ShapeDtypeStruct(a_re.shape, f32)] * 2 + [jax.ShapeDtypeStruct(bt_re.shape, f32)] * 2,
        compiler_params=_cp(),
    )(a_re, a_im, ls, bt_re, bt_im)


def _s5_params_bwd(a_re, a_im, ls, bt_re, bt_im, d_ar, d_ai, d_br, d_bi):
    def body(ar, ai, l, br, bi, c_ar, c_ai, c_br, c_bi, g_ar, g_ai, g_l, g_br, g_bi):
        _, vjp = jax.vjp(_s5_param_fn, ar[...], ai[...], l[...], br[...], bi[...])
        g_ar[...], g_ai[...], g_l[...], g_br[...], g_bi[...] = vjp((c_ar[...], c_ai[...], c_br[...], c_bi[...]))

    return pl.pallas_call(
        body, name="s5_params_bwd",
        out_shape=[jax.ShapeDtypeStruct(a_re.shape, f32)] * 2 + [jax.ShapeDtypeStruct(ls.shape, f32)]
        + [jax.ShapeDtypeStruct(bt_re.shape, f32)] * 2,
        compiler_params=_cp(),
    )(a_re, a_im, ls, bt_re, bt_im, d_ar, d_ai, d_br, d_bi)


S5_CHUNK = 256


def _s5_power_tables(ab_ref, p_ref, w_ref, conj):
    T, L = SUBLANES, S5_LANES
    are = ab_ref[0:1, 0:L]
    aim = ab_ref[0:1, L:2 * L]
    if conj:
        aim = -aim
    pre, pim = are, aim
    for n in range(3):
        p_ref[n:n + 1, 0:L] = pre
        p_ref[n:n + 1, L:2 * L] = pim
        pre, pim = pre * pre - pim * pim, 2.0 * pre * pim
    row = _tile_rows((T, L))
    wre = jnp.zeros((T, L), f32)
    wim = jnp.zeros((T, L), f32)
    pre, pim = are, aim
    for n in range(T):
        tgt = (T - 1 - n) if conj else n
        wre = jnp.where(row == tgt, pre, wre)
        wim = jnp.where(row == tgt, pim, wim)
        pre, pim = pre * are - pim * aim, pre * aim + pim * are
    w_ref[:, 0:L] = wre
    w_ref[:, L:2 * L] = wim


def _s5_scan(bu, ab):
    S, L2 = bu.shape
    L = L2 // 2
    rt = _pick(S, 256, SUBLANES)
    T = SUBLANES
    CH = S5_CHUNK

    def body(bu_ref, ab_ref, x_ref, p_ref, w_ref, carry):
        @pl.when(pl.program_id(0) == 0)
        def _():
            carry[...] = jnp.zeros_like(carry)
            _s5_power_tables(ab_ref, p_ref, w_ref, conj=False)

        row = _tile_rows((T, CH))

        def step(t, _):
            r0 = pl.multiple_of(t * T, T)
            for c in range(L // CH):
                lre, lim = pl.ds(c * CH, CH), pl.ds(L + c * CH, CH)
                xr, xi = bu_ref[pl.ds(r0, T), lre], bu_ref[pl.ds(r0, T), lim]
                for n, k in enumerate((1, 2, 4)):
                    pr, pi = p_ref[n:n + 1, lre], p_ref[n:n + 1, lim]
                    sr = jnp.where(row >= k, pltpu.roll(xr, k, 0), 0.0)
                    si = jnp.where(row >= k, pltpu.roll(xi, k, 0), 0.0)
                    xr, xi = xr + pr * sr - pi * si, xi + pr * si + pi * sr
                cr, ci = carry[T - 1:T, lre], carry[T - 1:T, lim]
                wr, wi = w_ref[:, lre], w_ref[:, lim]
                xr, xi = xr + wr * cr - wi * ci, xi + wr * ci + wi * cr
                carry[:, lre] = xr
                carry[:, lim] = xi
                x_ref[pl.ds(r0, T), lre] = xr
                x_ref[pl.ds(r0, T), lim] = xi
            return 0

        lax.fori_loop(0, rt // T, step, 0)

    row_spec = pl.BlockSpec((rt, L2), lambda i: (i, 0))
    return pl.pallas_call(
        body, name="s5_scan", grid=(S // rt,), in_specs=[row_spec, pl.BlockSpec((1, L2), lambda i: (0, 0))],
        out_specs=row_spec, out_shape=jax.ShapeDtypeStruct((S, L2), f32),
        scratch_shapes=[pltpu.VMEM((T, L2), f32), pltpu.VMEM((T, L2), f32), pltpu.VMEM((T, L2), f32)],
        compiler_params=_cp("arbitrary"),
    )(bu, ab)


def _s5_scan_bwd(dx, x, ab):
    S, L2 = dx.shape
    L = L2 // 2
    rt = _pick(S, 256, SUBLANES)
    T = SUBLANES
    CH = S5_CHUNK
    nblk = S // rt
    per = rt // T

    def body(dx_ref, x_ref, xp_ref, ab_ref, g_ref, da_ref, p_ref, w_ref, carry, acc):
        pid = pl.program_id(0)

        @pl.when(pid == 0)
        def _():
            carry[...] = jnp.zeros_like(carry)
            acc[...] = jnp.zeros_like(acc)
            _s5_power_tables(ab_ref, p_ref, w_ref, conj=True)

        row = _tile_rows((T, CH))
        first_block = pid == nblk - 1

        def step(j, _):
            t = per - 1 - j
            r0 = pl.multiple_of(t * T, T)
            rp = pl.multiple_of(jnp.maximum(t - 1, 0) * T, T)
            for c in range(L // CH):
                lre, lim = pl.ds(c * CH, CH), pl.ds(L + c * CH, CH)
                gr, gi = dx_ref[pl.ds(r0, T), lre], dx_ref[pl.ds(r0, T), lim]
                for n, k in enumerate((1, 2, 4)):
                    pr, pi = p_ref[n:n + 1, lre], p_ref[n:n + 1, lim]
                    sr = jnp.where(row < T - k, pltpu.roll(gr, T - k, 0), 0.0)
                    si = jnp.where(row < T - k, pltpu.roll(gi, T - k, 0), 0.0)
                    gr, gi = gr + pr * sr - pi * si, gi + pr * si + pi * sr
                cr, ci = carry[0:1, lre], carry[0:1, lim]
                wr, wi = w_ref[:, lre], w_ref[:, lim]
                gr, gi = gr + wr * cr - wi * ci, gi + wr * ci + wi * cr
                carry[:, lre] = gr
                carry[:, lim] = gi
                g_ref[pl.ds(r0, T), lre] = gr
                g_ref[pl.ds(r0, T), lim] = gi
                xr, xi = x_ref[pl.ds(r0, T), lre], x_ref[pl.ds(r0, T), lim]
                in_blk_r, in_blk_i = x_ref[pl.ds(rp, T), lre], x_ref[pl.ds(rp, T), lim]
                hal_r = jnp.where(first_block, 0.0, xp_ref[:, lre])
                hal_i = jnp.where(first_block, 0.0, xp_ref[:, lim])
                pvr = jnp.where(t == 0, hal_r, in_blk_r)[T - 1:T, :]
                pvi = jnp.where(t == 0, hal_i, in_blk_i)[T - 1:T, :]
                sxr = jnp.where(row >= 1, pltpu.roll(xr, 1, 0), pvr)
                sxi = jnp.where(row >= 1, pltpu.roll(xi, 1, 0), pvi)
                acc[:, lre] += gr * sxr + gi * sxi
                acc[:, lim] += gi * sxr - gr * sxi
            return 0

        lax.fori_loop(0, per, step, 0)

        @pl.when(pid == nblk - 1)
        def _():
            da_ref[...] = jnp.sum(acc[...], axis=0, keepdims=True)

    row_spec = pl.BlockSpec((rt, L2), lambda i: (nblk - 1 - i, 0))
    halo = pl.BlockSpec((T, L2), lambda i: (jnp.maximum((nblk - 1 - i) * per - 1, 0), 0))
    vec = pl.BlockSpec((1, L2), lambda i: (0, 0))
    return pl.pallas_call(
        body, name="s5_scan_bwd", grid=(nblk,), in_specs=[row_spec, row_spec, halo, vec],
        out_specs=[row_spec, vec],
        out_shape=[jax.ShapeDtypeStruct((S, L2), f32), jax.ShapeDtypeStruct((1, L2), f32)],
        scratch_shapes=[pltpu.VMEM((T, L2), f32)] * 4,
        compiler_params=_cp("arbitrary"),
    )(dx, x, x, ab)


def _S5_U_SPEC(rt):
    return pl.BlockSpec((rt, LRU_WIDTH), lambda i: (i, PROJ_S5_U))


def _s5_out(yc, proj, d, wglu, bglu):
    S, W = yc.shape
    rt = _pick(S, ROW_TILE, SUBLANES)

    def body(yc_ref, u_ref, d_ref, w_ref, b_ref, o_ref, y_ref):
        y = yc_ref[...] + d_ref[...] * u_ref[:, 0:W]
        yg = _gelu(y)
        z = _dot(yg.astype(bf16), w_ref[...], NN) + b_ref[...]
        o_ref[...] = yg * _sigmoid(z)
        y_ref[...] = y

    row = pl.BlockSpec((rt, W), lambda i: (i, 0))
    vec = pl.BlockSpec((1, W), lambda i: (0, 0))
    mat = pl.BlockSpec((W, W), lambda i: (0, 0))
    return pl.pallas_call(
        body, name="s5_out", grid=(S // rt,), in_specs=[row, _S5_U_SPEC(rt), vec, mat, vec], out_specs=[row, row],
        out_shape=[jax.ShapeDtypeStruct((S, W), f32)] * 2, compiler_params=_cp("parallel"),
    )(yc, proj, d.reshape(1, W), wglu, bglu.reshape(1, W))


def _s5_out_bwd(dssm, y, proj, d, wglu, bglu):
    S, W = y.shape
    rt = _pick(S, ROW_TILE, SUBLANES)

    def body(do_ref, y_ref, u_ref, d_ref, w_ref, b_ref, dy_ref, du_ref, dw_ref, acc_ref):
        @pl.when(pl.program_id(0) == 0)
        def _():
            dw_ref[...] = jnp.zeros_like(dw_ref)
            acc_ref[...] = jnp.zeros_like(acc_ref)

        do, y = do_ref[...], y_ref[...]
        yg = _gelu(y)
        ygb = yg.astype(bf16)
        sg = _sigmoid(_dot(ygb, w_ref[...], NN) + b_ref[...])
        dz = do * yg * sg * (1.0 - sg)
        dzb = dz.astype(bf16)
        dyg = do * sg + _dot(dzb, w_ref[...], NT)
        dy = dyg * _gelu_grad(y)
        dy_ref[...] = dy
        du_ref[...] = dy * d_ref[...]
        dw_ref[...] += _dot(ygb, dzb, TN)
        acc_ref[0:1, :] += jnp.sum(dz, axis=0, keepdims=True)
        acc_ref[1:2, :] += jnp.sum(dy * u_ref[:, 0:W], axis=0, keepdims=True)

    row = pl.BlockSpec((rt, W), lambda i: (i, 0))
    vec = pl.BlockSpec((1, W), lambda i: (0, 0))
    mat = pl.BlockSpec((W, W), lambda i: (0, 0))
    acc = pl.BlockSpec((SUBLANES, W), lambda i: (0, 0))
    return pl.pallas_call(
        body, name="s5_out_bwd", grid=(S // rt,), in_specs=[row, row, _S5_U_SPEC(rt), vec, mat, vec],
        out_specs=[row, row, mat, acc],
        out_shape=[jax.ShapeDtypeStruct((S, W), f32)] * 2
        + [jax.ShapeDtypeStruct((W, W), f32), jax.ShapeDtypeStruct((SUBLANES, W), f32)],
        compiler_params=_cp("arbitrary"),
    )(dssm, y, proj, d.reshape(1, W), wglu, bglu.reshape(1, W))


MIX_SPLITS = ((0, ATTN_WIDTH), (ATTN_WIDTH, ATTN_WIDTH + LRU_WIDTH), (ATTN_WIDTH + LRU_WIDTH, D_MODEL))


def _mixnorm(attn, lru, ssm, g):
    S = attn.shape[0]
    rt = _pick(S, ROW_TILE, SUBLANES)

    def body(a_ref, l_ref, s_ref, g_ref, o_ref):
        for ref, (lo, hi) in zip((a_ref, l_ref, s_ref), MIX_SPLITS):
            x = ref[...]
            ms = jnp.mean(x * x, axis=-1, keepdims=True)
            o_ref[:, lo:hi] = (x * lax.rsqrt(ms + RMS_EPS) * g_ref[:, lo:hi]).astype(o_ref.dtype)

    rows = [pl.BlockSpec((rt, hi - lo), lambda i: (i, 0)) for lo, hi in MIX_SPLITS]
    return pl.pallas_call(
        body, name="mixnorm", grid=(S // rt,), in_specs=rows + [pl.BlockSpec((1, D_MODEL), lambda i: (0, 0))],
        out_specs=pl.BlockSpec((rt, D_MODEL), lambda i: (i, 0)),
        out_shape=jax.ShapeDtypeStruct((S, D_MODEL), bf16), compiler_params=_cp("parallel"),
    )(attn, lru, ssm, g.reshape(1, D_MODEL))


def _mixnorm_bwd(dmixed, attn, lru, ssm, g):
    S = attn.shape[0]
    rt = _pick(S, ROW_TILE, SUBLANES)

    def body(d_ref, a_ref, l_ref, s_ref, g_ref, da_ref, dl_ref, ds_ref, dlt_ref, acc_ref):
        @pl.when(pl.program_id(0) == 0)
        def _():
            acc_ref[...] = jnp.zeros_like(acc_ref)

        outs = []
        for ref, (lo, hi) in zip((a_ref, l_ref, s_ref), MIX_SPLITS):
            x = ref[...]
            dy = d_ref[:, lo:hi]
            rinv = lax.rsqrt(jnp.mean(x * x, axis=-1, keepdims=True) + RMS_EPS)
            dyg = dy * g_ref[:, lo:hi]
            outs.append(rinv * dyg - x * (rinv * rinv * rinv) * jnp.mean(dyg * x, axis=-1, keepdims=True))
            acc_ref[0:1, lo:hi] += jnp.sum(dy * x * rinv, axis=0, keepdims=True)
        da_ref[...], dl_ref[...], ds_ref[...] = outs
        hi_ = lax.broadcasted_iota(jnp.int32, (ATTN_WIDTH, ATTN_WIDTH), 0) // HEAD_DIM
        hj_ = lax.broadcasted_iota(jnp.int32, (ATTN_WIDTH, ATTN_WIDTH), 1) // HEAD_DIM
        same = jnp.where(hi_ == hj_, 1.0, 0.0).astype(f32)
        dlt_ref[...] = jnp.dot(outs[0] * a_ref[...], same, precision=lax.Precision.HIGHEST, preferred_element_type=f32)

    rows = [pl.BlockSpec((rt, hi - lo), lambda i: (i, 0)) for lo, hi in MIX_SPLITS]
    full = pl.BlockSpec((rt, D_MODEL), lambda i: (i, 0))
    return pl.pallas_call(
        body, name="mixnorm_bwd", grid=(S // rt,),
        in_specs=[full] + rows + [pl.BlockSpec((1, D_MODEL), lambda i: (0, 0))],
        out_specs=rows + [rows[0], pl.BlockSpec((SUBLANES, D_MODEL), lambda i: (0, 0))],
        out_shape=[jax.ShapeDtypeStruct((S, hi - lo), f32) for lo, hi in MIX_SPLITS]
        + [jax.ShapeDtypeStruct((S, ATTN_WIDTH), f32), jax.ShapeDtypeStruct((SUBLANES, D_MODEL), f32)],
        compiler_params=_cp("arbitrary"),
    )(dmixed, attn, lru, ssm, g.reshape(1, D_MODEL))


FFN_COL_TILE = 1536


def _ffn_conv(x, prev, w_ref, b_ref, K):
    y = b_ref[...] + w_ref[K - 1:K, :] * x
    for k in range(K - 1):
        y = y + w_ref[k:k + 1, :] * _shift_down(x, prev, K - 1 - k)
    return y


def _ffn_act(up, conv_w, conv_b):
    S, C2 = up.shape
    C = C2 // 2
    K = conv_w.shape[0]
    ct = FFN_COL_TILE
    nct = C // ct
    rt = _pick(S, ROW_TILE, SUBLANES)

    def body(g_ref, gp_ref, v_ref, vp_ref, wg_ref, wv_ref, bg_ref, bv_ref, o_ref):
        first = pl.program_id(1) == 0
        gate = _ffn_conv(g_ref[...], jnp.where(first, 0.0, gp_ref[...]), wg_ref, bg_ref, K)
        val = _ffn_conv(v_ref[...], jnp.where(first, 0.0, vp_ref[...]), wv_ref, bv_ref, K)
        o_ref[...] = (_gelu(gate) * val).astype(o_ref.dtype)

    def specs(off):
        return (pl.BlockSpec((rt, ct), lambda j, i: (i, j + off)), _prev_halo_spec(rt, ct, lambda j, i: j + off))

    def wspec(off, rows):
        return pl.BlockSpec((rows, ct), lambda j, i: (0, j + off))

    g_s, gp_s = specs(0)
    v_s, vp_s = specs(nct)
    return pl.pallas_call(
        body, name="ffn_act", grid=(nct, S // rt),
        in_specs=[g_s, gp_s, v_s, vp_s, wspec(0, K), wspec(nct, K), wspec(0, 1), wspec(nct, 1)],
        out_specs=pl.BlockSpec((rt, ct), lambda j, i: (i, j)),
        out_shape=jax.ShapeDtypeStruct((S, C), bf16), compiler_params=_cp("parallel", "parallel"),
    )(up, up, up, up, conv_w, conv_w, conv_b.reshape(1, C2), conv_b.reshape(1, C2))


def _ffn_act_bwd(dact, up, conv_w, conv_b):
    S, C2 = up.shape
    C = C2 // 2
    K = conv_w.shape[0]
    ct = FFN_COL_TILE
    nct = C // ct
    rt = _pick(S, ROW_TILE, SUBLANES)

    def body(d_ref, g_ref, gp_ref, v_ref, vp_ref, wg_ref, wv_ref, bg_ref, bv_ref, o_ref):
        first = pl.program_id(1) == 0
        gate = _ffn_conv(g_ref[...], jnp.where(first, 0.0, gp_ref[...]), wg_ref, bg_ref, K)
        val = _ffn_conv(v_ref[...], jnp.where(first, 0.0, vp_ref[...]), wv_ref, bv_ref, K)
        d = d_ref[...].astype(f32)
        gl, dgl = _gelu_pair(gate)
        o_ref[0] = (d * val * dgl).astype(o_ref.dtype)
        o_ref[1] = (d * gl).astype(o_ref.dtype)

    def specs(off):
        return (pl.BlockSpec((rt, ct), lambda j, i: (i, j + off)), _prev_halo_spec(rt, ct, lambda j, i: j + off))

    def wspec(off, rows):
        return pl.BlockSpec((rows, ct), lambda j, i: (0, j + off))

    g_s, gp_s = specs(0)
    v_s, vp_s = specs(nct)
    return pl.pallas_call(
        body, name="ffn_act_bwd", grid=(nct, S // rt),
        in_specs=[pl.BlockSpec((rt, ct), lambda j, i: (i, j)), g_s, gp_s, v_s, vp_s,
                  wspec(0, K), wspec(nct, K), wspec(0, 1), wspec(nct, 1)],
        out_specs=pl.BlockSpec((2, rt, ct), lambda j, i: (0, i, j)),
        out_shape=jax.ShapeDtypeStruct((2, S, C), bf16), compiler_params=_cp("parallel", "parallel"),
    )(dact, up, up, up, up, conv_w, conv_w, conv_b.reshape(1, C2), conv_b.reshape(1, C2))


ANY = pl.BlockSpec(memory_space=pl.ANY)


def _rows_for(cols):
    return max(16, (1 << 19) // cols)


def _chips(x, y):
    return [(1 - x, y), (x, 1 - y), (1 - x, 1 - y)]


class _Gather:
    def __init__(self, shards, split):
        self.shapes = [s.shape for s in shards]
        self.dtypes = [s.dtype for s in shards]
        self.split = list(split)
        self.nt = len(shards)
        self.in_specs = [ANY] * self.nt
        self.out_specs = [ANY] * self.nt
        self.out_shape = [jax.ShapeDtypeStruct((4,) + s, dt) for s, dt in zip(self.shapes, self.dtypes)]
        self.scratch = [pltpu.SemaphoreType.DMA((3, self.nt))] * 4 if self.nt else []

    def _part(self, ref, t, half):
        if not self.split[t]:
            return ref
        r = self.shapes[t][0] // 2
        return ref.at[pl.ds(half * r, r), :]

    def _ici(self, ins, outs, sems, k, t, chip, landing_chip):
        x, y, c = lax.axis_index("x"), lax.axis_index("y"), lax.axis_index("c")
        return pltpu.make_async_remote_copy(
            src_ref=self._part(ins[t], t, c), dst_ref=self._part(outs[t].at[landing_chip], t, c),
            send_sem=sems[0].at[k, t], recv_sem=sems[1].at[k, t], device_id=(chip[0], chip[1], c), device_id_type=MESH)

    def _d2d(self, outs, sems, k, t, q, half):
        x, y, c = lax.axis_index("x"), lax.axis_index("y"), lax.axis_index("c")
        rows = self._part(outs[t].at[q], t, half)
        return pltpu.make_async_remote_copy(
            src_ref=rows, dst_ref=rows, send_sem=sems[2].at[k, t], recv_sem=sems[3].at[k, t],
            device_id=(x, y, 1 - c), device_id_type=MESH)

    def start(self, ins, outs, sems):
        x, y = lax.axis_index("x"), lax.axis_index("y")
        me = 2 * x + y
        for k, chip in enumerate(_chips(x, y)):
            for t in range(self.nt):
                self._ici(ins, outs, sems, k, t, chip, me).start()

    def finish(self, ins, outs, sems):
        x, y, c = lax.axis_index("x"), lax.axis_index("y"), lax.axis_index("c")
        me = 2 * x + y
        chips = _chips(x, y)
        for k, chip in enumerate(chips):
            q = 2 * chip[0] + chip[1]
            for t in range(self.nt):
                self._ici(ins, outs, sems, k, t, chip, q).wait_recv()
                if self.split[t]:
                    self._d2d(outs, sems, k, t, q, c).start()
        for k, chip in enumerate(chips):
            q = 2 * chip[0] + chip[1]
            for t in range(self.nt):
                if self.split[t]:
                    self._d2d(outs, sems, k, t, q, 1 - c).wait_recv()
        for k, chip in enumerate(chips):
            q = 2 * chip[0] + chip[1]
            for t in range(self.nt):
                self._ici(ins, outs, sems, k, t, chip, me).wait_send()
                if self.split[t]:
                    self._d2d(outs, sems, k, t, q, c).wait_send()


def _gather_weights(shards, split):
    g = _Gather(shards, split)
    nt = g.nt

    def body(*refs):
        ins, outs, sems = refs[:nt], refs[nt:2 * nt], refs[2 * nt:]
        g.start(ins, outs, sems)
        g.finish(ins, outs, sems)

    return pl.pallas_call(
        body, name="gather_weights", in_specs=g.in_specs, out_specs=g.out_specs, out_shape=g.out_shape,
        scratch_shapes=g.scratch,
    )(*shards)


class _SiblingSend:
    def __init__(self, gs, dst_c, swap=()):
        self.nt, self.n = len(gs), len(gs) + len(swap)
        self.dst_c = list(dst_c) if isinstance(dst_c, (list, tuple)) else [dst_c] * self.nt
        self.in_specs = [ANY] * self.n
        self.out_specs = [ANY] * self.n
        self.out_shape = [jax.ShapeDtypeStruct(g.shape, g.dtype) for g in list(gs) + list(swap)]
        self.scratch = [pltpu.SemaphoreType.DMA((self.n,))] * 2 if self.n else []

    def _each(self, ins, outs, sems, sender, fn):
        x, y, c = lax.axis_index("x"), lax.axis_index("y"), lax.axis_index("c")

        def cp(t):
            return pltpu.make_async_remote_copy(
                src_ref=ins[t], dst_ref=outs[t], send_sem=sems[0].at[t], recv_sem=sems[1].at[t],
                device_id=(x, y, 1 - c), device_id_type=MESH)

        for dst in (0, 1):
            which = [t for t in range(self.nt) if self.dst_c[t] == dst]
            if which:
                @pl.when((c != dst) if sender else (c == dst))
                def _(which=which):
                    for t in which:
                        fn(cp(t))
        for t in range(self.nt, self.n):
            fn(cp(t))

    def start(self, ins, outs, sems):
        self._each(ins, outs, sems, True, lambda cp: cp.start())

    def finish(self, ins, outs, sems):
        self._each(ins, outs, sems, False, lambda cp: cp.wait_recv())
        self._each(ins, outs, sems, True, lambda cp: cp.wait_send())


def _sibling_send(gs, dst_c, swap=()):
    snd = _SiblingSend(gs, dst_c, swap)
    n = snd.n

    def body(*refs):
        ins, outs, sems = refs[:n], refs[n:2 * n], refs[2 * n:]
        snd.start(ins, outs, sems)
        snd.finish(ins, outs, sems)

    res = pl.pallas_call(
        body, name="sibling_send", in_specs=snd.in_specs, out_specs=snd.out_specs, out_shape=snd.out_shape,
        scratch_shapes=snd.scratch,
    )(*gs, *swap)
    return list(res[:snd.nt]), list(res[snd.nt:])


def _owner_flag(owner_c):
    return (lax.axis_index("c") == owner_c).astype(jnp.int32).reshape(1)


def _pair_sum(g, other, name, owner_c, col_slabs=False):
    R, C = g.shape
    cb = C // 4 if col_slabs else C
    rt = _pick(R, _rows_for(cb), 16)

    def body(on_ref, a_ref, o_ref, out_ref):
        out_ref[...] = (a_ref[...] + o_ref[...]).astype(out_ref.dtype)

    row = pl.BlockSpec((rt, cb), lambda q, i, on: (i * on[0], q * on[0]))
    if col_slabs:
        out_spec = pl.BlockSpec((None, rt, cb), lambda q, i, on: (q * on[0], i * on[0], 0))
        out_shape = jax.ShapeDtypeStruct((4, R, cb), bf16)
    else:
        out_spec, out_shape = row, jax.ShapeDtypeStruct((R, C), bf16)
    return pl.pallas_call(
        body, name=name,
        grid_spec=pltpu.PrefetchScalarGridSpec(num_scalar_prefetch=1, grid=(C // cb, R // rt),
                                               in_specs=[row, row], out_specs=out_spec),
        out_shape=out_shape, compiler_params=_cp("arbitrary", "arbitrary"),
    )(_owner_flag(owner_c), g, other)


class _ChipExchange:
    NORTH = 1

    def __init__(self, slabs, whole, only_c):
        self.ns, self.nw = len(slabs), len(whole)
        self.only_c = list(only_c) if isinstance(only_c, (list, tuple)) else [only_c] * self.ns
        self.only_c += [self.NORTH] * self.nw
        self.n = self.ns + self.nw
        self.in_specs = [ANY] * self.n
        self.out_specs = [ANY] * self.n
        self.out_shape = ([jax.ShapeDtypeStruct(s.shape, s.dtype) for s in slabs]
                          + [jax.ShapeDtypeStruct((4,) + w.shape, w.dtype) for w in whole])
        self.scratch = [pltpu.SemaphoreType.DMA((3, self.n))] * 2 if self.n else []
        self.scratch += [pltpu.SemaphoreType.DMA((3, self.nw))] * 2 if self.nw else []

    def _copy(self, ins, outs, sems, k, t, chip, landing_chip):
        c = lax.axis_index("c")
        src = ins[t].at[2 * chip[0] + chip[1]] if t < self.ns else ins[t]
        return pltpu.make_async_remote_copy(
            src_ref=src, dst_ref=outs[t].at[landing_chip], send_sem=sems[0].at[k, t], recv_sem=sems[1].at[k, t],
            device_id=(chip[0], chip[1], c), device_id_type=MESH)

    def _pass_on(self, outs, sems, k, t, chip):
        x, y, c = lax.axis_index("x"), lax.axis_index("y"), lax.axis_index("c")
        rows = outs[t].at[2 * chip[0] + chip[1]]
        return pltpu.make_async_remote_copy(
            src_ref=rows, dst_ref=rows, send_sem=sems[2].at[k, t - self.ns], recv_sem=sems[3].at[k, t - self.ns],
            device_id=(x, y, 1 - c), device_id_type=MESH)

    def _each(self, fn, north=True):
        x, y, c = lax.axis_index("x"), lax.axis_index("y"), lax.axis_index("c")
        chips = _chips(x, y)
        groups = {}
        if north:
            for t in range(self.n):
                groups.setdefault(self.only_c[t], []).append(t)
        elif self.nw:
            groups[1 - self.NORTH] = list(range(self.ns, self.n))
        for owner, which in groups.items():
            @pl.when(c == owner)
            def _(which=which):
                for k, chip in enumerate(chips):
                    for t in which:
                        fn(k, t, chip)

    def start(self, ins, outs, sems):
        me = 2 * lax.axis_index("x") + lax.axis_index("y")
        self._each(lambda k, t, chip: self._copy(ins, outs, sems, k, t, chip, me).start())

    def finish(self, ins, outs, sems):
        me = 2 * lax.axis_index("x") + lax.axis_index("y")

        def landed(k, t, chip):
            self._copy(ins, outs, sems, k, t, chip, 2 * chip[0] + chip[1]).wait_recv()
            if t >= self.ns:
                self._pass_on(outs, sems, k, t, chip).start()

        def sent(k, t, chip):
            self._copy(ins, outs, sems, k, t, chip, me).wait_send()
            if t >= self.ns:
                self._pass_on(outs, sems, k, t, chip).wait_send()

        self._each(landed)
        self._each(lambda k, t, chip: self._pass_on(outs, sems, k, t, chip).wait_recv(), north=False)
        self._each(sent)


def _chip_exchange(slabs, whole, only_c):
    ex = _ChipExchange(slabs, whole, only_c)
    n = ex.n

    def body(*refs):
        ins, outs, sems = refs[:n], refs[n:2 * n], refs[2 * n:]
        ex.start(ins, outs, sems)
        ex.finish(ins, outs, sems)

    res = pl.pallas_call(
        body, name="chip_exchange", in_specs=ex.in_specs, out_specs=ex.out_specs, out_shape=ex.out_shape,
        scratch_shapes=ex.scratch,
    )(*slabs, *whole)
    return list(res[:ex.ns]), list(res[ex.ns:])


def _sum_chips(recv, own, name, owner_c=None):
    n, r, C = recv.shape
    rt = _pick(r, _rows_for(C), 16)
    own3 = own.ndim == 3

    def body(on_ref, r_ref, o_ref, out_ref):
        me = 2 * lax.axis_index("x") + lax.axis_index("y")
        acc = None
        for q in range(n):
            term = jnp.where(me == q, o_ref[q] if own3 else o_ref[...], r_ref[q]).astype(f32)
            acc = term if acc is None else acc + term
        out_ref[...] = acc

    blk = pl.BlockSpec((n, rt, C), lambda i, on: (0, i * on[0], 0))
    row = pl.BlockSpec((rt, C), lambda i, on: (i * on[0], 0))
    flag = jnp.ones((1,), jnp.int32) if owner_c is None else _owner_flag(owner_c)
    return pl.pallas_call(
        body, name=name,
        grid_spec=pltpu.PrefetchScalarGridSpec(num_scalar_prefetch=1, grid=(r // rt,),
                                               in_specs=[blk, blk if own3 else row], out_specs=row),
        out_shape=jax.ShapeDtypeStruct((r, C), f32), compiler_params=_cp("arbitrary"),
    )(flag, recv, own)


def _adamw_layers(mine, theirs, owners, w, m, v, name):
    L, r, C = w.shape
    rt = _pick(r, _rows_for(C), 16)

    def body(own_ref, a0_ref, a1_ref, b0_ref, b1_ref, w_ref, m_ref, v_ref, g_ref, d_ref, mo_ref, vo_ref):
        layer = pl.program_id(0)
        g0 = jnp.where(own_ref[0] == 1, a0_ref[...], b0_ref[...])
        g1 = jnp.where(own_ref[1] == 1, a1_ref[...], b1_ref[...])
        g_ref[...] = jnp.where(layer == 0, g0, g1)
        _adamw_math(g_ref, w_ref, m_ref, v_ref, d_ref, mo_ref, vo_ref)

    def flat(layer, mine_side):
        def index(l, i, own):
            use = (l == layer) & (own[layer] == (1 if mine_side else 0))
            return (jnp.where(use, i, 0), 0)
        return pl.BlockSpec((rt, C), index)

    lay = pl.BlockSpec((None, rt, C), lambda l, i, own: (l, i, 0))
    c = lax.axis_index("c")
    own = jnp.stack([(c == owners[0]).astype(jnp.int32), (c == owners[1]).astype(jnp.int32)])
    return pl.pallas_call(
        body, name=name,
        grid_spec=pltpu.PrefetchScalarGridSpec(
            num_scalar_prefetch=1, grid=(L, r // rt),
            in_specs=[flat(0, True), flat(1, True), flat(0, False), flat(1, False)] + [lay] * 3, out_specs=[lay] * 4),
        out_shape=[jax.ShapeDtypeStruct((L, r, C), f32)] * 4, compiler_params=_cp("arbitrary", "arbitrary"),
    )(own, mine[0], mine[1], theirs[0], theirs[1], w, m, v)


def _adamw_math(g_ref, w_ref, m_ref, v_ref, d_ref, mo_ref, vo_ref):
    gg = g_ref[...]
    m_new = ADAM_B1 * m_ref[...] + (1.0 - ADAM_B1) * gg
    v_new = ADAM_B2 * v_ref[...] + (1.0 - ADAM_B2) * (gg * gg)
    m_hat = m_new / (1.0 - ADAM_B1 ** ADAM_STEP)
    v_hat = v_new / (1.0 - ADAM_B2 ** ADAM_STEP)
    d_ref[...] = -ADAM_LR * (m_hat / (jnp.sqrt(v_hat) + ADAM_EPS) + ADAM_WD * w_ref[...])
    mo_ref[...] = m_new
    vo_ref[...] = v_new


FLAT_TILE = 2048


def _add2(a, b, name):
    R = a.shape[0]
    rt = _pick(R, FLAT_TILE, SUBLANES)

    def body(a_ref, b_ref, o_ref):
        o_ref[...] = a_ref[...] + b_ref[...]

    row = pl.BlockSpec((rt, LANES), lambda i: (i, 0))
    return pl.pallas_call(
        body, name=name, grid=(R // rt,), in_specs=[row, row], out_specs=row,
        out_shape=jax.ShapeDtypeStruct((R, LANES), f32), compiler_params=_cp("parallel"),
    )(a, b)


def _adamw(g, w, m, v, name):
    R = g.shape[0]
    rt = _pick(R, FLAT_TILE, SUBLANES)

    def body(g_ref, w_ref, m_ref, v_ref, d_ref, mo_ref, vo_ref):
        _adamw_math(g_ref, w_ref, m_ref, v_ref, d_ref, mo_ref, vo_ref)

    row = pl.BlockSpec((rt, LANES), lambda i: (i, 0))
    return pl.pallas_call(
        body, name=name, grid=(R // rt,), in_specs=[row] * 4, out_specs=[row] * 3,
        out_shape=[jax.ShapeDtypeStruct((R, LANES), f32)] * 3, compiler_params=_cp("parallel"),
    )(g, w, m, v)


def _adamw_whole(g, w, m, v, name):
    def body(g_ref, w_ref, m_ref, v_ref, d_ref, mo_ref, vo_ref):
        _adamw_math(g_ref, w_ref, m_ref, v_ref, d_ref, mo_ref, vo_ref)

    return pl.pallas_call(
        body, name=name, out_shape=[jax.ShapeDtypeStruct(w.shape, f32)] * 3, compiler_params=_cp(),
    )(g, w, m, v)


def _pack(arrs, dtype, rows=None):
    flat = jnp.concatenate([a.astype(dtype).reshape(-1) for a in arrs])
    per = FLAT_TILE * LANES
    total = rows * LANES if rows else flat.shape[0] + (-flat.shape[0]) % per
    flat = jnp.pad(flat, (0, total - flat.shape[0]))
    return flat.reshape(-1, LANES)


def _unpack(buf, shapes):
    flat = buf.reshape(-1)
    out, off = [], 0
    for s in shapes:
        n = math.prod(s)
        out.append(flat[off:off + n].reshape(s))
        off += n
    return out


def _block_diag(w):
    n, a, b = w.shape
    eye = jnp.eye(n, dtype=w.dtype)
    return (w[:, :, None, :] * eye[:, None, :, None]).reshape(n * a, n * b)


def _diag_blocks(m, n):
    a, b = m.shape[0] // n, m.shape[1] // n
    idx = jnp.arange(n)
    return m.reshape(n, a, n, b)[idx, :, idx, :]


BIG = ("w_in", "w_out", "w_up", "w_down", "s5_w_glu")
BIG_COL_SHARDED = {"w_in": True, "w_out": False, "w_up": True, "w_down": False, "s5_w_glu": False}
CONV_SHARDED = ("lru_conv_w", "ffn_conv_w")
SMALL = ("lru_conv_b", "lru_wr", "lru_br", "lru_wi", "lru_bi", "lru_lambda", "s5_a_re", "s5_a_im", "s5_b_re",
         "s5_b_im", "s5_c_re", "s5_c_im", "s5_d", "s5_log_step", "s5_b_glu", "mix_norm_g", "ln1_g", "ln1_b",
         "ffn_conv_b", "ln2_g", "ln2_b")
WEIGHTS = ("w_in", "lru_conv_w", "lru_conv_b", "lru_wr", "lru_br", "lru_wi", "lru_bi", "lru_lambda", "s5_a_re",
           "s5_a_im", "s5_b_re", "s5_b_im", "s5_c_re", "s5_c_im", "s5_d", "s5_log_step", "s5_w_glu", "s5_b_glu",
           "mix_norm_g", "w_out", "ln1_g", "ln1_b", "w_up", "ffn_conv_w", "ffn_conv_b", "w_down", "ln2_g", "ln2_b")


def _assemble(slabs, col_sharded):
    _, L, r, c = slabs.shape
    if col_sharded:
        return slabs.transpose(1, 2, 0, 3).reshape(L, r, 4 * c)
    return slabs.transpose(1, 0, 2, 3).reshape(L, 4 * r, c)


def _s5_prepare(p):
    G = N_S5_GROUPS
    bt_re, bt_im = p["s5_b_re"].transpose(0, 2, 1), p["s5_b_im"].transpose(0, 2, 1)
    ls = p["s5_log_step"].reshape(G, 1)
    ab_re, ab_im, bb_re, bb_im = _s5_params(p["s5_a_re"], p["s5_a_im"], ls, bt_re, bt_im)
    ab = jnp.concatenate([ab_re.reshape(1, S5_LANES), ab_im.reshape(1, S5_LANES)], axis=1)
    bbcat = jnp.concatenate([_block_diag(bb_re), _block_diag(bb_im)], axis=1).astype(bf16)
    ccat = jnp.concatenate([_block_diag(p["s5_c_re"].transpose(0, 2, 1)),
                            -_block_diag(p["s5_c_im"].transpose(0, 2, 1))], axis=0).astype(bf16)
    bbcat_pad = jnp.concatenate([bbcat, jnp.zeros((LRU_WIDTH - S5_WIDTH, 2 * S5_LANES), bf16)], axis=0)
    return dict(bt_re=bt_re, bt_im=bt_im, ls=ls, ab=ab, bbcat=bbcat, bbcat_pad=bbcat_pad, ccat=ccat)


def _layer_fwd(h, p, cos, sin, pending, install):
    sv = {"h": h}
    proj = _mm(h, p["w_in"], "nn", "mm_proj", tn=D_IN_PAD)
    sv.update(proj=proj)
    qr, kr = _rope_fwd(proj, cos, sin)
    attn, ltot, gathered = _attn_fwd2(qr, kr, proj, [s for _, _, s in pending], [True] * len(pending))
    install(pending, gathered)
    sv.update(qr=qr, kr=kr, attn=attn, ltot=ltot)
    wr, wi = _block_diag(p["lru_wr"]).astype(bf16), _block_diag(p["lru_wi"]).astype(bf16)
    xc, r, i, log_a, u = _lru_pre(proj, p["lru_conv_w"], p["lru_conv_b"], wr, p["lru_br"], wi, p["lru_bi"],
                                  p["lru_lambda"])
    hl, lru = _lru_scan(log_a, u, proj)
    sv.update(wr=wr, wi=wi, xc=xc, r=r, i=i, log_a=log_a, hl=hl, lru=lru)
    s5 = _s5_prepare(p)
    bu = _mm(proj, s5["bbcat_pad"], "nn", "mm_s5_bu", a_win=(PROJ_S5_U, LRU_WIDTH))
    xs = _s5_scan(bu, s5["ab"])
    yc = _mm(xs, s5["ccat"], "nn", "mm_s5_y")
    ssm, y = _s5_out(yc, proj, p["s5_d"].reshape(-1), p["s5_w_glu"], p["s5_b_glu"])
    sv.update(s5=s5, xs=xs, y=y, ssm=ssm)
    mixed = _mixnorm(attn, lru, ssm, p["mix_norm_g"])
    mix = _mm(mixed, p["w_out"], "nn", "mm_out")
    h1, z1 = _ln_fwd(h, mix, p["ln1_g"], p["ln1_b"], "ln_fwd")
    sv.update(mixed=mixed, z1=z1, h1=h1)
    up = _mm(h1, p["w_up"], "nn", "mm_up", tn=1536)
    act = _ffn_act(up, p["ffn_conv_w"], p["ffn_conv_b"])
    ffn = _mm(act, p["w_down"], "nn", "mm_down")
    h2, z2 = _ln_fwd(h1, ffn, p["ln2_g"], p["ln2_b"], "ln_fwd")
    sv.update(up=up, act=act, z2=z2)
    return h2, sv


def _layer_bwd(dy_a, dy_b, p, sv, cos, sin, relay=None):
    gr = {}
    dz2, acc = _ln_bwd(dy_a, dy_b, sv["z2"], p["ln2_g"], "ln_bwd_top" if dy_a is None else "ln_bwd")
    gr["ln2_g"], gr["ln2_b"] = acc[0], acc[1]
    dact = _mm(dz2, p["w_down"], "nt", "mm_dact", out_dtype=bf16)
    gr["w_down"] = _mm(sv["act"], dz2, "tn", "mm_dw_down")
    dupc = _ffn_act_bwd(dact, sv["up"], p["ffn_conv_w"], p["ffn_conv_b"])
    others, others_dst, to_slabs = relay if relay else ((), 0, None)
    send = list(others) + ([gr["w_down"]] if relay else [])
    send_dst = [others_dst] * len(others) + ([EARLY_OWNER] if relay else [])
    dup, acc, from_sibling = _conv_bwd(dupc, sv["up"], p["ffn_conv_w"], "ffn_conv_bwd", col_tile=FFN_COL_TILE,
                                       out_dtype=bf16, send=send, send_dst=send_dst)
    slabs, owners = to_slabs(from_sibling[:-1], gr["w_down"], from_sibling[-1]) if relay else ((), 0)
    gr["ffn_conv_w"], gr["ffn_conv_b"] = acc[0:3], acc[3]
    dh1 = _mm(dup, p["w_up"], "nt", "mm_dh1", tk=2048)
    gr["w_up"] = _mm(sv["h1"], dup, "tn", "mm_dw_up", tn=1536)
    dz1, acc = _ln_bwd(dz2, dh1, sv["z1"], p["ln1_g"], "ln_bwd")
    gr["ln1_g"], gr["ln1_b"] = acc[0], acc[1]
    dmixed = _mm(dz1, p["w_out"], "nt", "mm_dmixed")
    gr["w_out"] = _mm(sv["mixed"], dz1, "tn", "mm_dw_out")
    dattn, dlru, dssm, delta, acc = _mixnorm_bwd(dmixed, sv["attn"], sv["lru"], sv["ssm"], p["mix_norm_g"])
    gr["mix_norm_g"] = acc[0]
    proj = sv["proj"]
    late = [gr["w_up"]] if relay else []
    dqr, dkr, dv, received, late_sibling = _attn_bwd2(sv["qr"], sv["kr"], proj, dattn, sv["ltot"], delta, slabs,
                                                      owners, late, EARLY_OWNER)
    dq, dk = _rope_bwd(dqr, dkr, cos, sin)
    g, dgate = _lru_scan_bwd(dlru, proj, sv["hl"], sv["log_a"])
    dxc, dwr, dwi, acc = _lru_gate_bwd(g, sv["hl"], sv["xc"], sv["r"], sv["i"], sv["log_a"], sv["wr"], sv["wi"],
                                       p["lru_lambda"])
    gr["lru_wr"], gr["lru_wi"] = _diag_blocks(dwr, N_LRU_HEADS), _diag_blocks(dwi, N_LRU_HEADS)
    gr["lru_br"], gr["lru_bi"], gr["lru_lambda"] = acc[0], acc[1], acc[2]
    dxr, acc, _ = _conv_bwd(dxc, proj, p["lru_conv_w"], "lru_conv_bwd", x_col_block=PROJ_LRU_X)
    gr["lru_conv_w"], gr["lru_conv_b"] = acc[0:4], acc[4]
    s5 = sv["s5"]
    G = N_S5_GROUPS
    dy, du_direct, dwglu, acc = _s5_out_bwd(dssm, sv["y"], proj, p["s5_d"].reshape(-1), p["s5_w_glu"],
                                            p["s5_b_glu"])
    gr["s5_w_glu"], gr["s5_b_glu"], gr["s5_d"] = dwglu, acc[0], acc[1].reshape(G, S5_GROUP)
    dxs = _mm(dy, s5["ccat"], "nt", "mm_s5_dx")
    dccat = _mm(sv["xs"], dy, "tn", "mm_s5_dc")
    gr["s5_c_re"] = _diag_blocks(dccat[:S5_LANES], G).transpose(0, 2, 1)
    gr["s5_c_im"] = -_diag_blocks(dccat[S5_LANES:], G).transpose(0, 2, 1)
    gs, dab = _s5_scan_bwd(dxs, sv["xs"], s5["ab"])
    du = _mm(gs, s5["bbcat"], "nt", "mm_s5_du", add=du_direct)
    dbbcat = _mm(proj, gs, "tn", "mm_s5_dbb", a_win=(PROJ_S5_U, LRU_WIDTH))[:S5_WIDTH]
    d_ar, d_ai, d_ls, d_btr, d_bti = _s5_params_bwd(
        p["s5_a_re"], p["s5_a_im"], s5["ls"], s5["bt_re"], s5["bt_im"],
        dab[:, :S5_LANES].reshape(G, S5_STATE), dab[:, S5_LANES:].reshape(G, S5_STATE),
        _diag_blocks(dbbcat[:, :S5_LANES], G), _diag_blocks(dbbcat[:, S5_LANES:], G))
    gr["s5_a_re"], gr["s5_a_im"], gr["s5_log_step"] = d_ar, d_ai, d_ls.reshape(G)
    gr["s5_b_re"], gr["s5_b_im"] = d_btr.transpose(0, 2, 1), d_bti.transpose(0, 2, 1)
    pad = jnp.zeros((du.shape[0], D_IN_PAD - D_IN), f32)
    dproj = jnp.concatenate([dq, dk, dv, dxr, dgate, du, pad], axis=1).astype(bf16)
    gr["w_in"] = _mm(sv["h"], dproj, "tn", "mm_dw_in", tn=768)[:, :D_IN]
    dh = _mm(dproj, p["w_in"], "nt", "mm_dh")
    return (dz1, dh, gr, slabs, received, late_sibling) if relay else (dz1, dh, gr)


def _train_step(d):
    x, target = d["x"][0], d["loss_target"][0]
    S = x.shape[0]
    me = 2 * lax.axis_index("x") + lax.axis_index("y")

    def rows2d(a):
        return a.reshape(a.shape[0] * a.shape[1], a.shape[2])

    params = [{n: d[n][l] for n in SMALL} for l in range(DEPTH)]

    def install(items, gathered):
        for (n, l, mine), g in zip(items, gathered):
            g = lax.dynamic_update_slice_in_dim(g, mine[None], me, axis=0)
            if n in CONV_SHARDED:
                full = _assemble(g.reshape((4,) + d[n].shape), True)
                for k in range(DEPTH):
                    params[k][n] = full[k]
                continue
            full = _assemble(g[:, None], BIG_COL_SHARDED[n])[0]
            if n == "w_in":
                full = jnp.pad(full, ((0, 0), (0, D_IN_PAD - D_IN)))
            params[l][n] = full

    def shard(n, l):
        return (n, l, d[n][l].astype(bf16))

    first = [shard("w_in", 0)] + [(n, None, rows2d(d[n])) for n in CONV_SHARDED]
    install(first, _gather_weights([s for _, _, s in first], [True] + [False] * len(CONV_SHARDED)))
    later = [[shard(n, 0) for n in BIG[1:]] + [shard("w_in", 1)], [shard(n, 1) for n in BIG[1:]]]

    cos, sin = _rope_tables(S)
    h, saved = x, []
    for l in range(DEPTH):
        h, sv = _layer_fwd(h, params[l], cos, sin, later[l], install)
        saved.append(sv)
    dy, loss_acc = _loss_head(h, target)
    def slab(n, g, other, owner):
        aligned = BIG_COL_SHARDED[n] and (g.shape[1] // 4) % LANES == 0
        p = _pair_sum(g, other, "pair_sum_" + n, owner, col_slabs=aligned)
        if BIG_COL_SHARDED[n] and not aligned:
            return p.reshape(p.shape[0], 4, p.shape[1] // 4).transpose(1, 0, 2)
        return p if aligned else p.reshape(4, p.shape[0] // 4, p.shape[1])

    own1 = {n: 1 - EARLY_OWNER for n in BIG}
    own0 = {n: (EARLY_OWNER if n in ("w_down", "w_up") else 1 - EARLY_OWNER) for n in BIG}

    def hidden_slabs(others1, w_down0, w_down0_sibling):
        slabs = [slab(n, grads[1][n], o, own1[n]) for n, o in zip(BIG, others1)]
        slabs.append(slab("w_down", w_down0, w_down0_sibling, own0["w_down"]))
        return slabs, [own1[n] for n in BIG] + [own0["w_down"]]

    da, db, grads = None, dy, [None] * DEPTH
    da, db, grads[1] = _layer_bwd(da, db, params[1], saved[1], cos, sin)
    relay = ([grads[1][n] for n in BIG], 1 - EARLY_OWNER, hidden_slabs)
    da, db, grads[0], hslabs, hrecv, (w_up0_sibling,) = _layer_bwd(da, db, params[0], saved[0], cos, sin, relay)
    out = {"grad_x": _axpy(da, db, "grad_x")[None]}

    small = SMALL + CONV_SHARDED
    sp = _pack([grads[l][n] for n in small for l in range(DEPTH)], f32)
    tail = [n for n in BIG if n != "w_down"]
    rest = [n for n in tail if n != "w_up"]
    others_rest, (sp_sibling,) = _sibling_send([grads[0][n] for n in rest], [own0[n] for n in rest], [sp])
    sibling0 = dict(zip(rest, others_rest), w_up=w_up0_sibling)
    tslabs = [slab(n, grads[0][n], sibling0[n], own0[n]) for n in tail]
    chip_small = _add2(sp, sp_sibling, "pair_sum_small")
    trecv, (recv_small,) = _chip_exchange(tslabs, [chip_small], [own0[n] for n in tail])
    mine0 = {n: _sum_chips(r, s, "sum_chips_" + n, own0[n]) for n, r, s in zip(tail, trecv, tslabs)}
    mine0["w_down"] = _sum_chips(hrecv[-1], hslabs[-1], "sum_chips_w_down", own0["w_down"])
    mine1 = {n: _sum_chips(r, s, "sum_chips_" + n, own1[n]) for n, r, s in zip(BIG, hrecv, hslabs)}
    sent, _ = _sibling_send([mine0[n] for n in BIG] + [mine1[n] for n in BIG],
                            [1 - own0[n] for n in BIG] + [1 - own1[n] for n in BIG])
    theirs0, theirs1 = dict(zip(BIG, sent[:len(BIG)])), dict(zip(BIG, sent[len(BIG):]))
    for n in BIG:
        upd = _adamw_layers((mine0[n], mine1[n]), (theirs0[n], theirs1[n]), (own0[n], own1[n]),
                            d[n], d["m_" + n], d["v_" + n], "adamw_" + n)
        for pre, u in zip(("grad_", "delta_", "new_m_", "new_v_"), upd):
            out[pre + n] = u

    total = _sum_chips(recv_small, chip_small, "sum_chips_small")
    rows = total.shape[0]
    upd = _adamw(total, _pack([d[n] for n in SMALL], f32, rows), _pack([d["m_" + n] for n in SMALL], f32, rows),
                 _pack([d["v_" + n] for n in SMALL], f32, rows), "adamw_small")
    small_shapes = [d[n].shape for n in SMALL]
    for pre, buf in zip(("grad_", "delta_", "new_m_", "new_v_"), (total,) + tuple(upd)):
        for n, a in zip(SMALL, _unpack(buf, small_shapes)):
            out[pre + n] = a
    conv_full = _unpack(total, small_shapes + [(DEPTH,) + grads[0][n].shape for n in CONV_SHARDED])
    for n, g in zip(CONV_SHARDED, conv_full[len(SMALL):]):
        L, K, C = g.shape
        g = lax.dynamic_index_in_dim(g.reshape(L, K, 4, C // 4), me, axis=2, keepdims=False)
        out["grad_" + n] = g
        for pre, u in zip(("delta_", "new_m_", "new_v_"), _adamw_whole(g, d[n], d["m_" + n], d["v_" + n], "adamw_" + n)):
            out[pre + n] = u

    loss_local, _ = lax.optimization_barrier((loss_acc[0, 0], upd[0]))
    out["loss"] = lax.psum(loss_local, ("x", "y", "c"))
    return (out["loss"], out["grad_x"]) + tuple(out[pre + n] for pre in ("grad_", "delta_", "new_m_", "new_v_")
                                                for n in WEIGHTS)


def kernel(
        x, w_in, lru_conv_w, lru_conv_b, lru_wr, lru_br, lru_wi, lru_bi, lru_lambda, s5_a_re, s5_a_im, s5_b_re,
        s5_b_im, s5_c_re, s5_c_im, s5_d, s5_log_step, s5_w_glu, s5_b_glu, mix_norm_g, w_out, ln1_g, ln1_b, w_up,
        ffn_conv_w, ffn_conv_b, w_down, ln2_g, ln2_b, loss_target, m_w_in, m_lru_conv_w, m_lru_conv_b, m_lru_wr,
        m_lru_br, m_lru_wi, m_lru_bi, m_lru_lambda, m_s5_a_re, m_s5_a_im, m_s5_b_re, m_s5_b_im, m_s5_c_re,
        m_s5_c_im, m_s5_d, m_s5_log_step, m_s5_w_glu, m_s5_b_glu, m_mix_norm_g, m_w_out, m_ln1_g, m_ln1_b,
        m_w_up, m_ffn_conv_w, m_ffn_conv_b, m_w_down, m_ln2_g, m_ln2_b, v_w_in, v_lru_conv_w, v_lru_conv_b,
        v_lru_wr, v_lru_br, v_lru_wi, v_lru_bi, v_lru_lambda, v_s5_a_re, v_s5_a_im, v_s5_b_re, v_s5_b_im,
        v_s5_c_re, v_s5_c_im, v_s5_d, v_s5_log_step, v_s5_w_glu, v_s5_b_glu, v_mix_norm_g, v_w_out, v_ln1_g,
        v_ln1_b, v_w_up, v_ffn_conv_w, v_ffn_conv_b, v_w_down, v_ln2_g, v_ln2_b
):
    return _train_step(dict(locals()))
```

```python
import math

import jax
import jax.numpy as jnp
from jax import lax
from jax.experimental import pallas as pl
from jax.experimental.pallas import tpu as pltpu

f32 = jnp.float32
bf16 = jnp.bfloat16
MESH = pl.DeviceIdType.MESH

D_MODEL = 1024
ATTN_WIDTH = 384
LRU_WIDTH = 384
S5_WIDTH = 256
HEAD_DIM = 64
N_LRU_HEADS = 6
N_S5_GROUPS = 16
S5_GROUP = 16
S5_STATE = 64
S5_LANES = N_S5_GROUPS * S5_STATE
D_FF = 3072
D_IN = 2176
LRU_C = 8.0
ROPE_THETA = 10000.0
DILATIONS = (1, 4, 16)
ATTN_BLOCK = 128
DEPTH = 2
ALPHA = (2 * DEPTH) ** 0.25
LN_EPS = 1e-5
RMS_EPS = 1e-6
ADAM_LR, ADAM_B1, ADAM_B2, ADAM_EPS, ADAM_WD, ADAM_STEP = 0.001, 0.9, 0.999, 1e-08, 0.01, 10

SUBLANES = 8
LANES = 128
VMEM_LIMIT = 56 * 1024 * 1024
ROW_TILE = 512
MM_SINGLE_K = 3072
D_IN_PAD = 2304
NEG = -1e30


def _cp(*sem):
    return pltpu.CompilerParams(dimension_semantics=sem if sem else None, vmem_limit_bytes=VMEM_LIMIT)


def _pick(dim, pref, align=LANES):
    if dim <= pref:
        return dim
    t = (pref // align) * align
    while t >= align:
        if dim % t == 0:
            return t
        t -= align
    return dim


def _gelu(x):
    return jax.nn.gelu(x)


def _gelu_grad(x):
    c = math.sqrt(2.0 / math.pi)
    t = jnp.tanh(c * (x + 0.044715 * x * x * x))
    return 0.5 * (1.0 + t) + 0.5 * x * (1.0 - t * t) * c * (1.0 + 3 * 0.044715 * x * x)


def _gelu_pair(x):
    c = math.sqrt(2.0 / math.pi)
    x2 = x * x
    t = jnp.tanh(c * x * (1.0 + 0.044715 * x2))
    return 0.5 * x * (1.0 + t), 0.5 * (1.0 + t) + 0.5 * x * (1.0 - t * t) * c * (1.0 + 3 * 0.044715 * x2)


def _sigmoid(x):
    return jax.nn.sigmoid(x)


def _expm1(x):
    p = 1.0 + x / 9.0
    for n in (8.0, 7.0, 6.0, 5.0, 4.0, 3.0, 2.0):
        p = 1.0 + (x / n) * p
    return jnp.where(jnp.abs(x) < 0.3, x * p, jnp.exp(x) - 1.0)


def _dot(a, b, dims):
    return lax.dot_general(a, b, (dims, ((), ())), preferred_element_type=f32)


NN = ((1,), (0,))
NT = ((1,), (1,))
TN = ((0,), (0,))


def _mm(a, b, mode, name, out_dtype=f32, tm=1024, tn=1024, tk=1024, add=None, a_win=None):
    if mode == "nn":
        (M, K), N = a.shape, b.shape[1]
    elif mode == "nt":
        (M, K), N = a.shape, b.shape[0]
    else:
        (K, M), N = a.shape, b.shape[1]
    win = 0
    if a_win is not None:
        win, w = a_win
        if mode == "tn":
            M, tm = w, w
        else:
            K = w
    single = mode != "tn" and K <= MM_SINGLE_K
    tm, tn = _pick(M, tm), _pick(N, tn)
    tk = K if single else _pick(K, tk)
    nk = K // tk
    dims = {"nn": NN, "nt": NT, "tn": TN}[mode]

    def body(a_ref, b_ref, *rest):
        prod = _dot(a_ref[...].astype(bf16), b_ref[...].astype(bf16), dims)
        if single:
            o_ref = rest[-1]
            o_ref[...] = (prod if add is None else prod + rest[0][...]).astype(o_ref.dtype)
            return
        o_ref, acc = rest[-2:]
        k = pl.program_id(2)

        @pl.when(k == 0)
        def _():
            acc[...] = prod if add is None else prod + rest[0][...]

        @pl.when(k > 0)
        def _():
            acc[...] += prod

        @pl.when(k == nk - 1)
        def _():
            o_ref[...] = acc[...].astype(o_ref.dtype)

    if mode == "tn":
        a_spec = pl.BlockSpec((tk, tm), lambda i, j, k: (k, i + win))
    else:
        a_spec = pl.BlockSpec((tm, tk), lambda i, j, k: (i, k + win))
    if mode == "nt":
        b_spec = pl.BlockSpec((tn, tk), lambda i, j, k: (j, k))
    else:
        b_spec = pl.BlockSpec((tk, tn), lambda i, j, k: (k, j))
    o_spec = pl.BlockSpec((tm, tn), lambda i, j, k: (i, j))
    return pl.pallas_call(
        body, name=name, grid=(M // tm, N // tn, nk),
        in_specs=[a_spec, b_spec] + ([] if add is None else [o_spec]), out_specs=o_spec,
        out_shape=jax.ShapeDtypeStruct((M, N), out_dtype),
        scratch_shapes=[] if single else [pltpu.VMEM((tm, tn), f32)],
        compiler_params=_cp("parallel", "parallel", "arbitrary"),
    )(*((a, b) if add is None else (a, b, add)))


def _shift_down(cur, prev8, k):
    if k == 0:
        return cur
    T, (R, C) = SUBLANES, cur.shape
    rot = pltpu.roll(cur.reshape(R // T, T, C), k, 1)
    before = jnp.concatenate([pltpu.roll(prev8, k, 0)[None], rot[:-1]], axis=0)
    row = lax.broadcasted_iota(jnp.int32, (R // T, T, C), 1)
    return jnp.where(row < k, before, rot).reshape(R, C)


def _shift_up(cur, next8, k):
    if k == 0:
        return cur
    T, (R, C) = SUBLANES, cur.shape
    rot = pltpu.roll(cur.reshape(R // T, T, C), T - k, 1)
    after = jnp.concatenate([rot[1:], pltpu.roll(next8, T - k, 0)[None]], axis=0)
    row = lax.broadcasted_iota(jnp.int32, (R // T, T, C), 1)
    return jnp.where(row < T - k, rot, after).reshape(R, C)


def _prev_halo_spec(rt, cols, ncolblk_fn):
    per = rt // SUBLANES
    return pl.BlockSpec((SUBLANES, cols), lambda *g: (jnp.maximum(g[-1] * per - 1, 0), ncolblk_fn(*g)))


def _ln_fwd(h, branch, g, b, name):
    S, D = h.shape
    rt = _pick(S, ROW_TILE, SUBLANES)

    def body(h_ref, m_ref, g_ref, b_ref, o_ref, z_ref):
        z = ALPHA * h_ref[...] + m_ref[...]
        mu = jnp.mean(z, axis=-1, keepdims=True)
        zc = z - mu
        var = jnp.mean(zc * zc, axis=-1, keepdims=True)
        o_ref[...] = zc * lax.rsqrt(var + LN_EPS) * g_ref[...] + b_ref[...]
        z_ref[...] = z

    row = pl.BlockSpec((rt, D), lambda i: (i, 0))
    vec = pl.BlockSpec((1, D), lambda i: (0, 0))
    return pl.pallas_call(
        body, name=name, grid=(S // rt,), in_specs=[row, row, vec, vec], out_specs=[row, row],
        out_shape=[jax.ShapeDtypeStruct((S, D), f32)] * 2, compiler_params=_cp("parallel"),
    )(h, branch, g.reshape(1, D), b.reshape(1, D))


def _ln_bwd(dy_a, dy_b, z, g, name):
    S, D = z.shape
    rt = _pick(S, ROW_TILE, SUBLANES)
    two = dy_a is not None

    def body(*refs):
        if two:
            a_ref, b_ref, z_ref, g_ref, dz_ref, acc_ref = refs
            dy = ALPHA * a_ref[...] + b_ref[...]
        else:
            b_ref, z_ref, g_ref, dz_ref, acc_ref = refs
            dy = b_ref[...]
        z = z_ref[...]
        mu = jnp.mean(z, axis=-1, keepdims=True)
        zc = z - mu
        var = jnp.mean(zc * zc, axis=-1, keepdims=True)
        rstd = lax.rsqrt(var + LN_EPS)
        xhat = zc * rstd
        dxh = dy * g_ref[...]
        m1 = jnp.mean(dxh, axis=-1, keepdims=True)
        m2 = jnp.mean(dxh * xhat, axis=-1, keepdims=True)
        dz_ref[...] = rstd * (dxh - m1 - xhat * m2)

        @pl.when(pl.program_id(0) == 0)
        def _():
            acc_ref[...] = jnp.zeros_like(acc_ref)

        acc_ref[0:1, :] += jnp.sum(dy * xhat, axis=0, keepdims=True)
        acc_ref[1:2, :] += jnp.sum(dy, axis=0, keepdims=True)

    row = pl.BlockSpec((rt, D), lambda i: (i, 0))
    vec = pl.BlockSpec((1, D), lambda i: (0, 0))
    acc = pl.BlockSpec((SUBLANES, D), lambda i: (0, 0))
    ins = ([dy_a] if two else []) + [dy_b, z, g.reshape(1, D)]
    return pl.pallas_call(
        body, name=name, grid=(S // rt,), in_specs=[row] * (len(ins) - 1) + [vec], out_specs=[row, acc],
        out_shape=[jax.ShapeDtypeStruct((S, D), f32), jax.ShapeDtypeStruct((SUBLANES, D), f32)],
        compiler_params=_cp("arbitrary"),
    )(*ins)


def _loss_head(y, target):
    S, D = y.shape
    rt = _pick(S, ROW_TILE, SUBLANES)

    def body(y_ref, t_ref, dy_ref, acc_ref):
        e = y_ref[...] - t_ref[...]
        dy_ref[...] = e * (1.0 / D)

        @pl.when(pl.program_id(0) == 0)
        def _():
            acc_ref[...] = jnp.zeros_like(acc_ref)

        part = jnp.sum(jnp.mean(e * e, axis=-1, keepdims=True), axis=0, keepdims=True)
        acc_ref[...] += 0.5 * part

    row = pl.BlockSpec((rt, D), lambda i: (i, 0))
    return pl.pallas_call(
        body, name="loss_head", grid=(S // rt,), in_specs=[row, row],
        out_specs=[row, pl.BlockSpec((1, 1), lambda i: (0, 0))],
        out_shape=[jax.ShapeDtypeStruct((S, D), f32), jax.ShapeDtypeStruct((1, 1), f32)],
        compiler_params=_cp("arbitrary"),
    )(y, target)


def _axpy(a, b, name):
    S, D = a.shape
    rt = _pick(S, ROW_TILE, SUBLANES)

    def body(a_ref, b_ref, o_ref):
        o_ref[...] = ALPHA * a_ref[...] + b_ref[...]

    row = pl.BlockSpec((rt, D), lambda i: (i, 0))
    return pl.pallas_call(
        body, name=name, grid=(S // rt,), in_specs=[row, row], out_specs=row,
        out_shape=jax.ShapeDtypeStruct((S, D), f32), compiler_params=_cp("parallel"),
    )(a, b)


def _rope_tables(S):
    rt = _pick(S, ROW_TILE, SUBLANES)

    def body(c_ref, s_ref):
        pos = (pl.program_id(0) * rt + lax.broadcasted_iota(jnp.int32, (rt, LANES), 0)).astype(f32)
        lane = lax.broadcasted_iota(jnp.int32, (rt, LANES), 1)
        j = (lane % (HEAD_DIM // 2)).astype(f32)
        inv = jnp.exp((-j * 2.0 / HEAD_DIM) * math.log(ROPE_THETA))
        ang = pos * inv
        c = jnp.cos(ang)
        s = jnp.where(lane % HEAD_DIM < HEAD_DIM // 2, -jnp.sin(ang), jnp.sin(ang))
        c_ref[...] = jnp.concatenate([c, c, c], axis=1)
        s_ref[...] = jnp.concatenate([s, s, s], axis=1)

    row = pl.BlockSpec((rt, ATTN_WIDTH), lambda i: (i, 0))
    return pl.pallas_call(
        body, name="rope_tables", grid=(S // rt,), in_specs=[], out_specs=[row, row],
        out_shape=[jax.ShapeDtypeStruct((S, ATTN_WIDTH), f32)] * 2, compiler_params=_cp("parallel"),
    )()


def _swap_halves(x):
    lane = lax.broadcasted_iota(jnp.int32, x.shape, 1)
    half = HEAD_DIM // 2
    return jnp.where(lane % HEAD_DIM < half, pltpu.roll(x, x.shape[1] - half, 1), pltpu.roll(x, half, 1))


def _rope_fwd(proj, cos, sin):
    S, W = proj.shape[0], ATTN_WIDTH
    rt = _pick(S, ROW_TILE, SUBLANES)

    def body(q_ref, k_ref, c_ref, s_ref, qo_ref, ko_ref):
        c, s = c_ref[...], s_ref[...]
        qo_ref[...] = q_ref[...] * c + _swap_halves(q_ref[...]) * s
        ko_ref[...] = k_ref[...] * c + _swap_halves(k_ref[...]) * s

    row = pl.BlockSpec((rt, W), lambda i: (i, 0))
    return pl.pallas_call(
        body, name="rope_fwd", grid=(S // rt,), in_specs=[row, pl.BlockSpec((rt, W), lambda i: (i, 1)), row, row],
        out_specs=[row, row], out_shape=[jax.ShapeDtypeStruct((S, W), f32)] * 2, compiler_params=_cp("parallel"),
    )(proj, proj, cos, sin)


def _rope_bwd(dq, dk, cos, sin):
    S, W = dq.shape
    rt = _pick(S, ROW_TILE, SUBLANES)

    def body(q_ref, k_ref, c_ref, s_ref, qo_ref, ko_ref):
        c, s = c_ref[...], s_ref[...]
        qo_ref[...] = q_ref[...] * c + _swap_halves(q_ref[...] * s)
        ko_ref[...] = k_ref[...] * c + _swap_halves(k_ref[...] * s)

    row = pl.BlockSpec((rt, W), lambda i: (i, 0))
    return pl.pallas_call(
        body, name="rope_bwd", grid=(S // rt,), in_specs=[row] * 4, out_specs=[row] * 2,
        out_shape=[jax.ShapeDtypeStruct((S, W), f32)] * 2, compiler_params=_cp("parallel"),
    )(dq, dk, cos, sin)


def _rows(ref, start, d):
    if d == 1:
        return ref[pl.ds(pl.multiple_of(start, ATTN_BLOCK), ATTN_BLOCK), :]
    return ref[pl.ds(start, ATTN_BLOCK, stride=d), :]


def _set_rows(ref, start, d, val):
    if d == 1:
        ref[pl.ds(pl.multiple_of(start, ATTN_BLOCK), ATTN_BLOCK), :] = val
    else:
        ref[pl.ds(start, ATTN_BLOCK, stride=d), :] = val


def _pair_spec(S, first_block):
    return pl.BlockSpec((S, LANES), lambda p: (0, p + first_block))


def _attn_fwd2(qr, kr, proj, shards=(), split=()):
    S = qr.shape[0]
    B = ATTN_BLOCK
    nb = S // B
    scale = HEAD_DIM ** -0.5

    gather = _Gather(shards, split)
    nt = gather.nt

    def body(*refs):
        q_ref, k_ref, v_ref = refs[:3]
        g_ins = refs[3:3 + nt]
        o_ref, l_ref = refs[3 + nt:5 + nt]
        g_outs = refs[5 + nt:5 + 2 * nt]
        m_s, l_s = refs[5 + 2 * nt:7 + 2 * nt]
        g_sems = refs[7 + 2 * nt:]
        if nt:
            @pl.when(pl.program_id(0) == 0)
            def _():
                gather.start(g_ins, g_outs, g_sems)

        qi = lax.broadcasted_iota(jnp.int32, (B, 2 * B), 0)
        ki = lax.broadcasted_iota(jnp.int32, (B, 2 * B), 1)
        dist = qi + B - ki
        band = (dist >= 0) & (dist <= B)
        for bi, d in enumerate(DILATIONS):
            bpc = nb // d

            def blk(b, carry, bi=bi, d=d, bpc=bpc):
                c, n = b // bpc, b % bpc
                start = c + d * B * n
                pstart = c + d * B * jnp.maximum(n - 1, 0)
                valid = band & ((ki >= B) | (n > 0))
                q = _rows(q_ref, start, d).astype(bf16)
                kcat = jnp.concatenate([_rows(k_ref, pstart, d), _rows(k_ref, start, d)], axis=0).astype(bf16)
                vcat = jnp.concatenate([_rows(v_ref, pstart, d), _rows(v_ref, start, d)], axis=0).astype(bf16)
                if bi > 0:
                    m_old, l_old, a_old = _rows(m_s, start, d), _rows(l_s, start, d), _rows(o_ref, start, d)
                ms, ls, accs = [], [], []
                for h in range(2):
                    sl = slice(h * HEAD_DIM, (h + 1) * HEAD_DIM)
                    c0 = h * HEAD_DIM
                    s = jnp.where(valid, _dot(q[:, sl], kcat[:, sl], NT) * scale, NEG)
                    m = jnp.max(s, axis=1, keepdims=True)
                    if bi > 0:
                        mo = m_old[:, c0:c0 + 1]
                        m = jnp.maximum(m, mo)
                        alpha = jnp.exp(mo - m)
                    p = jnp.exp(s - m)
                    l = jnp.sum(p, axis=1, keepdims=True)
                    acc = _dot(p.astype(bf16), vcat[:, sl], NN)
                    if bi > 0:
                        l = l + alpha * l_old[:, c0:c0 + 1]
                        acc = acc + alpha * a_old[:, sl]
                    ms.append(jnp.broadcast_to(m, (B, HEAD_DIM)))
                    ls.append(jnp.broadcast_to(l, (B, HEAD_DIM)))
                    accs.append(acc)
                _set_rows(m_s, start, d, jnp.concatenate(ms, axis=1))
                _set_rows(l_s, start, d, jnp.concatenate(ls, axis=1))
                _set_rows(o_ref, start, d, jnp.concatenate(accs, axis=1))
                return carry

            lax.fori_loop(0, nb, blk, 0, unroll=4)

        def fin(t, carry):
            rows = pl.ds(pl.multiple_of(t * B, B), B)
            l = l_s[rows, :]
            o_ref[rows, :] = o_ref[rows, :] / l
            l_ref[rows, :] = m_s[rows, :] + jnp.log(l)
            return carry

        lax.fori_loop(0, nb, fin, 0)
        if nt:
            @pl.when(pl.program_id(0) == pl.num_programs(0) - 1)
            def _():
                gather.finish(g_ins, g_outs, g_sems)

    pair = _pair_spec(S, 0)
    res = pl.pallas_call(
        body, name="attn_fwd_gather" if nt else "attn_fwd", grid=(3,),
        in_specs=[pair, pair, _pair_spec(S, 2 * ATTN_WIDTH // LANES)] + gather.in_specs,
        out_specs=[pair, pair] + gather.out_specs,
        out_shape=[jax.ShapeDtypeStruct((S, ATTN_WIDTH), f32)] * 2 + gather.out_shape,
        scratch_shapes=[pltpu.VMEM((S, LANES), f32)] * 2 + gather.scratch,
        compiler_params=_cp("arbitrary"),
    )(qr, kr, proj, *shards)
    return res[0], res[1], list(res[2:])


def _attn_bwd2(qr, kr, proj, dattn, ltot, delta, slabs=(), only_c=0, send=(), send_dst=0):
    S = qr.shape[0]
    B = ATTN_BLOCK
    nb = S // B
    scale = HEAD_DIM ** -0.5
    ex = _ChipExchange(slabs, (), only_c)
    snd = _SiblingSend(send, send_dst)
    n, ns = ex.n, snd.n
    hosted = n + ns

    def body(*refs):
        q_ref, k_ref, v_ref, do_ref, l_ref, d_ref = refs[:6]
        x_ins, s_ins = refs[6:6 + n], refs[6 + n:6 + hosted]
        dq_ref, dk_ref, dv_ref = refs[6 + hosted:9 + hosted]
        x_outs, s_outs = refs[9 + hosted:9 + hosted + n], refs[9 + hosted + n:9 + 2 * hosted]
        sems = refs[9 + 2 * hosted:]
        x_sems, s_sems = sems[:len(ex.scratch)], sems[len(ex.scratch):]
        if hosted:
            @pl.when(pl.program_id(0) == 0)
            def _():
                if n:
                    ex.start(x_ins, x_outs, x_sems)
                if ns:
                    snd.start(s_ins, s_outs, s_sems)

        qi = lax.broadcasted_iota(jnp.int32, (B, 2 * B), 0)
        ki = lax.broadcasted_iota(jnp.int32, (B, 2 * B), 1)
        dist1 = qi + B - ki
        band1 = (dist1 >= 0) & (dist1 <= B)
        ri = lax.broadcasted_iota(jnp.int32, (2 * B, B), 0)
        ci = lax.broadcasted_iota(jnp.int32, (2 * B, B), 1)
        dist2 = ri - ci
        band2 = (dist2 >= 0) & (dist2 <= B)
        for bi, d in enumerate(DILATIONS):
            bpc = nb // d

            def blk(b, carry, bi=bi, d=d, bpc=bpc):
                c, n = b // bpc, b % bpc
                start = c + d * B * n
                pstart = c + d * B * jnp.maximum(n - 1, 0)
                nstart = c + d * B * jnp.minimum(n + 1, bpc - 1)
                valid1 = band1 & ((ki >= B) | (n > 0))
                valid2 = band2 & ((ri < B) | (n + 1 < bpc))
                q_c, q_n = _rows(q_ref, start, d), _rows(q_ref, nstart, d)
                k_p, k_c = _rows(k_ref, pstart, d), _rows(k_ref, start, d)
                v_p, v_c = _rows(v_ref, pstart, d), _rows(v_ref, start, d)
                do_c, do_n = _rows(do_ref, start, d), _rows(do_ref, nstart, d)
                l_c, l_n = _rows(l_ref, start, d), _rows(l_ref, nstart, d)
                d_c, d_n = _rows(d_ref, start, d), _rows(d_ref, nstart, d)
                qc = q_c.astype(bf16)
                qcat = jnp.concatenate([q_c, q_n], axis=0).astype(bf16)
                kc = k_c.astype(bf16)
                kcat = jnp.concatenate([k_p, k_c], axis=0).astype(bf16)
                vc = v_c.astype(bf16)
                vcat = jnp.concatenate([v_p, v_c], axis=0).astype(bf16)
                doc = do_c.astype(bf16)
                docat = jnp.concatenate([do_c, do_n], axis=0).astype(bf16)
                lcat = jnp.concatenate([l_c, l_n], axis=0)
                dcat = jnp.concatenate([d_c, d_n], axis=0)
                dqs, dks, dvs = [], [], []
                for h in range(2):
                    sl = slice(h * HEAD_DIM, (h + 1) * HEAD_DIM)
                    c0 = h * HEAD_DIM
                    s1 = _dot(qc[:, sl], kcat[:, sl], NT) * scale
                    p1 = jnp.where(valid1, jnp.exp(s1 - l_c[:, c0:c0 + 1]), 0.0)
                    dp1 = _dot(doc[:, sl], vcat[:, sl], NT)
                    ds1 = p1 * (dp1 - d_c[:, c0:c0 + 1]) * scale
                    dqs.append(_dot(ds1.astype(bf16), kcat[:, sl], NN))
                    s2 = _dot(qcat[:, sl], kc[:, sl], NT) * scale
                    p2 = jnp.where(valid2, jnp.exp(s2 - lcat[:, c0:c0 + 1]), 0.0)
                    dvs.append(_dot(p2.astype(bf16), docat[:, sl], TN))
                    dp2 = _dot(docat[:, sl], vc[:, sl], NT)
                    ds2 = p2 * (dp2 - dcat[:, c0:c0 + 1]) * scale
                    dks.append(_dot(ds2.astype(bf16), qcat[:, sl], TN))
                for ref, parts in ((dq_ref, dqs), (dk_ref, dks), (dv_ref, dvs)):
                    new = jnp.concatenate(parts, axis=1)
                    if bi > 0:
                        new = new + _rows(ref, start, d)
                    _set_rows(ref, start, d, new)
                return carry

            lax.fori_loop(0, nb, blk, 0, unroll=4)

        if hosted:
            @pl.when(pl.program_id(0) == pl.num_programs(0) - 1)
            def _():
                if ns:
                    snd.finish(s_ins, s_outs, s_sems)
                if n:
                    ex.finish(x_ins, x_outs, x_sems)

    pair = _pair_spec(S, 0)
    res = pl.pallas_call(
        body, name="attn_bwd_exchange" if hosted else "attn_bwd", grid=(3,),
        in_specs=[pair, pair, _pair_spec(S, 2 * ATTN_WIDTH // LANES), pair, pair, pair] + ex.in_specs + snd.in_specs,
        out_specs=[pair] * 3 + ex.out_specs + snd.out_specs,
        out_shape=[jax.ShapeDtypeStruct((S, ATTN_WIDTH), f32)] * 3 + ex.out_shape + snd.out_shape,
        scratch_shapes=ex.scratch + snd.scratch, compiler_params=_cp("arbitrary"),
    )(qr, kr, proj, dattn, ltot, delta, *slabs, *send)
    return res[0], res[1], res[2], list(res[3:3 + n]), list(res[3 + n:])


def _softplus_neg(lam):
    return jnp.maximum(-lam, 0.0) + jnp.log1p(jnp.exp(-jnp.abs(lam)))


PROJ_LRU_X, PROJ_LRU_GATE, PROJ_S5_U = 3, 4, 5
EARLY_OWNER = 1


def _lru_pre(proj, conv_w, conv_b, wr, br, wi, bi, lam):
    S, W = proj.shape[0], LRU_WIDTH
    rt = _pick(S, ROW_TILE, SUBLANES)
    K = conv_w.shape[0]

    def body(x_ref, xp_ref, cw_ref, cb_ref, wr_ref, br_ref, wi_ref, bi_ref, lam_ref,
             xc_ref, r_ref, i_ref, la_ref, u_ref):
        prev = jnp.where(pl.program_id(0) == 0, 0.0, xp_ref[...])
        x = x_ref[...]
        xc = cb_ref[...] + cw_ref[K - 1:K, :] * x
        for k in range(K - 1):
            xc = xc + cw_ref[k:k + 1, :] * _shift_down(x, prev, K - 1 - k)
        xb = xc.astype(bf16)
        r = _sigmoid(_dot(xb, wr_ref[...], NN) + br_ref[...])
        i = _sigmoid(_dot(xb, wi_ref[...], NN) + bi_ref[...])
        log_a = -LRU_C * r * _softplus_neg(lam_ref[...])
        u = jnp.sqrt(-_expm1(2.0 * log_a)) * (i * xc)
        xc_ref[...], r_ref[...], i_ref[...], la_ref[...], u_ref[...] = xc, r, i, log_a, u

    row = pl.BlockSpec((rt, W), lambda i: (i, 0))
    xrow = pl.BlockSpec((rt, W), lambda i: (i, PROJ_LRU_X))
    halo = _prev_halo_spec(rt, W, lambda i: PROJ_LRU_X)
    vec = pl.BlockSpec((1, W), lambda i: (0, 0))
    return pl.pallas_call(
        body, name="lru_pre", grid=(S // rt,),
        in_specs=[xrow, halo, pl.BlockSpec((K, W), lambda i: (0, 0)), vec,
                  pl.BlockSpec((W, W), lambda i: (0, 0)), vec, pl.BlockSpec((W, W), lambda i: (0, 0)), vec, vec],
        out_specs=[row] * 5, out_shape=[jax.ShapeDtypeStruct((S, W), f32)] * 5, compiler_params=_cp("parallel"),
    )(proj, proj, conv_w, conv_b.reshape(1, W), wr, br.reshape(1, W), wi, bi.reshape(1, W), lam.reshape(1, W))


def _tile_rows(shape):
    return lax.broadcasted_iota(jnp.int32, shape, 0)


def _lru_scan(log_a, u, proj):
    S, W = u.shape
    rt = _pick(S, ROW_TILE, SUBLANES)
    T = SUBLANES

    def body(la_ref, u_ref, g_ref, h_ref, o_ref, carry):
        @pl.when(pl.program_id(0) == 0)
        def _():
            carry[...] = jnp.zeros_like(carry)

        row = _tile_rows((T, W))

        def step(t, hp):
            r0 = pl.multiple_of(t * T, T)
            a = jnp.exp(la_ref[pl.ds(r0, T), :])
            x = u_ref[pl.ds(r0, T), :]
            for k in (1, 2, 4):
                x = x + a * jnp.where(row >= k, pltpu.roll(x, k, 0), 0.0)
                a = a * jnp.where(row >= k, pltpu.roll(a, k, 0), 1.0)
            h = x + a * hp
            h_ref[pl.ds(r0, T), :] = h
            o_ref[pl.ds(r0, T), :] = h * _gelu(g_ref[pl.ds(r0, T), :])
            return h[T - 1:T, :]

        carry[0:1, :] = lax.fori_loop(0, rt // T, step, carry[0:1, :])

    row = pl.BlockSpec((rt, W), lambda i: (i, 0))
    grow = pl.BlockSpec((rt, W), lambda i: (i, PROJ_LRU_GATE))
    return pl.pallas_call(
        body, name="lru_scan", grid=(S // rt,), in_specs=[row, row, grow], out_specs=[row] * 2,
        out_shape=[jax.ShapeDtypeStruct((S, W), f32)] * 2, scratch_shapes=[pltpu.VMEM((T, W), f32)],
        compiler_params=_cp("arbitrary"),
    )(log_a, u, proj)


def _lru_scan_bwd(dlru, proj, h, log_a):
    S, W = h.shape
    rt = _pick(S, ROW_TILE, SUBLANES)
    T = SUBLANES
    nblk = S // rt

    def body(d_ref, g_ref, h_ref, la_ref, go_ref, dg_ref, carry):
        @pl.when(pl.program_id(0) == 0)
        def _():
            carry[...] = jnp.zeros_like(carry)

        row = _tile_rows((T, W))

        def step(j, c):
            gn, an = c
            t = rt // T - 1 - j
            r0 = pl.multiple_of(t * T, T)
            d = d_ref[pl.ds(r0, T), :]
            gate = g_ref[pl.ds(r0, T), :]
            a = jnp.exp(la_ref[pl.ds(r0, T), :])
            dg_ref[pl.ds(r0, T), :] = d * h_ref[pl.ds(r0, T), :] * _gelu_grad(gate)
            x = d * _gelu(gate)
            b = jnp.where(row < T - 1, pltpu.roll(a, T - 1, 0), an)
            for k in (1, 2, 4):
                x = x + b * jnp.where(row < T - k, pltpu.roll(x, T - k, 0), 0.0)
                b = b * jnp.where(row < T - k, pltpu.roll(b, T - k, 0), 1.0)
            g = x + b * gn
            go_ref[pl.ds(r0, T), :] = g
            return g[0:1, :], a[0:1, :]

        gn, an = lax.fori_loop(0, rt // T, step, (carry[0:1, :], carry[1:2, :]))
        carry[0:1, :] = gn
        carry[1:2, :] = an

    row = pl.BlockSpec((rt, W), lambda i: (nblk - 1 - i, 0))
    grow = pl.BlockSpec((rt, W), lambda i: (nblk - 1 - i, PROJ_LRU_GATE))
    return pl.pallas_call(
        body, name="lru_scan_bwd", grid=(nblk,), in_specs=[row, grow, row, row], out_specs=[row] * 2,
        out_shape=[jax.ShapeDtypeStruct((S, W), f32)] * 2, scratch_shapes=[pltpu.VMEM((T, W), f32)],
        compiler_params=_cp("arbitrary"),
    )(dlru, proj, h, log_a)


def _lru_gate_bwd(g, h, xc, r, i, log_a, wr, wi, lam):
    S, W = g.shape
    rt = _pick(S, ROW_TILE, SUBLANES)

    def body(g_ref, h_ref, hp_ref, xc_ref, r_ref, i_ref, la_ref, wr_ref, wi_ref, lam_ref,
             dxc_ref, dwr_ref, dwi_ref, acc_ref):
        @pl.when(pl.program_id(0) == 0)
        def _():
            dwr_ref[...] = jnp.zeros_like(dwr_ref)
            dwi_ref[...] = jnp.zeros_like(dwi_ref)
            acc_ref[...] = jnp.zeros_like(acc_ref)

        prev = jnp.where(pl.program_id(0) == 0, 0.0, hp_ref[...])
        gg, xc, r, i, log_a, lam = g_ref[...], xc_ref[...], r_ref[...], i_ref[...], la_ref[...], lam_ref[...]
        hm1 = _shift_down(h_ref[...], prev, 1)
        a = jnp.exp(log_a)
        s = jnp.sqrt(-_expm1(2.0 * log_a))
        da = gg * hm1
        di = gg * s * xc
        dxc = gg * s * i
        ds = gg * i * xc
        dlog_a = da * a - ds * (a * a / s)
        sp = _softplus_neg(lam)
        dr = dlog_a * (-LRU_C * sp)
        dsp = jnp.sum(dlog_a * (-LRU_C * r), axis=0, keepdims=True)
        dpr = dr * r * (1.0 - r)
        dpi = di * i * (1.0 - i)
        dprb, dpib, xb = dpr.astype(bf16), dpi.astype(bf16), xc.astype(bf16)
        dxc_ref[...] = dxc + _dot(dprb, wr_ref[...], NT) + _dot(dpib, wi_ref[...], NT)
        dwr_ref[...] += _dot(xb, dprb, TN)
        dwi_ref[...] += _dot(xb, dpib, TN)
        acc_ref[0:1, :] += jnp.sum(dpr, axis=0, keepdims=True)
        acc_ref[1:2, :] += jnp.sum(dpi, axis=0, keepdims=True)
        acc_ref[2:3, :] += dsp * (-_sigmoid(-lam))

    row = pl.BlockSpec((rt, W), lambda i: (i, 0))
    halo = _prev_halo_spec(rt, W, lambda i: 0)
    vec = pl.BlockSpec((1, W), lambda i: (0, 0))
    mat = pl.BlockSpec((W, W), lambda i: (0, 0))
    acc = pl.BlockSpec((SUBLANES, W), lambda i: (0, 0))
    return pl.pallas_call(
        body, name="lru_gate_bwd", grid=(S // rt,),
        in_specs=[row, row, halo, row, row, row, row, mat, mat, vec], out_specs=[row, mat, mat, acc],
        out_shape=[jax.ShapeDtypeStruct((S, W), f32), jax.ShapeDtypeStruct((W, W), f32),
                   jax.ShapeDtypeStruct((W, W), f32), jax.ShapeDtypeStruct((SUBLANES, W), f32)],
        compiler_params=_cp("arbitrary"),
    )(g, h, h, xc, r, i, log_a, wr, wi, lam.reshape(1, W))


def _conv_bwd(dy, x, conv_w, name, col_tile=None, out_dtype=f32, x_col_block=0, send=(), send_dst=0):
    if dy.ndim == 2:
        dy = dy[None]
    H, S, Ch = dy.shape
    C = H * Ch
    K = conv_w.shape[0]
    ct = Ch if col_tile is None else col_tile
    nct = Ch // ct
    rt = _pick(S, ROW_TILE, SUBLANES)
    nrt = S // rt
    snd = _SiblingSend(send, send_dst)
    n = snd.n

    def body(*refs):
        dy_ref, dyn_ref, x_ref, w_ref = refs[:4]
        s_ins = refs[4:4 + n]
        dx_ref, acc_ref = refs[4 + n:6 + n]
        s_outs, s_sems = refs[6 + n:6 + 2 * n], refs[6 + 2 * n:]
        i = pl.program_id(2)
        if n:
            @pl.when((pl.program_id(0) == 0) & (pl.program_id(1) == 0) & (i == 0))
            def _():
                snd.start(s_ins, s_outs, s_sems)

        @pl.when(i == 0)
        def _():
            acc_ref[...] = jnp.zeros_like(acc_ref)

        nxt = jnp.where(i == nrt - 1, 0.0, dyn_ref[...].astype(f32)[0:SUBLANES])
        dy, x = dy_ref[...].astype(f32), x_ref[...]
        ahead = [dy] + [_shift_up(dy, nxt, j) for j in range(1, K)]
        dx = w_ref[K - 1:K, :] * dy
        for k in range(K - 1):
            dx = dx + w_ref[k:k + 1, :] * ahead[K - 1 - k]
        dx_ref[...] = dx.astype(dx_ref.dtype)
        for k in range(K):
            acc_ref[k:k + 1, :] += jnp.sum(ahead[K - 1 - k] * x, axis=0, keepdims=True)
        acc_ref[K:K + 1, :] += jnp.sum(dy, axis=0, keepdims=True)
        if n:
            @pl.when((pl.program_id(0) == H - 1) & (pl.program_id(1) == nct - 1) & (i == nrt - 1))
            def _():
                snd.finish(s_ins, s_outs, s_sems)

    halo = SUBLANES * (4 // dy.dtype.itemsize)
    per, last = rt // halo, S // halo - 1
    dy_row = pl.BlockSpec((None, rt, ct), lambda h, j, i: (h, i, j))
    dy_next = pl.BlockSpec((None, halo, ct), lambda h, j, i: (h, jnp.minimum((i + 1) * per, last), j))
    row = pl.BlockSpec((rt, ct), lambda h, j, i: (i, h * nct + j))
    xrow = pl.BlockSpec((rt, ct), lambda h, j, i: (i, h * nct + j + x_col_block))
    res = pl.pallas_call(
        body, name=name, grid=(H, nct, nrt),
        in_specs=[dy_row, dy_next, xrow, pl.BlockSpec((K, ct), lambda h, j, i: (0, h * nct + j))] + snd.in_specs,
        out_specs=[row, pl.BlockSpec((SUBLANES, ct), lambda h, j, i: (0, h * nct + j))] + snd.out_specs,
        out_shape=[jax.ShapeDtypeStruct((S, C), out_dtype), jax.ShapeDtypeStruct((SUBLANES, C), f32)] + snd.out_shape,
        scratch_shapes=snd.scratch,
        compiler_params=_cp(*(("arbitrary",) * 3 if n else ("parallel", "parallel", "arbitrary"))),
    )(dy, dy, x, conv_w, *send)
    return res[0], res[1], list(res[2:])


def _s5_param_fn(a_re, a_im, ls, bt_re, bt_im):
    step = jnp.exp(ls)
    dt_re, dt_im = step * a_re, step * a_im
    mag = jnp.exp(dt_re)
    ab_re, ab_im = mag * jnp.cos(dt_im), mag * jnp.sin(dt_im)
    z_re, z_im = ab_re - 1.0, ab_im
    den = a_re * a_re + a_im * a_im
    f_re = (z_re * a_re + z_im * a_im) / den
    f_im = (z_im * a_re - z_re * a_im) / den
    bb_re = f_re[:, None, :] * bt_re - f_im[:, None, :] * bt_im
    bb_im = f_re[:, None, :] * bt_im + f_im[:, None, :] * bt_re
    return ab_re, ab_im, bb_re, bb_im


def _s5_params(a_re, a_im, ls, bt_re, bt_im):
    def body(ar, ai, l, br, bi, o_ar, o_ai, o_br, o_bi):
        o_ar[...], o_ai[...], o_br[...], o_bi[...] = _s5_param_fn(ar[...], ai[...], l[...], br[...], bi[...])

    return pl.pallas_call(
        body, name="s5_params",
        out_shape=[jax.ShapeDtypeStruct(a_re.shape, f32)] * 2 + [jax.ShapeDtypeStruct(bt_re.shape, f32)] * 2,
        compiler_params=_cp(),
    )(a_re, a_im, ls, bt_re, bt_im)


def _s5_params_bwd(a_re, a_im, ls, bt_re, bt_im, d_ar, d_ai, d_br, d_bi):
    def body(ar, ai, l, br, bi, c_ar, c_ai, c_br, c_bi, g_ar, g_ai, g_l, g_br, g_bi):
        _, vjp = jax.vjp(_s5_param_fn, ar[...], ai[...], l[...], br[...], bi[...])
        g_ar[...], g_ai[...], g_l[...], g_br[...], g_bi[...] = vjp((c_ar[...], c_ai[...], c_br[...], c_bi[...]))

    return pl.pallas_call(
        body, name="s5_params_bwd",
        out_shape=[jax.ShapeDtypeStruct(a_re.shape, f32)] * 2 + [jax.ShapeDtypeStruct(ls.shape, f32)]
        + [jax.ShapeDtypeStruct(bt_re.shape, f32)] * 2,
        compiler_params=_cp(),
    )(a_re, a_im, ls, bt_re, bt_im, d_ar, d_ai, d_br, d_bi)


S5_CHUNK = 256


def _s5_power_tables(ab_ref, p_ref, w_ref, conj):
    T, L = SUBLANES, S5_LANES
    are = ab_ref[0:1, 0:L]
    aim = ab_ref[0:1, L:2 * L]
    if conj:
        aim = -aim
    pre, pim = are, aim
    for n in range(3):
        p_ref[n:n + 1, 0:L] = pre
        p_ref[n:n + 1, L:2 * L] = pim
        pre, pim = pre * pre - pim * pim, 2.0 * pre * pim
    row = _tile_rows((T, L))
    wre = jnp.zeros((T, L), f32)
    wim = jnp.zeros((T, L), f32)
    pre, pim = are, aim
    for n in range(T):
        tgt = (T - 1 - n) if conj else n
        wre = jnp.where(row == tgt, pre, wre)
        wim = jnp.where(row == tgt, pim, wim)
        pre, pim = pre * are - pim * aim, pre * aim + pim * are
    w_ref[:, 0:L] = wre
    w_ref[:, L:2 * L] = wim


def _s5_scan(bu, ab):
    S, L2 = bu.shape
    L = L2 // 2
    rt = _pick(S, 256, SUBLANES)
    T = SUBLANES
    CH = S5_CHUNK

    def body(bu_ref, ab_ref, x_ref, p_ref, w_ref, carry):
        @pl.when(pl.program_id(0) == 0)
        def _():
            carry[...] = jnp.zeros_like(carry)
            _s5_power_tables(ab_ref, p_ref, w_ref, conj=False)

        row = _tile_rows((T, CH))

        def step(t, _):
            r0 = pl.multiple_of(t * T, T)
            for c in range(L // CH):
                lre, lim = pl.ds(c * CH, CH), pl.ds(L + c * CH, CH)
                xr, xi = bu_ref[pl.ds(r0, T), lre], bu_ref[pl.ds(r0, T), lim]
                for n, k in enumerate((1, 2, 4)):
                    pr, pi = p_ref[n:n + 1, lre], p_ref[n:n + 1, lim]
                    sr = jnp.where(row >= k, pltpu.roll(xr, k, 0), 0.0)
                    si = jnp.where(row >= k, pltpu.roll(xi, k, 0), 0.0)
                    xr, xi = xr + pr * sr - pi * si, xi + pr * si + pi * sr
                cr, ci = carry[T - 1:T, lre], carry[T - 1:T, lim]
                wr, wi = w_ref[:, lre], w_ref[:, lim]
                xr, xi = xr + wr * cr - wi * ci, xi + wr * ci + wi * cr
                carry[:, lre] = xr
                carry[:, lim] = xi
                x_ref[pl.ds(r0, T), lre] = xr
                x_ref[pl.ds(r0, T), lim] = xi
            return 0

        lax.fori_loop(0, rt // T, step, 0)

    row_spec = pl.BlockSpec((rt, L2), lambda i: (i, 0))
    return pl.pallas_call(
        body, name="s5_scan", grid=(S // rt,), in_specs=[row_spec, pl.BlockSpec((1, L2), lambda i: (0, 0))],
        out_specs=row_spec, out_shape=jax.ShapeDtypeStruct((S, L2), f32),
        scratch_shapes=[pltpu.VMEM((T, L2), f32), pltpu.VMEM((T, L2), f32), pltpu.VMEM((T, L2), f32)],
        compiler_params=_cp("arbitrary"),
    )(bu, ab)


def _s5_scan_bwd(dx, x, ab):
    S, L2 = dx.shape
    L = L2 // 2
    rt = _pick(S, 256, SUBLANES)
    T = SUBLANES
    CH = S5_CHUNK
    nblk = S // rt
    per = rt // T

    def body(dx_ref, x_ref, xp_ref, ab_ref, g_ref, da_ref, p_ref, w_ref, carry, acc):
        pid = pl.program_id(0)

        @pl.when(pid == 0)
        def _():
            carry[...] = jnp.zeros_like(carry)
            acc[...] = jnp.zeros_like(acc)
            _s5_power_tables(ab_ref, p_ref, w_ref, conj=True)

        row = _tile_rows((T, CH))
        first_block = pid == nblk - 1

        def step(j, _):
            t = per - 1 - j
            r0 = pl.multiple_of(t * T, T)
            rp = pl.multiple_of(jnp.maximum(t - 1, 0) * T, T)
            for c in range(L // CH):
                lre, lim = pl.ds(c * CH, CH), pl.ds(L + c * CH, CH)
                gr, gi = dx_ref[pl.ds(r0, T), lre], dx_ref[pl.ds(r0, T), lim]
                for n, k in enumerate((1, 2, 4)):
                    pr, pi = p_ref[n:n + 1, lre], p_ref[n:n + 1, lim]
                    sr = jnp.where(row < T - k, pltpu.roll(gr, T - k, 0), 0.0)
                    si = jnp.where(row < T - k, pltpu.roll(gi, T - k, 0), 0.0)
                    gr, gi = gr + pr * sr - pi * si, gi + pr * si + pi * sr
                cr, ci = carry[0:1, lre], carry[0:1, lim]
                wr, wi = w_ref[:, lre], w_ref[:, lim]
                gr, gi = gr + wr * cr - wi * ci, gi + wr * ci + wi * cr
                carry[:, lre] = gr
                carry[:, lim] = gi
                g_ref[pl.ds(r0, T), lre] = gr
                g_ref[pl.ds(r0, T), lim] = gi
                xr, xi = x_ref[pl.ds(r0, T), lre], x_ref[pl.ds(r0, T), lim]
                in_blk_r, in_blk_i = x_ref[pl.ds(rp, T), lre], x_ref[pl.ds(rp, T), lim]
                hal_r = jnp.where(first_block, 0.0, xp_ref[:, lre])
                hal_i = jnp.where(first_block, 0.0, xp_ref[:, lim])
                pvr = jnp.where(t == 0, hal_r, in_blk_r)[T - 1:T, :]
                pvi = jnp.where(t == 0, hal_i, in_blk_i)[T - 1:T, :]
                sxr = jnp.where(row >= 1, pltpu.roll(xr, 1, 0), pvr)
                sxi = jnp.where(row >= 1, pltpu.roll(xi, 1, 0), pvi)
                acc[:, lre] += gr * sxr + gi * sxi
                acc[:, lim] += gi * sxr - gr * sxi
            return 0

        lax.fori_loop(0, per, step, 0)

        @pl.when(pid == nblk - 1)
        def _():
            da_ref[...] = jnp.sum(acc[...], axis=0, keepdims=True)

    row_spec = pl.BlockSpec((rt, L2), lambda i: (nblk - 1 - i, 0))
    halo = pl.BlockSpec((T, L2), lambda i: (jnp.maximum((nblk - 1 - i) * per - 1, 0), 0))
    vec = pl.BlockSpec((1, L2), lambda i: (0, 0))
    return pl.pallas_call(
        body, name="s5_scan_bwd", grid=(nblk,), in_specs=[row_spec, row_spec, halo, vec],
        out_specs=[row_spec, vec],
        out_shape=[jax.ShapeDtypeStruct((S, L2), f32), jax.ShapeDtypeStruct((1, L2), f32)],
        scratch_shapes=[pltpu.VMEM((T, L2), f32)] * 4,
        compiler_params=_cp("arbitrary"),
    )(dx, x, x, ab)


def _S5_U_SPEC(rt):
    return pl.BlockSpec((rt, LRU_WIDTH), lambda i: (i, PROJ_S5_U))


def _s5_out(yc, proj, d, wglu, bglu):
    S, W = yc.shape
    rt = _pick(S, ROW_TILE, SUBLANES)

    def body(yc_ref, u_ref, d_ref, w_ref, b_ref, o_ref, y_ref):
        y = yc_ref[...] + d_ref[...] * u_ref[:, 0:W]
        yg = _gelu(y)
        z = _dot(yg.astype(bf16), w_ref[...], NN) + b_ref[...]
        o_ref[...] = yg * _sigmoid(z)
        y_ref[...] = y

    row = pl.BlockSpec((rt, W), lambda i: (i, 0))
    vec = pl.BlockSpec((1, W), lambda i: (0, 0))
    mat = pl.BlockSpec((W, W), lambda i: (0, 0))
    return pl.pallas_call(
        body, name="s5_out", grid=(S // rt,), in_specs=[row, _S5_U_SPEC(rt), vec, mat, vec], out_specs=[row, row],
        out_shape=[jax.ShapeDtypeStruct((S, W), f32)] * 2, compiler_params=_cp("parallel"),
    )(yc, proj, d.reshape(1, W), wglu, bglu.reshape(1, W))


def _s5_out_bwd(dssm, y, proj, d, wglu, bglu):
    S, W = y.shape
    rt = _pick(S, ROW_TILE, SUBLANES)

    def body(do_ref, y_ref, u_ref, d_ref, w_ref, b_ref, dy_ref, du_ref, dw_ref, acc_ref):
        @pl.when(pl.program_id(0) == 0)
        def _():
            dw_ref[...] = jnp.zeros_like(dw_ref)
            acc_ref[...] = jnp.zeros_like(acc_ref)

        do, y = do_ref[...], y_ref[...]
        yg = _gelu(y)
        ygb = yg.astype(bf16)
        sg = _sigmoid(_dot(ygb, w_ref[...], NN) + b_ref[...])
        dz = do * yg * sg * (1.0 - sg)
        dzb = dz.astype(bf16)
        dyg = do * sg + _dot(dzb, w_ref[...], NT)
        dy = dyg * _gelu_grad(y)
        dy_ref[...] = dy
        du_ref[...] = dy * d_ref[...]
        dw_ref[...] += _dot(ygb, dzb, TN)
        acc_ref[0:1, :] += jnp.sum(dz, axis=0, keepdims=True)
        acc_ref[1:2, :] += jnp.sum(dy * u_ref[:, 0:W], axis=0, keepdims=True)

    row = pl.BlockSpec((rt, W), lambda i: (i, 0))
    vec = pl.BlockSpec((1, W), lambda i: (0, 0))
    mat = pl.BlockSpec((W, W), lambda i: (0, 0))
    acc = pl.BlockSpec((SUBLANES, W), lambda i: (0, 0))
    return pl.pallas_call(
        body, name="s5_out_bwd", grid=(S // rt,), in_specs=[row, row, _S5_U_SPEC(rt), vec, mat, vec],
        out_specs=[row, row, mat, acc],
        out_shape=[jax.ShapeDtypeStruct((S, W), f32)] * 2
        + [jax.ShapeDtypeStruct((W, W), f32), jax.ShapeDtypeStruct((SUBLANES, W), f32)],
        compiler_params=_cp("arbitrary"),
    )(dssm, y, proj, d.reshape(1, W), wglu, bglu.reshape(1, W))


MIX_SPLITS = ((0, ATTN_WIDTH), (ATTN_WIDTH, ATTN_WIDTH + LRU_WIDTH), (ATTN_WIDTH + LRU_WIDTH, D_MODEL))


def _mixnorm(attn, lru, ssm, g):
    S = attn.shape[0]
    rt = _pick(S, ROW_TILE, SUBLANES)

    def body(a_ref, l_ref, s_ref, g_ref, o_ref):
        for ref, (lo, hi) in zip((a_ref, l_ref, s_ref), MIX_SPLITS):
            x = ref[...]
            ms = jnp.mean(x * x, axis=-1, keepdims=True)
            o_ref[:, lo:hi] = (x * lax.rsqrt(ms + RMS_EPS) * g_ref[:, lo:hi]).astype(o_ref.dtype)

    rows = [pl.BlockSpec((rt, hi - lo), lambda i: (i, 0)) for lo, hi in MIX_SPLITS]
    return pl.pallas_call(
        body, name="mixnorm", grid=(S // rt,), in_specs=rows + [pl.BlockSpec((1, D_MODEL), lambda i: (0, 0))],
        out_specs=pl.BlockSpec((rt, D_MODEL), lambda i: (i, 0)),
        out_shape=jax.ShapeDtypeStruct((S, D_MODEL), bf16), compiler_params=_cp("parallel"),
    )(attn, lru, ssm, g.reshape(1, D_MODEL))


def _mixnorm_bwd(dmixed, attn, lru, ssm, g):
    S = attn.shape[0]
    rt = _pick(S, ROW_TILE, SUBLANES)

    def body(d_ref, a_ref, l_ref, s_ref, g_ref, da_ref, dl_ref, ds_ref, dlt_ref, acc_ref):
        @pl.when(pl.program_id(0) == 0)
        def _():
            acc_ref[...] = jnp.zeros_like(acc_ref)

        outs = []
        for ref, (lo, hi) in zip((a_ref, l_ref, s_ref), MIX_SPLITS):
            x = ref[...]
            dy = d_ref[:, lo:hi]
            rinv = lax.rsqrt(jnp.mean(x * x, axis=-1, keepdims=True) + RMS_EPS)
            dyg = dy * g_ref[:, lo:hi]
            outs.append(rinv * dyg - x * (rinv * rinv * rinv) * jnp.mean(dyg * x, axis=-1, keepdims=True))
            acc_ref[0:1, lo:hi] += jnp.sum(dy * x * rinv, axis=0, keepdims=True)
        da_ref[...], dl_ref[...], ds_ref[...] = outs
        hi_ = lax.broadcasted_iota(jnp.int32, (ATTN_WIDTH, ATTN_WIDTH), 0) // HEAD_DIM
        hj_ = lax.broadcasted_iota(jnp.int32, (ATTN_WIDTH, ATTN_WIDTH), 1) // HEAD_DIM
        same = jnp.where(hi_ == hj_, 1.0, 0.0).astype(f32)
        dlt_ref[...] = jnp.dot(outs[0] * a_ref[...], same, precision=lax.Precision.HIGHEST, preferred_element_type=f32)

    rows = [pl.BlockSpec((rt, hi - lo), lambda i: (i, 0)) for lo, hi in MIX_SPLITS]
    full = pl.BlockSpec((rt, D_MODEL), lambda i: (i, 0))
    return pl.pallas_call(
        body, name="mixnorm_bwd", grid=(S // rt,),
        in_specs=[full] + rows + [pl.BlockSpec((1, D_MODEL), lambda i: (0, 0))],
        out_specs=rows + [rows[0], pl.BlockSpec((SUBLANES, D_MODEL), lambda i: (0, 0))],
        out_shape=[jax.ShapeDtypeStruct((S, hi - lo), f32) for lo, hi in MIX_SPLITS]
        + [jax.ShapeDtypeStruct((S, ATTN_WIDTH), f32), jax.ShapeDtypeStruct((SUBLANES, D_MODEL), f32)],
        compiler_params=_cp("arbitrary"),
    )(dmixed, attn, lru, ssm, g.reshape(1, D_MODEL))


FFN_COL_TILE = 1536


def _ffn_conv(x, prev, w_ref, b_ref, K):
    y = b_ref[...] + w_ref[K - 1:K, :] * x
    for k in range(K - 1):
        y = y + w_ref[k:k + 1, :] * _shift_down(x, prev, K - 1 - k)
    return y


def _ffn_act(up, conv_w, conv_b):
    S, C2 = up.shape
    C = C2 // 2
    K = conv_w.shape[0]
    ct = FFN_COL_TILE
    nct = C // ct
    rt = _pick(S, ROW_TILE, SUBLANES)

    def body(g_ref, gp_ref, v_ref, vp_ref, wg_ref, wv_ref, bg_ref, bv_ref, o_ref):
        first = pl.program_id(1) == 0
        gate = _ffn_conv(g_ref[...], jnp.where(first, 0.0, gp_ref[...]), wg_ref, bg_ref, K)
        val = _ffn_conv(v_ref[...], jnp.where(first, 0.0, vp_ref[...]), wv_ref, bv_ref, K)
        o_ref[...] = (_gelu(gate) * val).astype(o_ref.dtype)

    def specs(off):
        return (pl.BlockSpec((rt, ct), lambda j, i: (i, j + off)), _prev_halo_spec(rt, ct, lambda j, i: j + off))

    def wspec(off, rows):
        return pl.BlockSpec((rows, ct), lambda j, i: (0, j + off))

    g_s, gp_s = specs(0)
    v_s, vp_s = specs(nct)
    return pl.pallas_call(
        body, name="ffn_act", grid=(nct, S // rt),
        in_specs=[g_s, gp_s, v_s, vp_s, wspec(0, K), wspec(nct, K), wspec(0, 1), wspec(nct, 1)],
        out_specs=pl.BlockSpec((rt, ct), lambda j, i: (i, j)),
        out_shape=jax.ShapeDtypeStruct((S, C), bf16), compiler_params=_cp("parallel", "parallel"),
    )(up, up, up, up, conv_w, conv_w, conv_b.reshape(1, C2), conv_b.reshape(1, C2))


def _ffn_act_bwd(dact, up, conv_w, conv_b):
    S, C2 = up.shape
    C = C2 // 2
    K = conv_w.shape[0]
    ct = FFN_COL_TILE
    nct = C // ct
    rt = _pick(S, ROW_TILE, SUBLANES)

    def body(d_ref, g_ref, gp_ref, v_ref, vp_ref, wg_ref, wv_ref, bg_ref, bv_ref, o_ref):
        first = pl.program_id(1) == 0
        gate = _ffn_conv(g_ref[...], jnp.where(first, 0.0, gp_ref[...]), wg_ref, bg_ref, K)
        val = _ffn_conv(v_ref[...], jnp.where(first, 0.0, vp_ref[...]), wv_ref, bv_ref, K)
        d = d_ref[...].astype(f32)
        gl, dgl = _gelu_pair(gate)
        o_ref[0] = (d * val * dgl).astype(o_ref.dtype)
        o_ref[1] = (d * gl).astype(o_ref.dtype)

    def specs(off):
        return (pl.BlockSpec((rt, ct), lambda j, i: (i, j + off)), _prev_halo_spec(rt, ct, lambda j, i: j + off))

    def wspec(off, rows):
        return pl.BlockSpec((rows, ct), lambda j, i: (0, j + off))

    g_s, gp_s = specs(0)
    v_s, vp_s = specs(nct)
    return pl.pallas_call(
        body, name="ffn_act_bwd", grid=(nct, S // rt),
        in_specs=[pl.BlockSpec((rt, ct), lambda j, i: (i, j)), g_s, gp_s, v_s, vp_s,
                  wspec(0, K), wspec(nct, K), wspec(0, 1), wspec(nct, 1)],
        out_specs=pl.BlockSpec((2, rt, ct), lambda j, i: (0, i, j)),
        out_shape=jax.ShapeDtypeStruct((2, S, C), bf16), compiler_params=_cp("parallel", "parallel"),
    )(dact, up, up, up, up, conv_w, conv_w, conv_b.reshape(1, C2), conv_b.reshape(1, C2))


ANY = pl.BlockSpec(memory_space=pl.ANY)


def _rows_for(cols):
    return max(16, (1 << 19) // cols)


def _chips(x, y):
    return [(1 - x, y), (x, 1 - y), (1 - x, 1 - y)]


class _Gather:
    def __init__(self, shards, split):
        self.shapes = [s.shape for s in shards]
        self.dtypes = [s.dtype for s in shards]
        self.split = list(split)
        self.nt = len(shards)
        self.in_specs = [ANY] * self.nt
        self.out_specs = [ANY] * self.nt
        self.out_shape = [jax.ShapeDtypeStruct((4,) + s, dt) for s, dt in zip(self.shapes, self.dtypes)]
        self.scratch = [pltpu.SemaphoreType.DMA((3, self.nt))] * 4 if self.nt else []

    def _part(self, ref, t, half):
        if not self.split[t]:
            return ref
        r = self.shapes[t][0] // 2
        return ref.at[pl.ds(half * r, r), :]

    def _ici(self, ins, outs, sems, k, t, chip, landing_chip):
        x, y, c = lax.axis_index("x"), lax.axis_index("y"), lax.axis_index("c")
        return pltpu.make_async_remote_copy(
            src_ref=self._part(ins[t], t, c), dst_ref=self._part(outs[t].at[landing_chip], t, c),
            send_sem=sems[0].at[k, t], recv_sem=sems[1].at[k, t], device_id=(chip[0], chip[1], c), device_id_type=MESH)

    def _d2d(self, outs, sems, k, t, q, half):
        x, y, c = lax.axis_index("x"), lax.axis_index("y"), lax.axis_index("c")
        rows = self._part(outs[t].at[q], t, half)
        return pltpu.make_async_remote_copy(
            src_ref=rows, dst_ref=rows, send_sem=sems[2].at[k, t], recv_sem=sems[3].at[k, t],
            device_id=(x, y, 1 - c), device_id_type=MESH)

    def start(self, ins, outs, sems):
        x, y = lax.axis_index("x"), lax.axis_index("y")
        me = 2 * x + y
        for k, chip in enumerate(_chips(x, y)):
            for t in range(self.nt):
                self._ici(ins, outs, sems, k, t, chip, me).start()

    def finish(self, ins, outs, sems):
        x, y, c = lax.axis_index("x"), lax.axis_index("y"), lax.axis_index("c")
        me = 2 * x + y
        chips = _chips(x, y)
        for k, chip in enumerate(chips):
            q = 2 * chip[0] + chip[1]
            for t in range(self.nt):
                self._ici(ins, outs, sems, k, t, chip, q).wait_recv()
                if self.split[t]:
                    self._d2d(outs, sems, k, t, q, c).start()
        for k, chip in enumerate(chips):
            q = 2 * chip[0] + chip[1]
            for t in range(self.nt):
                if self.split[t]:
                    self._d2d(outs, sems, k, t, q, 1 - c).wait_recv()
        for k, chip in enumerate(chips):
            q = 2 * chip[0] + chip[1]
            for t in range(self.nt):
                self._ici(ins, outs, sems, k, t, chip, me).wait_send()
                if self.split[t]:
                    self._d2d(outs, sems, k, t, q, c).wait_send()


def _gather_weights(shards, split):
    g = _Gather(shards, split)
    nt = g.nt

    def body(*refs):
        ins, outs, sems = refs[:nt], refs[nt:2 * nt], refs[2 * nt:]
        g.start(ins, outs, sems)
        g.finish(ins, outs, sems)

    return pl.pallas_call(
        body, name="gather_weights", in_specs=g.in_specs, out_specs=g.out_specs, out_shape=g.out_shape,
        scratch_shapes=g.scratch,
    )(*shards)


class _SiblingSend:
    def __init__(self, gs, dst_c, swap=()):
        self.nt, self.n = len(gs), len(gs) + len(swap)
        self.dst_c = list(dst_c) if isinstance(dst_c, (list, tuple)) else [dst_c] * self.nt
        self.in_specs = [ANY] * self.n
        self.out_specs = [ANY] * self.n
        self.out_shape = [jax.ShapeDtypeStruct(g.shape, g.dtype) for g in list(gs) + list(swap)]
        self.scratch = [pltpu.SemaphoreType.DMA((self.n,))] * 2 if self.n else []

    def _each(self, ins, outs, sems, sender, fn):
        x, y, c = lax.axis_index("x"), lax.axis_index("y"), lax.axis_index("c")

        def cp(t):
            return pltpu.make_async_remote_copy(
                src_ref=ins[t], dst_ref=outs[t], send_sem=sems[0].at[t], recv_sem=sems[1].at[t],
                device_id=(x, y, 1 - c), device_id_type=MESH)

        for dst in (0, 1):
            which = [t for t in range(self.nt) if self.dst_c[t] == dst]
            if which:
                @pl.when((c != dst) if sender else (c == dst))
                def _(which=which):
                    for t in which:
                        fn(cp(t))
        for t in range(self.nt, self.n):
            fn(cp(t))

    def start(self, ins, outs, sems):
        self._each(ins, outs, sems, True, lambda cp: cp.start())

    def finish(self, ins, outs, sems):
        self._each(ins, outs, sems, False, lambda cp: cp.wait_recv())
        self._each(ins, outs, sems, True, lambda cp: cp.wait_send())


def _sibling_send(gs, dst_c, swap=()):
    snd = _SiblingSend(gs, dst_c, swap)
    n = snd.n

    def body(*refs):
        ins, outs, sems = refs[:n], refs[n:2 * n], refs[2 * n:]
        snd.start(ins, outs, sems)
        snd.finish(ins, outs, sems)

    res = pl.pallas_call(
        body, name="sibling_send", in_specs=snd.in_specs, out_specs=snd.out_specs, out_shape=snd.out_shape,
        scratch_shapes=snd.scratch,
    )(*gs, *swap)
    return list(res[:snd.nt]), list(res[snd.nt:])


def _owner_flag(owner_c):
    return (lax.axis_index("c") == owner_c).astype(jnp.int32).reshape(1)


def _pair_sum(g, other, name, owner_c, col_slabs=False):
    R, C = g.shape
    cb = C // 4 if col_slabs else C
    rt = _pick(R, _rows_for(cb), 16)

    def body(on_ref, a_ref, o_ref, out_ref):
        out_ref[...] = (a_ref[...] + o_ref[...]).astype(out_ref.dtype)

    row = pl.BlockSpec((rt, cb), lambda q, i, on: (i * on[0], q * on[0]))
    if col_slabs:
        out_spec = pl.BlockSpec((None, rt, cb), lambda q, i, on: (q * on[0], i * on[0], 0))
        out_shape = jax.ShapeDtypeStruct((4, R, cb), bf16)
    else:
        out_spec, out_shape = row, jax.ShapeDtypeStruct((R, C), bf16)
    return pl.pallas_call(
        body, name=name,
        grid_spec=pltpu.PrefetchScalarGridSpec(num_scalar_prefetch=1, grid=(C // cb, R // rt),
                                               in_specs=[row, row], out_specs=out_spec),
        out_shape=out_shape, compiler_params=_cp("arbitrary", "arbitrary"),
    )(_owner_flag(owner_c), g, other)


class _ChipExchange:
    NORTH = 1

    def __init__(self, slabs, whole, only_c):
        self.ns, self.nw = len(slabs), len(whole)
        self.only_c = list(only_c) if isinstance(only_c, (list, tuple)) else [only_c] * self.ns
        self.only_c += [self.NORTH] * self.nw
        self.n = self.ns + self.nw
        self.in_specs = [ANY] * self.n
        self.out_specs = [ANY] * self.n
        self.out_shape = ([jax.ShapeDtypeStruct(s.shape, s.dtype) for s in slabs]
                          + [jax.ShapeDtypeStruct((4,) + w.shape, w.dtype) for w in whole])
        self.scratch = [pltpu.SemaphoreType.DMA((3, self.n))] * 2 if self.n else []
        self.scratch += [pltpu.SemaphoreType.DMA((3, self.nw))] * 2 if self.nw else []

    def _copy(self, ins, outs, sems, k, t, chip, landing_chip):
        c = lax.axis_index("c")
        src = ins[t].at[2 * chip[0] + chip[1]] if t < self.ns else ins[t]
        return pltpu.make_async_remote_copy(
            src_ref=src, dst_ref=outs[t].at[landing_chip], send_sem=sems[0].at[k, t], recv_sem=sems[1].at[k, t],
            device_id=(chip[0], chip[1], c), device_id_type=MESH)

    def _pass_on(self, outs, sems, k, t, chip):
        x, y, c = lax.axis_index("x"), lax.axis_index("y"), lax.axis_index("c")
        rows = outs[t].at[2 * chip[0] + chip[1]]
        return pltpu.make_async_remote_copy(
            src_ref=rows, dst_ref=rows, send_sem=sems[2].at[k, t - self.ns], recv_sem=sems[3].at[k, t - self.ns],
            device_id=(x, y, 1 - c), device_id_type=MESH)

    def _each(self, fn, north=True):
        x, y, c = lax.axis_index("x"), lax.axis_index("y"), lax.axis_index("c")
        chips = _chips(x, y)
        groups = {}
        if north:
            for t in range(self.n):
                groups.setdefault(self.only_c[t], []).append(t)
        elif self.nw:
            groups[1 - self.NORTH] = list(range(self.ns, self.n))
        for owner, which in groups.items():
            @pl.when(c == owner)
            def _(which=which):
                for k, chip in enumerate(chips):
                    for t in which:
                        fn(k, t, chip)

    def start(self, ins, outs, sems):
        me = 2 * lax.axis_index("x") + lax.axis_index("y")
        self._each(lambda k, t, chip: self._copy(ins, outs, sems, k, t, chip, me).start())

    def finish(self, ins, outs, sems):
        me = 2 * lax.axis_index("x") + lax.axis_index("y")

        def landed(k, t, chip):
            self._copy(ins, outs, sems, k, t, chip, 2 * chip[0] + chip[1]).wait_recv()
            if t >= self.ns:
                self._pass_on(outs, sems, k, t, chip).start()

        def sent(k, t, chip):
            self._copy(ins, outs, sems, k, t, chip, me).wait_send()
            if t >= self.ns:
                self._pass_on(outs, sems, k, t, chip).wait_send()

        self._each(landed)
        self._each(lambda k, t, chip: self._pass_on(outs, sems, k, t, chip).wait_recv(), north=False)
        self._each(sent)


def _chip_exchange(slabs, whole, only_c):
    ex = _ChipExchange(slabs, whole, only_c)
    n = ex.n

    def body(*refs):
        ins, outs, sems = refs[:n], refs[n:2 * n], refs[2 * n:]
        ex.start(ins, outs, sems)
        ex.finish(ins, outs, sems)

    res = pl.pallas_call(
        body, name="chip_exchange", in_specs=ex.in_specs, out_specs=ex.out_specs, out_shape=ex.out_shape,
        scratch_shapes=ex.scratch,
    )(*slabs, *whole)
    return list(res[:ex.ns]), list(res[ex.ns:])


def _sum_chips(recv, own, name, owner_c=None):
    n, r, C = recv.shape
    rt = _pick(r, _rows_for(C), 16)
    own3 = own.ndim == 3

    def body(on_ref, r_ref, o_ref, out_ref):
        me = on_ref[1]
        acc = None
        for q in range(n):
            term = jnp.where(me == q, o_ref[...], r_ref[q]).astype(f32)
            acc = term if acc is None else acc + term
        out_ref[...] = acc

    blk = pl.BlockSpec((n, rt, C), lambda i, on: (0, i * on[0], 0))
    row = pl.BlockSpec((rt, C), lambda i, on: (i * on[0], 0))
    own_slab = pl.BlockSpec((None, rt, C), lambda i, on: (on[1], i * on[0], 0))
    flag = jnp.ones((1,), jnp.int32) if owner_c is None else _owner_flag(owner_c)
    me = (2 * lax.axis_index("x") + lax.axis_index("y")).astype(jnp.int32).reshape(1)
    return pl.pallas_call(
        body, name=name,
        grid_spec=pltpu.PrefetchScalarGridSpec(num_scalar_prefetch=1, grid=(r // rt,),
                                               in_specs=[blk, own_slab if own3 else row], out_specs=row),
        out_shape=jax.ShapeDtypeStruct((r, C), f32), compiler_params=_cp("arbitrary"),
    )(jnp.concatenate([flag, me]), recv, own)


def _adamw_layers(mine, theirs, owners, w, m, v, name):
    L, r, C = w.shape
    rt = _pick(r, _rows_for(C), 16)

    def body(own_ref, a0_ref, a1_ref, b0_ref, b1_ref, w_ref, m_ref, v_ref, g_ref, d_ref, mo_ref, vo_ref):
        layer = pl.program_id(0)
        g0 = jnp.where(own_ref[0] == 1, a0_ref[...], b0_ref[...])
        g1 = jnp.where(own_ref[1] == 1, a1_ref[...], b1_ref[...])
        g_ref[...] = jnp.where(layer == 0, g0, g1)
        _adamw_math(g_ref, w_ref, m_ref, v_ref, d_ref, mo_ref, vo_ref)

    def flat(layer, mine_side):
        def index(l, i, own):
            use = (l == layer) & (own[layer] == (1 if mine_side else 0))
            return (jnp.where(use, i, 0), 0)
        return pl.BlockSpec((rt, C), index)

    lay = pl.BlockSpec((None, rt, C), lambda l, i, own: (l, i, 0))
    c = lax.axis_index("c")
    own = jnp.stack([(c == owners[0]).astype(jnp.int32), (c == owners[1]).astype(jnp.int32)])
    return pl.pallas_call(
        body, name=name,
        grid_spec=pltpu.PrefetchScalarGridSpec(
            num_scalar_prefetch=1, grid=(L, r // rt),
            in_specs=[flat(0, True), flat(1, True), flat(0, False), flat(1, False)] + [lay] * 3, out_specs=[lay] * 4),
        out_shape=[jax.ShapeDtypeStruct((L, r, C), f32)] * 4, compiler_params=_cp("arbitrary", "arbitrary"),
    )(own, mine[0], mine[1], theirs[0], theirs[1], w, m, v)


def _adamw_math(g_ref, w_ref, m_ref, v_ref, d_ref, mo_ref, vo_ref):
    gg = g_ref[...]
    m_new = ADAM_B1 * m_ref[...] + (1.0 - ADAM_B1) * gg
    v_new = ADAM_B2 * v_ref[...] + (1.0 - ADAM_B2) * (gg * gg)
    m_hat = m_new / (1.0 - ADAM_B1 ** ADAM_STEP)
    v_hat = v_new / (1.0 - ADAM_B2 ** ADAM_STEP)
    d_ref[...] = -ADAM_LR * (m_hat / (jnp.sqrt(v_hat) + ADAM_EPS) + ADAM_WD * w_ref[...])
    mo_ref[...] = m_new
    vo_ref[...] = v_new


FLAT_TILE = 2048


def _add2(a, b, name):
    R = a.shape[0]
    rt = _pick(R, FLAT_TILE, SUBLANES)

    def body(a_ref, b_ref, o_ref):
        o_ref[...] = a_ref[...] + b_ref[...]

    row = pl.BlockSpec((rt, LANES), lambda i: (i, 0))
    return pl.pallas_call(
        body, name=name, grid=(R // rt,), in_specs=[row, row], out_specs=row,
        out_shape=jax.ShapeDtypeStruct((R, LANES), f32), compiler_params=_cp("parallel"),
    )(a, b)


def _adamw(g, w, m, v, name):
    R = g.shape[0]
    rt = _pick(R, FLAT_TILE, SUBLANES)

    def body(g_ref, w_ref, m_ref, v_ref, d_ref, mo_ref, vo_ref):
        _adamw_math(g_ref, w_ref, m_ref, v_ref, d_ref, mo_ref, vo_ref)

    row = pl.BlockSpec((rt, LANES), lambda i: (i, 0))
    return pl.pallas_call(
        body, name=name, grid=(R // rt,), in_specs=[row] * 4, out_specs=[row] * 3,
        out_shape=[jax.ShapeDtypeStruct((R, LANES), f32)] * 3, compiler_params=_cp("parallel"),
    )(g, w, m, v)


def _adamw_whole(g, w, m, v, name):
    def body(g_ref, w_ref, m_ref, v_ref, d_ref, mo_ref, vo_ref):
        _adamw_math(g_ref, w_ref, m_ref, v_ref, d_ref, mo_ref, vo_ref)

    return pl.pallas_call(
        body, name=name, out_shape=[jax.ShapeDtypeStruct(w.shape, f32)] * 3, compiler_params=_cp(),
    )(g, w, m, v)


def _pack(arrs, dtype, rows=None):
    flat = jnp.concatenate([a.astype(dtype).reshape(-1) for a in arrs])
    per = FLAT_TILE * LANES
    total = rows * LANES if rows else flat.shape[0] + (-flat.shape[0]) % per
    flat = jnp.pad(flat, (0, total - flat.shape[0]))
    return flat.reshape(-1, LANES)


def _unpack(buf, shapes):
    flat = buf.reshape(-1)
    out, off = [], 0
    for s in shapes:
        n = math.prod(s)
        out.append(flat[off:off + n].reshape(s))
        off += n
    return out


def _block_diag(w):
    n, a, b = w.shape
    eye = jnp.eye(n, dtype=w.dtype)
    return (w[:, :, None, :] * eye[:, None, :, None]).reshape(n * a, n * b)


def _diag_blocks(m, n):
    a, b = m.shape[0] // n, m.shape[1] // n
    idx = jnp.arange(n)
    return m.reshape(n, a, n, b)[idx, :, idx, :]


BIG = ("w_in", "w_out", "w_up", "w_down", "s5_w_glu")
BIG_COL_SHARDED = {"w_in": True, "w_out": False, "w_up": True, "w_down": False, "s5_w_glu": False}
CONV_SHARDED = ("lru_conv_w", "ffn_conv_w")
SMALL = ("lru_conv_b", "lru_wr", "lru_br", "lru_wi", "lru_bi", "lru_lambda", "s5_a_re", "s5_a_im", "s5_b_re",
         "s5_b_im", "s5_c_re", "s5_c_im", "s5_d", "s5_log_step", "s5_b_glu", "mix_norm_g", "ln1_g", "ln1_b",
         "ffn_conv_b", "ln2_g", "ln2_b")
WEIGHTS = ("w_in", "lru_conv_w", "lru_conv_b", "lru_wr", "lru_br", "lru_wi", "lru_bi", "lru_lambda", "s5_a_re",
           "s5_a_im", "s5_b_re", "s5_b_im", "s5_c_re", "s5_c_im", "s5_d", "s5_log_step", "s5_w_glu", "s5_b_glu",
           "mix_norm_g", "w_out", "ln1_g", "ln1_b", "w_up", "ffn_conv_w", "ffn_conv_b", "w_down", "ln2_g", "ln2_b")


def _assemble(slabs, col_sharded):
    _, L, r, c = slabs.shape
    if col_sharded:
        return slabs.transpose(1, 2, 0, 3).reshape(L, r, 4 * c)
    return slabs.transpose(1, 0, 2, 3).reshape(L, 4 * r, c)


def _s5_prepare(p):
    G = N_S5_GROUPS
    bt_re, bt_im = p["s5_b_re"].transpose(0, 2, 1), p["s5_b_im"].transpose(0, 2, 1)
    ls = p["s5_log_step"].reshape(G, 1)
    ab_re, ab_im, bb_re, bb_im = _s5_params(p["s5_a_re"], p["s5_a_im"], ls, bt_re, bt_im)
    ab = jnp.concatenate([ab_re.reshape(1, S5_LANES), ab_im.reshape(1, S5_LANES)], axis=1)
    bbcat = jnp.concatenate([_block_diag(bb_re), _block_diag(bb_im)], axis=1).astype(bf16)
    ccat = jnp.concatenate([_block_diag(p["s5_c_re"].transpose(0, 2, 1)),
                            -_block_diag(p["s5_c_im"].transpose(0, 2, 1))], axis=0).astype(bf16)
    bbcat_pad = jnp.concatenate([bbcat, jnp.zeros((LRU_WIDTH - S5_WIDTH, 2 * S5_LANES), bf16)], axis=0)
    return dict(bt_re=bt_re, bt_im=bt_im, ls=ls, ab=ab, bbcat=bbcat, bbcat_pad=bbcat_pad, ccat=ccat)


def _layer_fwd(h, p, cos, sin, pending, install):
    sv = {"h": h}
    proj = _mm(h, p["w_in"], "nn", "mm_proj", tn=D_IN_PAD)
    sv.update(proj=proj)
    qr, kr = _rope_fwd(proj, cos, sin)
    attn, ltot, gathered = _attn_fwd2(qr, kr, proj, [s for _, _, s in pending], [True] * len(pending))
    install(pending, gathered)
    sv.update(qr=qr, kr=kr, attn=attn, ltot=ltot)
    wr, wi = _block_diag(p["lru_wr"]).astype(bf16), _block_diag(p["lru_wi"]).astype(bf16)
    xc, r, i, log_a, u = _lru_pre(proj, p["lru_conv_w"], p["lru_conv_b"], wr, p["lru_br"], wi, p["lru_bi"],
                                  p["lru_lambda"])
    hl, lru = _lru_scan(log_a, u, proj)
    sv.update(wr=wr, wi=wi, xc=xc, r=r, i=i, log_a=log_a, hl=hl, lru=lru)
    s5 = _s5_prepare(p)
    bu = _mm(proj, s5["bbcat_pad"], "nn", "mm_s5_bu", a_win=(PROJ_S5_U, LRU_WIDTH))
    xs = _s5_scan(bu, s5["ab"])
    yc = _mm(xs, s5["ccat"], "nn", "mm_s5_y")
    ssm, y = _s5_out(yc, proj, p["s5_d"].reshape(-1), p["s5_w_glu"], p["s5_b_glu"])
    sv.update(s5=s5, xs=xs, y=y, ssm=ssm)
    mixed = _mixnorm(attn, lru, ssm, p["mix_norm_g"])
    mix = _mm(mixed, p["w_out"], "nn", "mm_out")
    h1, z1 = _ln_fwd(h, mix, p["ln1_g"], p["ln1_b"], "ln_fwd")
    sv.update(mixed=mixed, z1=z1, h1=h1)
    up = _mm(h1, p["w_up"], "nn", "mm_up", tn=1536)
    act = _ffn_act(up, p["ffn_conv_w"], p["ffn_conv_b"])
    ffn = _mm(act, p["w_down"], "nn", "mm_down")
    h2, z2 = _ln_fwd(h1, ffn, p["ln2_g"], p["ln2_b"], "ln_fwd")
    sv.update(up=up, act=act, z2=z2)
    return h2, sv


def _layer_bwd(dy_a, dy_b, p, sv, cos, sin, relay=None):
    gr = {}
    dz2, acc = _ln_bwd(dy_a, dy_b, sv["z2"], p["ln2_g"], "ln_bwd_top" if dy_a is None else "ln_bwd")
    gr["ln2_g"], gr["ln2_b"] = acc[0], acc[1]
    dact = _mm(dz2, p["w_down"], "nt", "mm_dact", out_dtype=bf16)
    gr["w_down"] = _mm(sv["act"], dz2, "tn", "mm_dw_down")
    dupc = _ffn_act_bwd(dact, sv["up"], p["ffn_conv_w"], p["ffn_conv_b"])
    others, others_dst, to_slabs = relay if relay else ((), 0, None)
    send = list(others) + ([gr["w_down"]] if relay else [])
    send_dst = [others_dst] * len(others) + ([EARLY_OWNER] if relay else [])
    dup, acc, from_sibling = _conv_bwd(dupc, sv["up"], p["ffn_conv_w"], "ffn_conv_bwd", col_tile=FFN_COL_TILE,
                                       out_dtype=bf16, send=send, send_dst=send_dst)
    slabs, owners = to_slabs(from_sibling[:-1], gr["w_down"], from_sibling[-1]) if relay else ((), 0)
    gr["ffn_conv_w"], gr["ffn_conv_b"] = acc[0:3], acc[3]
    dh1 = _mm(dup, p["w_up"], "nt", "mm_dh1", tk=2048)
    gr["w_up"] = _mm(sv["h1"], dup, "tn", "mm_dw_up", tn=1536)
    dz1, acc = _ln_bwd(dz2, dh1, sv["z1"], p["ln1_g"], "ln_bwd")
    gr["ln1_g"], gr["ln1_b"] = acc[0], acc[1]
    dmixed = _mm(dz1, p["w_out"], "nt", "mm_dmixed")
    gr["w_out"] = _mm(sv["mixed"], dz1, "tn", "mm_dw_out")
    dattn, dlru, dssm, delta, acc = _mixnorm_bwd(dmixed, sv["attn"], sv["lru"], sv["ssm"], p["mix_norm_g"])
    gr["mix_norm_g"] = acc[0]
    proj = sv["proj"]
    late = [gr["w_up"]] if relay else []
    dqr, dkr, dv, received, late_sibling = _attn_bwd2(sv["qr"], sv["kr"], proj, dattn, sv["ltot"], delta, slabs,
                                                      owners, late, EARLY_OWNER)
    dq, dk = _rope_bwd(dqr, dkr, cos, sin)
    g, dgate = _lru_scan_bwd(dlru, proj, sv["hl"], sv["log_a"])
    dxc, dwr, dwi, acc = _lru_gate_bwd(g, sv["hl"], sv["xc"], sv["r"], sv["i"], sv["log_a"], sv["wr"], sv["wi"],
                                       p["lru_lambda"])
    gr["lru_wr"], gr["lru_wi"] = _diag_blocks(dwr, N_LRU_HEADS), _diag_blocks(dwi, N_LRU_HEADS)
    gr["lru_br"], gr["lru_bi"], gr["lru_lambda"] = acc[0], acc[1], acc[2]
    dxr, acc, _ = _conv_bwd(dxc, proj, p["lru_conv_w"], "lru_conv_bwd", x_col_block=PROJ_LRU_X)
    gr["lru_conv_w"], gr["lru_conv_b"] = acc[0:4], acc[4]
    s5 = sv["s5"]
    G = N_S5_GROUPS
    dy, du_direct, dwglu, acc = _s5_out_bwd(dssm, sv["y"], proj, p["s5_d"].reshape(-1), p["s5_w_glu"],
                                            p["s5_b_glu"])
    gr["s5_w_glu"], gr["s5_b_glu"], gr["s5_d"] = dwglu, acc[0], acc[1].reshape(G, S5_GROUP)
    dxs = _mm(dy, s5["ccat"], "nt", "mm_s5_dx")
    dccat = _mm(sv["xs"], dy, "tn", "mm_s5_dc")
    gr["s5_c_re"] = _diag_blocks(dccat[:S5_LANES], G).transpose(0, 2, 1)
    gr["s5_c_im"] = -_diag_blocks(dccat[S5_LANES:], G).transpose(0, 2, 1)
    gs, dab = _s5_scan_bwd(dxs, sv["xs"], s5["ab"])
    du = _mm(gs, s5["bbcat"], "nt", "mm_s5_du", add=du_direct)
    dbbcat = _mm(proj, gs, "tn", "mm_s5_dbb", a_win=(PROJ_S5_U, LRU_WIDTH))[:S5_WIDTH]
    d_ar, d_ai, d_ls, d_btr, d_bti = _s5_params_bwd(
        p["s5_a_re"], p["s5_a_im"], s5["ls"], s5["bt_re"], s5["bt_im"],
        dab[:, :S5_LANES].reshape(G, S5_STATE), dab[:, S5_LANES:].reshape(G, S5_STATE),
        _diag_blocks(dbbcat[:, :S5_LANES], G), _diag_blocks(dbbcat[:, S5_LANES:], G))
    gr["s5_a_re"], gr["s5_a_im"], gr["s5_log_step"] = d_ar, d_ai, d_ls.reshape(G)
    gr["s5_b_re"], gr["s5_b_im"] = d_btr.transpose(0, 2, 1), d_bti.transpose(0, 2, 1)
    pad = jnp.zeros((du.shape[0], D_IN_PAD - D_IN), f32)
    dproj = jnp.concatenate([dq, dk, dv, dxr, dgate, du, pad], axis=1).astype(bf16)
    gr["w_in"] = _mm(sv["h"], dproj, "tn", "mm_dw_in", tn=768)[:, :D_IN]
    dh = _mm(dproj, p["w_in"], "nt", "mm_dh")
    return (dz1, dh, gr, slabs, received, late_sibling) if relay else (dz1, dh, gr)


def _train_step(d):
    x, target = d["x"][0], d["loss_target"][0]
    S = x.shape[0]
    me = 2 * lax.axis_index("x") + lax.axis_index("y")

    def rows2d(a):
        return a.reshape(a.shape[0] * a.shape[1], a.shape[2])

    params = [{n: d[n][l] for n in SMALL} for l in range(DEPTH)]

    def install(items, gathered):
        for (n, l, mine), g in zip(items, gathered):
            g = lax.dynamic_update_slice_in_dim(g, mine[None], me, axis=0)
            if n in CONV_SHARDED:
                full = _assemble(g.reshape((4,) + d[n].shape), True)
                for k in range(DEPTH):
                    params[k][n] = full[k]
                continue
            full = _assemble(g[:, None], BIG_COL_SHARDED[n])[0]
            if n == "w_in":
                full = jnp.pad(full, ((0, 0), (0, D_IN_PAD - D_IN)))
            params[l][n] = full

    def shard(n, l):
        return (n, l, d[n][l].astype(bf16))

    first = [shard("w_in", 0)] + [(n, None, rows2d(d[n])) for n in CONV_SHARDED]
    install(first, _gather_weights([s for _, _, s in first], [True] + [False] * len(CONV_SHARDED)))
    later = [[shard(n, 0) for n in BIG[1:]] + [shard("w_in", 1)], [shard(n, 1) for n in BIG[1:]]]

    cos, sin = _rope_tables(S)
    h, saved = x, []
    for l in range(DEPTH):
        h, sv = _layer_fwd(h, params[l], cos, sin, later[l], install)
        saved.append(sv)
    dy, loss_acc = _loss_head(h, target)
    def slab(n, g, other, owner):
        aligned = BIG_COL_SHARDED[n] and (g.shape[1] // 4) % LANES == 0
        p = _pair_sum(g, other, "pair_sum_" + n, owner, col_slabs=aligned)
        if BIG_COL_SHARDED[n] and not aligned:
            return p.reshape(p.shape[0], 4, p.shape[1] // 4).transpose(1, 0, 2)
        return p if aligned else p.reshape(4, p.shape[0] // 4, p.shape[1])

    own1 = {n: 1 - EARLY_OWNER for n in BIG}
    own0 = {n: (EARLY_OWNER if n in ("w_down", "w_up") else 1 - EARLY_OWNER) for n in BIG}

    def hidden_slabs(others1, w_down0, w_down0_sibling):
        slabs = [slab(n, grads[1][n], o, own1[n]) for n, o in zip(BIG, others1)]
        slabs.append(slab("w_down", w_down0, w_down0_sibling, own0["w_down"]))
        return slabs, [own1[n] for n in BIG] + [own0["w_down"]]

    da, db, grads = None, dy, [None] * DEPTH
    da, db, grads[1] = _layer_bwd(da, db, params[1], saved[1], cos, sin)
    relay = ([grads[1][n] for n in BIG], 1 - EARLY_OWNER, hidden_slabs)
    da, db, grads[0], hslabs, hrecv, (w_up0_sibling,) = _layer_bwd(da, db, params[0], saved[0], cos, sin, relay)
    out = {"grad_x": _axpy(da, db, "grad_x")[None]}

    small = SMALL + CONV_SHARDED
    sp = _pack([grads[l][n] for n in small for l in range(DEPTH)], f32)
    tail = [n for n in BIG if n != "w_down"]
    rest = [n for n in tail if n != "w_up"]
    others_rest, (sp_sibling,) = _sibling_send([grads[0][n] for n in rest], [own0[n] for n in rest], [sp])
    sibling0 = dict(zip(rest, others_rest), w_up=w_up0_sibling)
    tslabs = [slab(n, grads[0][n], sibling0[n], own0[n]) for n in tail]
    chip_small = _add2(sp, sp_sibling, "pair_sum_small")
    trecv, (recv_small,) = _chip_exchange(tslabs, [chip_small], [own0[n] for n in tail])
    mine0 = {n: _sum_chips(r, s, "sum_chips_" + n, own0[n]) for n, r, s in zip(tail, trecv, tslabs)}
    mine0["w_down"] = _sum_chips(hrecv[-1], hslabs[-1], "sum_chips_w_down", own0["w_down"])
    mine1 = {n: _sum_chips(r, s, "sum_chips_" + n, own1[n]) for n, r, s in zip(BIG, hrecv, hslabs)}
    sent, _ = _sibling_send([mine0[n] for n in BIG] + [mine1[n] for n in BIG],
                            [1 - own0[n] for n in BIG] + [1 - own1[n] for n in BIG])
    theirs0, theirs1 = dict(zip(BIG, sent[:len(BIG)])), dict(zip(BIG, sent[len(BIG):]))
    for n in BIG:
        upd = _adamw_layers((mine0[n], mine1[n]), (theirs0[n], theirs1[n]), (own0[n], own1[n]),
                            d[n], d["m_" + n], d["v_" + n], "adamw_" + n)
        for pre, u in zip(("grad_", "delta_", "new_m_", "new_v_"), upd):
            out[pre + n] = u

    total = _sum_chips(recv_small, chip_small, "sum_chips_small")
    rows = total.shape[0]
    upd = _adamw(total, _pack([d[n] for n in SMALL], f32, rows), _pack([d["m_" + n] for n in SMALL], f32, rows),
                 _pack([d["v_" + n] for n in SMALL], f32, rows), "adamw_small")
    small_shapes = [d[n].shape for n in SMALL]
    for pre, buf in zip(("grad_", "delta_", "new_m_", "new_v_"), (total,) + tuple(upd)):
        for n, a in zip(SMALL, _unpack(buf, small_shapes)):
            out[pre + n] = a
    conv_full = _unpack(total, small_shapes + [(DEPTH,) + grads[0][n].shape for n in CONV_SHARDED])
    for n, g in zip(CONV_SHARDED, conv_full[len(SMALL):]):
        L, K, C = g.shape
        g = lax.dynamic_index_in_dim(g.reshape(L, K, 4, C // 4), me, axis=2, keepdims=False)
        out["grad_" + n] = g
        for pre, u in zip(("delta_", "new_m_", "new_v_"), _adamw_whole(g, d[n], d["m_" + n], d["v_" + n], "adamw_" + n)):
            out[pre + n] = u

    loss_local, _ = lax.optimization_barrier((loss_acc[0, 0], upd[0]))
    out["loss"] = lax.psum(loss_local, ("x", "y", "c"))
    return (out["loss"], out["grad_x"]) + tuple(out[pre + n] for pre in ("grad_", "delta_", "new_m_", "new_v_")
                                                for n in WEIGHTS)


def kernel(
        x, w_in, lru_conv_w, lru_conv_b, lru_wr, lru_br, lru_wi, lru_bi, lru_lambda, s5_a_re, s5_a_im, s5_b_re,
        s5_b_im, s5_c_re, s5_c_im, s5_d, s5_log_step, s5_w_glu, s5_b_glu, mix_norm_g, w_out, ln1_g, ln1_b, w_up,
        ffn_conv_w, ffn_conv_b, w_down, ln2_g, ln2_b, loss_target, m_w_in, m_lru_conv_w, m_lru_conv_b, m_lru_wr,
        m_lru_br, m_lru_wi, m_lru_bi, m_lru_lambda, m_s5_a_re, m_s5_a_im, m_s5_b_re, m_s5_b_im, m_s5_c_re,
        m_s5_c_im, m_s5_d, m_s5_log_step, m_s5_w_glu, m_s5_b_glu, m_mix_norm_g, m_w_out, m_ln1_g, m_ln1_b,
        m_w_up, m_ffn_conv_w, m_ffn_conv_b, m_w_down, m_ln2_g, m_ln2_b, v_w_in, v_lru_conv_w, v_lru_conv_b,
        v_lru_wr, v_lru_br, v_lru_wi, v_lru_bi, v_lru_lambda, v_s5_a_re, v_s5_a_im, v_s5_b_re, v_s5_b_im,
        v_s5_c_re, v_s5_c_im, v_s5_d, v_s5_log_step, v_s5_w_glu, v_s5_b_glu, v_mix_norm_g, v_w_out, v_ln1_g,
        v_ln1_b, v_w_up, v_ffn_conv_w, v_ffn_conv_b, v_w_down, v_ln2_g, v_ln2_b
):
    return _train_step(dict(locals()))
```

```python
import math

import jax
import jax.numpy as jnp
from jax import lax
from jax.experimental import pallas as pl
from jax.experimental.pallas import tpu as pltpu

f32 = jnp.float32
bf16 = jnp.bfloat16
MESH = pl.DeviceIdType.MESH

D_MODEL = 1024
ATTN_WIDTH = 384
LRU_WIDTH = 384
S5_WIDTH = 256
HEAD_DIM = 64
N_LRU_HEADS = 6
N_S5_GROUPS = 16
S5_GROUP = 16
S5_STATE = 64
S5_LANES = N_S5_GROUPS * S5_STATE
D_FF = 3072
D_IN = 2176
LRU_C = 8.0
ROPE_THETA = 10000.0
DILATIONS = (1, 4, 16)
ATTN_BLOCK = 128
DEPTH = 2
ALPHA = (2 * DEPTH) ** 0.25
LN_EPS = 1e-5
RMS_EPS = 1e-6
ADAM_LR, ADAM_B1, ADAM_B2, ADAM_EPS, ADAM_WD, ADAM_STEP = 0.001, 0.9, 0.999, 1e-08, 0.01, 10

SUBLANES = 8
LANES = 128
VMEM_LIMIT = 56 * 1024 * 1024
ROW_TILE = 512
MM_SINGLE_K = 3072
D_IN_PAD = 2304
NEG = -1e30


def _cp(*sem):
    return pltpu.CompilerParams(dimension_semantics=sem if sem else None, vmem_limit_bytes=VMEM_LIMIT)


def _pick(dim, pref, align=LANES):
    if dim <= pref:
        return dim
    t = (pref // align) * align
    while t >= align:
        if dim % t == 0:
            return t
        t -= align
    return dim


def _gelu(x):
    return jax.nn.gelu(x)


def _gelu_grad(x):
    c = math.sqrt(2.0 / math.pi)
    t = jnp.tanh(c * (x + 0.044715 * x * x * x))
    return 0.5 * (1.0 + t) + 0.5 * x * (1.0 - t * t) * c * (1.0 + 3 * 0.044715 * x * x)


def _gelu_pair(x):
    c = math.sqrt(2.0 / math.pi)
    x2 = x * x
    t = jnp.tanh(c * x * (1.0 + 0.044715 * x2))
    return 0.5 * x * (1.0 + t), 0.5 * (1.0 + t) + 0.5 * x * (1.0 - t * t) * c * (1.0 + 3 * 0.044715 * x2)


def _sigmoid(x):
    return jax.nn.sigmoid(x)


def _expm1(x):
    p = 1.0 + x / 9.0
    for n in (8.0, 7.0, 6.0, 5.0, 4.0, 3.0, 2.0):
        p = 1.0 + (x / n) * p
    return jnp.where(jnp.abs(x) < 0.3, x * p, jnp.exp(x) - 1.0)


def _dot(a, b, dims):
    return lax.dot_general(a, b, (dims, ((), ())), preferred_element_type=f32)


NN = ((1,), (0,))
NT = ((1,), (1,))
TN = ((0,), (0,))


def _mm(a, b, mode, name, out_dtype=f32, tm=1024, tn=1024, tk=1024, add=None, a_win=None):
    if mode == "nn":
        (M, K), N = a.shape, b.shape[1]
    elif mode == "nt":
        (M, K), N = a.shape, b.shape[0]
    else:
        (K, M), N = a.shape, b.shape[1]
    win = 0
    if a_win is not None:
        win, w = a_win
        if mode == "tn":
            M, tm = w, w
        else:
            K = w
    single = mode != "tn" and K <= MM_SINGLE_K
    tm, tn = _pick(M, tm), _pick(N, tn)
    tk = K if single else _pick(K, tk)
    nk = K // tk
    dims = {"nn": NN, "nt": NT, "tn": TN}[mode]

    def body(a_ref, b_ref, *rest):
        prod = _dot(a_ref[...].astype(bf16), b_ref[...].astype(bf16), dims)
        if single:
            o_ref = rest[-1]
            o_ref[...] = (prod if add is None else prod + rest[0][...]).astype(o_ref.dtype)
            return
        o_ref, acc = rest[-2:]
        k = pl.program_id(2)

        @pl.when(k == 0)
        def _():
            acc[...] = prod if add is None else prod + rest[0][...]

        @pl.when(k > 0)
        def _():
            acc[...] += prod

        @pl.when(k == nk - 1)
        def _():
            o_ref[...] = acc[...].astype(o_ref.dtype)

    if mode == "tn":
        a_spec = pl.BlockSpec((tk, tm), lambda i, j, k: (k, i + win))
    else:
        a_spec = pl.BlockSpec((tm, tk), lambda i, j, k: (i, k + win))
    if mode == "nt":
        b_spec = pl.BlockSpec((tn, tk), lambda i, j, k: (j, k))
    else:
        b_spec = pl.BlockSpec((tk, tn), lambda i, j, k: (k, j))
    o_spec = pl.BlockSpec((tm, tn), lambda i, j, k: (i, j))
    return pl.pallas_call(
        body, name=name, grid=(M // tm, N // tn, nk),
        in_specs=[a_spec, b_spec] + ([] if add is None else [o_spec]), out_specs=o_spec,
        out_shape=jax.ShapeDtypeStruct((M, N), out_dtype),
        scratch_shapes=[] if single else [pltpu.VMEM((tm, tn), f32)],
        compiler_params=_cp("parallel", "parallel", "arbitrary"),
    )(*((a, b) if add is None else (a, b, add)))


def _shift_down(cur, prev8, k):
    if k == 0:
        return cur
    T, (R, C) = SUBLANES, cur.shape
    rot = pltpu.roll(cur.reshape(R // T, T, C), k, 1)
    before = jnp.concatenate([pltpu.roll(prev8, k, 0)[None], rot[:-1]], axis=0)
    row = lax.broadcasted_iota(jnp.int32, (R // T, T, C), 1)
    return jnp.where(row < k, before, rot).reshape(R, C)


def _shift_up(cur, next8, k):
    if k == 0:
        return cur
    T, (R, C) = SUBLANES, cur.shape
    rot = pltpu.roll(cur.reshape(R // T, T, C), T - k, 1)
    after = jnp.concatenate([rot[1:], pltpu.roll(next8, T - k, 0)[None]], axis=0)
    row = lax.broadcasted_iota(jnp.int32, (R // T, T, C), 1)
    return jnp.where(row < T - k, rot, after).reshape(R, C)


def _prev_halo_spec(rt, cols, ncolblk_fn):
    per = rt // SUBLANES
    return pl.BlockSpec((SUBLANES, cols), lambda *g: (jnp.maximum(g[-1] * per - 1, 0), ncolblk_fn(*g)))


def _mm_ln_fwd(a, w, h, g, b, name):
    S, K = a.shape
    D = w.shape[1]
    rt = _pick(S, ROW_TILE, SUBLANES)

    def body(a_ref, w_ref, h_ref, g_ref, b_ref, o_ref, z_ref):
        z = ALPHA * h_ref[...] + _dot(a_ref[...].astype(bf16), w_ref[...].astype(bf16), NN)
        mu = jnp.mean(z, axis=-1, keepdims=True)
        zc = z - mu
        var = jnp.mean(zc * zc, axis=-1, keepdims=True)
        o_ref[...] = zc * lax.rsqrt(var + LN_EPS) * g_ref[...] + b_ref[...]
        z_ref[...] = z

    row = pl.BlockSpec((rt, D), lambda i: (i, 0))
    vec = pl.BlockSpec((1, D), lambda i: (0, 0))
    return pl.pallas_call(
        body, name=name, grid=(S // rt,),
        in_specs=[pl.BlockSpec((rt, K), lambda i: (i, 0)), pl.BlockSpec((K, D), lambda i: (0, 0)), row, vec, vec],
        out_specs=[row, row], out_shape=[jax.ShapeDtypeStruct((S, D), f32)] * 2, compiler_params=_cp("parallel"),
    )(a, w, h, g.reshape(1, D), b.reshape(1, D))


def _ln_bwd(dy_a, dy_b, z, g, name):
    S, D = z.shape
    rt = _pick(S, ROW_TILE, SUBLANES)
    two = dy_a is not None

    def body(*refs):
        if two:
            a_ref, b_ref, z_ref, g_ref, dz_ref, acc_ref = refs
            dy = ALPHA * a_ref[...] + b_ref[...]
        else:
            b_ref, z_ref, g_ref, dz_ref, acc_ref = refs
            dy = b_ref[...]
        z = z_ref[...]
        mu = jnp.mean(z, axis=-1, keepdims=True)
        zc = z - mu
        var = jnp.mean(zc * zc, axis=-1, keepdims=True)
        rstd = lax.rsqrt(var + LN_EPS)
        xhat = zc * rstd
        dxh = dy * g_ref[...]
        m1 = jnp.mean(dxh, axis=-1, keepdims=True)
        m2 = jnp.mean(dxh * xhat, axis=-1, keepdims=True)
        dz_ref[...] = rstd * (dxh - m1 - xhat * m2)

        @pl.when(pl.program_id(0) == 0)
        def _():
            acc_ref[...] = jnp.zeros_like(acc_ref)

        acc_ref[0:1, :] += jnp.sum(dy * xhat, axis=0, keepdims=True)
        acc_ref[1:2, :] += jnp.sum(dy, axis=0, keepdims=True)

    row = pl.BlockSpec((rt, D), lambda i: (i, 0))
    vec = pl.BlockSpec((1, D), lambda i: (0, 0))
    acc = pl.BlockSpec((SUBLANES, D), lambda i: (0, 0))
    ins = ([dy_a] if two else []) + [dy_b, z, g.reshape(1, D)]
    return pl.pallas_call(
        body, name=name, grid=(S // rt,), in_specs=[row] * (len(ins) - 1) + [vec], out_specs=[row, acc],
        out_shape=[jax.ShapeDtypeStruct((S, D), f32), jax.ShapeDtypeStruct((SUBLANES, D), f32)],
        compiler_params=_cp("arbitrary"),
    )(*ins)


def _loss_head(y, target):
    S, D = y.shape
    rt = _pick(S, ROW_TILE, SUBLANES)

    def body(y_ref, t_ref, dy_ref, acc_ref):
        e = y_ref[...] - t_ref[...]
        dy_ref[...] = e * (1.0 / D)

        @pl.when(pl.program_id(0) == 0)
        def _():
            acc_ref[...] = jnp.zeros_like(acc_ref)

        part = jnp.sum(jnp.mean(e * e, axis=-1, keepdims=True), axis=0, keepdims=True)
        acc_ref[...] += 0.5 * part

    row = pl.BlockSpec((rt, D), lambda i: (i, 0))
    return pl.pallas_call(
        body, name="loss_head", grid=(S // rt,), in_specs=[row, row],
        out_specs=[row, pl.BlockSpec((1, 1), lambda i: (0, 0))],
        out_shape=[jax.ShapeDtypeStruct((S, D), f32), jax.ShapeDtypeStruct((1, 1), f32)],
        compiler_params=_cp("arbitrary"),
    )(y, target)


def _axpy(a, b, name):
    S, D = a.shape
    rt = _pick(S, ROW_TILE, SUBLANES)

    def body(a_ref, b_ref, o_ref):
        o_ref[...] = ALPHA * a_ref[...] + b_ref[...]

    row = pl.BlockSpec((rt, D), lambda i: (i, 0))
    return pl.pallas_call(
        body, name=name, grid=(S // rt,), in_specs=[row, row], out_specs=row,
        out_shape=jax.ShapeDtypeStruct((S, D), f32), compiler_params=_cp("parallel"),
    )(a, b)


def _rope_tables(S):
    rt = _pick(S, ROW_TILE, SUBLANES)

    def body(c_ref, s_ref):
        pos = (pl.program_id(0) * rt + lax.broadcasted_iota(jnp.int32, (rt, LANES), 0)).astype(f32)
        lane = lax.broadcasted_iota(jnp.int32, (rt, LANES), 1)
        j = (lane % (HEAD_DIM // 2)).astype(f32)
        inv = jnp.exp((-j * 2.0 / HEAD_DIM) * math.log(ROPE_THETA))
        ang = pos * inv
        c = jnp.cos(ang)
        s = jnp.where(lane % HEAD_DIM < HEAD_DIM // 2, -jnp.sin(ang), jnp.sin(ang))
        c_ref[...] = jnp.concatenate([c, c, c], axis=1)
        s_ref[...] = jnp.concatenate([s, s, s], axis=1)

    row = pl.BlockSpec((rt, ATTN_WIDTH), lambda i: (i, 0))
    return pl.pallas_call(
        body, name="rope_tables", grid=(S // rt,), in_specs=[], out_specs=[row, row],
        out_shape=[jax.ShapeDtypeStruct((S, ATTN_WIDTH), f32)] * 2, compiler_params=_cp("parallel"),
    )()


def _swap_halves(x):
    lane = lax.broadcasted_iota(jnp.int32, x.shape, 1)
    half = HEAD_DIM // 2
    return jnp.where(lane % HEAD_DIM < half, pltpu.roll(x, x.shape[1] - half, 1), pltpu.roll(x, half, 1))


def _rope_fwd(proj, cos, sin):
    S, W = proj.shape[0], ATTN_WIDTH
    rt = _pick(S, ROW_TILE, SUBLANES)

    def body(q_ref, k_ref, c_ref, s_ref, qo_ref, ko_ref):
        c, s = c_ref[...], s_ref[...]
        qo_ref[...] = q_ref[...] * c + _swap_halves(q_ref[...]) * s
        ko_ref[...] = k_ref[...] * c + _swap_halves(k_ref[...]) * s

    row = pl.BlockSpec((rt, W), lambda i: (i, 0))
    return pl.pallas_call(
        body, name="rope_fwd", grid=(S // rt,), in_specs=[row, pl.BlockSpec((rt, W), lambda i: (i, 1)), row, row],
        out_specs=[row, row], out_shape=[jax.ShapeDtypeStruct((S, W), f32)] * 2, compiler_params=_cp("parallel"),
    )(proj, proj, cos, sin)


def _rope_bwd(dq, dk, cos, sin):
    S, W = dq.shape
    rt = _pick(S, ROW_TILE, SUBLANES)

    def body(q_ref, k_ref, c_ref, s_ref, qo_ref, ko_ref):
        c, s = c_ref[...], s_ref[...]
        qo_ref[...] = q_ref[...] * c + _swap_halves(q_ref[...] * s)
        ko_ref[...] = k_ref[...] * c + _swap_halves(k_ref[...] * s)

    row = pl.BlockSpec((rt, W), lambda i: (i, 0))
    return pl.pallas_call(
        body, name="rope_bwd", grid=(S // rt,), in_specs=[row] * 4, out_specs=[row] * 2,
        out_shape=[jax.ShapeDtypeStruct((S, W), f32)] * 2, compiler_params=_cp("parallel"),
    )(dq, dk, cos, sin)


def _rows(ref, start, d):
    if d == 1:
        return ref[pl.ds(pl.multiple_of(start, ATTN_BLOCK), ATTN_BLOCK), :]
    return ref[pl.ds(start, ATTN_BLOCK, stride=d), :]


def _set_rows(ref, start, d, val):
    if d == 1:
        ref[pl.ds(pl.multiple_of(start, ATTN_BLOCK), ATTN_BLOCK), :] = val
    else:
        ref[pl.ds(start, ATTN_BLOCK, stride=d), :] = val


def _pair_spec(S, first_block):
    return pl.BlockSpec((S, LANES), lambda p: (0, p + first_block))


def _attn_fwd2(qr, kr, proj, shards=(), split=()):
    S = qr.shape[0]
    B = ATTN_BLOCK
    nb = S // B
    scale = HEAD_DIM ** -0.5

    gather = _Gather(shards, split)
    nt = gather.nt

    def body(*refs):
        q_ref, k_ref, v_ref = refs[:3]
        g_ins = refs[3:3 + nt]
        o_ref, l_ref = refs[3 + nt:5 + nt]
        g_outs = refs[5 + nt:5 + 2 * nt]
        m_s, l_s = refs[5 + 2 * nt:7 + 2 * nt]
        g_sems = refs[7 + 2 * nt:]
        if nt:
            @pl.when(pl.program_id(0) == 0)
            def _():
                gather.start(g_ins, g_outs, g_sems)

        qi = lax.broadcasted_iota(jnp.int32, (B, 2 * B), 0)
        ki = lax.broadcasted_iota(jnp.int32, (B, 2 * B), 1)
        dist = qi + B - ki
        band = (dist >= 0) & (dist <= B)
        for bi, d in enumerate(DILATIONS):
            bpc = nb // d

            def blk(b, carry, bi=bi, d=d, bpc=bpc):
                c, n = b // bpc, b % bpc
                start = c + d * B * n
                pstart = c + d * B * jnp.maximum(n - 1, 0)
                valid = band & ((ki >= B) | (n > 0))
                q = _rows(q_ref, start, d).astype(bf16)
                kcat = jnp.concatenate([_rows(k_ref, pstart, d), _rows(k_ref, start, d)], axis=0).astype(bf16)
                vcat = jnp.concatenate([_rows(v_ref, pstart, d), _rows(v_ref, start, d)], axis=0).astype(bf16)
                if bi > 0:
                    m_old, l_old, a_old = _rows(m_s, start, d), _rows(l_s, start, d), _rows(o_ref, start, d)
                ms, ls, accs = [], [], []
                for h in range(2):
                    sl = slice(h * HEAD_DIM, (h + 1) * HEAD_DIM)
                    c0 = h * HEAD_DIM
                    s = jnp.where(valid, _dot(q[:, sl], kcat[:, sl], NT) * scale, NEG)
                    m = jnp.max(s, axis=1, keepdims=True)
                    if bi > 0:
                        mo = m_old[:, c0:c0 + 1]
                        m = jnp.maximum(m, mo)
                        alpha = jnp.exp(mo - m)
                    p = jnp.exp(s - m)
                    l = jnp.sum(p, axis=1, keepdims=True)
                    acc = _dot(p.astype(bf16), vcat[:, sl], NN)
                    if bi > 0:
                        l = l + alpha * l_old[:, c0:c0 + 1]
                        acc = acc + alpha * a_old[:, sl]
                    ms.append(jnp.broadcast_to(m, (B, HEAD_DIM)))
                    ls.append(jnp.broadcast_to(l, (B, HEAD_DIM)))
                    accs.append(acc)
                _set_rows(m_s, start, d, jnp.concatenate(ms, axis=1))
                _set_rows(l_s, start, d, jnp.concatenate(ls, axis=1))
                _set_rows(o_ref, start, d, jnp.concatenate(accs, axis=1))
                return carry

            lax.fori_loop(0, nb, blk, 0, unroll=4)

        def fin(t, carry):
            rows = pl.ds(pl.multiple_of(t * B, B), B)
            l = l_s[rows, :]
            o_ref[rows, :] = o_ref[rows, :] / l
            l_ref[rows, :] = m_s[rows, :] + jnp.log(l)
            return carry

        lax.fori_loop(0, nb, fin, 0)
        if nt:
            @pl.when(pl.program_id(0) == pl.num_programs(0) - 1)
            def _():
                gather.finish(g_ins, g_outs, g_sems)

    pair = _pair_spec(S, 0)
    res = pl.pallas_call(
        body, name="attn_fwd_gather" if nt else "attn_fwd", grid=(3,),
        in_specs=[pair, pair, _pair_spec(S, 2 * ATTN_WIDTH // LANES)] + gather.in_specs,
        out_specs=[pair, pair] + gather.out_specs,
        out_shape=[jax.ShapeDtypeStruct((S, ATTN_WIDTH), f32)] * 2 + gather.out_shape,
        scratch_shapes=[pltpu.VMEM((S, LANES), f32)] * 2 + gather.scratch,
        compiler_params=_cp("arbitrary"),
    )(qr, kr, proj, *shards)
    return res[0], res[1], list(res[2:])


def _attn_bwd2(qr, kr, proj, dattn, ltot, delta, slabs=(), only_c=0, send=(), send_dst=0):
    S = qr.shape[0]
    B = ATTN_BLOCK
    nb = S // B
    scale = HEAD_DIM ** -0.5
    ex = _ChipExchange(slabs, (), only_c)
    snd = _SiblingSend(send, send_dst)
    n, ns = ex.n, snd.n
    hosted = n + ns

    def body(*refs):
        q_ref, k_ref, v_ref, do_ref, l_ref, d_ref = refs[:6]
        x_ins, s_ins = refs[6:6 + n], refs[6 + n:6 + hosted]
        dq_ref, dk_ref, dv_ref = refs[6 + hosted:9 + hosted]
        x_outs, s_outs = refs[9 + hosted:9 + hosted + n], refs[9 + hosted + n:9 + 2 * hosted]
        sems = refs[9 + 2 * hosted:]
        x_sems, s_sems = sems[:len(ex.scratch)], sems[len(ex.scratch):]
        if hosted:
            @pl.when(pl.program_id(0) == 0)
            def _():
                if n:
                    ex.start(x_ins, x_outs, x_sems)
                if ns:
                    snd.start(s_ins, s_outs, s_sems)

        qi = lax.broadcasted_iota(jnp.int32, (B, 2 * B), 0)
        ki = lax.broadcasted_iota(jnp.int32, (B, 2 * B), 1)
        dist1 = qi + B - ki
        band1 = (dist1 >= 0) & (dist1 <= B)
        ri = lax.broadcasted_iota(jnp.int32, (2 * B, B), 0)
        ci = lax.broadcasted_iota(jnp.int32, (2 * B, B), 1)
        dist2 = ri - ci
        band2 = (dist2 >= 0) & (dist2 <= B)
        for bi, d in enumerate(DILATIONS):
            bpc = nb // d

            def blk(b, carry, bi=bi, d=d, bpc=bpc):
                c, n = b // bpc, b % bpc
                start = c + d * B * n
                pstart = c + d * B * jnp.maximum(n - 1, 0)
                nstart = c + d * B * jnp.minimum(n + 1, bpc - 1)
                valid1 = band1 & ((ki >= B) | (n > 0))
                valid2 = band2 & ((ri < B) | (n + 1 < bpc))
                q_c, q_n = _rows(q_ref, start, d), _rows(q_ref, nstart, d)
                k_p, k_c = _rows(k_ref, pstart, d), _rows(k_ref, start, d)
                v_p, v_c = _rows(v_ref, pstart, d), _rows(v_ref, start, d)
                do_c, do_n = _rows(do_ref, start, d), _rows(do_ref, nstart, d)
                l_c, l_n = _rows(l_ref, start, d), _rows(l_ref, nstart, d)
                d_c, d_n = _rows(d_ref, start, d), _rows(d_ref, nstart, d)
                qc = q_c.astype(bf16)
                qcat = jnp.concatenate([q_c, q_n], axis=0).astype(bf16)
                kc = k_c.astype(bf16)
                kcat = jnp.concatenate([k_p, k_c], axis=0).astype(bf16)
                vc = v_c.astype(bf16)
                vcat = jnp.concatenate([v_p, v_c], axis=0).astype(bf16)
                doc = do_c.astype(bf16)
                docat = jnp.concatenate([do_c, do_n], axis=0).astype(bf16)
                lcat = jnp.concatenate([l_c, l_n], axis=0)
                dcat = jnp.concatenate([d_c, d_n], axis=0)
                dqs, dks, dvs = [], [], []
                for h in range(2):
                    sl = slice(h * HEAD_DIM, (h + 1) * HEAD_DIM)
                    c0 = h * HEAD_DIM
                    s1 = _dot(qc[:, sl], kcat[:, sl], NT) * scale
                    p1 = jnp.where(valid1, jnp.exp(s1 - l_c[:, c0:c0 + 1]), 0.0)
                    dp1 = _dot(doc[:, sl], vcat[:, sl], NT)
                    ds1 = p1 * (dp1 - d_c[:, c0:c0 + 1]) * scale
                    dqs.append(_dot(ds1.astype(bf16), kcat[:, sl], NN))
                    s2 = _dot(qcat[:, sl], kc[:, sl], NT) * scale
                    p2 = jnp.where(valid2, jnp.exp(s2 - lcat[:, c0:c0 + 1]), 0.0)
                    dvs.append(_dot(p2.astype(bf16), docat[:, sl], TN))
                    dp2 = _dot(docat[:, sl], vc[:, sl], NT)
                    ds2 = p2 * (dp2 - dcat[:, c0:c0 + 1]) * scale
                    dks.append(_dot(ds2.astype(bf16), qcat[:, sl], TN))
                for ref, parts in ((dq_ref, dqs), (dk_ref, dks), (dv_ref, dvs)):
                    new = jnp.concatenate(parts, axis=1)
                    if bi > 0:
                        new = new + _rows(ref, start, d)
                    _set_rows(ref, start, d, new)
                return carry

            lax.fori_loop(0, nb, blk, 0, unroll=4)

        if hosted:
            @pl.when(pl.program_id(0) == pl.num_programs(0) - 1)
            def _():
                if ns:
                    snd.finish(s_ins, s_outs, s_sems)
                if n:
                    ex.finish(x_ins, x_outs, x_sems)

    pair = _pair_spec(S, 0)
    res = pl.pallas_call(
        body, name="attn_bwd_exchange" if hosted else "attn_bwd", grid=(3,),
        in_specs=[pair, pair, _pair_spec(S, 2 * ATTN_WIDTH // LANES), pair, pair, pair] + ex.in_specs + snd.in_specs,
        out_specs=[pair] * 3 + ex.out_specs + snd.out_specs,
        out_shape=[jax.ShapeDtypeStruct((S, ATTN_WIDTH), f32)] * 3 + ex.out_shape + snd.out_shape,
        scratch_shapes=ex.scratch + snd.scratch, compiler_params=_cp("arbitrary"),
    )(qr, kr, proj, dattn, ltot, delta, *slabs, *send)
    return res[0], res[1], res[2], list(res[3:3 + n]), list(res[3 + n:])


def _softplus_neg(lam):
    return jnp.maximum(-lam, 0.0) + jnp.log1p(jnp.exp(-jnp.abs(lam)))


PROJ_LRU_X, PROJ_LRU_GATE, PROJ_S5_U = 3, 4, 5
EARLY_OWNER = 1


def _lru_pre(proj, conv_w, conv_b, wr, br, wi, bi, lam):
    S, W = proj.shape[0], LRU_WIDTH
    rt = _pick(S, ROW_TILE, SUBLANES)
    K = conv_w.shape[0]

    def body(x_ref, xp_ref, cw_ref, cb_ref, wr_ref, br_ref, wi_ref, bi_ref, lam_ref,
             xc_ref, r_ref, i_ref, la_ref, u_ref):
        prev = jnp.where(pl.program_id(0) == 0, 0.0, xp_ref[...])
        x = x_ref[...]
        xc = cb_ref[...] + cw_ref[K - 1:K, :] * x
        for k in range(K - 1):
            xc = xc + cw_ref[k:k + 1, :] * _shift_down(x, prev, K - 1 - k)
        xb = xc.astype(bf16)
        r = _sigmoid(_dot(xb, wr_ref[...], NN) + br_ref[...])
        i = _sigmoid(_dot(xb, wi_ref[...], NN) + bi_ref[...])
        log_a = -LRU_C * r * _softplus_neg(lam_ref[...])
        u = jnp.sqrt(-_expm1(2.0 * log_a)) * (i * xc)
        xc_ref[...], r_ref[...], i_ref[...], la_ref[...], u_ref[...] = xc, r, i, log_a, u

    row = pl.BlockSpec((rt, W), lambda i: (i, 0))
    xrow = pl.BlockSpec((rt, W), lambda i: (i, PROJ_LRU_X))
    halo = _prev_halo_spec(rt, W, lambda i: PROJ_LRU_X)
    vec = pl.BlockSpec((1, W), lambda i: (0, 0))
    return pl.pallas_call(
        body, name="lru_pre", grid=(S // rt,),
        in_specs=[xrow, halo, pl.BlockSpec((K, W), lambda i: (0, 0)), vec,
                  pl.BlockSpec((W, W), lambda i: (0, 0)), vec, pl.BlockSpec((W, W), lambda i: (0, 0)), vec, vec],
        out_specs=[row] * 5, out_shape=[jax.ShapeDtypeStruct((S, W), f32)] * 5, compiler_params=_cp("parallel"),
    )(proj, proj, conv_w, conv_b.reshape(1, W), wr, br.reshape(1, W), wi, bi.reshape(1, W), lam.reshape(1, W))


def _tile_rows(shape):
    return lax.broadcasted_iota(jnp.int32, shape, 0)


def _lru_scan(log_a, u, proj):
    S, W = u.shape
    rt = _pick(S, ROW_TILE, SUBLANES)
    T = SUBLANES

    def body(la_ref, u_ref, g_ref, h_ref, o_ref, carry):
        @pl.when(pl.program_id(0) == 0)
        def _():
            carry[...] = jnp.zeros_like(carry)

        row = _tile_rows((T, W))

        def step(t, hp):
            r0 = pl.multiple_of(t * T, T)
            a = jnp.exp(la_ref[pl.ds(r0, T), :])
            x = u_ref[pl.ds(r0, T), :]
            for k in (1, 2, 4):
                x = x + a * jnp.where(row >= k, pltpu.roll(x, k, 0), 0.0)
                a = a * jnp.where(row >= k, pltpu.roll(a, k, 0), 1.0)
            h = x + a * hp
            h_ref[pl.ds(r0, T), :] = h
            o_ref[pl.ds(r0, T), :] = h * _gelu(g_ref[pl.ds(r0, T), :])
            return h[T - 1:T, :]

        carry[0:1, :] = lax.fori_loop(0, rt // T, step, carry[0:1, :])

    row = pl.BlockSpec((rt, W), lambda i: (i, 0))
    grow = pl.BlockSpec((rt, W), lambda i: (i, PROJ_LRU_GATE))
    return pl.pallas_call(
        body, name="lru_scan", grid=(S // rt,), in_specs=[row, row, grow], out_specs=[row] * 2,
        out_shape=[jax.ShapeDtypeStruct((S, W), f32)] * 2, scratch_shapes=[pltpu.VMEM((T, W), f32)],
        compiler_params=_cp("arbitrary"),
    )(log_a, u, proj)


def _lru_scan_bwd(dlru, proj, h, log_a):
    S, W = h.shape
    rt = _pick(S, ROW_TILE, SUBLANES)
    T = SUBLANES
    nblk = S // rt

    def body(d_ref, g_ref, h_ref, la_ref, go_ref, dg_ref, carry):
        @pl.when(pl.program_id(0) == 0)
        def _():
            carry[...] = jnp.zeros_like(carry)

        row = _tile_rows((T, W))

        def step(j, c):
            gn, an = c
            t = rt // T - 1 - j
            r0 = pl.multiple_of(t * T, T)
            d = d_ref[pl.ds(r0, T), :]
            gate = g_ref[pl.ds(r0, T), :]
            a = jnp.exp(la_ref[pl.ds(r0, T), :])
            dg_ref[pl.ds(r0, T), :] = d * h_ref[pl.ds(r0, T), :] * _gelu_grad(gate)
            x = d * _gelu(gate)
            b = jnp.where(row < T - 1, pltpu.roll(a, T - 1, 0), an)
            for k in (1, 2, 4):
                x = x + b * jnp.where(row < T - k, pltpu.roll(x, T - k, 0), 0.0)
                b = b * jnp.where(row < T - k, pltpu.roll(b, T - k, 0), 1.0)
            g = x + b * gn
            go_ref[pl.ds(r0, T), :] = g
            return g[0:1, :], a[0:1, :]

        gn, an = lax.fori_loop(0, rt // T, step, (carry[0:1, :], carry[1:2, :]))
        carry[0:1, :] = gn
        carry[1:2, :] = an

    row = pl.BlockSpec((rt, W), lambda i: (nblk - 1 - i, 0))
    grow = pl.BlockSpec((rt, W), lambda i: (nblk - 1 - i, PROJ_LRU_GATE))
    return pl.pallas_call(
        body, name="lru_scan_bwd", grid=(nblk,), in_specs=[row, grow, row, row], out_specs=[row] * 2,
        out_shape=[jax.ShapeDtypeStruct((S, W), f32)] * 2, scratch_shapes=[pltpu.VMEM((T, W), f32)],
        compiler_params=_cp("arbitrary"),
    )(dlru, proj, h, log_a)


def _lru_gate_bwd(g, h, xc, r, i, log_a, wr, wi, lam):
    S, W = g.shape
    rt = _pick(S, ROW_TILE, SUBLANES)

    def body(g_ref, h_ref, hp_ref, xc_ref, r_ref, i_ref, la_ref, wr_ref, wi_ref, lam_ref,
             dxc_ref, dwr_ref, dwi_ref, acc_ref):
        @pl.when(pl.program_id(0) == 0)
        def _():
            dwr_ref[...] = jnp.zeros_like(dwr_ref)
            dwi_ref[...] = jnp.zeros_like(dwi_ref)
            acc_ref[...] = jnp.zeros_like(acc_ref)

        prev = jnp.where(pl.program_id(0) == 0, 0.0, hp_ref[...])
        gg, xc, r, i, log_a, lam = g_ref[...], xc_ref[...], r_ref[...], i_ref[...], la_ref[...], lam_ref[...]
        hm1 = _shift_down(h_ref[...], prev, 1)
        a = jnp.exp(log_a)
        s = jnp.sqrt(-_expm1(2.0 * log_a))
        da = gg * hm1
        di = gg * s * xc
        dxc = gg * s * i
        ds = gg * i * xc
        dlog_a = da * a - ds * (a * a / s)
        sp = _softplus_neg(lam)
        dr = dlog_a * (-LRU_C * sp)
        dsp = jnp.sum(dlog_a * (-LRU_C * r), axis=0, keepdims=True)
        dpr = dr * r * (1.0 - r)
        dpi = di * i * (1.0 - i)
        dprb, dpib, xb = dpr.astype(bf16), dpi.astype(bf16), xc.astype(bf16)
        dxc_ref[...] = dxc + _dot(dprb, wr_ref[...], NT) + _dot(dpib, wi_ref[...], NT)
        dwr_ref[...] += _dot(xb, dprb, TN)
        dwi_ref[...] += _dot(xb, dpib, TN)
        acc_ref[0:1, :] += jnp.sum(dpr, axis=0, keepdims=True)
        acc_ref[1:2, :] += jnp.sum(dpi, axis=0, keepdims=True)
        acc_ref[2:3, :] += dsp * (-_sigmoid(-lam))

    row = pl.BlockSpec((rt, W), lambda i: (i, 0))
    halo = _prev_halo_spec(rt, W, lambda i: 0)
    vec = pl.BlockSpec((1, W), lambda i: (0, 0))
    mat = pl.BlockSpec((W, W), lambda i: (0, 0))
    acc = pl.BlockSpec((SUBLANES, W), lambda i: (0, 0))
    return pl.pallas_call(
        body, name="lru_gate_bwd", grid=(S // rt,),
        in_specs=[row, row, halo, row, row, row, row, mat, mat, vec], out_specs=[row, mat, mat, acc],
        out_shape=[jax.ShapeDtypeStruct((S, W), f32), jax.ShapeDtypeStruct((W, W), f32),
                   jax.ShapeDtypeStruct((W, W), f32), jax.ShapeDtypeStruct((SUBLANES, W), f32)],
        compiler_params=_cp("arbitrary"),
    )(g, h, h, xc, r, i, log_a, wr, wi, lam.reshape(1, W))


def _conv_bwd(dy, x, conv_w, name, col_tile=None, out_dtype=f32, x_col_block=0, send=(), send_dst=0):
    if dy.ndim == 2:
        dy = dy[None]
    H, S, Ch = dy.shape
    C = H * Ch
    K = conv_w.shape[0]
    ct = Ch if col_tile is None else col_tile
    nct = Ch // ct
    rt = _pick(S, ROW_TILE, SUBLANES)
    nrt = S // rt
    snd = _SiblingSend(send, send_dst)
    n = snd.n

    def body(*refs):
        dy_ref, dyn_ref, x_ref, w_ref = refs[:4]
        s_ins = refs[4:4 + n]
        dx_ref, acc_ref = refs[4 + n:6 + n]
        s_outs, s_sems = refs[6 + n:6 + 2 * n], refs[6 + 2 * n:]
        i = pl.program_id(2)
        if n:
            @pl.when((pl.program_id(0) == 0) & (pl.program_id(1) == 0) & (i == 0))
            def _():
                snd.start(s_ins, s_outs, s_sems)

        @pl.when(i == 0)
        def _():
            acc_ref[...] = jnp.zeros_like(acc_ref)

        nxt = jnp.where(i == nrt - 1, 0.0, dyn_ref[...].astype(f32)[0:SUBLANES])
        dy, x = dy_ref[...].astype(f32), x_ref[...]
        ahead = [dy] + [_shift_up(dy, nxt, j) for j in range(1, K)]
        dx = w_ref[K - 1:K, :] * dy
        for k in range(K - 1):
            dx = dx + w_ref[k:k + 1, :] * ahead[K - 1 - k]
        dx_ref[...] = dx.astype(dx_ref.dtype)
        for k in range(K):
            acc_ref[k:k + 1, :] += jnp.sum(ahead[K - 1 - k] * x, axis=0, keepdims=True)
        acc_ref[K:K + 1, :] += jnp.sum(dy, axis=0, keepdims=True)
        if n:
            @pl.when((pl.program_id(0) == H - 1) & (pl.program_id(1) == nct - 1) & (i == nrt - 1))
            def _():
                snd.finish(s_ins, s_outs, s_sems)

    halo = SUBLANES * (4 // dy.dtype.itemsize)
    per, last = rt // halo, S // halo - 1
    dy_row = pl.BlockSpec((None, rt, ct), lambda h, j, i: (h, i, j))
    dy_next = pl.BlockSpec((None, halo, ct), lambda h, j, i: (h, jnp.minimum((i + 1) * per, last), j))
    row = pl.BlockSpec((rt, ct), lambda h, j, i: (i, h * nct + j))
    xrow = pl.BlockSpec((rt, ct), lambda h, j, i: (i, h * nct + j + x_col_block))
    res = pl.pallas_call(
        body, name=name, grid=(H, nct, nrt),
        in_specs=[dy_row, dy_next, xrow, pl.BlockSpec((K, ct), lambda h, j, i: (0, h * nct + j))] + snd.in_specs,
        out_specs=[row, pl.BlockSpec((SUBLANES, ct), lambda h, j, i: (0, h * nct + j))] + snd.out_specs,
        out_shape=[jax.ShapeDtypeStruct((S, C), out_dtype), jax.ShapeDtypeStruct((SUBLANES, C), f32)] + snd.out_shape,
        scratch_shapes=snd.scratch,
        compiler_params=_cp(*(("arbitrary",) * 3 if n else ("parallel", "parallel", "arbitrary"))),
    )(dy, dy, x, conv_w, *send)
    return res[0], res[1], list(res[2:])


def _s5_param_fn(a_re, a_im, ls, bt_re, bt_im):
    step = jnp.exp(ls)
    dt_re, dt_im = step * a_re, step * a_im
    mag = jnp.exp(dt_re)
    ab_re, ab_im = mag * jnp.cos(dt_im), mag * jnp.sin(dt_im)
    z_re, z_im = ab_re - 1.0, ab_im
    den = a_re * a_re + a_im * a_im
    f_re = (z_re * a_re + z_im * a_im) / den
    f_im = (z_im * a_re - z_re * a_im) / den
    bb_re = f_re[:, None, :] * bt_re - f_im[:, None, :] * bt_im
    bb_im = f_re[:, None, :] * bt_im + f_im[:, None, :] * bt_re
    return ab_re, ab_im, bb_re, bb_im


def _s5_params(a_re, a_im, ls, bt_re, bt_im):
    def body(ar, ai, l, br, bi, o_ar, o_ai, o_br, o_bi):
        o_ar[...], o_ai[...], o_br[...], o_bi[...] = _s5_param_fn(ar[...], ai[...], l[...], br[...], bi[...])

    return pl.pallas_call(
        body, name="s5_params",
        out_shape=[jax.ShapeDtypeStruct(a_re.shape, f32)] * 2 + [jax.ShapeDtypeStruct(bt_re.shape, f32)] * 2,
        compiler_params=_cp(),
    )(a_re, a_im, ls, bt_re, bt_im)


def _s5_params_bwd(a_re, a_im, ls, bt_re, bt_im, d_ar, d_ai, d_br, d_bi):
    def body(ar, ai, l, br, bi, c_ar, c_ai, c_br, c_bi, g_ar, g_ai, g_l, g_br, g_bi):
        _, vjp = jax.vjp(_s5_param_fn, ar[...], ai[...], l[...], br[...], bi[...])
        g_ar[...], g_ai[...], g_l[...], g_br[...], g_bi[...] = vjp((c_ar[...], c_ai[...], c_br[...], c_bi[...]))

    return pl.pallas_call(
        body, name="s5_params_bwd",
        out_shape=[jax.ShapeDtypeStruct(a_re.shape, f32)] * 2 + [jax.ShapeDtypeStruct(ls.shape, f32)]
        + [jax.ShapeDtypeStruct(bt_re.shape, f32)] * 2,
        compiler_params=_cp(),
    )(a_re, a_im, ls, bt_re, bt_im, d_ar, d_ai, d_br, d_bi)


S5_CHUNK = 256


def _s5_power_tables(ab_ref, p_ref, w_ref, conj):
    T, L = SUBLANES, S5_LANES
    are = ab_ref[0:1, 0:L]
    aim = ab_ref[0:1, L:2 * L]
    if conj:
        aim = -aim
    pre, pim = are, aim
    for n in range(3):
        p_ref[n:n + 1, 0:L] = pre
        p_ref[n:n + 1, L:2 * L] = pim
        pre, pim = pre * pre - pim * pim, 2.0 * pre * pim
    row = _tile_rows((T, L))
    wre = jnp.zeros((T, L), f32)
    wim = jnp.zeros((T, L), f32)
    pre, pim = are, aim
    for n in range(T):
        tgt = (T - 1 - n) if conj else n
        wre = jnp.where(row == tgt, pre, wre)
        wim = jnp.where(row == tgt, pim, wim)
        pre, pim = pre * are - pim * aim, pre * aim + pim * are
    w_ref[:, 0:L] = wre
    w_ref[:, L:2 * L] = wim


def _s5_scan(bu, ab):
    S, L2 = bu.shape
    L = L2 // 2
    rt = _pick(S, 256, SUBLANES)
    T = SUBLANES
    CH = S5_CHUNK

    def body(bu_ref, ab_ref, x_ref, p_ref, w_ref, carry):
        @pl.when(pl.program_id(0) == 0)
        def _():
            carry[...] = jnp.zeros_like(carry)
            _s5_power_tables(ab_ref, p_ref, w_ref, conj=False)

        row = _tile_rows((T, CH))

        def step(t, _):
            r0 = pl.multiple_of(t * T, T)
            for c in range(L // CH):
                lre, lim = pl.ds(c * CH, CH), pl.ds(L + c * CH, CH)
                xr, xi = bu_ref[pl.ds(r0, T), lre], bu_ref[pl.ds(r0, T), lim]
                for n, k in enumerate((1, 2, 4)):
                    pr, pi = p_ref[n:n + 1, lre], p_ref[n:n + 1, lim]
                    sr = jnp.where(row >= k, pltpu.roll(xr, k, 0), 0.0)
                    si = jnp.where(row >= k, pltpu.roll(xi, k, 0), 0.0)
                    xr, xi = xr + pr * sr - pi * si, xi + pr * si + pi * sr
                cr, ci = carry[T - 1:T, lre], carry[T - 1:T, lim]
                wr, wi = w_ref[:, lre], w_ref[:, lim]
                xr, xi = xr + wr * cr - wi * ci, xi + wr * ci + wi * cr
                carry[:, lre] = xr
                carry[:, lim] = xi
                x_ref[pl.ds(r0, T), lre] = xr
                x_ref[pl.ds(r0, T), lim] = xi
            return 0

        lax.fori_loop(0, rt // T, step, 0)

    row_spec = pl.BlockSpec((rt, L2), lambda i: (i, 0))
    return pl.pallas_call(
        body, name="s5_scan", grid=(S // rt,), in_specs=[row_spec, pl.BlockSpec((1, L2), lambda i: (0, 0))],
        out_specs=row_spec, out_shape=jax.ShapeDtypeStruct((S, L2), f32),
        scratch_shapes=[pltpu.VMEM((T, L2), f32), pltpu.VMEM((T, L2), f32), pltpu.VMEM((T, L2), f32)],
        compiler_params=_cp("arbitrary"),
    )(bu, ab)


def _s5_scan_bwd(dx, x, ab):
    S, L2 = dx.shape
    L = L2 // 2
    rt = _pick(S, 256, SUBLANES)
    T = SUBLANES
    CH = S5_CHUNK
    nblk = S // rt
    per = rt // T

    def body(dx_ref, x_ref, xp_ref, ab_ref, g_ref, da_ref, p_ref, w_ref, carry, acc):
        pid = pl.program_id(0)

        @pl.when(pid == 0)
        def _():
            carry[...] = jnp.zeros_like(carry)
            acc[...] = jnp.zeros_like(acc)
            _s5_power_tables(ab_ref, p_ref, w_ref, conj=True)

        row = _tile_rows((T, CH))
        first_block = pid == nblk - 1

        def step(j, _):
            t = per - 1 - j
            r0 = pl.multiple_of(t * T, T)
            rp = pl.multiple_of(jnp.maximum(t - 1, 0) * T, T)
            for c in range(L // CH):
                lre, lim = pl.ds(c * CH, CH), pl.ds(L + c * CH, CH)
                gr, gi = dx_ref[pl.ds(r0, T), lre], dx_ref[pl.ds(r0, T), lim]
                for n, k in enumerate((1, 2, 4)):
                    pr, pi = p_ref[n:n + 1, lre], p_ref[n:n + 1, lim]
                    sr = jnp.where(row < T - k, pltpu.roll(gr, T - k, 0), 0.0)
                    si = jnp.where(row < T - k, pltpu.roll(gi, T - k, 0), 0.0)
                    gr, gi = gr + pr * sr - pi * si, gi + pr * si + pi * sr
                cr, ci = carry[0:1, lre], carry[0:1, lim]
                wr, wi = w_ref[:, lre], w_ref[:, lim]
                gr, gi = gr + wr * cr - wi * ci, gi + wr * ci + wi * cr
                carry[:, lre] = gr
                carry[:, lim] = gi
                g_ref[pl.ds(r0, T), lre] = gr
                g_ref[pl.ds(r0, T), lim] = gi
                xr, xi = x_ref[pl.ds(r0, T), lre], x_ref[pl.ds(r0, T), lim]
                in_blk_r, in_blk_i = x_ref[pl.ds(rp, T), lre], x_ref[pl.ds(rp, T), lim]
                hal_r = jnp.where(first_block, 0.0, xp_ref[:, lre])
                hal_i = jnp.where(first_block, 0.0, xp_ref[:, lim])
                pvr = jnp.where(t == 0, hal_r, in_blk_r)[T - 1:T, :]
                pvi = jnp.where(t == 0, hal_i, in_blk_i)[T - 1:T, :]
                sxr = jnp.where(row >= 1, pltpu.roll(xr, 1, 0), pvr)
                sxi = jnp.where(row >= 1, pltpu.roll(xi, 1, 0), pvi)
                acc[:, lre] += gr * sxr + gi * sxi
                acc[:, lim] += gi * sxr - gr * sxi
            return 0

        lax.fori_loop(0, per, step, 0)

        @pl.when(pid == nblk - 1)
        def _():
            da_ref[...] = jnp.sum(acc[...], axis=0, keepdims=True)

    row_spec = pl.BlockSpec((rt, L2), lambda i: (nblk - 1 - i, 0))
    halo = pl.BlockSpec((T, L2), lambda i: (jnp.maximum((nblk - 1 - i) * per - 1, 0), 0))
    vec = pl.BlockSpec((1, L2), lambda i: (0, 0))
    return pl.pallas_call(
        body, name="s5_scan_bwd", grid=(nblk,), in_specs=[row_spec, row_spec, halo, vec],
        out_specs=[row_spec, vec],
        out_shape=[jax.ShapeDtypeStruct((S, L2), f32), jax.ShapeDtypeStruct((1, L2), f32)],
        scratch_shapes=[pltpu.VMEM((T, L2), f32)] * 4,
        compiler_params=_cp("arbitrary"),
    )(dx, x, x, ab)


def _S5_U_SPEC(rt):
    return pl.BlockSpec((rt, LRU_WIDTH), lambda i: (i, PROJ_S5_U))


def _s5_out(yc, proj, d, wglu, bglu):
    S, W = yc.shape
    rt = _pick(S, ROW_TILE, SUBLANES)

    def body(yc_ref, u_ref, d_ref, w_ref, b_ref, o_ref, y_ref):
        y = yc_ref[...] + d_ref[...] * u_ref[:, 0:W]
        yg = _gelu(y)
        z = _dot(yg.astype(bf16), w_ref[...], NN) + b_ref[...]
        o_ref[...] = yg * _sigmoid(z)
        y_ref[...] = y

    row = pl.BlockSpec((rt, W), lambda i: (i, 0))
    vec = pl.BlockSpec((1, W), lambda i: (0, 0))
    mat = pl.BlockSpec((W, W), lambda i: (0, 0))
    return pl.pallas_call(
        body, name="s5_out", grid=(S // rt,), in_specs=[row, _S5_U_SPEC(rt), vec, mat, vec], out_specs=[row, row],
        out_shape=[jax.ShapeDtypeStruct((S, W), f32)] * 2, compiler_params=_cp("parallel"),
    )(yc, proj, d.reshape(1, W), wglu, bglu.reshape(1, W))


def _s5_out_bwd(dssm, y, proj, d, wglu, bglu):
    S, W = y.shape
    rt = _pick(S, ROW_TILE, SUBLANES)

    def body(do_ref, y_ref, u_ref, d_ref, w_ref, b_ref, dy_ref, du_ref, dw_ref, acc_ref):
        @pl.when(pl.program_id(0) == 0)
        def _():
            dw_ref[...] = jnp.zeros_like(dw_ref)
            acc_ref[...] = jnp.zeros_like(acc_ref)

        do, y = do_ref[...], y_ref[...]
        yg = _gelu(y)
        ygb = yg.astype(bf16)
        sg = _sigmoid(_dot(ygb, w_ref[...], NN) + b_ref[...])
        dz = do * yg * sg * (1.0 - sg)
        dzb = dz.astype(bf16)
        dyg = do * sg + _dot(dzb, w_ref[...], NT)
        dy = dyg * _gelu_grad(y)
        dy_ref[...] = dy
        du_ref[...] = dy * d_ref[...]
        dw_ref[...] += _dot(ygb, dzb, TN)
        acc_ref[0:1, :] += jnp.sum(dz, axis=0, keepdims=True)
        acc_ref[1:2, :] += jnp.sum(dy * u_ref[:, 0:W], axis=0, keepdims=True)

    row = pl.BlockSpec((rt, W), lambda i: (i, 0))
    vec = pl.BlockSpec((1, W), lambda i: (0, 0))
    mat = pl.BlockSpec((W, W), lambda i: (0, 0))
    acc = pl.BlockSpec((SUBLANES, W), lambda i: (0, 0))
    return pl.pallas_call(
        body, name="s5_out_bwd", grid=(S // rt,), in_specs=[row, row, _S5_U_SPEC(rt), vec, mat, vec],
        out_specs=[row, row, mat, acc],
        out_shape=[jax.ShapeDtypeStruct((S, W), f32)] * 2
        + [jax.ShapeDtypeStruct((W, W), f32), jax.ShapeDtypeStruct((SUBLANES, W), f32)],
        compiler_params=_cp("arbitrary"),
    )(dssm, y, proj, d.reshape(1, W), wglu, bglu.reshape(1, W))


MIX_SPLITS = ((0, ATTN_WIDTH), (ATTN_WIDTH, ATTN_WIDTH + LRU_WIDTH), (ATTN_WIDTH + LRU_WIDTH, D_MODEL))


def _mixnorm(attn, lru, ssm, g):
    S = attn.shape[0]
    rt = _pick(S, ROW_TILE, SUBLANES)

    def body(a_ref, l_ref, s_ref, g_ref, o_ref):
        for ref, (lo, hi) in zip((a_ref, l_ref, s_ref), MIX_SPLITS):
            x = ref[...]
            ms = jnp.mean(x * x, axis=-1, keepdims=True)
            o_ref[:, lo:hi] = (x * lax.rsqrt(ms + RMS_EPS) * g_ref[:, lo:hi]).astype(o_ref.dtype)

    rows = [pl.BlockSpec((rt, hi - lo), lambda i: (i, 0)) for lo, hi in MIX_SPLITS]
    return pl.pallas_call(
        body, name="mixnorm", grid=(S // rt,), in_specs=rows + [pl.BlockSpec((1, D_MODEL), lambda i: (0, 0))],
        out_specs=pl.BlockSpec((rt, D_MODEL), lambda i: (i, 0)),
        out_shape=jax.ShapeDtypeStruct((S, D_MODEL), bf16), compiler_params=_cp("parallel"),
    )(attn, lru, ssm, g.reshape(1, D_MODEL))


def _mixnorm_bwd(dmixed, attn, lru, ssm, g):
    S = attn.shape[0]
    rt = _pick(S, ROW_TILE, SUBLANES)

    def body(d_ref, a_ref, l_ref, s_ref, g_ref, da_ref, dl_ref, ds_ref, dlt_ref, acc_ref):
        @pl.when(pl.program_id(0) == 0)
        def _():
            acc_ref[...] = jnp.zeros_like(acc_ref)

        outs = []
        for ref, (lo, hi) in zip((a_ref, l_ref, s_ref), MIX_SPLITS):
            x = ref[...]
            dy = d_ref[:, lo:hi]
            rinv = lax.rsqrt(jnp.mean(x * x, axis=-1, keepdims=True) + RMS_EPS)
            dyg = dy * g_ref[:, lo:hi]
            outs.append(rinv * dyg - x * (rinv * rinv * rinv) * jnp.mean(dyg * x, axis=-1, keepdims=True))
            acc_ref[0:1, lo:hi] += jnp.sum(dy * x * rinv, axis=0, keepdims=True)
        da_ref[...], dl_ref[...], ds_ref[...] = outs
        hi_ = lax.broadcasted_iota(jnp.int32, (ATTN_WIDTH, ATTN_WIDTH), 0) // HEAD_DIM
        hj_ = lax.broadcasted_iota(jnp.int32, (ATTN_WIDTH, ATTN_WIDTH), 1) // HEAD_DIM
        same = jnp.where(hi_ == hj_, 1.0, 0.0).astype(f32)
        dlt_ref[...] = jnp.dot(outs[0] * a_ref[...], same, precision=lax.Precision.HIGHEST, preferred_element_type=f32)

    rows = [pl.BlockSpec((rt, hi - lo), lambda i: (i, 0)) for lo, hi in MIX_SPLITS]
    full = pl.BlockSpec((rt, D_MODEL), lambda i: (i, 0))
    return pl.pallas_call(
        body, name="mixnorm_bwd", grid=(S // rt,),
        in_specs=[full] + rows + [pl.BlockSpec((1, D_MODEL), lambda i: (0, 0))],
        out_specs=rows + [rows[0], pl.BlockSpec((SUBLANES, D_MODEL), lambda i: (0, 0))],
        out_shape=[jax.ShapeDtypeStruct((S, hi - lo), f32) for lo, hi in MIX_SPLITS]
        + [jax.ShapeDtypeStruct((S, ATTN_WIDTH), f32), jax.ShapeDtypeStruct((SUBLANES, D_MODEL), f32)],
        compiler_params=_cp("arbitrary"),
    )(dmixed, attn, lru, ssm, g.reshape(1, D_MODEL))


FFN_COL_TILE = 1536


def _ffn_conv(x, prev, w_ref, b_ref, K):
    y = b_ref[...] + w_ref[K - 1:K, :] * x
    for k in range(K - 1):
        y = y + w_ref[k:k + 1, :] * _shift_down(x, prev, K - 1 - k)
    return y


def _ffn_act(up, conv_w, conv_b):
    S, C2 = up.shape
    C = C2 // 2
    K = conv_w.shape[0]
    ct = FFN_COL_TILE
    nct = C // ct
    rt = _pick(S, ROW_TILE, SUBLANES)

    def body(g_ref, gp_ref, v_ref, vp_ref, wg_ref, wv_ref, bg_ref, bv_ref, o_ref):
        first = pl.program_id(1) == 0
        gate = _ffn_conv(g_ref[...], jnp.where(first, 0.0, gp_ref[...]), wg_ref, bg_ref, K)
        val = _ffn_conv(v_ref[...], jnp.where(first, 0.0, vp_ref[...]), wv_ref, bv_ref, K)
        o_ref[...] = (_gelu(gate) * val).astype(o_ref.dtype)

    def specs(off):
        return (pl.BlockSpec((rt, ct), lambda j, i: (i, j + off)), _prev_halo_spec(rt, ct, lambda j, i: j + off))

    def wspec(off, rows):
        return pl.BlockSpec((rows, ct), lambda j, i: (0, j + off))

    g_s, gp_s = specs(0)
    v_s, vp_s = specs(nct)
    return pl.pallas_call(
        body, name="ffn_act", grid=(nct, S // rt),
        in_specs=[g_s, gp_s, v_s, vp_s, wspec(0, K), wspec(nct, K), wspec(0, 1), wspec(nct, 1)],
        out_specs=pl.BlockSpec((rt, ct), lambda j, i: (i, j)),
        out_shape=jax.ShapeDtypeStruct((S, C), bf16), compiler_params=_cp("parallel", "parallel"),
    )(up, up, up, up, conv_w, conv_w, conv_b.reshape(1, C2), conv_b.reshape(1, C2))


def _ffn_act_bwd(dact, up, conv_w, conv_b):
    S, C2 = up.shape
    C = C2 // 2
    K = conv_w.shape[0]
    ct = FFN_COL_TILE
    nct = C // ct
    rt = _pick(S, ROW_TILE, SUBLANES)

    def body(d_ref, g_ref, gp_ref, v_ref, vp_ref, wg_ref, wv_ref, bg_ref, bv_ref, o_ref):
        first = pl.program_id(1) == 0
        gate = _ffn_conv(g_ref[...], jnp.where(first, 0.0, gp_ref[...]), wg_ref, bg_ref, K)
        val = _ffn_conv(v_ref[...], jnp.where(first, 0.0, vp_ref[...]), wv_ref, bv_ref, K)
        d = d_ref[...].astype(f32)
        gl, dgl = _gelu_pair(gate)
        o_ref[0] = (d * val * dgl).astype(o_ref.dtype)
        o_ref[1] = (d * gl).astype(o_ref.dtype)

    def specs(off):
        return (pl.BlockSpec((rt, ct), lambda j, i: (i, j + off)), _prev_halo_spec(rt, ct, lambda j, i: j + off))

    def wspec(off, rows):
        return pl.BlockSpec((rows, ct), lambda j, i: (0, j + off))

    g_s, gp_s = specs(0)
    v_s, vp_s = specs(nct)
    return pl.pallas_call(
        body, name="ffn_act_bwd", grid=(nct, S // rt),
        in_specs=[pl.BlockSpec((rt, ct), lambda j, i: (i, j)), g_s, gp_s, v_s, vp_s,
                  wspec(0, K), wspec(nct, K), wspec(0, 1), wspec(nct, 1)],
        out_specs=pl.BlockSpec((2, rt, ct), lambda j, i: (0, i, j)),
        out_shape=jax.ShapeDtypeStruct((2, S, C), bf16), compiler_params=_cp("parallel", "parallel"),
    )(dact, up, up, up, up, conv_w, conv_w, conv_b.reshape(1, C2), conv_b.reshape(1, C2))


ANY = pl.BlockSpec(memory_space=pl.ANY)


def _rows_for(cols):
    return max(16, (1 << 19) // cols)


def _chips(x, y):
    return [(1 - x, y), (x, 1 - y), (1 - x, 1 - y)]


class _Gather:
    def __init__(self, shards, split):
        self.shapes = [s.shape for s in shards]
        self.dtypes = [s.dtype for s in shards]
        self.split = list(split)
        self.nt = len(shards)
        self.in_specs = [ANY] * self.nt
        self.out_specs = [ANY] * self.nt
        self.out_shape = [jax.ShapeDtypeStruct((4,) + s, dt) for s, dt in zip(self.shapes, self.dtypes)]
        self.scratch = [pltpu.SemaphoreType.DMA((3, self.nt))] * 4 if self.nt else []

    def _part(self, ref, t, half):
        if not self.split[t]:
            return ref
        r = self.shapes[t][0] // 2
        return ref.at[pl.ds(half * r, r), :]

    def _ici(self, ins, outs, sems, k, t, chip, landing_chip):
        x, y, c = lax.axis_index("x"), lax.axis_index("y"), lax.axis_index("c")
        return pltpu.make_async_remote_copy(
            src_ref=self._part(ins[t], t, c), dst_ref=self._part(outs[t].at[landing_chip], t, c),
            send_sem=sems[0].at[k, t], recv_sem=sems[1].at[k, t], device_id=(chip[0], chip[1], c), device_id_type=MESH)

    def _d2d(self, outs, sems, k, t, q, half):
        x, y, c = lax.axis_index("x"), lax.axis_index("y"), lax.axis_index("c")
        rows = self._part(outs[t].at[q], t, half)
        return pltpu.make_async_remote_copy(
            src_ref=rows, dst_ref=rows, send_sem=sems[2].at[k, t], recv_sem=sems[3].at[k, t],
            device_id=(x, y, 1 - c), device_id_type=MESH)

    def start(self, ins, outs, sems):
        x, y = lax.axis_index("x"), lax.axis_index("y")
        me = 2 * x + y
        for k, chip in enumerate(_chips(x, y)):
            for t in range(self.nt):
                self._ici(ins, outs, sems, k, t, chip, me).start()

    def finish(self, ins, outs, sems):
        x, y, c = lax.axis_index("x"), lax.axis_index("y"), lax.axis_index("c")
        me = 2 * x + y
        chips = _chips(x, y)
        for k, chip in enumerate(chips):
            q = 2 * chip[0] + chip[1]
            for t in range(self.nt):
                self._ici(ins, outs, sems, k, t, chip, q).wait_recv()
                if self.split[t]:
                    self._d2d(outs, sems, k, t, q, c).start()
        for k, chip in enumerate(chips):
            q = 2 * chip[0] + chip[1]
            for t in range(self.nt):
                if self.split[t]:
                    self._d2d(outs, sems, k, t, q, 1 - c).wait_recv()
        for k, chip in enumerate(chips):
            q = 2 * chip[0] + chip[1]
            for t in range(self.nt):
                self._ici(ins, outs, sems, k, t, chip, me).wait_send()
                if self.split[t]:
                    self._d2d(outs, sems, k, t, q, c).wait_send()


def _gather_weights(shards, split):
    g = _Gather(shards, split)
    nt = g.nt

    def body(*refs):
        ins, outs, sems = refs[:nt], refs[nt:2 * nt], refs[2 * nt:]
        g.start(ins, outs, sems)
        g.finish(ins, outs, sems)

    return pl.pallas_call(
        body, name="gather_weights", in_specs=g.in_specs, out_specs=g.out_specs, out_shape=g.out_shape,
        scratch_shapes=g.scratch,
    )(*shards)


class _SiblingSend:
    def __init__(self, gs, dst_c, swap=()):
        self.nt, self.n = len(gs), len(gs) + len(swap)
        self.dst_c = list(dst_c) if isinstance(dst_c, (list, tuple)) else [dst_c] * self.nt
        self.in_specs = [ANY] * self.n
        self.out_specs = [ANY] * self.n
        self.out_shape = [jax.ShapeDtypeStruct(g.shape, g.dtype) for g in list(gs) + list(swap)]
        self.scratch = [pltpu.SemaphoreType.DMA((self.n,))] * 2 if self.n else []

    def _each(self, ins, outs, sems, sender, fn):
        x, y, c = lax.axis_index("x"), lax.axis_index("y"), lax.axis_index("c")

        def cp(t):
            return pltpu.make_async_remote_copy(
                src_ref=ins[t], dst_ref=outs[t], send_sem=sems[0].at[t], recv_sem=sems[1].at[t],
                device_id=(x, y, 1 - c), device_id_type=MESH)

        for dst in (0, 1):
            which = [t for t in range(self.nt) if self.dst_c[t] == dst]
            if which:
                @pl.when((c != dst) if sender else (c == dst))
                def _(which=which):
                    for t in which:
                        fn(cp(t))
        for t in range(self.nt, self.n):
            fn(cp(t))

    def start(self, ins, outs, sems):
        self._each(ins, outs, sems, True, lambda cp: cp.start())

    def finish(self, ins, outs, sems):
        self._each(ins, outs, sems, False, lambda cp: cp.wait_recv())
        self._each(ins, outs, sems, True, lambda cp: cp.wait_send())


def _sibling_send(gs, dst_c, swap=()):
    snd = _SiblingSend(gs, dst_c, swap)
    n = snd.n

    def body(*refs):
        ins, outs, sems = refs[:n], refs[n:2 * n], refs[2 * n:]
        snd.start(ins, outs, sems)
        snd.finish(ins, outs, sems)

    res = pl.pallas_call(
        body, name="sibling_send", in_specs=snd.in_specs, out_specs=snd.out_specs, out_shape=snd.out_shape,
        scratch_shapes=snd.scratch,
    )(*gs, *swap)
    return list(res[:snd.nt]), list(res[snd.nt:])


def _owner_flag(owner_c):
    return (lax.axis_index("c") == owner_c).astype(jnp.int32).reshape(1)


def _pair_sum(g, other, name, owner_c, col_slabs=False):
    R, C = g.shape
    cb = C // 4 if col_slabs else C
    rt = _pick(R, _rows_for(cb), 16)

    def body(on_ref, a_ref, o_ref, out_ref):
        out_ref[...] = (a_ref[...] + o_ref[...]).astype(out_ref.dtype)

    row = pl.BlockSpec((rt, cb), lambda q, i, on: (i * on[0], q * on[0]))
    if col_slabs:
        out_spec = pl.BlockSpec((None, rt, cb), lambda q, i, on: (q * on[0], i * on[0], 0))
        out_shape = jax.ShapeDtypeStruct((4, R, cb), bf16)
    else:
        out_spec, out_shape = row, jax.ShapeDtypeStruct((R, C), bf16)
    return pl.pallas_call(
        body, name=name,
        grid_spec=pltpu.PrefetchScalarGridSpec(num_scalar_prefetch=1, grid=(C // cb, R // rt),
                                               in_specs=[row, row], out_specs=out_spec),
        out_shape=out_shape, compiler_params=_cp("arbitrary", "arbitrary"),
    )(_owner_flag(owner_c), g, other)


class _ChipExchange:
    NORTH = 1

    def __init__(self, slabs, whole, only_c):
        self.ns, self.nw = len(slabs), len(whole)
        self.only_c = list(only_c) if isinstance(only_c, (list, tuple)) else [only_c] * self.ns
        self.only_c += [self.NORTH] * self.nw
        self.n = self.ns + self.nw
        self.in_specs = [ANY] * self.n
        self.out_specs = [ANY] * self.n
        self.out_shape = ([jax.ShapeDtypeStruct(s.shape, s.dtype) for s in slabs]
                          + [jax.ShapeDtypeStruct((4,) + w.shape, w.dtype) for w in whole])
        self.scratch = [pltpu.SemaphoreType.DMA((3, self.n))] * 2 if self.n else []
        self.scratch += [pltpu.SemaphoreType.DMA((3, self.nw))] * 2 if self.nw else []

    def _copy(self, ins, outs, sems, k, t, chip, landing_chip):
        c = lax.axis_index("c")
        src = ins[t].at[2 * chip[0] + chip[1]] if t < self.ns else ins[t]
        return pltpu.make_async_remote_copy(
            src_ref=src, dst_ref=outs[t].at[landing_chip], send_sem=sems[0].at[k, t], recv_sem=sems[1].at[k, t],
            device_id=(chip[0], chip[1], c), device_id_type=MESH)

    def _pass_on(self, outs, sems, k, t, chip):
        x, y, c = lax.axis_index("x"), lax.axis_index("y"), lax.axis_index("c")
        rows = outs[t].at[2 * chip[0] + chip[1]]
        return pltpu.make_async_remote_copy(
            src_ref=rows, dst_ref=rows, send_sem=sems[2].at[k, t - self.ns], recv_sem=sems[3].at[k, t - self.ns],
            device_id=(x, y, 1 - c), device_id_type=MESH)

    def _each(self, fn, north=True):
        x, y, c = lax.axis_index("x"), lax.axis_index("y"), lax.axis_index("c")
        chips = _chips(x, y)
        groups = {}
        if north:
            for t in range(self.n):
                groups.setdefault(self.only_c[t], []).append(t)
        elif self.nw:
            groups[1 - self.NORTH] = list(range(self.ns, self.n))
        for owner, which in groups.items():
            @pl.when(c == owner)
            def _(which=which):
                for k, chip in enumerate(chips):
                    for t in which:
                        fn(k, t, chip)

    def start(self, ins, outs, sems):
        me = 2 * lax.axis_index("x") + lax.axis_index("y")
        self._each(lambda k, t, chip: self._copy(ins, outs, sems, k, t, chip, me).start())

    def finish(self, ins, outs, sems):
        me = 2 * lax.axis_index("x") + lax.axis_index("y")

        def landed(k, t, chip):
            self._copy(ins, outs, sems, k, t, chip, 2 * chip[0] + chip[1]).wait_recv()
            if t >= self.ns:
                self._pass_on(outs, sems, k, t, chip).start()

        def sent(k, t, chip):
            self._copy(ins, outs, sems, k, t, chip, me).wait_send()
            if t >= self.ns:
                self._pass_on(outs, sems, k, t, chip).wait_send()

        self._each(landed)
        self._each(lambda k, t, chip: self._pass_on(outs, sems, k, t, chip).wait_recv(), north=False)
        self._each(sent)


def _chip_exchange(slabs, whole, only_c):
    ex = _ChipExchange(slabs, whole, only_c)
    n = ex.n

    def body(*refs):
        ins, outs, sems = refs[:n], refs[n:2 * n], refs[2 * n:]
        ex.start(ins, outs, sems)
        ex.finish(ins, outs, sems)

    res = pl.pallas_call(
        body, name="chip_exchange", in_specs=ex.in_specs, out_specs=ex.out_specs, out_shape=ex.out_shape,
        scratch_shapes=ex.scratch,
    )(*slabs, *whole)
    return list(res[:ex.ns]), list(res[ex.ns:])


def _sum_chips(recv, own, name, owner_c=None):
    n, r, C = recv.shape
    rt = _pick(r, _rows_for(C), 16)
    own3 = own.ndim == 3

    def body(on_ref, r_ref, o_ref, out_ref):
        me = 2 * lax.axis_index("x") + lax.axis_index("y")
        acc = None
        for q in range(n):
            term = jnp.where(me == q, o_ref[q] if own3 else o_ref[...], r_ref[q]).astype(f32)
            acc = term if acc is None else acc + term
        out_ref[...] = acc

    blk = pl.BlockSpec((n, rt, C), lambda i, on: (0, i * on[0], 0))
    row = pl.BlockSpec((rt, C), lambda i, on: (i * on[0], 0))
    flag = jnp.ones((1,), jnp.int32) if owner_c is None else _owner_flag(owner_c)
    return pl.pallas_call(
        body, name=name,
        grid_spec=pltpu.PrefetchScalarGridSpec(num_scalar_prefetch=1, grid=(r // rt,),
                                               in_specs=[blk, blk if own3 else row], out_specs=row),
        out_shape=jax.ShapeDtypeStruct((r, C), f32), compiler_params=_cp("arbitrary"),
    )(flag, recv, own)


def _adamw_layers(mine, theirs, owners, w, m, v, name):
    L, r, C = w.shape
    rt = _pick(r, _rows_for(C), 16)

    def body(own_ref, a0_ref, a1_ref, b0_ref, b1_ref, w_ref, m_ref, v_ref, g_ref, d_ref, mo_ref, vo_ref):
        layer = pl.program_id(0)
        g0 = jnp.where(own_ref[0] == 1, a0_ref[...], b0_ref[...])
        g1 = jnp.where(own_ref[1] == 1, a1_ref[...], b1_ref[...])
        g_ref[...] = jnp.where(layer == 0, g0, g1)
        _adamw_math(g_ref, w_ref, m_ref, v_ref, d_ref, mo_ref, vo_ref)

    def flat(layer, mine_side):
        def index(l, i, own):
            use = (l == layer) & (own[layer] == (1 if mine_side else 0))
            return (jnp.where(use, i, 0), 0)
        return pl.BlockSpec((rt, C), index)

    lay = pl.BlockSpec((None, rt, C), lambda l, i, own: (l, i, 0))
    c = lax.axis_index("c")
    own = jnp.stack([(c == owners[0]).astype(jnp.int32), (c == owners[1]).astype(jnp.int32)])
    return pl.pallas_call(
        body, name=name,
        grid_spec=pltpu.PrefetchScalarGridSpec(
            num_scalar_prefetch=1, grid=(L, r // rt),
            in_specs=[flat(0, True), flat(1, True), flat(0, False), flat(1, False)] + [lay] * 3, out_specs=[lay] * 4),
        out_shape=[jax.ShapeDtypeStruct((L, r, C), f32)] * 4, compiler_params=_cp("arbitrary", "arbitrary"),
    )(own, mine[0], mine[1], theirs[0], theirs[1], w, m, v)


def _adamw_math(g_ref, w_ref, m_ref, v_ref, d_ref, mo_ref, vo_ref):
    gg = g_ref[...]
    m_new = ADAM_B1 * m_ref[...] + (1.0 - ADAM_B1) * gg
    v_new = ADAM_B2 * v_ref[...] + (1.0 - ADAM_B2) * (gg * gg)
    m_hat = m_new / (1.0 - ADAM_B1 ** ADAM_STEP)
    v_hat = v_new / (1.0 - ADAM_B2 ** ADAM_STEP)
    d_ref[...] = -ADAM_LR * (m_hat / (jnp.sqrt(v_hat) + ADAM_EPS) + ADAM_WD * w_ref[...])
    mo_ref[...] = m_new
    vo_ref[...] = v_new


FLAT_TILE = 2048


def _add2(a, b, name):
    R = a.shape[0]
    rt = _pick(R, FLAT_TILE, SUBLANES)

    def body(a_ref, b_ref, o_ref):
        o_ref[...] = a_ref[...] + b_ref[...]

    row = pl.BlockSpec((rt, LANES), lambda i: (i, 0))
    return pl.pallas_call(
        body, name=name, grid=(R // rt,), in_specs=[row, row], out_specs=row,
        out_shape=jax.ShapeDtypeStruct((R, LANES), f32), compiler_params=_cp("parallel"),
    )(a, b)


def _adamw(g, w, m, v, name):
    R = g.shape[0]
    rt = _pick(R, FLAT_TILE, SUBLANES)

    def body(g_ref, w_ref, m_ref, v_ref, d_ref, mo_ref, vo_ref):
        _adamw_math(g_ref, w_ref, m_ref, v_ref, d_ref, mo_ref, vo_ref)

    row = pl.BlockSpec((rt, LANES), lambda i: (i, 0))
    return pl.pallas_call(
        body, name=name, grid=(R // rt,), in_specs=[row] * 4, out_specs=[row] * 3,
        out_shape=[jax.ShapeDtypeStruct((R, LANES), f32)] * 3, compiler_params=_cp("parallel"),
    )(g, w, m, v)


def _adamw_whole(g, w, m, v, name):
    def body(g_ref, w_ref, m_ref, v_ref, d_ref, mo_ref, vo_ref):
        _adamw_math(g_ref, w_ref, m_ref, v_ref, d_ref, mo_ref, vo_ref)

    return pl.pallas_call(
        body, name=name, out_shape=[jax.ShapeDtypeStruct(w.shape, f32)] * 3, compiler_params=_cp(),
    )(g, w, m, v)


def _pack(arrs, dtype, rows=None):
    flat = jnp.concatenate([a.astype(dtype).reshape(-1) for a in arrs])
    per = FLAT_TILE * LANES
    total = rows * LANES if rows else flat.shape[0] + (-flat.shape[0]) % per
    flat = jnp.pad(flat, (0, total - flat.shape[0]))
    return flat.reshape(-1, LANES)


def _unpack(buf, shapes):
    flat = buf.reshape(-1)
    out, off = [], 0
    for s in shapes:
        n = math.prod(s)
        out.append(flat[off:off + n].reshape(s))
        off += n
    return out


def _block_diag(w):
    n, a, b = w.shape
    eye = jnp.eye(n, dtype=w.dtype)
    return (w[:, :, None, :] * eye[:, None, :, None]).reshape(n * a, n * b)


def _diag_blocks(m, n):
    a, b = m.shape[0] // n, m.shape[1] // n
    idx = jnp.arange(n)
    return m.reshape(n, a, n, b)[idx, :, idx, :]


BIG = ("w_in", "w_out", "w_up", "w_down", "s5_w_glu")
BIG_COL_SHARDED = {"w_in": True, "w_out": False, "w_up": True, "w_down": False, "s5_w_glu": False}
CONV_SHARDED = ("lru_conv_w", "ffn_conv_w")
SMALL = ("lru_conv_b", "lru_wr", "lru_br", "lru_wi", "lru_bi", "lru_lambda", "s5_a_re", "s5_a_im", "s5_b_re",
         "s5_b_im", "s5_c_re", "s5_c_im", "s5_d", "s5_log_step", "s5_b_glu", "mix_norm_g", "ln1_g", "ln1_b",
         "ffn_conv_b", "ln2_g", "ln2_b")
WEIGHTS = ("w_in", "lru_conv_w", "lru_conv_b", "lru_wr", "lru_br", "lru_wi", "lru_bi", "lru_lambda", "s5_a_re",
           "s5_a_im", "s5_b_re", "s5_b_im", "s5_c_re", "s5_c_im", "s5_d", "s5_log_step", "s5_w_glu", "s5_b_glu",
           "mix_norm_g", "w_out", "ln1_g", "ln1_b", "w_up", "ffn_conv_w", "ffn_conv_b", "w_down", "ln2_g", "ln2_b")


def _assemble(slabs, col_sharded):
    _, L, r, c = slabs.shape
    if col_sharded:
        return slabs.transpose(1, 2, 0, 3).reshape(L, r, 4 * c)
    return slabs.transpose(1, 0, 2, 3).reshape(L, 4 * r, c)


def _s5_prepare(p):
    G = N_S5_GROUPS
    bt_re, bt_im = p["s5_b_re"].transpose(0, 2, 1), p["s5_b_im"].transpose(0, 2, 1)
    ls = p["s5_log_step"].reshape(G, 1)
    ab_re, ab_im, bb_re, bb_im = _s5_params(p["s5_a_re"], p["s5_a_im"], ls, bt_re, bt_im)
    ab = jnp.concatenate([ab_re.reshape(1, S5_LANES), ab_im.reshape(1, S5_LANES)], axis=1)
    bbcat = jnp.concatenate([_block_diag(bb_re), _block_diag(bb_im)], axis=1).astype(bf16)
    ccat = jnp.concatenate([_block_diag(p["s5_c_re"].transpose(0, 2, 1)),
                            -_block_diag(p["s5_c_im"].transpose(0, 2, 1))], axis=0).astype(bf16)
    bbcat_pad = jnp.concatenate([bbcat, jnp.zeros((LRU_WIDTH - S5_WIDTH, 2 * S5_LANES), bf16)], axis=0)
    return dict(bt_re=bt_re, bt_im=bt_im, ls=ls, ab=ab, bbcat=bbcat, bbcat_pad=bbcat_pad, ccat=ccat)


def _layer_fwd(h, p, cos, sin, pending, install):
    sv = {"h": h}
    proj = _mm(h, p["w_in"], "nn", "mm_proj", tn=D_IN_PAD)
    sv.update(proj=proj)
    qr, kr = _rope_fwd(proj, cos, sin)
    attn, ltot, gathered = _attn_fwd2(qr, kr, proj, [s for _, _, s in pending], [True] * len(pending))
    install(pending, gathered)
    sv.update(qr=qr, kr=kr, attn=attn, ltot=ltot)
    wr, wi = _block_diag(p["lru_wr"]).astype(bf16), _block_diag(p["lru_wi"]).astype(bf16)
    xc, r, i, log_a, u = _lru_pre(proj, p["lru_conv_w"], p["lru_conv_b"], wr, p["lru_br"], wi, p["lru_bi"],
                                  p["lru_lambda"])
    hl, lru = _lru_scan(log_a, u, proj)
    sv.update(wr=wr, wi=wi, xc=xc, r=r, i=i, log_a=log_a, hl=hl, lru=lru)
    s5 = _s5_prepare(p)
    bu = _mm(proj, s5["bbcat_pad"], "nn", "mm_s5_bu", a_win=(PROJ_S5_U, LRU_WIDTH))
    xs = _s5_scan(bu, s5["ab"])
    yc = _mm(xs, s5["ccat"], "nn", "mm_s5_y")
    ssm, y = _s5_out(yc, proj, p["s5_d"].reshape(-1), p["s5_w_glu"], p["s5_b_glu"])
    sv.update(s5=s5, xs=xs, y=y, ssm=ssm)
    mixed = _mixnorm(attn, lru, ssm, p["mix_norm_g"])
    h1, z1 = _mm_ln_fwd(mixed, p["w_out"], h, p["ln1_g"], p["ln1_b"], "mm_out_ln")
    sv.update(mixed=mixed, z1=z1, h1=h1)
    up = _mm(h1, p["w_up"], "nn", "mm_up", tn=1536)
    act = _ffn_act(up, p["ffn_conv_w"], p["ffn_conv_b"])
    h2, z2 = _mm_ln_fwd(act, p["w_down"], h1, p["ln2_g"], p["ln2_b"], "mm_down_ln")
    sv.update(up=up, act=act, z2=z2)
    return h2, sv


def _layer_bwd(dy_a, dy_b, p, sv, cos, sin, relay=None):
    gr = {}
    dz2, acc = _ln_bwd(dy_a, dy_b, sv["z2"], p["ln2_g"], "ln_bwd_top" if dy_a is None else "ln_bwd")
    gr["ln2_g"], gr["ln2_b"] = acc[0], acc[1]
    dact = _mm(dz2, p["w_down"], "nt", "mm_dact", out_dtype=bf16)
    gr["w_down"] = _mm(sv["act"], dz2, "tn", "mm_dw_down")
    dupc = _ffn_act_bwd(dact, sv["up"], p["ffn_conv_w"], p["ffn_conv_b"])
    others, others_dst, to_slabs = relay if relay else ((), 0, None)
    send = list(others) + ([gr["w_down"]] if relay else [])
    send_dst = [others_dst] * len(others) + ([EARLY_OWNER] if relay else [])
    dup, acc, from_sibling = _conv_bwd(dupc, sv["up"], p["ffn_conv_w"], "ffn_conv_bwd", col_tile=FFN_COL_TILE,
                                       out_dtype=bf16, send=send, send_dst=send_dst)
    slabs, owners = to_slabs(from_sibling[:-1], gr["w_down"], from_sibling[-1]) if relay else ((), 0)
    gr["ffn_conv_w"], gr["ffn_conv_b"] = acc[0:3], acc[3]
    dh1 = _mm(dup, p["w_up"], "nt", "mm_dh1", tk=2048)
    gr["w_up"] = _mm(sv["h1"], dup, "tn", "mm_dw_up", tn=1536)
    dz1, acc = _ln_bwd(dz2, dh1, sv["z1"], p["ln1_g"], "ln_bwd")
    gr["ln1_g"], gr["ln1_b"] = acc[0], acc[1]
    dmixed = _mm(dz1, p["w_out"], "nt", "mm_dmixed")
    gr["w_out"] = _mm(sv["mixed"], dz1, "tn", "mm_dw_out")
    dattn, dlru, dssm, delta, acc = _mixnorm_bwd(dmixed, sv["attn"], sv["lru"], sv["ssm"], p["mix_norm_g"])
    gr["mix_norm_g"] = acc[0]
    proj = sv["proj"]
    late = [gr["w_up"]] if relay else []
    dqr, dkr, dv, received, late_sibling = _attn_bwd2(sv["qr"], sv["kr"], proj, dattn, sv["ltot"], delta, slabs,
                                                      owners, late, EARLY_OWNER)
    dq, dk = _rope_bwd(dqr, dkr, cos, sin)
    g, dgate = _lru_scan_bwd(dlru, proj, sv["hl"], sv["log_a"])
    dxc, dwr, dwi, acc = _lru_gate_bwd(g, sv["hl"], sv["xc"], sv["r"], sv["i"], sv["log_a"], sv["wr"], sv["wi"],
                                       p["lru_lambda"])
    gr["lru_wr"], gr["lru_wi"] = _diag_blocks(dwr, N_LRU_HEADS), _diag_blocks(dwi, N_LRU_HEADS)
    gr["lru_br"], gr["lru_bi"], gr["lru_lambda"] = acc[0], acc[1], acc[2]
    dxr, acc, _ = _conv_bwd(dxc, proj, p["lru_conv_w"], "lru_conv_bwd", x_col_block=PROJ_LRU_X)
    gr["lru_conv_w"], gr["lru_conv_b"] = acc[0:4], acc[4]
    s5 = sv["s5"]
    G = N_S5_GROUPS
    dy, du_direct, dwglu, acc = _s5_out_bwd(dssm, sv["y"], proj, p["s5_d"].reshape(-1), p["s5_w_glu"],
                                            p["s5_b_glu"])
    gr["s5_w_glu"], gr["s5_b_glu"], gr["s5_d"] = dwglu, acc[0], acc[1].reshape(G, S5_GROUP)
    dxs = _mm(dy, s5["ccat"], "nt", "mm_s5_dx")
    dccat = _mm(sv["xs"], dy, "tn", "mm_s5_dc")
    gr["s5_c_re"] = _diag_blocks(dccat[:S5_LANES], G).transpose(0, 2, 1)
    gr["s5_c_im"] = -_diag_blocks(dccat[S5_LANES:], G).transpose(0, 2, 1)
    gs, dab = _s5_scan_bwd(dxs, sv["xs"], s5["ab"])
    du = _mm(gs, s5["bbcat"], "nt", "mm_s5_du", add=du_direct)
    dbbcat = _mm(proj, gs, "tn", "mm_s5_dbb", a_win=(PROJ_S5_U, LRU_WIDTH))[:S5_WIDTH]
    d_ar, d_ai, d_ls, d_btr, d_bti = _s5_params_bwd(
        p["s5_a_re"], p["s5_a_im"], s5["ls"], s5["bt_re"], s5["bt_im"],
        dab[:, :S5_LANES].reshape(G, S5_STATE), dab[:, S5_LANES:].reshape(G, S5_STATE),
        _diag_blocks(dbbcat[:, :S5_LANES], G), _diag_blocks(dbbcat[:, S5_LANES:], G))
    gr["s5_a_re"], gr["s5_a_im"], gr["s5_log_step"] = d_ar, d_ai, d_ls.reshape(G)
    gr["s5_b_re"], gr["s5_b_im"] = d_btr.transpose(0, 2, 1), d_bti.transpose(0, 2, 1)
    pad = jnp.zeros((du.shape[0], D_IN_PAD - D_IN), f32)
    dproj = jnp.concatenate([dq, dk, dv, dxr, dgate, du, pad], axis=1).astype(bf16)
    gr["w_in"] = _mm(sv["h"], dproj, "tn", "mm_dw_in", tn=768)[:, :D_IN]
    dh = _mm(dproj, p["w_in"], "nt", "mm_dh")
    return (dz1, dh, gr, slabs, received, late_sibling) if relay else (dz1, dh, gr)


def _train_step(d):
    x, target = d["x"][0], d["loss_target"][0]
    S = x.shape[0]
    me = 2 * lax.axis_index("x") + lax.axis_index("y")

    def rows2d(a):
        return a.reshape(a.shape[0] * a.shape[1], a.shape[2])

    params = [{n: d[n][l] for n in SMALL} for l in range(DEPTH)]

    def install(items, gathered):
        for (n, l, mine), g in zip(items, gathered):
            g = lax.dynamic_update_slice_in_dim(g, mine[None], me, axis=0)
            if n in CONV_SHARDED:
                full = _assemble(g.reshape((4,) + d[n].shape), True)
                for k in range(DEPTH):
                    params[k][n] = full[k]
                continue
            full = _assemble(g[:, None], BIG_COL_SHARDED[n])[0]
            if n == "w_in":
                full = jnp.pad(full, ((0, 0), (0, D_IN_PAD - D_IN)))
            params[l][n] = full

    def shard(n, l):
        return (n, l, d[n][l].astype(bf16))

    first = [shard("w_in", 0)] + [(n, None, rows2d(d[n])) for n in CONV_SHARDED]
    install(first, _gather_weights([s for _, _, s in first], [True] + [False] * len(CONV_SHARDED)))
    later = [[shard(n, 0) for n in BIG[1:]] + [shard("w_in", 1)], [shard(n, 1) for n in BIG[1:]]]

    cos, sin = _rope_tables(S)
    h, saved = x, []
    for l in range(DEPTH):
        h, sv = _layer_fwd(h, params[l], cos, sin, later[l], install)
        saved.append(sv)
    dy, loss_acc = _loss_head(h, target)
    def slab(n, g, other, owner):
        aligned = BIG_COL_SHARDED[n] and (g.shape[1] // 4) % LANES == 0
        p = _pair_sum(g, other, "pair_sum_" + n, owner, col_slabs=aligned)
        if BIG_COL_SHARDED[n] and not aligned:
            return p.reshape(p.shape[0], 4, p.shape[1] // 4).transpose(1, 0, 2)
        return p if aligned else p.reshape(4, p.shape[0] // 4, p.shape[1])

    own1 = {n: 1 - EARLY_OWNER for n in BIG}
    own0 = {n: (EARLY_OWNER if n in ("w_down", "w_up") else 1 - EARLY_OWNER) for n in BIG}

    def hidden_slabs(others1, w_down0, w_down0_sibling):
        slabs = [slab(n, grads[1][n], o, own1[n]) for n, o in zip(BIG, others1)]
        slabs.append(slab("w_down", w_down0, w_down0_sibling, own0["w_down"]))
        return slabs, [own1[n] for n in BIG] + [own0["w_down"]]

    da, db, grads = None, dy, [None] * DEPTH
    da, db, grads[1] = _layer_bwd(da, db, params[1], saved[1], cos, sin)
    relay = ([grads[1][n] for n in BIG], 1 - EARLY_OWNER, hidden_slabs)
    da, db, grads[0], hslabs, hrecv, (w_up0_sibling,) = _layer_bwd(da, db, params[0], saved[0], cos, sin, relay)
    out = {"grad_x": _axpy(da, db, "grad_x")[None]}

    small = SMALL + CONV_SHARDED
    sp = _pack([grads[l][n] for n in small for l in range(DEPTH)], f32)
    tail = [n for n in BIG if n != "w_down"]
    rest = [n for n in tail if n != "w_up"]
    others_rest, (sp_sibling,) = _sibling_send([grads[0][n] for n in rest], [own0[n] for n in rest], [sp])
    sibling0 = dict(zip(rest, others_rest), w_up=w_up0_sibling)
    tslabs = [slab(n, grads[0][n], sibling0[n], own0[n]) for n in tail]
    chip_small = _add2(sp, sp_sibling, "pair_sum_small")
    trecv, (recv_small,) = _chip_exchange(tslabs, [chip_small], [own0[n] for n in tail])
    mine0 = {n: _sum_chips(r, s, "sum_chips_" + n, own0[n]) for n, r, s in zip(tail, trecv, tslabs)}
    mine0["w_down"] = _sum_chips(hrecv[-1], hslabs[-1], "sum_chips_w_down", own0["w_down"])
    mine1 = {n: _sum_chips(r, s, "sum_chips_" + n, own1[n]) for n, r, s in zip(BIG, hrecv, hslabs)}
    sent, _ = _sibling_send([mine0[n] for n in BIG] + [mine1[n] for n in BIG],
                            [1 - own0[n] for n in BIG] + [1 - own1[n] for n in BIG])
    theirs0, theirs1 = dict(zip(BIG, sent[:len(BIG)])), dict(zip(BIG, sent[len(BIG):]))
    for n in BIG:
        upd = _adamw_layers((mine0[n], mine1[n]), (theirs0[n], theirs1[n]), (own0[n], own1[n]),
                            d[n], d["m_" + n], d["v_" + n], "adamw_" + n)
        for pre, u in zip(("grad_", "delta_", "new_m_", "new_v_"), upd):
            out[pre + n] = u

    total = _sum_chips(recv_small, chip_small, "sum_chips_small")
    rows = total.shape[0]
    upd = _adamw(total, _pack([d[n] for n in SMALL], f32, rows), _pack([d["m_" + n] for n in SMALL], f32, rows),
                 _pack([d["v_" + n] for n in SMALL], f32, rows), "adamw_small")
    small_shapes = [d[n].shape for n in SMALL]
    for pre, buf in zip(("grad_", "delta_", "new_m_", "new_v_"), (total,) + tuple(upd)):
        for n, a in zip(SMALL, _unpack(buf, small_shapes)):
            out[pre + n] = a
    conv_full = _unpack(total, small_shapes + [(DEPTH,) + grads[0][n].shape for n in CONV_SHARDED])
    for n, g in zip(CONV_SHARDED, conv_full[len(SMALL):]):
        L, K, C = g.shape
        g = lax.dynamic_index_in_dim(g.reshape(L, K, 4, C // 4), me, axis=2, keepdims=False)
        out["grad_" + n] = g
        for pre, u in zip(("delta_", "new_m_", "new_v_"), _adamw_whole(g, d[n], d["m_" + n], d["v_" + n], "adamw_" + n)):
            out[pre + n] = u

    loss_local, _ = lax.optimization_barrier((loss_acc[0, 0], upd[0]))
    out["loss"] = lax.psum(loss_local, ("x", "y", "c"))
    return (out["loss"], out["grad_x"]) + tuple(out[pre + n] for pre in ("grad_", "delta_", "new_m_", "new_v_")
                                                for n in WEIGHTS)


def kernel(
        x, w_in, lru_conv_w, lru_conv_b, lru_wr, lru_br, lru_wi, lru_bi, lru_lambda, s5_a_re, s5_a_im, s5_b_re,
        s5_b_im, s5_c_re, s5_c_im, s5_d, s5_log_step, s5_w_glu, s5_b_glu, mix_norm_g, w_out, ln1_g, ln1_b, w_up,
        ffn_conv_w, ffn_conv_b, w_down, ln2_g, ln2_b, loss_target, m_w_in, m_lru_conv_w, m_lru_conv_b, m_lru_wr,
        m_lru_br, m_lru_wi, m_lru_bi, m_lru_lambda, m_s5_a_re, m_s5_a_im, m_s5_b_re, m_s5_b_im, m_s5_c_re,
        m_s5_c_im, m_s5_d, m_s5_log_step, m_s5_w_glu, m_s5_b_glu, m_mix_norm_g, m_w_out, m_ln1_g, m_ln1_b,
        m_w_up, m_ffn_conv_w, m_ffn_conv_b, m_w_down, m_ln2_g, m_ln2_b, v_w_in, v_lru_conv_w, v_lru_conv_b,
        v_lru_wr, v_lru_br, v_lru_wi, v_lru_bi, v_lru_lambda, v_s5_a_re, v_s5_a_im, v_s5_b_re, v_s5_b_im,
        v_s5_c_re, v_s5_c_im, v_s5_d, v_s5_log_step, v_s5_w_glu, v_s5_b_glu, v_mix_norm_g, v_w_out, v_ln1_g,
        v_ln1_b, v_w_up, v_ffn_conv_w, v_ffn_conv_b, v_w_down, v_ln2_g, v_ln2_b
):
    return _train_step(dict(locals()))
```

```python
import math

import jax
import jax.numpy as jnp
from jax import lax
from jax.experimental import pallas as pl
from jax.experimental.pallas import tpu as pltpu

f32 = jnp.float32
bf16 = jnp.bfloat16
MESH = pl.DeviceIdType.MESH

D_MODEL = 1024
ATTN_WIDTH = 384
LRU_WIDTH = 384
S5_WIDTH = 256
HEAD_DIM = 64
N_LRU_HEADS = 6
N_S5_GROUPS = 16
S5_GROUP = 16
S5_STATE = 64
S5_LANES = N_S5_GROUPS * S5_STATE
D_FF = 3072
D_IN = 2176
LRU_C = 8.0
ROPE_THETA = 10000.0
DILATIONS = (1, 4, 16)
ATTN_BLOCK = 128
DEPTH = 2
ALPHA = (2 * DEPTH) ** 0.25
LN_EPS = 1e-5
RMS_EPS = 1e-6
ADAM_LR, ADAM_B1, ADAM_B2, ADAM_EPS, ADAM_WD, ADAM_STEP = 0.001, 0.9, 0.999, 1e-08, 0.01, 10

SUBLANES = 8
LANES = 128
VMEM_LIMIT = 56 * 1024 * 1024
ROW_TILE = 512
MM_SINGLE_K = 3072
D_IN_PAD = 2304
NEG = -1e30


def _cp(*sem):
    return pltpu.CompilerParams(dimension_semantics=sem if sem else None, vmem_limit_bytes=VMEM_LIMIT)


def _pick(dim, pref, align=LANES):
    if dim <= pref:
        return dim
    t = (pref // align) * align
    while t >= align:
        if dim % t == 0:
            return t
        t -= align
    return dim


def _gelu(x):
    return jax.nn.gelu(x)


def _gelu_grad(x):
    c = math.sqrt(2.0 / math.pi)
    t = jnp.tanh(c * (x + 0.044715 * x * x * x))
    return 0.5 * (1.0 + t) + 0.5 * x * (1.0 - t * t) * c * (1.0 + 3 * 0.044715 * x * x)


def _gelu_pair(x):
    c = math.sqrt(2.0 / math.pi)
    x2 = x * x
    t = jnp.tanh(c * x * (1.0 + 0.044715 * x2))
    return 0.5 * x * (1.0 + t), 0.5 * (1.0 + t) + 0.5 * x * (1.0 - t * t) * c * (1.0 + 3 * 0.044715 * x2)


def _sigmoid(x):
    return jax.nn.sigmoid(x)


def _expm1(x):
    p = 1.0 + x / 9.0
    for n in (8.0, 7.0, 6.0, 5.0, 4.0, 3.0, 2.0):
        p = 1.0 + (x / n) * p
    return jnp.where(jnp.abs(x) < 0.3, x * p, jnp.exp(x) - 1.0)


def _dot(a, b, dims):
    return lax.dot_general(a, b, (dims, ((), ())), preferred_element_type=f32)


NN = ((1,), (0,))
NT = ((1,), (1,))
TN = ((0,), (0,))


def _mm(a, b, mode, name, out_dtype=f32, tm=1024, tn=1024, tk=1024, add=None, a_win=None, add_scale=1.0):
    if mode == "nn":
        (M, K), N = a.shape, b.shape[1]
    elif mode == "nt":
        (M, K), N = a.shape, b.shape[0]
    else:
        (K, M), N = a.shape, b.shape[1]
    win = 0
    if a_win is not None:
        win, w = a_win
        if mode == "tn":
            M, tm = w, w
        else:
            K = w
    single = mode != "tn" and K <= MM_SINGLE_K
    tm, tn = _pick(M, tm), _pick(N, tn)
    tk = K if single else _pick(K, tk)
    nk = K // tk
    dims = {"nn": NN, "nt": NT, "tn": TN}[mode]

    def body(a_ref, b_ref, *rest):
        prod = _dot(a_ref[...].astype(bf16), b_ref[...].astype(bf16), dims)
        if single:
            o_ref = rest[-1]
            o_ref[...] = (prod if add is None else prod + add_scale * rest[0][...]).astype(o_ref.dtype)
            return
        o_ref, acc = rest[-2:]
        k = pl.program_id(2)

        @pl.when(k == 0)
        def _():
            acc[...] = prod if add is None else prod + add_scale * rest[0][...]

        @pl.when(k > 0)
        def _():
            acc[...] += prod

        @pl.when(k == nk - 1)
        def _():
            o_ref[...] = acc[...].astype(o_ref.dtype)

    if mode == "tn":
        a_spec = pl.BlockSpec((tk, tm), lambda i, j, k: (k, i + win))
    else:
        a_spec = pl.BlockSpec((tm, tk), lambda i, j, k: (i, k + win))
    if mode == "nt":
        b_spec = pl.BlockSpec((tn, tk), lambda i, j, k: (j, k))
    else:
        b_spec = pl.BlockSpec((tk, tn), lambda i, j, k: (k, j))
    o_spec = pl.BlockSpec((tm, tn), lambda i, j, k: (i, j))
    return pl.pallas_call(
        body, name=name, grid=(M // tm, N // tn, nk),
        in_specs=[a_spec, b_spec] + ([] if add is None else [o_spec]), out_specs=o_spec,
        out_shape=jax.ShapeDtypeStruct((M, N), out_dtype),
        scratch_shapes=[] if single else [pltpu.VMEM((tm, tn), f32)],
        compiler_params=_cp("parallel", "parallel", "arbitrary"),
    )(*((a, b) if add is None else (a, b, add)))


def _shift_down(cur, prev8, k):
    if k == 0:
        return cur
    T, (R, C) = SUBLANES, cur.shape
    rot = pltpu.roll(cur.reshape(R // T, T, C), k, 1)
    before = jnp.concatenate([pltpu.roll(prev8, k, 0)[None], rot[:-1]], axis=0)
    row = lax.broadcasted_iota(jnp.int32, (R // T, T, C), 1)
    return jnp.where(row < k, before, rot).reshape(R, C)


def _shift_up(cur, next8, k):
    if k == 0:
        return cur
    T, (R, C) = SUBLANES, cur.shape
    rot = pltpu.roll(cur.reshape(R // T, T, C), T - k, 1)
    after = jnp.concatenate([rot[1:], pltpu.roll(next8, T - k, 0)[None]], axis=0)
    row = lax.broadcasted_iota(jnp.int32, (R // T, T, C), 1)
    return jnp.where(row < T - k, rot, after).reshape(R, C)


def _prev_halo_spec(rt, cols, ncolblk_fn):
    per = rt // SUBLANES
    return pl.BlockSpec((SUBLANES, cols), lambda *g: (jnp.maximum(g[-1] * per - 1, 0), ncolblk_fn(*g)))


def _mm_ln_fwd(a, w, h, g, b, name):
    S, K = a.shape
    D = w.shape[1]
    rt = _pick(S, ROW_TILE, SUBLANES)

    def body(a_ref, w_ref, h_ref, g_ref, b_ref, o_ref, z_ref):
        z = ALPHA * h_ref[...] + _dot(a_ref[...].astype(bf16), w_ref[...].astype(bf16), NN)
        mu = jnp.mean(z, axis=-1, keepdims=True)
        zc = z - mu
        var = jnp.mean(zc * zc, axis=-1, keepdims=True)
        o_ref[...] = zc * lax.rsqrt(var + LN_EPS) * g_ref[...] + b_ref[...]
        z_ref[...] = z

    row = pl.BlockSpec((rt, D), lambda i: (i, 0))
    vec = pl.BlockSpec((1, D), lambda i: (0, 0))
    return pl.pallas_call(
        body, name=name, grid=(S // rt,),
        in_specs=[pl.BlockSpec((rt, K), lambda i: (i, 0)), pl.BlockSpec((K, D), lambda i: (0, 0)), row, vec, vec],
        out_specs=[row, row], out_shape=[jax.ShapeDtypeStruct((S, D), f32)] * 2, compiler_params=_cp("parallel"),
    )(a, w, h, g.reshape(1, D), b.reshape(1, D))


def _ln_bwd(dy_a, dy_b, z, g, name):
    S, D = z.shape
    rt = _pick(S, ROW_TILE, SUBLANES)
    two = dy_a is not None

    def body(*refs):
        if two:
            a_ref, b_ref, z_ref, g_ref, dz_ref, acc_ref = refs
            dy = ALPHA * a_ref[...] + b_ref[...]
        else:
            b_ref, z_ref, g_ref, dz_ref, acc_ref = refs
            dy = b_ref[...]
        z = z_ref[...]
        mu = jnp.mean(z, axis=-1, keepdims=True)
        zc = z - mu
        var = jnp.mean(zc * zc, axis=-1, keepdims=True)
        rstd = lax.rsqrt(var + LN_EPS)
        xhat = zc * rstd
        dxh = dy * g_ref[...]
        m1 = jnp.mean(dxh, axis=-1, keepdims=True)
        m2 = jnp.mean(dxh * xhat, axis=-1, keepdims=True)
        dz_ref[...] = rstd * (dxh - m1 - xhat * m2)

        @pl.when(pl.program_id(0) == 0)
        def _():
            acc_ref[...] = jnp.zeros_like(acc_ref)

        acc_ref[0:1, :] += jnp.sum(dy * xhat, axis=0, keepdims=True)
        acc_ref[1:2, :] += jnp.sum(dy, axis=0, keepdims=True)

    row = pl.BlockSpec((rt, D), lambda i: (i, 0))
    vec = pl.BlockSpec((1, D), lambda i: (0, 0))
    acc = pl.BlockSpec((SUBLANES, D), lambda i: (0, 0))
    ins = ([dy_a] if two else []) + [dy_b, z, g.reshape(1, D)]
    return pl.pallas_call(
        body, name=name, grid=(S // rt,), in_specs=[row] * (len(ins) - 1) + [vec], out_specs=[row, acc],
        out_shape=[jax.ShapeDtypeStruct((S, D), f32), jax.ShapeDtypeStruct((SUBLANES, D), f32)],
        compiler_params=_cp("arbitrary"),
    )(*ins)


def _loss_head(y, target):
    S, D = y.shape
    rt = _pick(S, ROW_TILE, SUBLANES)

    def body(y_ref, t_ref, dy_ref, acc_ref):
        e = y_ref[...] - t_ref[...]
        dy_ref[...] = e * (1.0 / D)

        @pl.when(pl.program_id(0) == 0)
        def _():
            acc_ref[...] = jnp.zeros_like(acc_ref)

        part = jnp.sum(jnp.mean(e * e, axis=-1, keepdims=True), axis=0, keepdims=True)
        acc_ref[...] += 0.5 * part

    row = pl.BlockSpec((rt, D), lambda i: (i, 0))
    return pl.pallas_call(
        body, name="loss_head", grid=(S // rt,), in_specs=[row, row],
        out_specs=[row, pl.BlockSpec((1, 1), lambda i: (0, 0))],
        out_shape=[jax.ShapeDtypeStruct((S, D), f32), jax.ShapeDtypeStruct((1, 1), f32)],
        compiler_params=_cp("arbitrary"),
    )(y, target)


def _axpy(a, b, name):
    S, D = a.shape
    rt = _pick(S, ROW_TILE, SUBLANES)

    def body(a_ref, b_ref, o_ref):
        o_ref[...] = ALPHA * a_ref[...] + b_ref[...]

    row = pl.BlockSpec((rt, D), lambda i: (i, 0))
    return pl.pallas_call(
        body, name=name, grid=(S // rt,), in_specs=[row, row], out_specs=row,
        out_shape=jax.ShapeDtypeStruct((S, D), f32), compiler_params=_cp("parallel"),
    )(a, b)


def _rope_tables(S):
    rt = _pick(S, ROW_TILE, SUBLANES)

    def body(c_ref, s_ref):
        pos = (pl.program_id(0) * rt + lax.broadcasted_iota(jnp.int32, (rt, LANES), 0)).astype(f32)
        lane = lax.broadcasted_iota(jnp.int32, (rt, LANES), 1)
        j = (lane % (HEAD_DIM // 2)).astype(f32)
        inv = jnp.exp((-j * 2.0 / HEAD_DIM) * math.log(ROPE_THETA))
        ang = pos * inv
        c = jnp.cos(ang)
        s = jnp.where(lane % HEAD_DIM < HEAD_DIM // 2, -jnp.sin(ang), jnp.sin(ang))
        c_ref[...] = jnp.concatenate([c, c, c], axis=1)
        s_ref[...] = jnp.concatenate([s, s, s], axis=1)

    row = pl.BlockSpec((rt, ATTN_WIDTH), lambda i: (i, 0))
    return pl.pallas_call(
        body, name="rope_tables", grid=(S // rt,), in_specs=[], out_specs=[row, row],
        out_shape=[jax.ShapeDtypeStruct((S, ATTN_WIDTH), f32)] * 2, compiler_params=_cp("parallel"),
    )()


def _swap_halves(x):
    lane = lax.broadcasted_iota(jnp.int32, x.shape, 1)
    half = HEAD_DIM // 2
    return jnp.where(lane % HEAD_DIM < half, pltpu.roll(x, x.shape[1] - half, 1), pltpu.roll(x, half, 1))


def _rope_fwd(proj, cos, sin):
    S, W = proj.shape[0], ATTN_WIDTH
    rt = _pick(S, ROW_TILE, SUBLANES)

    def body(q_ref, k_ref, c_ref, s_ref, qo_ref, ko_ref):
        c, s = c_ref[...], s_ref[...]
        qo_ref[...] = q_ref[...] * c + _swap_halves(q_ref[...]) * s
        ko_ref[...] = k_ref[...] * c + _swap_halves(k_ref[...]) * s

    row = pl.BlockSpec((rt, W), lambda i: (i, 0))
    return pl.pallas_call(
        body, name="rope_fwd", grid=(S // rt,), in_specs=[row, pl.BlockSpec((rt, W), lambda i: (i, 1)), row, row],
        out_specs=[row, row], out_shape=[jax.ShapeDtypeStruct((S, W), f32)] * 2, compiler_params=_cp("parallel"),
    )(proj, proj, cos, sin)


def _rope_bwd(dq, dk, cos, sin):
    S, W = dq.shape
    rt = _pick(S, ROW_TILE, SUBLANES)

    def body(q_ref, k_ref, c_ref, s_ref, qo_ref, ko_ref):
        c, s = c_ref[...], s_ref[...]
        qo_ref[...] = q_ref[...] * c + _swap_halves(q_ref[...] * s)
        ko_ref[...] = k_ref[...] * c + _swap_halves(k_ref[...] * s)

    row = pl.BlockSpec((rt, W), lambda i: (i, 0))
    return pl.pallas_call(
        body, name="rope_bwd", grid=(S // rt,), in_specs=[row] * 4, out_specs=[row] * 2,
        out_shape=[jax.ShapeDtypeStruct((S, W), f32)] * 2, compiler_params=_cp("parallel"),
    )(dq, dk, cos, sin)


def _rows(ref, start, d):
    if d == 1:
        return ref[pl.ds(pl.multiple_of(start, ATTN_BLOCK), ATTN_BLOCK), :]
    return ref[pl.ds(start, ATTN_BLOCK, stride=d), :]


def _set_rows(ref, start, d, val):
    if d == 1:
        ref[pl.ds(pl.multiple_of(start, ATTN_BLOCK), ATTN_BLOCK), :] = val
    else:
        ref[pl.ds(start, ATTN_BLOCK, stride=d), :] = val


def _pair_spec(S, first_block):
    return pl.BlockSpec((S, LANES), lambda p: (0, p + first_block))


def _attn_fwd2(qr, kr, proj, shards=(), split=()):
    S = qr.shape[0]
    B = ATTN_BLOCK
    nb = S // B
    scale = HEAD_DIM ** -0.5

    gather = _Gather(shards, split)
    nt = gather.nt

    def body(*refs):
        q_ref, k_ref, v_ref = refs[:3]
        g_ins = refs[3:3 + nt]
        o_ref, l_ref = refs[3 + nt:5 + nt]
        g_outs = refs[5 + nt:5 + 2 * nt]
        m_s, l_s = refs[5 + 2 * nt:7 + 2 * nt]
        g_sems = refs[7 + 2 * nt:]
        if nt:
            @pl.when(pl.program_id(0) == 0)
            def _():
                gather.start(g_ins, g_outs, g_sems)

        qi = lax.broadcasted_iota(jnp.int32, (B, 2 * B), 0)
        ki = lax.broadcasted_iota(jnp.int32, (B, 2 * B), 1)
        dist = qi + B - ki
        band = (dist >= 0) & (dist <= B)
        for bi, d in enumerate(DILATIONS):
            bpc = nb // d

            def blk(b, carry, bi=bi, d=d, bpc=bpc):
                c, n = b // bpc, b % bpc
                start = c + d * B * n
                pstart = c + d * B * jnp.maximum(n - 1, 0)
                valid = band & ((ki >= B) | (n > 0))
                q = _rows(q_ref, start, d).astype(bf16)
                kcat = jnp.concatenate([_rows(k_ref, pstart, d), _rows(k_ref, start, d)], axis=0).astype(bf16)
                vcat = jnp.concatenate([_rows(v_ref, pstart, d), _rows(v_ref, start, d)], axis=0).astype(bf16)
                if bi > 0:
                    m_old, l_old, a_old = _rows(m_s, start, d), _rows(l_s, start, d), _rows(o_ref, start, d)
                ms, ls, accs = [], [], []
                for h in range(2):
                    sl = slice(h * HEAD_DIM, (h + 1) * HEAD_DIM)
                    c0 = h * HEAD_DIM
                    s = jnp.where(valid, _dot(q[:, sl], kcat[:, sl], NT) * scale, NEG)
                    m = jnp.max(s, axis=1, keepdims=True)
                    if bi > 0:
                        mo = m_old[:, c0:c0 + 1]
                        m = jnp.maximum(m, mo)
                        alpha = jnp.exp(mo - m)
                    p = jnp.exp(s - m)
                    l = jnp.sum(p, axis=1, keepdims=True)
                    acc = _dot(p.astype(bf16), vcat[:, sl], NN)
                    if bi > 0:
                        l = l + alpha * l_old[:, c0:c0 + 1]
                        acc = acc + alpha * a_old[:, sl]
                    ms.append(jnp.broadcast_to(m, (B, HEAD_DIM)))
                    ls.append(jnp.broadcast_to(l, (B, HEAD_DIM)))
                    accs.append(acc)
                _set_rows(m_s, start, d, jnp.concatenate(ms, axis=1))
                _set_rows(l_s, start, d, jnp.concatenate(ls, axis=1))
                _set_rows(o_ref, start, d, jnp.concatenate(accs, axis=1))
                return carry

            lax.fori_loop(0, nb, blk, 0, unroll=4)

        def fin(t, carry):
            rows = pl.ds(pl.multiple_of(t * B, B), B)
            l = l_s[rows, :]
            o_ref[rows, :] = o_ref[rows, :] / l
            l_ref[rows, :] = m_s[rows, :] + jnp.log(l)
            return carry

        lax.fori_loop(0, nb, fin, 0)
        if nt:
            @pl.when(pl.program_id(0) == pl.num_programs(0) - 1)
            def _():
                gather.finish(g_ins, g_outs, g_sems)

    pair = _pair_spec(S, 0)
    res = pl.pallas_call(
        body, name="attn_fwd_gather" if nt else "attn_fwd", grid=(3,),
        in_specs=[pair, pair, _pair_spec(S, 2 * ATTN_WIDTH // LANES)] + gather.in_specs,
        out_specs=[pair, pair] + gather.out_specs,
        out_shape=[jax.ShapeDtypeStruct((S, ATTN_WIDTH), f32)] * 2 + gather.out_shape,
        scratch_shapes=[pltpu.VMEM((S, LANES), f32)] * 2 + gather.scratch,
        compiler_params=_cp("arbitrary"),
    )(qr, kr, proj, *shards)
    return res[0], res[1], list(res[2:])


def _attn_bwd2(qr, kr, proj, dattn, ltot, delta, slabs=(), only_c=0, send=(), send_dst=0):
    S = qr.shape[0]
    B = ATTN_BLOCK
    nb = S // B
    scale = HEAD_DIM ** -0.5
    ex = _ChipExchange(slabs, (), only_c)
    snd = _SiblingSend(send, send_dst)
    n, ns = ex.n, snd.n
    hosted = n + ns

    def body(*refs):
        q_ref, k_ref, v_ref, do_ref, l_ref, d_ref = refs[:6]
        x_ins, s_ins = refs[6:6 + n], refs[6 + n:6 + hosted]
        dq_ref, dk_ref, dv_ref = refs[6 + hosted:9 + hosted]
        x_outs, s_outs = refs[9 + hosted:9 + hosted + n], refs[9 + hosted + n:9 + 2 * hosted]
        sems = refs[9 + 2 * hosted:]
        x_sems, s_sems = sems[:len(ex.scratch)], sems[len(ex.scratch):]
        if hosted:
            @pl.when(pl.program_id(0) == 0)
            def _():
                if n:
                    ex.start(x_ins, x_outs, x_sems)
                if ns:
                    snd.start(s_ins, s_outs, s_sems)

        qi = lax.broadcasted_iota(jnp.int32, (B, 2 * B), 0)
        ki = lax.broadcasted_iota(jnp.int32, (B, 2 * B), 1)
        dist1 = qi + B - ki
        band1 = (dist1 >= 0) & (dist1 <= B)
        ri = lax.broadcasted_iota(jnp.int32, (2 * B, B), 0)
        ci = lax.broadcasted_iota(jnp.int32, (2 * B, B), 1)
        dist2 = ri - ci
        band2 = (dist2 >= 0) & (dist2 <= B)
        for bi, d in enumerate(DILATIONS):
            bpc = nb // d

            def blk(b, carry, bi=bi, d=d, bpc=bpc):
                c, n = b // bpc, b % bpc
                start = c + d * B * n
                pstart = c + d * B * jnp.maximum(n - 1, 0)
                nstart = c + d * B * jnp.minimum(n + 1, bpc - 1)
                valid1 = band1 & ((ki >= B) | (n > 0))
                valid2 = band2 & ((ri < B) | (n + 1 < bpc))
                q_c, q_n = _rows(q_ref, start, d), _rows(q_ref, nstart, d)
                k_p, k_c = _rows(k_ref, pstart, d), _rows(k_ref, start, d)
                v_p, v_c = _rows(v_ref, pstart, d), _rows(v_ref, start, d)
                do_c, do_n = _rows(do_ref, start, d), _rows(do_ref, nstart, d)
                l_c, l_n = _rows(l_ref, start, d), _rows(l_ref, nstart, d)
                d_c, d_n = _rows(d_ref, start, d), _rows(d_ref, nstart, d)
                qc = q_c.astype(bf16)
                qcat = jnp.concatenate([q_c, q_n], axis=0).astype(bf16)
                kc = k_c.astype(bf16)
                kcat = jnp.concatenate([k_p, k_c], axis=0).astype(bf16)
                vc = v_c.astype(bf16)
                vcat = jnp.concatenate([v_p, v_c], axis=0).astype(bf16)
                doc = do_c.astype(bf16)
                docat = jnp.concatenate([do_c, do_n], axis=0).astype(bf16)
                lcat = jnp.concatenate([l_c, l_n], axis=0)
                dcat = jnp.concatenate([d_c, d_n], axis=0)
                dqs, dks, dvs = [], [], []
                for h in range(2):
                    sl = slice(h * HEAD_DIM, (h + 1) * HEAD_DIM)
                    c0 = h * HEAD_DIM
                    s1 = _dot(qc[:, sl], kcat[:, sl], NT) * scale
                    p1 = jnp.where(valid1, jnp.exp(s1 - l_c[:, c0:c0 + 1]), 0.0)
                    dp1 = _dot(doc[:, sl], vcat[:, sl], NT)
                    ds1 = p1 * (dp1 - d_c[:, c0:c0 + 1]) * scale
                    dqs.append(_dot(ds1.astype(bf16), kcat[:, sl], NN))
                    s2 = _dot(qcat[:, sl], kc[:, sl], NT) * scale
                    p2 = jnp.where(valid2, jnp.exp(s2 - lcat[:, c0:c0 + 1]), 0.0)
                    dvs.append(_dot(p2.astype(bf16), docat[:, sl], TN))
                    dp2 = _dot(docat[:, sl], vc[:, sl], NT)
                    ds2 = p2 * (dp2 - dcat[:, c0:c0 + 1]) * scale
                    dks.append(_dot(ds2.astype(bf16), qcat[:, sl], TN))
                for ref, parts in ((dq_ref, dqs), (dk_ref, dks), (dv_ref, dvs)):
                    new = jnp.concatenate(parts, axis=1)
                    if bi > 0:
                        new = new + _rows(ref, start, d)
                    _set_rows(ref, start, d, new)
                return carry

            lax.fori_loop(0, nb, blk, 0, unroll=4)

        if hosted:
            @pl.when(pl.program_id(0) == pl.num_programs(0) - 1)
            def _():
                if ns:
                    snd.finish(s_ins, s_outs, s_sems)
                if n:
                    ex.finish(x_ins, x_outs, x_sems)

    pair = _pair_spec(S, 0)
    res = pl.pallas_call(
        body, name="attn_bwd_exchange" if hosted else "attn_bwd", grid=(3,),
        in_specs=[pair, pair, _pair_spec(S, 2 * ATTN_WIDTH // LANES), pair, pair, pair] + ex.in_specs + snd.in_specs,
        out_specs=[pair] * 3 + ex.out_specs + snd.out_specs,
        out_shape=[jax.ShapeDtypeStruct((S, ATTN_WIDTH), f32)] * 3 + ex.out_shape + snd.out_shape,
        scratch_shapes=ex.scratch + snd.scratch, compiler_params=_cp("arbitrary"),
    )(qr, kr, proj, dattn, ltot, delta, *slabs, *send)
    return res[0], res[1], res[2], list(res[3:3 + n]), list(res[3 + n:])


def _softplus_neg(lam):
    return jnp.maximum(-lam, 0.0) + jnp.log1p(jnp.exp(-jnp.abs(lam)))


PROJ_LRU_X, PROJ_LRU_GATE, PROJ_S5_U = 3, 4, 5
EARLY_OWNER = 1


def _lru_pre(proj, conv_w, conv_b, wr, br, wi, bi, lam):
    S, W = proj.shape[0], LRU_WIDTH
    rt = _pick(S, ROW_TILE, SUBLANES)
    K = conv_w.shape[0]

    def body(x_ref, xp_ref, cw_ref, cb_ref, wr_ref, br_ref, wi_ref, bi_ref, lam_ref,
             xc_ref, r_ref, i_ref, la_ref, u_ref):
        prev = jnp.where(pl.program_id(0) == 0, 0.0, xp_ref[...])
        x = x_ref[...]
        xc = cb_ref[...] + cw_ref[K - 1:K, :] * x
        for k in range(K - 1):
            xc = xc + cw_ref[k:k + 1, :] * _shift_down(x, prev, K - 1 - k)
        xb = xc.astype(bf16)
        r = _sigmoid(_dot(xb, wr_ref[...], NN) + br_ref[...])
        i = _sigmoid(_dot(xb, wi_ref[...], NN) + bi_ref[...])
        log_a = -LRU_C * r * _softplus_neg(lam_ref[...])
        u = jnp.sqrt(-_expm1(2.0 * log_a)) * (i * xc)
        xc_ref[...], r_ref[...], i_ref[...], la_ref[...], u_ref[...] = xc, r, i, log_a, u

    row = pl.BlockSpec((rt, W), lambda i: (i, 0))
    xrow = pl.BlockSpec((rt, W), lambda i: (i, PROJ_LRU_X))
    halo = _prev_halo_spec(rt, W, lambda i: PROJ_LRU_X)
    vec = pl.BlockSpec((1, W), lambda i: (0, 0))
    return pl.pallas_call(
        body, name="lru_pre", grid=(S // rt,),
        in_specs=[xrow, halo, pl.BlockSpec((K, W), lambda i: (0, 0)), vec,
                  pl.BlockSpec((W, W), lambda i: (0, 0)), vec, pl.BlockSpec((W, W), lambda i: (0, 0)), vec, vec],
        out_specs=[row] * 5, out_shape=[jax.ShapeDtypeStruct((S, W), f32)] * 5, compiler_params=_cp("parallel"),
    )(proj, proj, conv_w, conv_b.reshape(1, W), wr, br.reshape(1, W), wi, bi.reshape(1, W), lam.reshape(1, W))


def _tile_rows(shape):
    return lax.broadcasted_iota(jnp.int32, shape, 0)


def _lru_scan(log_a, u, proj):
    S, W = u.shape
    rt = _pick(S, ROW_TILE, SUBLANES)
    T = SUBLANES

    def body(la_ref, u_ref, g_ref, h_ref, o_ref, carry):
        @pl.when(pl.program_id(0) == 0)
        def _():
            carry[...] = jnp.zeros_like(carry)

        row = _tile_rows((T, W))

        def step(t, hp):
            r0 = pl.multiple_of(t * T, T)
            a = jnp.exp(la_ref[pl.ds(r0, T), :])
            x = u_ref[pl.ds(r0, T), :]
            for k in (1, 2, 4):
                x = x + a * jnp.where(row >= k, pltpu.roll(x, k, 0), 0.0)
                a = a * jnp.where(row >= k, pltpu.roll(a, k, 0), 1.0)
            h = x + a * hp
            h_ref[pl.ds(r0, T), :] = h
            o_ref[pl.ds(r0, T), :] = h * _gelu(g_ref[pl.ds(r0, T), :])
            return h[T - 1:T, :]

        carry[0:1, :] = lax.fori_loop(0, rt // T, step, carry[0:1, :])

    row = pl.BlockSpec((rt, W), lambda i: (i, 0))
    grow = pl.BlockSpec((rt, W), lambda i: (i, PROJ_LRU_GATE))
    return pl.pallas_call(
        body, name="lru_scan", grid=(S // rt,), in_specs=[row, row, grow], out_specs=[row] * 2,
        out_shape=[jax.ShapeDtypeStruct((S, W), f32)] * 2, scratch_shapes=[pltpu.VMEM((T, W), f32)],
        compiler_params=_cp("arbitrary"),
    )(log_a, u, proj)


def _lru_scan_bwd(dlru, proj, h, log_a):
    S, W = h.shape
    rt = _pick(S, ROW_TILE, SUBLANES)
    T = SUBLANES
    nblk = S // rt

    def body(d_ref, g_ref, h_ref, la_ref, go_ref, dg_ref, carry):
        @pl.when(pl.program_id(0) == 0)
        def _():
            carry[...] = jnp.zeros_like(carry)

        row = _tile_rows((T, W))

        def step(j, c):
            gn, an = c
            t = rt // T - 1 - j
            r0 = pl.multiple_of(t * T, T)
            d = d_ref[pl.ds(r0, T), :]
            gate = g_ref[pl.ds(r0, T), :]
            a = jnp.exp(la_ref[pl.ds(r0, T), :])
            dg_ref[pl.ds(r0, T), :] = d * h_ref[pl.ds(r0, T), :] * _gelu_grad(gate)
            x = d * _gelu(gate)
            b = jnp.where(row < T - 1, pltpu.roll(a, T - 1, 0), an)
            for k in (1, 2, 4):
                x = x + b * jnp.where(row < T - k, pltpu.roll(x, T - k, 0), 0.0)
                b = b * jnp.where(row < T - k, pltpu.roll(b, T - k, 0), 1.0)
            g = x + b * gn
            go_ref[pl.ds(r0, T), :] = g
            return g[0:1, :], a[0:1, :]

        gn, an = lax.fori_loop(0, rt // T, step, (carry[0:1, :], carry[1:2, :]))
        carry[0:1, :] = gn
        carry[1:2, :] = an

    row = pl.BlockSpec((rt, W), lambda i: (nblk - 1 - i, 0))
    grow = pl.BlockSpec((rt, W), lambda i: (nblk - 1 - i, PROJ_LRU_GATE))
    return pl.pallas_call(
        body, name="lru_scan_bwd", grid=(nblk,), in_specs=[row, grow, row, row], out_specs=[row] * 2,
        out_shape=[jax.ShapeDtypeStruct((S, W), f32)] * 2, scratch_shapes=[pltpu.VMEM((T, W), f32)],
        compiler_params=_cp("arbitrary"),
    )(dlru, proj, h, log_a)


def _lru_gate_bwd(g, h, xc, r, i, log_a, wr, wi, lam):
    S, W = g.shape
    rt = _pick(S, ROW_TILE, SUBLANES)

    def body(g_ref, h_ref, hp_ref, xc_ref, r_ref, i_ref, la_ref, wr_ref, wi_ref, lam_ref,
             dxc_ref, dwr_ref, dwi_ref, acc_ref):
        @pl.when(pl.program_id(0) == 0)
        def _():
            dwr_ref[...] = jnp.zeros_like(dwr_ref)
            dwi_ref[...] = jnp.zeros_like(dwi_ref)
            acc_ref[...] = jnp.zeros_like(acc_ref)

        prev = jnp.where(pl.program_id(0) == 0, 0.0, hp_ref[...])
        gg, xc, r, i, log_a, lam = g_ref[...], xc_ref[...], r_ref[...], i_ref[...], la_ref[...], lam_ref[...]
        hm1 = _shift_down(h_ref[...], prev, 1)
        a = jnp.exp(log_a)
        s = jnp.sqrt(-_expm1(2.0 * log_a))
        da = gg * hm1
        di = gg * s * xc
        dxc = gg * s * i
        ds = gg * i * xc
        dlog_a = da * a - ds * (a * a / s)
        sp = _softplus_neg(lam)
        dr = dlog_a * (-LRU_C * sp)
        dsp = jnp.sum(dlog_a * (-LRU_C * r), axis=0, keepdims=True)
        dpr = dr * r * (1.0 - r)
        dpi = di * i * (1.0 - i)
        dprb, dpib, xb = dpr.astype(bf16), dpi.astype(bf16), xc.astype(bf16)
        dxc_ref[...] = dxc + _dot(dprb, wr_ref[...], NT) + _dot(dpib, wi_ref[...], NT)
        dwr_ref[...] += _dot(xb, dprb, TN)
        dwi_ref[...] += _dot(xb, dpib, TN)
        acc_ref[0:1, :] += jnp.sum(dpr, axis=0, keepdims=True)
        acc_ref[1:2, :] += jnp.sum(dpi, axis=0, keepdims=True)
        acc_ref[2:3, :] += dsp * (-_sigmoid(-lam))

    row = pl.BlockSpec((rt, W), lambda i: (i, 0))
    halo = _prev_halo_spec(rt, W, lambda i: 0)
    vec = pl.BlockSpec((1, W), lambda i: (0, 0))
    mat = pl.BlockSpec((W, W), lambda i: (0, 0))
    acc = pl.BlockSpec((SUBLANES, W), lambda i: (0, 0))
    return pl.pallas_call(
        body, name="lru_gate_bwd", grid=(S // rt,),
        in_specs=[row, row, halo, row, row, row, row, mat, mat, vec], out_specs=[row, mat, mat, acc],
        out_shape=[jax.ShapeDtypeStruct((S, W), f32), jax.ShapeDtypeStruct((W, W), f32),
                   jax.ShapeDtypeStruct((W, W), f32), jax.ShapeDtypeStruct((SUBLANES, W), f32)],
        compiler_params=_cp("arbitrary"),
    )(g, h, h, xc, r, i, log_a, wr, wi, lam.reshape(1, W))


def _conv_bwd(dy, x, conv_w, name, col_tile=None, out_dtype=f32, x_col_block=0, send=(), send_dst=0):
    if dy.ndim == 2:
        dy = dy[None]
    H, S, Ch = dy.shape
    C = H * Ch
    K = conv_w.shape[0]
    ct = Ch if col_tile is None else col_tile
    nct = Ch // ct
    rt = _pick(S, ROW_TILE, SUBLANES)
    nrt = S // rt
    snd = _SiblingSend(send, send_dst)
    n = snd.n

    def body(*refs):
        dy_ref, dyn_ref, x_ref, w_ref = refs[:4]
        s_ins = refs[4:4 + n]
        dx_ref, acc_ref = refs[4 + n:6 + n]
        s_outs, s_sems = refs[6 + n:6 + 2 * n], refs[6 + 2 * n:]
        i = pl.program_id(2)
        if n:
            @pl.when((pl.program_id(0) == 0) & (pl.program_id(1) == 0) & (i == 0))
            def _():
                snd.start(s_ins, s_outs, s_sems)

        @pl.when(i == 0)
        def _():
            acc_ref[...] = jnp.zeros_like(acc_ref)

        nxt = jnp.where(i == nrt - 1, 0.0, dyn_ref[...].astype(f32)[0:SUBLANES])
        dy, x = dy_ref[...].astype(f32), x_ref[...]
        ahead = [dy] + [_shift_up(dy, nxt, j) for j in range(1, K)]
        dx = w_ref[K - 1:K, :] * dy
        for k in range(K - 1):
            dx = dx + w_ref[k:k + 1, :] * ahead[K - 1 - k]
        dx_ref[...] = dx.astype(dx_ref.dtype)
        for k in range(K):
            acc_ref[k:k + 1, :] += jnp.sum(ahead[K - 1 - k] * x, axis=0, keepdims=True)
        acc_ref[K:K + 1, :] += jnp.sum(dy, axis=0, keepdims=True)
        if n:
            @pl.when((pl.program_id(0) == H - 1) & (pl.program_id(1) == nct - 1) & (i == nrt - 1))
            def _():
                snd.finish(s_ins, s_outs, s_sems)

    halo = SUBLANES * (4 // dy.dtype.itemsize)
    per, last = rt // halo, S // halo - 1
    dy_row = pl.BlockSpec((None, rt, ct), lambda h, j, i: (h, i, j))
    dy_next = pl.BlockSpec((None, halo, ct), lambda h, j, i: (h, jnp.minimum((i + 1) * per, last), j))
    row = pl.BlockSpec((rt, ct), lambda h, j, i: (i, h * nct + j))
    xrow = pl.BlockSpec((rt, ct), lambda h, j, i: (i, h * nct + j + x_col_block))
    res = pl.pallas_call(
        body, name=name, grid=(H, nct, nrt),
        in_specs=[dy_row, dy_next, xrow, pl.BlockSpec((K, ct), lambda h, j, i: (0, h * nct + j))] + snd.in_specs,
        out_specs=[row, pl.BlockSpec((SUBLANES, ct), lambda h, j, i: (0, h * nct + j))] + snd.out_specs,
        out_shape=[jax.ShapeDtypeStruct((S, C), out_dtype), jax.ShapeDtypeStruct((SUBLANES, C), f32)] + snd.out_shape,
        scratch_shapes=snd.scratch,
        compiler_params=_cp(*(("arbitrary",) * 3 if n else ("parallel", "parallel", "arbitrary"))),
    )(dy, dy, x, conv_w, *send)
    return res[0], res[1], list(res[2:])


def _s5_param_fn(a_re, a_im, ls, bt_re, bt_im):
    step = jnp.exp(ls)
    dt_re, dt_im = step * a_re, step * a_im
    mag = jnp.exp(dt_re)
    ab_re, ab_im = mag * jnp.cos(dt_im), mag * jnp.sin(dt_im)
    z_re, z_im = ab_re - 1.0, ab_im
    den = a_re * a_re + a_im * a_im
    f_re = (z_re * a_re + z_im * a_im) / den
    f_im = (z_im * a_re - z_re * a_im) / den
    bb_re = f_re[:, None, :] * bt_re - f_im[:, None, :] * bt_im
    bb_im = f_re[:, None, :] * bt_im + f_im[:, None, :] * bt_re
    return ab_re, ab_im, bb_re, bb_im


def _s5_params(a_re, a_im, ls, bt_re, bt_im):
    def body(ar, ai, l, br, bi, o_ar, o_ai, o_br, o_bi):
        o_ar[...], o_ai[...], o_br[...], o_bi[...] = _s5_param_fn(ar[...], ai[...], l[...], br[...], bi[...])

    return pl.pallas_call(
        body, name="s5_params",
        out_shape=[jax.ShapeDtypeStruct(a_re.shape, f32)] * 2 + [jax.ShapeDtypeStruct(bt_re.shape, f32)] * 2,
        compiler_params=_cp(),
    )(a_re, a_im, ls, bt_re, bt_im)


def _s5_params_bwd(a_re, a_im, ls, bt_re, bt_im, d_ar, d_ai, d_br, d_bi):
    def body(ar, ai, l, br, bi, c_ar, c_ai, c_br, c_bi, g_ar, g_ai, g_l, g_br, g_bi):
        _, vjp = jax.vjp(_s5_param_fn, ar[...], ai[...], l[...], br[...], bi[...])
        g_ar[...], g_ai[...], g_l[...], g_br[...], g_bi[...] = vjp((c_ar[...], c_ai[...], c_br[...], c_bi[...]))

    return pl.pallas_call(
        body, name="s5_params_bwd",
        out_shape=[jax.ShapeDtypeStruct(a_re.shape, f32)] * 2 + [jax.ShapeDtypeStruct(ls.shape, f32)]
        + [jax.ShapeDtypeStruct(bt_re.shape, f32)] * 2,
        compiler_params=_cp(),
    )(a_re, a_im, ls, bt_re, bt_im, d_ar, d_ai, d_br, d_bi)


S5_CHUNK = 256


def _s5_power_tables(ab_ref, p_ref, w_ref, conj):
    T, L = SUBLANES, S5_LANES
    are = ab_ref[0:1, 0:L]
    aim = ab_ref[0:1, L:2 * L]
    if conj:
        aim = -aim
    pre, pim = are, aim
    for n in range(3):
        p_ref[n:n + 1, 0:L] = pre
        p_ref[n:n + 1, L:2 * L] = pim
        pre, pim = pre * pre - pim * pim, 2.0 * pre * pim
    row = _tile_rows((T, L))
    wre = jnp.zeros((T, L), f32)
    wim = jnp.zeros((T, L), f32)
    pre, pim = are, aim
    for n in range(T):
        tgt = (T - 1 - n) if conj else n
        wre = jnp.where(row == tgt, pre, wre)
        wim = jnp.where(row == tgt, pim, wim)
        pre, pim = pre * are - pim * aim, pre * aim + pim * are
    w_ref[:, 0:L] = wre
    w_ref[:, L:2 * L] = wim


def _s5_scan(bu, ab):
    S, L2 = bu.shape
    L = L2 // 2
    rt = _pick(S, 256, SUBLANES)
    T = SUBLANES
    CH = S5_CHUNK

    def body(bu_ref, ab_ref, x_ref, p_ref, w_ref, carry):
        @pl.when(pl.program_id(0) == 0)
        def _():
            carry[...] = jnp.zeros_like(carry)
            _s5_power_tables(ab_ref, p_ref, w_ref, conj=False)

        row = _tile_rows((T, CH))

        def step(t, _):
            r0 = pl.multiple_of(t * T, T)
            for c in range(L // CH):
                lre, lim = pl.ds(c * CH, CH), pl.ds(L + c * CH, CH)
                xr, xi = bu_ref[pl.ds(r0, T), lre], bu_ref[pl.ds(r0, T), lim]
                for n, k in enumerate((1, 2, 4)):
                    pr, pi = p_ref[n:n + 1, lre], p_ref[n:n + 1, lim]
                    sr = jnp.where(row >= k, pltpu.roll(xr, k, 0), 0.0)
                    si = jnp.where(row >= k, pltpu.roll(xi, k, 0), 0.0)
                    xr, xi = xr + pr * sr - pi * si, xi + pr * si + pi * sr
                cr, ci = carry[T - 1:T, lre], carry[T - 1:T, lim]
                wr, wi = w_ref[:, lre], w_ref[:, lim]
                xr, xi = xr + wr * cr - wi * ci, xi + wr * ci + wi * cr
                carry[:, lre] = xr
                carry[:, lim] = xi
                x_ref[pl.ds(r0, T), lre] = xr
                x_ref[pl.ds(r0, T), lim] = xi
            return 0

        lax.fori_loop(0, rt // T, step, 0)

    row_spec = pl.BlockSpec((rt, L2), lambda i: (i, 0))
    return pl.pallas_call(
        body, name="s5_scan", grid=(S // rt,), in_specs=[row_spec, pl.BlockSpec((1, L2), lambda i: (0, 0))],
        out_specs=row_spec, out_shape=jax.ShapeDtypeStruct((S, L2), f32),
        scratch_shapes=[pltpu.VMEM((T, L2), f32), pltpu.VMEM((T, L2), f32), pltpu.VMEM((T, L2), f32)],
        compiler_params=_cp("arbitrary"),
    )(bu, ab)


def _s5_scan_bwd(dx, x, ab):
    S, L2 = dx.shape
    L = L2 // 2
    rt = _pick(S, 256, SUBLANES)
    T = SUBLANES
    CH = S5_CHUNK
    nblk = S // rt
    per = rt // T

    def body(dx_ref, x_ref, xp_ref, ab_ref, g_ref, da_ref, p_ref, w_ref, carry, acc):
        pid = pl.program_id(0)

        @pl.when(pid == 0)
        def _():
            carry[...] = jnp.zeros_like(carry)
            acc[...] = jnp.zeros_like(acc)
            _s5_power_tables(ab_ref, p_ref, w_ref, conj=True)

        row = _tile_rows((T, CH))
        first_block = pid == nblk - 1

        def step(j, _):
            t = per - 1 - j
            r0 = pl.multiple_of(t * T, T)
            rp = pl.multiple_of(jnp.maximum(t - 1, 0) * T, T)
            for c in range(L // CH):
                lre, lim = pl.ds(c * CH, CH), pl.ds(L + c * CH, CH)
                gr, gi = dx_ref[pl.ds(r0, T), lre], dx_ref[pl.ds(r0, T), lim]
                for n, k in enumerate((1, 2, 4)):
                    pr, pi = p_ref[n:n + 1, lre], p_ref[n:n + 1, lim]
                    sr = jnp.where(row < T - k, pltpu.roll(gr, T - k, 0), 0.0)
                    si = jnp.where(row < T - k, pltpu.roll(gi, T - k, 0), 0.0)
                    gr, gi = gr + pr * sr - pi * si, gi + pr * si + pi * sr
                cr, ci = carry[0:1, lre], carry[0:1, lim]
                wr, wi = w_ref[:, lre], w_ref[:, lim]
                gr, gi = gr + wr * cr - wi * ci, gi + wr * ci + wi * cr
                carry[:, lre] = gr
                carry[:, lim] = gi
                g_ref[pl.ds(r0, T), lre] = gr
                g_ref[pl.ds(r0, T), lim] = gi
                xr, xi = x_ref[pl.ds(r0, T), lre], x_ref[pl.ds(r0, T), lim]
                in_blk_r, in_blk_i = x_ref[pl.ds(rp, T), lre], x_ref[pl.ds(rp, T), lim]
                hal_r = jnp.where(first_block, 0.0, xp_ref[:, lre])
                hal_i = jnp.where(first_block, 0.0, xp_ref[:, lim])
                pvr = jnp.where(t == 0, hal_r, in_blk_r)[T - 1:T, :]
                pvi = jnp.where(t == 0, hal_i, in_blk_i)[T - 1:T, :]
                sxr = jnp.where(row >= 1, pltpu.roll(xr, 1, 0), pvr)
                sxi = jnp.where(row >= 1, pltpu.roll(xi, 1, 0), pvi)
                acc[:, lre] += gr * sxr + gi * sxi
                acc[:, lim] += gi * sxr - gr * sxi
            return 0

        lax.fori_loop(0, per, step, 0)

        @pl.when(pid == nblk - 1)
        def _():
            da_ref[...] = jnp.sum(acc[...], axis=0, keepdims=True)

    row_spec = pl.BlockSpec((rt, L2), lambda i: (nblk - 1 - i, 0))
    halo = pl.BlockSpec((T, L2), lambda i: (jnp.maximum((nblk - 1 - i) * per - 1, 0), 0))
    vec = pl.BlockSpec((1, L2), lambda i: (0, 0))
    return pl.pallas_call(
        body, name="s5_scan_bwd", grid=(nblk,), in_specs=[row_spec, row_spec, halo, vec],
        out_specs=[row_spec, vec],
        out_shape=[jax.ShapeDtypeStruct((S, L2), f32), jax.ShapeDtypeStruct((1, L2), f32)],
        scratch_shapes=[pltpu.VMEM((T, L2), f32)] * 4,
        compiler_params=_cp("arbitrary"),
    )(dx, x, x, ab)


def _S5_U_SPEC(rt):
    return pl.BlockSpec((rt, LRU_WIDTH), lambda i: (i, PROJ_S5_U))


def _s5_out(yc, proj, d, wglu, bglu):
    S, W = yc.shape
    rt = _pick(S, ROW_TILE, SUBLANES)

    def body(yc_ref, u_ref, d_ref, w_ref, b_ref, o_ref, y_ref):
        y = yc_ref[...] + d_ref[...] * u_ref[:, 0:W]
        yg = _gelu(y)
        z = _dot(yg.astype(bf16), w_ref[...], NN) + b_ref[...]
        o_ref[...] = yg * _sigmoid(z)
        y_ref[...] = y

    row = pl.BlockSpec((rt, W), lambda i: (i, 0))
    vec = pl.BlockSpec((1, W), lambda i: (0, 0))
    mat = pl.BlockSpec((W, W), lambda i: (0, 0))
    return pl.pallas_call(
        body, name="s5_out", grid=(S // rt,), in_specs=[row, _S5_U_SPEC(rt), vec, mat, vec], out_specs=[row, row],
        out_shape=[jax.ShapeDtypeStruct((S, W), f32)] * 2, compiler_params=_cp("parallel"),
    )(yc, proj, d.reshape(1, W), wglu, bglu.reshape(1, W))


def _s5_out_bwd(dssm, y, proj, d, wglu, bglu):
    S, W = y.shape
    rt = _pick(S, ROW_TILE, SUBLANES)

    def body(do_ref, y_ref, u_ref, d_ref, w_ref, b_ref, dy_ref, du_ref, dw_ref, acc_ref):
        @pl.when(pl.program_id(0) == 0)
        def _():
            dw_ref[...] = jnp.zeros_like(dw_ref)
            acc_ref[...] = jnp.zeros_like(acc_ref)

        do, y = do_ref[...], y_ref[...]
        yg = _gelu(y)
        ygb = yg.astype(bf16)
        sg = _sigmoid(_dot(ygb, w_ref[...], NN) + b_ref[...])
        dz = do * yg * sg * (1.0 - sg)
        dzb = dz.astype(bf16)
        dyg = do * sg + _dot(dzb, w_ref[...], NT)
        dy = dyg * _gelu_grad(y)
        dy_ref[...] = dy
        du_ref[...] = dy * d_ref[...]
        dw_ref[...] += _dot(ygb, dzb, TN)
        acc_ref[0:1, :] += jnp.sum(dz, axis=0, keepdims=True)
        acc_ref[1:2, :] += jnp.sum(dy * u_ref[:, 0:W], axis=0, keepdims=True)

    row = pl.BlockSpec((rt, W), lambda i: (i, 0))
    vec = pl.BlockSpec((1, W), lambda i: (0, 0))
    mat = pl.BlockSpec((W, W), lambda i: (0, 0))
    acc = pl.BlockSpec((SUBLANES, W), lambda i: (0, 0))
    return pl.pallas_call(
        body, name="s5_out_bwd", grid=(S // rt,), in_specs=[row, row, _S5_U_SPEC(rt), vec, mat, vec],
        out_specs=[row, row, mat, acc],
        out_shape=[jax.ShapeDtypeStruct((S, W), f32)] * 2
        + [jax.ShapeDtypeStruct((W, W), f32), jax.ShapeDtypeStruct((SUBLANES, W), f32)],
        compiler_params=_cp("arbitrary"),
    )(dssm, y, proj, d.reshape(1, W), wglu, bglu.reshape(1, W))


MIX_SPLITS = ((0, ATTN_WIDTH), (ATTN_WIDTH, ATTN_WIDTH + LRU_WIDTH), (ATTN_WIDTH + LRU_WIDTH, D_MODEL))


def _mixnorm(attn, lru, ssm, g):
    S = attn.shape[0]
    rt = _pick(S, ROW_TILE, SUBLANES)

    def body(a_ref, l_ref, s_ref, g_ref, o_ref):
        for ref, (lo, hi) in zip((a_ref, l_ref, s_ref), MIX_SPLITS):
            x = ref[...]
            ms = jnp.mean(x * x, axis=-1, keepdims=True)
            o_ref[:, lo:hi] = (x * lax.rsqrt(ms + RMS_EPS) * g_ref[:, lo:hi]).astype(o_ref.dtype)

    rows = [pl.BlockSpec((rt, hi - lo), lambda i: (i, 0)) for lo, hi in MIX_SPLITS]
    return pl.pallas_call(
        body, name="mixnorm", grid=(S // rt,), in_specs=rows + [pl.BlockSpec((1, D_MODEL), lambda i: (0, 0))],
        out_specs=pl.BlockSpec((rt, D_MODEL), lambda i: (i, 0)),
        out_shape=jax.ShapeDtypeStruct((S, D_MODEL), bf16), compiler_params=_cp("parallel"),
    )(attn, lru, ssm, g.reshape(1, D_MODEL))


def _mixnorm_bwd(dmixed, attn, lru, ssm, g):
    S = attn.shape[0]
    rt = _pick(S, ROW_TILE, SUBLANES)

    def body(d_ref, a_ref, l_ref, s_ref, g_ref, da_ref, dl_ref, ds_ref, dlt_ref, acc_ref):
        @pl.when(pl.program_id(0) == 0)
        def _():
            acc_ref[...] = jnp.zeros_like(acc_ref)

        outs = []
        for ref, (lo, hi) in zip((a_ref, l_ref, s_ref), MIX_SPLITS):
            x = ref[...]
            dy = d_ref[:, lo:hi]
            rinv = lax.rsqrt(jnp.mean(x * x, axis=-1, keepdims=True) + RMS_EPS)
            dyg = dy * g_ref[:, lo:hi]
            outs.append(rinv * dyg - x * (rinv * rinv * rinv) * jnp.mean(dyg * x, axis=-1, keepdims=True))
            acc_ref[0:1, lo:hi] += jnp.sum(dy * x * rinv, axis=0, keepdims=True)
        da_ref[...], dl_ref[...], ds_ref[...] = outs
        hi_ = lax.broadcasted_iota(jnp.int32, (ATTN_WIDTH, ATTN_WIDTH), 0) // HEAD_DIM
        hj_ = lax.broadcasted_iota(jnp.int32, (ATTN_WIDTH, ATTN_WIDTH), 1) // HEAD_DIM
        same = jnp.where(hi_ == hj_, 1.0, 0.0).astype(f32)
        dlt_ref[...] = jnp.dot(outs[0] * a_ref[...], same, precision=lax.Precision.HIGHEST, preferred_element_type=f32)

    rows = [pl.BlockSpec((rt, hi - lo), lambda i: (i, 0)) for lo, hi in MIX_SPLITS]
    full = pl.BlockSpec((rt, D_MODEL), lambda i: (i, 0))
    return pl.pallas_call(
        body, name="mixnorm_bwd", grid=(S // rt,),
        in_specs=[full] + rows + [pl.BlockSpec((1, D_MODEL), lambda i: (0, 0))],
        out_specs=rows + [rows[0], pl.BlockSpec((SUBLANES, D_MODEL), lambda i: (0, 0))],
        out_shape=[jax.ShapeDtypeStruct((S, hi - lo), f32) for lo, hi in MIX_SPLITS]
        + [jax.ShapeDtypeStruct((S, ATTN_WIDTH), f32), jax.ShapeDtypeStruct((SUBLANES, D_MODEL), f32)],
        compiler_params=_cp("arbitrary"),
    )(dmixed, attn, lru, ssm, g.reshape(1, D_MODEL))


FFN_COL_TILE = 1536


def _ffn_conv(x, prev, w_ref, b_ref, K):
    y = b_ref[...] + w_ref[K - 1:K, :] * x
    for k in range(K - 1):
        y = y + w_ref[k:k + 1, :] * _shift_down(x, prev, K - 1 - k)
    return y


def _ffn_act(up, conv_w, conv_b):
    S, C2 = up.shape
    C = C2 // 2
    K = conv_w.shape[0]
    ct = FFN_COL_TILE
    nct = C // ct
    rt = _pick(S, ROW_TILE, SUBLANES)

    def body(g_ref, gp_ref, v_ref, vp_ref, wg_ref, wv_ref, bg_ref, bv_ref, o_ref):
        first = pl.program_id(1) == 0
        gate = _ffn_conv(g_ref[...], jnp.where(first, 0.0, gp_ref[...]), wg_ref, bg_ref, K)
        val = _ffn_conv(v_ref[...], jnp.where(first, 0.0, vp_ref[...]), wv_ref, bv_ref, K)
        o_ref[...] = (_gelu(gate) * val).astype(o_ref.dtype)

    def specs(off):
        return (pl.BlockSpec((rt, ct), lambda j, i: (i, j + off)), _prev_halo_spec(rt, ct, lambda j, i: j + off))

    def wspec(off, rows):
        return pl.BlockSpec((rows, ct), lambda j, i: (0, j + off))

    g_s, gp_s = specs(0)
    v_s, vp_s = specs(nct)
    return pl.pallas_call(
        body, name="ffn_act", grid=(nct, S // rt),
        in_specs=[g_s, gp_s, v_s, vp_s, wspec(0, K), wspec(nct, K), wspec(0, 1), wspec(nct, 1)],
        out_specs=pl.BlockSpec((rt, ct), lambda j, i: (i, j)),
        out_shape=jax.ShapeDtypeStruct((S, C), bf16), compiler_params=_cp("parallel", "parallel"),
    )(up, up, up, up, conv_w, conv_w, conv_b.reshape(1, C2), conv_b.reshape(1, C2))


def _ffn_act_bwd(dact, up, conv_w, conv_b):
    S, C2 = up.shape
    C = C2 // 2
    K = conv_w.shape[0]
    ct = FFN_COL_TILE
    nct = C // ct
    rt = _pick(S, ROW_TILE, SUBLANES)

    def body(d_ref, g_ref, gp_ref, v_ref, vp_ref, wg_ref, wv_ref, bg_ref, bv_ref, o_ref):
        first = pl.program_id(1) == 0
        gate = _ffn_conv(g_ref[...], jnp.where(first, 0.0, gp_ref[...]), wg_ref, bg_ref, K)
        val = _ffn_conv(v_ref[...], jnp.where(first, 0.0, vp_ref[...]), wv_ref, bv_ref, K)
        d = d_ref[...].astype(f32)
        gl, dgl = _gelu_pair(gate)
        o_ref[0] = (d * val * dgl).astype(o_ref.dtype)
        o_ref[1] = (d * gl).astype(o_ref.dtype)

    def specs(off):
        return (pl.BlockSpec((rt, ct), lambda j, i: (i, j + off)), _prev_halo_spec(rt, ct, lambda j, i: j + off))

    def wspec(off, rows):
        return pl.BlockSpec((rows, ct), lambda j, i: (0, j + off))

    g_s, gp_s = specs(0)
    v_s, vp_s = specs(nct)
    return pl.pallas_call(
        body, name="ffn_act_bwd", grid=(nct, S // rt),
        in_specs=[pl.BlockSpec((rt, ct), lambda j, i: (i, j)), g_s, gp_s, v_s, vp_s,
                  wspec(0, K), wspec(nct, K), wspec(0, 1), wspec(nct, 1)],
        out_specs=pl.BlockSpec((2, rt, ct), lambda j, i: (0, i, j)),
        out_shape=jax.ShapeDtypeStruct((2, S, C), bf16), compiler_params=_cp("parallel", "parallel"),
    )(dact, up, up, up, up, conv_w, conv_w, conv_b.reshape(1, C2), conv_b.reshape(1, C2))


ANY = pl.BlockSpec(memory_space=pl.ANY)


def _rows_for(cols):
    return max(16, (1 << 19) // cols)


def _chips(x, y):
    return [(1 - x, y), (x, 1 - y), (1 - x, 1 - y)]


class _Gather:
    def __init__(self, shards, split):
        self.shapes = [s.shape for s in shards]
        self.dtypes = [s.dtype for s in shards]
        self.split = list(split)
        self.nt = len(shards)
        self.in_specs = [ANY] * self.nt
        self.out_specs = [ANY] * self.nt
        self.out_shape = [jax.ShapeDtypeStruct((4,) + s, dt) for s, dt in zip(self.shapes, self.dtypes)]
        self.scratch = [pltpu.SemaphoreType.DMA((3, self.nt))] * 4 if self.nt else []

    def _part(self, ref, t, half):
        if not self.split[t]:
            return ref
        r = self.shapes[t][0] // 2
        return ref.at[pl.ds(half * r, r), :]

    def _ici(self, ins, outs, sems, k, t, chip, landing_chip):
        x, y, c = lax.axis_index("x"), lax.axis_index("y"), lax.axis_index("c")
        return pltpu.make_async_remote_copy(
            src_ref=self._part(ins[t], t, c), dst_ref=self._part(outs[t].at[landing_chip], t, c),
            send_sem=sems[0].at[k, t], recv_sem=sems[1].at[k, t], device_id=(chip[0], chip[1], c), device_id_type=MESH)

    def _d2d(self, outs, sems, k, t, q, half):
        x, y, c = lax.axis_index("x"), lax.axis_index("y"), lax.axis_index("c")
        rows = self._part(outs[t].at[q], t, half)
        return pltpu.make_async_remote_copy(
            src_ref=rows, dst_ref=rows, send_sem=sems[2].at[k, t], recv_sem=sems[3].at[k, t],
            device_id=(x, y, 1 - c), device_id_type=MESH)

    def start(self, ins, outs, sems):
        x, y = lax.axis_index("x"), lax.axis_index("y")
        me = 2 * x + y
        for k, chip in enumerate(_chips(x, y)):
            for t in range(self.nt):
                self._ici(ins, outs, sems, k, t, chip, me).start()

    def finish(self, ins, outs, sems):
        x, y, c = lax.axis_index("x"), lax.axis_index("y"), lax.axis_index("c")
        me = 2 * x + y
        chips = _chips(x, y)
        for k, chip in enumerate(chips):
            q = 2 * chip[0] + chip[1]
            for t in range(self.nt):
                self._ici(ins, outs, sems, k, t, chip, q).wait_recv()
                if self.split[t]:
                    self._d2d(outs, sems, k, t, q, c).start()
        for k, chip in enumerate(chips):
            q = 2 * chip[0] + chip[1]
            for t in range(self.nt):
                if self.split[t]:
                    self._d2d(outs, sems, k, t, q, 1 - c).wait_recv()
        for k, chip in enumerate(chips):
            q = 2 * chip[0] + chip[1]
            for t in range(self.nt):
                self._ici(ins, outs, sems, k, t, chip, me).wait_send()
                if self.split[t]:
                    self._d2d(outs, sems, k, t, q, c).wait_send()


def _gather_weights(shards, split):
    g = _Gather(shards, split)
    nt = g.nt

    def body(*refs):
        ins, outs, sems = refs[:nt], refs[nt:2 * nt], refs[2 * nt:]
        g.start(ins, outs, sems)
        g.finish(ins, outs, sems)

    return pl.pallas_call(
        body, name="gather_weights", in_specs=g.in_specs, out_specs=g.out_specs, out_shape=g.out_shape,
        scratch_shapes=g.scratch,
    )(*shards)


class _SiblingSend:
    def __init__(self, gs, dst_c, swap=()):
        self.nt, self.n = len(gs), len(gs) + len(swap)
        self.dst_c = list(dst_c) if isinstance(dst_c, (list, tuple)) else [dst_c] * self.nt
        self.in_specs = [ANY] * self.n
        self.out_specs = [ANY] * self.n
        self.out_shape = [jax.ShapeDtypeStruct(g.shape, g.dtype) for g in list(gs) + list(swap)]
        self.scratch = [pltpu.SemaphoreType.DMA((self.n,))] * 2 if self.n else []

    def _each(self, ins, outs, sems, sender, fn):
        x, y, c = lax.axis_index("x"), lax.axis_index("y"), lax.axis_index("c")

        def cp(t):
            return pltpu.make_async_remote_copy(
                src_ref=ins[t], dst_ref=outs[t], send_sem=sems[0].at[t], recv_sem=sems[1].at[t],
                device_id=(x, y, 1 - c), device_id_type=MESH)

        for dst in (0, 1):
            which = [t for t in range(self.nt) if self.dst_c[t] == dst]
            if which:
                @pl.when((c != dst) if sender else (c == dst))
                def _(which=which):
                    for t in which:
                        fn(cp(t))
        for t in range(self.nt, self.n):
            fn(cp(t))

    def start(self, ins, outs, sems):
        self._each(ins, outs, sems, True, lambda cp: cp.start())

    def finish(self, ins, outs, sems):
        self._each(ins, outs, sems, False, lambda cp: cp.wait_recv())
        self._each(ins, outs, sems, True, lambda cp: cp.wait_send())


def _sibling_send(gs, dst_c, swap=()):
    snd = _SiblingSend(gs, dst_c, swap)
    n = snd.n

    def body(*refs):
        ins, outs, sems = refs[:n], refs[n:2 * n], refs[2 * n:]
        snd.start(ins, outs, sems)
        snd.finish(ins, outs, sems)

    res = pl.pallas_call(
        body, name="sibling_send", in_specs=snd.in_specs, out_specs=snd.out_specs, out_shape=snd.out_shape,
        scratch_shapes=snd.scratch,
    )(*gs, *swap)
    return list(res[:snd.nt]), list(res[snd.nt:])


def _owner_flag(owner_c):
    return (lax.axis_index("c") == owner_c).astype(jnp.int32).reshape(1)


def _pair_sum(g, other, name, owner_c, col_slabs=False):
    R, C = g.shape
    cb = C // 4 if col_slabs else C
    rt = _pick(R, _rows_for(cb), 16)

    def body(on_ref, a_ref, o_ref, out_ref):
        out_ref[...] = (a_ref[...] + o_ref[...]).astype(out_ref.dtype)

    row = pl.BlockSpec((rt, cb), lambda q, i, on: (i * on[0], q * on[0]))
    if col_slabs:
        out_spec = pl.BlockSpec((None, rt, cb), lambda q, i, on: (q * on[0], i * on[0], 0))
        out_shape = jax.ShapeDtypeStruct((4, R, cb), bf16)
    else:
        out_spec, out_shape = row, jax.ShapeDtypeStruct((R, C), bf16)
    return pl.pallas_call(
        body, name=name,
        grid_spec=pltpu.PrefetchScalarGridSpec(num_scalar_prefetch=1, grid=(C // cb, R // rt),
                                               in_specs=[row, row], out_specs=out_spec),
        out_shape=out_shape, compiler_params=_cp("arbitrary", "arbitrary"),
    )(_owner_flag(owner_c), g, other)


class _ChipExchange:
    NORTH = 1

    def __init__(self, slabs, whole, only_c):
        self.ns, self.nw = len(slabs), len(whole)
        self.only_c = list(only_c) if isinstance(only_c, (list, tuple)) else [only_c] * self.ns
        self.only_c += [self.NORTH] * self.nw
        self.n = self.ns + self.nw
        self.in_specs = [ANY] * self.n
        self.out_specs = [ANY] * self.n
        self.out_shape = ([jax.ShapeDtypeStruct(s.shape, s.dtype) for s in slabs]
                          + [jax.ShapeDtypeStruct((4,) + w.shape, w.dtype) for w in whole])
        self.scratch = [pltpu.SemaphoreType.DMA((3, self.n))] * 2 if self.n else []
        self.scratch += [pltpu.SemaphoreType.DMA((3, self.nw))] * 2 if self.nw else []

    def _copy(self, ins, outs, sems, k, t, chip, landing_chip):
        c = lax.axis_index("c")
        src = ins[t].at[2 * chip[0] + chip[1]] if t < self.ns else ins[t]
        return pltpu.make_async_remote_copy(
            src_ref=src, dst_ref=outs[t].at[landing_chip], send_sem=sems[0].at[k, t], recv_sem=sems[1].at[k, t],
            device_id=(chip[0], chip[1], c), device_id_type=MESH)

    def _pass_on(self, outs, sems, k, t, chip):
        x, y, c = lax.axis_index("x"), lax.axis_index("y"), lax.axis_index("c")
        rows = outs[t].at[2 * chip[0] + chip[1]]
        return pltpu.make_async_remote_copy(
            src_ref=rows, dst_ref=rows, send_sem=sems[2].at[k, t - self.ns], recv_sem=sems[3].at[k, t - self.ns],
            device_id=(x, y, 1 - c), device_id_type=MESH)

    def _each(self, fn, north=True):
        x, y, c = lax.axis_index("x"), lax.axis_index("y"), lax.axis_index("c")
        chips = _chips(x, y)
        groups = {}
        if north:
            for t in range(self.n):
                groups.setdefault(self.only_c[t], []).append(t)
        elif self.nw:
            groups[1 - self.NORTH] = list(range(self.ns, self.n))
        for owner, which in groups.items():
            @pl.when(c == owner)
            def _(which=which):
                for k, chip in enumerate(chips):
                    for t in which:
                        fn(k, t, chip)

    def start(self, ins, outs, sems):
        me = 2 * lax.axis_index("x") + lax.axis_index("y")
        self._each(lambda k, t, chip: self._copy(ins, outs, sems, k, t, chip, me).start())

    def finish(self, ins, outs, sems):
        me = 2 * lax.axis_index("x") + lax.axis_index("y")

        def landed(k, t, chip):
            self._copy(ins, outs, sems, k, t, chip, 2 * chip[0] + chip[1]).wait_recv()
            if t >= self.ns:
                self._pass_on(outs, sems, k, t, chip).start()

        def sent(k, t, chip):
            self._copy(ins, outs, sems, k, t, chip, me).wait_send()
            if t >= self.ns:
                self._pass_on(outs, sems, k, t, chip).wait_send()

        self._each(landed)
        self._each(lambda k, t, chip: self._pass_on(outs, sems, k, t, chip).wait_recv(), north=False)
        self._each(sent)


def _chip_exchange(slabs, whole, only_c):
    ex = _ChipExchange(slabs, whole, only_c)
    n = ex.n

    def body(*refs):
        ins, outs, sems = refs[:n], refs[n:2 * n], refs[2 * n:]
        ex.start(ins, outs, sems)
        ex.finish(ins, outs, sems)

    res = pl.pallas_call(
        body, name="chip_exchange", in_specs=ex.in_specs, out_specs=ex.out_specs, out_shape=ex.out_shape,
        scratch_shapes=ex.scratch,
    )(*slabs, *whole)
    return list(res[:ex.ns]), list(res[ex.ns:])


def _sum_chips(recv, own, name, owner_c=None):
    n, r, C = recv.shape
    rt = _pick(r, _rows_for(C), 16)
    own3 = own.ndim == 3

    def body(on_ref, r_ref, o_ref, out_ref):
        me = 2 * lax.axis_index("x") + lax.axis_index("y")
        acc = None
        for q in range(n):
            term = jnp.where(me == q, o_ref[q] if own3 else o_ref[...], r_ref[q]).astype(f32)
            acc = term if acc is None else acc + term
        out_ref[...] = acc

    blk = pl.BlockSpec((n, rt, C), lambda i, on: (0, i * on[0], 0))
    row = pl.BlockSpec((rt, C), lambda i, on: (i * on[0], 0))
    flag = jnp.ones((1,), jnp.int32) if owner_c is None else _owner_flag(owner_c)
    return pl.pallas_call(
        body, name=name,
        grid_spec=pltpu.PrefetchScalarGridSpec(num_scalar_prefetch=1, grid=(r // rt,),
                                               in_specs=[blk, blk if own3 else row], out_specs=row),
        out_shape=jax.ShapeDtypeStruct((r, C), f32), compiler_params=_cp("arbitrary"),
    )(flag, recv, own)


def _adamw_layers(mine, theirs, owners, w, m, v, name):
    L, r, C = w.shape
    rt = _pick(r, _rows_for(C), 16)

    def body(own_ref, a0_ref, a1_ref, b0_ref, b1_ref, w_ref, m_ref, v_ref, g_ref, d_ref, mo_ref, vo_ref):
        layer = pl.program_id(0)
        g0 = jnp.where(own_ref[0] == 1, a0_ref[...], b0_ref[...])
        g1 = jnp.where(own_ref[1] == 1, a1_ref[...], b1_ref[...])
        g_ref[...] = jnp.where(layer == 0, g0, g1)
        _adamw_math(g_ref, w_ref, m_ref, v_ref, d_ref, mo_ref, vo_ref)

    def flat(layer, mine_side):
        def index(l, i, own):
            use = (l == layer) & (own[layer] == (1 if mine_side else 0))
            return (jnp.where(use, i, 0), 0)
        return pl.BlockSpec((rt, C), index)

    lay = pl.BlockSpec((None, rt, C), lambda l, i, own: (l, i, 0))
    c = lax.axis_index("c")
    own = jnp.stack([(c == owners[0]).astype(jnp.int32), (c == owners[1]).astype(jnp.int32)])
    return pl.pallas_call(
        body, name=name,
        grid_spec=pltpu.PrefetchScalarGridSpec(
            num_scalar_prefetch=1, grid=(L, r // rt),
            in_specs=[flat(0, True), flat(1, True), flat(0, False), flat(1, False)] + [lay] * 3, out_specs=[lay] * 4),
        out_shape=[jax.ShapeDtypeStruct((L, r, C), f32)] * 4, compiler_params=_cp("arbitrary", "arbitrary"),
    )(own, mine[0], mine[1], theirs[0], theirs[1], w, m, v)


def _adamw_math(g_ref, w_ref, m_ref, v_ref, d_ref, mo_ref, vo_ref):
    gg = g_ref[...]
    m_new = ADAM_B1 * m_ref[...] + (1.0 - ADAM_B1) * gg
    v_new = ADAM_B2 * v_ref[...] + (1.0 - ADAM_B2) * (gg * gg)
    m_hat = m_new / (1.0 - ADAM_B1 ** ADAM_STEP)
    v_hat = v_new / (1.0 - ADAM_B2 ** ADAM_STEP)
    d_ref[...] = -ADAM_LR * (m_hat / (jnp.sqrt(v_hat) + ADAM_EPS) + ADAM_WD * w_ref[...])
    mo_ref[...] = m_new
    vo_ref[...] = v_new


FLAT_TILE = 2048


def _add2(a, b, name):
    R = a.shape[0]
    rt = _pick(R, FLAT_TILE, SUBLANES)

    def body(a_ref, b_ref, o_ref):
        o_ref[...] = a_ref[...] + b_ref[...]

    row = pl.BlockSpec((rt, LANES), lambda i: (i, 0))
    return pl.pallas_call(
        body, name=name, grid=(R // rt,), in_specs=[row, row], out_specs=row,
        out_shape=jax.ShapeDtypeStruct((R, LANES), f32), compiler_params=_cp("parallel"),
    )(a, b)


def _adamw(g, w, m, v, name):
    R = g.shape[0]
    rt = _pick(R, FLAT_TILE, SUBLANES)

    def body(g_ref, w_ref, m_ref, v_ref, d_ref, mo_ref, vo_ref):
        _adamw_math(g_ref, w_ref, m_ref, v_ref, d_ref, mo_ref, vo_ref)

    row = pl.BlockSpec((rt, LANES), lambda i: (i, 0))
    return pl.pallas_call(
        body, name=name, grid=(R // rt,), in_specs=[row] * 4, out_specs=[row] * 3,
        out_shape=[jax.ShapeDtypeStruct((R, LANES), f32)] * 3, compiler_params=_cp("parallel"),
    )(g, w, m, v)


def _adamw_whole(g, w, m, v, name):
    def body(g_ref, w_ref, m_ref, v_ref, d_ref, mo_ref, vo_ref):
        _adamw_math(g_ref, w_ref, m_ref, v_ref, d_ref, mo_ref, vo_ref)

    return pl.pallas_call(
        body, name=name, out_shape=[jax.ShapeDtypeStruct(w.shape, f32)] * 3, compiler_params=_cp(),
    )(g, w, m, v)


def _pack(arrs, dtype, rows=None):
    flat = jnp.concatenate([a.astype(dtype).reshape(-1) for a in arrs])
    per = FLAT_TILE * LANES
    total = rows * LANES if rows else flat.shape[0] + (-flat.shape[0]) % per
    flat = jnp.pad(flat, (0, total - flat.shape[0]))
    return flat.reshape(-1, LANES)


def _unpack(buf, shapes):
    flat = buf.reshape(-1)
    out, off = [], 0
    for s in shapes:
        n = math.prod(s)
        out.append(flat[off:off + n].reshape(s))
        off += n
    return out


def _block_diag(w):
    n, a, b = w.shape
    eye = jnp.eye(n, dtype=w.dtype)
    return (w[:, :, None, :] * eye[:, None, :, None]).reshape(n * a, n * b)


def _diag_blocks(m, n):
    a, b = m.shape[0] // n, m.shape[1] // n
    idx = jnp.arange(n)
    return m.reshape(n, a, n, b)[idx, :, idx, :]


BIG = ("w_in", "w_out", "w_up", "w_down", "s5_w_glu")
BIG_COL_SHARDED = {"w_in": True, "w_out": False, "w_up": True, "w_down": False, "s5_w_glu": False}
CONV_SHARDED = ("lru_conv_w", "ffn_conv_w")
SMALL = ("lru_conv_b", "lru_wr", "lru_br", "lru_wi", "lru_bi", "lru_lambda", "s5_a_re", "s5_a_im", "s5_b_re",
         "s5_b_im", "s5_c_re", "s5_c_im", "s5_d", "s5_log_step", "s5_b_glu", "mix_norm_g", "ln1_g", "ln1_b",
         "ffn_conv_b", "ln2_g", "ln2_b")
WEIGHTS = ("w_in", "lru_conv_w", "lru_conv_b", "lru_wr", "lru_br", "lru_wi", "lru_bi", "lru_lambda", "s5_a_re",
           "s5_a_im", "s5_b_re", "s5_b_im", "s5_c_re", "s5_c_im", "s5_d", "s5_log_step", "s5_w_glu", "s5_b_glu",
           "mix_norm_g", "w_out", "ln1_g", "ln1_b", "w_up", "ffn_conv_w", "ffn_conv_b", "w_down", "ln2_g", "ln2_b")


def _assemble(slabs, col_sharded):
    _, L, r, c = slabs.shape
    if col_sharded:
        return slabs.transpose(1, 2, 0, 3).reshape(L, r, 4 * c)
    return slabs.transpose(1, 0, 2, 3).reshape(L, 4 * r, c)


def _s5_prepare(p):
    G = N_S5_GROUPS
    bt_re, bt_im = p["s5_b_re"].transpose(0, 2, 1), p["s5_b_im"].transpose(0, 2, 1)
    ls = p["s5_log_step"].reshape(G, 1)
    ab_re, ab_im, bb_re, bb_im = _s5_params(p["s5_a_re"], p["s5_a_im"], ls, bt_re, bt_im)
    ab = jnp.concatenate([ab_re.reshape(1, S5_LANES), ab_im.reshape(1, S5_LANES)], axis=1)
    bbcat = jnp.concatenate([_block_diag(bb_re), _block_diag(bb_im)], axis=1).astype(bf16)
    ccat = jnp.concatenate([_block_diag(p["s5_c_re"].transpose(0, 2, 1)),
                            -_block_diag(p["s5_c_im"].transpose(0, 2, 1))], axis=0).astype(bf16)
    bbcat_pad = jnp.concatenate([bbcat, jnp.zeros((LRU_WIDTH - S5_WIDTH, 2 * S5_LANES), bf16)], axis=0)
    return dict(bt_re=bt_re, bt_im=bt_im, ls=ls, ab=ab, bbcat=bbcat, bbcat_pad=bbcat_pad, ccat=ccat)


def _layer_fwd(h, p, cos, sin, pending, install):
    sv = {"h": h}
    proj = _mm(h, p["w_in"], "nn", "mm_proj", tn=D_IN_PAD)
    sv.update(proj=proj)
    qr, kr = _rope_fwd(proj, cos, sin)
    attn, ltot, gathered = _attn_fwd2(qr, kr, proj, [s for _, _, s in pending], [True] * len(pending))
    install(pending, gathered)
    sv.update(qr=qr, kr=kr, attn=attn, ltot=ltot)
    wr, wi = _block_diag(p["lru_wr"]).astype(bf16), _block_diag(p["lru_wi"]).astype(bf16)
    xc, r, i, log_a, u = _lru_pre(proj, p["lru_conv_w"], p["lru_conv_b"], wr, p["lru_br"], wi, p["lru_bi"],
                                  p["lru_lambda"])
    hl, lru = _lru_scan(log_a, u, proj)
    sv.update(wr=wr, wi=wi, xc=xc, r=r, i=i, log_a=log_a, hl=hl, lru=lru)
    s5 = _s5_prepare(p)
    bu = _mm(proj, s5["bbcat_pad"], "nn", "mm_s5_bu", a_win=(PROJ_S5_U, LRU_WIDTH))
    xs = _s5_scan(bu, s5["ab"])
    yc = _mm(xs, s5["ccat"], "nn", "mm_s5_y")
    ssm, y = _s5_out(yc, proj, p["s5_d"].reshape(-1), p["s5_w_glu"], p["s5_b_glu"])
    sv.update(s5=s5, xs=xs, y=y, ssm=ssm)
    mixed = _mixnorm(attn, lru, ssm, p["mix_norm_g"])
    h1, z1 = _mm_ln_fwd(mixed, p["w_out"], h, p["ln1_g"], p["ln1_b"], "mm_out_ln")
    sv.update(mixed=mixed, z1=z1, h1=h1)
    up = _mm(h1, p["w_up"], "nn", "mm_up", tn=1536)
    act = _ffn_act(up, p["ffn_conv_w"], p["ffn_conv_b"])
    h2, z2 = _mm_ln_fwd(act, p["w_down"], h1, p["ln2_g"], p["ln2_b"], "mm_down_ln")
    sv.update(up=up, act=act, z2=z2)
    return h2, sv


def _layer_bwd(dy_a, dy_b, p, sv, cos, sin, relay=None):
    gr = {}
    dz2, acc = _ln_bwd(dy_a, dy_b, sv["z2"], p["ln2_g"], "ln_bwd_top" if dy_a is None else "ln_bwd")
    gr["ln2_g"], gr["ln2_b"] = acc[0], acc[1]
    dact = _mm(dz2, p["w_down"], "nt", "mm_dact", out_dtype=bf16)
    gr["w_down"] = _mm(sv["act"], dz2, "tn", "mm_dw_down")
    dupc = _ffn_act_bwd(dact, sv["up"], p["ffn_conv_w"], p["ffn_conv_b"])
    others, others_dst, to_slabs = relay if relay else ((), 0, None)
    send = list(others) + ([gr["w_down"]] if relay else [])
    send_dst = [others_dst] * len(others) + ([EARLY_OWNER] if relay else [])
    dup, acc, from_sibling = _conv_bwd(dupc, sv["up"], p["ffn_conv_w"], "ffn_conv_bwd", col_tile=FFN_COL_TILE,
                                       out_dtype=bf16, send=send, send_dst=send_dst)
    slabs, owners = to_slabs(from_sibling[:-1], gr["w_down"], from_sibling[-1]) if relay else ((), 0)
    gr["ffn_conv_w"], gr["ffn_conv_b"] = acc[0:3], acc[3]
    dh1 = _mm(dup, p["w_up"], "nt", "mm_dh1", tk=2048)
    gr["w_up"] = _mm(sv["h1"], dup, "tn", "mm_dw_up", tn=1536)
    dz1, acc = _ln_bwd(dz2, dh1, sv["z1"], p["ln1_g"], "ln_bwd")
    gr["ln1_g"], gr["ln1_b"] = acc[0], acc[1]
    dmixed = _mm(dz1, p["w_out"], "nt", "mm_dmixed")
    gr["w_out"] = _mm(sv["mixed"], dz1, "tn", "mm_dw_out")
    dattn, dlru, dssm, delta, acc = _mixnorm_bwd(dmixed, sv["attn"], sv["lru"], sv["ssm"], p["mix_norm_g"])
    gr["mix_norm_g"] = acc[0]
    proj = sv["proj"]
    late = [gr["w_up"]] if relay else []
    dqr, dkr, dv, received, late_sibling = _attn_bwd2(sv["qr"], sv["kr"], proj, dattn, sv["ltot"], delta, slabs,
                                                      owners, late, EARLY_OWNER)
    dq, dk = _rope_bwd(dqr, dkr, cos, sin)
    g, dgate = _lru_scan_bwd(dlru, proj, sv["hl"], sv["log_a"])
    dxc, dwr, dwi, acc = _lru_gate_bwd(g, sv["hl"], sv["xc"], sv["r"], sv["i"], sv["log_a"], sv["wr"], sv["wi"],
                                       p["lru_lambda"])
    gr["lru_wr"], gr["lru_wi"] = _diag_blocks(dwr, N_LRU_HEADS), _diag_blocks(dwi, N_LRU_HEADS)
    gr["lru_br"], gr["lru_bi"], gr["lru_lambda"] = acc[0], acc[1], acc[2]
    dxr, acc, _ = _conv_bwd(dxc, proj, p["lru_conv_w"], "lru_conv_bwd", x_col_block=PROJ_LRU_X)
    gr["lru_conv_w"], gr["lru_conv_b"] = acc[0:4], acc[4]
    s5 = sv["s5"]
    G = N_S5_GROUPS
    dy, du_direct, dwglu, acc = _s5_out_bwd(dssm, sv["y"], proj, p["s5_d"].reshape(-1), p["s5_w_glu"],
                                            p["s5_b_glu"])
    gr["s5_w_glu"], gr["s5_b_glu"], gr["s5_d"] = dwglu, acc[0], acc[1].reshape(G, S5_GROUP)
    dxs = _mm(dy, s5["ccat"], "nt", "mm_s5_dx")
    dccat = _mm(sv["xs"], dy, "tn", "mm_s5_dc")
    gr["s5_c_re"] = _diag_blocks(dccat[:S5_LANES], G).transpose(0, 2, 1)
    gr["s5_c_im"] = -_diag_blocks(dccat[S5_LANES:], G).transpose(0, 2, 1)
    gs, dab = _s5_scan_bwd(dxs, sv["xs"], s5["ab"])
    du = _mm(gs, s5["bbcat"], "nt", "mm_s5_du", add=du_direct)
    dbbcat = _mm(proj, gs, "tn", "mm_s5_dbb", a_win=(PROJ_S5_U, LRU_WIDTH))[:S5_WIDTH]
    d_ar, d_ai, d_ls, d_btr, d_bti = _s5_params_bwd(
        p["s5_a_re"], p["s5_a_im"], s5["ls"], s5["bt_re"], s5["bt_im"],
        dab[:, :S5_LANES].reshape(G, S5_STATE), dab[:, S5_LANES:].reshape(G, S5_STATE),
        _diag_blocks(dbbcat[:, :S5_LANES], G), _diag_blocks(dbbcat[:, S5_LANES:], G))
    gr["s5_a_re"], gr["s5_a_im"], gr["s5_log_step"] = d_ar, d_ai, d_ls.reshape(G)
    gr["s5_b_re"], gr["s5_b_im"] = d_btr.transpose(0, 2, 1), d_bti.transpose(0, 2, 1)
    pad = jnp.zeros((du.shape[0], D_IN_PAD - D_IN), f32)
    dproj = jnp.concatenate([dq, dk, dv, dxr, dgate, du, pad], axis=1).astype(bf16)
    gr["w_in"] = _mm(sv["h"], dproj, "tn", "mm_dw_in", tn=768)[:, :D_IN]
    if relay:
        dx = _mm(dproj, p["w_in"], "nt", "mm_dx", add=dz1, add_scale=ALPHA)
        return dz1, dx, gr, slabs, received, late_sibling
    dh = _mm(dproj, p["w_in"], "nt", "mm_dh")
    return dz1, dh, gr


def _train_step(d):
    x, target = d["x"][0], d["loss_target"][0]
    S = x.shape[0]
    me = 2 * lax.axis_index("x") + lax.axis_index("y")

    def rows2d(a):
        return a.reshape(a.shape[0] * a.shape[1], a.shape[2])

    params = [{n: d[n][l] for n in SMALL} for l in range(DEPTH)]

    def install(items, gathered):
        for (n, l, mine), g in zip(items, gathered):
            g = lax.dynamic_update_slice_in_dim(g, mine[None], me, axis=0)
            if n in CONV_SHARDED:
                full = _assemble(g.reshape((4,) + d[n].shape), True)
                for k in range(DEPTH):
                    params[k][n] = full[k]
                continue
            full = _assemble(g[:, None], BIG_COL_SHARDED[n])[0]
            if n == "w_in":
                full = jnp.pad(full, ((0, 0), (0, D_IN_PAD - D_IN)))
            params[l][n] = full

    def shard(n, l):
        return (n, l, d[n][l].astype(bf16))

    first = [shard("w_in", 0)] + [(n, None, rows2d(d[n])) for n in CONV_SHARDED]
    install(first, _gather_weights([s for _, _, s in first], [True] + [False] * len(CONV_SHARDED)))
    later = [[shard(n, 0) for n in BIG[1:]] + [shard("w_in", 1)], [shard(n, 1) for n in BIG[1:]]]

    cos, sin = _rope_tables(S)
    h, saved = x, []
    for l in range(DEPTH):
        h, sv = _layer_fwd(h, params[l], cos, sin, later[l], install)
        saved.append(sv)
    dy, loss_acc = _loss_head(h, target)
    def slab(n, g, other, owner):
        aligned = BIG_COL_SHARDED[n] and (g.shape[1] // 4) % LANES == 0
        p = _pair_sum(g, other, "pair_sum_" + n, owner, col_slabs=aligned)
        if BIG_COL_SHARDED[n] and not aligned:
            return p.reshape(p.shape[0], 4, p.shape[1] // 4).transpose(1, 0, 2)
        return p if aligned else p.reshape(4, p.shape[0] // 4, p.shape[1])

    own1 = {n: 1 - EARLY_OWNER for n in BIG}
    own0 = {n: (EARLY_OWNER if n in ("w_down", "w_up") else 1 - EARLY_OWNER) for n in BIG}

    def hidden_slabs(others1, w_down0, w_down0_sibling):
        slabs = [slab(n, grads[1][n], o, own1[n]) for n, o in zip(BIG, others1)]
        slabs.append(slab("w_down", w_down0, w_down0_sibling, own0["w_down"]))
        return slabs, [own1[n] for n in BIG] + [own0["w_down"]]

    da, db, grads = None, dy, [None] * DEPTH
    da, db, grads[1] = _layer_bwd(da, db, params[1], saved[1], cos, sin)
    relay = ([grads[1][n] for n in BIG], 1 - EARLY_OWNER, hidden_slabs)
    da, db, grads[0], hslabs, hrecv, (w_up0_sibling,) = _layer_bwd(da, db, params[0], saved[0], cos, sin, relay)
    out = {"grad_x": db[None]}

    small = SMALL + CONV_SHARDED
    sp = _pack([grads[l][n] for n in small for l in range(DEPTH)], f32)
    tail = [n for n in BIG if n != "w_down"]
    rest = [n for n in tail if n != "w_up"]
    others_rest, (sp_sibling,) = _sibling_send([grads[0][n] for n in rest], [own0[n] for n in rest], [sp])
    sibling0 = dict(zip(rest, others_rest), w_up=w_up0_sibling)
    tslabs = [slab(n, grads[0][n], sibling0[n], own0[n]) for n in tail]
    chip_small = _add2(sp, sp_sibling, "pair_sum_small")
    trecv, (recv_small,) = _chip_exchange(tslabs, [chip_small], [own0[n] for n in tail])
    mine0 = {n: _sum_chips(r, s, "sum_chips_" + n, own0[n]) for n, r, s in zip(tail, trecv, tslabs)}
    mine0["w_down"] = _sum_chips(hrecv[-1], hslabs[-1], "sum_chips_w_down", own0["w_down"])
    mine1 = {n: _sum_chips(r, s, "sum_chips_" + n, own1[n]) for n, r, s in zip(BIG, hrecv, hslabs)}
    sent, _ = _sibling_send([mine0[n] for n in BIG] + [mine1[n] for n in BIG],
                            [1 - own0[n] for n in BIG] + [1 - own1[n] for n in BIG])
    theirs0, theirs1 = dict(zip(BIG, sent[:len(BIG)])), dict(zip(BIG, sent[len(BIG):]))
    for n in BIG:
        upd = _adamw_layers((mine0[n], mine1[n]), (theirs0[n], theirs1[n]), (own0[n], own1[n]),
                            d[n], d["m_" + n], d["v_" + n], "adamw_" + n)
        for pre, u in zip(("grad_", "delta_", "new_m_", "new_v_"), upd):
            out[pre + n] = u

    total = _sum_chips(recv_small, chip_small, "sum_chips_small")
    rows = total.shape[0]
    upd = _adamw(total, _pack([d[n] for n in SMALL], f32, rows), _pack([d["m_" + n] for n in SMALL], f32, rows),
                 _pack([d["v_" + n] for n in SMALL], f32, rows), "adamw_small")
    small_shapes = [d[n].shape for n in SMALL]
    for pre, buf in zip(("grad_", "delta_", "new_m_", "new_v_"), (total,) + tuple(upd)):
        for n, a in zip(SMALL, _unpack(buf, small_shapes)):
            out[pre + n] = a
    conv_full = _unpack(total, small_shapes + [(DEPTH,) + grads[0][n].shape for n in CONV_SHARDED])
    for n, g in zip(CONV_SHARDED, conv_full[len(SMALL):]):
        L, K, C = g.shape
        g = lax.dynamic_index_in_dim(g.reshape(L, K, 4, C // 4), me, axis=2, keepdims=False)
        out["grad_" + n] = g
        for pre, u in zip(("delta_", "new_m_", "new_v_"), _adamw_whole(g, d[n], d["m_" + n], d["v_" + n], "adamw_" + n)):
            out[pre + n] = u

    loss_local, _ = lax.optimization_barrier((loss_acc[0, 0], upd[0]))
    out["loss"] = lax.psum(loss_local, ("x", "y", "c"))
    return (out["loss"], out["grad_x"]) + tuple(out[pre + n] for pre in ("grad_", "delta_", "new_m_", "new_v_")
                                                for n in WEIGHTS)


def kernel(
        x, w_in, lru_conv_w, lru_conv_b, lru_wr, lru_br, lru_wi, lru_bi, lru_lambda, s5_a_re, s5_a_im, s5_b_re,
        s5_b_im, s5_c_re, s5_c_im, s5_d, s5_log_step, s5_w_glu, s5_b_glu, mix_norm_g, w_out, ln1_g, ln1_b, w_up,
        ffn_conv_w, ffn_conv_b, w_down, ln2_g, ln2_b, loss_target, m_w_in, m_lru_conv_w, m_lru_conv_b, m_lru_wr,
        m_lru_br, m_lru_wi, m_lru_bi, m_lru_lambda, m_s5_a_re, m_s5_a_im, m_s5_b_re, m_s5_b_im, m_s5_c_re,
        m_s5_c_im, m_s5_d, m_s5_log_step, m_s5_w_glu, m_s5_b_glu, m_mix_norm_g, m_w_out, m_ln1_g, m_ln1_b,
        m_w_up, m_ffn_conv_w, m_ffn_conv_b, m_w_down, m_ln2_g, m_ln2_b, v_w_in, v_lru_conv_w, v_lru_conv_b,
        v_lru_wr, v_lru_br, v_lru_wi, v_lru_bi, v_lru_lambda, v_s5_a_re, v_s5_a_im, v_s5_b_re, v_s5_b_im,
        v_s5_c_re, v_s5_c_im, v_s5_d, v_s5_log_step, v_s5_w_glu, v_s5_b_glu, v_mix_norm_g, v_w_out, v_ln1_g,
        v_ln1_b, v_w_up, v_ffn_conv_w, v_ffn_conv_b, v_w_down, v_ln2_g, v_ln2_b
):
    return _train_step(dict(locals()))
```
